```python
import jax, jax.numpy as jnp
from jax import lax
import numpy as np

D_MODEL = 1024
BATCH = 8
SEQ = 8192
DEPTH = 1

D_RNN = 5 * D_MODEL // 4
RG_BLOCKS = 16
RG_BLOCK_W = D_RNN // RG_BLOCKS
RG_CONV = 4
RG_C = 8.0
DN_QK_HEADS = D_MODEL // 128
DN_V_HEADS = 2 * DN_QK_HEADS
DN_DK = 128
DN_DV = 128
DN_QK = DN_QK_HEADS * DN_DK
DN_V = DN_V_HEADS * DN_DV
DN_CONV = 4
DN_CHUNK = 64
D_FF = 11 * D_MODEL // 4
FFN_CONV = 3
LN_EPS = 1e-5
RMS_EPS = 1e-6
L2_EPS = 1e-6
DEEPNORM_ALPHA = (2 * DEPTH) ** 0.25
DEEPNORM_BETA = (8 * DEPTH) ** -0.25
IN_SPLITS = (D_RNN, D_RNN, DN_QK, DN_QK, DN_V, DN_V, DN_V_HEADS, DN_V_HEADS, D_MODEL, D_MODEL)
D_IN = sum(IN_SPLITS)

kernel_name = "hybrid_rglru_gdn_convffn_deepnorm_adaln"

F32 = jnp.float32


def split_cols(t, sizes):
    idx = np.cumsum(sizes)[:-1].tolist()
    return jnp.split(t, idx, axis=-1)


def causal_dwconv(x, w):
    k, ch = w.shape
    return lax.conv_general_dilated(
        x, w[:, None, :].astype(x.dtype), window_strides=(1,), padding=[(k - 1, 0)],
        dimension_numbers=('NWC', 'WIO', 'NWC'), feature_group_count=ch)


def layer_norm(x, g, b):
    xf = x.astype(F32)
    mu = jnp.mean(xf, axis=-1, keepdims=True)
    xc = xf - mu
    var = jnp.mean(xc * xc, axis=-1, keepdims=True)
    return (xc * lax.rsqrt(var + LN_EPS) * g.astype(F32) + b.astype(F32)).astype(x.dtype)


def l2_normalize(t):
    return t * lax.rsqrt(jnp.sum(t * t, axis=-1, keepdims=True) + L2_EPS)


def rg_lru(xr, w_a, b_a, w_x, b_x, lam):
    bsz, s, _ = xr.shape
    xf = xr.astype(F32)
    xb = xf.reshape(bsz, s, RG_BLOCKS, RG_BLOCK_W)
    gate_r = jax.nn.sigmoid(jnp.einsum('bsni,nij->bsnj', xb, w_a.astype(F32)).reshape(bsz, s, D_RNN) + b_a.astype(F32))
    gate_i = jax.nn.sigmoid(jnp.einsum('bsni,nij->bsnj', xb, w_x.astype(F32)).reshape(bsz, s, D_RNN) + b_x.astype(F32))
    log_a = -RG_C * gate_r * jax.nn.softplus(-lam.astype(F32))
    a = jnp.exp(log_a)
    mult = jnp.sqrt(-jnp.expm1(2.0 * log_a))
    u = mult * gate_i * xf

    def combine(left, right):
        a_l, h_l = left
        a_r, h_r = right
        return a_l * a_r, a_r * h_l + h_r

    _, h = lax.associative_scan(combine, (a, u), axis=1)
    return h


def chunk_gated_delta_rule(q, k, v, g, beta):
    bsz, nh, s, dk = q.shape
    dv = v.shape[-1]
    n = s // DN_CHUNK
    c = DN_CHUNK
    rs = lambda t: t.reshape((bsz, nh, n, c) + t.shape[3:])
    q, k, v, g, beta = rs(q), rs(k), rs(v), rs(g), rs(beta)
    G = jnp.cumsum(g, axis=-1)
    causal = jnp.tril(jnp.ones((c, c), dtype=bool))
    strict = jnp.tril(jnp.ones((c, c), dtype=bool), k=-1)
    diff = G[..., :, None] - G[..., None, :]
    decay = jnp.exp(jnp.where(causal, diff, -jnp.inf))
    kb = k * beta[..., None]
    A = jnp.where(strict, jnp.einsum('bhnid,bhnjd->bhnij', kb, k) * decay, 0.0)
    T = A + jnp.eye(c, dtype=F32)
    u = lax.linalg.triangular_solve(T, v * beta[..., None], left_side=True, lower=True)
    w = lax.linalg.triangular_solve(T, kb * jnp.exp(G)[..., None], left_side=True, lower=True)
    qk = jnp.einsum('bhnid,bhnjd->bhnij', q, k) * decay
    q_dec = q * jnp.exp(G)[..., None]
    k_dec = k * jnp.exp(G[..., -1:] - G)[..., None]
    g_last = jnp.exp(G[..., -1])
    xs = tuple(jnp.moveaxis(t, 2, 0) for t in (qk, q_dec, k_dec, u, w, g_last))

    def step(state, inp):
        qk_n, qd_n, kd_n, u_n, w_n, gl_n = inp
        v_new = u_n - jnp.einsum('bhcd,bhde->bhce', w_n, state)
        o = jnp.einsum('bhcd,bhde->bhce', qd_n, state) + jnp.einsum('bhij,bhje->bhie', qk_n, v_new)
        state = gl_n[..., None, None] * state + jnp.einsum('bhcd,bhce->bhde', kd_n, v_new)
        return state, o

    _, o = lax.scan(step, jnp.zeros((bsz, nh, dk, dv), F32), xs)
    return jnp.moveaxis(o, 0, 2).reshape(bsz, nh, s, dv)


def gated_deltanet(q, k, v, z, a_in, b_in, conv_w, a_log, dt_bias, norm_w):
    bsz, s, _ = q.shape
    qkv = jax.nn.silu(causal_dwconv(jnp.concatenate([q, k, v], axis=-1).astype(F32), conv_w.astype(F32)))
    q, k, v = split_cols(qkv, (DN_QK, DN_QK, DN_V))
    rep = DN_V_HEADS // DN_QK_HEADS
    q = jnp.repeat(l2_normalize(q.reshape(bsz, s, DN_QK_HEADS, DN_DK)), rep, axis=2) * (DN_DK ** -0.5)
    k = jnp.repeat(l2_normalize(k.reshape(bsz, s, DN_QK_HEADS, DN_DK)), rep, axis=2)
    v = v.reshape(bsz, s, DN_V_HEADS, DN_DV)
    beta = jax.nn.sigmoid(b_in.astype(F32))
    g = -jnp.exp(a_log.astype(F32)) * jax.nn.softplus(a_in.astype(F32) + dt_bias.astype(F32))
    o = chunk_gated_delta_rule(jnp.swapaxes(q, 1, 2), jnp.swapaxes(k, 1, 2), jnp.swapaxes(v, 1, 2),
                               jnp.swapaxes(g, 1, 2), jnp.swapaxes(beta, 1, 2))
    o = jnp.swapaxes(o, 1, 2)
    o = o * lax.rsqrt(jnp.mean(o * o, axis=-1, keepdims=True) + RMS_EPS) * norm_w.astype(F32)
    o = o * jax.nn.silu(z.astype(F32).reshape(bsz, s, DN_V_HEADS, DN_DV))
    return o.reshape(bsz, s, DN_V)


def token_mixer(h, w_in, rg_conv_w, rg_conv_b, rg_w_a, rg_b_a, rg_w_x, rg_b_x, rg_lambda,
                dn_conv_w, dn_a_log, dn_dt_bias, dn_norm_w, w_proj_a, w_proj_b, w_out):
    proj = h @ w_in
    xr, gr, q, k, v, z, a_in, b_in, g_a, g_b = split_cols(proj, IN_SPLITS)
    xr = causal_dwconv(xr, rg_conv_w) + rg_conv_b
    rec = rg_lru(xr, rg_w_a, rg_b_a, rg_w_x, rg_b_x, rg_lambda) * jax.nn.gelu(gr.astype(F32))
    y_a = rec.astype(h.dtype) @ w_proj_a
    dn = gated_deltanet(q, k, v, z, a_in, b_in, dn_conv_w, dn_a_log, dn_dt_bias, dn_norm_w)
    y_b = dn.astype(h.dtype) @ w_proj_b
    merged = jax.nn.sigmoid(g_a) * y_a + jax.nn.sigmoid(g_b) * y_b
    return merged @ w_out


def conv_ffn(h, w_gate, w_up, conv_w, conv_b, w_down):
    gate = causal_dwconv(h @ w_gate, conv_w) + conv_b
    return (jax.nn.gelu(gate) * (h @ w_up)) @ w_down


def _fwd_setup_inputs(seed: int = 0) -> dict:
    key = jax.random.key(seed)
    ks = jax.random.split(key, 32)
    L, D = DEPTH, D_MODEL
    nrm = lambda kk, shape, scale: jax.random.normal(kk, shape, F32) * scale
    u_a = jax.random.uniform(ks[9], (L, D_RNN), F32, 0.9, 0.999)
    s_a = u_a ** (1.0 / RG_C)
    rg_lambda = jnp.log(s_a) - jnp.log1p(-s_a)
    dt = jnp.exp(jax.random.uniform(ks[12], (L, DN_V_HEADS), F32, np.log(1e-3), np.log(1e-1)))
    dt = jnp.maximum(dt, 1e-4)
    return {
        "x": nrm(ks[0], (BATCH, SEQ, D), 1.0),
        "c": nrm(ks[1], (BATCH, D), 1.0),
        "w_ada": nrm(ks[2], (L, D, 6 * D), 0.1 * D ** -0.5),
        "b_ada": nrm(ks[3], (L, 6 * D), 0.01),
        "w_in": nrm(ks[4], (L, D, D_IN), D ** -0.5),
        "rg_conv_w": nrm(ks[5], (L, RG_CONV, D_RNN), RG_CONV ** -0.5),
        "rg_conv_b": nrm(ks[6], (L, D_RNN), 0.01),
        "rg_w_a": nrm(ks[7], (L, RG_BLOCKS, RG_BLOCK_W, RG_BLOCK_W), RG_BLOCK_W ** -0.5),
        "rg_b_a": nrm(ks[8], (L, D_RNN), 0.01),
        "rg_w_x": nrm(ks[10], (L, RG_BLOCKS, RG_BLOCK_W, RG_BLOCK_W), RG_BLOCK_W ** -0.5),
        "rg_b_x": nrm(ks[11], (L, D_RNN), 0.01),
        "rg_lambda": rg_lambda,
        "dn_conv_w": nrm(ks[13], (L, DN_CONV, 2 * DN_QK + DN_V), DN_CONV ** -0.5),
        "dn_a_log": jnp.log(jax.random.uniform(ks[14], (L, DN_V_HEADS), F32, 1.0, 16.0)),
        "dn_dt_bias": dt + jnp.log(-jnp.expm1(-dt)),
        "dn_norm_w": 1.0 + nrm(ks[15], (L, DN_DV), 0.02),
        "w_proj_a": nrm(ks[16], (L, D_RNN, D), D_RNN ** -0.5),
        "w_proj_b": nrm(ks[17], (L, DN_V, D), DN_V ** -0.5),
        "w_out": nrm(ks[18], (L, D, D), DEEPNORM_BETA * D ** -0.5),
        "ln1_g": 1.0 + nrm(ks[19], (L, D), 0.02),
        "ln1_b": nrm(ks[20], (L, D), 0.01),
        "ffn_w_gate": nrm(ks[21], (L, D, D_FF), D ** -0.5),
        "ffn_w_up": nrm(ks[22], (L, D, D_FF), D ** -0.5),
        "ffn_conv_w": nrm(ks[23], (L, FFN_CONV, D_FF), FFN_CONV ** -0.5),
        "ffn_conv_b": nrm(ks[24], (L, D_FF), 0.01),
        "ffn_w_down": nrm(ks[25], (L, D_FF, D), DEEPNORM_BETA * D_FF ** -0.5),
        "ln2_g": 1.0 + nrm(ks[26], (L, D), 0.02),
        "ln2_b": nrm(ks[27], (L, D), 0.01),
    }


def _fwd_reference(x, c, w_ada, b_ada, w_in, rg_conv_w, rg_conv_b, rg_w_a, rg_b_a, rg_w_x, rg_b_x,
              rg_lambda, dn_conv_w, dn_a_log, dn_dt_bias, dn_norm_w, w_proj_a, w_proj_b, w_out,
              ln1_g, ln1_b, ffn_w_gate, ffn_w_up, ffn_conv_w, ffn_conv_b, ffn_w_down, ln2_g, ln2_b):
    for l in range(DEPTH):
        ada = jax.nn.silu(c) @ w_ada[l] + b_ada[l]
        sh1, sc1, gt1, sh2, sc2, gt2 = [t[:, None, :] for t in jnp.split(ada, 6, axis=-1)]
        h = x * (1.0 + sc1) + sh1
        mix = token_mixer(h, w_in[l], rg_conv_w[l], rg_conv_b[l], rg_w_a[l], rg_b_a[l], rg_w_x[l],
                          rg_b_x[l], rg_lambda[l], dn_conv_w[l], dn_a_log[l], dn_dt_bias[l],
                          dn_norm_w[l], w_proj_a[l], w_proj_b[l], w_out[l])
        x = layer_norm(DEEPNORM_ALPHA * x + (1.0 + gt1) * mix, ln1_g[l], ln1_b[l])
        h = x * (1.0 + sc2) + sh2
        ff = conv_ffn(h, ffn_w_gate[l], ffn_w_up[l], ffn_conv_w[l], ffn_conv_b[l], ffn_w_down[l])
        x = layer_norm(DEEPNORM_ALPHA * x + (1.0 + gt2) * ff, ln2_g[l], ln2_b[l])
    return x


import jax as _jax
import jax.numpy as _jnp

TWIN_FORMAT = 'train_step'
FWD_PARAMS = ['x', 'c', 'w_ada', 'b_ada', 'w_in', 'rg_conv_w', 'rg_conv_b', 'rg_w_a', 'rg_b_a', 'rg_w_x', 'rg_b_x', 'rg_lambda', 'dn_conv_w', 'dn_a_log', 'dn_dt_bias', 'dn_norm_w', 'w_proj_a', 'w_proj_b', 'w_out', 'ln1_g', 'ln1_b', 'ffn_w_gate', 'ffn_w_up', 'ffn_conv_w', 'ffn_conv_b', 'ffn_w_down', 'ln2_g', 'ln2_b']
TWIN_WEIGHTS = ['w_ada', 'b_ada', 'w_in', 'rg_conv_w', 'rg_conv_b', 'rg_w_a', 'rg_b_a', 'rg_w_x', 'rg_b_x', 'rg_lambda', 'dn_conv_w', 'dn_a_log', 'dn_dt_bias', 'dn_norm_w', 'w_proj_a', 'w_proj_b', 'w_out', 'ln1_g', 'ln1_b', 'ffn_w_gate', 'ffn_w_up', 'ffn_conv_w', 'ffn_conv_b', 'ffn_w_down', 'ln2_g', 'ln2_b']
TWIN_DIFF_INPUT = 'x'
TWIN_INPUTS = ['x', 'c', 'w_ada', 'b_ada', 'w_in', 'rg_conv_w', 'rg_conv_b', 'rg_w_a', 'rg_b_a', 'rg_w_x', 'rg_b_x', 'rg_lambda', 'dn_conv_w', 'dn_a_log', 'dn_dt_bias', 'dn_norm_w', 'w_proj_a', 'w_proj_b', 'w_out', 'ln1_g', 'ln1_b', 'ffn_w_gate', 'ffn_w_up', 'ffn_conv_w', 'ffn_conv_b', 'ffn_w_down', 'ln2_g', 'ln2_b', 'loss_target', 'm_w_ada', 'm_b_ada', 'm_w_in', 'm_rg_conv_w', 'm_rg_conv_b', 'm_rg_w_a', 'm_rg_b_a', 'm_rg_w_x', 'm_rg_b_x', 'm_rg_lambda', 'm_dn_conv_w', 'm_dn_a_log', 'm_dn_dt_bias', 'm_dn_norm_w', 'm_w_proj_a', 'm_w_proj_b', 'm_w_out', 'm_ln1_g', 'm_ln1_b', 'm_ffn_w_gate', 'm_ffn_w_up', 'm_ffn_conv_w', 'm_ffn_conv_b', 'm_ffn_w_down', 'm_ln2_g', 'm_ln2_b', 'v_w_ada', 'v_b_ada', 'v_w_in', 'v_rg_conv_w', 'v_rg_conv_b', 'v_rg_w_a', 'v_rg_b_a', 'v_rg_w_x', 'v_rg_b_x', 'v_rg_lambda', 'v_dn_conv_w', 'v_dn_a_log', 'v_dn_dt_bias', 'v_dn_norm_w', 'v_w_proj_a', 'v_w_proj_b', 'v_w_out', 'v_ln1_g', 'v_ln1_b', 'v_ffn_w_gate', 'v_ffn_w_up', 'v_ffn_conv_w', 'v_ffn_conv_b', 'v_ffn_w_down', 'v_ln2_g', 'v_ln2_b']
TWIN_OUTPUTS = ['loss', 'grad_x', 'grad_w_ada', 'grad_b_ada', 'grad_w_in', 'grad_rg_conv_w', 'grad_rg_conv_b', 'grad_rg_w_a', 'grad_rg_b_a', 'grad_rg_w_x', 'grad_rg_b_x', 'grad_rg_lambda', 'grad_dn_conv_w', 'grad_dn_a_log', 'grad_dn_dt_bias', 'grad_dn_norm_w', 'grad_w_proj_a', 'grad_w_proj_b', 'grad_w_out', 'grad_ln1_g', 'grad_ln1_b', 'grad_ffn_w_gate', 'grad_ffn_w_up', 'grad_ffn_conv_w', 'grad_ffn_conv_b', 'grad_ffn_w_down', 'grad_ln2_g', 'grad_ln2_b', 'delta_w_ada', 'delta_b_ada', 'delta_w_in', 'delta_rg_conv_w', 'delta_rg_conv_b', 'delta_rg_w_a', 'delta_rg_b_a', 'delta_rg_w_x', 'delta_rg_b_x', 'delta_rg_lambda', 'delta_dn_conv_w', 'delta_dn_a_log', 'delta_dn_dt_bias', 'delta_dn_norm_w', 'delta_w_proj_a', 'delta_w_proj_b', 'delta_w_out', 'delta_ln1_g', 'delta_ln1_b', 'delta_ffn_w_gate', 'delta_ffn_w_up', 'delta_ffn_conv_w', 'delta_ffn_conv_b', 'delta_ffn_w_down', 'delta_ln2_g', 'delta_ln2_b', 'new_m_w_ada', 'new_m_b_ada', 'new_m_w_in', 'new_m_rg_conv_w', 'new_m_rg_conv_b', 'new_m_rg_w_a', 'new_m_rg_b_a', 'new_m_rg_w_x', 'new_m_rg_b_x', 'new_m_rg_lambda', 'new_m_dn_conv_w', 'new_m_dn_a_log', 'new_m_dn_dt_bias', 'new_m_dn_norm_w', 'new_m_w_proj_a', 'new_m_w_proj_b', 'new_m_w_out', 'new_m_ln1_g', 'new_m_ln1_b', 'new_m_ffn_w_gate', 'new_m_ffn_w_up', 'new_m_ffn_conv_w', 'new_m_ffn_conv_b', 'new_m_ffn_w_down', 'new_m_ln2_g', 'new_m_ln2_b', 'new_v_w_ada', 'new_v_b_ada', 'new_v_w_in', 'new_v_rg_conv_w', 'new_v_rg_conv_b', 'new_v_rg_w_a', 'new_v_rg_b_a', 'new_v_rg_w_x', 'new_v_rg_b_x', 'new_v_rg_lambda', 'new_v_dn_conv_w', 'new_v_dn_a_log', 'new_v_dn_dt_bias', 'new_v_dn_norm_w', 'new_v_w_proj_a', 'new_v_w_proj_b', 'new_v_w_out', 'new_v_ln1_g', 'new_v_ln1_b', 'new_v_ffn_w_gate', 'new_v_ffn_w_up', 'new_v_ffn_conv_w', 'new_v_ffn_conv_b', 'new_v_ffn_w_down', 'new_v_ln2_g', 'new_v_ln2_b']
TWIN_LEAF_KINDS = {'loss': 'loss', 'grad_x': 'grad_x', 'grad_w_ada': 'grad_w', 'grad_b_ada': 'grad_w', 'grad_w_in': 'grad_w', 'grad_rg_conv_w': 'grad_w', 'grad_rg_conv_b': 'grad_w', 'grad_rg_w_a': 'grad_w', 'grad_rg_b_a': 'grad_w', 'grad_rg_w_x': 'grad_w', 'grad_rg_b_x': 'grad_w', 'grad_rg_lambda': 'grad_w', 'grad_dn_conv_w': 'grad_w', 'grad_dn_a_log': 'grad_w', 'grad_dn_dt_bias': 'grad_w', 'grad_dn_norm_w': 'grad_w', 'grad_w_proj_a': 'grad_w', 'grad_w_proj_b': 'grad_w', 'grad_w_out': 'grad_w', 'grad_ln1_g': 'grad_w', 'grad_ln1_b': 'grad_w', 'grad_ffn_w_gate': 'grad_w', 'grad_ffn_w_up': 'grad_w', 'grad_ffn_conv_w': 'grad_w', 'grad_ffn_conv_b': 'grad_w', 'grad_ffn_w_down': 'grad_w', 'grad_ln2_g': 'grad_w', 'grad_ln2_b': 'grad_w', 'delta_w_ada': 'delta_w', 'delta_b_ada': 'delta_w', 'delta_w_in': 'delta_w', 'delta_rg_conv_w': 'delta_w', 'delta_rg_conv_b': 'delta_w', 'delta_rg_w_a': 'delta_w', 'delta_rg_b_a': 'delta_w', 'delta_rg_w_x': 'delta_w', 'delta_rg_b_x': 'delta_w', 'delta_rg_lambda': 'delta_w', 'delta_dn_conv_w': 'delta_w', 'delta_dn_a_log': 'delta_w', 'delta_dn_dt_bias': 'delta_w', 'delta_dn_norm_w': 'delta_w', 'delta_w_proj_a': 'delta_w', 'delta_w_proj_b': 'delta_w', 'delta_w_out': 'delta_w', 'delta_ln1_g': 'delta_w', 'delta_ln1_b': 'delta_w', 'delta_ffn_w_gate': 'delta_w', 'delta_ffn_w_up': 'delta_w', 'delta_ffn_conv_w': 'delta_w', 'delta_ffn_conv_b': 'delta_w', 'delta_ffn_w_down': 'delta_w', 'delta_ln2_g': 'delta_w', 'delta_ln2_b': 'delta_w', 'new_m_w_ada': 'new_m', 'new_m_b_ada': 'new_m', 'new_m_w_in': 'new_m', 'new_m_rg_conv_w': 'new_m', 'new_m_rg_conv_b': 'new_m', 'new_m_rg_w_a': 'new_m', 'new_m_rg_b_a': 'new_m', 'new_m_rg_w_x': 'new_m', 'new_m_rg_b_x': 'new_m', 'new_m_rg_lambda': 'new_m', 'new_m_dn_conv_w': 'new_m', 'new_m_dn_a_log': 'new_m', 'new_m_dn_dt_bias': 'new_m', 'new_m_dn_norm_w': 'new_m', 'new_m_w_proj_a': 'new_m', 'new_m_w_proj_b': 'new_m', 'new_m_w_out': 'new_m', 'new_m_ln1_g': 'new_m', 'new_m_ln1_b': 'new_m', 'new_m_ffn_w_gate': 'new_m', 'new_m_ffn_w_up': 'new_m', 'new_m_ffn_conv_w': 'new_m', 'new_m_ffn_conv_b': 'new_m', 'new_m_ffn_w_down': 'new_m', 'new_m_ln2_g': 'new_m', 'new_m_ln2_b': 'new_m', 'new_v_w_ada': 'new_v', 'new_v_b_ada': 'new_v', 'new_v_w_in': 'new_v', 'new_v_rg_conv_w': 'new_v', 'new_v_rg_conv_b': 'new_v', 'new_v_rg_w_a': 'new_v', 'new_v_rg_b_a': 'new_v', 'new_v_rg_w_x': 'new_v', 'new_v_rg_b_x': 'new_v', 'new_v_rg_lambda': 'new_v', 'new_v_dn_conv_w': 'new_v', 'new_v_dn_a_log': 'new_v', 'new_v_dn_dt_bias': 'new_v', 'new_v_dn_norm_w': 'new_v', 'new_v_w_proj_a': 'new_v', 'new_v_w_proj_b': 'new_v', 'new_v_w_out': 'new_v', 'new_v_ln1_g': 'new_v', 'new_v_ln1_b': 'new_v', 'new_v_ffn_w_gate': 'new_v', 'new_v_ffn_w_up': 'new_v', 'new_v_ffn_conv_w': 'new_v', 'new_v_ffn_conv_b': 'new_v', 'new_v_ffn_w_down': 'new_v', 'new_v_ln2_g': 'new_v', 'new_v_ln2_b': 'new_v'}


def _forward(args):
    return _fwd_reference(*[args[k] for k in FWD_PARAMS])


def _output_shape():
    def fwd():
        inp = _fwd_setup_inputs(0)
        return _fwd_reference(*[inp[k] for k in FWD_PARAMS])
    out = _jax.eval_shape(fwd)
    return out.shape, out.dtype

N_MICROBATCH = 1
ADAM_LR = 0.001
ADAM_B1 = 0.9
ADAM_B2 = 0.999
ADAM_EPS = 1e-08
ADAM_WD = 0.01
ADAM_STEP = 10
PER_EXAMPLE_BATCH_AXIS = {'x': 0, 'c': 0, 'loss_target': 0}
SHARED_INPUTS = []
_WEIGHT_DTYPES = {'w_ada': _jnp.float32, 'b_ada': _jnp.float32, 'w_in': _jnp.float32, 'rg_conv_w': _jnp.float32, 'rg_conv_b': _jnp.float32, 'rg_w_a': _jnp.float32, 'rg_b_a': _jnp.float32, 'rg_w_x': _jnp.float32, 'rg_b_x': _jnp.float32, 'rg_lambda': _jnp.float32, 'dn_conv_w': _jnp.float32, 'dn_a_log': _jnp.float32, 'dn_dt_bias': _jnp.float32, 'dn_norm_w': _jnp.float32, 'w_proj_a': _jnp.float32, 'w_proj_b': _jnp.float32, 'w_out': _jnp.float32, 'ln1_g': _jnp.float32, 'ln1_b': _jnp.float32, 'ffn_w_gate': _jnp.float32, 'ffn_w_up': _jnp.float32, 'ffn_conv_w': _jnp.float32, 'ffn_conv_b': _jnp.float32, 'ffn_w_down': _jnp.float32, 'ln2_g': _jnp.float32, 'ln2_b': _jnp.float32}
MOMENT_SCALE = {'w_ada': 9.670685e-02, 'b_ada': 1.856971e-01, 'w_in': 2.738259e-02, 'rg_conv_w': 3.747288e-02, 'rg_conv_b': 3.602720e-01, 'rg_w_a': 1.082804e-02, 'rg_b_a': 8.310881e-03, 'rg_w_x': 1.973741e-02, 'rg_b_x': 1.460830e-02, 'rg_lambda': 1.734750e-02, 'dn_conv_w': 2.819768e-02, 'dn_a_log': 9.885778e-02, 'dn_dt_bias': 9.397737e-02, 'dn_norm_w': 1.251683e-01, 'w_proj_a': 3.923184e-02, 'w_proj_b': 4.261718e-02, 'w_out': 9.587748e-02, 'ln1_g': 2.067008e+00, 'ln1_b': 4.698860e-01, 'ffn_w_gate': 4.970077e-02, 'ffn_w_up': 4.823315e-02, 'ffn_conv_w': 5.127184e-02, 'ffn_conv_b': 4.764278e-02, 'ffn_w_down': 1.351975e-01, 'ln2_g': 6.401607e+01, 'ln2_b': 2.408627e+00}


def _to_microbatches(a, axis):
    t = _jnp.moveaxis(a, axis, 0)
    t = t.reshape((N_MICROBATCH, t.shape[0] // N_MICROBATCH) + t.shape[1:])
    return _jnp.moveaxis(t, 1, axis + 1)


def setup_inputs(seed: int = 0) -> dict:
    inp = _fwd_setup_inputs(seed)
    key = _jax.random.fold_in(_jax.random.key(seed), 7919)
    shape, _ = _output_shape()
    out = dict(inp)
    out["loss_target"] = _jax.random.normal(_jax.random.fold_in(key, 0), shape, _jnp.float32)
    for i, name in enumerate(TWIN_WEIGHTS):
        w = inp[name].astype(_jnp.float32)
        if MOMENT_SCALE is None:
            s = _jnp.sqrt(_jnp.mean(_jnp.square(w)) + 1e-30)
        else:
            s = MOMENT_SCALE[name]
        km, kv = _jax.random.split(_jax.random.fold_in(key, i + 1))
        out[name] = w
        out["m_" + name] = s * _jax.random.normal(km, w.shape, _jnp.float32)
        out["v_" + name] = (s * s) * _jax.random.uniform(kv, w.shape, _jnp.float32, 0.5, 1.5)
    if N_MICROBATCH > 1:
        for name, axis in PER_EXAMPLE_BATCH_AXIS.items():
            out[name] = _to_microbatches(out[name], axis)
    return {'x': out['x'], 'c': out['c'], 'w_ada': out['w_ada'], 'b_ada': out['b_ada'], 'w_in': out['w_in'], 'rg_conv_w': out['rg_conv_w'], 'rg_conv_b': out['rg_conv_b'], 'rg_w_a': out['rg_w_a'], 'rg_b_a': out['rg_b_a'], 'rg_w_x': out['rg_w_x'], 'rg_b_x': out['rg_b_x'], 'rg_lambda': out['rg_lambda'], 'dn_conv_w': out['dn_conv_w'], 'dn_a_log': out['dn_a_log'], 'dn_dt_bias': out['dn_dt_bias'], 'dn_norm_w': out['dn_norm_w'], 'w_proj_a': out['w_proj_a'], 'w_proj_b': out['w_proj_b'], 'w_out': out['w_out'], 'ln1_g': out['ln1_g'], 'ln1_b': out['ln1_b'], 'ffn_w_gate': out['ffn_w_gate'], 'ffn_w_up': out['ffn_w_up'], 'ffn_conv_w': out['ffn_conv_w'], 'ffn_conv_b': out['ffn_conv_b'], 'ffn_w_down': out['ffn_w_down'], 'ln2_g': out['ln2_g'], 'ln2_b': out['ln2_b'], 'loss_target': out['loss_target'], 'm_w_ada': out['m_w_ada'], 'm_b_ada': out['m_b_ada'], 'm_w_in': out['m_w_in'], 'm_rg_conv_w': out['m_rg_conv_w'], 'm_rg_conv_b': out['m_rg_conv_b'], 'm_rg_w_a': out['m_rg_w_a'], 'm_rg_b_a': out['m_rg_b_a'], 'm_rg_w_x': out['m_rg_w_x'], 'm_rg_b_x': out['m_rg_b_x'], 'm_rg_lambda': out['m_rg_lambda'], 'm_dn_conv_w': out['m_dn_conv_w'], 'm_dn_a_log': out['m_dn_a_log'], 'm_dn_dt_bias': out['m_dn_dt_bias'], 'm_dn_norm_w': out['m_dn_norm_w'], 'm_w_proj_a': out['m_w_proj_a'], 'm_w_proj_b': out['m_w_proj_b'], 'm_w_out': out['m_w_out'], 'm_ln1_g': out['m_ln1_g'], 'm_ln1_b': out['m_ln1_b'], 'm_ffn_w_gate': out['m_ffn_w_gate'], 'm_ffn_w_up': out['m_ffn_w_up'], 'm_ffn_conv_w': out['m_ffn_conv_w'], 'm_ffn_conv_b': out['m_ffn_conv_b'], 'm_ffn_w_down': out['m_ffn_w_down'], 'm_ln2_g': out['m_ln2_g'], 'm_ln2_b': out['m_ln2_b'], 'v_w_ada': out['v_w_ada'], 'v_b_ada': out['v_b_ada'], 'v_w_in': out['v_w_in'], 'v_rg_conv_w': out['v_rg_conv_w'], 'v_rg_conv_b': out['v_rg_conv_b'], 'v_rg_w_a': out['v_rg_w_a'], 'v_rg_b_a': out['v_rg_b_a'], 'v_rg_w_x': out['v_rg_w_x'], 'v_rg_b_x': out['v_rg_b_x'], 'v_rg_lambda': out['v_rg_lambda'], 'v_dn_conv_w': out['v_dn_conv_w'], 'v_dn_a_log': out['v_dn_a_log'], 'v_dn_dt_bias': out['v_dn_dt_bias'], 'v_dn_norm_w': out['v_dn_norm_w'], 'v_w_proj_a': out['v_w_proj_a'], 'v_w_proj_b': out['v_w_proj_b'], 'v_w_out': out['v_w_out'], 'v_ln1_g': out['v_ln1_g'], 'v_ln1_b': out['v_ln1_b'], 'v_ffn_w_gate': out['v_ffn_w_gate'], 'v_ffn_w_up': out['v_ffn_w_up'], 'v_ffn_conv_w': out['v_ffn_conv_w'], 'v_ffn_conv_b': out['v_ffn_conv_b'], 'v_ffn_w_down': out['v_ffn_w_down'], 'v_ln2_g': out['v_ln2_g'], 'v_ln2_b': out['v_ln2_b']}


def _loss(weights, diff, rest, loss_target):
    with _jax.named_scope("forward"):
        args = {**rest, TWIN_DIFF_INPUT: diff, **{k: w.astype(_WEIGHT_DTYPES[k]) for k, w in weights.items()}}
        y = _forward(args)
    with _jax.named_scope("loss_head"):
        err = _jnp.square(y.astype(_jnp.float32) - loss_target)
        return 0.5 * _jnp.sum(_jnp.mean(err, axis=-1)) if err.ndim else 0.5 * err


def _adamw(w, g, m, v):
    m = ADAM_B1 * m + (1.0 - ADAM_B1) * g
    v = ADAM_B2 * v + (1.0 - ADAM_B2) * _jnp.square(g)
    m_hat = m / (1.0 - ADAM_B1 ** ADAM_STEP)
    v_hat = v / (1.0 - ADAM_B2 ** ADAM_STEP)
    delta = -ADAM_LR * (m_hat / (_jnp.sqrt(v_hat) + ADAM_EPS) + ADAM_WD * w)
    return delta, m, v


def reference(x, c, w_ada, b_ada, w_in, rg_conv_w, rg_conv_b, rg_w_a, rg_b_a, rg_w_x, rg_b_x, rg_lambda, dn_conv_w, dn_a_log, dn_dt_bias, dn_norm_w, w_proj_a, w_proj_b, w_out, ln1_g, ln1_b, ffn_w_gate, ffn_w_up, ffn_conv_w, ffn_conv_b, ffn_w_down, ln2_g, ln2_b, loss_target, m_w_ada, m_b_ada, m_w_in, m_rg_conv_w, m_rg_conv_b, m_rg_w_a, m_rg_b_a, m_rg_w_x, m_rg_b_x, m_rg_lambda, m_dn_conv_w, m_dn_a_log, m_dn_dt_bias, m_dn_norm_w, m_w_proj_a, m_w_proj_b, m_w_out, m_ln1_g, m_ln1_b, m_ffn_w_gate, m_ffn_w_up, m_ffn_conv_w, m_ffn_conv_b, m_ffn_w_down, m_ln2_g, m_ln2_b, v_w_ada, v_b_ada, v_w_in, v_rg_conv_w, v_rg_conv_b, v_rg_w_a, v_rg_b_a, v_rg_w_x, v_rg_b_x, v_rg_lambda, v_dn_conv_w, v_dn_a_log, v_dn_dt_bias, v_dn_norm_w, v_w_proj_a, v_w_proj_b, v_w_out, v_ln1_g, v_ln1_b, v_ffn_w_gate, v_ffn_w_up, v_ffn_conv_w, v_ffn_conv_b, v_ffn_w_down, v_ln2_g, v_ln2_b):
    given = dict(x=x, c=c, w_ada=w_ada, b_ada=b_ada, w_in=w_in, rg_conv_w=rg_conv_w, rg_conv_b=rg_conv_b, rg_w_a=rg_w_a, rg_b_a=rg_b_a, rg_w_x=rg_w_x, rg_b_x=rg_b_x, rg_lambda=rg_lambda, dn_conv_w=dn_conv_w, dn_a_log=dn_a_log, dn_dt_bias=dn_dt_bias, dn_norm_w=dn_norm_w, w_proj_a=w_proj_a, w_proj_b=w_proj_b, w_out=w_out, ln1_g=ln1_g, ln1_b=ln1_b, ffn_w_gate=ffn_w_gate, ffn_w_up=ffn_w_up, ffn_conv_w=ffn_conv_w, ffn_conv_b=ffn_conv_b, ffn_w_down=ffn_w_down, ln2_g=ln2_g, ln2_b=ln2_b, loss_target=loss_target, m_w_ada=m_w_ada, m_b_ada=m_b_ada, m_w_in=m_w_in, m_rg_conv_w=m_rg_conv_w, m_rg_conv_b=m_rg_conv_b, m_rg_w_a=m_rg_w_a, m_rg_b_a=m_rg_b_a, m_rg_w_x=m_rg_w_x, m_rg_b_x=m_rg_b_x, m_rg_lambda=m_rg_lambda, m_dn_conv_w=m_dn_conv_w, m_dn_a_log=m_dn_a_log, m_dn_dt_bias=m_dn_dt_bias, m_dn_norm_w=m_dn_norm_w, m_w_proj_a=m_w_proj_a, m_w_proj_b=m_w_proj_b, m_w_out=m_w_out, m_ln1_g=m_ln1_g, m_ln1_b=m_ln1_b, m_ffn_w_gate=m_ffn_w_gate, m_ffn_w_up=m_ffn_w_up, m_ffn_conv_w=m_ffn_conv_w, m_ffn_conv_b=m_ffn_conv_b, m_ffn_w_down=m_ffn_w_down, m_ln2_g=m_ln2_g, m_ln2_b=m_ln2_b, v_w_ada=v_w_ada, v_b_ada=v_b_ada, v_w_in=v_w_in, v_rg_conv_w=v_rg_conv_w, v_rg_conv_b=v_rg_conv_b, v_rg_w_a=v_rg_w_a, v_rg_b_a=v_rg_b_a, v_rg_w_x=v_rg_w_x, v_rg_b_x=v_rg_b_x, v_rg_lambda=v_rg_lambda, v_dn_conv_w=v_dn_conv_w, v_dn_a_log=v_dn_a_log, v_dn_dt_bias=v_dn_dt_bias, v_dn_norm_w=v_dn_norm_w, v_w_proj_a=v_w_proj_a, v_w_proj_b=v_w_proj_b, v_w_out=v_w_out, v_ln1_g=v_ln1_g, v_ln1_b=v_ln1_b, v_ffn_w_gate=v_ffn_w_gate, v_ffn_w_up=v_ffn_w_up, v_ffn_conv_w=v_ffn_conv_w, v_ffn_conv_b=v_ffn_conv_b, v_ffn_w_down=v_ffn_w_down, v_ln2_g=v_ln2_g, v_ln2_b=v_ln2_b)
    weights = {n: given[n] for n in TWIN_WEIGHTS}
    shared = {n: given[n] for n in SHARED_INPUTS}
    per_example = {n: given[n] for n in ['x', 'c']}
    grad_fn = _jax.value_and_grad(_loss, argnums=(0, 1))

    def one_microbatch(ex, loss_target):
        ex = dict(ex)
        diff = ex.pop(TWIN_DIFF_INPUT)
        return grad_fn(weights, diff, {**shared, **ex}, loss_target)

    if N_MICROBATCH == 1:
        loss, (grad_w, grad_x) = one_microbatch(per_example, given["loss_target"])
    else:
        def body(carry, xs):
            loss_sum, grad_sum = carry
            l_k, (gw_k, gx_k) = one_microbatch(xs[0], xs[1])
            with _jax.named_scope("update"):
                return (loss_sum + l_k, _jax.tree.map(_jnp.add, grad_sum, gw_k)), gx_k

        init = (_jnp.zeros((), _jnp.float32), _jax.tree.map(_jnp.zeros_like, weights))
        (loss, grad_w), grad_x = _jax.lax.scan(body, init, (per_example, given["loss_target"]))
    with _jax.named_scope("update"):
        delta_w, new_m, new_v = {}, {}, {}
        for n in TWIN_WEIGHTS:
            delta_w[n], new_m[n], new_v[n] = _adamw(weights[n], grad_w[n], given["m_" + n], given["v_" + n])
    return (loss, grad_x, *[grad_w[n] for n in TWIN_WEIGHTS], *[delta_w[n] for n in TWIN_WEIGHTS],
            *[new_m[n] for n in TWIN_WEIGHTS], *[new_v[n] for n in TWIN_WEIGHTS])
```

```python
import functools
import math

import jax
import jax.numpy as jnp
from jax import lax
from jax.experimental import pallas as pl
from jax.experimental.pallas import tpu as pltpu

F32 = jnp.float32
BF16 = jnp.bfloat16
MXU_DTYPE = BF16
WIRE_DTYPE = BF16
HI = lax.Precision.HIGHEST
MESH = pl.DeviceIdType.MESH

N_DEV = 8
LANES = 128
SUBLANES = 8
VMEM_LIMIT = 56 * 1024 * 1024

RG_C = 8.0
DN_CHUNK = 64
LN_EPS = 1e-5
RMS_EPS = 1e-6
L2_EPS = 1e-6
DEPTH = 1
DEEPNORM_ALPHA = (2 * DEPTH) ** 0.25
ADAM_LR = 0.001
ADAM_B1 = 0.9
ADAM_B2 = 0.999
ADAM_EPS = 1e-08
ADAM_WD = 0.01
ADAM_STEP = 10


def _tile(n, cap, unit=LANES):
    best = None
    for t in range(unit, min(n, cap) + 1, unit):
        if n % t == 0:
            best = t
    return best if best is not None else n


def _round_up(n, m):
    return (n + m - 1) // m * m


def mm(a, b, *, name, ta=False, tb=False, a_act=None, bias=None, out_dtype=F32,
       tm_cap=1024, tn_cap=512, tk_cap=1024):
    m, k = (a.shape[1], a.shape[0]) if ta else a.shape
    n = b.shape[0] if tb else b.shape[1]
    assert k == (b.shape[1] if tb else b.shape[0]), (a.shape, b.shape, ta, tb)
    tm, tn, tk = _tile(m, tm_cap), _tile(n, tn_cap), _tile(k, tk_cap)
    nk = k // tk
    dims = (((0 if ta else 1,), (1 if tb else 0,)), ((), ()))

    def body(*refs):
        if bias is None:
            a_ref, b_ref, o_ref, acc_ref = refs
        else:
            a_ref, b_ref, bias_ref, o_ref, acc_ref = refs
        kk = pl.program_id(2)

        @pl.when(kk == 0)
        def _():
            acc_ref[...] = jnp.zeros_like(acc_ref)

        av = a_ref[...]
        if a_act == "silu":
            av = jax.nn.silu(av.astype(F32))
        acc_ref[...] += lax.dot_general(av.astype(MXU_DTYPE), b_ref[...].astype(MXU_DTYPE), dims,
                                        preferred_element_type=F32)

        @pl.when(kk == nk - 1)
        def _():
            r = acc_ref[...]
            if bias is not None:
                r = r + bias_ref[...]
            o_ref[...] = r.astype(o_ref.dtype)

    a_spec = pl.BlockSpec((tk, tm), lambda i, j, q: (q, i)) if ta else pl.BlockSpec((tm, tk), lambda i, j, q: (i, q))
    b_spec = pl.BlockSpec((tn, tk), lambda i, j, q: (j, q)) if tb else pl.BlockSpec((tk, tn), lambda i, j, q: (q, j))
    in_specs, args = [a_spec, b_spec], [a, b]
    if bias is not None:
        in_specs.append(pl.BlockSpec((1, tn), lambda i, j, q: (0, j)))
        args.append(bias)
    return pl.pallas_call(
        body, name=name, grid=(m // tm, n // tn, nk),
        in_specs=in_specs, out_specs=pl.BlockSpec((tm, tn), lambda i, j, q: (i, j)),
        out_shape=jax.ShapeDtypeStruct((m, n), out_dtype),
        scratch_shapes=[pltpu.VMEM((tm, tn), F32)],
        compiler_params=pltpu.CompilerParams(dimension_semantics=("parallel", "parallel", "arbitrary"),
                                             vmem_limit_bytes=VMEM_LIMIT),
    )(*args)


class In:
    def __init__(self, arr, block, imap, acc=False, grad=True, parts=None, gshape=None, gimap=None):
        self.arr, self.block, self.imap, self.acc, self.grad, self.parts = arr, block, imap, acc, grad, parts
        self.gshape = arr.shape if gshape is None else gshape
        self.gimap = imap if gimap is None else gimap


class Out:
    def __init__(self, shape, block, imap, dtype=F32):
        self.shape, self.block, self.imap, self.dtype = shape, block, imap, dtype


def _load(in_refs, ins):
    vals = []
    for r, i in zip(in_refs, ins):
        if i.parts is None:
            vals.append(r[...])
        else:
            vals.extend(r[p] for p in i.parts)
    return vals


def _stage_params():
    return pltpu.CompilerParams(dimension_semantics=("parallel", "arbitrary"), vmem_limit_bytes=VMEM_LIMIT)


def stage_fwd(name, f, grid, ins, outs, carries=()):
    n_in, n_out, n_c = len(ins), len(outs), len(carries)

    def body(*refs):
        in_refs, out_refs = refs[:n_in], refs[n_in:n_in + n_out]
        hist_refs, c_refs = refs[n_in + n_out:n_in + n_out + n_c], refs[n_in + n_out + n_c:]
        if n_c:
            @pl.when(pl.program_id(1) == 0)
            def _():
                for c in c_refs:
                    c[...] = jnp.zeros_like(c)
        cin = [c[...] for c in c_refs]
        for h, c in zip(hist_refs, cin):
            h[...] = c
        o, cout = f(*_load(in_refs, ins), *cin)
        for r, v in zip(out_refs, o):
            r[...] = v.astype(r.dtype)
        for c, v in zip(c_refs, cout):
            c[...] = v

    hist_spec = lambda c: pl.BlockSpec((None, None) + tuple(c), lambda o, s: (o, s) + (0,) * len(c))
    res = pl.pallas_call(
        body, name=name, grid=grid,
        in_specs=[pl.BlockSpec(i.block, i.imap) for i in ins],
        out_specs=[pl.BlockSpec(o.block, o.imap) for o in outs] + [hist_spec(c) for c in carries],
        out_shape=[jax.ShapeDtypeStruct(o.shape, o.dtype) for o in outs]
        + [jax.ShapeDtypeStruct(tuple(grid) + tuple(c), F32) for c in carries],
        scratch_shapes=[pltpu.VMEM(tuple(c), F32) for c in carries],
        compiler_params=_stage_params(),
    )(*[i.arr for i in ins])
    return list(res[:n_out]), list(res[n_out:])


def stage_bwd(name, f, grid, ins, outs, cots, carries=(), hists=(), add_to=None, gdtypes=None):
    n_in, n_out, n_c = len(ins), len(outs), len(carries)
    ns = grid[1]
    add_to = add_to or {}
    gdtypes = gdtypes or {}
    add_idx = sorted(add_to)
    g_idx = [k for k, i in enumerate(ins) if i.grad]
    cots = [c if isinstance(c, (tuple, list)) else (c,) for c in cots]
    n_cot = [len(c) for c in cots]
    rev = lambda imap: (lambda o, s: imap(o, ns - 1 - s))

    def body(*refs):
        p = 0
        in_refs = refs[p:p + n_in]; p += n_in
        cot_refs = []
        for cnt in n_cot:
            cot_refs.append(refs[p:p + cnt]); p += cnt
        hist_refs = refs[p:p + n_c]; p += n_c
        add_refs = refs[p:p + len(add_idx)]; p += len(add_idx)
        g_refs = refs[p:p + len(g_idx)]; p += len(g_idx)
        dc_refs = refs[p:]
        first = pl.program_id(1) == 0
        if n_c:
            @pl.when(first)
            def _():
                for c in dc_refs:
                    c[...] = jnp.zeros_like(c)
        vals = _load(in_refs, ins)
        cin = [h[...] for h in hist_refs]
        (o, cout), vjp = jax.vjp(lambda *a: f(*a), *vals, *cin)
        cot_o = []
        for crs, v in zip(cot_refs, o):
            c = crs[0][...].astype(v.dtype)
            for extra in crs[1:]:
                c = c + extra[...].astype(v.dtype)
            cot_o.append(c)
        cot_c = tuple(c[...] for c in dc_refs)
        grads = vjp((tuple(cot_o), cot_c))
        pos, per_in = 0, []
        for i in ins:
            cnt = 1 if i.parts is None else len(i.parts)
            per_in.append(grads[pos:pos + cnt])
            pos += cnt
        dcin = grads[pos:]
        for gr, k in zip(g_refs, g_idx):
            i, gs = ins[k], per_in[k]
            if i.acc:
                @pl.when(first)
                def _(gr=gr):
                    gr[...] = jnp.zeros_like(gr)
                if i.parts is None:
                    gr[...] += gs[0].astype(gr.dtype)
                else:
                    for pt, g in zip(i.parts, gs):
                        gr[pt] += g.astype(gr.dtype)
            else:
                g = gs[0]
                if k in add_to:
                    g = g + add_refs[add_idx.index(k)][...].astype(g.dtype)
                gr[...] = g.astype(gr.dtype)
        for c, v in zip(dc_refs, dcin):
            c[...] = v

    in_specs = [pl.BlockSpec(i.block, rev(i.imap)) for i in ins]
    for o_, cnt in zip(outs, n_cot):
        in_specs += [pl.BlockSpec(o_.block, rev(o_.imap))] * cnt
    in_specs += [pl.BlockSpec((None, None) + tuple(c), (lambda c: (lambda o, s: (o, ns - 1 - s) + (0,) * len(c)))(c))
                 for c in carries]
    in_specs += [pl.BlockSpec(ins[k].block, rev(ins[k].gimap)) for k in add_idx]
    out_specs, out_shape = [], []
    for k in g_idx:
        i = ins[k]
        if i.acc:
            out_specs.append(pl.BlockSpec(i.block, (lambda im: (lambda o, s: im(o, 0)))(i.imap)))
        else:
            out_specs.append(pl.BlockSpec(i.block, rev(i.gimap)))
        out_shape.append(jax.ShapeDtypeStruct(i.gshape, gdtypes.get(k, F32)))
    res = pl.pallas_call(
        body, name=name, grid=grid, in_specs=in_specs, out_specs=out_specs, out_shape=out_shape,
        scratch_shapes=[pltpu.VMEM(tuple(c), F32) for c in carries],
        compiler_params=_stage_params(),
    )(*[i.arr for i in ins], *[a for c in cots for a in c], *hists, *[add_to[k] for k in add_idx])
    return list(res)


def _iota_rows(shape):
    return lax.broadcasted_iota(jnp.int32, shape, 0)


@functools.partial(jax.custom_vjp, nondiff_argnums=(1,))
def _roll_rows(x, s):
    return pltpu.roll(x, s % x.shape[0], 0)


def _roll_rows_fwd(x, s):
    return _roll_rows(x, s), None


def _roll_rows_bwd(s, _, g):
    return (_roll_rows(g, -s),)


_roll_rows.defvjp(_roll_rows_fwd, _roll_rows_bwd)


@jax.custom_vjp
def _drop_head(xx):
    return xx[SUBLANES:]


def _drop_head_fwd(xx):
    return xx[SUBLANES:], None


def _drop_head_bwd(_, g):
    return (jnp.concatenate([jnp.zeros((SUBLANES, g.shape[1]), g.dtype), g], axis=0),)


_drop_head.defvjp(_drop_head_fwd, _drop_head_bwd)


@jax.custom_vjp
def _last_rows(x):
    return x[x.shape[0] - SUBLANES:]


def _last_rows_fwd(x):
    return x[x.shape[0] - SUBLANES:], x.shape[0]


def _last_rows_bwd(n, g):
    return (jnp.concatenate([jnp.zeros((n - SUBLANES, g.shape[1]), g.dtype), g], axis=0),)


_last_rows.defvjp(_last_rows_fwd, _last_rows_bwd)


def _last_row(x):
    n = x.shape[0]
    return jnp.sum(jnp.where(_iota_rows(x.shape) == n - 1, x, 0.0), axis=0, keepdims=True)


def _scan_steps(n):
    s = 1
    while s < n:
        yield s
        s *= 2


def _block_scan_impl(a, u, h0):
    n = a.shape[0]
    row = _iota_rows(a.shape)
    for s in _scan_steps(n):
        keep = row >= s
        a_s = jnp.where(keep, pltpu.roll(a, s, 0), 1.0)
        u_s = jnp.where(keep, pltpu.roll(u, s, 0), 0.0)
        u = u + a * u_s
        a = a * a_s
    return u + a * h0


@jax.custom_vjp
def _block_scan(a, u, h0):
    return _block_scan_impl(a, u, h0)


def _block_scan_fwd(a, u, h0):
    h = _block_scan_impl(a, u, h0)
    return h, (a, h, h0)


def _block_scan_bwd(res, dh):
    a, h, h0 = res
    n = a.shape[0]
    row = _iota_rows(a.shape)
    b = jnp.where(row < n - 1, pltpu.roll(a, n - 1, 0), 0.0)
    lam = dh
    for s in _scan_steps(n):
        keep = row < n - s
        b_s = jnp.where(keep, pltpu.roll(b, n - s, 0), 1.0)
        l_s = jnp.where(keep, pltpu.roll(lam, n - s, 0), 0.0)
        lam = lam + b * l_s
        b = b * b_s
    h_prev = jnp.where(row >= 1, pltpu.roll(h, 1, 0), jnp.broadcast_to(h0, h.shape))
    d_h0 = jnp.sum(jnp.where(row == 0, a * lam, 0.0), axis=0, keepdims=True)
    return lam * h_prev, lam, d_h0


_block_scan.defvjp(_block_scan_fwd, _block_scan_bwd)


def _dot_hi(a, b, dims=(((1,), (0,)), ((), ()))):
    return lax.dot_general(a, b, dims, precision=HI, preferred_element_type=F32)


def _neumann_inverse(a):
    n = a.shape[0]
    eye = (lax.broadcasted_iota(jnp.int32, (n, n), 0) == lax.broadcasted_iota(jnp.int32, (n, n), 1)).astype(F32)
    x, p = eye - a, a
    for _ in range(int(math.log2(n)) - 1):
        p = _dot_hi(p, p)
        x = x + _dot_hi(x, p)
    return x


@jax.custom_vjp
def _unit_lower_inverse(a):
    return _neumann_inverse(a)


def _unit_lower_inverse_fwd(a):
    x = _neumann_inverse(a)
    return x, x


def _unit_lower_inverse_bwd(x, g):
    t = _dot_hi(x, g, (((0,), (0,)), ((), ())))
    return (-_dot_hi(t, x, (((1,), (1,)), ((), ()))),)


_unit_lower_inverse.defvjp(_unit_lower_inverse_fwd, _unit_lower_inverse_bwd)


def _softplus(x):
    return jnp.maximum(x, 0.0) + jnp.log1p(jnp.exp(-jnp.abs(x)))


def _neg_expm1(x):
    series = -x * (1.0 + x * (0.5 + x * (1.0 / 6.0 + x * (1.0 / 24.0 + x * (1.0 / 120.0)))))
    return jnp.where(x > -0.03, series, 1.0 - jnp.exp(x))


def _group_indicator(c, width):
    li = lax.broadcasted_iota(jnp.int32, (c, LANES), 0)
    gi = lax.broadcasted_iota(jnp.int32, (c, LANES), 1) * width
    return ((li >= gi) & (li < gi + width)).astype(F32)


def _spread(r, ind):
    return _dot_hi(r, ind, (((1,), (1,)), ((), ())))


def f_modulate(x, sc, sh):
    return (x * (1.0 + sc) + sh,), ()


def f_deepnorm(x, y, gt, g, b):
    v = DEEPNORM_ALPHA * x + (1.0 + gt) * y
    mu = jnp.mean(v, axis=-1, keepdims=True)
    vc = v - mu
    var = jnp.mean(vc * vc, axis=-1, keepdims=True)
    return (vc * lax.rsqrt(var + LN_EPS) * g + b,), ()


def _causal_conv(x, prev, ws):
    xx = jnp.concatenate([prev, x], axis=0)
    k = len(ws)
    y = ws[k - 1] * x
    for j in range(k - 1):
        y = y + ws[j] * _drop_head(_roll_rows(xx, k - 1 - j))
    return y


def f_rg_conv(x, w0, w1, w2, w3, b, prev):
    return (_causal_conv(x, prev, (w0, w1, w2, w3)) + b,), (_last_rows(x),)


def f_dn_conv(x, w0, w1, w2, w3, prev):
    return (jax.nn.silu(_causal_conv(x, prev, (w0, w1, w2, w3))),), (_last_rows(x),)


def f_ffn_act(gp, up, w0, w1, w2, b, prev):
    return (jax.nn.gelu(_causal_conv(gp, prev, (w0, w1, w2)) + b) * up,), (_last_rows(gp),)


def f_rglru(xc, pre_r, pre_i, gr, b_a, b_x, lam, h0):
    gate_r = jax.nn.sigmoid(pre_r + b_a)
    gate_i = jax.nn.sigmoid(pre_i + b_x)
    log_a = -RG_C * gate_r * _softplus(-lam)
    a = jnp.exp(log_a)
    mult = jnp.sqrt(_neg_expm1(2.0 * log_a))
    h = _block_scan(a, mult * gate_i * xc, h0)
    return (h * jax.nn.gelu(gr),), (_last_row(h),)


def f_dn_prep(q, k, a_in, b_in, a_log, dt_bias):
    ind = _group_indicator(q.shape[1], LANES)

    def l2n(t):
        return t * _spread(lax.rsqrt(_dot_hi(t * t, ind) + L2_EPS), ind)

    g = -jnp.exp(a_log) * _softplus(a_in + dt_bias)
    return (l2n(q) * (LANES ** -0.5), l2n(k), g, jax.nn.sigmoid(b_in)), ()


def f_dn_out(o, z, nw):
    ind = _group_indicator(o.shape[1], LANES)
    r = _spread(lax.rsqrt(_dot_hi(o * o, ind) * (1.0 / LANES) + RMS_EPS), ind)
    return (o * r * nw * jax.nn.silu(z),), ()


def f_merge(ga, gb, ya, yb):
    return (jax.nn.sigmoid(ga) * ya + jax.nn.sigmoid(gb) * yb,), ()


def _delta_chunk(q, k, v, g, beta, state):
    c = q.shape[0]
    ri = lax.broadcasted_iota(jnp.int32, (c, c), 0)
    ci = lax.broadcasted_iota(jnp.int32, (c, c), 1)
    incl, strict = ri >= ci, ri > ci
    gb = jnp.broadcast_to(g, (c, c))
    g_row = _dot_hi(incl.astype(F32), gb)
    g_col = _dot_hi(jnp.ones((c, c), F32), jnp.where(ri <= ci, gb, 0.0))
    decay = jnp.exp(jnp.where(incl, g_row - g_col, -jnp.inf))
    big_g = jnp.sum(jnp.where(ci == 0, g_row, 0.0), axis=1, keepdims=True)
    g_last = jnp.sum(g, axis=0, keepdims=True)
    exp_g = jnp.exp(big_g)
    kb = k * beta
    nt = (((1,), (1,)), ((), ()))
    a = jnp.where(strict, _dot_hi(kb, k, nt) * decay, 0.0)
    t_inv = _unit_lower_inverse(a)
    u = _dot_hi(t_inv, v * beta)
    w = _dot_hi(t_inv, kb * exp_g)
    qk = _dot_hi(q, k, nt) * decay
    v_new = u - _dot_hi(w, state)
    o = _dot_hi(q * exp_g, state) + _dot_hi(qk, v_new)
    k_dec = k * jnp.exp(g_last - big_g)
    new_state = jnp.exp(g_last) * state + _dot_hi(k_dec, v_new, (((0,), (0,)), ((), ())))
    return o, new_state


def delta_fwd(qn, kn, qkv, v_blk, g, beta, n_vh):
    t, qk_w = qn.shape
    vdim = n_vh * LANES
    rep = vdim // qk_w
    nc = t // DN_CHUNK

    def body(q_ref, k_ref, v_ref, g_ref, b_ref, o_ref, hist_ref, s_ref):
        @pl.when(pl.program_id(0) == 0)
        def _():
            s_ref[...] = jnp.zeros_like(s_ref)
        for h in range(n_vh):
            j = h // rep
            st = s_ref[h]
            hist_ref[h] = st
            o, ns = _delta_chunk(q_ref[:, j * LANES:(j + 1) * LANES], k_ref[:, j * LANES:(j + 1) * LANES],
                                 v_ref[:, h * LANES:(h + 1) * LANES], g_ref[:, h:h + 1], b_ref[:, h:h + 1], st)
            o_ref[:, h * LANES:(h + 1) * LANES] = o
            s_ref[h] = ns

    return pl.pallas_call(
        body, name="delta_fwd", grid=(nc,),
        in_specs=[pl.BlockSpec((DN_CHUNK, qk_w), lambda s: (s, 0)), pl.BlockSpec((DN_CHUNK, qk_w), lambda s: (s, 0)),
                  pl.BlockSpec((DN_CHUNK, vdim), lambda s: (s, v_blk)),
                  pl.BlockSpec((DN_CHUNK, LANES), lambda s: (s, 0)), pl.BlockSpec((DN_CHUNK, LANES), lambda s: (s, 0))],
        out_specs=[pl.BlockSpec((DN_CHUNK, vdim), lambda s: (s, 0)),
                   pl.BlockSpec((None, n_vh, LANES, LANES), lambda s: (s, 0, 0, 0))],
        out_shape=[jax.ShapeDtypeStruct((t, vdim), F32), jax.ShapeDtypeStruct((nc, n_vh, LANES, LANES), F32)],
        scratch_shapes=[pltpu.VMEM((n_vh, LANES, LANES), F32)],
        compiler_params=pltpu.CompilerParams(dimension_semantics=("arbitrary",), vmem_limit_bytes=VMEM_LIMIT),
    )(qn, kn, qkv, g, beta)


def delta_bwd(qn, kn, qkv, v_blk, g, beta, hist, d_o, n_vh):
    t, qk_w = qn.shape
    vdim = n_vh * LANES
    rep = vdim // qk_w
    nc = t // DN_CHUNK

    def body(q_ref, k_ref, v_ref, g_ref, b_ref, hist_ref, do_ref, dq_ref, dk_ref, dv_ref, dg_ref, db_ref, ds_ref):
        @pl.when(pl.program_id(0) == 0)
        def _():
            ds_ref[...] = jnp.zeros_like(ds_ref)
        lane = lax.broadcasted_iota(jnp.int32, (1, LANES), 1)
        dg_all = jnp.zeros((DN_CHUNK, LANES), F32)
        db_all = jnp.zeros((DN_CHUNK, LANES), F32)
        dq_acc, dk_acc = None, None
        for h in range(n_vh):
            j = h // rep
            prim = (q_ref[:, j * LANES:(j + 1) * LANES], k_ref[:, j * LANES:(j + 1) * LANES],
                    v_ref[:, h * LANES:(h + 1) * LANES], g_ref[:, h:h + 1], b_ref[:, h:h + 1], hist_ref[h])
            _, vjp = jax.vjp(_delta_chunk, *prim)
            dq, dk, dv, dg, db, dst = vjp((do_ref[:, h * LANES:(h + 1) * LANES], ds_ref[h]))
            ds_ref[h] = dst
            dv_ref[:, h * LANES:(h + 1) * LANES] = dv
            onehot = (lane == h).astype(F32)
            dg_all = dg_all + dg * onehot
            db_all = db_all + db * onehot
            dq_acc = dq if h % rep == 0 else dq_acc + dq
            dk_acc = dk if h % rep == 0 else dk_acc + dk
            if h % rep == rep - 1:
                dq_ref[:, j * LANES:(j + 1) * LANES] = dq_acc
                dk_ref[:, j * LANES:(j + 1) * LANES] = dk_acc
        dg_ref[...] = dg_all
        db_ref[...] = db_all

    r = lambda s: nc - 1 - s
    return pl.pallas_call(
        body, name="delta_bwd", grid=(nc,),
        in_specs=[pl.BlockSpec((DN_CHUNK, qk_w), lambda s: (r(s), 0)), pl.BlockSpec((DN_CHUNK, qk_w), lambda s: (r(s), 0)),
                  pl.BlockSpec((DN_CHUNK, vdim), lambda s: (r(s), v_blk)),
                  pl.BlockSpec((DN_CHUNK, LANES), lambda s: (r(s), 0)), pl.BlockSpec((DN_CHUNK, LANES), lambda s: (r(s), 0)),
                  pl.BlockSpec((None, n_vh, LANES, LANES), lambda s: (r(s), 0, 0, 0)),
                  pl.BlockSpec((DN_CHUNK, vdim), lambda s: (r(s), 0))],
        out_specs=[pl.BlockSpec((DN_CHUNK, qk_w), lambda s: (r(s), 0)), pl.BlockSpec((DN_CHUNK, qk_w), lambda s: (r(s), 0)),
                   pl.BlockSpec((DN_CHUNK, vdim), lambda s: (r(s), 0)),
                   pl.BlockSpec((DN_CHUNK, LANES), lambda s: (r(s), 0)), pl.BlockSpec((DN_CHUNK, LANES), lambda s: (r(s), 0))],
        out_shape=[jax.ShapeDtypeStruct((t, qk_w), F32), jax.ShapeDtypeStruct((t, qk_w), F32),
                   jax.ShapeDtypeStruct((t, vdim), F32),
                   jax.ShapeDtypeStruct((t, LANES), F32), jax.ShapeDtypeStruct((t, LANES), F32)],
        scratch_shapes=[pltpu.VMEM((n_vh, LANES, LANES), F32)],
        compiler_params=pltpu.CompilerParams(dimension_semantics=("arbitrary",), vmem_limit_bytes=VMEM_LIMIT),
    )(qn, kn, qkv, g, beta, hist, d_o)


def loss_head(y, target, tb):
    t, d = y.shape

    def body(y_ref, t_ref, dy_ref, loss_ref):
        @pl.when(pl.program_id(0) == 0)
        def _():
            loss_ref[...] = jnp.zeros_like(loss_ref)
        err = y_ref[...] - t_ref[...]
        dy_ref[...] = err * (1.0 / d)
        loss_ref[...] += 0.5 * jnp.sum(jnp.sum(err * err, axis=1, keepdims=True), axis=0, keepdims=True) * (1.0 / d)

    return pl.pallas_call(
        body, name="loss_head", grid=(t // tb,),
        in_specs=[pl.BlockSpec((tb, d), lambda s: (s, 0))] * 2,
        out_specs=[pl.BlockSpec((tb, d), lambda s: (s, 0)), pl.BlockSpec((1, 1), lambda s: (0, 0))],
        out_shape=[jax.ShapeDtypeStruct((t, d), F32), jax.ShapeDtypeStruct((1, 1), F32)],
        compiler_params=pltpu.CompilerParams(dimension_semantics=("arbitrary",), vmem_limit_bytes=VMEM_LIMIT),
    )(y, target)


def _exchange(name, arrs, scatter):
    n = len(arrs)

    def body(*refs):
        in_refs, out_refs = refs[:n], refs[n:2 * n]
        send_sems, recv_sems, local_sems = refs[2 * n:]
        x, y, c = lax.axis_index("x"), lax.axis_index("y"), lax.axis_index("c")
        me = 4 * x + 2 * y + c

        def peer(k):
            return (x ^ ((k >> 2) & 1), y ^ ((k >> 1) & 1), c ^ (k & 1))

        def lin(p):
            return 4 * p[0] + 2 * p[1] + p[2]

        local = []
        for i in range(n):
            src = in_refs[i].at[me] if scatter else in_refs[i]
            cp = pltpu.make_async_copy(src, out_refs[i].at[me], local_sems.at[i])
            cp.start()
            local.append(cp)
        sends = []
        for k in range(1, N_DEV):
            p = peer(k)
            for i in range(n):
                src = in_refs[i].at[lin(p)] if scatter else in_refs[i]
                cp = pltpu.make_async_remote_copy(src_ref=src, dst_ref=out_refs[i].at[me],
                                                  send_sem=send_sems.at[i, k - 1], recv_sem=recv_sems.at[i, k - 1],
                                                  device_id=p, device_id_type=MESH)
                cp.start()
                sends.append(cp)
        for k in range(1, N_DEV):
            p = peer(k)
            for i in range(n):
                src = in_refs[i].at[me] if scatter else in_refs[i]
                pltpu.make_async_remote_copy(src_ref=src, dst_ref=out_refs[i].at[lin(p)],
                                             send_sem=send_sems.at[i, k - 1], recv_sem=recv_sems.at[i, k - 1],
                                             device_id=p, device_id_type=MESH).wait_recv()
        for cp in sends:
            cp.wait_send()
        for cp in local:
            cp.wait()

    hbm = pl.BlockSpec(memory_space=pl.ANY)
    res = pl.pallas_call(
        body, name=name,
        in_specs=[hbm] * n, out_specs=[hbm] * n,
        out_shape=[jax.ShapeDtypeStruct(a.shape if scatter else (N_DEV,) + a.shape, a.dtype) for a in arrs],
        scratch_shapes=[pltpu.SemaphoreType.DMA((n, N_DEV - 1)), pltpu.SemaphoreType.DMA((n, N_DEV - 1)),
                        pltpu.SemaphoreType.DMA((n,))],
        compiler_params=pltpu.CompilerParams(has_side_effects=True),
    )(*arrs)
    return list(res)


def all_gather(name, arrs):
    return _exchange(name, arrs, scatter=False)


def all_to_all(name, arrs):
    return _exchange(name, arrs, scatter=True)


def _adamw_math(w, g, m, v):
    m = ADAM_B1 * m + (1.0 - ADAM_B1) * g
    v = ADAM_B2 * v + (1.0 - ADAM_B2) * (g * g)
    m_hat = m / (1.0 - ADAM_B1 ** ADAM_STEP)
    v_hat = v / (1.0 - ADAM_B2 ** ADAM_STEP)
    delta = -ADAM_LR * (m_hat / (jnp.sqrt(v_hat) + ADAM_EPS) + ADAM_WD * w)
    return delta, m, v


def adamw(name, w, parts, m, v, rows_cap=128):
    r, c = w.shape
    np_ = parts.shape[0]
    tr = _tile(r, rows_cap, SUBLANES * (4 // parts.dtype.itemsize))

    def body(w_ref, p_ref, m_ref, v_ref, g_ref, d_ref, nm_ref, nv_ref):
        g = p_ref[0].astype(F32)
        for k in range(1, np_):
            g = g + p_ref[k].astype(F32)
        delta, nm, nv = _adamw_math(w_ref[...], g, m_ref[...], v_ref[...])
        g_ref[...] = g
        d_ref[...] = delta
        nm_ref[...] = nm
        nv_ref[...] = nv

    spec = pl.BlockSpec((tr, c), lambda i: (i, 0))
    return pl.pallas_call(
        body, name=name, grid=(r // tr,),
        in_specs=[spec, pl.BlockSpec((np_, tr, c), lambda i: (0, i, 0)), spec, spec],
        out_specs=[spec] * 4, out_shape=[jax.ShapeDtypeStruct((r, c), F32)] * 4,
        compiler_params=pltpu.CompilerParams(dimension_semantics=("parallel",), vmem_limit_bytes=VMEM_LIMIT),
    )(w, parts, m, v)


def sum_parts(name, parts, rows_cap=256):
    np_, r, c = parts.shape
    tr = _tile(r, rows_cap, SUBLANES)

    def body(p_ref, o_ref):
        g = p_ref[0].astype(F32)
        for k in range(1, np_):
            g = g + p_ref[k].astype(F32)
        o_ref[...] = g

    return pl.pallas_call(
        body, name=name, grid=(r // tr,),
        in_specs=[pl.BlockSpec((np_, tr, c), lambda i: (0, i, 0))],
        out_specs=pl.BlockSpec((tr, c), lambda i: (i, 0)),
        out_shape=jax.ShapeDtypeStruct((r, c), F32),
        compiler_params=pltpu.CompilerParams(dimension_semantics=("parallel",), vmem_limit_bytes=VMEM_LIMIT),
    )(parts)


def _pack(arrs):
    flat = jnp.concatenate([a.reshape(-1).astype(F32) for a in arrs])
    n = flat.shape[0]
    return jnp.pad(flat, (0, _round_up(n, LANES * SUBLANES) - n)).reshape(-1, LANES)


def _unpack(packed, like):
    flat, out, pos = packed.reshape(-1), [], 0
    for a in like:
        out.append(flat[pos:pos + a.size].reshape(a.shape))
        pos += a.size
    return out


def kernel(x, c, w_ada, b_ada, w_in, rg_conv_w, rg_conv_b, rg_w_a, rg_b_a, rg_w_x, rg_b_x, rg_lambda, dn_conv_w, dn_a_log, dn_dt_bias, dn_norm_w, w_proj_a, w_proj_b, w_out, ln1_g, ln1_b, ffn_w_gate, ffn_w_up, ffn_conv_w, ffn_conv_b, ffn_w_down, ln2_g, ln2_b, loss_target, m_w_ada, m_b_ada, m_w_in, m_rg_conv_w, m_rg_conv_b, m_rg_w_a, m_rg_b_a, m_rg_w_x, m_rg_b_x, m_rg_lambda, m_dn_conv_w, m_dn_a_log, m_dn_dt_bias, m_dn_norm_w, m_w_proj_a, m_w_proj_b, m_w_out, m_ln1_g, m_ln1_b, m_ffn_w_gate, m_ffn_w_up, m_ffn_conv_w, m_ffn_conv_b, m_ffn_w_down, m_ln2_g, m_ln2_b, v_w_ada, v_b_ada, v_w_in, v_rg_conv_w, v_rg_conv_b, v_rg_w_a, v_rg_b_a, v_rg_w_x, v_rg_b_x, v_rg_lambda, v_dn_conv_w, v_dn_a_log, v_dn_dt_bias, v_dn_norm_w, v_w_proj_a, v_w_proj_b, v_w_out, v_ln1_g, v_ln1_b, v_ffn_w_gate, v_ffn_w_up, v_ffn_conv_w, v_ffn_conv_b, v_ffn_w_down, v_ln2_g, v_ln2_b):
    names = ['w_ada', 'b_ada', 'w_in', 'rg_conv_w', 'rg_conv_b', 'rg_w_a', 'rg_b_a', 'rg_w_x', 'rg_b_x', 'rg_lambda',
             'dn_conv_w', 'dn_a_log', 'dn_dt_bias', 'dn_norm_w', 'w_proj_a', 'w_proj_b', 'w_out', 'ln1_g', 'ln1_b',
             'ffn_w_gate', 'ffn_w_up', 'ffn_conv_w', 'ffn_conv_b', 'ffn_w_down', 'ln2_g', 'ln2_b']
    loc = locals()
    W = {n: loc[n][0] for n in names}
    M = {n: loc['m_' + n][0] for n in names}
    V = {n: loc['v_' + n][0] for n in names}

    me = 4 * lax.axis_index("x") + 2 * lax.axis_index("y") + lax.axis_index("c")
    xs, tgt = x[0], loss_target[0]
    t, d = xs.shape
    d_rnn = W['rg_conv_b'].shape[0]
    n_blk = W['rg_w_a'].shape[0]
    n_vh = W['dn_a_log'].shape[0]
    assert W['dn_norm_w'].shape[0] == LANES
    vdim = n_vh * LANES
    d_ff = W['ffn_conv_b'].shape[0]
    d_in = W['w_in'].shape[1] * N_DEV
    qk = (d_in - 2 * d_rnn - 2 * vdim - 2 * n_vh - 2 * d) // 2
    assert vdim == 2 * qk and qk % LANES == 0 and n_vh <= LANES
    splits = (d_rnn, d_rnn, qk, qk, vdim, vdim, n_vh, n_vh, d, d)
    offs = [0]
    for s_ in splits:
        offs.append(offs[-1] + s_)

    tb = _tile(t, 256, SUBLANES)

    big = ['w_in', 'w_proj_a', 'w_proj_b', 'w_out', 'ffn_w_gate', 'ffn_w_up', 'ffn_w_down']
    small_sh = ['rg_conv_w', 'dn_conv_w', 'ffn_conv_w']
    gathered = all_gather("gather_weights", [W[n].astype(WIRE_DTYPE) for n in big] + [W[n] for n in small_sh])
    g_in, g_pa, g_pb, g_out, g_fg, g_fu, g_fd, g_rcw, g_dcw, g_fcw = gathered
    cols = lambda g: jnp.transpose(g, (1, 0, 2)).reshape(g.shape[1], -1)
    rows = lambda g: g.reshape(-1, g.shape[2])
    w_in_f = cols(g_in)
    padl = lambda a: jnp.pad(a, ((0, 0), (0, LANES - a.shape[1])))
    groups = [w_in_f[:, offs[i]:offs[i + 1]] for i in range(10)]
    groups[6], groups[7] = padl(groups[6]), padl(groups[7])
    go = [0]
    for g_ in groups:
        go.append(go[-1] + g_.shape[1])
    n_pad = _round_up(go[-1], 512)
    wp = jnp.pad(jnp.concatenate(groups, axis=1), ((0, 0), (0, n_pad - go[-1])))
    o_xr, o_gr, o_q, o_k, o_v, o_z, o_a, o_b, o_ga, o_gb = go[:10]
    w_pa, w_pb, w_o, w_fd = rows(g_pa), rows(g_pb), rows(g_out), rows(g_fd)
    w_gu = jnp.concatenate([cols(g_fg), cols(g_fu)], axis=1)
    rcw, dcw, fcw = cols(g_rcw), cols(g_dcw), cols(g_fcw)
    eye_b = jnp.eye(n_blk, dtype=F32)
    bd = lambda w: (w[:, :, None, :] * eye_b[:, None, :, None]).reshape(d_rnn, d_rnn)
    w_bd = jnp.concatenate([bd(W['rg_w_a']), bd(W['rg_w_x'])], axis=1)
    row1 = lambda a: a.reshape(1, -1)
    padv = lambda a: jnp.pad(row1(a), ((0, 0), (0, LANES - a.shape[0])))
    nw_t = jnp.tile(row1(W['dn_norm_w']), (1, n_vh))

    (c_all,) = all_gather("gather_c", [c])
    c_pad = jnp.pad(c_all.reshape(N_DEV, d), ((0, LANES - N_DEV), (0, 0)))
    ada_w = W['w_ada'].shape[1]
    b_ada_me = lax.dynamic_slice(W['b_ada'], (me * ada_w,), (ada_w,)).reshape(1, ada_w)
    ada_sh = mm(c_pad, W['w_ada'], name="ada_fwd", a_act="silu", bias=b_ada_me)
    (ada_all,) = all_gather("gather_ada", [ada_sh[:N_DEV]])
    ada_me = lax.dynamic_slice(ada_all, (0, me, 0), (N_DEV, 1, ada_w)).reshape(6, 1, d)
    sh1, sc1, gt1, sh2, sc2, gt2 = [ada_me[i] for i in range(6)]

    nt = t // tb

    def act(a, bw, col0=0, width=None, grad=True):
        width = a.shape[1] if width is None else width
        assert col0 % bw == 0 and width % bw == 0
        c0 = col0 // bw
        return In(a, (tb, bw), lambda o, s: (s, c0 + o), grad=grad, gshape=(t, width), gimap=lambda o, s: (s, o))

    def prm(a, bw, parts=None):
        return In(a, (a.shape[0], bw), lambda o, s: (0, o), acc=True, parts=parts)

    def out(width, bw):
        return Out((t, width), (tb, bw), lambda o, s: (s, o))

    krows = lambda k_: [(slice(j, j + 1), slice(None)) for j in range(k_)]

    mod1_ins = [act(xs, d), prm(sc1, d), prm(sh1, d)]
    (h1,), _ = stage_fwd("mod1_fwd", f_modulate, (1, nt), mod1_ins, [out(d, d)])
    proj = mm(h1, wp, name="proj_fwd")

    cb_r = _tile(math.gcd(d_rnn, o_gr), 256)
    rgc_ins = [act(proj, cb_r, o_xr, d_rnn), prm(rcw, cb_r, krows(4)), prm(row1(W['rg_conv_b']), cb_r)]
    rgc_grid, rgc_car = (d_rnn // cb_r, nt), [(SUBLANES, cb_r)]
    (xc,), rgc_hist = stage_fwd("rg_conv_fwd", f_rg_conv, rgc_grid, rgc_ins, [out(d_rnn, cb_r)], rgc_car)
    gates = mm(xc, w_bd, name="rg_gates_fwd")
    lru_ins = [act(xc, cb_r), act(gates, cb_r, 0, d_rnn), act(gates, cb_r, d_rnn, d_rnn), act(proj, cb_r, o_gr, d_rnn),
               prm(row1(W['rg_b_a']), cb_r), prm(row1(W['rg_b_x']), cb_r), prm(row1(W['rg_lambda']), cb_r)]
    lru_car = [(1, cb_r)]
    (rec,), lru_hist = stage_fwd("rglru_fwd", f_rglru, rgc_grid, lru_ins, [out(d_rnn, cb_r)], lru_car)
    y_a = mm(rec, w_pa, name="proj_a_fwd")

    qkv_w = 2 * qk + vdim
    cb_q = _tile(math.gcd(qkv_w, o_q), 256)
    dnc_ins = [act(proj, cb_q, o_q, qkv_w), prm(dcw, cb_q, krows(4))]
    dnc_grid, dnc_car = (qkv_w // cb_q, nt), [(SUBLANES, cb_q)]
    (qkv_c,), dnc_hist = stage_fwd("dn_conv_fwd", f_dn_conv, dnc_grid, dnc_ins, [out(qkv_w, cb_q)], dnc_car)
    prep_ins = [act(qkv_c, qk, 0, qk), act(qkv_c, qk, qk, qk), act(proj, LANES, o_a, LANES), act(proj, LANES, o_b, LANES),
                prm(padv(W['dn_a_log']), LANES), prm(padv(W['dn_dt_bias']), LANES)]
    prep_outs = [out(qk, qk), out(qk, qk), out(LANES, LANES), out(LANES, LANES)]
    (qn, kn, g_dn, beta_dn), _ = stage_fwd("dn_prep_fwd", f_dn_prep, (1, nt), prep_ins, prep_outs)
    o_dn, dn_hist = delta_fwd(qn, kn, qkv_c, 1, g_dn, beta_dn, n_vh)
    cb_z = _tile(math.gcd(vdim, o_z), 512)
    dno_ins = [act(o_dn, cb_z), act(proj, cb_z, o_z, vdim), prm(nw_t, cb_z)]
    dno_grid = (vdim // cb_z, nt)
    (dn,), _ = stage_fwd("dn_out_fwd", f_dn_out, dno_grid, dno_ins, [out(vdim, cb_z)])
    y_b = mm(dn, w_pb, name="proj_b_fwd")

    cb_m = _tile(math.gcd(math.gcd(d, o_ga), o_gb), 512)
    mrg_ins = [act(proj, cb_m, o_ga, d), act(proj, cb_m, o_gb, d), act(y_a, cb_m), act(y_b, cb_m)]
    mrg_grid = (d // cb_m, nt)
    (merged,), _ = stage_fwd("merge_fwd", f_merge, mrg_grid, mrg_ins, [out(d, cb_m)])
    mix = mm(merged, w_o, name="w_out_fwd")
    ln1_ins = [act(xs, d), act(mix, d), prm(gt1, d), prm(row1(W['ln1_g']), d), prm(row1(W['ln1_b']), d)]
    (x1,), _ = stage_fwd("ln1_fwd", f_deepnorm, (1, nt), ln1_ins, [out(d, d)])

    mod2_ins = [act(x1, d), prm(sc2, d), prm(sh2, d)]
    (h2,), _ = stage_fwd("mod2_fwd", f_modulate, (1, nt), mod2_ins, [out(d, d)])
    gu = mm(h2, w_gu, name="ffn_in_fwd")
    cb_f = _tile(d_ff, 256)
    ffa_ins = [act(gu, cb_f, 0, d_ff), act(gu, cb_f, d_ff, d_ff), prm(fcw, cb_f, krows(3)), prm(row1(W['ffn_conv_b']), cb_f)]
    ffa_grid, ffa_car = (d_ff // cb_f, nt), [(SUBLANES, cb_f)]
    (act_ff,), ffa_hist = stage_fwd("ffn_act_fwd", f_ffn_act, ffa_grid, ffa_ins, [out(d_ff, cb_f)], ffa_car)
    ff = mm(act_ff, w_fd, name="ffn_down_fwd")
    ln2_ins = [act(x1, d), act(ff, d), prm(gt2, d), prm(row1(W['ln2_g']), d), prm(row1(W['ln2_b']), d)]
    (x2,), _ = stage_fwd("ln2_fwd", f_deepnorm, (1, nt), ln2_ins, [out(d, d)])
    dy, loss_loc = loss_head(x2, tgt, tb)

    dx1_a, d_ff_o, d_gt2, d_ln2g, d_ln2b = stage_bwd("ln2_bwd", f_deepnorm, (1, nt), ln2_ins, [out(d, d)], [dy])
    d_act = mm(d_ff_o, w_fd, name="ffn_down_bwd_x", tb=True)
    gw_fd = mm(act_ff, d_ff_o, name="ffn_down_bwd_w", ta=True)
    d_gp, d_up, d_fcw, d_fcb = stage_bwd("ffn_act_bwd", f_ffn_act, ffa_grid, ffa_ins, [out(d_ff, cb_f)], [d_act],
                                         ffa_car, ffa_hist, gdtypes={0: MXU_DTYPE, 1: MXU_DTYPE})
    d_gu = jnp.concatenate([d_gp, d_up], axis=1)
    d_h2 = mm(d_gu, w_gu, name="ffn_in_bwd_x", tb=True)
    gw_gu = mm(h2, d_gu, name="ffn_in_bwd_w", ta=True)
    d_x1, d_sc2, d_sh2 = stage_bwd("mod2_bwd", f_modulate, (1, nt), mod2_ins, [out(d, d)], [d_h2], add_to={0: dx1_a})
    dx_a, d_mix, d_gt1, d_ln1g, d_ln1b = stage_bwd("ln1_bwd", f_deepnorm, (1, nt), ln1_ins, [out(d, d)], [d_x1])
    d_merged = mm(d_mix, w_o, name="w_out_bwd_x", tb=True)
    gw_o = mm(merged, d_mix, name="w_out_bwd_w", ta=True)
    d_ga, d_gb, d_ya, d_yb = stage_bwd("merge_bwd", f_merge, mrg_grid, mrg_ins, [out(d, cb_m)], [d_merged],
                                       gdtypes={0: MXU_DTYPE, 1: MXU_DTYPE})
    d_rec = mm(d_ya, w_pa, name="proj_a_bwd_x", tb=True)
    gw_pa = mm(rec, d_ya, name="proj_a_bwd_w", ta=True)
    d_dn = mm(d_yb, w_pb, name="proj_b_bwd_x", tb=True)
    gw_pb = mm(dn, d_yb, name="proj_b_bwd_w", ta=True)

    d_o, d_z, d_nwt = stage_bwd("dn_out_bwd", f_dn_out, dno_grid, dno_ins, [out(vdim, cb_z)], [d_dn],
                                gdtypes={1: MXU_DTYPE})
    d_qn, d_kn, d_v, d_g, d_beta = delta_bwd(qn, kn, qkv_c, 1, g_dn, beta_dn, dn_hist, d_o, n_vh)
    d_qc, d_kc, d_a, d_b, d_alog, d_dtb = stage_bwd("dn_prep_bwd", f_dn_prep, (1, nt), prep_ins, prep_outs,
                                                    [d_qn, d_kn, d_g, d_beta], gdtypes={2: MXU_DTYPE, 3: MXU_DTYPE})
    d_qkv_c = jnp.concatenate([d_qc, d_kc, d_v], axis=1)
    d_qkv, d_dcw = stage_bwd("dn_conv_bwd", f_dn_conv, dnc_grid, dnc_ins, [out(qkv_w, cb_q)], [d_qkv_c],
                             dnc_car, dnc_hist, gdtypes={0: MXU_DTYPE})

    d_xc_a, d_pr, d_pi, d_gr, d_ba, d_bx, d_lam = stage_bwd(
        "rglru_bwd", f_rglru, rgc_grid, lru_ins, [out(d_rnn, cb_r)], [d_rec], lru_car, lru_hist,
        gdtypes={1: MXU_DTYPE, 2: MXU_DTYPE, 3: MXU_DTYPE})
    d_gates = jnp.concatenate([d_pr, d_pi], axis=1)
    d_xc_b = mm(d_gates, w_bd, name="rg_gates_bwd_x", tb=True)
    gw_bd = mm(xc, d_gates, name="rg_gates_bwd_w", ta=True)
    d_xr, d_rcw, d_rcb = stage_bwd("rg_conv_bwd", f_rg_conv, rgc_grid, rgc_ins, [out(d_rnn, cb_r)], [(d_xc_a, d_xc_b)],
                                   rgc_car, rgc_hist, gdtypes={0: MXU_DTYPE})

    d_proj = jnp.concatenate([d_xr, d_gr, d_qkv, d_z, d_a, d_b, d_ga, d_gb,
                              jnp.zeros((t, n_pad - go[-1]), MXU_DTYPE)], axis=1)
    d_h1 = mm(d_proj, wp, name="proj_bwd_x", tb=True)
    gw_p = mm(h1, d_proj, name="proj_bwd_w", ta=True)
    grad_x, d_sc1, d_sh1 = stage_bwd("mod1_bwd", f_modulate, (1, nt), mod1_ins, [out(d, d)], [d_h1], add_to={0: dx_a})

    gw_in = jnp.concatenate([gw_p[:, go[i]:go[i] + splits[i]] for i in range(10)], axis=1)
    col_blocks = lambda g: jnp.transpose(g.reshape(g.shape[0], N_DEV, -1), (1, 0, 2))
    row_blocks = lambda g: g.reshape(N_DEV, -1, g.shape[1])
    big_blocks = [col_blocks(gw_in), row_blocks(gw_pa), row_blocks(gw_pb), row_blocks(gw_o),
                  col_blocks(gw_gu[:, :d_ff]), col_blocks(gw_gu[:, d_ff:]), row_blocks(gw_fd)]
    big_parts = all_to_all("scatter_grads", [b_.astype(WIRE_DTYPE) for b_ in big_blocks])

    diag = lambda g: jnp.einsum('nimj,nm->nij', g.reshape(n_blk, d_rnn // n_blk, n_blk, d_rnn // n_blk), eye_b)
    d_ada_me = jnp.concatenate([d_sh1, d_sc1, d_gt1, d_sh2, d_sc2, d_gt2], axis=1)
    small_names = ['b_ada', 'rg_conv_w', 'rg_conv_b', 'rg_w_a', 'rg_b_a', 'rg_w_x', 'rg_b_x', 'rg_lambda', 'dn_conv_w',
                   'dn_a_log', 'dn_dt_bias', 'dn_norm_w', 'ln1_g', 'ln1_b', 'ffn_conv_w', 'ffn_conv_b', 'ln2_g', 'ln2_b']
    small_loc = {
        'b_ada': d_ada_me, 'rg_conv_w': d_rcw, 'rg_conv_b': d_rcb,
        'rg_w_a': diag(gw_bd[:, :d_rnn]), 'rg_b_a': d_ba, 'rg_w_x': diag(gw_bd[:, d_rnn:]), 'rg_b_x': d_bx,
        'rg_lambda': d_lam, 'dn_conv_w': d_dcw, 'dn_a_log': d_alog[:, :n_vh], 'dn_dt_bias': d_dtb[:, :n_vh],
        'dn_norm_w': jnp.sum(d_nwt.reshape(n_vh, LANES), axis=0), 'ln1_g': d_ln1g, 'ln1_b': d_ln1b,
        'ffn_conv_w': d_fcw, 'ffn_conv_b': d_fcb, 'ln2_g': d_ln2g, 'ln2_b': d_ln2b}
    small_list = [small_loc[n] for n in small_names]
    (small_all,) = all_gather("gather_small_grads", [_pack(small_list)])
    small_sum = _unpack(sum_parts("sum_small_grads", small_all), small_list)
    g_small = dict(zip(small_names, small_sum))
    d_ada_all = small_all.reshape(N_DEV, -1)[:, :6 * d]
    d_ada_cols = lax.dynamic_slice(d_ada_all, (0, me * ada_w), (N_DEV, ada_w))
    d_ada_pad = jnp.pad(d_ada_cols, ((0, LANES - N_DEV), (0, 0)))
    gw_ada = mm(c_pad, d_ada_pad, name="ada_bwd_w", ta=True, a_act="silu")

    res = {}
    big_parts = dict(zip(big, big_parts))
    big_parts['w_ada'] = gw_ada[None]
    for n in ['w_ada'] + big:
        res[n] = adamw("adamw_" + n, W[n], big_parts[n], M[n], V[n])
    for n in small_sh:
        w_ = W[n].shape[1]
        g_small[n] = lax.dynamic_slice(g_small[n], (0, me * w_), (W[n].shape[0], w_))
    for n in small_names:
        g_small[n] = g_small[n].reshape(W[n].shape)
    pk = lambda dct: _pack([dct[n] for n in small_names])
    s_g, s_d, s_m, s_v = adamw("adamw_small", pk(W), pk(g_small)[None], pk(M), pk(V))
    like = [W[n] for n in small_names]
    for n, g_, d_, m_, v_ in zip(small_names, _unpack(s_g, like), _unpack(s_d, like), _unpack(s_m, like), _unpack(s_v, like)):
        res[n] = (g_, d_, m_, v_)

    loss = lax.psum(loss_loc[0, 0], ("x", "y", "c"))
    outs = [loss, grad_x[None]]
    for j in range(4):
        outs += [res[n][j].reshape(loc[n].shape) for n in names]
    return tuple(outs)
```

```python
import functools
import math

import jax
import jax.numpy as jnp
from jax import lax
from jax.experimental import pallas as pl
from jax.experimental.pallas import tpu as pltpu

F32 = jnp.float32
BF16 = jnp.bfloat16
MXU_DTYPE = BF16
WIRE_DTYPE = BF16
DN_DTYPE = BF16
HI = lax.Precision.HIGHEST
MESH = pl.DeviceIdType.MESH

N_DEV = 8
LANES = 128
SUBLANES = 8
VMEM_LIMIT = 56 * 1024 * 1024

RG_C = 8.0
DN_CHUNK = 64
LN_EPS = 1e-5
RMS_EPS = 1e-6
L2_EPS = 1e-6
DEPTH = 1
DEEPNORM_ALPHA = (2 * DEPTH) ** 0.25
ADAM_LR = 0.001
ADAM_B1 = 0.9
ADAM_B2 = 0.999
ADAM_EPS = 1e-08
ADAM_WD = 0.01
ADAM_STEP = 10


def _tile(n, cap, unit=LANES):
    best = None
    for t in range(unit, min(n, cap) + 1, unit):
        if n % t == 0:
            best = t
    return best if best is not None else n


def _round_up(n, m):
    return (n + m - 1) // m * m


def mm(a, b, *, name, ta=False, tb=False, a_act=None, bias=None, out_dtype=F32,
       tm_cap=1024, tn_cap=512, tk_cap=1024):
    m, k = (a.shape[1], a.shape[0]) if ta else a.shape
    n = b.shape[0] if tb else b.shape[1]
    assert k == (b.shape[1] if tb else b.shape[0]), (a.shape, b.shape, ta, tb)
    tm, tn, tk = _tile(m, tm_cap), _tile(n, tn_cap), _tile(k, tk_cap)
    nk = k // tk
    dims = (((0 if ta else 1,), (1 if tb else 0,)), ((), ()))

    def body(*refs):
        if bias is None:
            a_ref, b_ref, o_ref, acc_ref = refs
        else:
            a_ref, b_ref, bias_ref, o_ref, acc_ref = refs
        kk = pl.program_id(2)

        @pl.when(kk == 0)
        def _():
            acc_ref[...] = jnp.zeros_like(acc_ref)

        av = a_ref[...]
        if a_act == "silu":
            av = jax.nn.silu(av.astype(F32))
        acc_ref[...] += lax.dot_general(av.astype(MXU_DTYPE), b_ref[...].astype(MXU_DTYPE), dims,
                                        preferred_element_type=F32)

        @pl.when(kk == nk - 1)
        def _():
            r = acc_ref[...]
            if bias is not None:
                r = r + bias_ref[...]
            o_ref[...] = r.astype(o_ref.dtype)

    a_spec = pl.BlockSpec((tk, tm), lambda i, j, q: (q, i)) if ta else pl.BlockSpec((tm, tk), lambda i, j, q: (i, q))
    b_spec = pl.BlockSpec((tn, tk), lambda i, j, q: (j, q)) if tb else pl.BlockSpec((tk, tn), lambda i, j, q: (q, j))
    in_specs, args = [a_spec, b_spec], [a, b]
    if bias is not None:
        in_specs.append(pl.BlockSpec((1, tn), lambda i, j, q: (0, j)))
        args.append(bias)
    return pl.pallas_call(
        body, name=name, grid=(m // tm, n // tn, nk),
        in_specs=in_specs, out_specs=pl.BlockSpec((tm, tn), lambda i, j, q: (i, j)),
        out_shape=jax.ShapeDtypeStruct((m, n), out_dtype),
        scratch_shapes=[pltpu.VMEM((tm, tn), F32)],
        compiler_params=pltpu.CompilerParams(dimension_semantics=("parallel", "parallel", "arbitrary"),
                                             vmem_limit_bytes=VMEM_LIMIT),
    )(*args)


class In:
    def __init__(self, arr, block, imap, acc=False, grad=True, parts=None, gshape=None, gimap=None):
        self.arr, self.block, self.imap, self.acc, self.grad, self.parts = arr, block, imap, acc, grad, parts
        self.gshape = arr.shape if gshape is None else gshape
        self.gimap = imap if gimap is None else gimap


class Out:
    def __init__(self, shape, block, imap, dtype=F32):
        self.shape, self.block, self.imap, self.dtype = shape, block, imap, dtype


def _load(in_refs, ins):
    vals = []
    for r, i in zip(in_refs, ins):
        if i.parts is None:
            vals.append(r[...])
        else:
            vals.extend(r[p] for p in i.parts)
    return vals


def _stage_params():
    return pltpu.CompilerParams(dimension_semantics=("parallel", "arbitrary"), vmem_limit_bytes=VMEM_LIMIT)


def stage_fwd(name, f, grid, ins, outs, carries=()):
    n_in, n_out, n_c = len(ins), len(outs), len(carries)

    def body(*refs):
        in_refs, out_refs = refs[:n_in], refs[n_in:n_in + n_out]
        hist_refs, c_refs = refs[n_in + n_out:n_in + n_out + n_c], refs[n_in + n_out + n_c:]
        if n_c:
            @pl.when(pl.program_id(1) == 0)
            def _():
                for c in c_refs:
                    c[...] = jnp.zeros_like(c)
        cin = [c[...] for c in c_refs]
        for h, c in zip(hist_refs, cin):
            h[...] = c
        o, cout = f(*_load(in_refs, ins), *cin)
        for r, v in zip(out_refs, o):
            r[...] = v.astype(r.dtype)
        for c, v in zip(c_refs, cout):
            c[...] = v

    hist_spec = lambda c: pl.BlockSpec((None, None) + tuple(c), lambda o, s: (o, s) + (0,) * len(c))
    res = pl.pallas_call(
        body, name=name, grid=grid,
        in_specs=[pl.BlockSpec(i.block, i.imap) for i in ins],
        out_specs=[pl.BlockSpec(o.block, o.imap) for o in outs] + [hist_spec(c) for c in carries],
        out_shape=[jax.ShapeDtypeStruct(o.shape, o.dtype) for o in outs]
        + [jax.ShapeDtypeStruct(tuple(grid) + tuple(c), F32) for c in carries],
        scratch_shapes=[pltpu.VMEM(tuple(c), F32) for c in carries],
        compiler_params=_stage_params(),
    )(*[i.arr for i in ins])
    return list(res[:n_out]), list(res[n_out:])


def stage_bwd(name, f, grid, ins, outs, cots, carries=(), hists=(), add_to=None, gdtypes=None):
    n_in, n_out, n_c = len(ins), len(outs), len(carries)
    ns = grid[1]
    add_to = add_to or {}
    gdtypes = gdtypes or {}
    add_idx = sorted(add_to)
    g_idx = [k for k, i in enumerate(ins) if i.grad]
    cots = [c if isinstance(c, (tuple, list)) else (c,) for c in cots]
    n_cot = [len(c) for c in cots]
    rev = lambda imap: (lambda o, s: imap(o, ns - 1 - s))

    def body(*refs):
        p = 0
        in_refs = refs[p:p + n_in]; p += n_in
        cot_refs = []
        for cnt in n_cot:
            cot_refs.append(refs[p:p + cnt]); p += cnt
        hist_refs = refs[p:p + n_c]; p += n_c
        add_refs = refs[p:p + len(add_idx)]; p += len(add_idx)
        g_refs = refs[p:p + len(g_idx)]; p += len(g_idx)
        dc_refs = refs[p:]
        first = pl.program_id(1) == 0
        if n_c:
            @pl.when(first)
            def _():
                for c in dc_refs:
                    c[...] = jnp.zeros_like(c)
        vals = _load(in_refs, ins)
        cin = [h[...] for h in hist_refs]
        (o, cout), vjp = jax.vjp(lambda *a: f(*a), *vals, *cin)
        cot_o = []
        for crs, v in zip(cot_refs, o):
            c = crs[0][...].astype(v.dtype)
            for extra in crs[1:]:
                c = c + extra[...].astype(v.dtype)
            cot_o.append(c)
        cot_c = tuple(c[...] for c in dc_refs)
        grads = vjp((tuple(cot_o), cot_c))
        pos, per_in = 0, []
        for i in ins:
            cnt = 1 if i.parts is None else len(i.parts)
            per_in.append(grads[pos:pos + cnt])
            pos += cnt
        dcin = grads[pos:]
        for gr, k in zip(g_refs, g_idx):
            i, gs = ins[k], per_in[k]
            if i.acc:
                @pl.when(first)
                def _(gr=gr):
                    gr[...] = jnp.zeros_like(gr)
                if i.parts is None:
                    gr[...] += gs[0].astype(gr.dtype)
                else:
                    for pt, g in zip(i.parts, gs):
                        gr[pt] += g.astype(gr.dtype)
            else:
                g = gs[0]
                if k in add_to:
                    g = g + add_refs[add_idx.index(k)][...].astype(g.dtype)
                gr[...] = g.astype(gr.dtype)
        for c, v in zip(dc_refs, dcin):
            c[...] = v

    in_specs = [pl.BlockSpec(i.block, rev(i.imap)) for i in ins]
    for o_, cnt in zip(outs, n_cot):
        in_specs += [pl.BlockSpec(o_.block, rev(o_.imap))] * cnt
    in_specs += [pl.BlockSpec((None, None) + tuple(c), (lambda c: (lambda o, s: (o, ns - 1 - s) + (0,) * len(c)))(c))
                 for c in carries]
    in_specs += [pl.BlockSpec(ins[k].block, rev(ins[k].gimap)) for k in add_idx]
    out_specs, out_shape = [], []
    for k in g_idx:
        i = ins[k]
        if i.acc:
            out_specs.append(pl.BlockSpec(i.block, (lambda im: (lambda o, s: im(o, 0)))(i.imap)))
        else:
            out_specs.append(pl.BlockSpec(i.block, rev(i.gimap)))
        out_shape.append(jax.ShapeDtypeStruct(i.gshape, gdtypes.get(k, F32)))
    res = pl.pallas_call(
        body, name=name, grid=grid, in_specs=in_specs, out_specs=out_specs, out_shape=out_shape,
        scratch_shapes=[pltpu.VMEM(tuple(c), F32) for c in carries],
        compiler_params=_stage_params(),
    )(*[i.arr for i in ins], *[a for c in cots for a in c], *hists, *[add_to[k] for k in add_idx])
    return list(res)


def _iota_rows(shape):
    return lax.broadcasted_iota(jnp.int32, shape, 0)


@functools.partial(jax.custom_vjp, nondiff_argnums=(1,))
def _roll_rows(x, s):
    return pltpu.roll(x, s % x.shape[0], 0)


def _roll_rows_fwd(x, s):
    return _roll_rows(x, s), None


def _roll_rows_bwd(s, _, g):
    return (_roll_rows(g, -s),)


_roll_rows.defvjp(_roll_rows_fwd, _roll_rows_bwd)


@jax.custom_vjp
def _drop_head(xx):
    return xx[SUBLANES:]


def _drop_head_fwd(xx):
    return xx[SUBLANES:], None


def _drop_head_bwd(_, g):
    return (jnp.concatenate([jnp.zeros((SUBLANES, g.shape[1]), g.dtype), g], axis=0),)


_drop_head.defvjp(_drop_head_fwd, _drop_head_bwd)


@jax.custom_vjp
def _last_rows(x):
    return x[x.shape[0] - SUBLANES:]


def _last_rows_fwd(x):
    return x[x.shape[0] - SUBLANES:], x.shape[0]


def _last_rows_bwd(n, g):
    return (jnp.concatenate([jnp.zeros((n - SUBLANES, g.shape[1]), g.dtype), g], axis=0),)


_last_rows.defvjp(_last_rows_fwd, _last_rows_bwd)


def _last_row(x):
    n = x.shape[0]
    return jnp.sum(jnp.where(_iota_rows(x.shape) == n - 1, x, 0.0), axis=0, keepdims=True)


def _scan_steps(n):
    s = 1
    while s < n:
        yield s
        s *= 2


def _block_scan_impl(a, u, h0):
    n = a.shape[0]
    row = _iota_rows(a.shape)
    for s in _scan_steps(n):
        keep = row >= s
        a_s = jnp.where(keep, pltpu.roll(a, s, 0), 1.0)
        u_s = jnp.where(keep, pltpu.roll(u, s, 0), 0.0)
        u = u + a * u_s
        a = a * a_s
    return u + a * h0


@jax.custom_vjp
def _block_scan(a, u, h0):
    return _block_scan_impl(a, u, h0)


def _block_scan_fwd(a, u, h0):
    h = _block_scan_impl(a, u, h0)
    return h, (a, h, h0)


def _block_scan_bwd(res, dh):
    a, h, h0 = res
    n = a.shape[0]
    row = _iota_rows(a.shape)
    b = jnp.where(row < n - 1, pltpu.roll(a, n - 1, 0), 0.0)
    lam = dh
    for s in _scan_steps(n):
        keep = row < n - s
        b_s = jnp.where(keep, pltpu.roll(b, n - s, 0), 1.0)
        l_s = jnp.where(keep, pltpu.roll(lam, n - s, 0), 0.0)
        lam = lam + b * l_s
        b = b * b_s
    h_prev = jnp.where(row >= 1, pltpu.roll(h, 1, 0), jnp.broadcast_to(h0, h.shape))
    d_h0 = jnp.sum(jnp.where(row == 0, a * lam, 0.0), axis=0, keepdims=True)
    return lam * h_prev, lam, d_h0


_block_scan.defvjp(_block_scan_fwd, _block_scan_bwd)


def _dot_hi(a, b, dims=(((1,), (0,)), ((), ()))):
    return lax.dot_general(a, b, dims, precision=HI, preferred_element_type=F32)


_NN = (((1,), (0,)), ((), ()))
_NT = (((1,), (1,)), ((), ()))
_TN = (((0,), (0,)), ((), ()))


def _raw_dot(a, b, dims):
    return lax.dot_general(a.astype(DN_DTYPE), b.astype(DN_DTYPE), dims, preferred_element_type=F32)


@jax.custom_vjp
def _nn(a, b):
    return _raw_dot(a, b, _NN)


_nn.defvjp(lambda a, b: (_raw_dot(a, b, _NN), (a, b)),
           lambda r, g: (_raw_dot(g, r[1], _NT), _raw_dot(r[0], g, _TN)))


@jax.custom_vjp
def _nt(a, b):
    return _raw_dot(a, b, _NT)


_nt.defvjp(lambda a, b: (_raw_dot(a, b, _NT), (a, b)),
           lambda r, g: (_raw_dot(g, r[1], _NN), _raw_dot(g, r[0], _TN)))


@jax.custom_vjp
def _tn(a, b):
    return _raw_dot(a, b, _TN)


_tn.defvjp(lambda a, b: (_raw_dot(a, b, _TN), (a, b)),
           lambda r, g: (_raw_dot(r[1], g, _NT), _raw_dot(r[0], g, _NN)))


def _neumann_inverse(a):
    n = a.shape[0]
    eye = (lax.broadcasted_iota(jnp.int32, (n, n), 0) == lax.broadcasted_iota(jnp.int32, (n, n), 1)).astype(F32)
    p = _raw_dot(a, a, _NN)
    e = p
    for _ in range(int(math.log2(n)) - 2):
        p = _raw_dot(p, p, _NN)
        e = e + p + _raw_dot(e, p, _NN)
    return eye - a + e - _raw_dot(a, e, _NN)


@jax.custom_vjp
def _unit_lower_inverse(a):
    return _neumann_inverse(a)


def _unit_lower_inverse_fwd(a):
    x = _neumann_inverse(a)
    return x, x


def _unit_lower_inverse_bwd(x, g):
    return (-_raw_dot(_raw_dot(x, g, _TN), x, _NT),)


_unit_lower_inverse.defvjp(_unit_lower_inverse_fwd, _unit_lower_inverse_bwd)


def _softplus(x):
    return jnp.maximum(x, 0.0) + jnp.log1p(jnp.exp(-jnp.abs(x)))


def _neg_expm1(x):
    series = -x * (1.0 + x * (0.5 + x * (1.0 / 6.0 + x * (1.0 / 24.0 + x * (1.0 / 120.0)))))
    return jnp.where(x > -0.03, series, 1.0 - jnp.exp(x))


def _group_indicator(c, width):
    li = lax.broadcasted_iota(jnp.int32, (c, LANES), 0)
    gi = lax.broadcasted_iota(jnp.int32, (c, LANES), 1) * width
    return ((li >= gi) & (li < gi + width)).astype(F32)


def _spread(r, ind):
    return _dot_hi(r, ind, (((1,), (1,)), ((), ())))


def f_modulate(x, sc, sh):
    return (x * (1.0 + sc) + sh,), ()


def f_deepnorm(x, y, gt, g, b):
    v = DEEPNORM_ALPHA * x + (1.0 + gt) * y
    mu = jnp.mean(v, axis=-1, keepdims=True)
    vc = v - mu
    var = jnp.mean(vc * vc, axis=-1, keepdims=True)
    return (vc * lax.rsqrt(var + LN_EPS) * g + b,), ()


def _causal_conv(x, prev, ws):
    xx = jnp.concatenate([prev, x], axis=0)
    k = len(ws)
    y = ws[k - 1] * x
    for j in range(k - 1):
        y = y + ws[j] * _drop_head(_roll_rows(xx, k - 1 - j))
    return y


def f_rg_conv(x, w0, w1, w2, w3, b, prev):
    return (_causal_conv(x, prev, (w0, w1, w2, w3)) + b,), (_last_rows(x),)


def f_dn_conv(x, w0, w1, w2, w3, prev):
    return (jax.nn.silu(_causal_conv(x, prev, (w0, w1, w2, w3))),), (_last_rows(x),)


def f_ffn_act(gp, up, w0, w1, w2, b, prev):
    return (jax.nn.gelu(_causal_conv(gp, prev, (w0, w1, w2)) + b) * up,), (_last_rows(gp),)


def f_rglru(xc, pre_r, pre_i, gr, b_a, b_x, lam, h0):
    gate_r = jax.nn.sigmoid(pre_r + b_a)
    gate_i = jax.nn.sigmoid(pre_i + b_x)
    log_a = -RG_C * gate_r * _softplus(-lam)
    a = jnp.exp(log_a)
    mult = jnp.sqrt(_neg_expm1(2.0 * log_a))
    h = _block_scan(a, mult * gate_i * xc, h0)
    return (h * jax.nn.gelu(gr),), (_last_row(h),)


def f_dn_prep(q, k, a_in, b_in, a_log, dt_bias):
    ind = _group_indicator(q.shape[1], LANES)

    def l2n(t):
        return t * _spread(lax.rsqrt(_dot_hi(t * t, ind) + L2_EPS), ind)

    g = -jnp.exp(a_log) * _softplus(a_in + dt_bias)
    n = g.shape[0]
    shift = int(math.log2(DN_CHUNK))
    ri = lax.broadcasted_iota(jnp.int32, (n, n), 0)
    ci = lax.broadcasted_iota(jnp.int32, (n, n), 1)
    tri = ((lax.shift_right_logical(ri, shift) == lax.shift_right_logical(ci, shift)) & (ri >= ci)).astype(F32)
    return (l2n(q) * (LANES ** -0.5), l2n(k), _dot_hi(tri, g), jax.nn.sigmoid(b_in)), ()


def f_dn_out(o, z, nw):
    ind = _group_indicator(o.shape[1], LANES)
    r = _spread(lax.rsqrt(_dot_hi(o * o, ind) * (1.0 / LANES) + RMS_EPS), ind)
    return (o * r * nw * jax.nn.silu(z),), ()


def f_merge(ga, gb, ya, yb):
    return (jax.nn.sigmoid(ga) * ya + jax.nn.sigmoid(gb) * yb,), ()


def _delta_intra(q, k, v, g_i, g_j, beta):
    c = q.shape[0]
    ri = lax.broadcasted_iota(jnp.int32, (c, c), 0)
    ci = lax.broadcasted_iota(jnp.int32, (c, c), 1)
    decay = jnp.exp(jnp.where(ri >= ci, g_i - g_j, -jnp.inf))
    g_last = jnp.sum(jnp.where(_iota_rows((c, 1)) == c - 1, g_i, 0.0), axis=0, keepdims=True)
    exp_g = jnp.exp(g_i)
    kb = k * beta
    t_inv = _unit_lower_inverse(jnp.where(ri > ci, _nt(kb, k) * decay, 0.0))
    u = _nn(t_inv, v * beta)
    w = _nn(t_inv, kb * exp_g)
    return u, w, _nt(q, k) * decay, q * exp_g, k * jnp.exp(g_last - g_i)


def _delta_inter(u, w, qk, q_dec, k_dec, g_last, state):
    v_new = u - _nn(w, state)
    o = _nn(q_dec, state) + _nn(qk, v_new)
    return o, jnp.exp(g_last) * state + _tn(k_dec, v_new)


def _chunk_spec(width, nc=None, col=0):
    if nc is None:
        return pl.BlockSpec((DN_CHUNK, width), lambda s: (s, col))
    return pl.BlockSpec((DN_CHUNK, width), lambda s: (nc - 1 - s, col))


def _delta_params(sem):
    return pltpu.CompilerParams(dimension_semantics=(sem,), vmem_limit_bytes=VMEM_LIMIT)


def _head(ref, h, width=LANES):
    return ref[:, h * LANES:h * LANES + width]


def delta_intra_fwd(qn, kn, qkv, v_blk, big_g, big_gt, beta, n_vh):
    t, qk_w = qn.shape
    vdim = n_vh * LANES
    rep = vdim // qk_w
    nc = t // DN_CHUNK

    def body(q_ref, k_ref, v_ref, g_ref, gt_ref, b_ref, u_ref, w_ref, qk_ref, qd_ref, kd_ref):
        for h in range(n_vh):
            j = h // rep
            u, w, qk, qd, kd = _delta_intra(_head(q_ref, j), _head(k_ref, j), _head(v_ref, h),
                                            g_ref[:, h:h + 1], gt_ref[h:h + 1, :], b_ref[:, h:h + 1])
            sl = slice(h * LANES, (h + 1) * LANES)
            u_ref[:, sl] = u
            w_ref[:, sl] = w.astype(w_ref.dtype)
            qk_ref[:, sl] = jnp.concatenate([qk, jnp.zeros_like(qk)], axis=1).astype(qk_ref.dtype)
            qd_ref[:, sl] = qd.astype(qd_ref.dtype)
            kd_ref[:, sl] = kd.astype(kd_ref.dtype)

    return pl.pallas_call(
        body, name="delta_intra_fwd", grid=(nc,),
        in_specs=[_chunk_spec(qk_w), _chunk_spec(qk_w), _chunk_spec(vdim, col=v_blk), _chunk_spec(LANES),
                  pl.BlockSpec((None, n_vh, DN_CHUNK), lambda s: (s, 0, 0)), _chunk_spec(LANES)],
        out_specs=[_chunk_spec(vdim)] * 5,
        out_shape=[jax.ShapeDtypeStruct((t, vdim), F32)] + [jax.ShapeDtypeStruct((t, vdim), DN_DTYPE)] * 4,
        compiler_params=_delta_params("parallel"),
    )(qn, kn, qkv, big_g, big_gt, beta)


def delta_inter_fwd(u, w, qk, q_dec, k_dec, big_g, n_vh):
    t, vdim = u.shape
    nc = t // DN_CHUNK

    def body(u_ref, w_ref, qk_ref, qd_ref, kd_ref, g_ref, o_ref, hist_ref, s_ref):
        @pl.when(pl.program_id(0) == 0)
        def _():
            s_ref[...] = jnp.zeros_like(s_ref)
        for h in range(n_vh):
            st = s_ref[h]
            hist_ref[h] = st
            o, ns = _delta_inter(_head(u_ref, h), _head(w_ref, h), _head(qk_ref, h, DN_CHUNK), _head(qd_ref, h),
                                 _head(kd_ref, h), g_ref[DN_CHUNK - 1:DN_CHUNK, h:h + 1], st)
            o_ref[:, h * LANES:(h + 1) * LANES] = o
            s_ref[h] = ns

    return pl.pallas_call(
        body, name="delta_inter_fwd", grid=(nc,),
        in_specs=[_chunk_spec(vdim)] * 5 + [_chunk_spec(LANES)],
        out_specs=[_chunk_spec(vdim), pl.BlockSpec((None, n_vh, LANES, LANES), lambda s: (s, 0, 0, 0))],
        out_shape=[jax.ShapeDtypeStruct((t, vdim), F32), jax.ShapeDtypeStruct((nc, n_vh, LANES, LANES), F32)],
        scratch_shapes=[pltpu.VMEM((n_vh, LANES, LANES), F32)],
        compiler_params=_delta_params("arbitrary"),
    )(u, w, qk, q_dec, k_dec, big_g)


def delta_inter_bwd(u, w, qk, q_dec, k_dec, big_g, hist, d_o, n_vh):
    t, vdim = u.shape
    nc = t // DN_CHUNK

    def body(u_ref, w_ref, qk_ref, qd_ref, kd_ref, g_ref, hist_ref, do_ref,
             du_ref, dw_ref, dqk_ref, dqd_ref, dkd_ref, dg_ref, ds_ref):
        @pl.when(pl.program_id(0) == 0)
        def _():
            ds_ref[...] = jnp.zeros_like(ds_ref)
        lane = lax.broadcasted_iota(jnp.int32, (1, LANES), 1)
        dgl_all = jnp.zeros((1, LANES), F32)
        for h in range(n_vh):
            prim = (_head(u_ref, h), _head(w_ref, h).astype(F32), _head(qk_ref, h, DN_CHUNK).astype(F32),
                    _head(qd_ref, h).astype(F32), _head(kd_ref, h).astype(F32),
                    g_ref[DN_CHUNK - 1:DN_CHUNK, h:h + 1], hist_ref[h])
            _, vjp = jax.vjp(_delta_inter, *prim)
            du, dw, dqk, dqd, dkd, dgl, dst = vjp((_head(do_ref, h), ds_ref[h]))
            ds_ref[h] = dst
            sl = slice(h * LANES, (h + 1) * LANES)
            du_ref[:, sl] = du
            dw_ref[:, sl] = dw
            dqk_ref[:, sl] = jnp.concatenate([dqk, jnp.zeros_like(dqk)], axis=1)
            dqd_ref[:, sl] = dqd
            dkd_ref[:, sl] = dkd
            dgl_all = dgl_all + dgl * (lane == h).astype(F32)
        last = _iota_rows((DN_CHUNK, LANES)) == DN_CHUNK - 1
        dg_ref[...] = jnp.where(last, jnp.broadcast_to(dgl_all, (DN_CHUNK, LANES)), 0.0)

    rv = lambda w_: _chunk_spec(w_, nc)
    return pl.pallas_call(
        body, name="delta_inter_bwd", grid=(nc,),
        in_specs=[rv(vdim)] * 5 + [rv(LANES), pl.BlockSpec((None, n_vh, LANES, LANES), lambda s: (nc - 1 - s, 0, 0, 0)),
                                   rv(vdim)],
        out_specs=[rv(vdim)] * 5 + [rv(LANES)],
        out_shape=[jax.ShapeDtypeStruct((t, vdim), F32)] * 5 + [jax.ShapeDtypeStruct((t, LANES), F32)],
        scratch_shapes=[pltpu.VMEM((n_vh, LANES, LANES), F32)],
        compiler_params=_delta_params("arbitrary"),
    )(u, w, qk, q_dec, k_dec, big_g, hist, d_o)


def delta_intra_bwd(qn, kn, qkv, v_blk, big_g, big_gt, beta, cots, n_vh):
    t, qk_w = qn.shape
    vdim = n_vh * LANES
    rep = vdim // qk_w
    nc = t // DN_CHUNK

    def body(q_ref, k_ref, v_ref, g_ref, gt_ref, b_ref, du_ref, dw_ref, dqk_ref, dqd_ref, dkd_ref,
             dq_ref, dk_ref, dv_ref, dg_ref, dgt_ref, db_ref):
        lane = lax.broadcasted_iota(jnp.int32, (1, LANES), 1)
        dg_all = jnp.zeros((DN_CHUNK, LANES), F32)
        db_all = jnp.zeros((DN_CHUNK, LANES), F32)
        dq_acc, dk_acc = None, None
        for h in range(n_vh):
            j = h // rep
            prim = (_head(q_ref, j), _head(k_ref, j), _head(v_ref, h),
                    g_ref[:, h:h + 1], gt_ref[h:h + 1, :], b_ref[:, h:h + 1])
            _, vjp = jax.vjp(_delta_intra, *prim)
            dq, dk, dv, dgi, dgj, db = vjp((_head(du_ref, h), _head(dw_ref, h), _head(dqk_ref, h, DN_CHUNK),
                                            _head(dqd_ref, h), _head(dkd_ref, h)))
            dv_ref[:, h * LANES:(h + 1) * LANES] = dv
            dgt_ref[h:h + 1, :] = dgj
            onehot = (lane == h).astype(F32)
            dg_all = dg_all + dgi * onehot
            db_all = db_all + db * onehot
            dq_acc = dq if h % rep == 0 else dq_acc + dq
            dk_acc = dk if h % rep == 0 else dk_acc + dk
            if h % rep == rep - 1:
                dq_ref[:, j * LANES:(j + 1) * LANES] = dq_acc
                dk_ref[:, j * LANES:(j + 1) * LANES] = dk_acc
        dg_ref[...] = dg_all
        db_ref[...] = db_all

    gt_spec = pl.BlockSpec((None, n_vh, DN_CHUNK), lambda s: (s, 0, 0))
    return pl.pallas_call(
        body, name="delta_intra_bwd", grid=(nc,),
        in_specs=[_chunk_spec(qk_w), _chunk_spec(qk_w), _chunk_spec(vdim, col=v_blk), _chunk_spec(LANES), gt_spec,
                  _chunk_spec(LANES)] + [_chunk_spec(vdim)] * 5,
        out_specs=[_chunk_spec(qk_w), _chunk_spec(qk_w), _chunk_spec(vdim), _chunk_spec(LANES), gt_spec,
                   _chunk_spec(LANES)],
        out_shape=[jax.ShapeDtypeStruct((t, qk_w), F32), jax.ShapeDtypeStruct((t, qk_w), F32),
                   jax.ShapeDtypeStruct((t, vdim), F32), jax.ShapeDtypeStruct((t, LANES), F32),
                   jax.ShapeDtypeStruct((nc, n_vh, DN_CHUNK), F32), jax.ShapeDtypeStruct((t, LANES), F32)],
        compiler_params=_delta_params("parallel"),
    )(qn, kn, qkv, big_g, big_gt, beta, *cots)


def loss_head(y, target, tb):
    t, d = y.shape

    def body(y_ref, t_ref, dy_ref, loss_ref):
        @pl.when(pl.program_id(0) == 0)
        def _():
            loss_ref[...] = jnp.zeros_like(loss_ref)
        err = y_ref[...] - t_ref[...]
        dy_ref[...] = err * (1.0 / d)
        loss_ref[...] += 0.5 * jnp.sum(jnp.sum(err * err, axis=1, keepdims=True), axis=0, keepdims=True) * (1.0 / d)

    return pl.pallas_call(
        body, name="loss_head", grid=(t // tb,),
        in_specs=[pl.BlockSpec((tb, d), lambda s: (s, 0))] * 2,
        out_specs=[pl.BlockSpec((tb, d), lambda s: (s, 0)), pl.BlockSpec((1, 1), lambda s: (0, 0))],
        out_shape=[jax.ShapeDtypeStruct((t, d), F32), jax.ShapeDtypeStruct((1, 1), F32)],
        compiler_params=pltpu.CompilerParams(dimension_semantics=("arbitrary",), vmem_limit_bytes=VMEM_LIMIT),
    )(y, target)


def _exchange(name, arrs, scatter):
    n = len(arrs)

    def body(*refs):
        in_refs, out_refs = refs[:n], refs[n:2 * n]
        send_sems, recv_sems, local_sems = refs[2 * n:]
        x, y, c = lax.axis_index("x"), lax.axis_index("y"), lax.axis_index("c")
        me = 4 * x + 2 * y + c

        def peer(k):
            return (x ^ ((k >> 2) & 1), y ^ ((k >> 1) & 1), c ^ (k & 1))

        def lin(p):
            return 4 * p[0] + 2 * p[1] + p[2]

        local = []
        for i in range(n):
            src = in_refs[i].at[me] if scatter else in_refs[i]
            cp = pltpu.make_async_copy(src, out_refs[i].at[me], local_sems.at[i])
            cp.start()
            local.append(cp)
        sends = []
        for k in range(1, N_DEV):
            p = peer(k)
            for i in range(n):
                src = in_refs[i].at[lin(p)] if scatter else in_refs[i]
                cp = pltpu.make_async_remote_copy(src_ref=src, dst_ref=out_refs[i].at[me],
                                                  send_sem=send_sems.at[i, k - 1], recv_sem=recv_sems.at[i, k - 1],
                                                  device_id=p, device_id_type=MESH)
                cp.start()
                sends.append(cp)
        for k in range(1, N_DEV):
            p = peer(k)
            for i in range(n):
                src = in_refs[i].at[me] if scatter else in_refs[i]
                pltpu.make_async_remote_copy(src_ref=src, dst_ref=out_refs[i].at[lin(p)],
                                             send_sem=send_sems.at[i, k - 1], recv_sem=recv_sems.at[i, k - 1],
                                             device_id=p, device_id_type=MESH).wait_recv()
        for cp in sends:
            cp.wait_send()
        for cp in local:
            cp.wait()

    hbm = pl.BlockSpec(memory_space=pl.ANY)
    res = pl.pallas_call(
        body, name=name,
        in_specs=[hbm] * n, out_specs=[hbm] * n,
        out_shape=[jax.ShapeDtypeStruct(a.shape if scatter else (N_DEV,) + a.shape, a.dtype) for a in arrs],
        scratch_shapes=[pltpu.SemaphoreType.DMA((n, N_DEV - 1)), pltpu.SemaphoreType.DMA((n, N_DEV - 1)),
                        pltpu.SemaphoreType.DMA((n,))],
        compiler_params=pltpu.CompilerParams(has_side_effects=True),
    )(*arrs)
    return list(res)


def all_gather(name, arrs):
    return _exchange(name, arrs, scatter=False)


def all_to_all(name, arrs):
    return _exchange(name, arrs, scatter=True)


def _adamw_math(w, g, m, v):
    m = ADAM_B1 * m + (1.0 - ADAM_B1) * g
    v = ADAM_B2 * v + (1.0 - ADAM_B2) * (g * g)
    m_hat = m / (1.0 - ADAM_B1 ** ADAM_STEP)
    v_hat = v / (1.0 - ADAM_B2 ** ADAM_STEP)
    delta = -ADAM_LR * (m_hat / (jnp.sqrt(v_hat) + ADAM_EPS) + ADAM_WD * w)
    return delta, m, v


def adamw(name, w, parts, m, v, rows_cap=128):
    r, c = w.shape
    np_ = parts.shape[0]
    tr = _tile(r, rows_cap, SUBLANES * (4 // parts.dtype.itemsize))

    def body(w_ref, p_ref, m_ref, v_ref, g_ref, d_ref, nm_ref, nv_ref):
        g = p_ref[0].astype(F32)
        for k in range(1, np_):
            g = g + p_ref[k].astype(F32)
        delta, nm, nv = _adamw_math(w_ref[...], g, m_ref[...], v_ref[...])
        g_ref[...] = g
        d_ref[...] = delta
        nm_ref[...] = nm
        nv_ref[...] = nv

    spec = pl.BlockSpec((tr, c), lambda i: (i, 0))
    return pl.pallas_call(
        body, name=name, grid=(r // tr,),
        in_specs=[spec, pl.BlockSpec((np_, tr, c), lambda i: (0, i, 0)), spec, spec],
        out_specs=[spec] * 4, out_shape=[jax.ShapeDtypeStruct((r, c), F32)] * 4,
        compiler_params=pltpu.CompilerParams(dimension_semantics=("parallel",), vmem_limit_bytes=VMEM_LIMIT),
    )(w, parts, m, v)


def sum_parts(name, parts, rows_cap=256):
    np_, r, c = parts.shape
    tr = _tile(r, rows_cap, SUBLANES)

    def body(p_ref, o_ref):
        g = p_ref[0].astype(F32)
        for k in range(1, np_):
            g = g + p_ref[k].astype(F32)
        o_ref[...] = g

    return pl.pallas_call(
        body, name=name, grid=(r // tr,),
        in_specs=[pl.BlockSpec((np_, tr, c), lambda i: (0, i, 0))],
        out_specs=pl.BlockSpec((tr, c), lambda i: (i, 0)),
        out_shape=jax.ShapeDtypeStruct((r, c), F32),
        compiler_params=pltpu.CompilerParams(dimension_semantics=("parallel",), vmem_limit_bytes=VMEM_LIMIT),
    )(parts)


def _pack(arrs):
    flat = jnp.concatenate([a.reshape(-1).astype(F32) for a in arrs])
    n = flat.shape[0]
    return jnp.pad(flat, (0, _round_up(n, LANES * SUBLANES) - n)).reshape(-1, LANES)


def _unpack(packed, like):
    flat, out, pos = packed.reshape(-1), [], 0
    for a in like:
        out.append(flat[pos:pos + a.size].reshape(a.shape))
        pos += a.size
    return out


def kernel(x, c, w_ada, b_ada, w_in, rg_conv_w, rg_conv_b, rg_w_a, rg_b_a, rg_w_x, rg_b_x, rg_lambda, dn_conv_w, dn_a_log, dn_dt_bias, dn_norm_w, w_proj_a, w_proj_b, w_out, ln1_g, ln1_b, ffn_w_gate, ffn_w_up, ffn_conv_w, ffn_conv_b, ffn_w_down, ln2_g, ln2_b, loss_target, m_w_ada, m_b_ada, m_w_in, m_rg_conv_w, m_rg_conv_b, m_rg_w_a, m_rg_b_a, m_rg_w_x, m_rg_b_x, m_rg_lambda, m_dn_conv_w, m_dn_a_log, m_dn_dt_bias, m_dn_norm_w, m_w_proj_a, m_w_proj_b, m_w_out, m_ln1_g, m_ln1_b, m_ffn_w_gate, m_ffn_w_up, m_ffn_conv_w, m_ffn_conv_b, m_ffn_w_down, m_ln2_g, m_ln2_b, v_w_ada, v_b_ada, v_w_in, v_rg_conv_w, v_rg_conv_b, v_rg_w_a, v_rg_b_a, v_rg_w_x, v_rg_b_x, v_rg_lambda, v_dn_conv_w, v_dn_a_log, v_dn_dt_bias, v_dn_norm_w, v_w_proj_a, v_w_proj_b, v_w_out, v_ln1_g, v_ln1_b, v_ffn_w_gate, v_ffn_w_up, v_ffn_conv_w, v_ffn_conv_b, v_ffn_w_down, v_ln2_g, v_ln2_b):
    names = ['w_ada', 'b_ada', 'w_in', 'rg_conv_w', 'rg_conv_b', 'rg_w_a', 'rg_b_a', 'rg_w_x', 'rg_b_x', 'rg_lambda',
             'dn_conv_w', 'dn_a_log', 'dn_dt_bias', 'dn_norm_w', 'w_proj_a', 'w_proj_b', 'w_out', 'ln1_g', 'ln1_b',
             'ffn_w_gate', 'ffn_w_up', 'ffn_conv_w', 'ffn_conv_b', 'ffn_w_down', 'ln2_g', 'ln2_b']
    loc = locals()
    W = {n: loc[n][0] for n in names}
    M = {n: loc['m_' + n][0] for n in names}
    V = {n: loc['v_' + n][0] for n in names}

    me = 4 * lax.axis_index("x") + 2 * lax.axis_index("y") + lax.axis_index("c")
    xs, tgt = x[0], loss_target[0]
    t, d = xs.shape
    d_rnn = W['rg_conv_b'].shape[0]
    n_blk = W['rg_w_a'].shape[0]
    n_vh = W['dn_a_log'].shape[0]
    assert W['dn_norm_w'].shape[0] == LANES
    vdim = n_vh * LANES
    d_ff = W['ffn_conv_b'].shape[0]
    d_in = W['w_in'].shape[1] * N_DEV
    qk = (d_in - 2 * d_rnn - 2 * vdim - 2 * n_vh - 2 * d) // 2
    assert vdim == 2 * qk and qk % LANES == 0 and n_vh <= LANES
    splits = (d_rnn, d_rnn, qk, qk, vdim, vdim, n_vh, n_vh, d, d)
    offs = [0]
    for s_ in splits:
        offs.append(offs[-1] + s_)

    tb = _tile(t, 256, SUBLANES)

    big = ['w_in', 'w_proj_a', 'w_proj_b', 'w_out', 'ffn_w_gate', 'ffn_w_up', 'ffn_w_down']
    small_sh = ['rg_conv_w', 'dn_conv_w', 'ffn_conv_w']
    gathered = all_gather("gather_weights", [W[n].astype(WIRE_DTYPE) for n in big] + [W[n] for n in small_sh])
    g_in, g_pa, g_pb, g_out, g_fg, g_fu, g_fd, g_rcw, g_dcw, g_fcw = gathered
    cols = lambda g: jnp.transpose(g, (1, 0, 2)).reshape(g.shape[1], -1)
    rows = lambda g: g.reshape(-1, g.shape[2])
    w_in_f = cols(g_in)
    padl = lambda a: jnp.pad(a, ((0, 0), (0, LANES - a.shape[1])))
    groups = [w_in_f[:, offs[i]:offs[i + 1]] for i in range(10)]
    groups[6], groups[7] = padl(groups[6]), padl(groups[7])
    go = [0]
    for g_ in groups:
        go.append(go[-1] + g_.shape[1])
    n_pad = _round_up(go[-1], 512)
    wp = jnp.pad(jnp.concatenate(groups, axis=1), ((0, 0), (0, n_pad - go[-1])))
    o_xr, o_gr, o_q, o_k, o_v, o_z, o_a, o_b, o_ga, o_gb = go[:10]
    w_pa, w_pb, w_o, w_fd = rows(g_pa), rows(g_pb), rows(g_out), rows(g_fd)
    w_gu = jnp.concatenate([cols(g_fg), cols(g_fu)], axis=1)
    rcw, dcw, fcw = cols(g_rcw), cols(g_dcw), cols(g_fcw)
    eye_b = jnp.eye(n_blk, dtype=F32)
    bd = lambda w: (w[:, :, None, :] * eye_b[:, None, :, None]).reshape(d_rnn, d_rnn)
    w_bd = jnp.concatenate([bd(W['rg_w_a']), bd(W['rg_w_x'])], axis=1)
    row1 = lambda a: a.reshape(1, -1)
    padv = lambda a: jnp.pad(row1(a), ((0, 0), (0, LANES - a.shape[0])))
    nw_t = jnp.tile(row1(W['dn_norm_w']), (1, n_vh))

    (c_all,) = all_gather("gather_c", [c])
    c_pad = jnp.pad(c_all.reshape(N_DEV, d), ((0, LANES - N_DEV), (0, 0)))
    ada_w = W['w_ada'].shape[1]
    b_ada_me = lax.dynamic_slice(W['b_ada'], (me * ada_w,), (ada_w,)).reshape(1, ada_w)
    ada_sh = mm(c_pad, W['w_ada'], name="ada_fwd", a_act="silu", bias=b_ada_me)
    (ada_all,) = all_gather("gather_ada", [ada_sh[:N_DEV]])
    ada_me = lax.dynamic_slice(ada_all, (0, me, 0), (N_DEV, 1, ada_w)).reshape(6, 1, d)
    sh1, sc1, gt1, sh2, sc2, gt2 = [ada_me[i] for i in range(6)]

    nt = t // tb

    def act(a, bw, col0=0, width=None, grad=True):
        width = a.shape[1] if width is None else width
        assert col0 % bw == 0 and width % bw == 0
        c0 = col0 // bw
        return In(a, (tb, bw), lambda o, s: (s, c0 + o), grad=grad, gshape=(t, width), gimap=lambda o, s: (s, o))

    def prm(a, bw, parts=None):
        return In(a, (a.shape[0], bw), lambda o, s: (0, o), acc=True, parts=parts)

    def out(width, bw):
        return Out((t, width), (tb, bw), lambda o, s: (s, o))

    krows = lambda k_: [(slice(j, j + 1), slice(None)) for j in range(k_)]

    mod1_ins = [act(xs, d), prm(sc1, d), prm(sh1, d)]
    (h1,), _ = stage_fwd("mod1_fwd", f_modulate, (1, nt), mod1_ins, [out(d, d)])
    proj = mm(h1, wp, name="proj_fwd")

    cb_r = _tile(math.gcd(d_rnn, o_gr), 256)
    rgc_ins = [act(proj, cb_r, o_xr, d_rnn), prm(rcw, cb_r, krows(4)), prm(row1(W['rg_conv_b']), cb_r)]
    rgc_grid, rgc_car = (d_rnn // cb_r, nt), [(SUBLANES, cb_r)]
    (xc,), rgc_hist = stage_fwd("rg_conv_fwd", f_rg_conv, rgc_grid, rgc_ins, [out(d_rnn, cb_r)], rgc_car)
    gates = mm(xc, w_bd, name="rg_gates_fwd")
    lru_ins = [act(xc, cb_r), act(gates, cb_r, 0, d_rnn), act(gates, cb_r, d_rnn, d_rnn), act(proj, cb_r, o_gr, d_rnn),
               prm(row1(W['rg_b_a']), cb_r), prm(row1(W['rg_b_x']), cb_r), prm(row1(W['rg_lambda']), cb_r)]
    lru_car = [(1, cb_r)]
    (rec,), lru_hist = stage_fwd("rglru_fwd", f_rglru, rgc_grid, lru_ins, [out(d_rnn, cb_r)], lru_car)
    y_a = mm(rec, w_pa, name="proj_a_fwd")

    qkv_w = 2 * qk + vdim
    cb_q = _tile(math.gcd(qkv_w, o_q), 256)
    dnc_ins = [act(proj, cb_q, o_q, qkv_w), prm(dcw, cb_q, krows(4))]
    dnc_grid, dnc_car = (qkv_w // cb_q, nt), [(SUBLANES, cb_q)]
    (qkv_c,), dnc_hist = stage_fwd("dn_conv_fwd", f_dn_conv, dnc_grid, dnc_ins, [out(qkv_w, cb_q)], dnc_car)
    prep_ins = [act(qkv_c, qk, 0, qk), act(qkv_c, qk, qk, qk), act(proj, LANES, o_a, LANES), act(proj, LANES, o_b, LANES),
                prm(padv(W['dn_a_log']), LANES), prm(padv(W['dn_dt_bias']), LANES)]
    prep_outs = [out(qk, qk), out(qk, qk), out(LANES, LANES), out(LANES, LANES)]
    (qn, kn, g_dn, beta_dn), _ = stage_fwd("dn_prep_fwd", f_dn_prep, (1, nt), prep_ins, prep_outs)
    n_ch = t // DN_CHUNK
    gt_dn = jnp.transpose(g_dn.reshape(n_ch, DN_CHUNK, LANES)[:, :, :n_vh], (0, 2, 1))
    dn_mid = delta_intra_fwd(qn, kn, qkv_c, 1, g_dn, gt_dn, beta_dn, n_vh)
    o_dn, dn_hist = delta_inter_fwd(*dn_mid, g_dn, n_vh)
    cb_z = _tile(math.gcd(vdim, o_z), 512)
    dno_ins = [act(o_dn, cb_z), act(proj, cb_z, o_z, vdim), prm(nw_t, cb_z)]
    dno_grid = (vdim // cb_z, nt)
    (dn,), _ = stage_fwd("dn_out_fwd", f_dn_out, dno_grid, dno_ins, [out(vdim, cb_z)])
    y_b = mm(dn, w_pb, name="proj_b_fwd")

    cb_m = _tile(math.gcd(math.gcd(d, o_ga), o_gb), 512)
    mrg_ins = [act(proj, cb_m, o_ga, d), act(proj, cb_m, o_gb, d), act(y_a, cb_m), act(y_b, cb_m)]
    mrg_grid = (d // cb_m, nt)
    (merged,), _ = stage_fwd("merge_fwd", f_merge, mrg_grid, mrg_ins, [out(d, cb_m)])
    mix = mm(merged, w_o, name="w_out_fwd")
    ln1_ins = [act(xs, d), act(mix, d), prm(gt1, d), prm(row1(W['ln1_g']), d), prm(row1(W['ln1_b']), d)]
    (x1,), _ = stage_fwd("ln1_fwd", f_deepnorm, (1, nt), ln1_ins, [out(d, d)])

    mod2_ins = [act(x1, d), prm(sc2, d), prm(sh2, d)]
    (h2,), _ = stage_fwd("mod2_fwd", f_modulate, (1, nt), mod2_ins, [out(d, d)])
    gu = mm(h2, w_gu, name="ffn_in_fwd")
    cb_f = _tile(d_ff, 256)
    ffa_ins = [act(gu, cb_f, 0, d_ff), act(gu, cb_f, d_ff, d_ff), prm(fcw, cb_f, krows(3)), prm(row1(W['ffn_conv_b']), cb_f)]
    ffa_grid, ffa_car = (d_ff // cb_f, nt), [(SUBLANES, cb_f)]
    (act_ff,), ffa_hist = stage_fwd("ffn_act_fwd", f_ffn_act, ffa_grid, ffa_ins, [out(d_ff, cb_f)], ffa_car)
    ff = mm(act_ff, w_fd, name="ffn_down_fwd")
    ln2_ins = [act(x1, d), act(ff, d), prm(gt2, d), prm(row1(W['ln2_g']), d), prm(row1(W['ln2_b']), d)]
    (x2,), _ = stage_fwd("ln2_fwd", f_deepnorm, (1, nt), ln2_ins, [out(d, d)])
    dy, loss_loc = loss_head(x2, tgt, tb)

    dx1_a, d_ff_o, d_gt2, d_ln2g, d_ln2b = stage_bwd("ln2_bwd", f_deepnorm, (1, nt), ln2_ins, [out(d, d)], [dy])
    d_act = mm(d_ff_o, w_fd, name="ffn_down_bwd_x", tb=True)
    gw_fd = mm(act_ff, d_ff_o, name="ffn_down_bwd_w", ta=True)
    d_gp, d_up, d_fcw, d_fcb = stage_bwd("ffn_act_bwd", f_ffn_act, ffa_grid, ffa_ins, [out(d_ff, cb_f)], [d_act],
                                         ffa_car, ffa_hist, gdtypes={0: MXU_DTYPE, 1: MXU_DTYPE})
    d_gu = jnp.concatenate([d_gp, d_up], axis=1)
    d_h2 = mm(d_gu, w_gu, name="ffn_in_bwd_x", tb=True)
    gw_gu = mm(h2, d_gu, name="ffn_in_bwd_w", ta=True)
    d_x1, d_sc2, d_sh2 = stage_bwd("mod2_bwd", f_modulate, (1, nt), mod2_ins, [out(d, d)], [d_h2], add_to={0: dx1_a})
    dx_a, d_mix, d_gt1, d_ln1g, d_ln1b = stage_bwd("ln1_bwd", f_deepnorm, (1, nt), ln1_ins, [out(d, d)], [d_x1])
    d_merged = mm(d_mix, w_o, name="w_out_bwd_x", tb=True)
    gw_o = mm(merged, d_mix, name="w_out_bwd_w", ta=True)
    d_ga, d_gb, d_ya, d_yb = stage_bwd("merge_bwd", f_merge, mrg_grid, mrg_ins, [out(d, cb_m)], [d_merged],
                                       gdtypes={0: MXU_DTYPE, 1: MXU_DTYPE})
    d_rec = mm(d_ya, w_pa, name="proj_a_bwd_x", tb=True)
    gw_pa = mm(rec, d_ya, name="proj_a_bwd_w", ta=True)
    d_dn = mm(d_yb, w_pb, name="proj_b_bwd_x", tb=True)
    gw_pb = mm(dn, d_yb, name="proj_b_bwd_w", ta=True)

    d_o, d_z, d_nwt = stage_bwd("dn_out_bwd", f_dn_out, dno_grid, dno_ins, [out(vdim, cb_z)], [d_dn],
                                gdtypes={1: MXU_DTYPE})
    *d_mid, d_g_state = delta_inter_bwd(*dn_mid, g_dn, dn_hist, d_o, n_vh)
    d_qn, d_kn, d_v, d_g_col, d_gt, d_beta = delta_intra_bwd(qn, kn, qkv_c, 1, g_dn, gt_dn, beta_dn, d_mid, n_vh)
    d_g_row = jnp.pad(jnp.transpose(d_gt, (0, 2, 1)).reshape(t, n_vh), ((0, 0), (0, LANES - n_vh)))
    d_qc, d_kc, d_a, d_b, d_alog, d_dtb = stage_bwd("dn_prep_bwd", f_dn_prep, (1, nt), prep_ins, prep_outs,
                                                    [d_qn, d_kn, (d_g_state, d_g_col, d_g_row), d_beta],
                                                    gdtypes={2: MXU_DTYPE, 3: MXU_DTYPE})
    d_qkv_c = jnp.concatenate([d_qc, d_kc, d_v], axis=1)
    d_qkv, d_dcw = stage_bwd("dn_conv_bwd", f_dn_conv, dnc_grid, dnc_ins, [out(qkv_w, cb_q)], [d_qkv_c],
                             dnc_car, dnc_hist, gdtypes={0: MXU_DTYPE})

    d_xc_a, d_pr, d_pi, d_gr, d_ba, d_bx, d_lam = stage_bwd(
        "rglru_bwd", f_rglru, rgc_grid, lru_ins, [out(d_rnn, cb_r)], [d_rec], lru_car, lru_hist,
        gdtypes={1: MXU_DTYPE, 2: MXU_DTYPE, 3: MXU_DTYPE})
    d_gates = jnp.concatenate([d_pr, d_pi], axis=1)
    d_xc_b = mm(d_gates, w_bd, name="rg_gates_bwd_x", tb=True)
    gw_bd = mm(xc, d_gates, name="rg_gates_bwd_w", ta=True)
    d_xr, d_rcw, d_rcb = stage_bwd("rg_conv_bwd", f_rg_conv, rgc_grid, rgc_ins, [out(d_rnn, cb_r)], [(d_xc_a, d_xc_b)],
                                   rgc_car, rgc_hist, gdtypes={0: MXU_DTYPE})

    d_proj = jnp.concatenate([d_xr, d_gr, d_qkv, d_z, d_a, d_b, d_ga, d_gb,
                              jnp.zeros((t, n_pad - go[-1]), MXU_DTYPE)], axis=1)
    d_h1 = mm(d_proj, wp, name="proj_bwd_x", tb=True)
    gw_p = mm(h1, d_proj, name="proj_bwd_w", ta=True)
    grad_x, d_sc1, d_sh1 = stage_bwd("mod1_bwd", f_modulate, (1, nt), mod1_ins, [out(d, d)], [d_h1], add_to={0: dx_a})

    gw_in = jnp.concatenate([gw_p[:, go[i]:go[i] + splits[i]] for i in range(10)], axis=1)
    col_blocks = lambda g: jnp.transpose(g.reshape(g.shape[0], N_DEV, -1), (1, 0, 2))
    row_blocks = lambda g: g.reshape(N_DEV, -1, g.shape[1])
    big_blocks = [col_blocks(gw_in), row_blocks(gw_pa), row_blocks(gw_pb), row_blocks(gw_o),
                  col_blocks(gw_gu[:, :d_ff]), col_blocks(gw_gu[:, d_ff:]), row_blocks(gw_fd)]
    big_parts = all_to_all("scatter_grads", [b_.astype(WIRE_DTYPE) for b_ in big_blocks])

    diag = lambda g: jnp.einsum('nimj,nm->nij', g.reshape(n_blk, d_rnn // n_blk, n_blk, d_rnn // n_blk), eye_b)
    d_ada_me = jnp.concatenate([d_sh1, d_sc1, d_gt1, d_sh2, d_sc2, d_gt2], axis=1)
    small_names = ['b_ada', 'rg_conv_w', 'rg_conv_b', 'rg_w_a', 'rg_b_a', 'rg_w_x', 'rg_b_x', 'rg_lambda', 'dn_conv_w',
                   'dn_a_log', 'dn_dt_bias', 'dn_norm_w', 'ln1_g', 'ln1_b', 'ffn_conv_w', 'ffn_conv_b', 'ln2_g', 'ln2_b']
    small_loc = {
        'b_ada': d_ada_me, 'rg_conv_w': d_rcw, 'rg_conv_b': d_rcb,
        'rg_w_a': diag(gw_bd[:, :d_rnn]), 'rg_b_a': d_ba, 'rg_w_x': diag(gw_bd[:, d_rnn:]), 'rg_b_x': d_bx,
        'rg_lambda': d_lam, 'dn_conv_w': d_dcw, 'dn_a_log': d_alog[:, :n_vh], 'dn_dt_bias': d_dtb[:, :n_vh],
        'dn_norm_w': jnp.sum(d_nwt.reshape(n_vh, LANES), axis=0), 'ln1_g': d_ln1g, 'ln1_b': d_ln1b,
        'ffn_conv_w': d_fcw, 'ffn_conv_b': d_fcb, 'ln2_g': d_ln2g, 'ln2_b': d_ln2b}
    small_list = [small_loc[n] for n in small_names]
    (small_all,) = all_gather("gather_small_grads", [_pack(small_list)])
    small_sum = _unpack(sum_parts("sum_small_grads", small_all), small_list)
    g_small = dict(zip(small_names, small_sum))
    d_ada_all = small_all.reshape(N_DEV, -1)[:, :6 * d]
    d_ada_cols = lax.dynamic_slice(d_ada_all, (0, me * ada_w), (N_DEV, ada_w))
    d_ada_pad = jnp.pad(d_ada_cols, ((0, LANES - N_DEV), (0, 0)))
    gw_ada = mm(c_pad, d_ada_pad, name="ada_bwd_w", ta=True, a_act="silu")

    res = {}
    big_parts = dict(zip(big, big_parts))
    big_parts['w_ada'] = gw_ada[None]
    for n in ['w_ada'] + big:
        res[n] = adamw("adamw_" + n, W[n], big_parts[n], M[n], V[n])
    for n in small_sh:
        w_ = W[n].shape[1]
        g_small[n] = lax.dynamic_slice(g_small[n], (0, me * w_), (W[n].shape[0], w_))
    for n in small_names:
        g_small[n] = g_small[n].reshape(W[n].shape)
    pk = lambda dct: _pack([dct[n] for n in small_names])
    s_g, s_d, s_m, s_v = adamw("adamw_small", pk(W), pk(g_small)[None], pk(M), pk(V))
    like = [W[n] for n in small_names]
    for n, g_, d_, m_, v_ in zip(small_names, _unpack(s_g, like), _unpack(s_d, like), _unpack(s_m, like), _unpack(s_v, like)):
        res[n] = (g_, d_, m_, v_)

    loss = lax.psum(loss_loc[0, 0], ("x", "y", "c"))
    outs = [loss, grad_x[None]]
    for j in range(4):
        outs += [res[n][j].reshape(loc[n].shape) for n in names]
    return tuple(outs)
```

```python
import functools
import math

import jax
import jax.numpy as jnp
from jax import lax
from jax.experimental import pallas as pl
from jax.experimental.pallas import tpu as pltpu

F32 = jnp.float32
BF16 = jnp.bfloat16
MXU_DTYPE = BF16
WIRE_DTYPE = BF16
DN_DTYPE = BF16
HI = lax.Precision.HIGHEST
MESH = pl.DeviceIdType.MESH

N_DEV = 8
LANES = 128
SUBLANES = 8
VMEM_LIMIT = 56 * 1024 * 1024

RG_C = 8.0
DN_CHUNK = 64
DN_HEAD_GROUP = 8
LN_EPS = 1e-5
RMS_EPS = 1e-6
L2_EPS = 1e-6
DEPTH = 1
DEEPNORM_ALPHA = (2 * DEPTH) ** 0.25
ADAM_LR = 0.001
ADAM_B1 = 0.9
ADAM_B2 = 0.999
ADAM_EPS = 1e-08
ADAM_WD = 0.01
ADAM_STEP = 10


def _tile(n, cap, unit=LANES):
    best = None
    for t in range(unit, min(n, cap) + 1, unit):
        if n % t == 0:
            best = t
    return best if best is not None else n


def _round_up(n, m):
    return (n + m - 1) // m * m


def mm(a, b, *, name, ta=False, tb=False, a_act=None, bias=None, out_dtype=F32,
       tm_cap=1024, tn_cap=512, tk_cap=1024):
    m, k = (a.shape[1], a.shape[0]) if ta else a.shape
    n = b.shape[0] if tb else b.shape[1]
    assert k == (b.shape[1] if tb else b.shape[0]), (a.shape, b.shape, ta, tb)
    tm, tn, tk = _tile(m, tm_cap), _tile(n, tn_cap), _tile(k, tk_cap)
    nk = k // tk
    dims = (((0 if ta else 1,), (1 if tb else 0,)), ((), ()))

    def body(*refs):
        if bias is None:
            a_ref, b_ref, o_ref, acc_ref = refs
        else:
            a_ref, b_ref, bias_ref, o_ref, acc_ref = refs
        kk = pl.program_id(2)

        @pl.when(kk == 0)
        def _():
            acc_ref[...] = jnp.zeros_like(acc_ref)

        av = a_ref[...]
        if a_act == "silu":
            av = jax.nn.silu(av.astype(F32))
        acc_ref[...] += lax.dot_general(av.astype(MXU_DTYPE), b_ref[...].astype(MXU_DTYPE), dims,
                                        preferred_element_type=F32)

        @pl.when(kk == nk - 1)
        def _():
            r = acc_ref[...]
            if bias is not None:
                r = r + bias_ref[...]
            o_ref[...] = r.astype(o_ref.dtype)

    a_spec = pl.BlockSpec((tk, tm), lambda i, j, q: (q, i)) if ta else pl.BlockSpec((tm, tk), lambda i, j, q: (i, q))
    b_spec = pl.BlockSpec((tn, tk), lambda i, j, q: (j, q)) if tb else pl.BlockSpec((tk, tn), lambda i, j, q: (q, j))
    in_specs, args = [a_spec, b_spec], [a, b]
    if bias is not None:
        in_specs.append(pl.BlockSpec((1, tn), lambda i, j, q: (0, j)))
        args.append(bias)
    return pl.pallas_call(
        body, name=name, grid=(m // tm, n // tn, nk),
        in_specs=in_specs, out_specs=pl.BlockSpec((tm, tn), lambda i, j, q: (i, j)),
        out_shape=jax.ShapeDtypeStruct((m, n), out_dtype),
        scratch_shapes=[pltpu.VMEM((tm, tn), F32)],
        compiler_params=pltpu.CompilerParams(dimension_semantics=("parallel", "parallel", "arbitrary"),
                                             vmem_limit_bytes=VMEM_LIMIT),
    )(*args)


class In:
    def __init__(self, arr, block, imap, acc=False, grad=True, parts=None, gshape=None, gimap=None):
        self.arr, self.block, self.imap, self.acc, self.grad, self.parts = arr, block, imap, acc, grad, parts
        self.gshape = arr.shape if gshape is None else gshape
        self.gimap = imap if gimap is None else gimap


class Out:
    def __init__(self, shape, block, imap, dtype=F32):
        self.shape, self.block, self.imap, self.dtype = shape, block, imap, dtype


def _load(in_refs, ins):
    vals = []
    for r, i in zip(in_refs, ins):
        if i.parts is None:
            vals.append(r[...])
        else:
            vals.extend(r[p] for p in i.parts)
    return vals


def _stage_params():
    return pltpu.CompilerParams(dimension_semantics=("parallel", "arbitrary"), vmem_limit_bytes=VMEM_LIMIT)


def stage_fwd(name, f, grid, ins, outs, carries=()):
    n_in, n_out, n_c = len(ins), len(outs), len(carries)

    def body(*refs):
        in_refs, out_refs = refs[:n_in], refs[n_in:n_in + n_out]
        hist_refs, c_refs = refs[n_in + n_out:n_in + n_out + n_c], refs[n_in + n_out + n_c:]
        if n_c:
            @pl.when(pl.program_id(1) == 0)
            def _():
                for c in c_refs:
                    c[...] = jnp.zeros_like(c)
        cin = [c[...] for c in c_refs]
        for h, c in zip(hist_refs, cin):
            h[...] = c
        o, cout = f(*_load(in_refs, ins), *cin)
        for r, v in zip(out_refs, o):
            r[...] = v.astype(r.dtype)
        for c, v in zip(c_refs, cout):
            c[...] = v

    hist_spec = lambda c: pl.BlockSpec((None, None) + tuple(c), lambda o, s: (o, s) + (0,) * len(c))
    res = pl.pallas_call(
        body, name=name, grid=grid,
        in_specs=[pl.BlockSpec(i.block, i.imap) for i in ins],
        out_specs=[pl.BlockSpec(o.block, o.imap) for o in outs] + [hist_spec(c) for c in carries],
        out_shape=[jax.ShapeDtypeStruct(o.shape, o.dtype) for o in outs]
        + [jax.ShapeDtypeStruct(tuple(grid) + tuple(c), F32) for c in carries],
        scratch_shapes=[pltpu.VMEM(tuple(c), F32) for c in carries],
        compiler_params=_stage_params(),
    )(*[i.arr for i in ins])
    return list(res[:n_out]), list(res[n_out:])


def stage_bwd(name, f, grid, ins, outs, cots, carries=(), hists=(), add_to=None, gdtypes=None):
    n_in, n_out, n_c = len(ins), len(outs), len(carries)
    ns = grid[1]
    add_to = add_to or {}
    gdtypes = gdtypes or {}
    add_idx = sorted(add_to)
    g_idx = [k for k, i in enumerate(ins) if i.grad]
    cots = [c if isinstance(c, (tuple, list)) else (c,) for c in cots]
    n_cot = [len(c) for c in cots]
    rev = lambda imap: (lambda o, s: imap(o, ns - 1 - s))

    def body(*refs):
        p = 0
        in_refs = refs[p:p + n_in]; p += n_in
        cot_refs = []
        for cnt in n_cot:
            cot_refs.append(refs[p:p + cnt]); p += cnt
        hist_refs = refs[p:p + n_c]; p += n_c
        add_refs = refs[p:p + len(add_idx)]; p += len(add_idx)
        g_refs = refs[p:p + len(g_idx)]; p += len(g_idx)
        dc_refs = refs[p:]
        first = pl.program_id(1) == 0
        if n_c:
            @pl.when(first)
            def _():
                for c in dc_refs:
                    c[...] = jnp.zeros_like(c)
        vals = _load(in_refs, ins)
        cin = [h[...] for h in hist_refs]
        (o, cout), vjp = jax.vjp(lambda *a: f(*a), *vals, *cin)
        cot_o = []
        for crs, v in zip(cot_refs, o):
            c = crs[0][...].astype(v.dtype)
            for extra in crs[1:]:
                c = c + extra[...].astype(v.dtype)
            cot_o.append(c)
        cot_c = tuple(c[...] for c in dc_refs)
        grads = vjp((tuple(cot_o), cot_c))
        pos, per_in = 0, []
        for i in ins:
            cnt = 1 if i.parts is None else len(i.parts)
            per_in.append(grads[pos:pos + cnt])
            pos += cnt
        dcin = grads[pos:]
        for gr, k in zip(g_refs, g_idx):
            i, gs = ins[k], per_in[k]
            if i.acc:
                @pl.when(first)
                def _(gr=gr):
                    gr[...] = jnp.zeros_like(gr)
                if i.parts is None:
                    gr[...] += gs[0].astype(gr.dtype)
                else:
                    for pt, g in zip(i.parts, gs):
                        gr[pt] += g.astype(gr.dtype)
            else:
                g = gs[0]
                if k in add_to:
                    g = g + add_refs[add_idx.index(k)][...].astype(g.dtype)
                gr[...] = g.astype(gr.dtype)
        for c, v in zip(dc_refs, dcin):
            c[...] = v

    in_specs = [pl.BlockSpec(i.block, rev(i.imap)) for i in ins]
    for o_, cnt in zip(outs, n_cot):
        in_specs += [pl.BlockSpec(o_.block, rev(o_.imap))] * cnt
    in_specs += [pl.BlockSpec((None, None) + tuple(c), (lambda c: (lambda o, s: (o, ns - 1 - s) + (0,) * len(c)))(c))
                 for c in carries]
    in_specs += [pl.BlockSpec(ins[k].block, rev(ins[k].gimap)) for k in add_idx]
    out_specs, out_shape = [], []
    for k in g_idx:
        i = ins[k]
        if i.acc:
            out_specs.append(pl.BlockSpec(i.block, (lambda im: (lambda o, s: im(o, 0)))(i.imap)))
        else:
            out_specs.append(pl.BlockSpec(i.block, rev(i.gimap)))
        out_shape.append(jax.ShapeDtypeStruct(i.gshape, gdtypes.get(k, F32)))
    res = pl.pallas_call(
        body, name=name, grid=grid, in_specs=in_specs, out_specs=out_specs, out_shape=out_shape,
        scratch_shapes=[pltpu.VMEM(tuple(c), F32) for c in carries],
        compiler_params=_stage_params(),
    )(*[i.arr for i in ins], *[a for c in cots for a in c], *hists, *[add_to[k] for k in add_idx])
    return list(res)


def _iota_rows(shape):
    return lax.broadcasted_iota(jnp.int32, shape, 0)


@functools.partial(jax.custom_vjp, nondiff_argnums=(1,))
def _roll_rows(x, s):
    return pltpu.roll(x, s % x.shape[0], 0)


def _roll_rows_fwd(x, s):
    return _roll_rows(x, s), None


def _roll_rows_bwd(s, _, g):
    return (_roll_rows(g, -s),)


_roll_rows.defvjp(_roll_rows_fwd, _roll_rows_bwd)


@jax.custom_vjp
def _drop_head(xx):
    return xx[SUBLANES:]


def _drop_head_fwd(xx):
    return xx[SUBLANES:], None


def _drop_head_bwd(_, g):
    return (jnp.concatenate([jnp.zeros((SUBLANES, g.shape[1]), g.dtype), g], axis=0),)


_drop_head.defvjp(_drop_head_fwd, _drop_head_bwd)


@jax.custom_vjp
def _last_rows(x):
    return x[x.shape[0] - SUBLANES:]


def _last_rows_fwd(x):
    return x[x.shape[0] - SUBLANES:], x.shape[0]


def _last_rows_bwd(n, g):
    return (jnp.concatenate([jnp.zeros((n - SUBLANES, g.shape[1]), g.dtype), g], axis=0),)


_last_rows.defvjp(_last_rows_fwd, _last_rows_bwd)


def _last_row(x):
    n = x.shape[0]
    return jnp.sum(jnp.where(_iota_rows(x.shape) == n - 1, x, 0.0), axis=0, keepdims=True)


def _scan_steps(n):
    s = 1
    while s < n:
        yield s
        s *= 2


def _block_scan_impl(a, u, h0):
    n = a.shape[0]
    row = _iota_rows(a.shape)
    for s in _scan_steps(n):
        keep = row >= s
        a_s = jnp.where(keep, pltpu.roll(a, s, 0), 1.0)
        u_s = jnp.where(keep, pltpu.roll(u, s, 0), 0.0)
        u = u + a * u_s
        a = a * a_s
    return u + a * h0


@jax.custom_vjp
def _block_scan(a, u, h0):
    return _block_scan_impl(a, u, h0)


def _block_scan_fwd(a, u, h0):
    h = _block_scan_impl(a, u, h0)
    return h, (a, h, h0)


def _block_scan_bwd(res, dh):
    a, h, h0 = res
    n = a.shape[0]
    row = _iota_rows(a.shape)
    b = jnp.where(row < n - 1, pltpu.roll(a, n - 1, 0), 0.0)
    lam = dh
    for s in _scan_steps(n):
        keep = row < n - s
        b_s = jnp.where(keep, pltpu.roll(b, n - s, 0), 1.0)
        l_s = jnp.where(keep, pltpu.roll(lam, n - s, 0), 0.0)
        lam = lam + b * l_s
        b = b * b_s
    h_prev = jnp.where(row >= 1, pltpu.roll(h, 1, 0), jnp.broadcast_to(h0, h.shape))
    d_h0 = jnp.sum(jnp.where(row == 0, a * lam, 0.0), axis=0, keepdims=True)
    return lam * h_prev, lam, d_h0


_block_scan.defvjp(_block_scan_fwd, _block_scan_bwd)


def _dot_hi(a, b, dims=(((1,), (0,)), ((), ()))):
    return lax.dot_general(a, b, dims, precision=HI, preferred_element_type=F32)


_NN, _NT, _TN = "nn", "nt", "tn"
_CONTRACT = {_NN: (1, 0), _NT: (1, 1), _TN: (0, 0)}


def _raw_dot(a, b, kind):
    ca, cb = _CONTRACT[kind]
    lead = a.ndim - 2
    dims = (((ca + lead,), (cb + lead,)), (tuple(range(lead)), tuple(range(lead))))
    return lax.dot_general(a.astype(DN_DTYPE), b.astype(DN_DTYPE), dims, preferred_element_type=F32)


@jax.custom_vjp
def _nn(a, b):
    return _raw_dot(a, b, _NN)


_nn.defvjp(lambda a, b: (_raw_dot(a, b, _NN), (a, b)),
           lambda r, g: (_raw_dot(g, r[1], _NT), _raw_dot(r[0], g, _TN)))


@jax.custom_vjp
def _nt(a, b):
    return _raw_dot(a, b, _NT)


_nt.defvjp(lambda a, b: (_raw_dot(a, b, _NT), (a, b)),
           lambda r, g: (_raw_dot(g, r[1], _NN), _raw_dot(g, r[0], _TN)))


@jax.custom_vjp
def _tn(a, b):
    return _raw_dot(a, b, _TN)


_tn.defvjp(lambda a, b: (_raw_dot(a, b, _TN), (a, b)),
           lambda r, g: (_raw_dot(r[1], g, _NT), _raw_dot(r[0], g, _NN)))


def _neumann_inverse(a):
    n = a.shape[-1]
    eye = (lax.broadcasted_iota(jnp.int32, (n, n), 0) == lax.broadcasted_iota(jnp.int32, (n, n), 1)).astype(F32)
    p = _raw_dot(a, a, _NN)
    e = p
    for _ in range(int(math.log2(n)) - 2):
        p = _raw_dot(p, p, _NN)
        e = e + p + _raw_dot(e, p, _NN)
    return eye - a + e - _raw_dot(a, e, _NN)


@jax.custom_vjp
def _unit_lower_inverse(a):
    return _neumann_inverse(a)


def _unit_lower_inverse_fwd(a):
    x = _neumann_inverse(a)
    return x, x


def _unit_lower_inverse_bwd(x, g):
    return (-_raw_dot(_raw_dot(x, g, _TN), x, _NT),)


_unit_lower_inverse.defvjp(_unit_lower_inverse_fwd, _unit_lower_inverse_bwd)


def _softplus(x):
    return jnp.maximum(x, 0.0) + jnp.log1p(jnp.exp(-jnp.abs(x)))


def _neg_expm1(x):
    series = -x * (1.0 + x * (0.5 + x * (1.0 / 6.0 + x * (1.0 / 24.0 + x * (1.0 / 120.0)))))
    return jnp.where(x > -0.03, series, 1.0 - jnp.exp(x))


def _group_indicator(c, width):
    li = lax.broadcasted_iota(jnp.int32, (c, LANES), 0)
    gi = lax.broadcasted_iota(jnp.int32, (c, LANES), 1) * width
    return ((li >= gi) & (li < gi + width)).astype(F32)


def _spread(r, ind):
    return _dot_hi(r, ind, (((1,), (1,)), ((), ())))


def f_modulate(x, sc, sh):
    return (x * (1.0 + sc) + sh,), ()


def f_deepnorm(x, y, gt, g, b):
    v = DEEPNORM_ALPHA * x + (1.0 + gt) * y
    mu = jnp.mean(v, axis=-1, keepdims=True)
    vc = v - mu
    var = jnp.mean(vc * vc, axis=-1, keepdims=True)
    return (vc * lax.rsqrt(var + LN_EPS) * g + b,), ()


def _causal_conv(x, prev, ws):
    xx = jnp.concatenate([prev, x], axis=0)
    k = len(ws)
    y = ws[k - 1] * x
    for j in range(k - 1):
        y = y + ws[j] * _drop_head(_roll_rows(xx, k - 1 - j))
    return y


def f_rg_conv(x, w0, w1, w2, w3, b, prev):
    return (_causal_conv(x, prev, (w0, w1, w2, w3)) + b,), (_last_rows(x),)


def f_dn_conv(x, w0, w1, w2, w3, prev):
    return (jax.nn.silu(_causal_conv(x, prev, (w0, w1, w2, w3))),), (_last_rows(x),)


def f_ffn_act(gp, up, w0, w1, w2, b, prev):
    return (jax.nn.gelu(_causal_conv(gp, prev, (w0, w1, w2)) + b) * up,), (_last_rows(gp),)


def f_rglru(xc, pre_r, pre_i, gr, b_a, b_x, lam, h0):
    gate_r = jax.nn.sigmoid(pre_r + b_a)
    gate_i = jax.nn.sigmoid(pre_i + b_x)
    log_a = -RG_C * gate_r * _softplus(-lam)
    a = jnp.exp(log_a)
    mult = jnp.sqrt(_neg_expm1(2.0 * log_a))
    h = _block_scan(a, mult * gate_i * xc, h0)
    return (h * jax.nn.gelu(gr),), (_last_row(h),)


def f_dn_prep(q, k, a_in, b_in, a_log, dt_bias):
    ind = _group_indicator(q.shape[1], LANES)

    def l2n(t):
        return t * _spread(lax.rsqrt(_dot_hi(t * t, ind) + L2_EPS), ind)

    g = -jnp.exp(a_log) * _softplus(a_in + dt_bias)
    n = g.shape[0]
    shift = int(math.log2(DN_CHUNK))
    ri = lax.broadcasted_iota(jnp.int32, (n, n), 0)
    ci = lax.broadcasted_iota(jnp.int32, (n, n), 1)
    tri = ((lax.shift_right_logical(ri, shift) == lax.shift_right_logical(ci, shift)) & (ri >= ci)).astype(F32)
    return (l2n(q) * (LANES ** -0.5), l2n(k), _dot_hi(tri, g), jax.nn.sigmoid(b_in)), ()


def f_dn_out(o, z, nw):
    ind = _group_indicator(o.shape[1], LANES)
    r = _spread(lax.rsqrt(_dot_hi(o * o, ind) * (1.0 / LANES) + RMS_EPS), ind)
    return (o * r * nw * jax.nn.silu(z),), ()


def f_merge(ga, gb, ya, yb):
    return (jax.nn.sigmoid(ga) * ya + jax.nn.sigmoid(gb) * yb,), ()


def _delta_intra(q, k, v, g_i, g_j, beta):
    c = q.shape[-2]
    ri = lax.broadcasted_iota(jnp.int32, (c, c), 0)
    ci = lax.broadcasted_iota(jnp.int32, (c, c), 1)
    decay = jnp.exp(jnp.where(ri >= ci, g_i - g_j, -jnp.inf))
    g_last = jnp.sum(jnp.where(_iota_rows((c, 1)) == c - 1, g_i, 0.0), axis=-2, keepdims=True)
    exp_g = jnp.exp(g_i)
    kb = k * beta
    t_inv = _unit_lower_inverse(jnp.where(ri > ci, _nt(kb, k) * decay, 0.0))
    u = _nn(t_inv, v * beta)
    w = _nn(t_inv, kb * exp_g)
    return u, w, _nt(q, k) * decay, q * exp_g, k * jnp.exp(g_last - g_i)


def _delta_inter(u, w, qk, q_dec, k_dec, g_last, state):
    v_new = u - _nn(w, state)
    o = _nn(q_dec, state) + _nn(qk, v_new)
    return o, jnp.exp(g_last) * state + _tn(k_dec, v_new)


def _chunk_spec(width, nc=None, col=0):
    if nc is None:
        return pl.BlockSpec((DN_CHUNK, width), lambda s: (s, col))
    return pl.BlockSpec((DN_CHUNK, width), lambda s: (nc - 1 - s, col))


def _delta_params(sem):
    return pltpu.CompilerParams(dimension_semantics=(sem,), vmem_limit_bytes=VMEM_LIMIT)


def _head(ref, h, width=LANES):
    return ref[:, h * LANES:h * LANES + width]


def _head_groups(n_vh):
    hb = min(DN_HEAD_GROUP, n_vh)
    return [range(h0, h0 + hb) for h0 in range(0, n_vh, hb)]


def _stack(hs, f):
    return jnp.stack([f(h) for h in hs])


def _intra_operands(hs, rep, q_ref, k_ref, v_ref, g_ref, gt_ref, b_ref):
    return (_stack(hs, lambda h: _head(q_ref, h // rep)), _stack(hs, lambda h: _head(k_ref, h // rep)),
            _stack(hs, lambda h: _head(v_ref, h)), _stack(hs, lambda h: g_ref[:, h:h + 1]),
            _stack(hs, lambda h: gt_ref[h:h + 1, :]), _stack(hs, lambda h: b_ref[:, h:h + 1]))


def _inter_operands(hs, u_ref, w_ref, qk_ref, qd_ref, kd_ref, g_ref):
    f32 = lambda ref, width=LANES: _stack(hs, lambda h: _head(ref, h, width).astype(F32))
    return (f32(u_ref), f32(w_ref), f32(qk_ref, DN_CHUNK), f32(qd_ref), f32(kd_ref),
            _stack(hs, lambda h: g_ref[DN_CHUNK - 1:DN_CHUNK, h:h + 1]))


def delta_intra_fwd(qn, kn, qkv, v_blk, big_g, big_gt, beta, n_vh):
    t, qk_w = qn.shape
    vdim = n_vh * LANES
    rep = vdim // qk_w
    nc = t // DN_CHUNK

    def body(q_ref, k_ref, v_ref, g_ref, gt_ref, b_ref, u_ref, w_ref, qk_ref, qd_ref, kd_ref):
        for hs in _head_groups(n_vh):
            u, w, qk, qd, kd = _delta_intra(*_intra_operands(hs, rep, q_ref, k_ref, v_ref, g_ref, gt_ref, b_ref))
            for i, h in enumerate(hs):
                sl = slice(h * LANES, (h + 1) * LANES)
                u_ref[:, sl] = u[i]
                w_ref[:, sl] = w[i].astype(w_ref.dtype)
                qk_ref[:, sl] = jnp.concatenate([qk[i], jnp.zeros_like(qk[i])], axis=1).astype(qk_ref.dtype)
                qd_ref[:, sl] = qd[i].astype(qd_ref.dtype)
                kd_ref[:, sl] = kd[i].astype(kd_ref.dtype)

    return pl.pallas_call(
        body, name="delta_intra_fwd", grid=(nc,),
        in_specs=[_chunk_spec(qk_w), _chunk_spec(qk_w), _chunk_spec(vdim, col=v_blk), _chunk_spec(LANES),
                  pl.BlockSpec((None, n_vh, DN_CHUNK), lambda s: (s, 0, 0)), _chunk_spec(LANES)],
        out_specs=[_chunk_spec(vdim)] * 5,
        out_shape=[jax.ShapeDtypeStruct((t, vdim), F32)] + [jax.ShapeDtypeStruct((t, vdim), DN_DTYPE)] * 4,
        compiler_params=_delta_params("parallel"),
    )(qn, kn, qkv, big_g, big_gt, beta)


def delta_inter_fwd(u, w, qk, q_dec, k_dec, big_g, n_vh):
    t, vdim = u.shape
    nc = t // DN_CHUNK

    def body(u_ref, w_ref, qk_ref, qd_ref, kd_ref, g_ref, o_ref, hist_ref, s_ref):
        @pl.when(pl.program_id(0) == 0)
        def _():
            s_ref[...] = jnp.zeros_like(s_ref)
        for hs in _head_groups(n_vh):
            grp = slice(hs[0], hs[-1] + 1)
            st = s_ref[grp]
            hist_ref[grp] = st
            o, ns = _delta_inter(*_inter_operands(hs, u_ref, w_ref, qk_ref, qd_ref, kd_ref, g_ref), st)
            for i, h in enumerate(hs):
                o_ref[:, h * LANES:(h + 1) * LANES] = o[i]
            s_ref[grp] = ns

    return pl.pallas_call(
        body, name="delta_inter_fwd", grid=(nc,),
        in_specs=[_chunk_spec(vdim)] * 5 + [_chunk_spec(LANES)],
        out_specs=[_chunk_spec(vdim), pl.BlockSpec((None, n_vh, LANES, LANES), lambda s: (s, 0, 0, 0))],
        out_shape=[jax.ShapeDtypeStruct((t, vdim), F32), jax.ShapeDtypeStruct((nc, n_vh, LANES, LANES), F32)],
        scratch_shapes=[pltpu.VMEM((n_vh, LANES, LANES), F32)],
        compiler_params=_delta_params("arbitrary"),
    )(u, w, qk, q_dec, k_dec, big_g)


def delta_inter_bwd(u, w, qk, q_dec, k_dec, big_g, hist, d_o, n_vh):
    t, vdim = u.shape
    nc = t // DN_CHUNK

    def body(u_ref, w_ref, qk_ref, qd_ref, kd_ref, g_ref, hist_ref, do_ref,
             du_ref, dw_ref, dqk_ref, dqd_ref, dkd_ref, dg_ref, ds_ref):
        @pl.when(pl.program_id(0) == 0)
        def _():
            ds_ref[...] = jnp.zeros_like(ds_ref)
        lane = lax.broadcasted_iota(jnp.int32, (1, LANES), 1)
        dgl_all = jnp.zeros((1, LANES), F32)
        for hs in _head_groups(n_vh):
            grp = slice(hs[0], hs[-1] + 1)
            prim = _inter_operands(hs, u_ref, w_ref, qk_ref, qd_ref, kd_ref, g_ref) + (hist_ref[grp],)
            _, vjp = jax.vjp(_delta_inter, *prim)
            du, dw, dqk, dqd, dkd, dgl, dst = vjp((_stack(hs, lambda h: _head(do_ref, h)), ds_ref[grp]))
            ds_ref[grp] = dst
            for i, h in enumerate(hs):
                sl = slice(h * LANES, (h + 1) * LANES)
                du_ref[:, sl] = du[i]
                dw_ref[:, sl] = dw[i]
                dqk_ref[:, sl] = jnp.concatenate([dqk[i], jnp.zeros_like(dqk[i])], axis=1)
                dqd_ref[:, sl] = dqd[i]
                dkd_ref[:, sl] = dkd[i]
                dgl_all = dgl_all + dgl[i] * (lane == h).astype(F32)
        last = _iota_rows((DN_CHUNK, LANES)) == DN_CHUNK - 1
        dg_ref[...] = jnp.where(last, jnp.broadcast_to(dgl_all, (DN_CHUNK, LANES)), 0.0)

    rv = lambda w_: _chunk_spec(w_, nc)
    return pl.pallas_call(
        body, name="delta_inter_bwd", grid=(nc,),
        in_specs=[rv(vdim)] * 5 + [rv(LANES), pl.BlockSpec((None, n_vh, LANES, LANES), lambda s: (nc - 1 - s, 0, 0, 0)),
                                   rv(vdim)],
        out_specs=[rv(vdim)] * 5 + [rv(LANES)],
        out_shape=[jax.ShapeDtypeStruct((t, vdim), F32)] * 5 + [jax.ShapeDtypeStruct((t, LANES), F32)],
        scratch_shapes=[pltpu.VMEM((n_vh, LANES, LANES), F32)],
        compiler_params=_delta_params("arbitrary"),
    )(u, w, qk, q_dec, k_dec, big_g, hist, d_o)


def delta_intra_bwd(qn, kn, qkv, v_blk, big_g, big_gt, beta, cots, n_vh):
    t, qk_w = qn.shape
    vdim = n_vh * LANES
    rep = vdim // qk_w
    nc = t // DN_CHUNK

    def body(q_ref, k_ref, v_ref, g_ref, gt_ref, b_ref, du_ref, dw_ref, dqk_ref, dqd_ref, dkd_ref,
             dq_ref, dk_ref, dv_ref, dg_ref, dgt_ref, db_ref):
        lane = lax.broadcasted_iota(jnp.int32, (1, LANES), 1)
        dg_all = jnp.zeros((DN_CHUNK, LANES), F32)
        db_all = jnp.zeros((DN_CHUNK, LANES), F32)
        dq_acc, dk_acc = None, None
        for hs in _head_groups(n_vh):
            _, vjp = jax.vjp(_delta_intra, *_intra_operands(hs, rep, q_ref, k_ref, v_ref, g_ref, gt_ref, b_ref))
            cot = lambda ref, width=LANES: _stack(hs, lambda h: _head(ref, h, width))
            dq, dk, dv, dgi, dgj, db = vjp((cot(du_ref), cot(dw_ref), cot(dqk_ref, DN_CHUNK), cot(dqd_ref), cot(dkd_ref)))
            for i, h in enumerate(hs):
                j = h // rep
                dv_ref[:, h * LANES:(h + 1) * LANES] = dv[i]
                dgt_ref[h:h + 1, :] = dgj[i]
                onehot = (lane == h).astype(F32)
                dg_all = dg_all + dgi[i] * onehot
                db_all = db_all + db[i] * onehot
                dq_acc = dq[i] if h % rep == 0 else dq_acc + dq[i]
                dk_acc = dk[i] if h % rep == 0 else dk_acc + dk[i]
                if h % rep == rep - 1:
                    dq_ref[:, j * LANES:(j + 1) * LANES] = dq_acc
                    dk_ref[:, j * LANES:(j + 1) * LANES] = dk_acc
        dg_ref[...] = dg_all
        db_ref[...] = db_all

    gt_spec = pl.BlockSpec((None, n_vh, DN_CHUNK), lambda s: (s, 0, 0))
    return pl.pallas_call(
        body, name="delta_intra_bwd", grid=(nc,),
        in_specs=[_chunk_spec(qk_w), _chunk_spec(qk_w), _chunk_spec(vdim, col=v_blk), _chunk_spec(LANES), gt_spec,
                  _chunk_spec(LANES)] + [_chunk_spec(vdim)] * 5,
        out_specs=[_chunk_spec(qk_w), _chunk_spec(qk_w), _chunk_spec(vdim), _chunk_spec(LANES), gt_spec,
                   _chunk_spec(LANES)],
        out_shape=[jax.ShapeDtypeStruct((t, qk_w), F32), jax.ShapeDtypeStruct((t, qk_w), F32),
                   jax.ShapeDtypeStruct((t, vdim), F32), jax.ShapeDtypeStruct((t, LANES), F32),
                   jax.ShapeDtypeStruct((nc, n_vh, DN_CHUNK), F32), jax.ShapeDtypeStruct((t, LANES), F32)],
        compiler_params=_delta_params("parallel"),
    )(qn, kn, qkv, big_g, big_gt, beta, *cots)


def loss_head(y, target, tb):
    t, d = y.shape

    def body(y_ref, t_ref, dy_ref, loss_ref):
        @pl.when(pl.program_id(0) == 0)
        def _():
            loss_ref[...] = jnp.zeros_like(loss_ref)
        err = y_ref[...] - t_ref[...]
        dy_ref[...] = err * (1.0 / d)
        loss_ref[...] += 0.5 * jnp.sum(jnp.sum(err * err, axis=1, keepdims=True), axis=0, keepdims=True) * (1.0 / d)

    return pl.pallas_call(
        body, name="loss_head", grid=(t // tb,),
        in_specs=[pl.BlockSpec((tb, d), lambda s: (s, 0))] * 2,
        out_specs=[pl.BlockSpec((tb, d), lambda s: (s, 0)), pl.BlockSpec((1, 1), lambda s: (0, 0))],
        out_shape=[jax.ShapeDtypeStruct((t, d), F32), jax.ShapeDtypeStruct((1, 1), F32)],
        compiler_params=pltpu.CompilerParams(dimension_semantics=("arbitrary",), vmem_limit_bytes=VMEM_LIMIT),
    )(y, target)


def _exchange(name, arrs, scatter):
    n = len(arrs)

    def body(*refs):
        in_refs, out_refs = refs[:n], refs[n:2 * n]
        send_sems, recv_sems, local_sems = refs[2 * n:]
        x, y, c = lax.axis_index("x"), lax.axis_index("y"), lax.axis_index("c")
        me = 4 * x + 2 * y + c

        def peer(k):
            return (x ^ ((k >> 2) & 1), y ^ ((k >> 1) & 1), c ^ (k & 1))

        def lin(p):
            return 4 * p[0] + 2 * p[1] + p[2]

        local = []
        for i in range(n):
            src = in_refs[i].at[me] if scatter else in_refs[i]
            cp = pltpu.make_async_copy(src, out_refs[i].at[me], local_sems.at[i])
            cp.start()
            local.append(cp)
        sends = []
        for k in range(1, N_DEV):
            p = peer(k)
            for i in range(n):
                src = in_refs[i].at[lin(p)] if scatter else in_refs[i]
                cp = pltpu.make_async_remote_copy(src_ref=src, dst_ref=out_refs[i].at[me],
                                                  send_sem=send_sems.at[i, k - 1], recv_sem=recv_sems.at[i, k - 1],
                                                  device_id=p, device_id_type=MESH)
                cp.start()
                sends.append(cp)
        for k in range(1, N_DEV):
            p = peer(k)
            for i in range(n):
                src = in_refs[i].at[me] if scatter else in_refs[i]
                pltpu.make_async_remote_copy(src_ref=src, dst_ref=out_refs[i].at[lin(p)],
                                             send_sem=send_sems.at[i, k - 1], recv_sem=recv_sems.at[i, k - 1],
                                             device_id=p, device_id_type=MESH).wait_recv()
        for cp in sends:
            cp.wait_send()
        for cp in local:
            cp.wait()

    hbm = pl.BlockSpec(memory_space=pl.ANY)
    res = pl.pallas_call(
        body, name=name,
        in_specs=[hbm] * n, out_specs=[hbm] * n,
        out_shape=[jax.ShapeDtypeStruct(a.shape if scatter else (N_DEV,) + a.shape, a.dtype) for a in arrs],
        scratch_shapes=[pltpu.SemaphoreType.DMA((n, N_DEV - 1)), pltpu.SemaphoreType.DMA((n, N_DEV - 1)),
                        pltpu.SemaphoreType.DMA((n,))],
        compiler_params=pltpu.CompilerParams(has_side_effects=True),
    )(*arrs)
    return list(res)


def all_gather(name, arrs):
    return _exchange(name, arrs, scatter=False)


def all_to_all(name, arrs):
    return _exchange(name, arrs, scatter=True)


def _adamw_math(w, g, m, v):
    m = ADAM_B1 * m + (1.0 - ADAM_B1) * g
    v = ADAM_B2 * v + (1.0 - ADAM_B2) * (g * g)
    m_hat = m / (1.0 - ADAM_B1 ** ADAM_STEP)
    v_hat = v / (1.0 - ADAM_B2 ** ADAM_STEP)
    delta = -ADAM_LR * (m_hat / (jnp.sqrt(v_hat) + ADAM_EPS) + ADAM_WD * w)
    return delta, m, v


def adamw(name, w, parts, m, v, rows_cap=128):
    r, c = w.shape
    np_ = parts.shape[0]
    tr = _tile(r, rows_cap, SUBLANES * (4 // parts.dtype.itemsize))

    def body(w_ref, p_ref, m_ref, v_ref, g_ref, d_ref, nm_ref, nv_ref):
        g = p_ref[0].astype(F32)
        for k in range(1, np_):
            g = g + p_ref[k].astype(F32)
        delta, nm, nv = _adamw_math(w_ref[...], g, m_ref[...], v_ref[...])
        g_ref[...] = g
        d_ref[...] = delta
        nm_ref[...] = nm
        nv_ref[...] = nv

    spec = pl.BlockSpec((tr, c), lambda i: (i, 0))
    return pl.pallas_call(
        body, name=name, grid=(r // tr,),
        in_specs=[spec, pl.BlockSpec((np_, tr, c), lambda i: (0, i, 0)), spec, spec],
        out_specs=[spec] * 4, out_shape=[jax.ShapeDtypeStruct((r, c), F32)] * 4,
        compiler_params=pltpu.CompilerParams(dimension_semantics=("parallel",), vmem_limit_bytes=VMEM_LIMIT),
    )(w, parts, m, v)


def sum_parts(name, parts, rows_cap=256):
    np_, r, c = parts.shape
    tr = _tile(r, rows_cap, SUBLANES)

    def body(p_ref, o_ref):
        g = p_ref[0].astype(F32)
        for k in range(1, np_):
            g = g + p_ref[k].astype(F32)
        o_ref[...] = g

    return pl.pallas_call(
        body, name=name, grid=(r // tr,),
        in_specs=[pl.BlockSpec((np_, tr, c), lambda i: (0, i, 0))],
        out_specs=pl.BlockSpec((tr, c), lambda i: (i, 0)),
        out_shape=jax.ShapeDtypeStruct((r, c), F32),
        compiler_params=pltpu.CompilerParams(dimension_semantics=("parallel",), vmem_limit_bytes=VMEM_LIMIT),
    )(parts)


def _pack(arrs):
    flat = jnp.concatenate([a.reshape(-1).astype(F32) for a in arrs])
    n = flat.shape[0]
    return jnp.pad(flat, (0, _round_up(n, LANES * SUBLANES) - n)).reshape(-1, LANES)


def _unpack(packed, like):
    flat, out, pos = packed.reshape(-1), [], 0
    for a in like:
        out.append(flat[pos:pos + a.size].reshape(a.shape))
        pos += a.size
    return out


def kernel(x, c, w_ada, b_ada, w_in, rg_conv_w, rg_conv_b, rg_w_a, rg_b_a, rg_w_x, rg_b_x, rg_lambda, dn_conv_w, dn_a_log, dn_dt_bias, dn_norm_w, w_proj_a, w_proj_b, w_out, ln1_g, ln1_b, ffn_w_gate, ffn_w_up, ffn_conv_w, ffn_conv_b, ffn_w_down, ln2_g, ln2_b, loss_target, m_w_ada, m_b_ada, m_w_in, m_rg_conv_w, m_rg_conv_b, m_rg_w_a, m_rg_b_a, m_rg_w_x, m_rg_b_x, m_rg_lambda, m_dn_conv_w, m_dn_a_log, m_dn_dt_bias, m_dn_norm_w, m_w_proj_a, m_w_proj_b, m_w_out, m_ln1_g, m_ln1_b, m_ffn_w_gate, m_ffn_w_up, m_ffn_conv_w, m_ffn_conv_b, m_ffn_w_down, m_ln2_g, m_ln2_b, v_w_ada, v_b_ada, v_w_in, v_rg_conv_w, v_rg_conv_b, v_rg_w_a, v_rg_b_a, v_rg_w_x, v_rg_b_x, v_rg_lambda, v_dn_conv_w, v_dn_a_log, v_dn_dt_bias, v_dn_norm_w, v_w_proj_a, v_w_proj_b, v_w_out, v_ln1_g, v_ln1_b, v_ffn_w_gate, v_ffn_w_up, v_ffn_conv_w, v_ffn_conv_b, v_ffn_w_down, v_ln2_g, v_ln2_b):
    names = ['w_ada', 'b_ada', 'w_in', 'rg_conv_w', 'rg_conv_b', 'rg_w_a', 'rg_b_a', 'rg_w_x', 'rg_b_x', 'rg_lambda',
             'dn_conv_w', 'dn_a_log', 'dn_dt_bias', 'dn_norm_w', 'w_proj_a', 'w_proj_b', 'w_out', 'ln1_g', 'ln1_b',
             'ffn_w_gate', 'ffn_w_up', 'ffn_conv_w', 'ffn_conv_b', 'ffn_w_down', 'ln2_g', 'ln2_b']
    loc = locals()
    W = {n: loc[n][0] for n in names}
    M = {n: loc['m_' + n][0] for n in names}
    V = {n: loc['v_' + n][0] for n in names}

    me = 4 * lax.axis_index("x") + 2 * lax.axis_index("y") + lax.axis_index("c")
    xs, tgt = x[0], loss_target[0]
    t, d = xs.shape
    d_rnn = W['rg_conv_b'].shape[0]
    n_blk = W['rg_w_a'].shape[0]
    n_vh = W['dn_a_log'].shape[0]
    assert W['dn_norm_w'].shape[0] == LANES
    vdim = n_vh * LANES
    d_ff = W['ffn_conv_b'].shape[0]
    d_in = W['w_in'].shape[1] * N_DEV
    qk = (d_in - 2 * d_rnn - 2 * vdim - 2 * n_vh - 2 * d) // 2
    assert vdim == 2 * qk and qk % LANES == 0 and n_vh <= LANES
    splits = (d_rnn, d_rnn, qk, qk, vdim, vdim, n_vh, n_vh, d, d)
    offs = [0]
    for s_ in splits:
        offs.append(offs[-1] + s_)

    tb = _tile(t, 256, SUBLANES)

    big = ['w_in', 'w_proj_a', 'w_proj_b', 'w_out', 'ffn_w_gate', 'ffn_w_up', 'ffn_w_down']
    small_sh = ['rg_conv_w', 'dn_conv_w', 'ffn_conv_w']
    gathered = all_gather("gather_weights", [W[n].astype(WIRE_DTYPE) for n in big] + [W[n] for n in small_sh])
    g_in, g_pa, g_pb, g_out, g_fg, g_fu, g_fd, g_rcw, g_dcw, g_fcw = gathered
    cols = lambda g: jnp.transpose(g, (1, 0, 2)).reshape(g.shape[1], -1)
    rows = lambda g: g.reshape(-1, g.shape[2])
    w_in_f = cols(g_in)
    padl = lambda a: jnp.pad(a, ((0, 0), (0, LANES - a.shape[1])))
    groups = [w_in_f[:, offs[i]:offs[i + 1]] for i in range(10)]
    groups[6], groups[7] = padl(groups[6]), padl(groups[7])
    go = [0]
    for g_ in groups:
        go.append(go[-1] + g_.shape[1])
    n_pad = _round_up(go[-1], 512)
    wp = jnp.pad(jnp.concatenate(groups, axis=1), ((0, 0), (0, n_pad - go[-1])))
    o_xr, o_gr, o_q, o_k, o_v, o_z, o_a, o_b, o_ga, o_gb = go[:10]
    w_pa, w_pb, w_o, w_fd = rows(g_pa), rows(g_pb), rows(g_out), rows(g_fd)
    w_gu = jnp.concatenate([cols(g_fg), cols(g_fu)], axis=1)
    rcw, dcw, fcw = cols(g_rcw), cols(g_dcw), cols(g_fcw)
    eye_b = jnp.eye(n_blk, dtype=F32)
    bd = lambda w: (w[:, :, None, :] * eye_b[:, None, :, None]).reshape(d_rnn, d_rnn)
    w_bd = jnp.concatenate([bd(W['rg_w_a']), bd(W['rg_w_x'])], axis=1)
    row1 = lambda a: a.reshape(1, -1)
    padv = lambda a: jnp.pad(row1(a), ((0, 0), (0, LANES - a.shape[0])))
    nw_t = jnp.tile(row1(W['dn_norm_w']), (1, n_vh))

    (c_all,) = all_gather("gather_c", [c])
    c_pad = jnp.pad(c_all.reshape(N_DEV, d), ((0, LANES - N_DEV), (0, 0)))
    ada_w = W['w_ada'].shape[1]
    b_ada_me = lax.dynamic_slice(W['b_ada'], (me * ada_w,), (ada_w,)).reshape(1, ada_w)
    ada_sh = mm(c_pad, W['w_ada'], name="ada_fwd", a_act="silu", bias=b_ada_me)
    (ada_all,) = all_gather("gather_ada", [ada_sh[:N_DEV]])
    ada_me = lax.dynamic_slice(ada_all, (0, me, 0), (N_DEV, 1, ada_w)).reshape(6, 1, d)
    sh1, sc1, gt1, sh2, sc2, gt2 = [ada_me[i] for i in range(6)]

    nt = t // tb

    def act(a, bw, col0=0, width=None, grad=True):
        width = a.shape[1] if width is None else width
        assert col0 % bw == 0 and width % bw == 0
        c0 = col0 // bw
        return In(a, (tb, bw), lambda o, s: (s, c0 + o), grad=grad, gshape=(t, width), gimap=lambda o, s: (s, o))

    def prm(a, bw, parts=None):
        return In(a, (a.shape[0], bw), lambda o, s: (0, o), acc=True, parts=parts)

    def out(width, bw):
        return Out((t, width), (tb, bw), lambda o, s: (s, o))

    krows = lambda k_: [(slice(j, j + 1), slice(None)) for j in range(k_)]

    mod1_ins = [act(xs, d), prm(sc1, d), prm(sh1, d)]
    (h1,), _ = stage_fwd("mod1_fwd", f_modulate, (1, nt), mod1_ins, [out(d, d)])
    proj = mm(h1, wp, name="proj_fwd")

    cb_r = _tile(math.gcd(d_rnn, o_gr), 256)
    rgc_ins = [act(proj, cb_r, o_xr, d_rnn), prm(rcw, cb_r, krows(4)), prm(row1(W['rg_conv_b']), cb_r)]
    rgc_grid, rgc_car = (d_rnn // cb_r, nt), [(SUBLANES, cb_r)]
    (xc,), rgc_hist = stage_fwd("rg_conv_fwd", f_rg_conv, rgc_grid, rgc_ins, [out(d_rnn, cb_r)], rgc_car)
    gates = mm(xc, w_bd, name="rg_gates_fwd")
    lru_ins = [act(xc, cb_r), act(gates, cb_r, 0, d_rnn), act(gates, cb_r, d_rnn, d_rnn), act(proj, cb_r, o_gr, d_rnn),
               prm(row1(W['rg_b_a']), cb_r), prm(row1(W['rg_b_x']), cb_r), prm(row1(W['rg_lambda']), cb_r)]
    lru_car = [(1, cb_r)]
    (rec,), lru_hist = stage_fwd("rglru_fwd", f_rglru, rgc_grid, lru_ins, [out(d_rnn, cb_r)], lru_car)
    y_a = mm(rec, w_pa, name="proj_a_fwd")

    qkv_w = 2 * qk + vdim
    cb_q = _tile(math.gcd(qkv_w, o_q), 256)
    dnc_ins = [act(proj, cb_q, o_q, qkv_w), prm(dcw, cb_q, krows(4))]
    dnc_grid, dnc_car = (qkv_w // cb_q, nt), [(SUBLANES, cb_q)]
    (qkv_c,), dnc_hist = stage_fwd("dn_conv_fwd", f_dn_conv, dnc_grid, dnc_ins, [out(qkv_w, cb_q)], dnc_car)
    prep_ins = [act(qkv_c, qk, 0, qk), act(qkv_c, qk, qk, qk), act(proj, LANES, o_a, LANES), act(proj, LANES, o_b, LANES),
                prm(padv(W['dn_a_log']), LANES), prm(padv(W['dn_dt_bias']), LANES)]
    prep_outs = [out(qk, qk), out(qk, qk), out(LANES, LANES), out(LANES, LANES)]
    (qn, kn, g_dn, beta_dn), _ = stage_fwd("dn_prep_fwd", f_dn_prep, (1, nt), prep_ins, prep_outs)
    n_ch = t // DN_CHUNK
    gt_dn = jnp.transpose(g_dn.reshape(n_ch, DN_CHUNK, LANES)[:, :, :n_vh], (0, 2, 1))
    dn_mid = delta_intra_fwd(qn, kn, qkv_c, 1, g_dn, gt_dn, beta_dn, n_vh)
    o_dn, dn_hist = delta_inter_fwd(*dn_mid, g_dn, n_vh)
    cb_z = _tile(math.gcd(vdim, o_z), 512)
    dno_ins = [act(o_dn, cb_z), act(proj, cb_z, o_z, vdim), prm(nw_t, cb_z)]
    dno_grid = (vdim // cb_z, nt)
    (dn,), _ = stage_fwd("dn_out_fwd", f_dn_out, dno_grid, dno_ins, [out(vdim, cb_z)])
    y_b = mm(dn, w_pb, name="proj_b_fwd")

    cb_m = _tile(math.gcd(math.gcd(d, o_ga), o_gb), 512)
    mrg_ins = [act(proj, cb_m, o_ga, d), act(proj, cb_m, o_gb, d), act(y_a, cb_m), act(y_b, cb_m)]
    mrg_grid = (d // cb_m, nt)
    (merged,), _ = stage_fwd("merge_fwd", f_merge, mrg_grid, mrg_ins, [out(d, cb_m)])
    mix = mm(merged, w_o, name="w_out_fwd")
    ln1_ins = [act(xs, d), act(mix, d), prm(gt1, d), prm(row1(W['ln1_g']), d), prm(row1(W['ln1_b']), d)]
    (x1,), _ = stage_fwd("ln1_fwd", f_deepnorm, (1, nt), ln1_ins, [out(d, d)])

    mod2_ins = [act(x1, d), prm(sc2, d), prm(sh2, d)]
    (h2,), _ = stage_fwd("mod2_fwd", f_modulate, (1, nt), mod2_ins, [out(d, d)])
    gu = mm(h2, w_gu, name="ffn_in_fwd")
    cb_f = _tile(d_ff, 256)
    ffa_ins = [act(gu, cb_f, 0, d_ff), act(gu, cb_f, d_ff, d_ff), prm(fcw, cb_f, krows(3)), prm(row1(W['ffn_conv_b']), cb_f)]
    ffa_grid, ffa_car = (d_ff // cb_f, nt), [(SUBLANES, cb_f)]
    (act_ff,), ffa_hist = stage_fwd("ffn_act_fwd", f_ffn_act, ffa_grid, ffa_ins, [out(d_ff, cb_f)], ffa_car)
    ff = mm(act_ff, w_fd, name="ffn_down_fwd")
    ln2_ins = [act(x1, d), act(ff, d), prm(gt2, d), prm(row1(W['ln2_g']), d), prm(row1(W['ln2_b']), d)]
    (x2,), _ = stage_fwd("ln2_fwd", f_deepnorm, (1, nt), ln2_ins, [out(d, d)])
    dy, loss_loc = loss_head(x2, tgt, tb)

    dx1_a, d_ff_o, d_gt2, d_ln2g, d_ln2b = stage_bwd("ln2_bwd", f_deepnorm, (1, nt), ln2_ins, [out(d, d)], [dy])
    d_act = mm(d_ff_o, w_fd, name="ffn_down_bwd_x", tb=True)
    gw_fd = mm(act_ff, d_ff_o, name="ffn_down_bwd_w", ta=True)
    d_gp, d_up, d_fcw, d_fcb = stage_bwd("ffn_act_bwd", f_ffn_act, ffa_grid, ffa_ins, [out(d_ff, cb_f)], [d_act],
                                         ffa_car, ffa_hist, gdtypes={0: MXU_DTYPE, 1: MXU_DTYPE})
    d_gu = jnp.concatenate([d_gp, d_up], axis=1)
    d_h2 = mm(d_gu, w_gu, name="ffn_in_bwd_x", tb=True)
    gw_gu = mm(h2, d_gu, name="ffn_in_bwd_w", ta=True)
    d_x1, d_sc2, d_sh2 = stage_bwd("mod2_bwd", f_modulate, (1, nt), mod2_ins, [out(d, d)], [d_h2], add_to={0: dx1_a})
    dx_a, d_mix, d_gt1, d_ln1g, d_ln1b = stage_bwd("ln1_bwd", f_deepnorm, (1, nt), ln1_ins, [out(d, d)], [d_x1])
    d_merged = mm(d_mix, w_o, name="w_out_bwd_x", tb=True)
    gw_o = mm(merged, d_mix, name="w_out_bwd_w", ta=True)
    d_ga, d_gb, d_ya, d_yb = stage_bwd("merge_bwd", f_merge, mrg_grid, mrg_ins, [out(d, cb_m)], [d_merged],
                                       gdtypes={0: MXU_DTYPE, 1: MXU_DTYPE})
    d_rec = mm(d_ya, w_pa, name="proj_a_bwd_x", tb=True)
    gw_pa = mm(rec, d_ya, name="proj_a_bwd_w", ta=True)
    d_dn = mm(d_yb, w_pb, name="proj_b_bwd_x", tb=True)
    gw_pb = mm(dn, d_yb, name="proj_b_bwd_w", ta=True)

    d_o, d_z, d_nwt = stage_bwd("dn_out_bwd", f_dn_out, dno_grid, dno_ins, [out(vdim, cb_z)], [d_dn],
                                gdtypes={1: MXU_DTYPE})
    *d_mid, d_g_state = delta_inter_bwd(*dn_mid, g_dn, dn_hist, d_o, n_vh)
    d_qn, d_kn, d_v, d_g_col, d_gt, d_beta = delta_intra_bwd(qn, kn, qkv_c, 1, g_dn, gt_dn, beta_dn, d_mid, n_vh)
    d_g_row = jnp.pad(jnp.transpose(d_gt, (0, 2, 1)).reshape(t, n_vh), ((0, 0), (0, LANES - n_vh)))
    d_qc, d_kc, d_a, d_b, d_alog, d_dtb = stage_bwd("dn_prep_bwd", f_dn_prep, (1, nt), prep_ins, prep_outs,
                                                    [d_qn, d_kn, (d_g_state, d_g_col, d_g_row), d_beta],
                                                    gdtypes={2: MXU_DTYPE, 3: MXU_DTYPE})
    d_qkv_c = jnp.concatenate([d_qc, d_kc, d_v], axis=1)
    d_qkv, d_dcw = stage_bwd("dn_conv_bwd", f_dn_conv, dnc_grid, dnc_ins, [out(qkv_w, cb_q)], [d_qkv_c],
                             dnc_car, dnc_hist, gdtypes={0: MXU_DTYPE})

    d_xc_a, d_pr, d_pi, d_gr, d_ba, d_bx, d_lam = stage_bwd(
        "rglru_bwd", f_rglru, rgc_grid, lru_ins, [out(d_rnn, cb_r)], [d_rec], lru_car, lru_hist,
        gdtypes={1: MXU_DTYPE, 2: MXU_DTYPE, 3: MXU_DTYPE})
    d_gates = jnp.concatenate([d_pr, d_pi], axis=1)
    d_xc_b = mm(d_gates, w_bd, name="rg_gates_bwd_x", tb=True)
    gw_bd = mm(xc, d_gates, name="rg_gates_bwd_w", ta=True)
    d_xr, d_rcw, d_rcb = stage_bwd("rg_conv_bwd", f_rg_conv, rgc_grid, rgc_ins, [out(d_rnn, cb_r)], [(d_xc_a, d_xc_b)],
                                   rgc_car, rgc_hist, gdtypes={0: MXU_DTYPE})

    d_proj = jnp.concatenate([d_xr, d_gr, d_qkv, d_z, d_a, d_b, d_ga, d_gb,
                              jnp.zeros((t, n_pad - go[-1]), MXU_DTYPE)], axis=1)
    d_h1 = mm(d_proj, wp, name="proj_bwd_x", tb=True)
    gw_p = mm(h1, d_proj, name="proj_bwd_w", ta=True)
    grad_x, d_sc1, d_sh1 = stage_bwd("mod1_bwd", f_modulate, (1, nt), mod1_ins, [out(d, d)], [d_h1], add_to={0: dx_a})

    gw_in = jnp.concatenate([gw_p[:, go[i]:go[i] + splits[i]] for i in range(10)], axis=1)
    col_blocks = lambda g: jnp.transpose(g.reshape(g.shape[0], N_DEV, -1), (1, 0, 2))
    row_blocks = lambda g: g.reshape(N_DEV, -1, g.shape[1])
    big_blocks = [col_blocks(gw_in), row_blocks(gw_pa), row_blocks(gw_pb), row_blocks(gw_o),
                  col_blocks(gw_gu[:, :d_ff]), col_blocks(gw_gu[:, d_ff:]), row_blocks(gw_fd)]
    big_parts = all_to_all("scatter_grads", [b_.astype(WIRE_DTYPE) for b_ in big_blocks])

    diag = lambda g: jnp.einsum('nimj,nm->nij', g.reshape(n_blk, d_rnn // n_blk, n_blk, d_rnn // n_blk), eye_b)
    d_ada_me = jnp.concatenate([d_sh1, d_sc1, d_gt1, d_sh2, d_sc2, d_gt2], axis=1)
    small_names = ['b_ada', 'rg_conv_w', 'rg_conv_b', 'rg_w_a', 'rg_b_a', 'rg_w_x', 'rg_b_x', 'rg_lambda', 'dn_conv_w',
                   'dn_a_log', 'dn_dt_bias', 'dn_norm_w', 'ln1_g', 'ln1_b', 'ffn_conv_w', 'ffn_conv_b', 'ln2_g', 'ln2_b']
    small_loc = {
        'b_ada': d_ada_me, 'rg_conv_w': d_rcw, 'rg_conv_b': d_rcb,
        'rg_w_a': diag(gw_bd[:, :d_rnn]), 'rg_b_a': d_ba, 'rg_w_x': diag(gw_bd[:, d_rnn:]), 'rg_b_x': d_bx,
        'rg_lambda': d_lam, 'dn_conv_w': d_dcw, 'dn_a_log': d_alog[:, :n_vh], 'dn_dt_bias': d_dtb[:, :n_vh],
        'dn_norm_w': jnp.sum(d_nwt.reshape(n_vh, LANES), axis=0), 'ln1_g': d_ln1g, 'ln1_b': d_ln1b,
        'ffn_conv_w': d_fcw, 'ffn_conv_b': d_fcb, 'ln2_g': d_ln2g, 'ln2_b': d_ln2b}
    small_list = [small_loc[n] for n in small_names]
    (small_all,) = all_gather("gather_small_grads", [_pack(small_list)])
    small_sum = _unpack(sum_parts("sum_small_grads", small_all), small_list)
    g_small = dict(zip(small_names, small_sum))
    d_ada_all = small_all.reshape(N_DEV, -1)[:, :6 * d]
    d_ada_cols = lax.dynamic_slice(d_ada_all, (0, me * ada_w), (N_DEV, ada_w))
    d_ada_pad = jnp.pad(d_ada_cols, ((0, LANES - N_DEV), (0, 0)))
    gw_ada = mm(c_pad, d_ada_pad, name="ada_bwd_w", ta=True, a_act="silu")

    res = {}
    big_parts = dict(zip(big, big_parts))
    big_parts['w_ada'] = gw_ada[None]
    for n in ['w_ada'] + big:
        res[n] = adamw("adamw_" + n, W[n], big_parts[n], M[n], V[n])
    for n in small_sh:
        w_ = W[n].shape[1]
        g_small[n] = lax.dynamic_slice(g_small[n], (0, me * w_), (W[n].shape[0], w_))
    for n in small_names:
        g_small[n] = g_small[n].reshape(W[n].shape)
    pk = lambda dct: _pack([dct[n] for n in small_names])
    s_g, s_d, s_m, s_v = adamw("adamw_small", pk(W), pk(g_small)[None], pk(M), pk(V))
    like = [W[n] for n in small_names]
    for n, g_, d_, m_, v_ in zip(small_names, _unpack(s_g, like), _unpack(s_d, like), _unpack(s_m, like), _unpack(s_v, like)):
        res[n] = (g_, d_, m_, v_)

    loss = lax.psum(loss_loc[0, 0], ("x", "y", "c"))
    outs = [loss, grad_x[None]]
    for j in range(4):
        outs += [res[n][j].reshape(loc[n].shape) for n in names]
    return tuple(outs)
```

```python
import functools
import math

import jax
import jax.numpy as jnp
from jax import lax
from jax.experimental import pallas as pl
from jax.experimental.pallas import tpu as pltpu

F32 = jnp.float32
BF16 = jnp.bfloat16
MXU_DTYPE = BF16
WIRE_DTYPE = BF16
DN_DTYPE = BF16
HI = lax.Precision.HIGHEST
MESH = pl.DeviceIdType.MESH

N_DEV = 8
LANES = 128
SUBLANES = 8
VMEM_LIMIT = 56 * 1024 * 1024

RG_C = 8.0
DN_CHUNK = 64
DN_HEAD_GROUP = 8
LN_EPS = 1e-5
RMS_EPS = 1e-6
L2_EPS = 1e-6
DEPTH = 1
DEEPNORM_ALPHA = (2 * DEPTH) ** 0.25
ADAM_LR = 0.001
ADAM_B1 = 0.9
ADAM_B2 = 0.999
ADAM_EPS = 1e-08
ADAM_WD = 0.01
ADAM_STEP = 10


def _tile(n, cap, unit=LANES):
    best = None
    for t in range(unit, min(n, cap) + 1, unit):
        if n % t == 0:
            best = t
    return best if best is not None else n


def _round_up(n, m):
    return (n + m - 1) // m * m


def mm(a, b, *, name, ta=False, tb=False, a_act=None, bias=None, out_dtype=F32,
       tm_cap=1024, tn_cap=512, tk_cap=1024):
    m, k = (a.shape[1], a.shape[0]) if ta else a.shape
    n = b.shape[0] if tb else b.shape[1]
    assert k == (b.shape[1] if tb else b.shape[0]), (a.shape, b.shape, ta, tb)
    tm, tn, tk = _tile(m, tm_cap), _tile(n, tn_cap), _tile(k, tk_cap)
    nk = k // tk
    dims = (((0 if ta else 1,), (1 if tb else 0,)), ((), ()))

    def body(*refs):
        refs = list(refs)
        acc_ref = refs.pop() if nk > 1 else None
        if bias is None:
            a_ref, b_ref, o_ref = refs
        else:
            a_ref, b_ref, bias_ref, o_ref = refs
        kk = pl.program_id(2)
        av = a_ref[...]
        if a_act == "silu":
            av = jax.nn.silu(av.astype(F32))
        prod = lax.dot_general(av.astype(MXU_DTYPE), b_ref[...].astype(MXU_DTYPE), dims, preferred_element_type=F32)

        def finish(r):
            if bias is not None:
                r = r + bias_ref[...]
            o_ref[...] = r.astype(o_ref.dtype)

        if nk == 1:
            finish(prod)
        else:
            @pl.when(kk == 0)
            def _():
                acc_ref[...] = prod

            @pl.when((kk > 0) & (kk < nk - 1))
            def _():
                acc_ref[...] += prod

            @pl.when(kk == nk - 1)
            def _():
                finish(acc_ref[...] + prod)

    a_spec = pl.BlockSpec((tk, tm), lambda i, j, q: (q, i)) if ta else pl.BlockSpec((tm, tk), lambda i, j, q: (i, q))
    b_spec = pl.BlockSpec((tn, tk), lambda i, j, q: (j, q)) if tb else pl.BlockSpec((tk, tn), lambda i, j, q: (q, j))
    in_specs, args = [a_spec, b_spec], [a, b]
    if bias is not None:
        in_specs.append(pl.BlockSpec((1, tn), lambda i, j, q: (0, j)))
        args.append(bias)
    return pl.pallas_call(
        body, name=name, grid=(m // tm, n // tn, nk),
        in_specs=in_specs, out_specs=pl.BlockSpec((tm, tn), lambda i, j, q: (i, j)),
        out_shape=jax.ShapeDtypeStruct((m, n), out_dtype),
        scratch_shapes=[pltpu.VMEM((tm, tn), F32)] if nk > 1 else [],
        compiler_params=pltpu.CompilerParams(dimension_semantics=("parallel", "parallel", "arbitrary"),
                                             vmem_limit_bytes=VMEM_LIMIT),
    )(*args)


class In:
    def __init__(self, arr, block, imap, acc=False, grad=True, parts=None, gshape=None, gimap=None):
        self.arr, self.block, self.imap, self.acc, self.grad, self.parts = arr, block, imap, acc, grad, parts
        self.gshape = arr.shape if gshape is None else gshape
        self.gimap = imap if gimap is None else gimap


class Out:
    def __init__(self, shape, block, imap, dtype=F32):
        self.shape, self.block, self.imap, self.dtype = shape, block, imap, dtype


def _load(in_refs, ins):
    vals = []
    for r, i in zip(in_refs, ins):
        if i.parts is None:
            vals.append(r[...])
        else:
            vals.extend(r[p] for p in i.parts)
    return vals


def _stage_params():
    return pltpu.CompilerParams(dimension_semantics=("parallel", "arbitrary"), vmem_limit_bytes=VMEM_LIMIT)


def stage_fwd(name, f, grid, ins, outs, carries=()):
    n_in, n_out, n_c = len(ins), len(outs), len(carries)

    def body(*refs):
        in_refs, out_refs = refs[:n_in], refs[n_in:n_in + n_out]
        hist_refs, c_refs = refs[n_in + n_out:n_in + n_out + n_c], refs[n_in + n_out + n_c:]
        if n_c:
            @pl.when(pl.program_id(1) == 0)
            def _():
                for c in c_refs:
                    c[...] = jnp.zeros_like(c)
        cin = [c[...] for c in c_refs]
        for h, c in zip(hist_refs, cin):
            h[...] = c
        o, cout = f(*_load(in_refs, ins), *cin)
        for r, v in zip(out_refs, o):
            r[...] = v.astype(r.dtype)
        for c, v in zip(c_refs, cout):
            c[...] = v

    hist_spec = lambda c: pl.BlockSpec((None, None) + tuple(c), lambda o, s: (o, s) + (0,) * len(c))
    res = pl.pallas_call(
        body, name=name, grid=grid,
        in_specs=[pl.BlockSpec(i.block, i.imap) for i in ins],
        out_specs=[pl.BlockSpec(o.block, o.imap) for o in outs] + [hist_spec(c) for c in carries],
        out_shape=[jax.ShapeDtypeStruct(o.shape, o.dtype) for o in outs]
        + [jax.ShapeDtypeStruct(tuple(grid) + tuple(c), F32) for c in carries],
        scratch_shapes=[pltpu.VMEM(tuple(c), F32) for c in carries],
        compiler_params=_stage_params(),
    )(*[i.arr for i in ins])
    return list(res[:n_out]), list(res[n_out:])


def stage_bwd(name, f, grid, ins, outs, cots, carries=(), hists=(), add_to=None, gdtypes=None):
    n_in, n_out, n_c = len(ins), len(outs), len(carries)
    ns = grid[1]
    add_to = add_to or {}
    gdtypes = gdtypes or {}
    add_idx = sorted(add_to)
    g_idx = [k for k, i in enumerate(ins) if i.grad]
    cots = [c if isinstance(c, (tuple, list)) else (c,) for c in cots]
    n_cot = [len(c) for c in cots]
    rev = lambda imap: (lambda o, s: imap(o, ns - 1 - s))

    def body(*refs):
        p = 0
        in_refs = refs[p:p + n_in]; p += n_in
        cot_refs = []
        for cnt in n_cot:
            cot_refs.append(refs[p:p + cnt]); p += cnt
        hist_refs = refs[p:p + n_c]; p += n_c
        add_refs = refs[p:p + len(add_idx)]; p += len(add_idx)
        g_refs = refs[p:p + len(g_idx)]; p += len(g_idx)
        dc_refs = refs[p:]
        first = pl.program_id(1) == 0
        if n_c:
            @pl.when(first)
            def _():
                for c in dc_refs:
                    c[...] = jnp.zeros_like(c)
        vals = _load(in_refs, ins)
        cin = [h[...] for h in hist_refs]
        (o, cout), vjp = jax.vjp(lambda *a: f(*a), *vals, *cin)
        cot_o = []
        for crs, v in zip(cot_refs, o):
            c = crs[0][...].astype(v.dtype)
            for extra in crs[1:]:
                c = c + extra[...].astype(v.dtype)
            cot_o.append(c)
        cot_c = tuple(c[...] for c in dc_refs)
        grads = vjp((tuple(cot_o), cot_c))
        pos, per_in = 0, []
        for i in ins:
            cnt = 1 if i.parts is None else len(i.parts)
            per_in.append(grads[pos:pos + cnt])
            pos += cnt
        dcin = grads[pos:]
        for gr, k in zip(g_refs, g_idx):
            i, gs = ins[k], per_in[k]
            if i.acc:
                @pl.when(first)
                def _(gr=gr):
                    gr[...] = jnp.zeros_like(gr)
                if i.parts is None:
                    gr[...] += gs[0].astype(gr.dtype)
                else:
                    for pt, g in zip(i.parts, gs):
                        gr[pt] += g.astype(gr.dtype)
            else:
                g = gs[0]
                if k in add_to:
                    g = g + add_refs[add_idx.index(k)][...].astype(g.dtype)
                gr[...] = g.astype(gr.dtype)
        for c, v in zip(dc_refs, dcin):
            c[...] = v

    in_specs = [pl.BlockSpec(i.block, rev(i.imap)) for i in ins]
    for o_, cnt in zip(outs, n_cot):
        in_specs += [pl.BlockSpec(o_.block, rev(o_.imap))] * cnt
    in_specs += [pl.BlockSpec((None, None) + tuple(c), (lambda c: (lambda o, s: (o, ns - 1 - s) + (0,) * len(c)))(c))
                 for c in carries]
    in_specs += [pl.BlockSpec(ins[k].block, rev(ins[k].gimap)) for k in add_idx]
    out_specs, out_shape = [], []
    for k in g_idx:
        i = ins[k]
        if i.acc:
            out_specs.append(pl.BlockSpec(i.block, (lambda im: (lambda o, s: im(o, 0)))(i.imap)))
        else:
            out_specs.append(pl.BlockSpec(i.block, rev(i.gimap)))
        out_shape.append(jax.ShapeDtypeStruct(i.gshape, gdtypes.get(k, F32)))
    res = pl.pallas_call(
        body, name=name, grid=grid, in_specs=in_specs, out_specs=out_specs, out_shape=out_shape,
        scratch_shapes=[pltpu.VMEM(tuple(c), F32) for c in carries],
        compiler_params=_stage_params(),
    )(*[i.arr for i in ins], *[a for c in cots for a in c], *hists, *[add_to[k] for k in add_idx])
    return list(res)


def _iota_rows(shape):
    return lax.broadcasted_iota(jnp.int32, shape, 0)


@functools.partial(jax.custom_vjp, nondiff_argnums=(1,))
def _roll_rows(x, s):
    return pltpu.roll(x, s % x.shape[0], 0)


def _roll_rows_fwd(x, s):
    return _roll_rows(x, s), None


def _roll_rows_bwd(s, _, g):
    return (_roll_rows(g, -s),)


_roll_rows.defvjp(_roll_rows_fwd, _roll_rows_bwd)


@jax.custom_vjp
def _drop_head(xx):
    return xx[SUBLANES:]


def _drop_head_fwd(xx):
    return xx[SUBLANES:], None


def _drop_head_bwd(_, g):
    return (jnp.concatenate([jnp.zeros((SUBLANES, g.shape[1]), g.dtype), g], axis=0),)


_drop_head.defvjp(_drop_head_fwd, _drop_head_bwd)


@jax.custom_vjp
def _last_rows(x):
    return x[x.shape[0] - SUBLANES:]


def _last_rows_fwd(x):
    return x[x.shape[0] - SUBLANES:], x.shape[0]


def _last_rows_bwd(n, g):
    return (jnp.concatenate([jnp.zeros((n - SUBLANES, g.shape[1]), g.dtype), g], axis=0),)


_last_rows.defvjp(_last_rows_fwd, _last_rows_bwd)


def _last_row(x):
    n = x.shape[0]
    return jnp.sum(jnp.where(_iota_rows(x.shape) == n - 1, x, 0.0), axis=0, keepdims=True)


def _scan_steps(n):
    s = 1
    while s < n:
        yield s
        s *= 2


def _block_scan_impl(a, u, h0):
    n = a.shape[0]
    row = _iota_rows(a.shape)
    for s in _scan_steps(n):
        keep = row >= s
        a_s = jnp.where(keep, pltpu.roll(a, s, 0), 1.0)
        u_s = jnp.where(keep, pltpu.roll(u, s, 0), 0.0)
        u = u + a * u_s
        a = a * a_s
    return u + a * h0


@jax.custom_vjp
def _block_scan(a, u, h0):
    return _block_scan_impl(a, u, h0)


def _block_scan_fwd(a, u, h0):
    h = _block_scan_impl(a, u, h0)
    return h, (a, h, h0)


def _block_scan_bwd(res, dh):
    a, h, h0 = res
    n = a.shape[0]
    row = _iota_rows(a.shape)
    b = jnp.where(row < n - 1, pltpu.roll(a, n - 1, 0), 0.0)
    lam = dh
    for s in _scan_steps(n):
        keep = row < n - s
        b_s = jnp.where(keep, pltpu.roll(b, n - s, 0), 1.0)
        l_s = jnp.where(keep, pltpu.roll(lam, n - s, 0), 0.0)
        lam = lam + b * l_s
        b = b * b_s
    h_prev = jnp.where(row >= 1, pltpu.roll(h, 1, 0), jnp.broadcast_to(h0, h.shape))
    d_h0 = jnp.sum(jnp.where(row == 0, a * lam, 0.0), axis=0, keepdims=True)
    return lam * h_prev, lam, d_h0


_block_scan.defvjp(_block_scan_fwd, _block_scan_bwd)


def _dot_hi(a, b, dims=(((1,), (0,)), ((), ()))):
    return lax.dot_general(a, b, dims, precision=HI, preferred_element_type=F32)


_NN, _NT, _TN = "nn", "nt", "tn"
_CONTRACT = {_NN: (1, 0), _NT: (1, 1), _TN: (0, 0)}


def _raw_dot(a, b, kind):
    ca, cb = _CONTRACT[kind]
    lead = a.ndim - 2
    dims = (((ca + lead,), (cb + lead,)), (tuple(range(lead)), tuple(range(lead))))
    return lax.dot_general(a.astype(DN_DTYPE), b.astype(DN_DTYPE), dims, preferred_element_type=F32)


@jax.custom_vjp
def _nn(a, b):
    return _raw_dot(a, b, _NN)


_nn.defvjp(lambda a, b: (_raw_dot(a, b, _NN), (a, b)),
           lambda r, g: (_raw_dot(g, r[1], _NT), _raw_dot(r[0], g, _TN)))


@jax.custom_vjp
def _nt(a, b):
    return _raw_dot(a, b, _NT)


_nt.defvjp(lambda a, b: (_raw_dot(a, b, _NT), (a, b)),
           lambda r, g: (_raw_dot(g, r[1], _NN), _raw_dot(g, r[0], _TN)))


@jax.custom_vjp
def _tn(a, b):
    return _raw_dot(a, b, _TN)


_tn.defvjp(lambda a, b: (_raw_dot(a, b, _TN), (a, b)),
           lambda r, g: (_raw_dot(r[1], g, _NT), _raw_dot(r[0], g, _NN)))


def _neumann_inverse(a):
    n = a.shape[-1]
    eye = (lax.broadcasted_iota(jnp.int32, (n, n), 0) == lax.broadcasted_iota(jnp.int32, (n, n), 1)).astype(F32)
    p = _raw_dot(a, a, _NN)
    e = p
    for _ in range(int(math.log2(n)) - 2):
        p = _raw_dot(p, p, _NN)
        e = e + p + _raw_dot(e, p, _NN)
    return eye - a + e - _raw_dot(a, e, _NN)


@jax.custom_vjp
def _unit_lower_inverse(a):
    return _neumann_inverse(a)


def _unit_lower_inverse_fwd(a):
    x = _neumann_inverse(a)
    return x, x


def _unit_lower_inverse_bwd(x, g):
    return (-_raw_dot(_raw_dot(x, g, _TN), x, _NT),)


_unit_lower_inverse.defvjp(_unit_lower_inverse_fwd, _unit_lower_inverse_bwd)


def _softplus(x):
    return jnp.maximum(x, 0.0) + jnp.log1p(jnp.exp(-jnp.abs(x)))


def _neg_expm1(x):
    series = -x * (1.0 + x * (0.5 + x * (1.0 / 6.0 + x * (1.0 / 24.0 + x * (1.0 / 120.0)))))
    return jnp.where(x > -0.03, series, 1.0 - jnp.exp(x))


def f_modulate(x, sc, sh):
    return (x * (1.0 + sc) + sh,), ()


def f_deepnorm(x, y, gt, g, b):
    v = DEEPNORM_ALPHA * x + (1.0 + gt) * y
    mu = jnp.mean(v, axis=-1, keepdims=True)
    vc = v - mu
    var = jnp.mean(vc * vc, axis=-1, keepdims=True)
    return (vc * lax.rsqrt(var + LN_EPS) * g + b,), ()


def _causal_conv(x, prev, ws):
    xx = jnp.concatenate([prev, x], axis=0)
    k = len(ws)
    y = ws[k - 1] * x
    for j in range(k - 1):
        y = y + ws[j] * _drop_head(_roll_rows(xx, k - 1 - j))
    return y


def f_rg_conv(x, w0, w1, w2, w3, b, prev):
    return (_causal_conv(x, prev, (w0, w1, w2, w3)) + b,), (_last_rows(x),)


def f_dn_conv(x, w0, w1, w2, w3, prev):
    return (jax.nn.silu(_causal_conv(x, prev, (w0, w1, w2, w3))),), (_last_rows(x),)


def f_ffn_act(gp, up, w0, w1, w2, b, prev):
    return (jax.nn.gelu(_causal_conv(gp, prev, (w0, w1, w2)) + b) * up,), (_last_rows(gp),)


def f_rglru(xc, pre_r, pre_i, gr, b_a, b_x, lam, h0):
    gate_r = jax.nn.sigmoid(pre_r + b_a)
    gate_i = jax.nn.sigmoid(pre_i + b_x)
    log_a = -RG_C * gate_r * _softplus(-lam)
    a = jnp.exp(log_a)
    mult = jnp.sqrt(_neg_expm1(2.0 * log_a))
    h = _block_scan(a, mult * gate_i * xc, h0)
    return (h * jax.nn.gelu(gr),), (_last_row(h),)


def f_l2norm(scale, x):
    return (x * lax.rsqrt(jnp.sum(x * x, axis=-1, keepdims=True) + L2_EPS) * scale,), ()


def f_dn_gates(a_in, b_in, a_log, dt_bias):
    g = -jnp.exp(a_log) * _softplus(a_in + dt_bias)
    n = g.shape[0]
    shift = int(math.log2(DN_CHUNK))
    ri = lax.broadcasted_iota(jnp.int32, (n, n), 0)
    ci = lax.broadcasted_iota(jnp.int32, (n, n), 1)
    tri = ((lax.shift_right_logical(ri, shift) == lax.shift_right_logical(ci, shift)) & (ri >= ci)).astype(F32)
    return (_dot_hi(tri, g), jax.nn.sigmoid(b_in)), ()


def f_dn_out(o, z, nw):
    r = lax.rsqrt(jnp.mean(o * o, axis=-1, keepdims=True) + RMS_EPS)
    return (o * r * nw * jax.nn.silu(z),), ()


def f_merge(ga, gb, ya, yb):
    return (jax.nn.sigmoid(ga) * ya + jax.nn.sigmoid(gb) * yb,), ()


def _delta_intra(q, k, v, g_i, g_j, beta):
    c = q.shape[-2]
    ri = lax.broadcasted_iota(jnp.int32, (c, c), 0)
    ci = lax.broadcasted_iota(jnp.int32, (c, c), 1)
    decay = jnp.exp(jnp.where(ri >= ci, g_i - g_j, -jnp.inf))
    g_last = jnp.sum(jnp.where(_iota_rows((c, 1)) == c - 1, g_i, 0.0), axis=-2, keepdims=True)
    exp_g = jnp.exp(g_i)
    kb = k * beta
    t_inv = _unit_lower_inverse(jnp.where(ri > ci, _nt(kb, k) * decay, 0.0))
    u = _nn(t_inv, v * beta)
    w = _nn(t_inv, kb * exp_g)
    return u, w, _nt(q, k) * decay, q * exp_g, k * jnp.exp(g_last - g_i)


def _delta_inter(u, w, qk, q_dec, k_dec, g_last, state):
    v_new = u - _nn(w, state)
    o = _nn(q_dec, state) + _nn(qk, v_new)
    return o, jnp.exp(g_last) * state + _tn(k_dec, v_new)


def _chunk_spec(width, nc=None, col=0):
    if nc is None:
        return pl.BlockSpec((DN_CHUNK, width), lambda s: (s, col))
    return pl.BlockSpec((DN_CHUNK, width), lambda s: (nc - 1 - s, col))


def _delta_params(sem):
    return pltpu.CompilerParams(dimension_semantics=(sem,), vmem_limit_bytes=VMEM_LIMIT)


def _head(ref, h, width=LANES):
    return ref[:, h * LANES:h * LANES + width]


def _head_groups(n_vh):
    hb = min(DN_HEAD_GROUP, n_vh)
    return [range(h0, h0 + hb) for h0 in range(0, n_vh, hb)]


def _stack(hs, f):
    return jnp.stack([f(h) for h in hs])


def _intra_operands(hs, rep, q_ref, k_ref, v_ref, g_ref, gt_ref, b_ref):
    return (_stack(hs, lambda h: _head(q_ref, h // rep)), _stack(hs, lambda h: _head(k_ref, h // rep)),
            _stack(hs, lambda h: _head(v_ref, h)), _stack(hs, lambda h: g_ref[:, h:h + 1]),
            _stack(hs, lambda h: gt_ref[h:h + 1, :]), _stack(hs, lambda h: b_ref[:, h:h + 1]))


def _inter_operands(hs, u_ref, w_ref, qk_ref, qd_ref, kd_ref, g_ref):
    f32 = lambda ref, width=LANES: _stack(hs, lambda h: _head(ref, h, width).astype(F32))
    return (f32(u_ref), f32(w_ref), f32(qk_ref, DN_CHUNK), f32(qd_ref), f32(kd_ref),
            _stack(hs, lambda h: g_ref[DN_CHUNK - 1:DN_CHUNK, h:h + 1]))


def delta_intra_fwd(qn, kn, qkv, v_blk, big_g, big_gt, beta, n_vh):
    t, qk_w = qn.shape
    vdim = n_vh * LANES
    rep = vdim // qk_w
    nc = t // DN_CHUNK

    def body(q_ref, k_ref, v_ref, g_ref, gt_ref, b_ref, u_ref, w_ref, qk_ref, qd_ref, kd_ref):
        for hs in _head_groups(n_vh):
            u, w, qk, qd, kd = _delta_intra(*_intra_operands(hs, rep, q_ref, k_ref, v_ref, g_ref, gt_ref, b_ref))
            for i, h in enumerate(hs):
                sl = slice(h * LANES, (h + 1) * LANES)
                u_ref[:, sl] = u[i]
                w_ref[:, sl] = w[i].astype(w_ref.dtype)
                qk_ref[:, sl] = jnp.concatenate([qk[i], jnp.zeros_like(qk[i])], axis=1).astype(qk_ref.dtype)
                qd_ref[:, sl] = qd[i].astype(qd_ref.dtype)
                kd_ref[:, sl] = kd[i].astype(kd_ref.dtype)

    return pl.pallas_call(
        body, name="delta_intra_fwd", grid=(nc,),
        in_specs=[_chunk_spec(qk_w), _chunk_spec(qk_w), _chunk_spec(vdim, col=v_blk), _chunk_spec(LANES),
                  pl.BlockSpec((None, n_vh, DN_CHUNK), lambda s: (s, 0, 0)), _chunk_spec(LANES)],
        out_specs=[_chunk_spec(vdim)] * 5,
        out_shape=[jax.ShapeDtypeStruct((t, vdim), F32)] + [jax.ShapeDtypeStruct((t, vdim), DN_DTYPE)] * 4,
        compiler_params=_delta_params("parallel"),
    )(qn, kn, qkv, big_g, big_gt, beta)


def delta_inter_fwd(u, w, qk, q_dec, k_dec, big_g, n_vh):
    t, vdim = u.shape
    nc = t // DN_CHUNK

    def body(u_ref, w_ref, qk_ref, qd_ref, kd_ref, g_ref, o_ref, hist_ref, s_ref):
        @pl.when(pl.program_id(0) == 0)
        def _():
            s_ref[...] = jnp.zeros_like(s_ref)
        for hs in _head_groups(n_vh):
            grp = slice(hs[0], hs[-1] + 1)
            st = s_ref[grp]
            hist_ref[grp] = st
            o, ns = _delta_inter(*_inter_operands(hs, u_ref, w_ref, qk_ref, qd_ref, kd_ref, g_ref), st)
            for i, h in enumerate(hs):
                o_ref[:, h * LANES:(h + 1) * LANES] = o[i]
            s_ref[grp] = ns

    return pl.pallas_call(
        body, name="delta_inter_fwd", grid=(nc,),
        in_specs=[_chunk_spec(vdim)] * 5 + [_chunk_spec(LANES)],
        out_specs=[_chunk_spec(vdim), pl.BlockSpec((None, n_vh, LANES, LANES), lambda s: (s, 0, 0, 0))],
        out_shape=[jax.ShapeDtypeStruct((t, vdim), F32), jax.ShapeDtypeStruct((nc, n_vh, LANES, LANES), F32)],
        scratch_shapes=[pltpu.VMEM((n_vh, LANES, LANES), F32)],
        compiler_params=_delta_params("arbitrary"),
    )(u, w, qk, q_dec, k_dec, big_g)


def delta_inter_bwd(u, w, qk, q_dec, k_dec, big_g, hist, d_o, n_vh):
    t, vdim = u.shape
    nc = t // DN_CHUNK

    def body(u_ref, w_ref, qk_ref, qd_ref, kd_ref, g_ref, hist_ref, do_ref,
             du_ref, dw_ref, dqk_ref, dqd_ref, dkd_ref, dg_ref, ds_ref):
        @pl.when(pl.program_id(0) == 0)
        def _():
            ds_ref[...] = jnp.zeros_like(ds_ref)
        lane = lax.broadcasted_iota(jnp.int32, (1, LANES), 1)
        dgl_all = jnp.zeros((1, LANES), F32)
        for hs in _head_groups(n_vh):
            grp = slice(hs[0], hs[-1] + 1)
            prim = _inter_operands(hs, u_ref, w_ref, qk_ref, qd_ref, kd_ref, g_ref) + (hist_ref[grp],)
            _, vjp = jax.vjp(_delta_inter, *prim)
            du, dw, dqk, dqd, dkd, dgl, dst = vjp((_stack(hs, lambda h: _head(do_ref, h)), ds_ref[grp]))
            ds_ref[grp] = dst
            for i, h in enumerate(hs):
                sl = slice(h * LANES, (h + 1) * LANES)
                du_ref[:, sl] = du[i]
                dw_ref[:, sl] = dw[i]
                dqk_ref[:, sl] = jnp.concatenate([dqk[i], jnp.zeros_like(dqk[i])], axis=1)
                dqd_ref[:, sl] = dqd[i]
                dkd_ref[:, sl] = dkd[i]
                dgl_all = dgl_all + dgl[i] * (lane == h).astype(F32)
        last = _iota_rows((DN_CHUNK, LANES)) == DN_CHUNK - 1
        dg_ref[...] = jnp.where(last, jnp.broadcast_to(dgl_all, (DN_CHUNK, LANES)), 0.0)

    rv = lambda w_: _chunk_spec(w_, nc)
    return pl.pallas_call(
        body, name="delta_inter_bwd", grid=(nc,),
        in_specs=[rv(vdim)] * 5 + [rv(LANES), pl.BlockSpec((None, n_vh, LANES, LANES), lambda s: (nc - 1 - s, 0, 0, 0)),
                                   rv(vdim)],
        out_specs=[rv(vdim)] * 5 + [rv(LANES)],
        out_shape=[jax.ShapeDtypeStruct((t, vdim), F32)] * 5 + [jax.ShapeDtypeStruct((t, LANES), F32)],
        scratch_shapes=[pltpu.VMEM((n_vh, LANES, LANES), F32)],
        compiler_params=_delta_params("arbitrary"),
    )(u, w, qk, q_dec, k_dec, big_g, hist, d_o)


def delta_intra_bwd(qn, kn, qkv, v_blk, big_g, big_gt, beta, cots, n_vh):
    t, qk_w = qn.shape
    vdim = n_vh * LANES
    rep = vdim // qk_w
    nc = t // DN_CHUNK

    def body(q_ref, k_ref, v_ref, g_ref, gt_ref, b_ref, du_ref, dw_ref, dqk_ref, dqd_ref, dkd_ref,
             dq_ref, dk_ref, dv_ref, dg_ref, dgt_ref, db_ref):
        lane = lax.broadcasted_iota(jnp.int32, (1, LANES), 1)
        dg_all = jnp.zeros((DN_CHUNK, LANES), F32)
        db_all = jnp.zeros((DN_CHUNK, LANES), F32)
        dq_acc, dk_acc = None, None
        for hs in _head_groups(n_vh):
            _, vjp = jax.vjp(_delta_intra, *_intra_operands(hs, rep, q_ref, k_ref, v_ref, g_ref, gt_ref, b_ref))
            cot = lambda ref, width=LANES: _stack(hs, lambda h: _head(ref, h, width))
            dq, dk, dv, dgi, dgj, db = vjp((cot(du_ref), cot(dw_ref), cot(dqk_ref, DN_CHUNK), cot(dqd_ref), cot(dkd_ref)))
            for i, h in enumerate(hs):
                j = h // rep
                dv_ref[:, h * LANES:(h + 1) * LANES] = dv[i]
                dgt_ref[h:h + 1, :] = dgj[i]
                onehot = (lane == h).astype(F32)
                dg_all = dg_all + dgi[i] * onehot
                db_all = db_all + db[i] * onehot
                dq_acc = dq[i] if h % rep == 0 else dq_acc + dq[i]
                dk_acc = dk[i] if h % rep == 0 else dk_acc + dk[i]
                if h % rep == rep - 1:
                    dq_ref[:, j * LANES:(j + 1) * LANES] = dq_acc
                    dk_ref[:, j * LANES:(j + 1) * LANES] = dk_acc
        dg_ref[...] = dg_all
        db_ref[...] = db_all

    gt_spec = pl.BlockSpec((None, n_vh, DN_CHUNK), lambda s: (s, 0, 0))
    return pl.pallas_call(
        body, name="delta_intra_bwd", grid=(nc,),
        in_specs=[_chunk_spec(qk_w), _chunk_spec(qk_w), _chunk_spec(vdim, col=v_blk), _chunk_spec(LANES), gt_spec,
                  _chunk_spec(LANES)] + [_chunk_spec(vdim)] * 5,
        out_specs=[_chunk_spec(qk_w), _chunk_spec(qk_w), _chunk_spec(vdim), _chunk_spec(LANES), gt_spec,
                   _chunk_spec(LANES)],
        out_shape=[jax.ShapeDtypeStruct((t, qk_w), F32), jax.ShapeDtypeStruct((t, qk_w), F32),
                   jax.ShapeDtypeStruct((t, vdim), F32), jax.ShapeDtypeStruct((t, LANES), F32),
                   jax.ShapeDtypeStruct((nc, n_vh, DN_CHUNK), F32), jax.ShapeDtypeStruct((t, LANES), F32)],
        compiler_params=_delta_params("parallel"),
    )(qn, kn, qkv, big_g, big_gt, beta, *cots)


def loss_head(y, target, tb):
    t, d = y.shape

    def body(y_ref, t_ref, dy_ref, loss_ref):
        @pl.when(pl.program_id(0) == 0)
        def _():
            loss_ref[...] = jnp.zeros_like(loss_ref)
        err = y_ref[...] - t_ref[...]
        dy_ref[...] = err * (1.0 / d)
        loss_ref[...] += 0.5 * jnp.sum(jnp.sum(err * err, axis=1, keepdims=True), axis=0, keepdims=True) * (1.0 / d)

    return pl.pallas_call(
        body, name="loss_head", grid=(t // tb,),
        in_specs=[pl.BlockSpec((tb, d), lambda s: (s, 0))] * 2,
        out_specs=[pl.BlockSpec((tb, d), lambda s: (s, 0)), pl.BlockSpec((1, 1), lambda s: (0, 0))],
        out_shape=[jax.ShapeDtypeStruct((t, d), F32), jax.ShapeDtypeStruct((1, 1), F32)],
        compiler_params=pltpu.CompilerParams(dimension_semantics=("arbitrary",), vmem_limit_bytes=VMEM_LIMIT),
    )(y, target)


def _exchange(name, arrs, scatter):
    n = len(arrs)

    def body(*refs):
        in_refs, out_refs = refs[:n], refs[n:2 * n]
        send_sems, recv_sems, local_sems = refs[2 * n:]
        x, y, c = lax.axis_index("x"), lax.axis_index("y"), lax.axis_index("c")
        me = 4 * x + 2 * y + c

        def peer(k):
            return (x ^ ((k >> 2) & 1), y ^ ((k >> 1) & 1), c ^ (k & 1))

        def lin(p):
            return 4 * p[0] + 2 * p[1] + p[2]

        local = []
        for i in range(n):
            src = in_refs[i].at[me] if scatter else in_refs[i]
            cp = pltpu.make_async_copy(src, out_refs[i].at[me], local_sems.at[i])
            cp.start()
            local.append(cp)
        sends = []
        for k in range(1, N_DEV):
            p = peer(k)
            for i in range(n):
                src = in_refs[i].at[lin(p)] if scatter else in_refs[i]
                cp = pltpu.make_async_remote_copy(src_ref=src, dst_ref=out_refs[i].at[me],
                                                  send_sem=send_sems.at[i, k - 1], recv_sem=recv_sems.at[i, k - 1],
                                                  device_id=p, device_id_type=MESH)
                cp.start()
                sends.append(cp)
        for k in range(1, N_DEV):
            p = peer(k)
            for i in range(n):
                src = in_refs[i].at[me] if scatter else in_refs[i]
                pltpu.make_async_remote_copy(src_ref=src, dst_ref=out_refs[i].at[lin(p)],
                                             send_sem=send_sems.at[i, k - 1], recv_sem=recv_sems.at[i, k - 1],
                                             device_id=p, device_id_type=MESH).wait_recv()
        for cp in sends:
            cp.wait_send()
        for cp in local:
            cp.wait()

    hbm = pl.BlockSpec(memory_space=pl.ANY)
    res = pl.pallas_call(
        body, name=name,
        in_specs=[hbm] * n, out_specs=[hbm] * n,
        out_shape=[jax.ShapeDtypeStruct(a.shape if scatter else (N_DEV,) + a.shape, a.dtype) for a in arrs],
        scratch_shapes=[pltpu.SemaphoreType.DMA((n, N_DEV - 1)), pltpu.SemaphoreType.DMA((n, N_DEV - 1)),
                        pltpu.SemaphoreType.DMA((n,))],
        compiler_params=pltpu.CompilerParams(has_side_effects=True),
    )(*arrs)
    return list(res)


def all_gather(name, arrs):
    return _exchange(name, arrs, scatter=False)


def all_to_all(name, arrs):
    return _exchange(name, arrs, scatter=True)


def _adamw_math(w, g, m, v):
    m = ADAM_B1 * m + (1.0 - ADAM_B1) * g
    v = ADAM_B2 * v + (1.0 - ADAM_B2) * (g * g)
    m_hat = m / (1.0 - ADAM_B1 ** ADAM_STEP)
    v_hat = v / (1.0 - ADAM_B2 ** ADAM_STEP)
    delta = -ADAM_LR * (m_hat / (jnp.sqrt(v_hat) + ADAM_EPS) + ADAM_WD * w)
    return delta, m, v


def adamw(name, w, parts, m, v, rows_cap=128):
    r, c = w.shape
    np_ = parts.shape[0]
    tr = _tile(r, rows_cap, SUBLANES * (4 // parts.dtype.itemsize))

    def body(w_ref, p_ref, m_ref, v_ref, g_ref, d_ref, nm_ref, nv_ref):
        g = p_ref[0].astype(F32)
        for k in range(1, np_):
            g = g + p_ref[k].astype(F32)
        delta, nm, nv = _adamw_math(w_ref[...], g, m_ref[...], v_ref[...])
        g_ref[...] = g
        d_ref[...] = delta
        nm_ref[...] = nm
        nv_ref[...] = nv

    spec = pl.BlockSpec((tr, c), lambda i: (i, 0))
    return pl.pallas_call(
        body, name=name, grid=(r // tr,),
        in_specs=[spec, pl.BlockSpec((np_, tr, c), lambda i: (0, i, 0)), spec, spec],
        out_specs=[spec] * 4, out_shape=[jax.ShapeDtypeStruct((r, c), F32)] * 4,
        compiler_params=pltpu.CompilerParams(dimension_semantics=("parallel",), vmem_limit_bytes=VMEM_LIMIT),
    )(w, parts, m, v)


def sum_parts(name, parts, rows_cap=256):
    np_, r, c = parts.shape
    tr = _tile(r, rows_cap, SUBLANES)

    def body(p_ref, o_ref):
        g = p_ref[0].astype(F32)
        for k in range(1, np_):
            g = g + p_ref[k].astype(F32)
        o_ref[...] = g

    return pl.pallas_call(
        body, name=name, grid=(r // tr,),
        in_specs=[pl.BlockSpec((np_, tr, c), lambda i: (0, i, 0))],
        out_specs=pl.BlockSpec((tr, c), lambda i: (i, 0)),
        out_shape=jax.ShapeDtypeStruct((r, c), F32),
        compiler_params=pltpu.CompilerParams(dimension_semantics=("parallel",), vmem_limit_bytes=VMEM_LIMIT),
    )(parts)


def _pack(arrs):
    flat = jnp.concatenate([a.reshape(-1).astype(F32) for a in arrs])
    n = flat.shape[0]
    return jnp.pad(flat, (0, _round_up(n, LANES * SUBLANES) - n)).reshape(-1, LANES)


def _unpack(packed, like):
    flat, out, pos = packed.reshape(-1), [], 0
    for a in like:
        out.append(flat[pos:pos + a.size].reshape(a.shape))
        pos += a.size
    return out


def kernel(x, c, w_ada, b_ada, w_in, rg_conv_w, rg_conv_b, rg_w_a, rg_b_a, rg_w_x, rg_b_x, rg_lambda, dn_conv_w, dn_a_log, dn_dt_bias, dn_norm_w, w_proj_a, w_proj_b, w_out, ln1_g, ln1_b, ffn_w_gate, ffn_w_up, ffn_conv_w, ffn_conv_b, ffn_w_down, ln2_g, ln2_b, loss_target, m_w_ada, m_b_ada, m_w_in, m_rg_conv_w, m_rg_conv_b, m_rg_w_a, m_rg_b_a, m_rg_w_x, m_rg_b_x, m_rg_lambda, m_dn_conv_w, m_dn_a_log, m_dn_dt_bias, m_dn_norm_w, m_w_proj_a, m_w_proj_b, m_w_out, m_ln1_g, m_ln1_b, m_ffn_w_gate, m_ffn_w_up, m_ffn_conv_w, m_ffn_conv_b, m_ffn_w_down, m_ln2_g, m_ln2_b, v_w_ada, v_b_ada, v_w_in, v_rg_conv_w, v_rg_conv_b, v_rg_w_a, v_rg_b_a, v_rg_w_x, v_rg_b_x, v_rg_lambda, v_dn_conv_w, v_dn_a_log, v_dn_dt_bias, v_dn_norm_w, v_w_proj_a, v_w_proj_b, v_w_out, v_ln1_g, v_ln1_b, v_ffn_w_gate, v_ffn_w_up, v_ffn_conv_w, v_ffn_conv_b, v_ffn_w_down, v_ln2_g, v_ln2_b):
    names = ['w_ada', 'b_ada', 'w_in', 'rg_conv_w', 'rg_conv_b', 'rg_w_a', 'rg_b_a', 'rg_w_x', 'rg_b_x', 'rg_lambda',
             'dn_conv_w', 'dn_a_log', 'dn_dt_bias', 'dn_norm_w', 'w_proj_a', 'w_proj_b', 'w_out', 'ln1_g', 'ln1_b',
             'ffn_w_gate', 'ffn_w_up', 'ffn_conv_w', 'ffn_conv_b', 'ffn_w_down', 'ln2_g', 'ln2_b']
    loc = locals()
    W = {n: loc[n][0] for n in names}
    M = {n: loc['m_' + n][0] for n in names}
    V = {n: loc['v_' + n][0] for n in names}

    me = 4 * lax.axis_index("x") + 2 * lax.axis_index("y") + lax.axis_index("c")
    xs, tgt = x[0], loss_target[0]
    t, d = xs.shape
    d_rnn = W['rg_conv_b'].shape[0]
    n_blk = W['rg_w_a'].shape[0]
    n_vh = W['dn_a_log'].shape[0]
    assert W['dn_norm_w'].shape[0] == LANES
    vdim = n_vh * LANES
    d_ff = W['ffn_conv_b'].shape[0]
    d_in = W['w_in'].shape[1] * N_DEV
    qk = (d_in - 2 * d_rnn - 2 * vdim - 2 * n_vh - 2 * d) // 2
    assert vdim == 2 * qk and qk % LANES == 0 and n_vh <= LANES
    splits = (d_rnn, d_rnn, qk, qk, vdim, vdim, n_vh, n_vh, d, d)
    offs = [0]
    for s_ in splits:
        offs.append(offs[-1] + s_)

    tb = _tile(t, 256, SUBLANES)

    big = ['w_in', 'w_proj_a', 'w_proj_b', 'w_out', 'ffn_w_gate', 'ffn_w_up', 'ffn_w_down']
    small_sh = ['rg_conv_w', 'dn_conv_w', 'ffn_conv_w']
    gathered = all_gather("gather_weights", [W[n].astype(WIRE_DTYPE) for n in big] + [W[n] for n in small_sh])
    g_in, g_pa, g_pb, g_out, g_fg, g_fu, g_fd, g_rcw, g_dcw, g_fcw = gathered
    cols = lambda g: jnp.transpose(g, (1, 0, 2)).reshape(g.shape[1], -1)
    rows = lambda g: g.reshape(-1, g.shape[2])
    w_in_f = cols(g_in)
    padl = lambda a: jnp.pad(a, ((0, 0), (0, LANES - a.shape[1])))
    groups = [w_in_f[:, offs[i]:offs[i + 1]] for i in range(10)]
    groups[6], groups[7] = padl(groups[6]), padl(groups[7])
    go = [0]
    for g_ in groups:
        go.append(go[-1] + g_.shape[1])
    n_pad = _round_up(go[-1], 512)
    wp = jnp.pad(jnp.concatenate(groups, axis=1), ((0, 0), (0, n_pad - go[-1])))
    o_xr, o_gr, o_q, o_k, o_v, o_z, o_a, o_b, o_ga, o_gb = go[:10]
    w_pa, w_pb, w_o, w_fd = rows(g_pa), rows(g_pb), rows(g_out), rows(g_fd)
    w_gu = jnp.concatenate([cols(g_fg), cols(g_fu)], axis=1)
    rcw, dcw, fcw = cols(g_rcw), cols(g_dcw), cols(g_fcw)
    eye_b = jnp.eye(n_blk, dtype=F32)
    bd = lambda w: (w[:, :, None, :] * eye_b[:, None, :, None]).reshape(d_rnn, d_rnn)
    w_bd = jnp.concatenate([bd(W['rg_w_a']), bd(W['rg_w_x'])], axis=1)
    row1 = lambda a: a.reshape(1, -1)
    padv = lambda a: jnp.pad(row1(a), ((0, 0), (0, LANES - a.shape[0])))
    nw_t = jnp.tile(row1(W['dn_norm_w']), (1, n_vh))

    (c_all,) = all_gather("gather_c", [c])
    c_pad = jnp.pad(c_all.reshape(N_DEV, d), ((0, LANES - N_DEV), (0, 0)))
    ada_w = W['w_ada'].shape[1]
    b_ada_me = lax.dynamic_slice(W['b_ada'], (me * ada_w,), (ada_w,)).reshape(1, ada_w)
    ada_sh = mm(c_pad, W['w_ada'], name="ada_fwd", a_act="silu", bias=b_ada_me)
    (ada_all,) = all_gather("gather_ada", [ada_sh[:N_DEV]])
    ada_me = lax.dynamic_slice(ada_all, (0, me, 0), (N_DEV, 1, ada_w)).reshape(6, 1, d)
    sh1, sc1, gt1, sh2, sc2, gt2 = [ada_me[i] for i in range(6)]

    nt = t // tb

    def act(a, bw, col0=0, width=None, grad=True, rows=tb):
        width = a.shape[1] if width is None else width
        assert col0 % bw == 0 and width % bw == 0
        c0 = col0 // bw
        return In(a, (rows, bw), lambda o, s: (s, c0 + o), grad=grad, gshape=(t, width), gimap=lambda o, s: (s, o))

    def prm(a, bw, parts=None):
        return In(a, (a.shape[0], bw), lambda o, s: (0, o), acc=True, parts=parts)

    def out(width, bw, rows=tb):
        return Out((t, width), (rows, bw), lambda o, s: (s, o))

    tbh = _tile(t, 1024, SUBLANES)
    nth = t // tbh
    tr = lambda a: jnp.transpose(a).astype(MXU_DTYPE)

    krows = lambda k_: [(slice(j, j + 1), slice(None)) for j in range(k_)]

    mod1_ins = [act(xs, d), prm(sc1, d), prm(sh1, d)]
    (h1,), _ = stage_fwd("mod1_fwd", f_modulate, (1, nt), mod1_ins, [out(d, d)])
    proj = mm(h1, wp, name="proj_fwd")

    cb_r = _tile(math.gcd(d_rnn, o_gr), 256)
    rgc_ins = [act(proj, cb_r, o_xr, d_rnn), prm(rcw, cb_r, krows(4)), prm(row1(W['rg_conv_b']), cb_r)]
    rgc_grid, rgc_car = (d_rnn // cb_r, nt), [(SUBLANES, cb_r)]
    (xc,), rgc_hist = stage_fwd("rg_conv_fwd", f_rg_conv, rgc_grid, rgc_ins, [out(d_rnn, cb_r)], rgc_car)
    gates = mm(xc, w_bd, name="rg_gates_fwd")
    lru_ins = [act(xc, cb_r), act(gates, cb_r, 0, d_rnn), act(gates, cb_r, d_rnn, d_rnn), act(proj, cb_r, o_gr, d_rnn),
               prm(row1(W['rg_b_a']), cb_r), prm(row1(W['rg_b_x']), cb_r), prm(row1(W['rg_lambda']), cb_r)]
    lru_car = [(1, cb_r)]
    (rec,), lru_hist = stage_fwd("rglru_fwd", f_rglru, rgc_grid, lru_ins, [out(d_rnn, cb_r)], lru_car)
    y_a = mm(rec, w_pa, name="proj_a_fwd")

    qkv_w = 2 * qk + vdim
    cb_q = _tile(math.gcd(qkv_w, o_q), 256)
    dnc_ins = [act(proj, cb_q, o_q, qkv_w), prm(dcw, cb_q, krows(4))]
    dnc_grid, dnc_car = (qkv_w // cb_q, nt), [(SUBLANES, cb_q)]
    (qkv_c,), dnc_hist = stage_fwd("dn_conv_fwd", f_dn_conv, dnc_grid, dnc_ins, [out(qkv_w, cb_q)], dnc_car)
    f_qnorm, f_knorm = functools.partial(f_l2norm, LANES ** -0.5), functools.partial(f_l2norm, 1.0)
    qn_ins, kn_ins = [act(qkv_c, LANES, 0, qk, rows=tbh)], [act(qkv_c, LANES, qk, qk, rows=tbh)]
    nrm_grid, nrm_outs = (qk // LANES, nth), [out(qk, LANES, tbh)]
    (qn,), _ = stage_fwd("dn_qnorm_fwd", f_qnorm, nrm_grid, qn_ins, nrm_outs)
    (kn,), _ = stage_fwd("dn_knorm_fwd", f_knorm, nrm_grid, kn_ins, nrm_outs)
    gate_ins = [act(proj, LANES, o_a, LANES), act(proj, LANES, o_b, LANES),
                prm(padv(W['dn_a_log']), LANES), prm(padv(W['dn_dt_bias']), LANES)]
    gate_outs = [out(LANES, LANES), out(LANES, LANES)]
    (g_dn, beta_dn), _ = stage_fwd("dn_gates_fwd", f_dn_gates, (1, nt), gate_ins, gate_outs)
    n_ch = t // DN_CHUNK
    gt_dn = jnp.transpose(g_dn.reshape(n_ch, DN_CHUNK, LANES)[:, :, :n_vh], (0, 2, 1))
    dn_mid = delta_intra_fwd(qn, kn, qkv_c, 1, g_dn, gt_dn, beta_dn, n_vh)
    o_dn, dn_hist = delta_inter_fwd(*dn_mid, g_dn, n_vh)
    dno_ins = [act(o_dn, LANES, rows=tbh), act(proj, LANES, o_z, vdim, rows=tbh), prm(nw_t, LANES)]
    dno_grid, dno_outs = (n_vh, nth), [out(vdim, LANES, tbh)]
    (dn,), _ = stage_fwd("dn_out_fwd", f_dn_out, dno_grid, dno_ins, dno_outs)
    y_b = mm(dn, w_pb, name="proj_b_fwd")

    cb_m = _tile(math.gcd(math.gcd(d, o_ga), o_gb), 512)
    mrg_ins = [act(proj, cb_m, o_ga, d), act(proj, cb_m, o_gb, d), act(y_a, cb_m), act(y_b, cb_m)]
    mrg_grid = (d // cb_m, nt)
    (merged,), _ = stage_fwd("merge_fwd", f_merge, mrg_grid, mrg_ins, [out(d, cb_m)])
    mix = mm(merged, w_o, name="w_out_fwd")
    ln1_ins = [act(xs, d), act(mix, d), prm(gt1, d), prm(row1(W['ln1_g']), d), prm(row1(W['ln1_b']), d)]
    (x1,), _ = stage_fwd("ln1_fwd", f_deepnorm, (1, nt), ln1_ins, [out(d, d)])

    mod2_ins = [act(x1, d), prm(sc2, d), prm(sh2, d)]
    (h2,), _ = stage_fwd("mod2_fwd", f_modulate, (1, nt), mod2_ins, [out(d, d)])
    gu = mm(h2, w_gu, name="ffn_in_fwd")
    cb_f = _tile(d_ff, 256)
    ffa_ins = [act(gu, cb_f, 0, d_ff), act(gu, cb_f, d_ff, d_ff), prm(fcw, cb_f, krows(3)), prm(row1(W['ffn_conv_b']), cb_f)]
    ffa_grid, ffa_car = (d_ff // cb_f, nt), [(SUBLANES, cb_f)]
    (act_ff,), ffa_hist = stage_fwd("ffn_act_fwd", f_ffn_act, ffa_grid, ffa_ins, [out(d_ff, cb_f)], ffa_car)
    ff = mm(act_ff, w_fd, name="ffn_down_fwd")
    ln2_ins = [act(x1, d), act(ff, d), prm(gt2, d), prm(row1(W['ln2_g']), d), prm(row1(W['ln2_b']), d)]
    (x2,), _ = stage_fwd("ln2_fwd", f_deepnorm, (1, nt), ln2_ins, [out(d, d)])
    dy, loss_loc = loss_head(x2, tgt, tb)

    dx1_a, d_ff_o, d_gt2, d_ln2g, d_ln2b = stage_bwd("ln2_bwd", f_deepnorm, (1, nt), ln2_ins, [out(d, d)], [dy])
    d_act = mm(d_ff_o, w_fd, name="ffn_down_bwd_x", tb=True)
    gw_fd = mm(tr(act_ff), d_ff_o, name="ffn_down_bwd_w")
    d_gp, d_up, d_fcw, d_fcb = stage_bwd("ffn_act_bwd", f_ffn_act, ffa_grid, ffa_ins, [out(d_ff, cb_f)], [d_act],
                                         ffa_car, ffa_hist, gdtypes={0: MXU_DTYPE, 1: MXU_DTYPE})
    d_gu = jnp.concatenate([d_gp, d_up], axis=1)
    d_h2 = mm(d_gu, w_gu, name="ffn_in_bwd_x", tb=True)
    gw_gu = mm(tr(h2), d_gu, name="ffn_in_bwd_w")
    d_x1, d_sc2, d_sh2 = stage_bwd("mod2_bwd", f_modulate, (1, nt), mod2_ins, [out(d, d)], [d_h2], add_to={0: dx1_a})
    dx_a, d_mix, d_gt1, d_ln1g, d_ln1b = stage_bwd("ln1_bwd", f_deepnorm, (1, nt), ln1_ins, [out(d, d)], [d_x1])
    d_merged = mm(d_mix, w_o, name="w_out_bwd_x", tb=True)
    gw_o = mm(tr(merged), d_mix, name="w_out_bwd_w")
    d_ga, d_gb, d_ya, d_yb = stage_bwd("merge_bwd", f_merge, mrg_grid, mrg_ins, [out(d, cb_m)], [d_merged],
                                       gdtypes={0: MXU_DTYPE, 1: MXU_DTYPE})
    d_rec = mm(d_ya, w_pa, name="proj_a_bwd_x", tb=True)
    gw_pa = mm(tr(rec), d_ya, name="proj_a_bwd_w")
    d_dn = mm(d_yb, w_pb, name="proj_b_bwd_x", tb=True)
    gw_pb = mm(tr(dn), d_yb, name="proj_b_bwd_w")

    d_o, d_z, d_nwt = stage_bwd("dn_out_bwd", f_dn_out, dno_grid, dno_ins, dno_outs, [d_dn], gdtypes={1: MXU_DTYPE})
    *d_mid, d_g_state = delta_inter_bwd(*dn_mid, g_dn, dn_hist, d_o, n_vh)
    d_qn, d_kn, d_v, d_g_col, d_gt, d_beta = delta_intra_bwd(qn, kn, qkv_c, 1, g_dn, gt_dn, beta_dn, d_mid, n_vh)
    d_g_row = jnp.pad(jnp.transpose(d_gt, (0, 2, 1)).reshape(t, n_vh), ((0, 0), (0, LANES - n_vh)))
    (d_qc,) = stage_bwd("dn_qnorm_bwd", f_qnorm, nrm_grid, qn_ins, nrm_outs, [d_qn])
    (d_kc,) = stage_bwd("dn_knorm_bwd", f_knorm, nrm_grid, kn_ins, nrm_outs, [d_kn])
    d_a, d_b, d_alog, d_dtb = stage_bwd("dn_gates_bwd", f_dn_gates, (1, nt), gate_ins, gate_outs,
                                        [(d_g_state, d_g_col, d_g_row), d_beta], gdtypes={0: MXU_DTYPE, 1: MXU_DTYPE})
    d_qkv_c = jnp.concatenate([d_qc, d_kc, d_v], axis=1)
    d_qkv, d_dcw = stage_bwd("dn_conv_bwd", f_dn_conv, dnc_grid, dnc_ins, [out(qkv_w, cb_q)], [d_qkv_c],
                             dnc_car, dnc_hist, gdtypes={0: MXU_DTYPE})

    d_xc_a, d_pr, d_pi, d_gr, d_ba, d_bx, d_lam = stage_bwd(
        "rglru_bwd", f_rglru, rgc_grid, lru_ins, [out(d_rnn, cb_r)], [d_rec], lru_car, lru_hist,
        gdtypes={1: MXU_DTYPE, 2: MXU_DTYPE, 3: MXU_DTYPE})
    d_gates = jnp.concatenate([d_pr, d_pi], axis=1)
    d_xc_b = mm(d_gates, w_bd, name="rg_gates_bwd_x", tb=True)
    gw_bd = mm(tr(xc), d_gates, name="rg_gates_bwd_w")
    d_xr, d_rcw, d_rcb = stage_bwd("rg_conv_bwd", f_rg_conv, rgc_grid, rgc_ins, [out(d_rnn, cb_r)], [(d_xc_a, d_xc_b)],
                                   rgc_car, rgc_hist, gdtypes={0: MXU_DTYPE})

    d_proj = jnp.concatenate([d_xr, d_gr, d_qkv, d_z, d_a, d_b, d_ga, d_gb,
                              jnp.zeros((t, n_pad - go[-1]), MXU_DTYPE)], axis=1)
    d_h1 = mm(d_proj, wp, name="proj_bwd_x", tb=True)
    gw_p = mm(tr(h1), d_proj, name="proj_bwd_w")
    grad_x, d_sc1, d_sh1 = stage_bwd("mod1_bwd", f_modulate, (1, nt), mod1_ins, [out(d, d)], [d_h1], add_to={0: dx_a})

    gw_in = jnp.concatenate([gw_p[:, go[i]:go[i] + splits[i]] for i in range(10)], axis=1)
    col_blocks = lambda g: jnp.transpose(g.reshape(g.shape[0], N_DEV, -1), (1, 0, 2))
    row_blocks = lambda g: g.reshape(N_DEV, -1, g.shape[1])
    big_blocks = [col_blocks(gw_in), row_blocks(gw_pa), row_blocks(gw_pb), row_blocks(gw_o),
                  col_blocks(gw_gu[:, :d_ff]), col_blocks(gw_gu[:, d_ff:]), row_blocks(gw_fd)]
    big_parts = all_to_all("scatter_grads", [b_.astype(WIRE_DTYPE) for b_ in big_blocks])

    diag = lambda g: jnp.einsum('nimj,nm->nij', g.reshape(n_blk, d_rnn // n_blk, n_blk, d_rnn // n_blk), eye_b)
    d_ada_me = jnp.concatenate([d_sh1, d_sc1, d_gt1, d_sh2, d_sc2, d_gt2], axis=1)
    small_names = ['b_ada', 'rg_conv_w', 'rg_conv_b', 'rg_w_a', 'rg_b_a', 'rg_w_x', 'rg_b_x', 'rg_lambda', 'dn_conv_w',
                   'dn_a_log', 'dn_dt_bias', 'dn_norm_w', 'ln1_g', 'ln1_b', 'ffn_conv_w', 'ffn_conv_b', 'ln2_g', 'ln2_b']
    small_loc = {
        'b_ada': d_ada_me, 'rg_conv_w': d_rcw, 'rg_conv_b': d_rcb,
        'rg_w_a': diag(gw_bd[:, :d_rnn]), 'rg_b_a': d_ba, 'rg_w_x': diag(gw_bd[:, d_rnn:]), 'rg_b_x': d_bx,
        'rg_lambda': d_lam, 'dn_conv_w': d_dcw, 'dn_a_log': d_alog[:, :n_vh], 'dn_dt_bias': d_dtb[:, :n_vh],
        'dn_norm_w': jnp.sum(d_nwt.reshape(n_vh, LANES), axis=0), 'ln1_g': d_ln1g, 'ln1_b': d_ln1b,
        'ffn_conv_w': d_fcw, 'ffn_conv_b': d_fcb, 'ln2_g': d_ln2g, 'ln2_b': d_ln2b}
    small_list = [small_loc[n] for n in small_names]
    (small_all,) = all_gather("gather_small_grads", [_pack(small_list)])
    small_sum = _unpack(sum_parts("sum_small_grads", small_all), small_list)
    g_small = dict(zip(small_names, small_sum))
    d_ada_all = small_all.reshape(N_DEV, -1)[:, :6 * d]
    d_ada_cols = lax.dynamic_slice(d_ada_all, (0, me * ada_w), (N_DEV, ada_w))
    d_ada_pad = jnp.pad(d_ada_cols, ((0, LANES - N_DEV), (0, 0)))
    gw_ada = mm(c_pad, d_ada_pad, name="ada_bwd_w", ta=True, a_act="silu")

    res = {}
    big_parts = dict(zip(big, big_parts))
    big_parts['w_ada'] = gw_ada[None]
    for n in ['w_ada'] + big:
        res[n] = adamw("adamw_" + n, W[n], big_parts[n], M[n], V[n])
    for n in small_sh:
        w_ = W[n].shape[1]
        g_small[n] = lax.dynamic_slice(g_small[n], (0, me * w_), (W[n].shape[0], w_))
    for n in small_names:
        g_small[n] = g_small[n].reshape(W[n].shape)
    pk = lambda dct: _pack([dct[n] for n in small_names])
    s_g, s_d, s_m, s_v = adamw("adamw_small", pk(W), pk(g_small)[None], pk(M), pk(V))
    like = [W[n] for n in small_names]
    for n, g_, d_, m_, v_ in zip(small_names, _unpack(s_g, like), _unpack(s_d, like), _unpack(s_m, like), _unpack(s_v, like)):
        res[n] = (g_, d_, m_, v_)

    loss = lax.psum(loss_loc[0, 0], ("x", "y", "c"))
    outs = [loss, grad_x[None]]
    for j in range(4):
        outs += [res[n][j].reshape(loc[n].shape) for n in names]
    return tuple(outs)
```

```python
import functools
import math

import jax
import jax.numpy as jnp
from jax import lax
from jax.experimental import pallas as pl
from jax.experimental.pallas import tpu as pltpu

F32 = jnp.float32
BF16 = jnp.bfloat16
MXU_DTYPE = BF16
WIRE_DTYPE = BF16
DN_DTYPE = BF16
HI = lax.Precision.HIGHEST
MESH = pl.DeviceIdType.MESH

N_DEV = 8
LANES = 128
SUBLANES = 8
VMEM_LIMIT = 56 * 1024 * 1024

RG_C = 8.0
DN_CHUNK = 64
DN_HEAD_GROUP = 8
LN_EPS = 1e-5
RMS_EPS = 1e-6
L2_EPS = 1e-6
DEPTH = 1
DEEPNORM_ALPHA = (2 * DEPTH) ** 0.25
ADAM_LR = 0.001
ADAM_B1 = 0.9
ADAM_B2 = 0.999
ADAM_EPS = 1e-08
ADAM_WD = 0.01
ADAM_STEP = 10


def _tile(n, cap, unit=LANES):
    best = None
    for t in range(unit, min(n, cap) + 1, unit):
        if n % t == 0:
            best = t
    return best if best is not None else n


def _round_up(n, m):
    return (n + m - 1) // m * m


_HBM = pl.BlockSpec(memory_space=pl.ANY)


def _exchange_sems(n):
    return [pltpu.SemaphoreType.DMA((n, N_DEV - 1)), pltpu.SemaphoreType.DMA((n, N_DEV - 1)),
            pltpu.SemaphoreType.DMA((n,))]


def _exchange_out_shape(arrs, scatter):
    return [jax.ShapeDtypeStruct(a.shape if scatter else (N_DEV,) + a.shape, a.dtype) for a in arrs]


def _exchange_copies(in_refs, out_refs, sems, scatter, phase):
    send_sems, recv_sems, local_sems = sems
    x, y, c = lax.axis_index("x"), lax.axis_index("y"), lax.axis_index("c")
    me = 4 * x + 2 * y + c
    peers = [(x ^ ((k >> 2) & 1), y ^ ((k >> 1) & 1), c ^ (k & 1)) for k in range(N_DEV)]
    lin = lambda p: 4 * p[0] + 2 * p[1] + p[2]
    block = lambda i, dev: in_refs[i].at[dev] if scatter else in_refs[i]

    def local(i):
        return pltpu.make_async_copy(block(i, me), out_refs[i].at[me], local_sems.at[i])

    def remote(i, k, src_dev, dst_row):
        return pltpu.make_async_remote_copy(src_ref=block(i, src_dev), dst_ref=out_refs[i].at[dst_row],
                                            send_sem=send_sems.at[i, k - 1], recv_sem=recv_sems.at[i, k - 1],
                                            device_id=peers[k], device_id_type=MESH)

    pairs = [(i, k) for k in range(1, N_DEV) for i in range(len(in_refs))]
    if phase == "start":
        for i in range(len(in_refs)):
            local(i).start()
        for i, k in pairs:
            remote(i, k, lin(peers[k]), me).start()
    else:
        for i, k in pairs:
            remote(i, k, me, lin(peers[k])).wait_recv()
        for i, k in pairs:
            remote(i, k, lin(peers[k]), me).wait_send()
        for i in range(len(in_refs)):
            local(i).wait()


def mm(a, b, *, name, ta=False, tb=False, a_act=None, bias=None, out_dtype=F32,
       tm_cap=1024, tn_cap=512, tk_cap=1024, gather=(), scatter=()):
    m, k = (a.shape[1], a.shape[0]) if ta else a.shape
    n = b.shape[0] if tb else b.shape[1]
    assert k == (b.shape[1] if tb else b.shape[0]), (a.shape, b.shape, ta, tb)
    tm, tn, tk = _tile(m, tm_cap), _tile(n, tn_cap), _tile(k, tk_cap)
    nk = k // tk
    grid = (m // tm, n // tn, nk)
    dims = (((0 if ta else 1,), (1 if tb else 0,)), ((), ()))
    assert not (gather and scatter)
    xch = list(gather) + list(scatter)
    nx = len(xch)

    def body(*refs):
        refs = list(refs)
        if nx:
            sems = refs[-3:]
            del refs[-3:]
        acc_ref = refs.pop() if nk > 1 else None
        if nx:
            x_out = refs[-nx:]
            del refs[-nx:]
            x_in = refs[-nx - 1:-1]
            del refs[-nx - 1:-1]
            step = (pl.program_id(0) * grid[1] + pl.program_id(1)) * grid[2] + pl.program_id(2)

            @pl.when(step == 0)
            def _():
                _exchange_copies(x_in, x_out, sems, bool(scatter), "start")
        if bias is None:
            a_ref, b_ref, o_ref = refs
        else:
            a_ref, b_ref, bias_ref, o_ref = refs
        kk = pl.program_id(2)
        av = a_ref[...]
        if a_act == "silu":
            av = jax.nn.silu(av.astype(F32))
        prod = lax.dot_general(av.astype(MXU_DTYPE), b_ref[...].astype(MXU_DTYPE), dims, preferred_element_type=F32)

        def finish(r):
            if bias is not None:
                r = r + bias_ref[...]
            o_ref[...] = r.astype(o_ref.dtype)

        if nk == 1:
            finish(prod)
        else:
            @pl.when(kk == 0)
            def _():
                acc_ref[...] = prod

            @pl.when((kk > 0) & (kk < nk - 1))
            def _():
                acc_ref[...] += prod

            @pl.when(kk == nk - 1)
            def _():
                finish(acc_ref[...] + prod)

        if nx:
            @pl.when(step == grid[0] * grid[1] * grid[2] - 1)
            def _():
                _exchange_copies(x_in, x_out, sems, bool(scatter), "wait")

    a_spec = pl.BlockSpec((tk, tm), lambda i, j, q: (q, i)) if ta else pl.BlockSpec((tm, tk), lambda i, j, q: (i, q))
    b_spec = pl.BlockSpec((tn, tk), lambda i, j, q: (j, q)) if tb else pl.BlockSpec((tk, tn), lambda i, j, q: (q, j))
    in_specs, args = [a_spec, b_spec], [a, b]
    if bias is not None:
        in_specs.append(pl.BlockSpec((1, tn), lambda i, j, q: (0, j)))
        args.append(bias)
    o_spec, o_shape = pl.BlockSpec((tm, tn), lambda i, j, q: (i, j)), jax.ShapeDtypeStruct((m, n), out_dtype)
    acc = [pltpu.VMEM((tm, tn), F32)] if nk > 1 else []
    if not nx:
        return pl.pallas_call(
            body, name=name, grid=grid, in_specs=in_specs, out_specs=o_spec, out_shape=o_shape, scratch_shapes=acc,
            compiler_params=pltpu.CompilerParams(dimension_semantics=("parallel", "parallel", "arbitrary"),
                                                 vmem_limit_bytes=VMEM_LIMIT),
        )(*args)
    return pl.pallas_call(
        body, name=name, grid=grid, in_specs=in_specs + [_HBM] * nx, out_specs=[o_spec] + [_HBM] * nx,
        out_shape=[o_shape] + _exchange_out_shape(xch, bool(scatter)), scratch_shapes=acc + _exchange_sems(nx),
        compiler_params=pltpu.CompilerParams(dimension_semantics=("arbitrary", "arbitrary", "arbitrary"),
                                             vmem_limit_bytes=VMEM_LIMIT, has_side_effects=True),
    )(*args, *xch)


class In:
    def __init__(self, arr, block, imap, acc=False, grad=True, parts=None, gshape=None, gimap=None):
        self.arr, self.block, self.imap, self.acc, self.grad, self.parts = arr, block, imap, acc, grad, parts
        self.gshape = arr.shape if gshape is None else gshape
        self.gimap = imap if gimap is None else gimap


class Out:
    def __init__(self, shape, block, imap, dtype=F32):
        self.shape, self.block, self.imap, self.dtype = shape, block, imap, dtype


def _load(in_refs, ins):
    vals = []
    for r, i in zip(in_refs, ins):
        if i.parts is None:
            vals.append(r[...])
        else:
            vals.extend(r[p] for p in i.parts)
    return vals


def _stage_params():
    return pltpu.CompilerParams(dimension_semantics=("parallel", "arbitrary"), vmem_limit_bytes=VMEM_LIMIT)


def stage_fwd(name, f, grid, ins, outs, carries=()):
    n_in, n_out, n_c = len(ins), len(outs), len(carries)

    def body(*refs):
        in_refs, out_refs = refs[:n_in], refs[n_in:n_in + n_out]
        hist_refs, c_refs = refs[n_in + n_out:n_in + n_out + n_c], refs[n_in + n_out + n_c:]
        if n_c:
            @pl.when(pl.program_id(1) == 0)
            def _():
                for c in c_refs:
                    c[...] = jnp.zeros_like(c)
        cin = [c[...] for c in c_refs]
        for h, c in zip(hist_refs, cin):
            h[...] = c
        o, cout = f(*_load(in_refs, ins), *cin)
        for r, v in zip(out_refs, o):
            r[...] = v.astype(r.dtype)
        for c, v in zip(c_refs, cout):
            c[...] = v

    hist_spec = lambda c: pl.BlockSpec((None, None) + tuple(c), lambda o, s: (o, s) + (0,) * len(c))
    res = pl.pallas_call(
        body, name=name, grid=grid,
        in_specs=[pl.BlockSpec(i.block, i.imap) for i in ins],
        out_specs=[pl.BlockSpec(o.block, o.imap) for o in outs] + [hist_spec(c) for c in carries],
        out_shape=[jax.ShapeDtypeStruct(o.shape, o.dtype) for o in outs]
        + [jax.ShapeDtypeStruct(tuple(grid) + tuple(c), F32) for c in carries],
        scratch_shapes=[pltpu.VMEM(tuple(c), F32) for c in carries],
        compiler_params=_stage_params(),
    )(*[i.arr for i in ins])
    return list(res[:n_out]), list(res[n_out:])


def stage_bwd(name, f, grid, ins, outs, cots, carries=(), hists=(), add_to=None, gdtypes=None):
    n_in, n_out, n_c = len(ins), len(outs), len(carries)
    ns = grid[1]
    add_to = add_to or {}
    gdtypes = gdtypes or {}
    add_idx = sorted(add_to)
    g_idx = [k for k, i in enumerate(ins) if i.grad]
    cots = [c if isinstance(c, (tuple, list)) else (c,) for c in cots]
    n_cot = [len(c) for c in cots]
    rev = lambda imap: (lambda o, s: imap(o, ns - 1 - s))

    def body(*refs):
        p = 0
        in_refs = refs[p:p + n_in]; p += n_in
        cot_refs = []
        for cnt in n_cot:
            cot_refs.append(refs[p:p + cnt]); p += cnt
        hist_refs = refs[p:p + n_c]; p += n_c
        add_refs = refs[p:p + len(add_idx)]; p += len(add_idx)
        g_refs = refs[p:p + len(g_idx)]; p += len(g_idx)
        dc_refs = refs[p:]
        first = pl.program_id(1) == 0
        if n_c:
            @pl.when(first)
            def _():
                for c in dc_refs:
                    c[...] = jnp.zeros_like(c)
        vals = _load(in_refs, ins)
        cin = [h[...] for h in hist_refs]
        (o, cout), vjp = jax.vjp(lambda *a: f(*a), *vals, *cin)
        cot_o = []
        for crs, v in zip(cot_refs, o):
            c = crs[0][...].astype(v.dtype)
            for extra in crs[1:]:
                c = c + extra[...].astype(v.dtype)
            cot_o.append(c)
        cot_c = tuple(c[...] for c in dc_refs)
        grads = vjp((tuple(cot_o), cot_c))
        pos, per_in = 0, []
        for i in ins:
            cnt = 1 if i.parts is None else len(i.parts)
            per_in.append(grads[pos:pos + cnt])
            pos += cnt
        dcin = grads[pos:]
        for gr, k in zip(g_refs, g_idx):
            i, gs = ins[k], per_in[k]
            if i.acc:
                @pl.when(first)
                def _(gr=gr):
                    gr[...] = jnp.zeros_like(gr)
                if i.parts is None:
                    gr[...] += gs[0].astype(gr.dtype)
                else:
                    for pt, g in zip(i.parts, gs):
                        gr[pt] += g.astype(gr.dtype)
            else:
                g = gs[0]
                if k in add_to:
                    g = g + add_refs[add_idx.index(k)][...].astype(g.dtype)
                gr[...] = g.astype(gr.dtype)
        for c, v in zip(dc_refs, dcin):
            c[...] = v

    in_specs = [pl.BlockSpec(i.block, rev(i.imap)) for i in ins]
    for o_, cnt in zip(outs, n_cot):
        in_specs += [pl.BlockSpec(o_.block, rev(o_.imap))] * cnt
    in_specs += [pl.BlockSpec((None, None) + tuple(c), (lambda c: (lambda o, s: (o, ns - 1 - s) + (0,) * len(c)))(c))
                 for c in carries]
    in_specs += [pl.BlockSpec(ins[k].block, rev(ins[k].gimap)) for k in add_idx]
    out_specs, out_shape = [], []
    for k in g_idx:
        i = ins[k]
        if i.acc:
            out_specs.append(pl.BlockSpec(i.block, (lambda im: (lambda o, s: im(o, 0)))(i.imap)))
        else:
            out_specs.append(pl.BlockSpec(i.block, rev(i.gimap)))
        out_shape.append(jax.ShapeDtypeStruct(i.gshape, gdtypes.get(k, F32)))
    res = pl.pallas_call(
        body, name=name, grid=grid, in_specs=in_specs, out_specs=out_specs, out_shape=out_shape,
        scratch_shapes=[pltpu.VMEM(tuple(c), F32) for c in carries],
        compiler_params=_stage_params(),
    )(*[i.arr for i in ins], *[a for c in cots for a in c], *hists, *[add_to[k] for k in add_idx])
    return list(res)


def _iota_rows(shape):
    return lax.broadcasted_iota(jnp.int32, shape, 0)


@functools.partial(jax.custom_vjp, nondiff_argnums=(1,))
def _roll_rows(x, s):
    return pltpu.roll(x, s % x.shape[0], 0)


def _roll_rows_fwd(x, s):
    return _roll_rows(x, s), None


def _roll_rows_bwd(s, _, g):
    return (_roll_rows(g, -s),)


_roll_rows.defvjp(_roll_rows_fwd, _roll_rows_bwd)


@jax.custom_vjp
def _drop_head(xx):
    return xx[SUBLANES:]


def _drop_head_fwd(xx):
    return xx[SUBLANES:], None


def _drop_head_bwd(_, g):
    return (jnp.concatenate([jnp.zeros((SUBLANES, g.shape[1]), g.dtype), g], axis=0),)


_drop_head.defvjp(_drop_head_fwd, _drop_head_bwd)


@jax.custom_vjp
def _last_rows(x):
    return x[x.shape[0] - SUBLANES:]


def _last_rows_fwd(x):
    return x[x.shape[0] - SUBLANES:], x.shape[0]


def _last_rows_bwd(n, g):
    return (jnp.concatenate([jnp.zeros((n - SUBLANES, g.shape[1]), g.dtype), g], axis=0),)


_last_rows.defvjp(_last_rows_fwd, _last_rows_bwd)


def _last_row(x):
    n = x.shape[0]
    return jnp.sum(jnp.where(_iota_rows(x.shape) == n - 1, x, 0.0), axis=0, keepdims=True)


def _scan_steps(n):
    s = 1
    while s < n:
        yield s
        s *= 2


def _block_scan_impl(a, u, h0):
    n = a.shape[0]
    row = _iota_rows(a.shape)
    for s in _scan_steps(n):
        keep = row >= s
        a_s = jnp.where(keep, pltpu.roll(a, s, 0), 1.0)
        u_s = jnp.where(keep, pltpu.roll(u, s, 0), 0.0)
        u = u + a * u_s
        a = a * a_s
    return u + a * h0


@jax.custom_vjp
def _block_scan(a, u, h0):
    return _block_scan_impl(a, u, h0)


def _block_scan_fwd(a, u, h0):
    h = _block_scan_impl(a, u, h0)
    return h, (a, h, h0)


def _block_scan_bwd(res, dh):
    a, h, h0 = res
    n = a.shape[0]
    row = _iota_rows(a.shape)
    b = jnp.where(row < n - 1, pltpu.roll(a, n - 1, 0), 0.0)
    lam = dh
    for s in _scan_steps(n):
        keep = row < n - s
        b_s = jnp.where(keep, pltpu.roll(b, n - s, 0), 1.0)
        l_s = jnp.where(keep, pltpu.roll(lam, n - s, 0), 0.0)
        lam = lam + b * l_s
        b = b * b_s
    h_prev = jnp.where(row >= 1, pltpu.roll(h, 1, 0), jnp.broadcast_to(h0, h.shape))
    d_h0 = jnp.sum(jnp.where(row == 0, a * lam, 0.0), axis=0, keepdims=True)
    return lam * h_prev, lam, d_h0


_block_scan.defvjp(_block_scan_fwd, _block_scan_bwd)


def _dot_hi(a, b, dims=(((1,), (0,)), ((), ()))):
    return lax.dot_general(a, b, dims, precision=HI, preferred_element_type=F32)


_NN, _NT, _TN = "nn", "nt", "tn"
_CONTRACT = {_NN: (1, 0), _NT: (1, 1), _TN: (0, 0)}


def _raw_dot(a, b, kind):
    ca, cb = _CONTRACT[kind]
    lead = a.ndim - 2
    dims = (((ca + lead,), (cb + lead,)), (tuple(range(lead)), tuple(range(lead))))
    return lax.dot_general(a.astype(DN_DTYPE), b.astype(DN_DTYPE), dims, preferred_element_type=F32)


@jax.custom_vjp
def _nn(a, b):
    return _raw_dot(a, b, _NN)


_nn.defvjp(lambda a, b: (_raw_dot(a, b, _NN), (a, b)),
           lambda r, g: (_raw_dot(g, r[1], _NT), _raw_dot(r[0], g, _TN)))


@jax.custom_vjp
def _nt(a, b):
    return _raw_dot(a, b, _NT)


_nt.defvjp(lambda a, b: (_raw_dot(a, b, _NT), (a, b)),
           lambda r, g: (_raw_dot(g, r[1], _NN), _raw_dot(g, r[0], _TN)))


@jax.custom_vjp
def _tn(a, b):
    return _raw_dot(a, b, _TN)


_tn.defvjp(lambda a, b: (_raw_dot(a, b, _TN), (a, b)),
           lambda r, g: (_raw_dot(r[1], g, _NT), _raw_dot(r[0], g, _NN)))


def _neumann_inverse(a):
    n = a.shape[-1]
    eye = (lax.broadcasted_iota(jnp.int32, (n, n), 0) == lax.broadcasted_iota(jnp.int32, (n, n), 1)).astype(F32)
    p = _raw_dot(a, a, _NN)
    e = p
    for _ in range(int(math.log2(n)) - 2):
        p = _raw_dot(p, p, _NN)
        e = e + p + _raw_dot(e, p, _NN)
    return eye - a + e - _raw_dot(a, e, _NN)


@jax.custom_vjp
def _unit_lower_inverse(a):
    return _neumann_inverse(a)


def _unit_lower_inverse_fwd(a):
    x = _neumann_inverse(a)
    return x, x


def _unit_lower_inverse_bwd(x, g):
    return (-_raw_dot(_raw_dot(x, g, _TN), x, _NT),)


_unit_lower_inverse.defvjp(_unit_lower_inverse_fwd, _unit_lower_inverse_bwd)


def _softplus(x):
    return jnp.maximum(x, 0.0) + jnp.log1p(jnp.exp(-jnp.abs(x)))


def _neg_expm1(x):
    series = -x * (1.0 + x * (0.5 + x * (1.0 / 6.0 + x * (1.0 / 24.0 + x * (1.0 / 120.0)))))
    return jnp.where(x > -0.03, series, 1.0 - jnp.exp(x))


def f_modulate(x, sc, sh):
    return (x * (1.0 + sc) + sh,), ()


def f_deepnorm(x, y, gt, g, b):
    v = DEEPNORM_ALPHA * x + (1.0 + gt) * y
    mu = jnp.mean(v, axis=-1, keepdims=True)
    vc = v - mu
    var = jnp.mean(vc * vc, axis=-1, keepdims=True)
    return (vc * lax.rsqrt(var + LN_EPS) * g + b,), ()


def _causal_conv(x, prev, ws):
    xx = jnp.concatenate([prev, x], axis=0)
    k = len(ws)
    y = ws[k - 1] * x
    for j in range(k - 1):
        y = y + ws[j] * _drop_head(_roll_rows(xx, k - 1 - j))
    return y


def f_rg_conv(x, w0, w1, w2, w3, b, prev):
    return (_causal_conv(x, prev, (w0, w1, w2, w3)) + b,), (_last_rows(x),)


def f_dn_conv(x, w0, w1, w2, w3, prev):
    return (jax.nn.silu(_causal_conv(x, prev, (w0, w1, w2, w3))),), (_last_rows(x),)


def f_ffn_act(gp, up, w0, w1, w2, b, prev):
    return (jax.nn.gelu(_causal_conv(gp, prev, (w0, w1, w2)) + b) * up,), (_last_rows(gp),)


def f_rglru(xc, pre_r, pre_i, gr, b_a, b_x, lam, h0):
    gate_r = jax.nn.sigmoid(pre_r + b_a)
    gate_i = jax.nn.sigmoid(pre_i + b_x)
    log_a = -RG_C * gate_r * _softplus(-lam)
    a = jnp.exp(log_a)
    mult = jnp.sqrt(_neg_expm1(2.0 * log_a))
    h = _block_scan(a, mult * gate_i * xc, h0)
    return (h * jax.nn.gelu(gr),), (_last_row(h),)


def f_l2norm(scale, x):
    return (x * lax.rsqrt(jnp.sum(x * x, axis=-1, keepdims=True) + L2_EPS) * scale,), ()


def f_dn_gates(a_in, b_in, a_log, dt_bias):
    g = -jnp.exp(a_log) * _softplus(a_in + dt_bias)
    n = g.shape[0]
    shift = int(math.log2(DN_CHUNK))
    ri = lax.broadcasted_iota(jnp.int32, (n, n), 0)
    ci = lax.broadcasted_iota(jnp.int32, (n, n), 1)
    tri = ((lax.shift_right_logical(ri, shift) == lax.shift_right_logical(ci, shift)) & (ri >= ci)).astype(F32)
    return (_dot_hi(tri, g), jax.nn.sigmoid(b_in)), ()


def f_dn_out(o, z, nw):
    r = lax.rsqrt(jnp.mean(o * o, axis=-1, keepdims=True) + RMS_EPS)
    return (o * r * nw * jax.nn.silu(z),), ()


def f_merge(ga, gb, ya, yb):
    return (jax.nn.sigmoid(ga) * ya + jax.nn.sigmoid(gb) * yb,), ()


def _delta_intra(q, k, v, g_i, g_j, beta):
    c = q.shape[-2]
    ri = lax.broadcasted_iota(jnp.int32, (c, c), 0)
    ci = lax.broadcasted_iota(jnp.int32, (c, c), 1)
    decay = jnp.exp(jnp.where(ri >= ci, g_i - g_j, -jnp.inf))
    g_last = jnp.sum(jnp.where(_iota_rows((c, 1)) == c - 1, g_i, 0.0), axis=-2, keepdims=True)
    exp_g = jnp.exp(g_i)
    kb = k * beta
    t_inv = _unit_lower_inverse(jnp.where(ri > ci, _nt(kb, k) * decay, 0.0))
    u = _nn(t_inv, v * beta)
    w = _nn(t_inv, kb * exp_g)
    return u, w, _nt(q, k) * decay, q * exp_g, k * jnp.exp(g_last - g_i)


def _delta_inter(u, w, qk, q_dec, k_dec, g_last, state):
    v_new = u - _nn(w, state)
    o = _nn(q_dec, state) + _nn(qk, v_new)
    return o, jnp.exp(g_last) * state + _tn(k_dec, v_new)


def _chunk_spec(width, nc=None, col=0):
    if nc is None:
        return pl.BlockSpec((DN_CHUNK, width), lambda s: (s, col))
    return pl.BlockSpec((DN_CHUNK, width), lambda s: (nc - 1 - s, col))


def _delta_params(sem):
    return pltpu.CompilerParams(dimension_semantics=(sem,), vmem_limit_bytes=VMEM_LIMIT)


def _head(ref, h, width=LANES):
    return ref[:, h * LANES:h * LANES + width]


def _head_groups(n_vh):
    hb = min(DN_HEAD_GROUP, n_vh)
    return [range(h0, h0 + hb) for h0 in range(0, n_vh, hb)]


def _stack(hs, f):
    return jnp.stack([f(h) for h in hs])


def _intra_operands(hs, rep, q_ref, k_ref, v_ref, g_ref, gt_ref, b_ref):
    return (_stack(hs, lambda h: _head(q_ref, h // rep)), _stack(hs, lambda h: _head(k_ref, h // rep)),
            _stack(hs, lambda h: _head(v_ref, h)), _stack(hs, lambda h: g_ref[:, h:h + 1]),
            _stack(hs, lambda h: gt_ref[h:h + 1, :]), _stack(hs, lambda h: b_ref[:, h:h + 1]))


def _inter_operands(hs, u_ref, w_ref, qk_ref, qd_ref, kd_ref, g_ref):
    f32 = lambda ref, width=LANES: _stack(hs, lambda h: _head(ref, h, width).astype(F32))
    return (f32(u_ref), f32(w_ref), f32(qk_ref, DN_CHUNK), f32(qd_ref), f32(kd_ref),
            _stack(hs, lambda h: g_ref[DN_CHUNK - 1:DN_CHUNK, h:h + 1]))


def delta_intra_fwd(qn, kn, qkv, v_blk, big_g, big_gt, beta, n_vh):
    t, qk_w = qn.shape
    vdim = n_vh * LANES
    rep = vdim // qk_w
    nc = t // DN_CHUNK

    def body(q_ref, k_ref, v_ref, g_ref, gt_ref, b_ref, u_ref, w_ref, qk_ref, qd_ref, kd_ref):
        for hs in _head_groups(n_vh):
            u, w, qk, qd, kd = _delta_intra(*_intra_operands(hs, rep, q_ref, k_ref, v_ref, g_ref, gt_ref, b_ref))
            for i, h in enumerate(hs):
                sl = slice(h * LANES, (h + 1) * LANES)
                u_ref[:, sl] = u[i]
                w_ref[:, sl] = w[i].astype(w_ref.dtype)
                qk_ref[:, sl] = jnp.concatenate([qk[i], jnp.zeros_like(qk[i])], axis=1).astype(qk_ref.dtype)
                qd_ref[:, sl] = qd[i].astype(qd_ref.dtype)
                kd_ref[:, sl] = kd[i].astype(kd_ref.dtype)

    return pl.pallas_call(
        body, name="delta_intra_fwd", grid=(nc,),
        in_specs=[_chunk_spec(qk_w), _chunk_spec(qk_w), _chunk_spec(vdim, col=v_blk), _chunk_spec(LANES),
                  pl.BlockSpec((None, n_vh, DN_CHUNK), lambda s: (s, 0, 0)), _chunk_spec(LANES)],
        out_specs=[_chunk_spec(vdim)] * 5,
        out_shape=[jax.ShapeDtypeStruct((t, vdim), F32)] + [jax.ShapeDtypeStruct((t, vdim), DN_DTYPE)] * 4,
        compiler_params=_delta_params("parallel"),
    )(qn, kn, qkv, big_g, big_gt, beta)


def delta_inter_fwd(u, w, qk, q_dec, k_dec, big_g, n_vh):
    t, vdim = u.shape
    nc = t // DN_CHUNK

    def body(u_ref, w_ref, qk_ref, qd_ref, kd_ref, g_ref, o_ref, hist_ref, s_ref):
        @pl.when(pl.program_id(0) == 0)
        def _():
            s_ref[...] = jnp.zeros_like(s_ref)
        for hs in _head_groups(n_vh):
            grp = slice(hs[0], hs[-1] + 1)
            st = s_ref[grp]
            hist_ref[grp] = st
            o, ns = _delta_inter(*_inter_operands(hs, u_ref, w_ref, qk_ref, qd_ref, kd_ref, g_ref), st)
            for i, h in enumerate(hs):
                o_ref[:, h * LANES:(h + 1) * LANES] = o[i]
            s_ref[grp] = ns

    return pl.pallas_call(
        body, name="delta_inter_fwd", grid=(nc,),
        in_specs=[_chunk_spec(vdim)] * 5 + [_chunk_spec(LANES)],
        out_specs=[_chunk_spec(vdim), pl.BlockSpec((None, n_vh, LANES, LANES), lambda s: (s, 0, 0, 0))],
        out_shape=[jax.ShapeDtypeStruct((t, vdim), F32), jax.ShapeDtypeStruct((nc, n_vh, LANES, LANES), F32)],
        scratch_shapes=[pltpu.VMEM((n_vh, LANES, LANES), F32)],
        compiler_params=_delta_params("arbitrary"),
    )(u, w, qk, q_dec, k_dec, big_g)


def delta_inter_bwd(u, w, qk, q_dec, k_dec, big_g, hist, d_o, n_vh):
    t, vdim = u.shape
    nc = t // DN_CHUNK

    def body(u_ref, w_ref, qk_ref, qd_ref, kd_ref, g_ref, hist_ref, do_ref,
             du_ref, dw_ref, dqk_ref, dqd_ref, dkd_ref, dg_ref, ds_ref):
        @pl.when(pl.program_id(0) == 0)
        def _():
            ds_ref[...] = jnp.zeros_like(ds_ref)
        lane = lax.broadcasted_iota(jnp.int32, (1, LANES), 1)
        dgl_all = jnp.zeros((1, LANES), F32)
        for hs in _head_groups(n_vh):
            grp = slice(hs[0], hs[-1] + 1)
            prim = _inter_operands(hs, u_ref, w_ref, qk_ref, qd_ref, kd_ref, g_ref) + (hist_ref[grp],)
            _, vjp = jax.vjp(_delta_inter, *prim)
            du, dw, dqk, dqd, dkd, dgl, dst = vjp((_stack(hs, lambda h: _head(do_ref, h)), ds_ref[grp]))
            ds_ref[grp] = dst
            for i, h in enumerate(hs):
                sl = slice(h * LANES, (h + 1) * LANES)
                du_ref[:, sl] = du[i]
                dw_ref[:, sl] = dw[i]
                dqk_ref[:, sl] = jnp.concatenate([dqk[i], jnp.zeros_like(dqk[i])], axis=1)
                dqd_ref[:, sl] = dqd[i]
                dkd_ref[:, sl] = dkd[i]
                dgl_all = dgl_all + dgl[i] * (lane == h).astype(F32)
        last = _iota_rows((DN_CHUNK, LANES)) == DN_CHUNK - 1
        dg_ref[...] = jnp.where(last, jnp.broadcast_to(dgl_all, (DN_CHUNK, LANES)), 0.0)

    rv = lambda w_: _chunk_spec(w_, nc)
    return pl.pallas_call(
        body, name="delta_inter_bwd", grid=(nc,),
        in_specs=[rv(vdim)] * 5 + [rv(LANES), pl.BlockSpec((None, n_vh, LANES, LANES), lambda s: (nc - 1 - s, 0, 0, 0)),
                                   rv(vdim)],
        out_specs=[rv(vdim)] * 5 + [rv(LANES)],
        out_shape=[jax.ShapeDtypeStruct((t, vdim), F32)] * 5 + [jax.ShapeDtypeStruct((t, LANES), F32)],
        scratch_shapes=[pltpu.VMEM((n_vh, LANES, LANES), F32)],
        compiler_params=_delta_params("arbitrary"),
    )(u, w, qk, q_dec, k_dec, big_g, hist, d_o)


def delta_intra_bwd(qn, kn, qkv, v_blk, big_g, big_gt, beta, cots, n_vh):
    t, qk_w = qn.shape
    vdim = n_vh * LANES
    rep = vdim // qk_w
    nc = t // DN_CHUNK

    def body(q_ref, k_ref, v_ref, g_ref, gt_ref, b_ref, du_ref, dw_ref, dqk_ref, dqd_ref, dkd_ref,
             dq_ref, dk_ref, dv_ref, dg_ref, dgt_ref, db_ref):
        lane = lax.broadcasted_iota(jnp.int32, (1, LANES), 1)
        dg_all = jnp.zeros((DN_CHUNK, LANES), F32)
        db_all = jnp.zeros((DN_CHUNK, LANES), F32)
        dq_acc, dk_acc = None, None
        for hs in _head_groups(n_vh):
            _, vjp = jax.vjp(_delta_intra, *_intra_operands(hs, rep, q_ref, k_ref, v_ref, g_ref, gt_ref, b_ref))
            cot = lambda ref, width=LANES: _stack(hs, lambda h: _head(ref, h, width))
            dq, dk, dv, dgi, dgj, db = vjp((cot(du_ref), cot(dw_ref), cot(dqk_ref, DN_CHUNK), cot(dqd_ref), cot(dkd_ref)))
            for i, h in enumerate(hs):
                j = h // rep
                dv_ref[:, h * LANES:(h + 1) * LANES] = dv[i]
                dgt_ref[h:h + 1, :] = dgj[i]
                onehot = (lane == h).astype(F32)
                dg_all = dg_all + dgi[i] * onehot
                db_all = db_all + db[i] * onehot
                dq_acc = dq[i] if h % rep == 0 else dq_acc + dq[i]
                dk_acc = dk[i] if h % rep == 0 else dk_acc + dk[i]
                if h % rep == rep - 1:
                    dq_ref[:, j * LANES:(j + 1) * LANES] = dq_acc
                    dk_ref[:, j * LANES:(j + 1) * LANES] = dk_acc
        dg_ref[...] = dg_all
        db_ref[...] = db_all

    gt_spec = pl.BlockSpec((None, n_vh, DN_CHUNK), lambda s: (s, 0, 0))
    return pl.pallas_call(
        body, name="delta_intra_bwd", grid=(nc,),
        in_specs=[_chunk_spec(qk_w), _chunk_spec(qk_w), _chunk_spec(vdim, col=v_blk), _chunk_spec(LANES), gt_spec,
                  _chunk_spec(LANES)] + [_chunk_spec(vdim)] * 5,
        out_specs=[_chunk_spec(qk_w), _chunk_spec(qk_w), _chunk_spec(vdim), _chunk_spec(LANES), gt_spec,
                   _chunk_spec(LANES)],
        out_shape=[jax.ShapeDtypeStruct((t, qk_w), F32), jax.ShapeDtypeStruct((t, qk_w), F32),
                   jax.ShapeDtypeStruct((t, vdim), F32), jax.ShapeDtypeStruct((t, LANES), F32),
                   jax.ShapeDtypeStruct((nc, n_vh, DN_CHUNK), F32), jax.ShapeDtypeStruct((t, LANES), F32)],
        compiler_params=_delta_params("parallel"),
    )(qn, kn, qkv, big_g, big_gt, beta, *cots)


def loss_head(y, target, tb):
    t, d = y.shape

    def body(y_ref, t_ref, dy_ref, loss_ref):
        @pl.when(pl.program_id(0) == 0)
        def _():
            loss_ref[...] = jnp.zeros_like(loss_ref)
        err = y_ref[...] - t_ref[...]
        dy_ref[...] = err * (1.0 / d)
        loss_ref[...] += 0.5 * jnp.sum(jnp.sum(err * err, axis=1, keepdims=True), axis=0, keepdims=True) * (1.0 / d)

    return pl.pallas_call(
        body, name="loss_head", grid=(t // tb,),
        in_specs=[pl.BlockSpec((tb, d), lambda s: (s, 0))] * 2,
        out_specs=[pl.BlockSpec((tb, d), lambda s: (s, 0)), pl.BlockSpec((1, 1), lambda s: (0, 0))],
        out_shape=[jax.ShapeDtypeStruct((t, d), F32), jax.ShapeDtypeStruct((1, 1), F32)],
        compiler_params=pltpu.CompilerParams(dimension_semantics=("arbitrary",), vmem_limit_bytes=VMEM_LIMIT),
    )(y, target)


def all_gather(name, arrs):
    n = len(arrs)

    def body(*refs):
        in_refs, out_refs, sems = refs[:n], refs[n:2 * n], refs[2 * n:]
        _exchange_copies(in_refs, out_refs, sems, False, "start")
        _exchange_copies(in_refs, out_refs, sems, False, "wait")

    res = pl.pallas_call(
        body, name=name,
        in_specs=[_HBM] * n, out_specs=[_HBM] * n,
        out_shape=_exchange_out_shape(arrs, False), scratch_shapes=_exchange_sems(n),
        compiler_params=pltpu.CompilerParams(has_side_effects=True),
    )(*arrs)
    return list(res)


def _adamw_math(w, g, m, v):
    m = ADAM_B1 * m + (1.0 - ADAM_B1) * g
    v = ADAM_B2 * v + (1.0 - ADAM_B2) * (g * g)
    m_hat = m / (1.0 - ADAM_B1 ** ADAM_STEP)
    v_hat = v / (1.0 - ADAM_B2 ** ADAM_STEP)
    delta = -ADAM_LR * (m_hat / (jnp.sqrt(v_hat) + ADAM_EPS) + ADAM_WD * w)
    return delta, m, v


def adamw(name, w, parts, m, v, rows_cap=128):
    r, c = w.shape
    np_ = parts.shape[0]
    tr = _tile(r, rows_cap, SUBLANES * (4 // parts.dtype.itemsize))

    def body(w_ref, p_ref, m_ref, v_ref, g_ref, d_ref, nm_ref, nv_ref):
        g = p_ref[0].astype(F32)
        for k in range(1, np_):
            g = g + p_ref[k].astype(F32)
        delta, nm, nv = _adamw_math(w_ref[...], g, m_ref[...], v_ref[...])
        g_ref[...] = g
        d_ref[...] = delta
        nm_ref[...] = nm
        nv_ref[...] = nv

    spec = pl.BlockSpec((tr, c), lambda i: (i, 0))
    return pl.pallas_call(
        body, name=name, grid=(r // tr,),
        in_specs=[spec, pl.BlockSpec((np_, tr, c), lambda i: (0, i, 0)), spec, spec],
        out_specs=[spec] * 4, out_shape=[jax.ShapeDtypeStruct((r, c), F32)] * 4,
        compiler_params=pltpu.CompilerParams(dimension_semantics=("parallel",), vmem_limit_bytes=VMEM_LIMIT),
    )(w, parts, m, v)


def sum_parts(name, parts, rows_cap=256):
    np_, r, c = parts.shape
    tr = _tile(r, rows_cap, SUBLANES)

    def body(p_ref, o_ref):
        g = p_ref[0].astype(F32)
        for k in range(1, np_):
            g = g + p_ref[k].astype(F32)
        o_ref[...] = g

    return pl.pallas_call(
        body, name=name, grid=(r // tr,),
        in_specs=[pl.BlockSpec((np_, tr, c), lambda i: (0, i, 0))],
        out_specs=pl.BlockSpec((tr, c), lambda i: (i, 0)),
        out_shape=jax.ShapeDtypeStruct((r, c), F32),
        compiler_params=pltpu.CompilerParams(dimension_semantics=("parallel",), vmem_limit_bytes=VMEM_LIMIT),
    )(parts)


def _pack(arrs):
    flat = jnp.concatenate([a.reshape(-1).astype(F32) for a in arrs])
    n = flat.shape[0]
    return jnp.pad(flat, (0, _round_up(n, LANES * SUBLANES) - n)).reshape(-1, LANES)


def _unpack(packed, like):
    flat, out, pos = packed.reshape(-1), [], 0
    for a in like:
        out.append(flat[pos:pos + a.size].reshape(a.shape))
        pos += a.size
    return out


def kernel(x, c, w_ada, b_ada, w_in, rg_conv_w, rg_conv_b, rg_w_a, rg_b_a, rg_w_x, rg_b_x, rg_lambda, dn_conv_w, dn_a_log, dn_dt_bias, dn_norm_w, w_proj_a, w_proj_b, w_out, ln1_g, ln1_b, ffn_w_gate, ffn_w_up, ffn_conv_w, ffn_conv_b, ffn_w_down, ln2_g, ln2_b, loss_target, m_w_ada, m_b_ada, m_w_in, m_rg_conv_w, m_rg_conv_b, m_rg_w_a, m_rg_b_a, m_rg_w_x, m_rg_b_x, m_rg_lambda, m_dn_conv_w, m_dn_a_log, m_dn_dt_bias, m_dn_norm_w, m_w_proj_a, m_w_proj_b, m_w_out, m_ln1_g, m_ln1_b, m_ffn_w_gate, m_ffn_w_up, m_ffn_conv_w, m_ffn_conv_b, m_ffn_w_down, m_ln2_g, m_ln2_b, v_w_ada, v_b_ada, v_w_in, v_rg_conv_w, v_rg_conv_b, v_rg_w_a, v_rg_b_a, v_rg_w_x, v_rg_b_x, v_rg_lambda, v_dn_conv_w, v_dn_a_log, v_dn_dt_bias, v_dn_norm_w, v_w_proj_a, v_w_proj_b, v_w_out, v_ln1_g, v_ln1_b, v_ffn_w_gate, v_ffn_w_up, v_ffn_conv_w, v_ffn_conv_b, v_ffn_w_down, v_ln2_g, v_ln2_b):
    names = ['w_ada', 'b_ada', 'w_in', 'rg_conv_w', 'rg_conv_b', 'rg_w_a', 'rg_b_a', 'rg_w_x', 'rg_b_x', 'rg_lambda',
             'dn_conv_w', 'dn_a_log', 'dn_dt_bias', 'dn_norm_w', 'w_proj_a', 'w_proj_b', 'w_out', 'ln1_g', 'ln1_b',
             'ffn_w_gate', 'ffn_w_up', 'ffn_conv_w', 'ffn_conv_b', 'ffn_w_down', 'ln2_g', 'ln2_b']
    loc = locals()
    W = {n: loc[n][0] for n in names}
    M = {n: loc['m_' + n][0] for n in names}
    V = {n: loc['v_' + n][0] for n in names}

    me = 4 * lax.axis_index("x") + 2 * lax.axis_index("y") + lax.axis_index("c")
    xs, tgt = x[0], loss_target[0]
    t, d = xs.shape
    d_rnn = W['rg_conv_b'].shape[0]
    n_blk = W['rg_w_a'].shape[0]
    n_vh = W['dn_a_log'].shape[0]
    assert W['dn_norm_w'].shape[0] == LANES
    vdim = n_vh * LANES
    d_ff = W['ffn_conv_b'].shape[0]
    d_in = W['w_in'].shape[1] * N_DEV
    qk = (d_in - 2 * d_rnn - 2 * vdim - 2 * n_vh - 2 * d) // 2
    assert vdim == 2 * qk and qk % LANES == 0 and n_vh <= LANES
    splits = (d_rnn, d_rnn, qk, qk, vdim, vdim, n_vh, n_vh, d, d)
    offs = [0]
    for s_ in splits:
        offs.append(offs[-1] + s_)

    tb = _tile(t, 256, SUBLANES)

    big = ['w_in', 'w_proj_a', 'w_proj_b', 'w_out', 'ffn_w_gate', 'ffn_w_up', 'ffn_w_down']
    small_sh = ['rg_conv_w', 'dn_conv_w', 'ffn_conv_w']
    first = all_gather("gather_first", [W['w_in'].astype(WIRE_DTYPE)] + [W[n] for n in small_sh] + [c])
    g_in, g_rcw, g_dcw, g_fcw, c_all = first
    cols = lambda g: jnp.transpose(g, (1, 0, 2)).reshape(g.shape[1], -1)
    rows = lambda g: g.reshape(-1, g.shape[2])
    w_in_f = cols(g_in)
    padl = lambda a: jnp.pad(a, ((0, 0), (0, LANES - a.shape[1])))
    groups = [w_in_f[:, offs[i]:offs[i + 1]] for i in range(10)]
    groups[6], groups[7] = padl(groups[6]), padl(groups[7])
    go = [0]
    for g_ in groups:
        go.append(go[-1] + g_.shape[1])
    n_pad = _round_up(go[-1], 512)
    wp = jnp.pad(jnp.concatenate(groups, axis=1), ((0, 0), (0, n_pad - go[-1])))
    o_xr, o_gr, o_q, o_k, o_v, o_z, o_a, o_b, o_ga, o_gb = go[:10]
    rcw, dcw, fcw = cols(g_rcw), cols(g_dcw), cols(g_fcw)
    eye_b = jnp.eye(n_blk, dtype=F32)
    bd = lambda w: (w[:, :, None, :] * eye_b[:, None, :, None]).reshape(d_rnn, d_rnn)
    w_bd = jnp.concatenate([bd(W['rg_w_a']), bd(W['rg_w_x'])], axis=1)
    row1 = lambda a: a.reshape(1, -1)
    padv = lambda a: jnp.pad(row1(a), ((0, 0), (0, LANES - a.shape[0])))
    nw_t = jnp.tile(row1(W['dn_norm_w']), (1, n_vh))

    c_pad =jnp.pad(c_all.reshape(N_DEV, d), ((0, LANES - N_DEV), (0, 0)))
    ada_w = W['w_ada'].shape[1]
    b_ada_me = lax.dynamic_slice(W['b_ada'], (me * ada_w,), (ada_w,)).reshape(1, ada_w)
    ada_sh = mm(c_pad, W['w_ada'], name="ada_fwd", a_act="silu", bias=b_ada_me)
    (ada_all,) = all_gather("gather_ada", [ada_sh[:N_DEV]])
    ada_me = lax.dynamic_slice(ada_all, (0, me, 0), (N_DEV, 1, ada_w)).reshape(6, 1, d)
    sh1, sc1, gt1, sh2, sc2, gt2 = [ada_me[i] for i in range(6)]

    nt = t // tb

    def act(a, bw, col0=0, width=None, grad=True, rows=tb):
        width = a.shape[1] if width is None else width
        assert col0 % bw == 0 and width % bw == 0
        c0 = col0 // bw
        return In(a, (rows, bw), lambda o, s: (s, c0 + o), grad=grad, gshape=(t, width), gimap=lambda o, s: (s, o))

    def prm(a, bw, parts=None):
        return In(a, (a.shape[0], bw), lambda o, s: (0, o), acc=True, parts=parts)

    def out(width, bw, rows=tb):
        return Out((t, width), (rows, bw), lambda o, s: (s, o))

    tbh = _tile(t, 1024, SUBLANES)
    nth = t // tbh
    tr = lambda a: jnp.transpose(a).astype(MXU_DTYPE)

    krows = lambda k_: [(slice(j, j + 1), slice(None)) for j in range(k_)]

    mod1_ins = [act(xs, d), prm(sc1, d), prm(sh1, d)]
    (h1,), _ = stage_fwd("mod1_fwd", f_modulate, (1, nt), mod1_ins, [out(d, d)])
    proj, g_pa, g_pb, g_out, g_fg, g_fu, g_fd = mm(h1, wp, name="proj_fwd",
                                                   gather=[W[n].astype(WIRE_DTYPE) for n in big[1:]])
    w_pa, w_pb, w_o, w_fd = rows(g_pa), rows(g_pb), rows(g_out), rows(g_fd)
    w_gu = jnp.concatenate([cols(g_fg), cols(g_fu)], axis=1)

    cb_r = _tile(math.gcd(d_rnn, o_gr), 256)
    rgc_ins = [act(proj, cb_r, o_xr, d_rnn), prm(rcw, cb_r, krows(4)), prm(row1(W['rg_conv_b']), cb_r)]
    rgc_grid, rgc_car = (d_rnn // cb_r, nt), [(SUBLANES, cb_r)]
    (xc,), rgc_hist = stage_fwd("rg_conv_fwd", f_rg_conv, rgc_grid, rgc_ins, [out(d_rnn, cb_r)], rgc_car)
    gates = mm(xc, w_bd, name="rg_gates_fwd")
    lru_ins = [act(xc, cb_r), act(gates, cb_r, 0, d_rnn), act(gates, cb_r, d_rnn, d_rnn), act(proj, cb_r, o_gr, d_rnn),
               prm(row1(W['rg_b_a']), cb_r), prm(row1(W['rg_b_x']), cb_r), prm(row1(W['rg_lambda']), cb_r)]
    lru_car = [(1, cb_r)]
    (rec,), lru_hist = stage_fwd("rglru_fwd", f_rglru, rgc_grid, lru_ins, [out(d_rnn, cb_r)], lru_car)
    y_a = mm(rec, w_pa, name="proj_a_fwd")

    qkv_w = 2 * qk + vdim
    cb_q = _tile(math.gcd(qkv_w, o_q), 256)
    dnc_ins = [act(proj, cb_q, o_q, qkv_w), prm(dcw, cb_q, krows(4))]
    dnc_grid, dnc_car = (qkv_w // cb_q, nt), [(SUBLANES, cb_q)]
    (qkv_c,), dnc_hist = stage_fwd("dn_conv_fwd", f_dn_conv, dnc_grid, dnc_ins, [out(qkv_w, cb_q)], dnc_car)
    f_qnorm, f_knorm = functools.partial(f_l2norm, LANES ** -0.5), functools.partial(f_l2norm, 1.0)
    qn_ins, kn_ins = [act(qkv_c, LANES, 0, qk, rows=tbh)], [act(qkv_c, LANES, qk, qk, rows=tbh)]
    nrm_grid, nrm_outs = (qk // LANES, nth), [out(qk, LANES, tbh)]
    (qn,), _ = stage_fwd("dn_qnorm_fwd", f_qnorm, nrm_grid, qn_ins, nrm_outs)
    (kn,), _ = stage_fwd("dn_knorm_fwd", f_knorm, nrm_grid, kn_ins, nrm_outs)
    gate_ins = [act(proj, LANES, o_a, LANES), act(proj, LANES, o_b, LANES),
                prm(padv(W['dn_a_log']), LANES), prm(padv(W['dn_dt_bias']), LANES)]
    gate_outs = [out(LANES, LANES), out(LANES, LANES)]
    (g_dn, beta_dn), _ = stage_fwd("dn_gates_fwd", f_dn_gates, (1, nt), gate_ins, gate_outs)
    n_ch = t // DN_CHUNK
    gt_dn = jnp.transpose(g_dn.reshape(n_ch, DN_CHUNK, LANES)[:, :, :n_vh], (0, 2, 1))
    dn_mid = delta_intra_fwd(qn, kn, qkv_c, 1, g_dn, gt_dn, beta_dn, n_vh)
    o_dn, dn_hist = delta_inter_fwd(*dn_mid, g_dn, n_vh)
    dno_ins = [act(o_dn, LANES, rows=tbh), act(proj, LANES, o_z, vdim, rows=tbh), prm(nw_t, LANES)]
    dno_grid, dno_outs = (n_vh, nth), [out(vdim, LANES, tbh)]
    (dn,), _ = stage_fwd("dn_out_fwd", f_dn_out, dno_grid, dno_ins, dno_outs)
    y_b = mm(dn, w_pb, name="proj_b_fwd")

    cb_m = _tile(math.gcd(math.gcd(d, o_ga), o_gb), 512)
    mrg_ins = [act(proj, cb_m, o_ga, d), act(proj, cb_m, o_gb, d), act(y_a, cb_m), act(y_b, cb_m)]
    mrg_grid = (d // cb_m, nt)
    (merged,), _ = stage_fwd("merge_fwd", f_merge, mrg_grid, mrg_ins, [out(d, cb_m)])
    mix = mm(merged, w_o, name="w_out_fwd")
    ln1_ins = [act(xs, d), act(mix, d), prm(gt1, d), prm(row1(W['ln1_g']), d), prm(row1(W['ln1_b']), d)]
    (x1,), _ = stage_fwd("ln1_fwd", f_deepnorm, (1, nt), ln1_ins, [out(d, d)])

    mod2_ins = [act(x1, d), prm(sc2, d), prm(sh2, d)]
    (h2,), _ = stage_fwd("mod2_fwd", f_modulate, (1, nt), mod2_ins, [out(d, d)])
    gu = mm(h2, w_gu, name="ffn_in_fwd")
    cb_f = _tile(d_ff, 256)
    ffa_ins = [act(gu, cb_f, 0, d_ff), act(gu, cb_f, d_ff, d_ff), prm(fcw, cb_f, krows(3)), prm(row1(W['ffn_conv_b']), cb_f)]
    ffa_grid, ffa_car = (d_ff // cb_f, nt), [(SUBLANES, cb_f)]
    (act_ff,), ffa_hist = stage_fwd("ffn_act_fwd", f_ffn_act, ffa_grid, ffa_ins, [out(d_ff, cb_f)], ffa_car)
    ff = mm(act_ff, w_fd, name="ffn_down_fwd")
    ln2_ins = [act(x1, d), act(ff, d), prm(gt2, d), prm(row1(W['ln2_g']), d), prm(row1(W['ln2_b']), d)]
    (x2,), _ = stage_fwd("ln2_fwd", f_deepnorm, (1, nt), ln2_ins, [out(d, d)])
    dy, loss_loc = loss_head(x2, tgt, tb)

    dx1_a, d_ff_o, d_gt2, d_ln2g, d_ln2b = stage_bwd("ln2_bwd", f_deepnorm, (1, nt), ln2_ins, [out(d, d)], [dy])
    d_act = mm(d_ff_o, w_fd, name="ffn_down_bwd_x", tb=True)
    gw_fd = mm(tr(act_ff), d_ff_o, name="ffn_down_bwd_w")
    d_gp, d_up, d_fcw, d_fcb = stage_bwd("ffn_act_bwd", f_ffn_act, ffa_grid, ffa_ins, [out(d_ff, cb_f)], [d_act],
                                         ffa_car, ffa_hist, gdtypes={0: MXU_DTYPE, 1: MXU_DTYPE})
    d_gu = jnp.concatenate([d_gp, d_up], axis=1)
    col_blocks = lambda g: jnp.transpose(g.reshape(g.shape[0], N_DEV, -1), (1, 0, 2)).astype(WIRE_DTYPE)
    row_blocks = lambda g: g.reshape(N_DEV, -1, g.shape[1]).astype(WIRE_DTYPE)
    big_parts = {}
    d_h2, big_parts['ffn_w_down'] = mm(d_gu, w_gu, name="ffn_in_bwd_x", tb=True, scatter=[row_blocks(gw_fd)])
    gw_gu = mm(tr(h2), d_gu, name="ffn_in_bwd_w")
    d_x1, d_sc2, d_sh2 = stage_bwd("mod2_bwd", f_modulate, (1, nt), mod2_ins, [out(d, d)], [d_h2], add_to={0: dx1_a})
    dx_a, d_mix, d_gt1, d_ln1g, d_ln1b = stage_bwd("ln1_bwd", f_deepnorm, (1, nt), ln1_ins, [out(d, d)], [d_x1])
    d_merged = mm(d_mix, w_o, name="w_out_bwd_x", tb=True)
    gw_o = mm(tr(merged), d_mix, name="w_out_bwd_w")
    d_ga, d_gb, d_ya, d_yb = stage_bwd("merge_bwd", f_merge, mrg_grid, mrg_ins, [out(d, cb_m)], [d_merged],
                                       gdtypes={0: MXU_DTYPE, 1: MXU_DTYPE})
    d_rec = mm(d_ya, w_pa, name="proj_a_bwd_x", tb=True)
    gw_pa = mm(tr(rec), d_ya, name="proj_a_bwd_w")
    d_dn = mm(d_yb, w_pb, name="proj_b_bwd_x", tb=True)
    gw_pb = mm(tr(dn), d_yb, name="proj_b_bwd_w")

    d_o, d_z, d_nwt = stage_bwd("dn_out_bwd", f_dn_out, dno_grid, dno_ins, dno_outs, [d_dn], gdtypes={1: MXU_DTYPE})
    *d_mid, d_g_state = delta_inter_bwd(*dn_mid, g_dn, dn_hist, d_o, n_vh)
    d_qn, d_kn, d_v, d_g_col, d_gt, d_beta = delta_intra_bwd(qn, kn, qkv_c, 1, g_dn, gt_dn, beta_dn, d_mid, n_vh)
    d_g_row = jnp.pad(jnp.transpose(d_gt, (0, 2, 1)).reshape(t, n_vh), ((0, 0), (0, LANES - n_vh)))
    (d_qc,) = stage_bwd("dn_qnorm_bwd", f_qnorm, nrm_grid, qn_ins, nrm_outs, [d_qn])
    (d_kc,) = stage_bwd("dn_knorm_bwd", f_knorm, nrm_grid, kn_ins, nrm_outs, [d_kn])
    d_a, d_b, d_alog, d_dtb = stage_bwd("dn_gates_bwd", f_dn_gates, (1, nt), gate_ins, gate_outs,
                                        [(d_g_state, d_g_col, d_g_row), d_beta], gdtypes={0: MXU_DTYPE, 1: MXU_DTYPE})
    d_qkv_c = jnp.concatenate([d_qc, d_kc, d_v], axis=1)
    d_qkv, d_dcw = stage_bwd("dn_conv_bwd", f_dn_conv, dnc_grid, dnc_ins, [out(qkv_w, cb_q)], [d_qkv_c],
                             dnc_car, dnc_hist, gdtypes={0: MXU_DTYPE})

    d_xc_a, d_pr, d_pi, d_gr, d_ba, d_bx, d_lam = stage_bwd(
        "rglru_bwd", f_rglru, rgc_grid, lru_ins, [out(d_rnn, cb_r)], [d_rec], lru_car, lru_hist,
        gdtypes={1: MXU_DTYPE, 2: MXU_DTYPE, 3: MXU_DTYPE})
    d_gates = jnp.concatenate([d_pr, d_pi], axis=1)
    (d_xc_b, big_parts['ffn_w_gate'], big_parts['ffn_w_up'], big_parts['w_out'], big_parts['w_proj_a'],
     big_parts['w_proj_b']) = mm(d_gates, w_bd, name="rg_gates_bwd_x", tb=True,
                                 scatter=[col_blocks(gw_gu[:, :d_ff]), col_blocks(gw_gu[:, d_ff:]), row_blocks(gw_o),
                                          row_blocks(gw_pa), row_blocks(gw_pb)])
    gw_bd = mm(tr(xc), d_gates, name="rg_gates_bwd_w")
    d_xr, d_rcw, d_rcb = stage_bwd("rg_conv_bwd", f_rg_conv, rgc_grid, rgc_ins, [out(d_rnn, cb_r)], [(d_xc_a, d_xc_b)],
                                   rgc_car, rgc_hist, gdtypes={0: MXU_DTYPE})

    diag = lambda g: jnp.einsum('nimj,nm->nij', g.reshape(n_blk, d_rnn // n_blk, n_blk, d_rnn // n_blk), eye_b)
    small_names = ['rg_conv_w', 'rg_conv_b', 'rg_w_a', 'rg_b_a', 'rg_w_x', 'rg_b_x', 'rg_lambda', 'dn_conv_w',
                   'dn_a_log', 'dn_dt_bias', 'dn_norm_w', 'ln1_g', 'ln1_b', 'ffn_conv_w', 'ffn_conv_b', 'ln2_g', 'ln2_b']
    small_loc = {
        'rg_conv_w': d_rcw, 'rg_conv_b': d_rcb,
        'rg_w_a': diag(gw_bd[:, :d_rnn]), 'rg_b_a': d_ba, 'rg_w_x': diag(gw_bd[:, d_rnn:]), 'rg_b_x': d_bx,
        'rg_lambda': d_lam, 'dn_conv_w': d_dcw, 'dn_a_log': d_alog[:, :n_vh], 'dn_dt_bias': d_dtb[:, :n_vh],
        'dn_norm_w': jnp.sum(d_nwt.reshape(n_vh, LANES), axis=0), 'ln1_g': d_ln1g, 'ln1_b': d_ln1b,
        'ffn_conv_w': d_fcw, 'ffn_conv_b': d_fcb, 'ln2_g': d_ln2g, 'ln2_b': d_ln2b}
    small_list = [small_loc[n] for n in small_names]

    d_proj = jnp.concatenate([d_xr, d_gr, d_qkv, d_z, d_a, d_b, d_ga, d_gb,
                              jnp.zeros((t, n_pad - go[-1]), MXU_DTYPE)], axis=1)
    gw_p, small_all = mm(tr(h1), d_proj, name="proj_bwd_w", gather=[_pack(small_list)])
    gw_in = jnp.concatenate([gw_p[:, go[i]:go[i] + splits[i]] for i in range(10)], axis=1)
    d_h1, big_parts['w_in'] = mm(d_proj, wp, name="proj_bwd_x", tb=True, scatter=[col_blocks(gw_in)])
    grad_x, d_sc1, d_sh1 = stage_bwd("mod1_bwd", f_modulate, (1, nt), mod1_ins, [out(d, d)], [d_h1], add_to={0: dx_a})

    g_small = dict(zip(small_names, _unpack(sum_parts("sum_small_grads", small_all), small_list)))
    d_ada_me = jnp.concatenate([d_sh1, d_sc1, d_gt1, d_sh2, d_sc2, d_gt2], axis=1)
    (d_ada_all,) = all_gather("gather_d_ada", [d_ada_me.reshape(-1, LANES)])
    g_small['b_ada'] = sum_parts("sum_d_ada", d_ada_all)
    small_names = ['b_ada'] + small_names
    d_ada_cols = lax.dynamic_slice(d_ada_all.reshape(N_DEV, 6 * d), (0, me * ada_w), (N_DEV, ada_w))
    d_ada_pad = jnp.pad(d_ada_cols, ((0, LANES - N_DEV), (0, 0)))
    gw_ada = mm(c_pad, d_ada_pad, name="ada_bwd_w", ta=True, a_act="silu")

    res = {}
    big_parts['w_ada'] = gw_ada[None]
    for n in ['w_ada'] + big:
        res[n] = adamw("adamw_" + n, W[n], big_parts[n], M[n], V[n])
    for n in small_sh:
        w_ = W[n].shape[1]
        g_small[n] = lax.dynamic_slice(g_small[n], (0, me * w_), (W[n].shape[0], w_))
    for n in small_names:
        g_small[n] = g_small[n].reshape(W[n].shape)
    pk = lambda dct: _pack([dct[n] for n in small_names])
    s_g, s_d, s_m, s_v = adamw("adamw_small", pk(W), pk(g_small)[None], pk(M), pk(V))
    like = [W[n] for n in small_names]
    for n, g_, d_, m_, v_ in zip(small_names, _unpack(s_g, like), _unpack(s_d, like), _unpack(s_m, like), _unpack(s_v, like)):
        res[n] = (g_, d_, m_, v_)

    loss = lax.psum(loss_loc[0, 0], ("x", "y", "c"))
    outs = [loss, grad_x[None]]
    for j in range(4):
        outs += [res[n][j].reshape(loc[n].shape) for n in names]
    return tuple(outs)
```

```python
import functools
import math

import jax
import jax.numpy as jnp
from jax import lax
from jax.experimental import pallas as pl
from jax.experimental.pallas import tpu as pltpu

F32 = jnp.float32
BF16 = jnp.bfloat16
MXU_DTYPE = BF16
WIRE_DTYPE = BF16
DN_DTYPE = BF16
HI = lax.Precision.HIGHEST
MESH = pl.DeviceIdType.MESH

N_DEV = 8
LANES = 128
SUBLANES = 8
VMEM_LIMIT = 56 * 1024 * 1024

RG_C = 8.0
DN_CHUNK = 64
DN_HEAD_GROUP = 8
LN_EPS = 1e-5
RMS_EPS = 1e-6
L2_EPS = 1e-6
DEPTH = 1
DEEPNORM_ALPHA = (2 * DEPTH) ** 0.25
ADAM_LR = 0.001
ADAM_B1 = 0.9
ADAM_B2 = 0.999
ADAM_EPS = 1e-08
ADAM_WD = 0.01
ADAM_STEP = 10


def _tile(n, cap, unit=LANES):
    best = None
    for t in range(unit, min(n, cap) + 1, unit):
        if n % t == 0:
            best = t
    return best if best is not None else n


def _round_up(n, m):
    return (n + m - 1) // m * m


_HBM = pl.BlockSpec(memory_space=pl.ANY)


def _exchange_sems(n):
    return [pltpu.SemaphoreType.DMA((n, N_DEV - 1)), pltpu.SemaphoreType.DMA((n, N_DEV - 1)),
            pltpu.SemaphoreType.DMA((n,))]


def _exchange_out_shape(arrs, scatter):
    return [jax.ShapeDtypeStruct(a.shape if scatter else (N_DEV,) + a.shape, a.dtype) for a in arrs]


def _exchange_copies(in_refs, out_refs, sems, scatter, phase):
    send_sems, recv_sems, local_sems = sems
    x, y, c = lax.axis_index("x"), lax.axis_index("y"), lax.axis_index("c")
    me = 4 * x + 2 * y + c
    peers = [(x ^ ((k >> 2) & 1), y ^ ((k >> 1) & 1), c ^ (k & 1)) for k in range(N_DEV)]
    lin = lambda p: 4 * p[0] + 2 * p[1] + p[2]
    block = lambda i, dev: in_refs[i].at[dev] if scatter else in_refs[i]

    def local(i):
        return pltpu.make_async_copy(block(i, me), out_refs[i].at[me], local_sems.at[i])

    def remote(i, k, src_dev, dst_row):
        return pltpu.make_async_remote_copy(src_ref=block(i, src_dev), dst_ref=out_refs[i].at[dst_row],
                                            send_sem=send_sems.at[i, k - 1], recv_sem=recv_sems.at[i, k - 1],
                                            device_id=peers[k], device_id_type=MESH)

    pairs = [(i, k) for k in range(1, N_DEV) for i in range(len(in_refs))]
    if phase == "start":
        for i in range(len(in_refs)):
            local(i).start()
        for i, k in pairs:
            remote(i, k, lin(peers[k]), me).start()
    else:
        for i, k in pairs:
            remote(i, k, me, lin(peers[k])).wait_recv()
        for i, k in pairs:
            remote(i, k, lin(peers[k]), me).wait_send()
        for i in range(len(in_refs)):
            local(i).wait()


def mm(a, b, *, name, ta=False, tb=False, a_act=None, bias=None, out_dtype=F32,
       tm_cap=1024, tn_cap=512, tk_cap=1024, gather=(), scatter=()):
    a_segs = list(a) if isinstance(a, (list, tuple)) else [a]
    b_segs = list(b) if isinstance(b, (list, tuple)) else [b]
    ns = len(a_segs)
    assert ns == len(b_segs) and (ns == 1 or a_act is None)
    m = a_segs[0].shape[1] if ta else a_segs[0].shape[0]
    n = b_segs[0].shape[0] if tb else b_segs[0].shape[1]
    ks = [x.shape[0] if ta else x.shape[1] for x in a_segs]
    assert ks == [y.shape[1] if tb else y.shape[0] for y in b_segs], (ks, ta, tb)
    tm, tn = _tile(m, tm_cap), _tile(n, tn_cap)
    tks = [_tile(k_, tk_cap) for k_ in ks]
    cnt = [k_ // t_ for k_, t_ in zip(ks, tks)]
    lo = [sum(cnt[:s]) for s in range(ns)]
    nk = sum(cnt)
    grid = (m // tm, n // tn, nk)
    dims = (((0 if ta else 1,), (1 if tb else 0,)), ((), ()))
    xch = list(gather) + list(scatter)
    nx, ng = len(xch), len(gather)
    n_main = 2 * ns + (bias is not None)

    def body(*refs):
        a_refs, b_refs = refs[:ns], refs[ns:2 * ns]
        bias_ref = refs[2 * ns] if bias is not None else None
        x_in, o_ref, x_out = refs[n_main:n_main + nx], refs[n_main + nx], refs[n_main + nx + 1:n_main + 2 * nx + 1]
        rest = refs[n_main + 2 * nx + 1:]
        acc_ref = rest[0] if nk > 1 else None
        sems = rest[1 if nk > 1 else 0:]
        groups = []
        if ng:
            groups.append((x_in[:ng], x_out[:ng], sems[:3], False))
        if nx > ng:
            groups.append((x_in[ng:], x_out[ng:], sems[-3:], True))
        if nx:
            step = (pl.program_id(0) * grid[1] + pl.program_id(1)) * grid[2] + pl.program_id(2)

            @pl.when(step == 0)
            def _():
                for gi, go_, gs, sc in groups:
                    _exchange_copies(gi, go_, gs, sc, "start")
        kk = pl.program_id(2)

        def finish(r):
            if bias is not None:
                r = r + bias_ref[...]
            o_ref[...] = r.astype(o_ref.dtype)

        def segment(s):
            av = a_refs[s][...]
            if a_act == "silu":
                av = jax.nn.silu(av.astype(F32))
            prod = lax.dot_general(av.astype(MXU_DTYPE), b_refs[s][...].astype(MXU_DTYPE), dims,
                                   preferred_element_type=F32)
            if nk == 1:
                finish(prod)
                return
            opens, closes = lo[s] == 0, lo[s] + cnt[s] == nk
            if opens:
                @pl.when(kk == 0)
                def _():
                    acc_ref[...] = prod
            inner = [kk > 0] * opens + [kk < nk - 1] * closes
            if inner:
                @pl.when(functools.reduce(lambda p, q: p & q, inner))
                def _():
                    acc_ref[...] += prod
            else:
                acc_ref[...] += prod
            if closes:
                @pl.when(kk == nk - 1)
                def _():
                    finish(acc_ref[...] + prod)

        for s in range(ns):
            if ns == 1:
                segment(s)
            else:
                pl.when((kk >= lo[s]) & (kk < lo[s] + cnt[s]))(functools.partial(segment, s))

        if nx:
            @pl.when(step == grid[0] * grid[1] * grid[2] - 1)
            def _():
                for gi, go_, gs, sc in groups:
                    _exchange_copies(gi, go_, gs, sc, "wait")

    def seg_index(s):
        return lambda q: jnp.clip(q - lo[s], 0, cnt[s] - 1) if ns > 1 else q

    a_specs, b_specs = [], []
    for s in range(ns):
        qi, tk = seg_index(s), tks[s]
        a_specs.append(pl.BlockSpec((tk, tm), (lambda qi: lambda i, j, q: (qi(q), i))(qi)) if ta
                       else pl.BlockSpec((tm, tk), (lambda qi: lambda i, j, q: (i, qi(q)))(qi)))
        b_specs.append(pl.BlockSpec((tn, tk), (lambda qi: lambda i, j, q: (j, qi(q)))(qi)) if tb
                       else pl.BlockSpec((tk, tn), (lambda qi: lambda i, j, q: (qi(q), j))(qi)))
    in_specs, args = a_specs + b_specs, a_segs + b_segs
    if bias is not None:
        in_specs.append(pl.BlockSpec((1, tn), lambda i, j, q: (0, j)))
        args.append(bias)
    o_spec, o_shape = pl.BlockSpec((tm, tn), lambda i, j, q: (i, j)), jax.ShapeDtypeStruct((m, n), out_dtype)
    acc = [pltpu.VMEM((tm, tn), F32)] if nk > 1 else []
    if not nx:
        return pl.pallas_call(
            body, name=name, grid=grid, in_specs=in_specs, out_specs=o_spec, out_shape=o_shape, scratch_shapes=acc,
            compiler_params=pltpu.CompilerParams(dimension_semantics=("parallel", "parallel", "arbitrary"),
                                                 vmem_limit_bytes=VMEM_LIMIT),
        )(*args)
    return pl.pallas_call(
        body, name=name, grid=grid, in_specs=in_specs + [_HBM] * nx, out_specs=[o_spec] + [_HBM] * nx,
        out_shape=[o_shape] + _exchange_out_shape(list(gather), False) + _exchange_out_shape(list(scatter), True),
        scratch_shapes=acc + (_exchange_sems(ng) if ng else []) + (_exchange_sems(nx - ng) if nx > ng else []),
        compiler_params=pltpu.CompilerParams(dimension_semantics=("arbitrary", "arbitrary", "arbitrary"),
                                             vmem_limit_bytes=VMEM_LIMIT, has_side_effects=True),
    )(*args, *xch)


class In:
    def __init__(self, arr, block, imap, acc=False, grad=True, parts=None, gshape=None, gimap=None):
        self.arr, self.block, self.imap, self.acc, self.grad, self.parts = arr, block, imap, acc, grad, parts
        self.gshape = arr.shape if gshape is None else gshape
        self.gimap = imap if gimap is None else gimap


class Out:
    def __init__(self, shape, block, imap, dtype=F32):
        self.shape, self.block, self.imap, self.dtype = shape, block, imap, dtype


def _load(in_refs, ins):
    vals = []
    for r, i in zip(in_refs, ins):
        if i.parts is None:
            vals.append(r[...])
        else:
            vals.extend(r[p] for p in i.parts)
    return vals


def _stage_params():
    return pltpu.CompilerParams(dimension_semantics=("parallel", "arbitrary"), vmem_limit_bytes=VMEM_LIMIT)


def stage_fwd(name, f, grid, ins, outs, carries=(), transposed=()):
    n_in, n_out, n_c, n_t = len(ins), len(outs), len(carries), len(transposed)

    def body(*refs):
        in_refs, out_refs = refs[:n_in], refs[n_in:n_in + n_out]
        hist_refs = refs[n_in + n_out:n_in + n_out + n_c]
        t_refs = refs[n_in + n_out + n_c:n_in + n_out + n_c + n_t]
        c_refs = refs[n_in + n_out + n_c + n_t:]
        if n_c:
            @pl.when(pl.program_id(1) == 0)
            def _():
                for c in c_refs:
                    c[...] = jnp.zeros_like(c)
        cin = [c[...] for c in c_refs]
        for h, c in zip(hist_refs, cin):
            h[...] = c
        o, cout = f(*_load(in_refs, ins), *cin)
        for r, v in zip(out_refs, o):
            r[...] = v.astype(r.dtype)
        for r, k in zip(t_refs, transposed):
            r[...] = o[k].T.astype(r.dtype)
        for c, v in zip(c_refs, cout):
            c[...] = v

    hist_spec = lambda c: pl.BlockSpec((None, None) + tuple(c), lambda o, s: (o, s) + (0,) * len(c))
    flip = lambda o_: pl.BlockSpec(o_.block[::-1], (lambda im: lambda o, s: im(o, s)[::-1])(o_.imap))
    res = pl.pallas_call(
        body, name=name, grid=grid,
        in_specs=[pl.BlockSpec(i.block, i.imap) for i in ins],
        out_specs=[pl.BlockSpec(o.block, o.imap) for o in outs] + [hist_spec(c) for c in carries]
        + [flip(outs[k]) for k in transposed],
        out_shape=[jax.ShapeDtypeStruct(o.shape, o.dtype) for o in outs]
        + [jax.ShapeDtypeStruct(tuple(grid) + tuple(c), F32) for c in carries]
        + [jax.ShapeDtypeStruct(outs[k].shape[::-1], MXU_DTYPE) for k in transposed],
        scratch_shapes=[pltpu.VMEM(tuple(c), F32) for c in carries],
        compiler_params=_stage_params(),
    )(*[i.arr for i in ins])
    res = list(res)
    if transposed:
        return res[:n_out], res[n_out:n_out + n_c], res[n_out + n_c:]
    return res[:n_out], res[n_out:]


def stage_bwd(name, f, grid, ins, outs, cots, carries=(), hists=(), add_to=None, gdtypes=None):
    n_in, n_out, n_c = len(ins), len(outs), len(carries)
    ns = grid[1]
    add_to = add_to or {}
    gdtypes = gdtypes or {}
    add_idx = sorted(add_to)
    g_idx = [k for k, i in enumerate(ins) if i.grad]
    cots = [c if isinstance(c, (tuple, list)) else (c,) for c in cots]
    n_cot = [len(c) for c in cots]
    rev = lambda imap: (lambda o, s: imap(o, ns - 1 - s))

    def body(*refs):
        p = 0
        in_refs = refs[p:p + n_in]; p += n_in
        cot_refs = []
        for cnt in n_cot:
            cot_refs.append(refs[p:p + cnt]); p += cnt
        hist_refs = refs[p:p + n_c]; p += n_c
        add_refs = refs[p:p + len(add_idx)]; p += len(add_idx)
        g_refs = refs[p:p + len(g_idx)]; p += len(g_idx)
        dc_refs = refs[p:]
        first = pl.program_id(1) == 0
        if n_c:
            @pl.when(first)
            def _():
                for c in dc_refs:
                    c[...] = jnp.zeros_like(c)
        vals = _load(in_refs, ins)
        cin = [h[...] for h in hist_refs]
        (o, cout), vjp = jax.vjp(lambda *a: f(*a), *vals, *cin)
        cot_o = []
        for crs, v in zip(cot_refs, o):
            c = crs[0][...].astype(v.dtype)
            for extra in crs[1:]:
                c = c + extra[...].astype(v.dtype)
            cot_o.append(c)
        cot_c = tuple(c[...] for c in dc_refs)
        grads = vjp((tuple(cot_o), cot_c))
        pos, per_in = 0, []
        for i in ins:
            cnt = 1 if i.parts is None else len(i.parts)
            per_in.append(grads[pos:pos + cnt])
            pos += cnt
        dcin = grads[pos:]
        for gr, k in zip(g_refs, g_idx):
            i, gs = ins[k], per_in[k]
            if i.acc:
                @pl.when(first)
                def _(gr=gr):
                    gr[...] = jnp.zeros_like(gr)
                if i.parts is None:
                    gr[...] += gs[0].astype(gr.dtype)
                else:
                    for pt, g in zip(i.parts, gs):
                        gr[pt] += g.astype(gr.dtype)
            else:
                g = gs[0]
                if k in add_to:
                    g = g + add_refs[add_idx.index(k)][...].astype(g.dtype)
                gr[...] = g.astype(gr.dtype)
        for c, v in zip(dc_refs, dcin):
            c[...] = v

    in_specs = [pl.BlockSpec(i.block, rev(i.imap)) for i in ins]
    for o_, cnt in zip(outs, n_cot):
        in_specs += [pl.BlockSpec(o_.block, rev(o_.imap))] * cnt
    in_specs += [pl.BlockSpec((None, None) + tuple(c), (lambda c: (lambda o, s: (o, ns - 1 - s) + (0,) * len(c)))(c))
                 for c in carries]
    in_specs += [pl.BlockSpec(ins[k].block, rev(ins[k].gimap)) for k in add_idx]
    out_specs, out_shape = [], []
    for k in g_idx:
        i = ins[k]
        if i.acc:
            out_specs.append(pl.BlockSpec(i.block, (lambda im: (lambda o, s: im(o, 0)))(i.imap)))
        else:
            out_specs.append(pl.BlockSpec(i.block, rev(i.gimap)))
        out_shape.append(jax.ShapeDtypeStruct(i.gshape, gdtypes.get(k, F32)))
    res = pl.pallas_call(
        body, name=name, grid=grid, in_specs=in_specs, out_specs=out_specs, out_shape=out_shape,
        scratch_shapes=[pltpu.VMEM(tuple(c), F32) for c in carries],
        compiler_params=_stage_params(),
    )(*[i.arr for i in ins], *[a for c in cots for a in c], *hists, *[add_to[k] for k in add_idx])
    return list(res)


def _iota_rows(shape):
    return lax.broadcasted_iota(jnp.int32, shape, 0)


@functools.partial(jax.custom_vjp, nondiff_argnums=(1,))
def _roll_rows(x, s):
    return pltpu.roll(x, s % x.shape[0], 0)


def _roll_rows_fwd(x, s):
    return _roll_rows(x, s), None


def _roll_rows_bwd(s, _, g):
    return (_roll_rows(g, -s),)


_roll_rows.defvjp(_roll_rows_fwd, _roll_rows_bwd)


@jax.custom_vjp
def _drop_head(xx):
    return xx[SUBLANES:]


def _drop_head_fwd(xx):
    return xx[SUBLANES:], None


def _drop_head_bwd(_, g):
    return (jnp.concatenate([jnp.zeros((SUBLANES, g.shape[1]), g.dtype), g], axis=0),)


_drop_head.defvjp(_drop_head_fwd, _drop_head_bwd)


@jax.custom_vjp
def _last_rows(x):
    return x[x.shape[0] - SUBLANES:]


def _last_rows_fwd(x):
    return x[x.shape[0] - SUBLANES:], x.shape[0]


def _last_rows_bwd(n, g):
    return (jnp.concatenate([jnp.zeros((n - SUBLANES, g.shape[1]), g.dtype), g], axis=0),)


_last_rows.defvjp(_last_rows_fwd, _last_rows_bwd)


def _last_row(x):
    n = x.shape[0]
    return jnp.sum(jnp.where(_iota_rows(x.shape) == n - 1, x, 0.0), axis=0, keepdims=True)


def _scan_steps(n):
    s = 1
    while s < n:
        yield s
        s *= 2


def _block_scan_impl(a, u, h0):
    n = a.shape[0]
    row = _iota_rows(a.shape)
    for s in _scan_steps(n):
        keep = row >= s
        a_s = jnp.where(keep, pltpu.roll(a, s, 0), 1.0)
        u_s = jnp.where(keep, pltpu.roll(u, s, 0), 0.0)
        u = u + a * u_s
        a = a * a_s
    return u + a * h0


@jax.custom_vjp
def _block_scan(a, u, h0):
    return _block_scan_impl(a, u, h0)


def _block_scan_fwd(a, u, h0):
    h = _block_scan_impl(a, u, h0)
    return h, (a, h, h0)


def _block_scan_bwd(res, dh):
    a, h, h0 = res
    n = a.shape[0]
    row = _iota_rows(a.shape)
    b = jnp.where(row < n - 1, pltpu.roll(a, n - 1, 0), 0.0)
    lam = dh
    for s in _scan_steps(n):
        keep = row < n - s
        b_s = jnp.where(keep, pltpu.roll(b, n - s, 0), 1.0)
        l_s = jnp.where(keep, pltpu.roll(lam, n - s, 0), 0.0)
        lam = lam + b * l_s
        b = b * b_s
    h_prev = jnp.where(row >= 1, pltpu.roll(h, 1, 0), jnp.broadcast_to(h0, h.shape))
    d_h0 = jnp.sum(jnp.where(row == 0, a * lam, 0.0), axis=0, keepdims=True)
    return lam * h_prev, lam, d_h0


_block_scan.defvjp(_block_scan_fwd, _block_scan_bwd)


def _dot_hi(a, b, dims=(((1,), (0,)), ((), ()))):
    return lax.dot_general(a, b, dims, precision=HI, preferred_element_type=F32)


_NN, _NT, _TN = "nn", "nt", "tn"
_CONTRACT = {_NN: (1, 0), _NT: (1, 1), _TN: (0, 0)}


def _raw_dot(a, b, kind):
    ca, cb = _CONTRACT[kind]
    lead = a.ndim - 2
    dims = (((ca + lead,), (cb + lead,)), (tuple(range(lead)), tuple(range(lead))))
    return lax.dot_general(a.astype(DN_DTYPE), b.astype(DN_DTYPE), dims, preferred_element_type=F32)


@jax.custom_vjp
def _nn(a, b):
    return _raw_dot(a, b, _NN)


_nn.defvjp(lambda a, b: (_raw_dot(a, b, _NN), (a, b)),
           lambda r, g: (_raw_dot(g, r[1], _NT), _raw_dot(r[0], g, _TN)))


@jax.custom_vjp
def _nt(a, b):
    return _raw_dot(a, b, _NT)


_nt.defvjp(lambda a, b: (_raw_dot(a, b, _NT), (a, b)),
           lambda r, g: (_raw_dot(g, r[1], _NN), _raw_dot(g, r[0], _TN)))


@jax.custom_vjp
def _tn(a, b):
    return _raw_dot(a, b, _TN)


_tn.defvjp(lambda a, b: (_raw_dot(a, b, _TN), (a, b)),
           lambda r, g: (_raw_dot(r[1], g, _NT), _raw_dot(r[0], g, _NN)))


def _neumann_inverse(a):
    n = a.shape[-1]
    eye = (lax.broadcasted_iota(jnp.int32, (n, n), 0) == lax.broadcasted_iota(jnp.int32, (n, n), 1)).astype(F32)
    p = _raw_dot(a, a, _NN)
    e = p
    for _ in range(int(math.log2(n)) - 2):
        p = _raw_dot(p, p, _NN)
        e = e + p + _raw_dot(e, p, _NN)
    return eye - a + e - _raw_dot(a, e, _NN)


@jax.custom_vjp
def _unit_lower_inverse(a):
    return _neumann_inverse(a)


def _unit_lower_inverse_fwd(a):
    x = _neumann_inverse(a)
    return x, x


def _unit_lower_inverse_bwd(x, g):
    return (-_raw_dot(_raw_dot(x, g, _TN), x, _NT),)


_unit_lower_inverse.defvjp(_unit_lower_inverse_fwd, _unit_lower_inverse_bwd)


def _softplus(x):
    return jnp.maximum(x, 0.0) + jnp.log1p(jnp.exp(-jnp.abs(x)))


def _neg_expm1(x):
    series = -x * (1.0 + x * (0.5 + x * (1.0 / 6.0 + x * (1.0 / 24.0 + x * (1.0 / 120.0)))))
    return jnp.where(x > -0.03, series, 1.0 - jnp.exp(x))


def f_modulate(x, sc, sh):
    return (x * (1.0 + sc) + sh,), ()


def f_deepnorm(x, y, gt, g, b):
    v = DEEPNORM_ALPHA * x + (1.0 + gt) * y
    mu = jnp.mean(v, axis=-1, keepdims=True)
    vc = v - mu
    var = jnp.mean(vc * vc, axis=-1, keepdims=True)
    return (vc * lax.rsqrt(var + LN_EPS) * g + b,), ()


def _causal_conv(x, prev, ws):
    xx = jnp.concatenate([prev, x], axis=0)
    k = len(ws)
    y = ws[k - 1] * x
    for j in range(k - 1):
        y = y + ws[j] * _drop_head(_roll_rows(xx, k - 1 - j))
    return y


def f_rg_conv(x, w0, w1, w2, w3, b, prev):
    return (_causal_conv(x, prev, (w0, w1, w2, w3)) + b,), (_last_rows(x),)


def f_dn_conv(x, w0, w1, w2, w3, prev):
    return (jax.nn.silu(_causal_conv(x, prev, (w0, w1, w2, w3))),), (_last_rows(x),)


def f_ffn_act(gp, up, w0, w1, w2, b, prev):
    return (jax.nn.gelu(_causal_conv(gp, prev, (w0, w1, w2)) + b) * up,), (_last_rows(gp),)


def f_rglru(xc, pre_r, pre_i, gr, b_a, b_x, lam, h0):
    gate_r = jax.nn.sigmoid(pre_r + b_a)
    gate_i = jax.nn.sigmoid(pre_i + b_x)
    log_a = -RG_C * gate_r * _softplus(-lam)
    a = jnp.exp(log_a)
    mult = jnp.sqrt(_neg_expm1(2.0 * log_a))
    h = _block_scan(a, mult * gate_i * xc, h0)
    return (h * jax.nn.gelu(gr),), (_last_row(h),)


def f_l2norm(scale, x):
    return (x * lax.rsqrt(jnp.sum(x * x, axis=-1, keepdims=True) + L2_EPS) * scale,), ()


def f_dn_gates(a_in, b_in, a_log, dt_bias):
    g = -jnp.exp(a_log) * _softplus(a_in + dt_bias)
    n = g.shape[0]
    shift = int(math.log2(DN_CHUNK))
    ri = lax.broadcasted_iota(jnp.int32, (n, n), 0)
    ci = lax.broadcasted_iota(jnp.int32, (n, n), 1)
    tri = ((lax.shift_right_logical(ri, shift) == lax.shift_right_logical(ci, shift)) & (ri >= ci)).astype(F32)
    return (_dot_hi(tri, g), jax.nn.sigmoid(b_in)), ()


def f_dn_out(o, z, nw):
    r = lax.rsqrt(jnp.mean(o * o, axis=-1, keepdims=True) + RMS_EPS)
    return (o * r * nw * jax.nn.silu(z),), ()


def f_merge(ga, gb, ya, yb):
    return (jax.nn.sigmoid(ga) * ya + jax.nn.sigmoid(gb) * yb,), ()


def _delta_intra(q, k, v, g_i, g_j, beta):
    c = q.shape[-2]
    ri = lax.broadcasted_iota(jnp.int32, (c, c), 0)
    ci = lax.broadcasted_iota(jnp.int32, (c, c), 1)
    decay = jnp.exp(jnp.where(ri >= ci, g_i - g_j, -jnp.inf))
    g_last = jnp.sum(jnp.where(_iota_rows((c, 1)) == c - 1, g_i, 0.0), axis=-2, keepdims=True)
    exp_g = jnp.exp(g_i)
    kb = k * beta
    t_inv = _unit_lower_inverse(jnp.where(ri > ci, _nt(kb, k) * decay, 0.0))
    u = _nn(t_inv, v * beta)
    w = _nn(t_inv, kb * exp_g)
    return u, w, _nt(q, k) * decay, q * exp_g, k * jnp.exp(g_last - g_i)


def _delta_inter(u, w, qk, q_dec, k_dec, g_last, state):
    v_new = u - _nn(w, state)
    o = _nn(q_dec, state) + _nn(qk, v_new)
    return o, jnp.exp(g_last) * state + _tn(k_dec, v_new)


def _chunk_spec(width, nc=None, col=0):
    if nc is None:
        return pl.BlockSpec((DN_CHUNK, width), lambda s: (s, col))
    return pl.BlockSpec((DN_CHUNK, width), lambda s: (nc - 1 - s, col))


def _delta_params(sem):
    return pltpu.CompilerParams(dimension_semantics=(sem,), vmem_limit_bytes=VMEM_LIMIT)


def _head(ref, h, width=LANES):
    return ref[:, h * LANES:h * LANES + width]


def _head_groups(n_vh):
    hb = min(DN_HEAD_GROUP, n_vh)
    return [range(h0, h0 + hb) for h0 in range(0, n_vh, hb)]


def _stack(hs, f):
    return jnp.stack([f(h) for h in hs])


def _intra_operands(hs, rep, q_ref, k_ref, v_ref, g_ref, gt_ref, b_ref):
    return (_stack(hs, lambda h: _head(q_ref, h // rep)), _stack(hs, lambda h: _head(k_ref, h // rep)),
            _stack(hs, lambda h: _head(v_ref, h)), _stack(hs, lambda h: g_ref[:, h:h + 1]),
            _stack(hs, lambda h: gt_ref[h:h + 1, :]), _stack(hs, lambda h: b_ref[:, h:h + 1]))


def _inter_operands(hs, u_ref, w_ref, qk_ref, qd_ref, kd_ref, g_ref):
    f32 = lambda ref, width=LANES: _stack(hs, lambda h: _head(ref, h, width).astype(F32))
    return (f32(u_ref), f32(w_ref), f32(qk_ref, DN_CHUNK), f32(qd_ref), f32(kd_ref),
            _stack(hs, lambda h: g_ref[DN_CHUNK - 1:DN_CHUNK, h:h + 1]))


def delta_intra_fwd(qn, kn, qkv, v_blk, big_g, big_gt, beta, n_vh):
    t, qk_w = qn.shape
    vdim = n_vh * LANES
    rep = vdim // qk_w
    nc = t // DN_CHUNK

    def body(q_ref, k_ref, v_ref, g_ref, gt_ref, b_ref, u_ref, w_ref, qk_ref, qd_ref, kd_ref):
        for hs in _head_groups(n_vh):
            u, w, qk, qd, kd = _delta_intra(*_intra_operands(hs, rep, q_ref, k_ref, v_ref, g_ref, gt_ref, b_ref))
            for i, h in enumerate(hs):
                sl = slice(h * LANES, (h + 1) * LANES)
                u_ref[:, sl] = u[i]
                w_ref[:, sl] = w[i].astype(w_ref.dtype)
                qk_ref[:, sl] = jnp.concatenate([qk[i], jnp.zeros_like(qk[i])], axis=1).astype(qk_ref.dtype)
                qd_ref[:, sl] = qd[i].astype(qd_ref.dtype)
                kd_ref[:, sl] = kd[i].astype(kd_ref.dtype)

    return pl.pallas_call(
        body, name="delta_intra_fwd", grid=(nc,),
        in_specs=[_chunk_spec(qk_w), _chunk_spec(qk_w), _chunk_spec(vdim, col=v_blk), _chunk_spec(LANES),
                  pl.BlockSpec((None, n_vh, DN_CHUNK), lambda s: (s, 0, 0)), _chunk_spec(LANES)],
        out_specs=[_chunk_spec(vdim)] * 5,
        out_shape=[jax.ShapeDtypeStruct((t, vdim), F32)] + [jax.ShapeDtypeStruct((t, vdim), DN_DTYPE)] * 4,
        compiler_params=_delta_params("parallel"),
    )(qn, kn, qkv, big_g, big_gt, beta)


def delta_inter_fwd(u, w, qk, q_dec, k_dec, big_g, n_vh):
    t, vdim = u.shape
    nc = t // DN_CHUNK

    def body(u_ref, w_ref, qk_ref, qd_ref, kd_ref, g_ref, o_ref, hist_ref, s_ref):
        @pl.when(pl.program_id(0) == 0)
        def _():
            s_ref[...] = jnp.zeros_like(s_ref)
        for hs in _head_groups(n_vh):
            grp = slice(hs[0], hs[-1] + 1)
            st = s_ref[grp]
            hist_ref[grp] = st
            o, ns = _delta_inter(*_inter_operands(hs, u_ref, w_ref, qk_ref, qd_ref, kd_ref, g_ref), st)
            for i, h in enumerate(hs):
                o_ref[:, h * LANES:(h + 1) * LANES] = o[i]
            s_ref[grp] = ns

    return pl.pallas_call(
        body, name="delta_inter_fwd", grid=(nc,),
        in_specs=[_chunk_spec(vdim)] * 5 + [_chunk_spec(LANES)],
        out_specs=[_chunk_spec(vdim), pl.BlockSpec((None, n_vh, LANES, LANES), lambda s: (s, 0, 0, 0))],
        out_shape=[jax.ShapeDtypeStruct((t, vdim), F32), jax.ShapeDtypeStruct((nc, n_vh, LANES, LANES), F32)],
        scratch_shapes=[pltpu.VMEM((n_vh, LANES, LANES), F32)],
        compiler_params=_delta_params("arbitrary"),
    )(u, w, qk, q_dec, k_dec, big_g)


def delta_inter_bwd(u, w, qk, q_dec, k_dec, big_g, hist, d_o, n_vh):
    t, vdim = u.shape
    nc = t // DN_CHUNK

    def body(u_ref, w_ref, qk_ref, qd_ref, kd_ref, g_ref, hist_ref, do_ref,
             du_ref, dw_ref, dqk_ref, dqd_ref, dkd_ref, dg_ref, ds_ref):
        @pl.when(pl.program_id(0) == 0)
        def _():
            ds_ref[...] = jnp.zeros_like(ds_ref)
        lane = lax.broadcasted_iota(jnp.int32, (1, LANES), 1)
        dgl_all = jnp.zeros((1, LANES), F32)
        for hs in _head_groups(n_vh):
            grp = slice(hs[0], hs[-1] + 1)
            prim = _inter_operands(hs, u_ref, w_ref, qk_ref, qd_ref, kd_ref, g_ref) + (hist_ref[grp],)
            _, vjp = jax.vjp(_delta_inter, *prim)
            du, dw, dqk, dqd, dkd, dgl, dst = vjp((_stack(hs, lambda h: _head(do_ref, h)), ds_ref[grp]))
            ds_ref[grp] = dst
            for i, h in enumerate(hs):
                sl = slice(h * LANES, (h + 1) * LANES)
                du_ref[:, sl] = du[i]
                dw_ref[:, sl] = dw[i]
                dqk_ref[:, sl] = jnp.concatenate([dqk[i], jnp.zeros_like(dqk[i])], axis=1)
                dqd_ref[:, sl] = dqd[i]
                dkd_ref[:, sl] = dkd[i]
                dgl_all = dgl_all + dgl[i] * (lane == h).astype(F32)
        last = _iota_rows((DN_CHUNK, LANES)) == DN_CHUNK - 1
        dg_ref[...] = jnp.where(last, jnp.broadcast_to(dgl_all, (DN_CHUNK, LANES)), 0.0)

    rv = lambda w_: _chunk_spec(w_, nc)
    return pl.pallas_call(
        body, name="delta_inter_bwd", grid=(nc,),
        in_specs=[rv(vdim)] * 5 + [rv(LANES), pl.BlockSpec((None, n_vh, LANES, LANES), lambda s: (nc - 1 - s, 0, 0, 0)),
                                   rv(vdim)],
        out_specs=[rv(vdim)] * 5 + [rv(LANES)],
        out_shape=[jax.ShapeDtypeStruct((t, vdim), F32)] * 5 + [jax.ShapeDtypeStruct((t, LANES), F32)],
        scratch_shapes=[pltpu.VMEM((n_vh, LANES, LANES), F32)],
        compiler_params=_delta_params("arbitrary"),
    )(u, w, qk, q_dec, k_dec, big_g, hist, d_o)


def delta_intra_bwd(qn, kn, qkv, v_blk, big_g, big_gt, beta, cots, n_vh):
    t, qk_w = qn.shape
    vdim = n_vh * LANES
    rep = vdim // qk_w
    nc = t // DN_CHUNK

    def body(q_ref, k_ref, v_ref, g_ref, gt_ref, b_ref, du_ref, dw_ref, dqk_ref, dqd_ref, dkd_ref,
             dq_ref, dk_ref, dv_ref, dg_ref, dgt_ref, db_ref):
        lane = lax.broadcasted_iota(jnp.int32, (1, LANES), 1)
        dg_all = jnp.zeros((DN_CHUNK, LANES), F32)
        db_all = jnp.zeros((DN_CHUNK, LANES), F32)
        dq_acc, dk_acc = None, None
        for hs in _head_groups(n_vh):
            _, vjp = jax.vjp(_delta_intra, *_intra_operands(hs, rep, q_ref, k_ref, v_ref, g_ref, gt_ref, b_ref))
            cot = lambda ref, width=LANES: _stack(hs, lambda h: _head(ref, h, width))
            dq, dk, dv, dgi, dgj, db = vjp((cot(du_ref), cot(dw_ref), cot(dqk_ref, DN_CHUNK), cot(dqd_ref), cot(dkd_ref)))
            for i, h in enumerate(hs):
                j = h // rep
                dv_ref[:, h * LANES:(h + 1) * LANES] = dv[i]
                dgt_ref[h:h + 1, :] = dgj[i]
                onehot = (lane == h).astype(F32)
                dg_all = dg_all + dgi[i] * onehot
                db_all = db_all + db[i] * onehot
                dq_acc = dq[i] if h % rep == 0 else dq_acc + dq[i]
                dk_acc = dk[i] if h % rep == 0 else dk_acc + dk[i]
                if h % rep == rep - 1:
                    dq_ref[:, j * LANES:(j + 1) * LANES] = dq_acc
                    dk_ref[:, j * LANES:(j + 1) * LANES] = dk_acc
        dg_ref[...] = dg_all
        db_ref[...] = db_all

    gt_spec = pl.BlockSpec((None, n_vh, DN_CHUNK), lambda s: (s, 0, 0))
    return pl.pallas_call(
        body, name="delta_intra_bwd", grid=(nc,),
        in_specs=[_chunk_spec(qk_w), _chunk_spec(qk_w), _chunk_spec(vdim, col=v_blk), _chunk_spec(LANES), gt_spec,
                  _chunk_spec(LANES)] + [_chunk_spec(vdim)] * 5,
        out_specs=[_chunk_spec(qk_w), _chunk_spec(qk_w), _chunk_spec(vdim), _chunk_spec(LANES), gt_spec,
                   _chunk_spec(LANES)],
        out_shape=[jax.ShapeDtypeStruct((t, qk_w), F32), jax.ShapeDtypeStruct((t, qk_w), F32),
                   jax.ShapeDtypeStruct((t, vdim), F32), jax.ShapeDtypeStruct((t, LANES), F32),
                   jax.ShapeDtypeStruct((nc, n_vh, DN_CHUNK), F32), jax.ShapeDtypeStruct((t, LANES), F32)],
        compiler_params=_delta_params("parallel"),
    )(qn, kn, qkv, big_g, big_gt, beta, *cots)


def loss_head(y, target, tb):
    t, d = y.shape

    def body(y_ref, t_ref, dy_ref, loss_ref):
        @pl.when(pl.program_id(0) == 0)
        def _():
            loss_ref[...] = jnp.zeros_like(loss_ref)
        err = y_ref[...] - t_ref[...]
        dy_ref[...] = err * (1.0 / d)
        loss_ref[...] += 0.5 * jnp.sum(jnp.sum(err * err, axis=1, keepdims=True), axis=0, keepdims=True) * (1.0 / d)

    return pl.pallas_call(
        body, name="loss_head", grid=(t // tb,),
        in_specs=[pl.BlockSpec((tb, d), lambda s: (s, 0))] * 2,
        out_specs=[pl.BlockSpec((tb, d), lambda s: (s, 0)), pl.BlockSpec((1, 1), lambda s: (0, 0))],
        out_shape=[jax.ShapeDtypeStruct((t, d), F32), jax.ShapeDtypeStruct((1, 1), F32)],
        compiler_params=pltpu.CompilerParams(dimension_semantics=("arbitrary",), vmem_limit_bytes=VMEM_LIMIT),
    )(y, target)


def all_gather(name, arrs):
    n = len(arrs)

    def body(*refs):
        in_refs, out_refs, sems = refs[:n], refs[n:2 * n], refs[2 * n:]
        _exchange_copies(in_refs, out_refs, sems, False, "start")
        _exchange_copies(in_refs, out_refs, sems, False, "wait")

    res = pl.pallas_call(
        body, name=name,
        in_specs=[_HBM] * n, out_specs=[_HBM] * n,
        out_shape=_exchange_out_shape(arrs, False), scratch_shapes=_exchange_sems(n),
        compiler_params=pltpu.CompilerParams(has_side_effects=True),
    )(*arrs)
    return list(res)


def _adamw_math(w, g, m, v):
    m = ADAM_B1 * m + (1.0 - ADAM_B1) * g
    v = ADAM_B2 * v + (1.0 - ADAM_B2) * (g * g)
    m_hat = m / (1.0 - ADAM_B1 ** ADAM_STEP)
    v_hat = v / (1.0 - ADAM_B2 ** ADAM_STEP)
    delta = -ADAM_LR * (m_hat / (jnp.sqrt(v_hat) + ADAM_EPS) + ADAM_WD * w)
    return delta, m, v


def adamw(name, w, parts, m, v, rows_cap=128):
    r, c = w.shape
    np_ = parts.shape[0]
    tr = _tile(r, rows_cap, SUBLANES * (4 // parts.dtype.itemsize))

    def body(w_ref, p_ref, m_ref, v_ref, g_ref, d_ref, nm_ref, nv_ref):
        g = p_ref[0].astype(F32)
        for k in range(1, np_):
            g = g + p_ref[k].astype(F32)
        delta, nm, nv = _adamw_math(w_ref[...], g, m_ref[...], v_ref[...])
        g_ref[...] = g
        d_ref[...] = delta
        nm_ref[...] = nm
        nv_ref[...] = nv

    spec = pl.BlockSpec((tr, c), lambda i: (i, 0))
    return pl.pallas_call(
        body, name=name, grid=(r // tr,),
        in_specs=[spec, pl.BlockSpec((np_, tr, c), lambda i: (0, i, 0)), spec, spec],
        out_specs=[spec] * 4, out_shape=[jax.ShapeDtypeStruct((r, c), F32)] * 4,
        compiler_params=pltpu.CompilerParams(dimension_semantics=("parallel",), vmem_limit_bytes=VMEM_LIMIT),
    )(w, parts, m, v)


def sum_parts(name, parts, rows_cap=256):
    np_, r, c = parts.shape
    tr = _tile(r, rows_cap, SUBLANES)

    def body(p_ref, o_ref):
        g = p_ref[0].astype(F32)
        for k in range(1, np_):
            g = g + p_ref[k].astype(F32)
        o_ref[...] = g

    return pl.pallas_call(
        body, name=name, grid=(r // tr,),
        in_specs=[pl.BlockSpec((np_, tr, c), lambda i: (0, i, 0))],
        out_specs=pl.BlockSpec((tr, c), lambda i: (i, 0)),
        out_shape=jax.ShapeDtypeStruct((r, c), F32),
        compiler_params=pltpu.CompilerParams(dimension_semantics=("parallel",), vmem_limit_bytes=VMEM_LIMIT),
    )(parts)


def _pack(arrs):
    flat = jnp.concatenate([a.reshape(-1).astype(F32) for a in arrs])
    n = flat.shape[0]
    return jnp.pad(flat, (0, _round_up(n, LANES * SUBLANES) - n)).reshape(-1, LANES)


def _unpack(packed, like):
    flat, out, pos = packed.reshape(-1), [], 0
    for a in like:
        out.append(flat[pos:pos + a.size].reshape(a.shape))
        pos += a.size
    return out


def kernel(x, c, w_ada, b_ada, w_in, rg_conv_w, rg_conv_b, rg_w_a, rg_b_a, rg_w_x, rg_b_x, rg_lambda, dn_conv_w, dn_a_log, dn_dt_bias, dn_norm_w, w_proj_a, w_proj_b, w_out, ln1_g, ln1_b, ffn_w_gate, ffn_w_up, ffn_conv_w, ffn_conv_b, ffn_w_down, ln2_g, ln2_b, loss_target, m_w_ada, m_b_ada, m_w_in, m_rg_conv_w, m_rg_conv_b, m_rg_w_a, m_rg_b_a, m_rg_w_x, m_rg_b_x, m_rg_lambda, m_dn_conv_w, m_dn_a_log, m_dn_dt_bias, m_dn_norm_w, m_w_proj_a, m_w_proj_b, m_w_out, m_ln1_g, m_ln1_b, m_ffn_w_gate, m_ffn_w_up, m_ffn_conv_w, m_ffn_conv_b, m_ffn_w_down, m_ln2_g, m_ln2_b, v_w_ada, v_b_ada, v_w_in, v_rg_conv_w, v_rg_conv_b, v_rg_w_a, v_rg_b_a, v_rg_w_x, v_rg_b_x, v_rg_lambda, v_dn_conv_w, v_dn_a_log, v_dn_dt_bias, v_dn_norm_w, v_w_proj_a, v_w_proj_b, v_w_out, v_ln1_g, v_ln1_b, v_ffn_w_gate, v_ffn_w_up, v_ffn_conv_w, v_ffn_conv_b, v_ffn_w_down, v_ln2_g, v_ln2_b):
    names = ['w_ada', 'b_ada', 'w_in', 'rg_conv_w', 'rg_conv_b', 'rg_w_a', 'rg_b_a', 'rg_w_x', 'rg_b_x', 'rg_lambda',
             'dn_conv_w', 'dn_a_log', 'dn_dt_bias', 'dn_norm_w', 'w_proj_a', 'w_proj_b', 'w_out', 'ln1_g', 'ln1_b',
             'ffn_w_gate', 'ffn_w_up', 'ffn_conv_w', 'ffn_conv_b', 'ffn_w_down', 'ln2_g', 'ln2_b']
    loc = locals()
    W = {n: loc[n][0] for n in names}
    M = {n: loc['m_' + n][0] for n in names}
    V = {n: loc['v_' + n][0] for n in names}

    me = 4 * lax.axis_index("x") + 2 * lax.axis_index("y") + lax.axis_index("c")
    xs, tgt = x[0], loss_target[0]
    t, d = xs.shape
    d_rnn = W['rg_conv_b'].shape[0]
    n_blk = W['rg_w_a'].shape[0]
    n_vh = W['dn_a_log'].shape[0]
    assert W['dn_norm_w'].shape[0] == LANES
    vdim = n_vh * LANES
    d_ff = W['ffn_conv_b'].shape[0]
    d_in = W['w_in'].shape[1] * N_DEV
    qk = (d_in - 2 * d_rnn - 2 * vdim - 2 * n_vh - 2 * d) // 2
    assert vdim == 2 * qk and qk % LANES == 0 and n_vh <= LANES
    splits = (d_rnn, d_rnn, qk, qk, vdim, vdim, n_vh, n_vh, d, d)
    offs = [0]
    for s_ in splits:
        offs.append(offs[-1] + s_)

    tb = _tile(t, 256, SUBLANES)

    big = ['w_in', 'w_proj_a', 'w_proj_b', 'w_out', 'ffn_w_gate', 'ffn_w_up', 'ffn_w_down']
    small_sh = ['rg_conv_w', 'dn_conv_w', 'ffn_conv_w']
    first = all_gather("gather_first", [W['w_in'].astype(WIRE_DTYPE)] + [W[n] for n in small_sh] + [c])
    g_in, g_rcw, g_dcw, g_fcw, c_all = first
    cols = lambda g: jnp.transpose(g, (1, 0, 2)).reshape(g.shape[1], -1)
    rows = lambda g: g.reshape(-1, g.shape[2])
    w_in_f = cols(g_in)
    padl = lambda a: jnp.pad(a, ((0, 0), (0, LANES - a.shape[1])))
    groups = [w_in_f[:, offs[i]:offs[i + 1]] for i in range(10)]
    groups[6], groups[7] = padl(groups[6]), padl(groups[7])
    go = [0]
    for g_ in groups:
        go.append(go[-1] + g_.shape[1])
    n_pad = _round_up(go[-1], 512)
    wp = jnp.pad(jnp.concatenate(groups, axis=1), ((0, 0), (0, n_pad - go[-1])))
    o_xr, o_gr, o_q, o_k, o_v, o_z, o_a, o_b, o_ga, o_gb = go[:10]
    rcw, dcw, fcw = cols(g_rcw), cols(g_dcw), cols(g_fcw)
    eye_b = jnp.eye(n_blk, dtype=F32)
    bd = lambda w: (w[:, :, None, :] * eye_b[:, None, :, None]).reshape(d_rnn, d_rnn)
    w_bd = jnp.concatenate([bd(W['rg_w_a']), bd(W['rg_w_x'])], axis=1)
    row1 = lambda a: a.reshape(1, -1)
    padv = lambda a: jnp.pad(row1(a), ((0, 0), (0, LANES - a.shape[0])))
    nw_t = jnp.tile(row1(W['dn_norm_w']), (1, n_vh))

    c_pad =jnp.pad(c_all.reshape(N_DEV, d), ((0, LANES - N_DEV), (0, 0)))
    ada_w = W['w_ada'].shape[1]
    b_ada_me = lax.dynamic_slice(W['b_ada'], (me * ada_w,), (ada_w,)).reshape(1, ada_w)
    ada_sh = mm(c_pad, W['w_ada'], name="ada_fwd", a_act="silu", bias=b_ada_me)
    (ada_all,) = all_gather("gather_ada", [ada_sh[:N_DEV]])
    ada_me = lax.dynamic_slice(ada_all, (0, me, 0), (N_DEV, 1, ada_w)).reshape(6, 1, d)
    sh1, sc1, gt1, sh2, sc2, gt2 = [ada_me[i] for i in range(6)]

    nt = t // tb

    def act(a, bw, col0=0, width=None, grad=True, rows=tb):
        width = a.shape[1] if width is None else width
        assert col0 % bw == 0 and width % bw == 0
        c0 = col0 // bw
        return In(a, (rows, bw), lambda o, s: (s, c0 + o), grad=grad, gshape=(t, width), gimap=lambda o, s: (s, o))

    def prm(a, bw, parts=None):
        return In(a, (a.shape[0], bw), lambda o, s: (0, o), acc=True, parts=parts)

    def out(width, bw, rows=tb):
        return Out((t, width), (rows, bw), lambda o, s: (s, o))

    tbh = _tile(t, 1024, SUBLANES)
    nth = t // tbh

    krows = lambda k_: [(slice(j, j + 1), slice(None)) for j in range(k_)]

    mod1_ins = [act(xs, d), prm(sc1, d), prm(sh1, d)]
    (h1,), _, (h1_t,) = stage_fwd("mod1_fwd", f_modulate, (1, nt), mod1_ins, [out(d, d)], transposed=[0])
    proj, g_pa, g_pb, g_out, g_fg, g_fu, g_fd = mm(h1, wp, name="proj_fwd",
                                                   gather=[W[n].astype(WIRE_DTYPE) for n in big[1:]])
    w_pa, w_pb, w_o, w_fd = rows(g_pa), rows(g_pb), rows(g_out), rows(g_fd)
    w_gate, w_up = cols(g_fg), cols(g_fu)
    w_gu = jnp.concatenate([w_gate, w_up], axis=1)

    cb_r = _tile(math.gcd(d_rnn, o_gr), 256)
    rgc_ins = [act(proj, cb_r, o_xr, d_rnn), prm(rcw, cb_r, krows(4)), prm(row1(W['rg_conv_b']), cb_r)]
    rgc_grid, rgc_car = (d_rnn // cb_r, nt), [(SUBLANES, cb_r)]
    (xc,), rgc_hist, (xc_t,) = stage_fwd("rg_conv_fwd", f_rg_conv, rgc_grid, rgc_ins, [out(d_rnn, cb_r)], rgc_car,
                                         transposed=[0])
    gates = mm(xc, w_bd, name="rg_gates_fwd")
    lru_ins = [act(xc, cb_r), act(gates, cb_r, 0, d_rnn), act(gates, cb_r, d_rnn, d_rnn), act(proj, cb_r, o_gr, d_rnn),
               prm(row1(W['rg_b_a']), cb_r), prm(row1(W['rg_b_x']), cb_r), prm(row1(W['rg_lambda']), cb_r)]
    lru_car = [(1, cb_r)]
    (rec,), lru_hist, (rec_t,) = stage_fwd("rglru_fwd", f_rglru, rgc_grid, lru_ins, [out(d_rnn, cb_r)], lru_car,
                                           transposed=[0])
    y_a = mm(rec, w_pa, name="proj_a_fwd")

    cb_q = _tile(math.gcd(math.gcd(qk, o_q), o_k), 256)
    dnc_car, dnc = [(SUBLANES, cb_q)], {}
    for nm, col0, width, w0 in (("q", o_q, qk, 0), ("k", o_k, qk, qk), ("v", o_v, vdim, 2 * qk)):
        ins_ = [act(proj, cb_q, col0, width), prm(dcw[:, w0:w0 + width], cb_q, krows(4))]
        grid_, outs_ = (width // cb_q, nt), [out(width, cb_q)]
        (y_,), hist_ = stage_fwd("dn_conv_%s_fwd" % nm, f_dn_conv, grid_, ins_, outs_, dnc_car)
        dnc[nm] = (y_, ins_, grid_, outs_, hist_)
    q_c, k_c, v_c = dnc["q"][0], dnc["k"][0], dnc["v"][0]
    f_qnorm, f_knorm = functools.partial(f_l2norm, LANES ** -0.5), functools.partial(f_l2norm, 1.0)
    qn_ins, kn_ins = [act(q_c, LANES, rows=tbh)], [act(k_c, LANES, rows=tbh)]
    nrm_grid, nrm_outs = (qk // LANES, nth), [out(qk, LANES, tbh)]
    (qn,), _ = stage_fwd("dn_qnorm_fwd", f_qnorm, nrm_grid, qn_ins, nrm_outs)
    (kn,), _ = stage_fwd("dn_knorm_fwd", f_knorm, nrm_grid, kn_ins, nrm_outs)
    gate_ins = [act(proj, LANES, o_a, LANES), act(proj, LANES, o_b, LANES),
                prm(padv(W['dn_a_log']), LANES), prm(padv(W['dn_dt_bias']), LANES)]
    gate_outs = [out(LANES, LANES), out(LANES, LANES)]
    (g_dn, beta_dn), _ = stage_fwd("dn_gates_fwd", f_dn_gates, (1, nt), gate_ins, gate_outs)
    n_ch = t // DN_CHUNK
    gt_dn = jnp.transpose(g_dn.reshape(n_ch, DN_CHUNK, LANES)[:, :, :n_vh], (0, 2, 1))
    dn_mid = delta_intra_fwd(qn, kn, v_c, 0, g_dn, gt_dn, beta_dn, n_vh)
    o_dn, dn_hist = delta_inter_fwd(*dn_mid, g_dn, n_vh)
    dno_ins = [act(o_dn, LANES, rows=tbh), act(proj, LANES, o_z, vdim, rows=tbh), prm(nw_t, LANES)]
    dno_grid, dno_outs = (n_vh, nth), [out(vdim, LANES, tbh)]
    (dn,), _, (dn_t,) = stage_fwd("dn_out_fwd", f_dn_out, dno_grid, dno_ins, dno_outs, transposed=[0])
    y_b = mm(dn, w_pb, name="proj_b_fwd")

    cb_m = _tile(math.gcd(math.gcd(d, o_ga), o_gb), 512)
    mrg_ins = [act(proj, cb_m, o_ga, d), act(proj, cb_m, o_gb, d), act(y_a, cb_m), act(y_b, cb_m)]
    mrg_grid = (d // cb_m, nt)
    (merged,), _, (merged_t,) = stage_fwd("merge_fwd", f_merge, mrg_grid, mrg_ins, [out(d, cb_m)], transposed=[0])
    mix = mm(merged, w_o, name="w_out_fwd")
    ln1_ins = [act(xs, d), act(mix, d), prm(gt1, d), prm(row1(W['ln1_g']), d), prm(row1(W['ln1_b']), d)]
    (x1,), _ = stage_fwd("ln1_fwd", f_deepnorm, (1, nt), ln1_ins, [out(d, d)])

    mod2_ins = [act(x1, d), prm(sc2, d), prm(sh2, d)]
    (h2,), _, (h2_t,) = stage_fwd("mod2_fwd", f_modulate, (1, nt), mod2_ins, [out(d, d)], transposed=[0])
    gu = mm(h2, w_gu, name="ffn_in_fwd")
    cb_f = _tile(d_ff, 256)
    ffa_ins = [act(gu, cb_f, 0, d_ff), act(gu, cb_f, d_ff, d_ff), prm(fcw, cb_f, krows(3)), prm(row1(W['ffn_conv_b']), cb_f)]
    ffa_grid, ffa_car = (d_ff // cb_f, nt), [(SUBLANES, cb_f)]
    (act_ff,), ffa_hist, (act_t,) = stage_fwd("ffn_act_fwd", f_ffn_act, ffa_grid, ffa_ins, [out(d_ff, cb_f)], ffa_car,
                                              transposed=[0])
    ff = mm(act_ff, w_fd, name="ffn_down_fwd")
    ln2_ins = [act(x1, d), act(ff, d), prm(gt2, d), prm(row1(W['ln2_g']), d), prm(row1(W['ln2_b']), d)]
    (x2,), _ = stage_fwd("ln2_fwd", f_deepnorm, (1, nt), ln2_ins, [out(d, d)])
    dy, loss_loc = loss_head(x2, tgt, tb)

    dx1_a, d_ff_o, d_gt2, d_ln2g, d_ln2b = stage_bwd("ln2_bwd", f_deepnorm, (1, nt), ln2_ins, [out(d, d)], [dy])
    d_act = mm(d_ff_o, w_fd, name="ffn_down_bwd_x", tb=True)
    gw_fd = mm(act_t, d_ff_o, name="ffn_down_bwd_w")
    d_gp, d_up, d_fcw, d_fcb = stage_bwd("ffn_act_bwd", f_ffn_act, ffa_grid, ffa_ins, [out(d_ff, cb_f)], [d_act],
                                         ffa_car, ffa_hist, gdtypes={0: MXU_DTYPE, 1: MXU_DTYPE})
    col_blocks = lambda g: jnp.transpose(g.reshape(g.shape[0], N_DEV, -1), (1, 0, 2)).astype(WIRE_DTYPE)
    row_blocks = lambda g: g.reshape(N_DEV, -1, g.shape[1]).astype(WIRE_DTYPE)
    big_parts = {}
    d_h2, big_parts['ffn_w_down'] = mm([d_gp, d_up], [w_gate, w_up], name="ffn_in_bwd_x", tb=True,
                                       scatter=[row_blocks(gw_fd)])
    gw_gate, gw_up = mm(h2_t, d_gp, name="ffn_gate_bwd_w"), mm(h2_t, d_up, name="ffn_up_bwd_w")
    d_x1, d_sc2, d_sh2 = stage_bwd("mod2_bwd", f_modulate, (1, nt), mod2_ins, [out(d, d)], [d_h2], add_to={0: dx1_a})
    dx_a, d_mix, d_gt1, d_ln1g, d_ln1b = stage_bwd("ln1_bwd", f_deepnorm, (1, nt), ln1_ins, [out(d, d)], [d_x1])
    d_merged = mm(d_mix, w_o, name="w_out_bwd_x", tb=True)
    gw_o = mm(merged_t, d_mix, name="w_out_bwd_w")
    d_ga, d_gb, d_ya, d_yb = stage_bwd("merge_bwd", f_merge, mrg_grid, mrg_ins, [out(d, cb_m)], [d_merged],
                                       gdtypes={0: MXU_DTYPE, 1: MXU_DTYPE})
    d_rec = mm(d_ya, w_pa, name="proj_a_bwd_x", tb=True)
    gw_pa = mm(rec_t, d_ya, name="proj_a_bwd_w")
    d_dn = mm(d_yb, w_pb, name="proj_b_bwd_x", tb=True)
    gw_pb = mm(dn_t, d_yb, name="proj_b_bwd_w")

    d_o, d_z, d_nwt = stage_bwd("dn_out_bwd", f_dn_out, dno_grid, dno_ins, dno_outs, [d_dn], gdtypes={1: MXU_DTYPE})
    *d_mid, d_g_state = delta_inter_bwd(*dn_mid, g_dn, dn_hist, d_o, n_vh)
    d_qn, d_kn, d_v, d_g_col, d_gt, d_beta = delta_intra_bwd(qn, kn, v_c, 0, g_dn, gt_dn, beta_dn, d_mid, n_vh)
    d_g_row = jnp.pad(jnp.transpose(d_gt, (0, 2, 1)).reshape(t, n_vh), ((0, 0), (0, LANES - n_vh)))
    (d_qc,) = stage_bwd("dn_qnorm_bwd", f_qnorm, nrm_grid, qn_ins, nrm_outs, [d_qn])
    (d_kc,) = stage_bwd("dn_knorm_bwd", f_knorm, nrm_grid, kn_ins, nrm_outs, [d_kn])
    d_a, d_b, d_alog, d_dtb = stage_bwd("dn_gates_bwd", f_dn_gates, (1, nt), gate_ins, gate_outs,
                                        [(d_g_state, d_g_col, d_g_row), d_beta], gdtypes={0: MXU_DTYPE, 1: MXU_DTYPE})
    d_win, d_dcw = {}, []
    for nm, cot in (("q", d_qc), ("k", d_kc), ("v", d_v)):
        _, ins_, grid_, outs_, hist_ = dnc[nm]
        d_win[nm], dw_ = stage_bwd("dn_conv_%s_bwd" % nm, f_dn_conv, grid_, ins_, outs_, [cot], dnc_car, hist_,
                                   gdtypes={0: MXU_DTYPE})
        d_dcw.append(dw_)
    d_dcw = jnp.concatenate(d_dcw, axis=1)

    d_xc_a, d_pr, d_pi, d_gr, d_ba, d_bx, d_lam = stage_bwd(
        "rglru_bwd", f_rglru, rgc_grid, lru_ins, [out(d_rnn, cb_r)], [d_rec], lru_car, lru_hist,
        gdtypes={1: MXU_DTYPE, 2: MXU_DTYPE, 3: MXU_DTYPE})
    (d_xc_b, big_parts['ffn_w_gate'], big_parts['ffn_w_up'], big_parts['w_out'], big_parts['w_proj_a'],
     big_parts['w_proj_b']) = mm([d_pr, d_pi], [w_bd[:, :d_rnn], w_bd[:, d_rnn:]], name="rg_gates_bwd_x", tb=True,
                                 scatter=[col_blocks(gw_gate), col_blocks(gw_up), row_blocks(gw_o),
                                          row_blocks(gw_pa), row_blocks(gw_pb)])
    gw_bd_a, gw_bd_x = mm(xc_t, d_pr, name="rg_gate_a_bwd_w"), mm(xc_t, d_pi, name="rg_gate_x_bwd_w")
    d_xr, d_rcw, d_rcb = stage_bwd("rg_conv_bwd", f_rg_conv, rgc_grid, rgc_ins, [out(d_rnn, cb_r)], [(d_xc_a, d_xc_b)],
                                   rgc_car, rgc_hist, gdtypes={0: MXU_DTYPE})

    diag = lambda g: jnp.einsum('nimj,nm->nij', g.reshape(n_blk, d_rnn // n_blk, n_blk, d_rnn // n_blk), eye_b)
    small_names = ['rg_conv_w', 'rg_conv_b', 'rg_w_a', 'rg_b_a', 'rg_w_x', 'rg_b_x', 'rg_lambda', 'dn_conv_w',
                   'dn_a_log', 'dn_dt_bias', 'dn_norm_w', 'ln1_g', 'ln1_b', 'ffn_conv_w', 'ffn_conv_b', 'ln2_g', 'ln2_b']
    small_loc = {
        'rg_conv_w': d_rcw, 'rg_conv_b': d_rcb,
        'rg_w_a': diag(gw_bd_a), 'rg_b_a': d_ba, 'rg_w_x': diag(gw_bd_x), 'rg_b_x': d_bx,
        'rg_lambda': d_lam, 'dn_conv_w': d_dcw, 'dn_a_log': d_alog[:, :n_vh], 'dn_dt_bias': d_dtb[:, :n_vh],
        'dn_norm_w': jnp.sum(d_nwt.reshape(n_vh, LANES), axis=0), 'ln1_g': d_ln1g, 'ln1_b': d_ln1b,
        'ffn_conv_w': d_fcw, 'ffn_conv_b': d_fcb, 'ln2_g': d_ln2g, 'ln2_b': d_ln2b}
    small_list = [small_loc[n] for n in small_names]

    d_segs = [d_xr, d_gr, d_win["q"], d_win["k"], d_win["v"], d_z, d_a, d_b, d_ga, d_gb]
    gw_segs = [mm(h1_t, dg, name="proj_bwd_w%d" % i) for i, dg in enumerate(d_segs)]
    gw_in = jnp.concatenate([g_[:, :splits[i]] for i, g_ in enumerate(gw_segs)], axis=1)
    d_h1, small_all, big_parts['w_in'] = mm(d_segs, groups, name="proj_bwd_x", tb=True,
                                            gather=[_pack(small_list)], scatter=[col_blocks(gw_in)])
    grad_x, d_sc1, d_sh1 = stage_bwd("mod1_bwd", f_modulate, (1, nt), mod1_ins, [out(d, d)], [d_h1], add_to={0: dx_a})

    g_small = dict(zip(small_names, _unpack(sum_parts("sum_small_grads", small_all), small_list)))
    d_ada_me = jnp.concatenate([d_sh1, d_sc1, d_gt1, d_sh2, d_sc2, d_gt2], axis=1)
    (d_ada_all,) = all_gather("gather_d_ada", [d_ada_me.reshape(-1, LANES)])
    g_small['b_ada'] = sum_parts("sum_d_ada", d_ada_all)
    small_names = ['b_ada'] + small_names
    d_ada_cols = lax.dynamic_slice(d_ada_all.reshape(N_DEV, 6 * d), (0, me * ada_w), (N_DEV, ada_w))
    d_ada_pad = jnp.pad(d_ada_cols, ((0, LANES - N_DEV), (0, 0)))
    gw_ada = mm(c_pad, d_ada_pad, name="ada_bwd_w", ta=True, a_act="silu")

    res = {}
    big_parts['w_ada'] = gw_ada[None]
    for n in ['w_ada'] + big:
        res[n] = adamw("adamw_" + n, W[n], big_parts[n], M[n], V[n])
    for n in small_sh:
        w_ = W[n].shape[1]
        g_small[n] = lax.dynamic_slice(g_small[n], (0, me * w_), (W[n].shape[0], w_))
    for n in small_names:
        g_small[n] = g_small[n].reshape(W[n].shape)
    pk = lambda dct: _pack([dct[n] for n in small_names])
    s_g, s_d, s_m, s_v = adamw("adamw_small", pk(W), pk(g_small)[None], pk(M), pk(V))
    like = [W[n] for n in small_names]
    for n, g_, d_, m_, v_ in zip(small_names, _unpack(s_g, like), _unpack(s_d, like), _unpack(s_m, like), _unpack(s_v, like)):
        res[n] = (g_, d_, m_, v_)

    loss = lax.psum(loss_loc[0, 0], ("x", "y", "c"))
    outs = [loss, grad_x[None]]
    for j in range(4):
        outs += [res[n][j].reshape(loc[n].shape) for n in names]
    return tuple(outs)
```

```python
import functools
import math

import jax
import jax.numpy as jnp
from jax import lax
from jax.experimental import pallas as pl
from jax.experimental.pallas import tpu as pltpu

F32 = jnp.float32
BF16 = jnp.bfloat16
MXU_DTYPE = BF16
WIRE_DTYPE = BF16
DN_DTYPE = BF16
HI = lax.Precision.HIGHEST
MESH = pl.DeviceIdType.MESH

N_DEV = 8
LANES = 128
SUBLANES = 8
VMEM_LIMIT = 56 * 1024 * 1024
MM_TM_CAP, MM_TN_CAP, MM_TK_CAP = 1024, 1536, 2048
MM_SPLIT_CAPS = dict(tm_cap=512, tn_cap=512, tk_cap=1024)

RG_C = 8.0
DN_CHUNK = 64
DN_HEAD_GROUP = 8
LN_EPS = 1e-5
RMS_EPS = 1e-6
L2_EPS = 1e-6
DEPTH = 1
DEEPNORM_ALPHA = (2 * DEPTH) ** 0.25
ADAM_LR = 0.001
ADAM_B1 = 0.9
ADAM_B2 = 0.999
ADAM_EPS = 1e-08
ADAM_WD = 0.01
ADAM_STEP = 10


def _tile(n, cap, unit=LANES):
    best = None
    for t in range(unit, min(n, cap) + 1, unit):
        if n % t == 0:
            best = t
    return best if best is not None else n


def _round_up(n, m):
    return (n + m - 1) // m * m


_HBM = pl.BlockSpec(memory_space=pl.ANY)


def _exchange_sems(n):
    return [pltpu.SemaphoreType.DMA((n, N_DEV - 1)), pltpu.SemaphoreType.DMA((n, N_DEV - 1)),
            pltpu.SemaphoreType.DMA((n,))]


def _exchange_out_shape(arrs, scatter):
    return [jax.ShapeDtypeStruct(a.shape if scatter else (N_DEV,) + a.shape, a.dtype) for a in arrs]


def _exchange_copies(in_refs, out_refs, sems, scatter, phase):
    send_sems, recv_sems, local_sems = sems
    x, y, c = lax.axis_index("x"), lax.axis_index("y"), lax.axis_index("c")
    me = 4 * x + 2 * y + c
    peers = [(x ^ ((k >> 2) & 1), y ^ ((k >> 1) & 1), c ^ (k & 1)) for k in range(N_DEV)]
    row = [4 * p[0] + 2 * p[1] + p[2] for p in peers]
    n = len(in_refs)

    def local(i):
        return pltpu.make_async_copy(in_refs[i].at[me] if scatter else in_refs[i], out_refs[i].at[me], local_sems.at[i])

    def remote(i, k, src, dst_row, to):
        return pltpu.make_async_remote_copy(src_ref=src, dst_ref=out_refs[i].at[dst_row],
                                            send_sem=send_sems.at[i, k - 1], recv_sem=recv_sems.at[i, k - 1],
                                            device_id=to, device_id_type=MESH)

    if scatter:
        sends = [(i, k, in_refs[i].at[row[k]], me, peers[k]) for k in range(1, N_DEV) for i in range(n)]
        passed = []
    else:
        sends = [(i, k, in_refs[i], me, peers[k]) for k in (1, 2, 4, 6) for i in range(n)]
        passed = [(i, k + 1, out_refs[i].at[row[k]], row[k], peers[1]) for k in (2, 4, 6) for i in range(n)]
    arrival = lambda i, k: remote(i, k, in_refs[i].at[me] if scatter else in_refs[i], row[k], peers[k])

    if phase == "start":
        for i in range(n):
            local(i).start()
        for cp in sends:
            remote(*cp).start()
    else:
        for cp in passed:
            arrival(cp[0], cp[1] - 1).wait_recv()
            remote(*cp).start()
        waited = {(cp[0], cp[1] - 1) for cp in passed}
        for k in range(1, N_DEV):
            for i in range(n):
                if (i, k) not in waited:
                    arrival(i, k).wait_recv()
        for cp in sends + passed:
            remote(*cp).wait_send()
        for i in range(n):
            local(i).wait()


def mm(a, b, *, name, ta=False, tb=False, a_act=None, bias=None, out_dtype=F32,
       tm_cap=MM_TM_CAP, tn_cap=MM_TN_CAP, tk_cap=MM_TK_CAP, gather=(), scatter=()):
    a_segs = list(a) if isinstance(a, (list, tuple)) else [a]
    b_segs = list(b) if isinstance(b, (list, tuple)) else [b]
    ns = len(a_segs)
    assert ns == len(b_segs) and (ns == 1 or a_act is None)
    m = a_segs[0].shape[1] if ta else a_segs[0].shape[0]
    n = b_segs[0].shape[0] if tb else b_segs[0].shape[1]
    ks = [x.shape[0] if ta else x.shape[1] for x in a_segs]
    assert ks == [y.shape[1] if tb else y.shape[0] for y in b_segs], (ks, ta, tb)
    tm, tn = _tile(m, tm_cap), _tile(n, tn_cap)
    tks = [_tile(k_, tk_cap) for k_ in ks]
    cnt = [k_ // t_ for k_, t_ in zip(ks, tks)]
    lo = [sum(cnt[:s]) for s in range(ns)]
    nk = sum(cnt)
    grid = (m // tm, n // tn, nk)
    dims = (((0 if ta else 1,), (1 if tb else 0,)), ((), ()))
    xch = list(gather) + list(scatter)
    nx, ng = len(xch), len(gather)
    n_main = 2 * ns + (bias is not None)

    def body(*refs):
        a_refs, b_refs = refs[:ns], refs[ns:2 * ns]
        bias_ref = refs[2 * ns] if bias is not None else None
        x_in, o_ref, x_out = refs[n_main:n_main + nx], refs[n_main + nx], refs[n_main + nx + 1:n_main + 2 * nx + 1]
        rest = refs[n_main + 2 * nx + 1:]
        acc_ref = rest[0] if nk > 1 else None
        sems = rest[1 if nk > 1 else 0:]
        groups = []
        if ng:
            groups.append((x_in[:ng], x_out[:ng], sems[:3], False))
        if nx > ng:
            groups.append((x_in[ng:], x_out[ng:], sems[-3:], True))
        if nx:
            step = (pl.program_id(0) * grid[1] + pl.program_id(1)) * grid[2] + pl.program_id(2)

            @pl.when(step == 0)
            def _():
                for gi, go_, gs, sc in groups:
                    _exchange_copies(gi, go_, gs, sc, "start")
        kk = pl.program_id(2)

        def finish(r):
            if bias is not None:
                r = r + bias_ref[...]
            o_ref[...] = r.astype(o_ref.dtype)

        def segment(s):
            av = a_refs[s][...]
            if a_act == "silu":
                av = jax.nn.silu(av.astype(F32))
            prod = lax.dot_general(av.astype(MXU_DTYPE), b_refs[s][...].astype(MXU_DTYPE), dims,
                                   preferred_element_type=F32)
            if nk == 1:
                finish(prod)
                return
            opens, closes = lo[s] == 0, lo[s] + cnt[s] == nk
            if opens:
                @pl.when(kk == 0)
                def _():
                    acc_ref[...] = prod
            inner = [kk > 0] * opens + [kk < nk - 1] * closes
            if inner:
                @pl.when(functools.reduce(lambda p, q: p & q, inner))
                def _():
                    acc_ref[...] += prod
            else:
                acc_ref[...] += prod
            if closes:
                @pl.when(kk == nk - 1)
                def _():
                    finish(acc_ref[...] + prod)

        for s in range(ns):
            if ns == 1:
                segment(s)
            else:
                pl.when((kk >= lo[s]) & (kk < lo[s] + cnt[s]))(functools.partial(segment, s))

        if nx:
            @pl.when(step == grid[0] * grid[1] * grid[2] - 1)
            def _():
                for gi, go_, gs, sc in groups:
                    _exchange_copies(gi, go_, gs, sc, "wait")

    def seg_index(s):
        return lambda q: jnp.clip(q - lo[s], 0, cnt[s] - 1) if ns > 1 else q

    a_specs, b_specs = [], []
    for s in range(ns):
        qi, tk = seg_index(s), tks[s]
        a_specs.append(pl.BlockSpec((tk, tm), (lambda qi: lambda i, j, q: (qi(q), i))(qi)) if ta
                       else pl.BlockSpec((tm, tk), (lambda qi: lambda i, j, q: (i, qi(q)))(qi)))
        b_specs.append(pl.BlockSpec((tn, tk), (lambda qi: lambda i, j, q: (j, qi(q)))(qi)) if tb
                       else pl.BlockSpec((tk, tn), (lambda qi: lambda i, j, q: (qi(q), j))(qi)))
    in_specs, args = a_specs + b_specs, a_segs + b_segs
    if bias is not None:
        in_specs.append(pl.BlockSpec((1, tn), lambda i, j, q: (0, j)))
        args.append(bias)
    o_spec, o_shape = pl.BlockSpec((tm, tn), lambda i, j, q: (i, j)), jax.ShapeDtypeStruct((m, n), out_dtype)
    acc = [pltpu.VMEM((tm, tn), F32)] if nk > 1 else []
    if not nx:
        return pl.pallas_call(
            body, name=name, grid=grid, in_specs=in_specs, out_specs=o_spec, out_shape=o_shape, scratch_shapes=acc,
            compiler_params=pltpu.CompilerParams(dimension_semantics=("parallel", "parallel", "arbitrary"),
                                                 vmem_limit_bytes=VMEM_LIMIT),
        )(*args)
    return pl.pallas_call(
        body, name=name, grid=grid, in_specs=in_specs + [_HBM] * nx, out_specs=[o_spec] + [_HBM] * nx,
        out_shape=[o_shape] + _exchange_out_shape(list(gather), False) + _exchange_out_shape(list(scatter), True),
        scratch_shapes=acc + (_exchange_sems(ng) if ng else []) + (_exchange_sems(nx - ng) if nx > ng else []),
        compiler_params=pltpu.CompilerParams(dimension_semantics=("arbitrary", "arbitrary", "arbitrary"),
                                             vmem_limit_bytes=VMEM_LIMIT, has_side_effects=True),
    )(*args, *xch)


class In:
    def __init__(self, arr, block, imap, acc=False, grad=True, parts=None, gshape=None, gimap=None):
        self.arr, self.block, self.imap, self.acc, self.grad, self.parts = arr, block, imap, acc, grad, parts
        self.gshape = arr.shape if gshape is None else gshape
        self.gimap = imap if gimap is None else gimap


class Out:
    def __init__(self, shape, block, imap, dtype=F32):
        self.shape, self.block, self.imap, self.dtype = shape, block, imap, dtype


def _load(in_refs, ins):
    vals = []
    for r, i in zip(in_refs, ins):
        if i.parts is None:
            vals.append(r[...])
        else:
            vals.extend(r[p] for p in i.parts)
    return vals


def _stage_params():
    return pltpu.CompilerParams(dimension_semantics=("parallel", "arbitrary"), vmem_limit_bytes=VMEM_LIMIT)


def stage_fwd(name, f, grid, ins, outs, carries=(), transposed=()):
    n_in, n_out, n_c, n_t = len(ins), len(outs), len(carries), len(transposed)

    def body(*refs):
        in_refs, out_refs = refs[:n_in], refs[n_in:n_in + n_out]
        hist_refs = refs[n_in + n_out:n_in + n_out + n_c]
        t_refs = refs[n_in + n_out + n_c:n_in + n_out + n_c + n_t]
        c_refs = refs[n_in + n_out + n_c + n_t:]
        if n_c:
            @pl.when(pl.program_id(1) == 0)
            def _():
                for c in c_refs:
                    c[...] = jnp.zeros_like(c)
        cin = [c[...] for c in c_refs]
        for h, c in zip(hist_refs, cin):
            h[...] = c
        o, cout = f(*_load(in_refs, ins), *cin)
        for r, v in zip(out_refs, o):
            r[...] = v.astype(r.dtype)
        for r, k in zip(t_refs, transposed):
            r[...] = o[k].T.astype(r.dtype)
        for c, v in zip(c_refs, cout):
            c[...] = v

    hist_spec = lambda c: pl.BlockSpec((None, None) + tuple(c), lambda o, s: (o, s) + (0,) * len(c))
    flip = lambda o_: pl.BlockSpec(o_.block[::-1], (lambda im: lambda o, s: im(o, s)[::-1])(o_.imap))
    res = pl.pallas_call(
        body, name=name, grid=grid,
        in_specs=[pl.BlockSpec(i.block, i.imap) for i in ins],
        out_specs=[pl.BlockSpec(o.block, o.imap) for o in outs] + [hist_spec(c) for c in carries]
        + [flip(outs[k]) for k in transposed],
        out_shape=[jax.ShapeDtypeStruct(o.shape, o.dtype) for o in outs]
        + [jax.ShapeDtypeStruct(tuple(grid) + tuple(c), F32) for c in carries]
        + [jax.ShapeDtypeStruct(outs[k].shape[::-1], MXU_DTYPE) for k in transposed],
        scratch_shapes=[pltpu.VMEM(tuple(c), F32) for c in carries],
        compiler_params=_stage_params(),
    )(*[i.arr for i in ins])
    res = list(res)
    if transposed:
        return res[:n_out], res[n_out:n_out + n_c], res[n_out + n_c:]
    return res[:n_out], res[n_out:]


def stage_bwd(name, f, grid, ins, outs, cots, carries=(), hists=(), add_to=None, gdtypes=None):
    n_in, n_out, n_c = len(ins), len(outs), len(carries)
    ns = grid[1]
    add_to = add_to or {}
    gdtypes = gdtypes or {}
    add_idx = sorted(add_to)
    g_idx = [k for k, i in enumerate(ins) if i.grad]
    cots = [c if isinstance(c, (tuple, list)) else (c,) for c in cots]
    n_cot = [len(c) for c in cots]
    rev = lambda imap: (lambda o, s: imap(o, ns - 1 - s))

    def body(*refs):
        p = 0
        in_refs = refs[p:p + n_in]; p += n_in
        cot_refs = []
        for cnt in n_cot:
            cot_refs.append(refs[p:p + cnt]); p += cnt
        hist_refs = refs[p:p + n_c]; p += n_c
        add_refs = refs[p:p + len(add_idx)]; p += len(add_idx)
        g_refs = refs[p:p + len(g_idx)]; p += len(g_idx)
        dc_refs = refs[p:]
        first = pl.program_id(1) == 0
        if n_c:
            @pl.when(first)
            def _():
                for c in dc_refs:
                    c[...] = jnp.zeros_like(c)
        vals = _load(in_refs, ins)
        cin = [h[...] for h in hist_refs]
        (o, cout), vjp = jax.vjp(lambda *a: f(*a), *vals, *cin)
        cot_o = []
        for crs, v in zip(cot_refs, o):
            c = crs[0][...].astype(v.dtype)
            for extra in crs[1:]:
                c = c + extra[...].astype(v.dtype)
            cot_o.append(c)
        cot_c = tuple(c[...] for c in dc_refs)
        grads = vjp((tuple(cot_o), cot_c))
        pos, per_in = 0, []
        for i in ins:
            cnt = 1 if i.parts is None else len(i.parts)
            per_in.append(grads[pos:pos + cnt])
            pos += cnt
        dcin = grads[pos:]
        for gr, k in zip(g_refs, g_idx):
            i, gs = ins[k], per_in[k]
            if i.acc:
                @pl.when(first)
                def _(gr=gr):
                    gr[...] = jnp.zeros_like(gr)
                if i.parts is None:
                    gr[...] += gs[0].astype(gr.dtype)
                else:
                    for pt, g in zip(i.parts, gs):
                        gr[pt] += g.astype(gr.dtype)
            else:
                g = gs[0]
                if k in add_to:
                    g = g + add_refs[add_idx.index(k)][...].astype(g.dtype)
                gr[...] = g.astype(gr.dtype)
        for c, v in zip(dc_refs, dcin):
            c[...] = v

    in_specs = [pl.BlockSpec(i.block, rev(i.imap)) for i in ins]
    for o_, cnt in zip(outs, n_cot):
        in_specs += [pl.BlockSpec(o_.block, rev(o_.imap))] * cnt
    in_specs += [pl.BlockSpec((None, None) + tuple(c), (lambda c: (lambda o, s: (o, ns - 1 - s) + (0,) * len(c)))(c))
                 for c in carries]
    in_specs += [pl.BlockSpec(ins[k].block, rev(ins[k].gimap)) for k in add_idx]
    out_specs, out_shape = [], []
    for k in g_idx:
        i = ins[k]
        if i.acc:
            out_specs.append(pl.BlockSpec(i.block, (lambda im: (lambda o, s: im(o, 0)))(i.imap)))
        else:
            out_specs.append(pl.BlockSpec(i.block, rev(i.gimap)))
        out_shape.append(jax.ShapeDtypeStruct(i.gshape, gdtypes.get(k, F32)))
    res = pl.pallas_call(
        body, name=name, grid=grid, in_specs=in_specs, out_specs=out_specs, out_shape=out_shape,
        scratch_shapes=[pltpu.VMEM(tuple(c), F32) for c in carries],
        compiler_params=_stage_params(),
    )(*[i.arr for i in ins], *[a for c in cots for a in c], *hists, *[add_to[k] for k in add_idx])
    return list(res)


def _iota_rows(shape):
    return lax.broadcasted_iota(jnp.int32, shape, 0)


@functools.partial(jax.custom_vjp, nondiff_argnums=(1,))
def _roll_rows(x, s):
    return pltpu.roll(x, s % x.shape[0], 0)


def _roll_rows_fwd(x, s):
    return _roll_rows(x, s), None


def _roll_rows_bwd(s, _, g):
    return (_roll_rows(g, -s),)


_roll_rows.defvjp(_roll_rows_fwd, _roll_rows_bwd)


@jax.custom_vjp
def _drop_head(xx):
    return xx[SUBLANES:]


def _drop_head_fwd(xx):
    return xx[SUBLANES:], None


def _drop_head_bwd(_, g):
    return (jnp.concatenate([jnp.zeros((SUBLANES, g.shape[1]), g.dtype), g], axis=0),)


_drop_head.defvjp(_drop_head_fwd, _drop_head_bwd)


@jax.custom_vjp
def _last_rows(x):
    return x[x.shape[0] - SUBLANES:]


def _last_rows_fwd(x):
    return x[x.shape[0] - SUBLANES:], x.shape[0]


def _last_rows_bwd(n, g):
    return (jnp.concatenate([jnp.zeros((n - SUBLANES, g.shape[1]), g.dtype), g], axis=0),)


_last_rows.defvjp(_last_rows_fwd, _last_rows_bwd)


def _last_row(x):
    n = x.shape[0]
    return jnp.sum(jnp.where(_iota_rows(x.shape) == n - 1, x, 0.0), axis=0, keepdims=True)


def _scan_steps(n):
    s = 1
    while s < n:
        yield s
        s *= 2


def _block_scan_impl(a, u, h0):
    n = a.shape[0]
    row = _iota_rows(a.shape)
    for s in _scan_steps(n):
        keep = row >= s
        a_s = jnp.where(keep, pltpu.roll(a, s, 0), 1.0)
        u_s = jnp.where(keep, pltpu.roll(u, s, 0), 0.0)
        u = u + a * u_s
        a = a * a_s
    return u + a * h0


@jax.custom_vjp
def _block_scan(a, u, h0):
    return _block_scan_impl(a, u, h0)


def _block_scan_fwd(a, u, h0):
    h = _block_scan_impl(a, u, h0)
    return h, (a, h, h0)


def _block_scan_bwd(res, dh):
    a, h, h0 = res
    n = a.shape[0]
    row = _iota_rows(a.shape)
    b = jnp.where(row < n - 1, pltpu.roll(a, n - 1, 0), 0.0)
    lam = dh
    for s in _scan_steps(n):
        keep = row < n - s
        b_s = jnp.where(keep, pltpu.roll(b, n - s, 0), 1.0)
        l_s = jnp.where(keep, pltpu.roll(lam, n - s, 0), 0.0)
        lam = lam + b * l_s
        b = b * b_s
    h_prev = jnp.where(row >= 1, pltpu.roll(h, 1, 0), jnp.broadcast_to(h0, h.shape))
    d_h0 = jnp.sum(jnp.where(row == 0, a * lam, 0.0), axis=0, keepdims=True)
    return lam * h_prev, lam, d_h0


_block_scan.defvjp(_block_scan_fwd, _block_scan_bwd)


def _dot_hi(a, b, dims=(((1,), (0,)), ((), ()))):
    return lax.dot_general(a, b, dims, precision=HI, preferred_element_type=F32)


_NN, _NT, _TN = "nn", "nt", "tn"
_CONTRACT = {_NN: (1, 0), _NT: (1, 1), _TN: (0, 0)}


def _raw_dot(a, b, kind):
    ca, cb = _CONTRACT[kind]
    lead = a.ndim - 2
    dims = (((ca + lead,), (cb + lead,)), (tuple(range(lead)), tuple(range(lead))))
    return lax.dot_general(a.astype(DN_DTYPE), b.astype(DN_DTYPE), dims, preferred_element_type=F32)


@jax.custom_vjp
def _nn(a, b):
    return _raw_dot(a, b, _NN)


_nn.defvjp(lambda a, b: (_raw_dot(a, b, _NN), (a, b)),
           lambda r, g: (_raw_dot(g, r[1], _NT), _raw_dot(r[0], g, _TN)))


@jax.custom_vjp
def _nt(a, b):
    return _raw_dot(a, b, _NT)


_nt.defvjp(lambda a, b: (_raw_dot(a, b, _NT), (a, b)),
           lambda r, g: (_raw_dot(g, r[1], _NN), _raw_dot(g, r[0], _TN)))


@jax.custom_vjp
def _tn(a, b):
    return _raw_dot(a, b, _TN)


_tn.defvjp(lambda a, b: (_raw_dot(a, b, _TN), (a, b)),
           lambda r, g: (_raw_dot(r[1], g, _NT), _raw_dot(r[0], g, _NN)))


def _neumann_inverse(a):
    n = a.shape[-1]
    eye = (lax.broadcasted_iota(jnp.int32, (n, n), 0) == lax.broadcasted_iota(jnp.int32, (n, n), 1)).astype(F32)
    p = _raw_dot(a, a, _NN)
    e = p
    for _ in range(int(math.log2(n)) - 2):
        p = _raw_dot(p, p, _NN)
        e = e + p + _raw_dot(e, p, _NN)
    return eye - a + e - _raw_dot(a, e, _NN)


@jax.custom_vjp
def _unit_lower_inverse(a):
    return _neumann_inverse(a)


def _unit_lower_inverse_fwd(a):
    x = _neumann_inverse(a)
    return x, x


def _unit_lower_inverse_bwd(x, g):
    return (-_raw_dot(_raw_dot(x, g, _TN), x, _NT),)


_unit_lower_inverse.defvjp(_unit_lower_inverse_fwd, _unit_lower_inverse_bwd)


def _softplus(x):
    return jnp.maximum(x, 0.0) + jnp.log1p(jnp.exp(-jnp.abs(x)))


def _neg_expm1(x):
    series = -x * (1.0 + x * (0.5 + x * (1.0 / 6.0 + x * (1.0 / 24.0 + x * (1.0 / 120.0)))))
    return jnp.where(x > -0.03, series, 1.0 - jnp.exp(x))


def f_modulate(x, sc, sh):
    return (x * (1.0 + sc) + sh,), ()


def f_deepnorm(x, y, gt, g, b):
    v = DEEPNORM_ALPHA * x + (1.0 + gt) * y
    mu = jnp.mean(v, axis=-1, keepdims=True)
    vc = v - mu
    var = jnp.mean(vc * vc, axis=-1, keepdims=True)
    return (vc * lax.rsqrt(var + LN_EPS) * g + b,), ()


def _causal_conv(x, prev, ws):
    xx = jnp.concatenate([prev, x], axis=0)
    k = len(ws)
    y = ws[k - 1] * x
    for j in range(k - 1):
        y = y + ws[j] * _drop_head(_roll_rows(xx, k - 1 - j))
    return y


def f_rg_conv(x, w0, w1, w2, w3, b, prev):
    return (_causal_conv(x, prev, (w0, w1, w2, w3)) + b,), (_last_rows(x),)


def f_dn_conv(x, w0, w1, w2, w3, prev):
    return (jax.nn.silu(_causal_conv(x, prev, (w0, w1, w2, w3))),), (_last_rows(x),)


def f_ffn_act(gp, up, w0, w1, w2, b, prev):
    return (jax.nn.gelu(_causal_conv(gp, prev, (w0, w1, w2)) + b) * up,), (_last_rows(gp),)


def f_rglru(xc, pre_r, pre_i, gr, b_a, b_x, lam, h0):
    gate_r = jax.nn.sigmoid(pre_r + b_a)
    gate_i = jax.nn.sigmoid(pre_i + b_x)
    log_a = -RG_C * gate_r * _softplus(-lam)
    a = jnp.exp(log_a)
    mult = jnp.sqrt(_neg_expm1(2.0 * log_a))
    h = _block_scan(a, mult * gate_i * xc, h0)
    return (h * jax.nn.gelu(gr),), (_last_row(h),)


def f_l2norm(scale, x):
    return (x * lax.rsqrt(jnp.sum(x * x, axis=-1, keepdims=True) + L2_EPS) * scale,), ()


def f_dn_gates(a_in, b_in, a_log, dt_bias):
    g = -jnp.exp(a_log) * _softplus(a_in + dt_bias)
    n = g.shape[0]
    shift = int(math.log2(DN_CHUNK))
    ri = lax.broadcasted_iota(jnp.int32, (n, n), 0)
    ci = lax.broadcasted_iota(jnp.int32, (n, n), 1)
    tri = ((lax.shift_right_logical(ri, shift) == lax.shift_right_logical(ci, shift)) & (ri >= ci)).astype(F32)
    return (_dot_hi(tri, g), jax.nn.sigmoid(b_in)), ()


def f_dn_out(o, z, nw):
    r = lax.rsqrt(jnp.mean(o * o, axis=-1, keepdims=True) + RMS_EPS)
    return (o * r * nw * jax.nn.silu(z),), ()


def f_merge(ga, gb, ya, yb):
    return (jax.nn.sigmoid(ga) * ya + jax.nn.sigmoid(gb) * yb,), ()


def _delta_intra(q, k, v, g_i, g_j, beta):
    c = q.shape[-2]
    ri = lax.broadcasted_iota(jnp.int32, (c, c), 0)
    ci = lax.broadcasted_iota(jnp.int32, (c, c), 1)
    decay = jnp.exp(jnp.where(ri >= ci, g_i - g_j, -jnp.inf))
    g_last = jnp.sum(jnp.where(_iota_rows((c, 1)) == c - 1, g_i, 0.0), axis=-2, keepdims=True)
    exp_g = jnp.exp(g_i)
    kb = k * beta
    t_inv = _unit_lower_inverse(jnp.where(ri > ci, _nt(kb, k) * decay, 0.0))
    u = _nn(t_inv, v * beta)
    w = _nn(t_inv, kb * exp_g)
    return u, w, _nt(q, k) * decay, q * exp_g, k * jnp.exp(g_last - g_i)


def _delta_inter(u, w, qk, q_dec, k_dec, g_last, state):
    v_new = u - _nn(w, state)
    o = _nn(q_dec, state) + _nn(qk, v_new)
    return o, jnp.exp(g_last) * state + _tn(k_dec, v_new)


def _chunk_spec(width, nc=None, col=0):
    if nc is None:
        return pl.BlockSpec((DN_CHUNK, width), lambda s: (s, col))
    return pl.BlockSpec((DN_CHUNK, width), lambda s: (nc - 1 - s, col))


def _delta_params(sem):
    return pltpu.CompilerParams(dimension_semantics=(sem,), vmem_limit_bytes=VMEM_LIMIT)


def _head(ref, h, width=LANES):
    return ref[:, h * LANES:h * LANES + width]


def _head_groups(n_vh):
    hb = min(DN_HEAD_GROUP, n_vh)
    return [range(h0, h0 + hb) for h0 in range(0, n_vh, hb)]


def _stack(hs, f):
    return jnp.stack([f(h) for h in hs])


def _intra_operands(hs, rep, q_ref, k_ref, v_ref, g_ref, gt_ref, b_ref):
    return (_stack(hs, lambda h: _head(q_ref, h // rep)), _stack(hs, lambda h: _head(k_ref, h // rep)),
            _stack(hs, lambda h: _head(v_ref, h)), _stack(hs, lambda h: g_ref[:, h:h + 1]),
            _stack(hs, lambda h: gt_ref[h:h + 1, :]), _stack(hs, lambda h: b_ref[:, h:h + 1]))


def _inter_operands(hs, u_ref, w_ref, qk_ref, qd_ref, kd_ref, g_ref):
    f32 = lambda ref, width=LANES: _stack(hs, lambda h: _head(ref, h, width).astype(F32))
    return (f32(u_ref), f32(w_ref), f32(qk_ref, DN_CHUNK), f32(qd_ref), f32(kd_ref),
            _stack(hs, lambda h: g_ref[DN_CHUNK - 1:DN_CHUNK, h:h + 1]))


def delta_intra_fwd(qn, kn, qkv, v_blk, big_g, big_gt, beta, n_vh):
    t, qk_w = qn.shape
    vdim = n_vh * LANES
    rep = vdim // qk_w
    nc = t // DN_CHUNK

    def body(q_ref, k_ref, v_ref, g_ref, gt_ref, b_ref, u_ref, w_ref, qk_ref, qd_ref, kd_ref):
        for hs in _head_groups(n_vh):
            u, w, qk, qd, kd = _delta_intra(*_intra_operands(hs, rep, q_ref, k_ref, v_ref, g_ref, gt_ref, b_ref))
            for i, h in enumerate(hs):
                sl = slice(h * LANES, (h + 1) * LANES)
                u_ref[:, sl] = u[i]
                w_ref[:, sl] = w[i].astype(w_ref.dtype)
                qk_ref[:, sl] = jnp.concatenate([qk[i], jnp.zeros_like(qk[i])], axis=1).astype(qk_ref.dtype)
                qd_ref[:, sl] = qd[i].astype(qd_ref.dtype)
                kd_ref[:, sl] = kd[i].astype(kd_ref.dtype)

    return pl.pallas_call(
        body, name="delta_intra_fwd", grid=(nc,),
        in_specs=[_chunk_spec(qk_w), _chunk_spec(qk_w), _chunk_spec(vdim, col=v_blk), _chunk_spec(LANES),
                  pl.BlockSpec((None, n_vh, DN_CHUNK), lambda s: (s, 0, 0)), _chunk_spec(LANES)],
        out_specs=[_chunk_spec(vdim)] * 5,
        out_shape=[jax.ShapeDtypeStruct((t, vdim), F32)] + [jax.ShapeDtypeStruct((t, vdim), DN_DTYPE)] * 4,
        compiler_params=_delta_params("parallel"),
    )(qn, kn, qkv, big_g, big_gt, beta)


def delta_inter_fwd(u, w, qk, q_dec, k_dec, big_g, n_vh):
    t, vdim = u.shape
    nc = t // DN_CHUNK

    def body(u_ref, w_ref, qk_ref, qd_ref, kd_ref, g_ref, o_ref, hist_ref, s_ref):
        @pl.when(pl.program_id(0) == 0)
        def _():
            s_ref[...] = jnp.zeros_like(s_ref)
        for hs in _head_groups(n_vh):
            grp = slice(hs[0], hs[-1] + 1)
            st = s_ref[grp]
            hist_ref[grp] = st
            o, ns = _delta_inter(*_inter_operands(hs, u_ref, w_ref, qk_ref, qd_ref, kd_ref, g_ref), st)
            for i, h in enumerate(hs):
                o_ref[:, h * LANES:(h + 1) * LANES] = o[i]
            s_ref[grp] = ns

    return pl.pallas_call(
        body, name="delta_inter_fwd", grid=(nc,),
        in_specs=[_chunk_spec(vdim)] * 5 + [_chunk_spec(LANES)],
        out_specs=[_chunk_spec(vdim), pl.BlockSpec((None, n_vh, LANES, LANES), lambda s: (s, 0, 0, 0))],
        out_shape=[jax.ShapeDtypeStruct((t, vdim), F32), jax.ShapeDtypeStruct((nc, n_vh, LANES, LANES), F32)],
        scratch_shapes=[pltpu.VMEM((n_vh, LANES, LANES), F32)],
        compiler_params=_delta_params("arbitrary"),
    )(u, w, qk, q_dec, k_dec, big_g)


def delta_inter_bwd(u, w, qk, q_dec, k_dec, big_g, hist, d_o, n_vh):
    t, vdim = u.shape
    nc = t // DN_CHUNK

    def body(u_ref, w_ref, qk_ref, qd_ref, kd_ref, g_ref, hist_ref, do_ref,
             du_ref, dw_ref, dqk_ref, dqd_ref, dkd_ref, dg_ref, ds_ref):
        @pl.when(pl.program_id(0) == 0)
        def _():
            ds_ref[...] = jnp.zeros_like(ds_ref)
        lane = lax.broadcasted_iota(jnp.int32, (1, LANES), 1)
        dgl_all = jnp.zeros((1, LANES), F32)
        for hs in _head_groups(n_vh):
            grp = slice(hs[0], hs[-1] + 1)
            prim = _inter_operands(hs, u_ref, w_ref, qk_ref, qd_ref, kd_ref, g_ref) + (hist_ref[grp],)
            _, vjp = jax.vjp(_delta_inter, *prim)
            du, dw, dqk, dqd, dkd, dgl, dst = vjp((_stack(hs, lambda h: _head(do_ref, h)), ds_ref[grp]))
            ds_ref[grp] = dst
            for i, h in enumerate(hs):
                sl = slice(h * LANES, (h + 1) * LANES)
                du_ref[:, sl] = du[i]
                dw_ref[:, sl] = dw[i]
                dqk_ref[:, sl] = jnp.concatenate([dqk[i], jnp.zeros_like(dqk[i])], axis=1)
                dqd_ref[:, sl] = dqd[i]
                dkd_ref[:, sl] = dkd[i]
                dgl_all = dgl_all + dgl[i] * (lane == h).astype(F32)
        last = _iota_rows((DN_CHUNK, LANES)) == DN_CHUNK - 1
        dg_ref[...] = jnp.where(last, jnp.broadcast_to(dgl_all, (DN_CHUNK, LANES)), 0.0)

    rv = lambda w_: _chunk_spec(w_, nc)
    return pl.pallas_call(
        body, name="delta_inter_bwd", grid=(nc,),
        in_specs=[rv(vdim)] * 5 + [rv(LANES), pl.BlockSpec((None, n_vh, LANES, LANES), lambda s: (nc - 1 - s, 0, 0, 0)),
                                   rv(vdim)],
        out_specs=[rv(vdim)] * 5 + [rv(LANES)],
        out_shape=[jax.ShapeDtypeStruct((t, vdim), F32)] * 5 + [jax.ShapeDtypeStruct((t, LANES), F32)],
        scratch_shapes=[pltpu.VMEM((n_vh, LANES, LANES), F32)],
        compiler_params=_delta_params("arbitrary"),
    )(u, w, qk, q_dec, k_dec, big_g, hist, d_o)


def delta_intra_bwd(qn, kn, qkv, v_blk, big_g, big_gt, beta, cots, n_vh):
    t, qk_w = qn.shape
    vdim = n_vh * LANES
    rep = vdim // qk_w
    nc = t // DN_CHUNK

    def body(q_ref, k_ref, v_ref, g_ref, gt_ref, b_ref, du_ref, dw_ref, dqk_ref, dqd_ref, dkd_ref,
             dq_ref, dk_ref, dv_ref, dg_ref, dgt_ref, db_ref):
        lane = lax.broadcasted_iota(jnp.int32, (1, LANES), 1)
        dg_all = jnp.zeros((DN_CHUNK, LANES), F32)
        db_all = jnp.zeros((DN_CHUNK, LANES), F32)
        dq_acc, dk_acc = None, None
        for hs in _head_groups(n_vh):
            _, vjp = jax.vjp(_delta_intra, *_intra_operands(hs, rep, q_ref, k_ref, v_ref, g_ref, gt_ref, b_ref))
            cot = lambda ref, width=LANES: _stack(hs, lambda h: _head(ref, h, width))
            dq, dk, dv, dgi, dgj, db = vjp((cot(du_ref), cot(dw_ref), cot(dqk_ref, DN_CHUNK), cot(dqd_ref), cot(dkd_ref)))
            for i, h in enumerate(hs):
                j = h // rep
                dv_ref[:, h * LANES:(h + 1) * LANES] = dv[i]
                dgt_ref[h:h + 1, :] = dgj[i]
                onehot = (lane == h).astype(F32)
                dg_all = dg_all + dgi[i] * onehot
                db_all = db_all + db[i] * onehot
                dq_acc = dq[i] if h % rep == 0 else dq_acc + dq[i]
                dk_acc = dk[i] if h % rep == 0 else dk_acc + dk[i]
                if h % rep == rep - 1:
                    dq_ref[:, j * LANES:(j + 1) * LANES] = dq_acc
                    dk_ref[:, j * LANES:(j + 1) * LANES] = dk_acc
        dg_ref[...] = dg_all
        db_ref[...] = db_all

    gt_spec = pl.BlockSpec((None, n_vh, DN_CHUNK), lambda s: (s, 0, 0))
    return pl.pallas_call(
        body, name="delta_intra_bwd", grid=(nc,),
        in_specs=[_chunk_spec(qk_w), _chunk_spec(qk_w), _chunk_spec(vdim, col=v_blk), _chunk_spec(LANES), gt_spec,
                  _chunk_spec(LANES)] + [_chunk_spec(vdim)] * 5,
        out_specs=[_chunk_spec(qk_w), _chunk_spec(qk_w), _chunk_spec(vdim), _chunk_spec(LANES), gt_spec,
                   _chunk_spec(LANES)],
        out_shape=[jax.ShapeDtypeStruct((t, qk_w), F32), jax.ShapeDtypeStruct((t, qk_w), F32),
                   jax.ShapeDtypeStruct((t, vdim), F32), jax.ShapeDtypeStruct((t, LANES), F32),
                   jax.ShapeDtypeStruct((nc, n_vh, DN_CHUNK), F32), jax.ShapeDtypeStruct((t, LANES), F32)],
        compiler_params=_delta_params("parallel"),
    )(qn, kn, qkv, big_g, big_gt, beta, *cots)


def loss_head(y, target, tb):
    t, d = y.shape

    def body(y_ref, t_ref, dy_ref, loss_ref):
        @pl.when(pl.program_id(0) == 0)
        def _():
            loss_ref[...] = jnp.zeros_like(loss_ref)
        err = y_ref[...] - t_ref[...]
        dy_ref[...] = err * (1.0 / d)
        loss_ref[...] += 0.5 * jnp.sum(jnp.sum(err * err, axis=1, keepdims=True), axis=0, keepdims=True) * (1.0 / d)

    return pl.pallas_call(
        body, name="loss_head", grid=(t // tb,),
        in_specs=[pl.BlockSpec((tb, d), lambda s: (s, 0))] * 2,
        out_specs=[pl.BlockSpec((tb, d), lambda s: (s, 0)), pl.BlockSpec((1, 1), lambda s: (0, 0))],
        out_shape=[jax.ShapeDtypeStruct((t, d), F32), jax.ShapeDtypeStruct((1, 1), F32)],
        compiler_params=pltpu.CompilerParams(dimension_semantics=("arbitrary",), vmem_limit_bytes=VMEM_LIMIT),
    )(y, target)


def all_gather(name, arrs):
    n = len(arrs)

    def body(*refs):
        in_refs, out_refs, sems = refs[:n], refs[n:2 * n], refs[2 * n:]
        _exchange_copies(in_refs, out_refs, sems, False, "start")
        _exchange_copies(in_refs, out_refs, sems, False, "wait")

    res = pl.pallas_call(
        body, name=name,
        in_specs=[_HBM] * n, out_specs=[_HBM] * n,
        out_shape=_exchange_out_shape(arrs, False), scratch_shapes=_exchange_sems(n),
        compiler_params=pltpu.CompilerParams(has_side_effects=True),
    )(*arrs)
    return list(res)


def _adamw_math(w, g, m, v):
    m = ADAM_B1 * m + (1.0 - ADAM_B1) * g
    v = ADAM_B2 * v + (1.0 - ADAM_B2) * (g * g)
    m_hat = m / (1.0 - ADAM_B1 ** ADAM_STEP)
    v_hat = v / (1.0 - ADAM_B2 ** ADAM_STEP)
    delta = -ADAM_LR * (m_hat / (jnp.sqrt(v_hat) + ADAM_EPS) + ADAM_WD * w)
    return delta, m, v


def adamw(name, w, parts, m, v, rows_cap=128):
    r, c = w.shape
    np_ = parts.shape[0]
    tr = _tile(r, rows_cap, SUBLANES * (4 // parts.dtype.itemsize))

    def body(w_ref, p_ref, m_ref, v_ref, g_ref, d_ref, nm_ref, nv_ref):
        g = p_ref[0].astype(F32)
        for k in range(1, np_):
            g = g + p_ref[k].astype(F32)
        delta, nm, nv = _adamw_math(w_ref[...], g, m_ref[...], v_ref[...])
        g_ref[...] = g
        d_ref[...] = delta
        nm_ref[...] = nm
        nv_ref[...] = nv

    spec = pl.BlockSpec((tr, c), lambda i: (i, 0))
    return pl.pallas_call(
        body, name=name, grid=(r // tr,),
        in_specs=[spec, pl.BlockSpec((np_, tr, c), lambda i: (0, i, 0)), spec, spec],
        out_specs=[spec] * 4, out_shape=[jax.ShapeDtypeStruct((r, c), F32)] * 4,
        compiler_params=pltpu.CompilerParams(dimension_semantics=("parallel",), vmem_limit_bytes=VMEM_LIMIT),
    )(w, parts, m, v)


def sum_parts(name, parts, rows_cap=256):
    np_, r, c = parts.shape
    tr = _tile(r, rows_cap, SUBLANES)

    def body(p_ref, o_ref):
        g = p_ref[0].astype(F32)
        for k in range(1, np_):
            g = g + p_ref[k].astype(F32)
        o_ref[...] = g

    return pl.pallas_call(
        body, name=name, grid=(r // tr,),
        in_specs=[pl.BlockSpec((np_, tr, c), lambda i: (0, i, 0))],
        out_specs=pl.BlockSpec((tr, c), lambda i: (i, 0)),
        out_shape=jax.ShapeDtypeStruct((r, c), F32),
        compiler_params=pltpu.CompilerParams(dimension_semantics=("parallel",), vmem_limit_bytes=VMEM_LIMIT),
    )(parts)


def _pack(arrs):
    flat = jnp.concatenate([a.reshape(-1).astype(F32) for a in arrs])
    n = flat.shape[0]
    return jnp.pad(flat, (0, _round_up(n, LANES * SUBLANES) - n)).reshape(-1, LANES)


def _unpack(packed, like):
    flat, out, pos = packed.reshape(-1), [], 0
    for a in like:
        out.append(flat[pos:pos + a.size].reshape(a.shape))
        pos += a.size
    return out


def kernel(x, c, w_ada, b_ada, w_in, rg_conv_w, rg_conv_b, rg_w_a, rg_b_a, rg_w_x, rg_b_x, rg_lambda, dn_conv_w, dn_a_log, dn_dt_bias, dn_norm_w, w_proj_a, w_proj_b, w_out, ln1_g, ln1_b, ffn_w_gate, ffn_w_up, ffn_conv_w, ffn_conv_b, ffn_w_down, ln2_g, ln2_b, loss_target, m_w_ada, m_b_ada, m_w_in, m_rg_conv_w, m_rg_conv_b, m_rg_w_a, m_rg_b_a, m_rg_w_x, m_rg_b_x, m_rg_lambda, m_dn_conv_w, m_dn_a_log, m_dn_dt_bias, m_dn_norm_w, m_w_proj_a, m_w_proj_b, m_w_out, m_ln1_g, m_ln1_b, m_ffn_w_gate, m_ffn_w_up, m_ffn_conv_w, m_ffn_conv_b, m_ffn_w_down, m_ln2_g, m_ln2_b, v_w_ada, v_b_ada, v_w_in, v_rg_conv_w, v_rg_conv_b, v_rg_w_a, v_rg_b_a, v_rg_w_x, v_rg_b_x, v_rg_lambda, v_dn_conv_w, v_dn_a_log, v_dn_dt_bias, v_dn_norm_w, v_w_proj_a, v_w_proj_b, v_w_out, v_ln1_g, v_ln1_b, v_ffn_w_gate, v_ffn_w_up, v_ffn_conv_w, v_ffn_conv_b, v_ffn_w_down, v_ln2_g, v_ln2_b):
    names = ['w_ada', 'b_ada', 'w_in', 'rg_conv_w', 'rg_conv_b', 'rg_w_a', 'rg_b_a', 'rg_w_x', 'rg_b_x', 'rg_lambda',
             'dn_conv_w', 'dn_a_log', 'dn_dt_bias', 'dn_norm_w', 'w_proj_a', 'w_proj_b', 'w_out', 'ln1_g', 'ln1_b',
             'ffn_w_gate', 'ffn_w_up', 'ffn_conv_w', 'ffn_conv_b', 'ffn_w_down', 'ln2_g', 'ln2_b']
    loc = locals()
    W = {n: loc[n][0] for n in names}
    M = {n: loc['m_' + n][0] for n in names}
    V = {n: loc['v_' + n][0] for n in names}

    me = 4 * lax.axis_index("x") + 2 * lax.axis_index("y") + lax.axis_index("c")
    xs, tgt = x[0], loss_target[0]
    t, d = xs.shape
    d_rnn = W['rg_conv_b'].shape[0]
    n_blk = W['rg_w_a'].shape[0]
    n_vh = W['dn_a_log'].shape[0]
    assert W['dn_norm_w'].shape[0] == LANES
    vdim = n_vh * LANES
    d_ff = W['ffn_conv_b'].shape[0]
    d_in = W['w_in'].shape[1] * N_DEV
    qk = (d_in - 2 * d_rnn - 2 * vdim - 2 * n_vh - 2 * d) // 2
    assert vdim == 2 * qk and qk % LANES == 0 and n_vh <= LANES
    splits = (d_rnn, d_rnn, qk, qk, vdim, vdim, n_vh, n_vh, d, d)
    offs = [0]
    for s_ in splits:
        offs.append(offs[-1] + s_)

    tb = _tile(t, 256, SUBLANES)

    big = ['w_in', 'w_proj_a', 'w_proj_b', 'w_out', 'ffn_w_gate', 'ffn_w_up', 'ffn_w_down']
    small_sh = ['rg_conv_w', 'dn_conv_w', 'ffn_conv_w']
    first = all_gather("gather_first", [W['w_in'].astype(WIRE_DTYPE)] + [W[n] for n in small_sh] + [c])
    g_in, g_rcw, g_dcw, g_fcw, c_all = first
    cols = lambda g: jnp.transpose(g, (1, 0, 2)).reshape(g.shape[1], -1)
    rows = lambda g: g.reshape(-1, g.shape[2])
    w_in_f = cols(g_in)
    padl = lambda a: jnp.pad(a, ((0, 0), (0, LANES - a.shape[1])))
    groups = [w_in_f[:, offs[i]:offs[i + 1]] for i in range(10)]
    groups[6], groups[7] = padl(groups[6]), padl(groups[7])
    go = [0]
    for g_ in groups:
        go.append(go[-1] + g_.shape[1])
    n_pad = _round_up(go[-1], 512)
    wp = jnp.pad(jnp.concatenate(groups, axis=1), ((0, 0), (0, n_pad - go[-1])))
    o_xr, o_gr, o_q, o_k, o_v, o_z, o_a, o_b, o_ga, o_gb = go[:10]
    rcw, dcw, fcw = cols(g_rcw), cols(g_dcw), cols(g_fcw)
    eye_b = jnp.eye(n_blk, dtype=F32)
    bd = lambda w: (w[:, :, None, :] * eye_b[:, None, :, None]).reshape(d_rnn, d_rnn)
    w_bd = jnp.concatenate([bd(W['rg_w_a']), bd(W['rg_w_x'])], axis=1)
    row1 = lambda a: a.reshape(1, -1)
    padv = lambda a: jnp.pad(row1(a), ((0, 0), (0, LANES - a.shape[0])))
    nw_t = jnp.tile(row1(W['dn_norm_w']), (1, n_vh))

    c_pad =jnp.pad(c_all.reshape(N_DEV, d), ((0, LANES - N_DEV), (0, 0)))
    ada_w = W['w_ada'].shape[1]
    b_ada_me = lax.dynamic_slice(W['b_ada'], (me * ada_w,), (ada_w,)).reshape(1, ada_w)
    ada_sh = mm(c_pad, W['w_ada'], name="ada_fwd", a_act="silu", bias=b_ada_me)
    (ada_all,) = all_gather("gather_ada", [ada_sh[:N_DEV]])
    ada_me = lax.dynamic_slice(ada_all, (0, me, 0), (N_DEV, 1, ada_w)).reshape(6, 1, d)
    sh1, sc1, gt1, sh2, sc2, gt2 = [ada_me[i] for i in range(6)]

    nt = t // tb

    def act(a, bw, col0=0, width=None, grad=True, rows=tb):
        width = a.shape[1] if width is None else width
        assert col0 % bw == 0 and width % bw == 0
        c0 = col0 // bw
        return In(a, (rows, bw), lambda o, s: (s, c0 + o), grad=grad, gshape=(t, width), gimap=lambda o, s: (s, o))

    def prm(a, bw, parts=None):
        return In(a, (a.shape[0], bw), lambda o, s: (0, o), acc=True, parts=parts)

    def out(width, bw, rows=tb):
        return Out((t, width), (rows, bw), lambda o, s: (s, o))

    tbh = _tile(t, 1024, SUBLANES)
    nth = t // tbh

    krows = lambda k_: [(slice(j, j + 1), slice(None)) for j in range(k_)]

    mod1_ins = [act(xs, d), prm(sc1, d), prm(sh1, d)]
    (h1,), _, (h1_t,) = stage_fwd("mod1_fwd", f_modulate, (1, nt), mod1_ins, [out(d, d)], transposed=[0])
    proj, g_pa, g_pb, g_out, g_fg, g_fu, g_fd = mm(h1, wp, name="proj_fwd",
                                                   gather=[W[n].astype(WIRE_DTYPE) for n in big[1:]])
    w_pa, w_pb, w_o, w_fd = rows(g_pa), rows(g_pb), rows(g_out), rows(g_fd)
    w_gate, w_up = cols(g_fg), cols(g_fu)
    w_gu = jnp.concatenate([w_gate, w_up], axis=1)

    cb_r = _tile(math.gcd(d_rnn, o_gr), 256)
    rgc_ins = [act(proj, cb_r, o_xr, d_rnn), prm(rcw, cb_r, krows(4)), prm(row1(W['rg_conv_b']), cb_r)]
    rgc_grid, rgc_car = (d_rnn // cb_r, nt), [(SUBLANES, cb_r)]
    (xc,), rgc_hist, (xc_t,) = stage_fwd("rg_conv_fwd", f_rg_conv, rgc_grid, rgc_ins, [out(d_rnn, cb_r)], rgc_car,
                                         transposed=[0])
    gates = mm(xc, w_bd, name="rg_gates_fwd")
    lru_ins = [act(xc, cb_r), act(gates, cb_r, 0, d_rnn), act(gates, cb_r, d_rnn, d_rnn), act(proj, cb_r, o_gr, d_rnn),
               prm(row1(W['rg_b_a']), cb_r), prm(row1(W['rg_b_x']), cb_r), prm(row1(W['rg_lambda']), cb_r)]
    lru_car = [(1, cb_r)]
    (rec,), lru_hist, (rec_t,) = stage_fwd("rglru_fwd", f_rglru, rgc_grid, lru_ins, [out(d_rnn, cb_r)], lru_car,
                                           transposed=[0])
    y_a = mm(rec, w_pa, name="proj_a_fwd")

    cb_q = _tile(math.gcd(math.gcd(qk, o_q), o_k), 256)
    dnc_car, dnc = [(SUBLANES, cb_q)], {}
    for nm, col0, width, w0 in (("q", o_q, qk, 0), ("k", o_k, qk, qk), ("v", o_v, vdim, 2 * qk)):
        ins_ = [act(proj, cb_q, col0, width), prm(dcw[:, w0:w0 + width], cb_q, krows(4))]
        grid_, outs_ = (width // cb_q, nt), [out(width, cb_q)]
        (y_,), hist_ = stage_fwd("dn_conv_%s_fwd" % nm, f_dn_conv, grid_, ins_, outs_, dnc_car)
        dnc[nm] = (y_, ins_, grid_, outs_, hist_)
    q_c, k_c, v_c = dnc["q"][0], dnc["k"][0], dnc["v"][0]
    f_qnorm, f_knorm = functools.partial(f_l2norm, LANES ** -0.5), functools.partial(f_l2norm, 1.0)
    qn_ins, kn_ins = [act(q_c, LANES, rows=tbh)], [act(k_c, LANES, rows=tbh)]
    nrm_grid, nrm_outs = (qk // LANES, nth), [out(qk, LANES, tbh)]
    (qn,), _ = stage_fwd("dn_qnorm_fwd", f_qnorm, nrm_grid, qn_ins, nrm_outs)
    (kn,), _ = stage_fwd("dn_knorm_fwd", f_knorm, nrm_grid, kn_ins, nrm_outs)
    gate_ins = [act(proj, LANES, o_a, LANES), act(proj, LANES, o_b, LANES),
                prm(padv(W['dn_a_log']), LANES), prm(padv(W['dn_dt_bias']), LANES)]
    gate_outs = [out(LANES, LANES), out(LANES, LANES)]
    (g_dn, beta_dn), _ = stage_fwd("dn_gates_fwd", f_dn_gates, (1, nt), gate_ins, gate_outs)
    n_ch = t // DN_CHUNK
    gt_dn = jnp.transpose(g_dn.reshape(n_ch, DN_CHUNK, LANES)[:, :, :n_vh], (0, 2, 1))
    dn_mid = delta_intra_fwd(qn, kn, v_c, 0, g_dn, gt_dn, beta_dn, n_vh)
    o_dn, dn_hist = delta_inter_fwd(*dn_mid, g_dn, n_vh)
    dno_ins = [act(o_dn, LANES, rows=tbh), act(proj, LANES, o_z, vdim, rows=tbh), prm(nw_t, LANES)]
    dno_grid, dno_outs = (n_vh, nth), [out(vdim, LANES, tbh)]
    (dn,), _, (dn_t,) = stage_fwd("dn_out_fwd", f_dn_out, dno_grid, dno_ins, dno_outs, transposed=[0])
    y_b = mm(dn, w_pb, name="proj_b_fwd")

    cb_m = _tile(math.gcd(math.gcd(d, o_ga), o_gb), 512)
    mrg_ins = [act(proj, cb_m, o_ga, d), act(proj, cb_m, o_gb, d), act(y_a, cb_m), act(y_b, cb_m)]
    mrg_grid = (d // cb_m, nt)
    (merged,), _, (merged_t,) = stage_fwd("merge_fwd", f_merge, mrg_grid, mrg_ins, [out(d, cb_m)], transposed=[0])
    mix = mm(merged, w_o, name="w_out_fwd")
    ln1_ins = [act(xs, d), act(mix, d), prm(gt1, d), prm(row1(W['ln1_g']), d), prm(row1(W['ln1_b']), d)]
    (x1,), _ = stage_fwd("ln1_fwd", f_deepnorm, (1, nt), ln1_ins, [out(d, d)])

    mod2_ins = [act(x1, d), prm(sc2, d), prm(sh2, d)]
    (h2,), _, (h2_t,) = stage_fwd("mod2_fwd", f_modulate, (1, nt), mod2_ins, [out(d, d)], transposed=[0])
    gu = mm(h2, w_gu, name="ffn_in_fwd")
    cb_f = _tile(d_ff, 256)
    ffa_ins = [act(gu, cb_f, 0, d_ff), act(gu, cb_f, d_ff, d_ff), prm(fcw, cb_f, krows(3)), prm(row1(W['ffn_conv_b']), cb_f)]
    ffa_grid, ffa_car = (d_ff // cb_f, nt), [(SUBLANES, cb_f)]
    (act_ff,), ffa_hist, (act_t,) = stage_fwd("ffn_act_fwd", f_ffn_act, ffa_grid, ffa_ins, [out(d_ff, cb_f)], ffa_car,
                                              transposed=[0])
    ff = mm(act_ff, w_fd, name="ffn_down_fwd")
    ln2_ins = [act(x1, d), act(ff, d), prm(gt2, d), prm(row1(W['ln2_g']), d), prm(row1(W['ln2_b']), d)]
    (x2,), _ = stage_fwd("ln2_fwd", f_deepnorm, (1, nt), ln2_ins, [out(d, d)])
    dy, loss_loc = loss_head(x2, tgt, tb)

    dx1_a, d_ff_o, d_gt2, d_ln2g, d_ln2b = stage_bwd("ln2_bwd", f_deepnorm, (1, nt), ln2_ins, [out(d, d)], [dy])
    d_act = mm(d_ff_o, w_fd, name="ffn_down_bwd_x", tb=True)
    gw_fd = mm(act_t, d_ff_o, name="ffn_down_bwd_w")
    d_gp, d_up, d_fcw, d_fcb = stage_bwd("ffn_act_bwd", f_ffn_act, ffa_grid, ffa_ins, [out(d_ff, cb_f)], [d_act],
                                         ffa_car, ffa_hist, gdtypes={0: MXU_DTYPE, 1: MXU_DTYPE})
    col_blocks = lambda g: jnp.transpose(g.reshape(g.shape[0], N_DEV, -1), (1, 0, 2)).astype(WIRE_DTYPE)
    row_blocks = lambda g: g.reshape(N_DEV, -1, g.shape[1]).astype(WIRE_DTYPE)
    big_parts = {}
    d_h2, big_parts['ffn_w_down'] = mm([d_gp, d_up], [w_gate, w_up], name="ffn_in_bwd_x", tb=True,
                                       scatter=[row_blocks(gw_fd)])
    gw_gate, gw_up = mm(h2_t, d_gp, name="ffn_gate_bwd_w"), mm(h2_t, d_up, name="ffn_up_bwd_w")
    d_x1, d_sc2, d_sh2 = stage_bwd("mod2_bwd", f_modulate, (1, nt), mod2_ins, [out(d, d)], [d_h2], add_to={0: dx1_a})
    dx_a, d_mix, d_gt1, d_ln1g, d_ln1b = stage_bwd("ln1_bwd", f_deepnorm, (1, nt), ln1_ins, [out(d, d)], [d_x1])
    d_merged = mm(d_mix, w_o, name="w_out_bwd_x", tb=True)
    gw_o = mm(merged_t, d_mix, name="w_out_bwd_w")
    d_ga, d_gb, d_ya, d_yb = stage_bwd("merge_bwd", f_merge, mrg_grid, mrg_ins, [out(d, cb_m)], [d_merged],
                                       gdtypes={0: MXU_DTYPE, 1: MXU_DTYPE})
    d_rec = mm(d_ya, w_pa, name="proj_a_bwd_x", tb=True)
    gw_pa = mm(rec_t, d_ya, name="proj_a_bwd_w")
    d_dn = mm(d_yb, w_pb, name="proj_b_bwd_x", tb=True)
    gw_pb = mm(dn_t, d_yb, name="proj_b_bwd_w")

    d_o, d_z, d_nwt = stage_bwd("dn_out_bwd", f_dn_out, dno_grid, dno_ins, dno_outs, [d_dn], gdtypes={1: MXU_DTYPE})
    *d_mid, d_g_state = delta_inter_bwd(*dn_mid, g_dn, dn_hist, d_o, n_vh)
    d_qn, d_kn, d_v, d_g_col, d_gt, d_beta = delta_intra_bwd(qn, kn, v_c, 0, g_dn, gt_dn, beta_dn, d_mid, n_vh)
    d_g_row = jnp.pad(jnp.transpose(d_gt, (0, 2, 1)).reshape(t, n_vh), ((0, 0), (0, LANES - n_vh)))
    (d_qc,) = stage_bwd("dn_qnorm_bwd", f_qnorm, nrm_grid, qn_ins, nrm_outs, [d_qn])
    (d_kc,) = stage_bwd("dn_knorm_bwd", f_knorm, nrm_grid, kn_ins, nrm_outs, [d_kn])
    d_a, d_b, d_alog, d_dtb = stage_bwd("dn_gates_bwd", f_dn_gates, (1, nt), gate_ins, gate_outs,
                                        [(d_g_state, d_g_col, d_g_row), d_beta], gdtypes={0: MXU_DTYPE, 1: MXU_DTYPE})
    d_win, d_dcw = {}, []
    for nm, cot in (("q", d_qc), ("k", d_kc), ("v", d_v)):
        _, ins_, grid_, outs_, hist_ = dnc[nm]
        d_win[nm], dw_ = stage_bwd("dn_conv_%s_bwd" % nm, f_dn_conv, grid_, ins_, outs_, [cot], dnc_car, hist_,
                                   gdtypes={0: MXU_DTYPE})
        d_dcw.append(dw_)
    d_dcw = jnp.concatenate(d_dcw, axis=1)

    d_xc_a, d_pr, d_pi, d_gr, d_ba, d_bx, d_lam = stage_bwd(
        "rglru_bwd", f_rglru, rgc_grid, lru_ins, [out(d_rnn, cb_r)], [d_rec], lru_car, lru_hist,
        gdtypes={1: MXU_DTYPE, 2: MXU_DTYPE, 3: MXU_DTYPE})
    (d_xc_b, big_parts['ffn_w_gate'], big_parts['ffn_w_up'], big_parts['w_out'], big_parts['w_proj_a'],
     big_parts['w_proj_b']) = mm([d_pr, d_pi], [w_bd[:, :d_rnn], w_bd[:, d_rnn:]], name="rg_gates_bwd_x", tb=True,
                                 scatter=[col_blocks(gw_gate), col_blocks(gw_up), row_blocks(gw_o),
                                          row_blocks(gw_pa), row_blocks(gw_pb)])
    gw_bd_a, gw_bd_x = mm(xc_t, d_pr, name="rg_gate_a_bwd_w"), mm(xc_t, d_pi, name="rg_gate_x_bwd_w")
    d_xr, d_rcw, d_rcb = stage_bwd("rg_conv_bwd", f_rg_conv, rgc_grid, rgc_ins, [out(d_rnn, cb_r)], [(d_xc_a, d_xc_b)],
                                   rgc_car, rgc_hist, gdtypes={0: MXU_DTYPE})

    diag = lambda g: jnp.einsum('nimj,nm->nij', g.reshape(n_blk, d_rnn // n_blk, n_blk, d_rnn // n_blk), eye_b)
    small_names = ['rg_conv_w', 'rg_conv_b', 'rg_w_a', 'rg_b_a', 'rg_w_x', 'rg_b_x', 'rg_lambda', 'dn_conv_w',
                   'dn_a_log', 'dn_dt_bias', 'dn_norm_w', 'ln1_g', 'ln1_b', 'ffn_conv_w', 'ffn_conv_b', 'ln2_g', 'ln2_b']
    small_loc = {
        'rg_conv_w': d_rcw, 'rg_conv_b': d_rcb,
        'rg_w_a': diag(gw_bd_a), 'rg_b_a': d_ba, 'rg_w_x': diag(gw_bd_x), 'rg_b_x': d_bx,
        'rg_lambda': d_lam, 'dn_conv_w': d_dcw, 'dn_a_log': d_alog[:, :n_vh], 'dn_dt_bias': d_dtb[:, :n_vh],
        'dn_norm_w': jnp.sum(d_nwt.reshape(n_vh, LANES), axis=0), 'ln1_g': d_ln1g, 'ln1_b': d_ln1b,
        'ffn_conv_w': d_fcw, 'ffn_conv_b': d_fcb, 'ln2_g': d_ln2g, 'ln2_b': d_ln2b}
    small_list = [small_loc[n] for n in small_names]

    d_segs = [d_xr, d_gr, d_win["q"], d_win["k"], d_win["v"], d_z, d_a, d_b, d_ga, d_gb]
    gw_segs = [mm(h1_t, dg, name="proj_bwd_w%d" % i) for i, dg in enumerate(d_segs)]
    gw_in = jnp.concatenate([g_[:, :splits[i]] for i, g_ in enumerate(gw_segs)], axis=1)
    d_h1, small_all, big_parts['w_in'] = mm(d_segs, groups, name="proj_bwd_x", tb=True,
                                            gather=[_pack(small_list)], scatter=[col_blocks(gw_in)], **MM_SPLIT_CAPS)
    grad_x, d_sc1, d_sh1 = stage_bwd("mod1_bwd", f_modulate, (1, nt), mod1_ins, [out(d, d)], [d_h1], add_to={0: dx_a})

    g_small = dict(zip(small_names, _unpack(sum_parts("sum_small_grads", small_all), small_list)))
    d_ada_me = jnp.concatenate([d_sh1, d_sc1, d_gt1, d_sh2, d_sc2, d_gt2], axis=1)
    (d_ada_all,) = all_gather("gather_d_ada", [d_ada_me.reshape(-1, LANES)])
    g_small['b_ada'] = sum_parts("sum_d_ada", d_ada_all)
    small_names = ['b_ada'] + small_names
    d_ada_cols = lax.dynamic_slice(d_ada_all.reshape(N_DEV, 6 * d), (0, me * ada_w), (N_DEV, ada_w))
    d_ada_pad = jnp.pad(d_ada_cols, ((0, LANES - N_DEV), (0, 0)))
    gw_ada = mm(c_pad, d_ada_pad, name="ada_bwd_w", ta=True, a_act="silu")

    res = {}
    big_parts['w_ada'] = gw_ada[None]
    for n in ['w_ada'] + big:
        res[n] = adamw("adamw_" + n, W[n], big_parts[n], M[n], V[n])
    for n in small_sh:
        w_ = W[n].shape[1]
        g_small[n] = lax.dynamic_slice(g_small[n], (0, me * w_), (W[n].shape[0], w_))
    for n in small_names:
        g_small[n] = g_small[n].reshape(W[n].shape)
    pk = lambda dct: _pack([dct[n] for n in small_names])
    s_g, s_d, s_m, s_v = adamw("adamw_small", pk(W), pk(g_small)[None], pk(M), pk(V))
    like = [W[n] for n in small_names]
    for n, g_, d_, m_, v_ in zip(small_names, _unpack(s_g, like), _unpack(s_d, like), _unpack(s_m, like), _unpack(s_v, like)):
        res[n] = (g_, d_, m_, v_)

    loss = lax.psum(loss_loc[0, 0], ("x", "y", "c"))
    outs = [loss, grad_x[None]]
    for j in range(4):
        outs += [res[n][j].reshape(loc[n].shape) for n in names]
    return tuple(outs)
```

```python
import functools
import math

import jax
import jax.numpy as jnp
from jax import lax
from jax.experimental import pallas as pl
from jax.experimental.pallas import tpu as pltpu

F32 = jnp.float32
BF16 = jnp.bfloat16
MXU_DTYPE = BF16
WIRE_DTYPE = BF16
DN_DTYPE = BF16
HI = lax.Precision.HIGHEST
MESH = pl.DeviceIdType.MESH

N_DEV = 8
LANES = 128
SUBLANES = 8
VMEM_LIMIT = 56 * 1024 * 1024
MM_TM_CAP, MM_TN_CAP, MM_TK_CAP = 1024, 1536, 2048
MM_SPLIT_CAPS = dict(tm_cap=1024, tn_cap=1024, tk_cap=1024)

RG_C = 8.0
DN_CHUNK = 64
DN_HEAD_GROUP = 8
LN_EPS = 1e-5
RMS_EPS = 1e-6
L2_EPS = 1e-6
DEPTH = 1
DEEPNORM_ALPHA = (2 * DEPTH) ** 0.25
ADAM_LR = 0.001
ADAM_B1 = 0.9
ADAM_B2 = 0.999
ADAM_EPS = 1e-08
ADAM_WD = 0.01
ADAM_STEP = 10


def _tile(n, cap, unit=LANES):
    best = None
    for t in range(unit, min(n, cap) + 1, unit):
        if n % t == 0:
            best = t
    return best if best is not None else n


def _round_up(n, m):
    return (n + m - 1) // m * m


_HBM = pl.BlockSpec(memory_space=pl.ANY)


def _exchange_sems(n):
    return [pltpu.SemaphoreType.DMA((n, N_DEV - 1)), pltpu.SemaphoreType.DMA((n, N_DEV - 1)),
            pltpu.SemaphoreType.DMA((n,))]


def _exchange_out_shape(arrs, scatter):
    return [jax.ShapeDtypeStruct(a.shape if scatter else (N_DEV,) + a.shape, a.dtype) for a in arrs]


def _exchange_copies(in_refs, out_refs, sems, scatter, phase):
    send_sems, recv_sems, local_sems = sems
    x, y, c = lax.axis_index("x"), lax.axis_index("y"), lax.axis_index("c")
    me = 4 * x + 2 * y + c
    peers = [(x ^ ((k >> 2) & 1), y ^ ((k >> 1) & 1), c ^ (k & 1)) for k in range(N_DEV)]
    row = [4 * p[0] + 2 * p[1] + p[2] for p in peers]
    n = len(in_refs)

    def local(i):
        return pltpu.make_async_copy(in_refs[i].at[me] if scatter else in_refs[i], out_refs[i].at[me], local_sems.at[i])

    def remote(i, k, src, dst_row, to):
        return pltpu.make_async_remote_copy(src_ref=src, dst_ref=out_refs[i].at[dst_row],
                                            send_sem=send_sems.at[i, k - 1], recv_sem=recv_sems.at[i, k - 1],
                                            device_id=to, device_id_type=MESH)

    if scatter:
        sends = [(i, k, in_refs[i].at[row[k]], me, peers[k]) for k in range(1, N_DEV) for i in range(n)]
        passed = []
    else:
        sends = [(i, k, in_refs[i], me, peers[k]) for k in (1, 2, 4, 6) for i in range(n)]
        passed = [(i, k + 1, out_refs[i].at[row[k]], row[k], peers[1]) for k in (2, 4, 6) for i in range(n)]
    arrival = lambda i, k: remote(i, k, in_refs[i].at[me] if scatter else in_refs[i], row[k], peers[k])

    if phase == "start":
        for i in range(n):
            local(i).start()
        for cp in sends:
            remote(*cp).start()
    else:
        for cp in passed:
            arrival(cp[0], cp[1] - 1).wait_recv()
            remote(*cp).start()
        waited = {(cp[0], cp[1] - 1) for cp in passed}
        for k in range(1, N_DEV):
            for i in range(n):
                if (i, k) not in waited:
                    arrival(i, k).wait_recv()
        for cp in sends + passed:
            remote(*cp).wait_send()
        for i in range(n):
            local(i).wait()


def mm(a, b, *, name, ta=False, tb=False, a_act=None, bias=None, out_dtype=F32,
       tm_cap=MM_TM_CAP, tn_cap=MM_TN_CAP, tk_cap=MM_TK_CAP, gather=(), scatter=()):
    a_segs = list(a) if isinstance(a, (list, tuple)) else [a]
    b_segs = list(b) if isinstance(b, (list, tuple)) else [b]
    ns = len(a_segs)
    assert ns == len(b_segs) and (ns == 1 or a_act is None)
    m = a_segs[0].shape[1] if ta else a_segs[0].shape[0]
    n = b_segs[0].shape[0] if tb else b_segs[0].shape[1]
    ks = [x.shape[0] if ta else x.shape[1] for x in a_segs]
    assert ks == [y.shape[1] if tb else y.shape[0] for y in b_segs], (ks, ta, tb)
    tm, tn = _tile(m, tm_cap), _tile(n, tn_cap)
    tks = [_tile(k_, tk_cap) for k_ in ks]
    cnt = [k_ // t_ for k_, t_ in zip(ks, tks)]
    lo = [sum(cnt[:s]) for s in range(ns)]
    nk = sum(cnt)
    grid = (m // tm, n // tn, nk)
    dims = (((0 if ta else 1,), (1 if tb else 0,)), ((), ()))
    xch = list(gather) + list(scatter)
    nx, ng = len(xch), len(gather)
    n_main = 2 * ns + (bias is not None)

    def body(*refs):
        a_refs, b_refs = refs[:ns], refs[ns:2 * ns]
        bias_ref = refs[2 * ns] if bias is not None else None
        x_in, o_ref, x_out = refs[n_main:n_main + nx], refs[n_main + nx], refs[n_main + nx + 1:n_main + 2 * nx + 1]
        rest = refs[n_main + 2 * nx + 1:]
        acc_ref = rest[0] if nk > 1 else None
        sems = rest[1 if nk > 1 else 0:]
        groups = []
        if ng:
            groups.append((x_in[:ng], x_out[:ng], sems[:3], False))
        if nx > ng:
            groups.append((x_in[ng:], x_out[ng:], sems[-3:], True))
        if nx:
            step = (pl.program_id(0) * grid[1] + pl.program_id(1)) * grid[2] + pl.program_id(2)

            @pl.when(step == 0)
            def _():
                for gi, go_, gs, sc in groups:
                    _exchange_copies(gi, go_, gs, sc, "start")
        kk = pl.program_id(2)

        def finish(r):
            if bias is not None:
                r = r + bias_ref[...]
            o_ref[...] = r.astype(o_ref.dtype)

        def segment(s):
            av = a_refs[s][...]
            if a_act == "silu":
                av = jax.nn.silu(av.astype(F32))
            prod = lax.dot_general(av.astype(MXU_DTYPE), b_refs[s][...].astype(MXU_DTYPE), dims,
                                   preferred_element_type=F32)
            if nk == 1:
                finish(prod)
                return
            opens, closes = lo[s] == 0, lo[s] + cnt[s] == nk
            if opens:
                @pl.when(kk == 0)
                def _():
                    acc_ref[...] = prod
            inner = [kk > 0] * opens + [kk < nk - 1] * closes
            if inner:
                @pl.when(functools.reduce(lambda p, q: p & q, inner))
                def _():
                    acc_ref[...] += prod
            else:
                acc_ref[...] += prod
            if closes:
                @pl.when(kk == nk - 1)
                def _():
                    finish(acc_ref[...] + prod)

        for s in range(ns):
            if ns == 1:
                segment(s)
            else:
                pl.when((kk >= lo[s]) & (kk < lo[s] + cnt[s]))(functools.partial(segment, s))

        if nx:
            @pl.when(step == grid[0] * grid[1] * grid[2] - 1)
            def _():
                for gi, go_, gs, sc in groups:
                    _exchange_copies(gi, go_, gs, sc, "wait")

    def seg_index(s):
        return lambda q: jnp.clip(q - lo[s], 0, cnt[s] - 1) if ns > 1 else q

    a_specs, b_specs = [], []
    for s in range(ns):
        qi, tk = seg_index(s), tks[s]
        a_specs.append(pl.BlockSpec((tk, tm), (lambda qi: lambda i, j, q: (qi(q), i))(qi)) if ta
                       else pl.BlockSpec((tm, tk), (lambda qi: lambda i, j, q: (i, qi(q)))(qi)))
        b_specs.append(pl.BlockSpec((tn, tk), (lambda qi: lambda i, j, q: (j, qi(q)))(qi)) if tb
                       else pl.BlockSpec((tk, tn), (lambda qi: lambda i, j, q: (qi(q), j))(qi)))
    in_specs, args = a_specs + b_specs, a_segs + b_segs
    if bias is not None:
        in_specs.append(pl.BlockSpec((1, tn), lambda i, j, q: (0, j)))
        args.append(bias)
    o_spec, o_shape = pl.BlockSpec((tm, tn), lambda i, j, q: (i, j)), jax.ShapeDtypeStruct((m, n), out_dtype)
    acc = [pltpu.VMEM((tm, tn), F32)] if nk > 1 else []
    if not nx:
        return pl.pallas_call(
            body, name=name, grid=grid, in_specs=in_specs, out_specs=o_spec, out_shape=o_shape, scratch_shapes=acc,
            compiler_params=pltpu.CompilerParams(dimension_semantics=("parallel", "parallel", "arbitrary"),
                                                 vmem_limit_bytes=VMEM_LIMIT),
        )(*args)
    return pl.pallas_call(
        body, name=name, grid=grid, in_specs=in_specs + [_HBM] * nx, out_specs=[o_spec] + [_HBM] * nx,
        out_shape=[o_shape] + _exchange_out_shape(list(gather), False) + _exchange_out_shape(list(scatter), True),
        scratch_shapes=acc + (_exchange_sems(ng) if ng else []) + (_exchange_sems(nx - ng) if nx > ng else []),
        compiler_params=pltpu.CompilerParams(dimension_semantics=("arbitrary", "arbitrary", "arbitrary"),
                                             vmem_limit_bytes=VMEM_LIMIT, has_side_effects=True),
    )(*args, *xch)


class In:
    def __init__(self, arr, block, imap, acc=False, grad=True, parts=None, gshape=None, gimap=None):
        self.arr, self.block, self.imap, self.acc, self.grad, self.parts = arr, block, imap, acc, grad, parts
        self.gshape = arr.shape if gshape is None else gshape
        self.gimap = imap if gimap is None else gimap


class Out:
    def __init__(self, shape, block, imap, dtype=F32):
        self.shape, self.block, self.imap, self.dtype = shape, block, imap, dtype


def _load(in_refs, ins):
    vals = []
    for r, i in zip(in_refs, ins):
        if i.parts is None:
            vals.append(r[...])
        else:
            vals.extend(r[p] for p in i.parts)
    return vals


def _stage_params():
    return pltpu.CompilerParams(dimension_semantics=("parallel", "arbitrary"), vmem_limit_bytes=VMEM_LIMIT)


def stage_fwd(name, f, grid, ins, outs, carries=(), transposed=()):
    n_in, n_out, n_c, n_t = len(ins), len(outs), len(carries), len(transposed)

    def body(*refs):
        in_refs, out_refs = refs[:n_in], refs[n_in:n_in + n_out]
        hist_refs = refs[n_in + n_out:n_in + n_out + n_c]
        t_refs = refs[n_in + n_out + n_c:n_in + n_out + n_c + n_t]
        c_refs = refs[n_in + n_out + n_c + n_t:]
        if n_c:
            @pl.when(pl.program_id(1) == 0)
            def _():
                for c in c_refs:
                    c[...] = jnp.zeros_like(c)
        cin = [c[...] for c in c_refs]
        for h, c in zip(hist_refs, cin):
            h[...] = c
        o, cout = f(*_load(in_refs, ins), *cin)
        for r, v in zip(out_refs, o):
            r[...] = v.astype(r.dtype)
        for r, k in zip(t_refs, transposed):
            r[...] = o[k].T.astype(r.dtype)
        for c, v in zip(c_refs, cout):
            c[...] = v

    hist_spec = lambda c: pl.BlockSpec((None, None) + tuple(c), lambda o, s: (o, s) + (0,) * len(c))
    flip = lambda o_: pl.BlockSpec(o_.block[::-1], (lambda im: lambda o, s: im(o, s)[::-1])(o_.imap))
    res = pl.pallas_call(
        body, name=name, grid=grid,
        in_specs=[pl.BlockSpec(i.block, i.imap) for i in ins],
        out_specs=[pl.BlockSpec(o.block, o.imap) for o in outs] + [hist_spec(c) for c in carries]
        + [flip(outs[k]) for k in transposed],
        out_shape=[jax.ShapeDtypeStruct(o.shape, o.dtype) for o in outs]
        + [jax.ShapeDtypeStruct(tuple(grid) + tuple(c), F32) for c in carries]
        + [jax.ShapeDtypeStruct(outs[k].shape[::-1], MXU_DTYPE) for k in transposed],
        scratch_shapes=[pltpu.VMEM(tuple(c), F32) for c in carries],
        compiler_params=_stage_params(),
    )(*[i.arr for i in ins])
    res = list(res)
    if transposed:
        return res[:n_out], res[n_out:n_out + n_c], res[n_out + n_c:]
    return res[:n_out], res[n_out:]


def stage_bwd(name, f, grid, ins, outs, cots, carries=(), hists=(), add_to=None, gdtypes=None):
    n_in, n_out, n_c = len(ins), len(outs), len(carries)
    ns = grid[1]
    add_to = add_to or {}
    gdtypes = gdtypes or {}
    add_idx = sorted(add_to)
    g_idx = [k for k, i in enumerate(ins) if i.grad]
    cots = [c if isinstance(c, (tuple, list)) else (c,) for c in cots]
    n_cot = [len(c) for c in cots]
    rev = lambda imap: (lambda o, s: imap(o, ns - 1 - s))

    def body(*refs):
        p = 0
        in_refs = refs[p:p + n_in]; p += n_in
        cot_refs = []
        for cnt in n_cot:
            cot_refs.append(refs[p:p + cnt]); p += cnt
        hist_refs = refs[p:p + n_c]; p += n_c
        add_refs = refs[p:p + len(add_idx)]; p += len(add_idx)
        g_refs = refs[p:p + len(g_idx)]; p += len(g_idx)
        dc_refs = refs[p:]
        first = pl.program_id(1) == 0
        if n_c:
            @pl.when(first)
            def _():
                for c in dc_refs:
                    c[...] = jnp.zeros_like(c)
        vals = _load(in_refs, ins)
        cin = [h[...] for h in hist_refs]
        (o, cout), vjp = jax.vjp(lambda *a: f(*a), *vals, *cin)
        cot_o = []
        for crs, v in zip(cot_refs, o):
            c = crs[0][...].astype(v.dtype)
            for extra in crs[1:]:
                c = c + extra[...].astype(v.dtype)
            cot_o.append(c)
        cot_c = tuple(c[...] for c in dc_refs)
        grads = vjp((tuple(cot_o), cot_c))
        pos, per_in = 0, []
        for i in ins:
            cnt = 1 if i.parts is None else len(i.parts)
            per_in.append(grads[pos:pos + cnt])
            pos += cnt
        dcin = grads[pos:]
        for gr, k in zip(g_refs, g_idx):
            i, gs = ins[k], per_in[k]
            if i.acc:
                @pl.when(first)
                def _(gr=gr):
                    gr[...] = jnp.zeros_like(gr)
                if i.parts is None:
                    gr[...] += gs[0].astype(gr.dtype)
                else:
                    for pt, g in zip(i.parts, gs):
                        gr[pt] += g.astype(gr.dtype)
            else:
                g = gs[0]
                if k in add_to:
                    g = g + add_refs[add_idx.index(k)][...].astype(g.dtype)
                gr[...] = g.astype(gr.dtype)
        for c, v in zip(dc_refs, dcin):
            c[...] = v

    in_specs = [pl.BlockSpec(i.block, rev(i.imap)) for i in ins]
    for o_, cnt in zip(outs, n_cot):
        in_specs += [pl.BlockSpec(o_.block, rev(o_.imap))] * cnt
    in_specs += [pl.BlockSpec((None, None) + tuple(c), (lambda c: (lambda o, s: (o, ns - 1 - s) + (0,) * len(c)))(c))
                 for c in carries]
    in_specs += [pl.BlockSpec(ins[k].block, rev(ins[k].gimap)) for k in add_idx]
    out_specs, out_shape = [], []
    for k in g_idx:
        i = ins[k]
        if i.acc:
            out_specs.append(pl.BlockSpec(i.block, (lambda im: (lambda o, s: im(o, 0)))(i.imap)))
        else:
            out_specs.append(pl.BlockSpec(i.block, rev(i.gimap)))
        out_shape.append(jax.ShapeDtypeStruct(i.gshape, gdtypes.get(k, F32)))
    res = pl.pallas_call(
        body, name=name, grid=grid, in_specs=in_specs, out_specs=out_specs, out_shape=out_shape,
        scratch_shapes=[pltpu.VMEM(tuple(c), F32) for c in carries],
        compiler_params=_stage_params(),
    )(*[i.arr for i in ins], *[a for c in cots for a in c], *hists, *[add_to[k] for k in add_idx])
    return list(res)


def _iota_rows(shape):
    return lax.broadcasted_iota(jnp.int32, shape, 0)


@functools.partial(jax.custom_vjp, nondiff_argnums=(1,))
def _roll_rows(x, s):
    return pltpu.roll(x, s % x.shape[0], 0)


def _roll_rows_fwd(x, s):
    return _roll_rows(x, s), None


def _roll_rows_bwd(s, _, g):
    return (_roll_rows(g, -s),)


_roll_rows.defvjp(_roll_rows_fwd, _roll_rows_bwd)


@jax.custom_vjp
def _drop_head(xx):
    return xx[SUBLANES:]


def _drop_head_fwd(xx):
    return xx[SUBLANES:], None


def _drop_head_bwd(_, g):
    return (jnp.concatenate([jnp.zeros((SUBLANES, g.shape[1]), g.dtype), g], axis=0),)


_drop_head.defvjp(_drop_head_fwd, _drop_head_bwd)


@jax.custom_vjp
def _last_rows(x):
    return x[x.shape[0] - SUBLANES:]


def _last_rows_fwd(x):
    return x[x.shape[0] - SUBLANES:], x.shape[0]


def _last_rows_bwd(n, g):
    return (jnp.concatenate([jnp.zeros((n - SUBLANES, g.shape[1]), g.dtype), g], axis=0),)


_last_rows.defvjp(_last_rows_fwd, _last_rows_bwd)


def _last_row(x):
    n = x.shape[0]
    return jnp.sum(jnp.where(_iota_rows(x.shape) == n - 1, x, 0.0), axis=0, keepdims=True)


def _scan_steps(n):
    s = 1
    while s < n:
        yield s
        s *= 2


def _block_scan_impl(a, u, h0):
    n = a.shape[0]
    row = _iota_rows(a.shape)
    for s in _scan_steps(n):
        keep = row >= s
        a_s = jnp.where(keep, pltpu.roll(a, s, 0), 1.0)
        u_s = jnp.where(keep, pltpu.roll(u, s, 0), 0.0)
        u = u + a * u_s
        a = a * a_s
    return u + a * h0


@jax.custom_vjp
def _block_scan(a, u, h0):
    return _block_scan_impl(a, u, h0)


def _block_scan_fwd(a, u, h0):
    h = _block_scan_impl(a, u, h0)
    return h, (a, h, h0)


def _block_scan_bwd(res, dh):
    a, h, h0 = res
    n = a.shape[0]
    row = _iota_rows(a.shape)
    b = jnp.where(row < n - 1, pltpu.roll(a, n - 1, 0), 0.0)
    lam = dh
    for s in _scan_steps(n):
        keep = row < n - s
        b_s = jnp.where(keep, pltpu.roll(b, n - s, 0), 1.0)
        l_s = jnp.where(keep, pltpu.roll(lam, n - s, 0), 0.0)
        lam = lam + b * l_s
        b = b * b_s
    h_prev = jnp.where(row >= 1, pltpu.roll(h, 1, 0), jnp.broadcast_to(h0, h.shape))
    d_h0 = jnp.sum(jnp.where(row == 0, a * lam, 0.0), axis=0, keepdims=True)
    return lam * h_prev, lam, d_h0


_block_scan.defvjp(_block_scan_fwd, _block_scan_bwd)


def _dot_hi(a, b, dims=(((1,), (0,)), ((), ()))):
    return lax.dot_general(a, b, dims, precision=HI, preferred_element_type=F32)


_NN, _NT, _TN = "nn", "nt", "tn"
_CONTRACT = {_NN: (1, 0), _NT: (1, 1), _TN: (0, 0)}


def _raw_dot(a, b, kind):
    ca, cb = _CONTRACT[kind]
    lead = a.ndim - 2
    dims = (((ca + lead,), (cb + lead,)), (tuple(range(lead)), tuple(range(lead))))
    return lax.dot_general(a.astype(DN_DTYPE), b.astype(DN_DTYPE), dims, preferred_element_type=F32)


@jax.custom_vjp
def _nn(a, b):
    return _raw_dot(a, b, _NN)


_nn.defvjp(lambda a, b: (_raw_dot(a, b, _NN), (a, b)),
           lambda r, g: (_raw_dot(g, r[1], _NT), _raw_dot(r[0], g, _TN)))


@jax.custom_vjp
def _nt(a, b):
    return _raw_dot(a, b, _NT)


_nt.defvjp(lambda a, b: (_raw_dot(a, b, _NT), (a, b)),
           lambda r, g: (_raw_dot(g, r[1], _NN), _raw_dot(g, r[0], _TN)))


@jax.custom_vjp
def _tn(a, b):
    return _raw_dot(a, b, _TN)


_tn.defvjp(lambda a, b: (_raw_dot(a, b, _TN), (a, b)),
           lambda r, g: (_raw_dot(r[1], g, _NT), _raw_dot(r[0], g, _NN)))


def _neumann_inverse(a):
    n = a.shape[-1]
    eye = (lax.broadcasted_iota(jnp.int32, (n, n), 0) == lax.broadcasted_iota(jnp.int32, (n, n), 1)).astype(F32)
    p = _raw_dot(a, a, _NN)
    e = p
    for _ in range(int(math.log2(n)) - 2):
        p = _raw_dot(p, p, _NN)
        e = e + p + _raw_dot(e, p, _NN)
    return eye - a + e - _raw_dot(a, e, _NN)


@jax.custom_vjp
def _unit_lower_inverse(a):
    return _neumann_inverse(a)


def _unit_lower_inverse_fwd(a):
    x = _neumann_inverse(a)
    return x, x


def _unit_lower_inverse_bwd(x, g):
    return (-_raw_dot(_raw_dot(x, g, _TN), x, _NT),)


_unit_lower_inverse.defvjp(_unit_lower_inverse_fwd, _unit_lower_inverse_bwd)


def _softplus(x):
    return jnp.maximum(x, 0.0) + jnp.log1p(jnp.exp(-jnp.abs(x)))


def _neg_expm1(x):
    series = -x * (1.0 + x * (0.5 + x * (1.0 / 6.0 + x * (1.0 / 24.0 + x * (1.0 / 120.0)))))
    return jnp.where(x > -0.03, series, 1.0 - jnp.exp(x))


def f_modulate(x, sc, sh):
    return (x * (1.0 + sc) + sh,), ()


def f_deepnorm(x, y, gt, g, b):
    v = DEEPNORM_ALPHA * x + (1.0 + gt) * y
    mu = jnp.mean(v, axis=-1, keepdims=True)
    vc = v - mu
    var = jnp.mean(vc * vc, axis=-1, keepdims=True)
    return (vc * lax.rsqrt(var + LN_EPS) * g + b,), ()


def _causal_conv(x, prev, ws):
    xx = jnp.concatenate([prev, x], axis=0)
    k = len(ws)
    y = ws[k - 1] * x
    for j in range(k - 1):
        y = y + ws[j] * _drop_head(_roll_rows(xx, k - 1 - j))
    return y


def f_rg_conv(x, w0, w1, w2, w3, b, prev):
    return (_causal_conv(x, prev, (w0, w1, w2, w3)) + b,), (_last_rows(x),)


def f_dn_conv(x, w0, w1, w2, w3, prev):
    return (jax.nn.silu(_causal_conv(x, prev, (w0, w1, w2, w3))),), (_last_rows(x),)


def f_ffn_act(gp, up, w0, w1, w2, b, prev):
    return (jax.nn.gelu(_causal_conv(gp, prev, (w0, w1, w2)) + b) * up,), (_last_rows(gp),)


def f_rglru(xc, pre_r, pre_i, gr, b_a, b_x, lam, h0):
    gate_r = jax.nn.sigmoid(pre_r + b_a)
    gate_i = jax.nn.sigmoid(pre_i + b_x)
    log_a = -RG_C * gate_r * _softplus(-lam)
    a = jnp.exp(log_a)
    mult = jnp.sqrt(_neg_expm1(2.0 * log_a))
    h = _block_scan(a, mult * gate_i * xc, h0)
    return (h * jax.nn.gelu(gr),), (_last_row(h),)


def f_l2norm(scale, x):
    return (x * lax.rsqrt(jnp.sum(x * x, axis=-1, keepdims=True) + L2_EPS) * scale,), ()


def f_dn_gates(a_in, b_in, a_log, dt_bias):
    g = -jnp.exp(a_log) * _softplus(a_in + dt_bias)
    n = g.shape[0]
    shift = int(math.log2(DN_CHUNK))
    ri = lax.broadcasted_iota(jnp.int32, (n, n), 0)
    ci = lax.broadcasted_iota(jnp.int32, (n, n), 1)
    tri = ((lax.shift_right_logical(ri, shift) == lax.shift_right_logical(ci, shift)) & (ri >= ci)).astype(F32)
    return (_dot_hi(tri, g), jax.nn.sigmoid(b_in)), ()


def f_dn_out(o, z, nw):
    r = lax.rsqrt(jnp.mean(o * o, axis=-1, keepdims=True) + RMS_EPS)
    return (o * r * nw * jax.nn.silu(z),), ()


def f_merge(ga, gb, ya, yb):
    return (jax.nn.sigmoid(ga) * ya + jax.nn.sigmoid(gb) * yb,), ()


def _delta_intra(q, k, v, g_i, g_j, beta):
    c = q.shape[-2]
    ri = lax.broadcasted_iota(jnp.int32, (c, c), 0)
    ci = lax.broadcasted_iota(jnp.int32, (c, c), 1)
    decay = jnp.exp(jnp.where(ri >= ci, g_i - g_j, -jnp.inf))
    g_last = jnp.sum(jnp.where(_iota_rows((c, 1)) == c - 1, g_i, 0.0), axis=-2, keepdims=True)
    exp_g = jnp.exp(g_i)
    kb = k * beta
    t_inv = _unit_lower_inverse(jnp.where(ri > ci, _nt(kb, k) * decay, 0.0))
    u = _nn(t_inv, v * beta)
    w = _nn(t_inv, kb * exp_g)
    return u, w, _nt(q, k) * decay, q * exp_g, k * jnp.exp(g_last - g_i)


def _delta_inter(u, w, qk, q_dec, k_dec, g_last, state):
    v_new = u - _nn(w, state)
    o = _nn(q_dec, state) + _nn(qk, v_new)
    return o, jnp.exp(g_last) * state + _tn(k_dec, v_new)


def _chunk_spec(width, nc=None, col=0):
    if nc is None:
        return pl.BlockSpec((DN_CHUNK, width), lambda s: (s, col))
    return pl.BlockSpec((DN_CHUNK, width), lambda s: (nc - 1 - s, col))


def _delta_params(sem):
    return pltpu.CompilerParams(dimension_semantics=(sem,), vmem_limit_bytes=VMEM_LIMIT)


def _head(ref, h, width=LANES):
    return ref[:, h * LANES:h * LANES + width]


def _head_groups(n_vh):
    hb = min(DN_HEAD_GROUP, n_vh)
    return [range(h0, h0 + hb) for h0 in range(0, n_vh, hb)]


def _stack(hs, f):
    return jnp.stack([f(h) for h in hs])


def _intra_operands(hs, rep, q_ref, k_ref, v_ref, g_ref, gt_ref, b_ref):
    return (_stack(hs, lambda h: _head(q_ref, h // rep)), _stack(hs, lambda h: _head(k_ref, h // rep)),
            _stack(hs, lambda h: _head(v_ref, h)), _stack(hs, lambda h: g_ref[:, h:h + 1]),
            _stack(hs, lambda h: gt_ref[h:h + 1, :]), _stack(hs, lambda h: b_ref[:, h:h + 1]))


def _inter_operands(hs, u_ref, w_ref, qk_ref, qd_ref, kd_ref, g_ref):
    f32 = lambda ref, width=LANES: _stack(hs, lambda h: _head(ref, h, width).astype(F32))
    return (f32(u_ref), f32(w_ref), f32(qk_ref, DN_CHUNK), f32(qd_ref), f32(kd_ref),
            _stack(hs, lambda h: g_ref[DN_CHUNK - 1:DN_CHUNK, h:h + 1]))


def delta_intra_fwd(qn, kn, qkv, v_blk, big_g, big_gt, beta, n_vh):
    t, qk_w = qn.shape
    vdim = n_vh * LANES
    rep = vdim // qk_w
    nc = t // DN_CHUNK

    def body(q_ref, k_ref, v_ref, g_ref, gt_ref, b_ref, u_ref, w_ref, qk_ref, qd_ref, kd_ref):
        for hs in _head_groups(n_vh):
            u, w, qk, qd, kd = _delta_intra(*_intra_operands(hs, rep, q_ref, k_ref, v_ref, g_ref, gt_ref, b_ref))
            for i, h in enumerate(hs):
                sl = slice(h * LANES, (h + 1) * LANES)
                u_ref[:, sl] = u[i]
                w_ref[:, sl] = w[i].astype(w_ref.dtype)
                qk_ref[:, sl] = jnp.concatenate([qk[i], jnp.zeros_like(qk[i])], axis=1).astype(qk_ref.dtype)
                qd_ref[:, sl] = qd[i].astype(qd_ref.dtype)
                kd_ref[:, sl] = kd[i].astype(kd_ref.dtype)

    return pl.pallas_call(
        body, name="delta_intra_fwd", grid=(nc,),
        in_specs=[_chunk_spec(qk_w), _chunk_spec(qk_w), _chunk_spec(vdim, col=v_blk), _chunk_spec(LANES),
                  pl.BlockSpec((None, n_vh, DN_CHUNK), lambda s: (s, 0, 0)), _chunk_spec(LANES)],
        out_specs=[_chunk_spec(vdim)] * 5,
        out_shape=[jax.ShapeDtypeStruct((t, vdim), F32)] + [jax.ShapeDtypeStruct((t, vdim), DN_DTYPE)] * 4,
        compiler_params=_delta_params("parallel"),
    )(qn, kn, qkv, big_g, big_gt, beta)


def delta_inter_fwd(u, w, qk, q_dec, k_dec, big_g, n_vh):
    t, vdim = u.shape
    nc = t // DN_CHUNK

    def body(u_ref, w_ref, qk_ref, qd_ref, kd_ref, g_ref, o_ref, hist_ref, s_ref):
        @pl.when(pl.program_id(0) == 0)
        def _():
            s_ref[...] = jnp.zeros_like(s_ref)
        for hs in _head_groups(n_vh):
            grp = slice(hs[0], hs[-1] + 1)
            st = s_ref[grp]
            hist_ref[grp] = st
            o, ns = _delta_inter(*_inter_operands(hs, u_ref, w_ref, qk_ref, qd_ref, kd_ref, g_ref), st)
            for i, h in enumerate(hs):
                o_ref[:, h * LANES:(h + 1) * LANES] = o[i]
            s_ref[grp] = ns

    return pl.pallas_call(
        body, name="delta_inter_fwd", grid=(nc,),
        in_specs=[_chunk_spec(vdim)] * 5 + [_chunk_spec(LANES)],
        out_specs=[_chunk_spec(vdim), pl.BlockSpec((None, n_vh, LANES, LANES), lambda s: (s, 0, 0, 0))],
        out_shape=[jax.ShapeDtypeStruct((t, vdim), F32), jax.ShapeDtypeStruct((nc, n_vh, LANES, LANES), F32)],
        scratch_shapes=[pltpu.VMEM((n_vh, LANES, LANES), F32)],
        compiler_params=_delta_params("arbitrary"),
    )(u, w, qk, q_dec, k_dec, big_g)


def delta_inter_bwd(u, w, qk, q_dec, k_dec, big_g, hist, d_o, n_vh):
    t, vdim = u.shape
    nc = t // DN_CHUNK

    def body(u_ref, w_ref, qk_ref, qd_ref, kd_ref, g_ref, hist_ref, do_ref,
             du_ref, dw_ref, dqk_ref, dqd_ref, dkd_ref, dg_ref, ds_ref):
        @pl.when(pl.program_id(0) == 0)
        def _():
            ds_ref[...] = jnp.zeros_like(ds_ref)
        lane = lax.broadcasted_iota(jnp.int32, (1, LANES), 1)
        dgl_all = jnp.zeros((1, LANES), F32)
        for hs in _head_groups(n_vh):
            grp = slice(hs[0], hs[-1] + 1)
            prim = _inter_operands(hs, u_ref, w_ref, qk_ref, qd_ref, kd_ref, g_ref) + (hist_ref[grp],)
            _, vjp = jax.vjp(_delta_inter, *prim)
            du, dw, dqk, dqd, dkd, dgl, dst = vjp((_stack(hs, lambda h: _head(do_ref, h)), ds_ref[grp]))
            ds_ref[grp] = dst
            for i, h in enumerate(hs):
                sl = slice(h * LANES, (h + 1) * LANES)
                du_ref[:, sl] = du[i]
                dw_ref[:, sl] = dw[i]
                dqk_ref[:, sl] = jnp.concatenate([dqk[i], jnp.zeros_like(dqk[i])], axis=1)
                dqd_ref[:, sl] = dqd[i]
                dkd_ref[:, sl] = dkd[i]
                dgl_all = dgl_all + dgl[i] * (lane == h).astype(F32)
        last = _iota_rows((DN_CHUNK, LANES)) == DN_CHUNK - 1
        dg_ref[...] = jnp.where(last, jnp.broadcast_to(dgl_all, (DN_CHUNK, LANES)), 0.0)

    rv = lambda w_: _chunk_spec(w_, nc)
    return pl.pallas_call(
        body, name="delta_inter_bwd", grid=(nc,),
        in_specs=[rv(vdim)] * 5 + [rv(LANES), pl.BlockSpec((None, n_vh, LANES, LANES), lambda s: (nc - 1 - s, 0, 0, 0)),
                                   rv(vdim)],
        out_specs=[rv(vdim)] * 5 + [rv(LANES)],
        out_shape=[jax.ShapeDtypeStruct((t, vdim), F32)] * 5 + [jax.ShapeDtypeStruct((t, LANES), F32)],
        scratch_shapes=[pltpu.VMEM((n_vh, LANES, LANES), F32)],
        compiler_params=_delta_params("arbitrary"),
    )(u, w, qk, q_dec, k_dec, big_g, hist, d_o)


def delta_intra_bwd(qn, kn, qkv, v_blk, big_g, big_gt, beta, cots, n_vh):
    t, qk_w = qn.shape
    vdim = n_vh * LANES
    rep = vdim // qk_w
    nc = t // DN_CHUNK

    def body(q_ref, k_ref, v_ref, g_ref, gt_ref, b_ref, du_ref, dw_ref, dqk_ref, dqd_ref, dkd_ref,
             dq_ref, dk_ref, dv_ref, dg_ref, dgt_ref, db_ref):
        lane = lax.broadcasted_iota(jnp.int32, (1, LANES), 1)
        dg_all = jnp.zeros((DN_CHUNK, LANES), F32)
        db_all = jnp.zeros((DN_CHUNK, LANES), F32)
        dq_acc, dk_acc = None, None
        for hs in _head_groups(n_vh):
            _, vjp = jax.vjp(_delta_intra, *_intra_operands(hs, rep, q_ref, k_ref, v_ref, g_ref, gt_ref, b_ref))
            cot = lambda ref, width=LANES: _stack(hs, lambda h: _head(ref, h, width))
            dq, dk, dv, dgi, dgj, db = vjp((cot(du_ref), cot(dw_ref), cot(dqk_ref, DN_CHUNK), cot(dqd_ref), cot(dkd_ref)))
            for i, h in enumerate(hs):
                j = h // rep
                dv_ref[:, h * LANES:(h + 1) * LANES] = dv[i]
                dgt_ref[h:h + 1, :] = dgj[i]
                onehot = (lane == h).astype(F32)
                dg_all = dg_all + dgi[i] * onehot
                db_all = db_all + db[i] * onehot
                dq_acc = dq[i] if h % rep == 0 else dq_acc + dq[i]
                dk_acc = dk[i] if h % rep == 0 else dk_acc + dk[i]
                if h % rep == rep - 1:
                    dq_ref[:, j * LANES:(j + 1) * LANES] = dq_acc
                    dk_ref[:, j * LANES:(j + 1) * LANES] = dk_acc
        dg_ref[...] = dg_all
        db_ref[...] = db_all

    gt_spec = pl.BlockSpec((None, n_vh, DN_CHUNK), lambda s: (s, 0, 0))
    return pl.pallas_call(
        body, name="delta_intra_bwd", grid=(nc,),
        in_specs=[_chunk_spec(qk_w), _chunk_spec(qk_w), _chunk_spec(vdim, col=v_blk), _chunk_spec(LANES), gt_spec,
                  _chunk_spec(LANES)] + [_chunk_spec(vdim)] * 5,
        out_specs=[_chunk_spec(qk_w), _chunk_spec(qk_w), _chunk_spec(vdim), _chunk_spec(LANES), gt_spec,
                   _chunk_spec(LANES)],
        out_shape=[jax.ShapeDtypeStruct((t, qk_w), F32), jax.ShapeDtypeStruct((t, qk_w), F32),
                   jax.ShapeDtypeStruct((t, vdim), F32), jax.ShapeDtypeStruct((t, LANES), F32),
                   jax.ShapeDtypeStruct((nc, n_vh, DN_CHUNK), F32), jax.ShapeDtypeStruct((t, LANES), F32)],
        compiler_params=_delta_params("parallel"),
    )(qn, kn, qkv, big_g, big_gt, beta, *cots)


def loss_head(y, target, tb):
    t, d = y.shape

    def body(y_ref, t_ref, dy_ref, loss_ref):
        @pl.when(pl.program_id(0) == 0)
        def _():
            loss_ref[...] = jnp.zeros_like(loss_ref)
        err = y_ref[...] - t_ref[...]
        dy_ref[...] = err * (1.0 / d)
        loss_ref[...] += 0.5 * jnp.sum(jnp.sum(err * err, axis=1, keepdims=True), axis=0, keepdims=True) * (1.0 / d)

    return pl.pallas_call(
        body, name="loss_head", grid=(t // tb,),
        in_specs=[pl.BlockSpec((tb, d), lambda s: (s, 0))] * 2,
        out_specs=[pl.BlockSpec((tb, d), lambda s: (s, 0)), pl.BlockSpec((1, 1), lambda s: (0, 0))],
        out_shape=[jax.ShapeDtypeStruct((t, d), F32), jax.ShapeDtypeStruct((1, 1), F32)],
        compiler_params=pltpu.CompilerParams(dimension_semantics=("arbitrary",), vmem_limit_bytes=VMEM_LIMIT),
    )(y, target)


def all_gather(name, arrs):
    n = len(arrs)

    def body(*refs):
        in_refs, out_refs, sems = refs[:n], refs[n:2 * n], refs[2 * n:]
        _exchange_copies(in_refs, out_refs, sems, False, "start")
        _exchange_copies(in_refs, out_refs, sems, False, "wait")

    res = pl.pallas_call(
        body, name=name,
        in_specs=[_HBM] * n, out_specs=[_HBM] * n,
        out_shape=_exchange_out_shape(arrs, False), scratch_shapes=_exchange_sems(n),
        compiler_params=pltpu.CompilerParams(has_side_effects=True),
    )(*arrs)
    return list(res)


def _adamw_math(w, g, m, v):
    m = ADAM_B1 * m + (1.0 - ADAM_B1) * g
    v = ADAM_B2 * v + (1.0 - ADAM_B2) * (g * g)
    m_hat = m / (1.0 - ADAM_B1 ** ADAM_STEP)
    v_hat = v / (1.0 - ADAM_B2 ** ADAM_STEP)
    delta = -ADAM_LR * (m_hat / (jnp.sqrt(v_hat) + ADAM_EPS) + ADAM_WD * w)
    return delta, m, v


def adamw(name, w, parts, m, v, rows_cap=128):
    r, c = w.shape
    np_ = parts.shape[0]
    tr = _tile(r, rows_cap, SUBLANES * (4 // parts.dtype.itemsize))

    def body(w_ref, p_ref, m_ref, v_ref, g_ref, d_ref, nm_ref, nv_ref):
        g = p_ref[0].astype(F32)
        for k in range(1, np_):
            g = g + p_ref[k].astype(F32)
        delta, nm, nv = _adamw_math(w_ref[...], g, m_ref[...], v_ref[...])
        g_ref[...] = g
        d_ref[...] = delta
        nm_ref[...] = nm
        nv_ref[...] = nv

    spec = pl.BlockSpec((tr, c), lambda i: (i, 0))
    return pl.pallas_call(
        body, name=name, grid=(r // tr,),
        in_specs=[spec, pl.BlockSpec((np_, tr, c), lambda i: (0, i, 0)), spec, spec],
        out_specs=[spec] * 4, out_shape=[jax.ShapeDtypeStruct((r, c), F32)] * 4,
        compiler_params=pltpu.CompilerParams(dimension_semantics=("parallel",), vmem_limit_bytes=VMEM_LIMIT),
    )(w, parts, m, v)


def sum_parts(name, parts, rows_cap=256):
    np_, r, c = parts.shape
    tr = _tile(r, rows_cap, SUBLANES)

    def body(p_ref, o_ref):
        g = p_ref[0].astype(F32)
        for k in range(1, np_):
            g = g + p_ref[k].astype(F32)
        o_ref[...] = g

    return pl.pallas_call(
        body, name=name, grid=(r // tr,),
        in_specs=[pl.BlockSpec((np_, tr, c), lambda i: (0, i, 0))],
        out_specs=pl.BlockSpec((tr, c), lambda i: (i, 0)),
        out_shape=jax.ShapeDtypeStruct((r, c), F32),
        compiler_params=pltpu.CompilerParams(dimension_semantics=("parallel",), vmem_limit_bytes=VMEM_LIMIT),
    )(parts)


def _pack(arrs):
    flat = jnp.concatenate([a.reshape(-1).astype(F32) for a in arrs])
    n = flat.shape[0]
    return jnp.pad(flat, (0, _round_up(n, LANES * SUBLANES) - n)).reshape(-1, LANES)


def _unpack(packed, like):
    flat, out, pos = packed.reshape(-1), [], 0
    for a in like:
        out.append(flat[pos:pos + a.size].reshape(a.shape))
        pos += a.size
    return out


def kernel(x, c, w_ada, b_ada, w_in, rg_conv_w, rg_conv_b, rg_w_a, rg_b_a, rg_w_x, rg_b_x, rg_lambda, dn_conv_w, dn_a_log, dn_dt_bias, dn_norm_w, w_proj_a, w_proj_b, w_out, ln1_g, ln1_b, ffn_w_gate, ffn_w_up, ffn_conv_w, ffn_conv_b, ffn_w_down, ln2_g, ln2_b, loss_target, m_w_ada, m_b_ada, m_w_in, m_rg_conv_w, m_rg_conv_b, m_rg_w_a, m_rg_b_a, m_rg_w_x, m_rg_b_x, m_rg_lambda, m_dn_conv_w, m_dn_a_log, m_dn_dt_bias, m_dn_norm_w, m_w_proj_a, m_w_proj_b, m_w_out, m_ln1_g, m_ln1_b, m_ffn_w_gate, m_ffn_w_up, m_ffn_conv_w, m_ffn_conv_b, m_ffn_w_down, m_ln2_g, m_ln2_b, v_w_ada, v_b_ada, v_w_in, v_rg_conv_w, v_rg_conv_b, v_rg_w_a, v_rg_b_a, v_rg_w_x, v_rg_b_x, v_rg_lambda, v_dn_conv_w, v_dn_a_log, v_dn_dt_bias, v_dn_norm_w, v_w_proj_a, v_w_proj_b, v_w_out, v_ln1_g, v_ln1_b, v_ffn_w_gate, v_ffn_w_up, v_ffn_conv_w, v_ffn_conv_b, v_ffn_w_down, v_ln2_g, v_ln2_b):
    names = ['w_ada', 'b_ada', 'w_in', 'rg_conv_w', 'rg_conv_b', 'rg_w_a', 'rg_b_a', 'rg_w_x', 'rg_b_x', 'rg_lambda',
             'dn_conv_w', 'dn_a_log', 'dn_dt_bias', 'dn_norm_w', 'w_proj_a', 'w_proj_b', 'w_out', 'ln1_g', 'ln1_b',
             'ffn_w_gate', 'ffn_w_up', 'ffn_conv_w', 'ffn_conv_b', 'ffn_w_down', 'ln2_g', 'ln2_b']
    loc = locals()
    W = {n: loc[n][0] for n in names}
    M = {n: loc['m_' + n][0] for n in names}
    V = {n: loc['v_' + n][0] for n in names}

    me = 4 * lax.axis_index("x") + 2 * lax.axis_index("y") + lax.axis_index("c")
    xs, tgt = x[0], loss_target[0]
    t, d = xs.shape
    d_rnn = W['rg_conv_b'].shape[0]
    n_blk = W['rg_w_a'].shape[0]
    n_vh = W['dn_a_log'].shape[0]
    assert W['dn_norm_w'].shape[0] == LANES
    vdim = n_vh * LANES
    d_ff = W['ffn_conv_b'].shape[0]
    d_in = W['w_in'].shape[1] * N_DEV
    qk = (d_in - 2 * d_rnn - 2 * vdim - 2 * n_vh - 2 * d) // 2
    assert vdim == 2 * qk and qk % LANES == 0 and n_vh <= LANES
    splits = (d_rnn, d_rnn, qk, qk, vdim, vdim, n_vh, n_vh, d, d)
    offs = [0]
    for s_ in splits:
        offs.append(offs[-1] + s_)

    tb = _tile(t, 256, SUBLANES)

    big = ['w_in', 'w_proj_a', 'w_proj_b', 'w_out', 'ffn_w_gate', 'ffn_w_up', 'ffn_w_down']
    small_sh = ['rg_conv_w', 'dn_conv_w', 'ffn_conv_w']
    first = all_gather("gather_first", [W['w_in'].astype(WIRE_DTYPE)] + [W[n] for n in small_sh] + [c])
    g_in, g_rcw, g_dcw, g_fcw, c_all = first
    cols = lambda g: jnp.transpose(g, (1, 0, 2)).reshape(g.shape[1], -1)
    rows = lambda g: g.reshape(-1, g.shape[2])
    w_in_f = cols(g_in)
    padl = lambda a: jnp.pad(a, ((0, 0), (0, LANES - a.shape[1])))
    groups = [w_in_f[:, offs[i]:offs[i + 1]] for i in range(10)]
    groups[6], groups[7] = padl(groups[6]), padl(groups[7])
    go = [0]
    for g_ in groups:
        go.append(go[-1] + g_.shape[1])
    n_pad = _round_up(go[-1], 512)
    wp = jnp.pad(jnp.concatenate(groups, axis=1), ((0, 0), (0, n_pad - go[-1])))
    o_xr, o_gr, o_q, o_k, o_v, o_z, o_a, o_b, o_ga, o_gb = go[:10]
    rcw, dcw, fcw = cols(g_rcw), cols(g_dcw), cols(g_fcw)
    eye_b = jnp.eye(n_blk, dtype=F32)
    bd = lambda w: (w[:, :, None, :] * eye_b[:, None, :, None]).reshape(d_rnn, d_rnn)
    w_bd = jnp.concatenate([bd(W['rg_w_a']), bd(W['rg_w_x'])], axis=1)
    row1 = lambda a: a.reshape(1, -1)
    padv = lambda a: jnp.pad(row1(a), ((0, 0), (0, LANES - a.shape[0])))
    nw_t = jnp.tile(row1(W['dn_norm_w']), (1, n_vh))

    c_pad =jnp.pad(c_all.reshape(N_DEV, d), ((0, LANES - N_DEV), (0, 0)))
    ada_w = W['w_ada'].shape[1]
    b_ada_me = lax.dynamic_slice(W['b_ada'], (me * ada_w,), (ada_w,)).reshape(1, ada_w)
    ada_sh = mm(c_pad, W['w_ada'], name="ada_fwd", a_act="silu", bias=b_ada_me)
    (ada_all,) = all_gather("gather_ada", [ada_sh[:N_DEV]])
    ada_me = lax.dynamic_slice(ada_all, (0, me, 0), (N_DEV, 1, ada_w)).reshape(6, 1, d)
    sh1, sc1, gt1, sh2, sc2, gt2 = [ada_me[i] for i in range(6)]

    nt = t // tb

    def act(a, bw, col0=0, width=None, grad=True, rows=tb):
        width = a.shape[1] if width is None else width
        assert col0 % bw == 0 and width % bw == 0
        c0 = col0 // bw
        return In(a, (rows, bw), lambda o, s: (s, c0 + o), grad=grad, gshape=(t, width), gimap=lambda o, s: (s, o))

    def prm(a, bw, parts=None):
        return In(a, (a.shape[0], bw), lambda o, s: (0, o), acc=True, parts=parts)

    def out(width, bw, rows=tb):
        return Out((t, width), (rows, bw), lambda o, s: (s, o))

    tbh = _tile(t, 1024, SUBLANES)
    nth = t // tbh
    tbc = _tile(t, 1024, SUBLANES)
    ntc = t // tbc

    krows = lambda k_: [(slice(j, j + 1), slice(None)) for j in range(k_)]

    mod1_ins = [act(xs, d), prm(sc1, d), prm(sh1, d)]
    (h1,), _, (h1_t,) = stage_fwd("mod1_fwd", f_modulate, (1, nt), mod1_ins, [out(d, d)], transposed=[0])
    proj, g_pa, g_pb, g_out, g_fg, g_fu, g_fd = mm(h1, wp, name="proj_fwd",
                                                   gather=[W[n].astype(WIRE_DTYPE) for n in big[1:]])
    w_pa, w_pb, w_o, w_fd = rows(g_pa), rows(g_pb), rows(g_out), rows(g_fd)
    w_gate, w_up = cols(g_fg), cols(g_fu)
    w_gu = jnp.concatenate([w_gate, w_up], axis=1)

    cb_r = _tile(math.gcd(d_rnn, o_gr), 256)
    rgc_ins = [act(proj, cb_r, o_xr, d_rnn, rows=tbc), prm(rcw, cb_r, krows(4)), prm(row1(W['rg_conv_b']), cb_r)]
    rgc_grid, rgc_car, rgc_outs = (d_rnn // cb_r, ntc), [(SUBLANES, cb_r)], [out(d_rnn, cb_r, tbc)]
    (xc,), rgc_hist, (xc_t,) = stage_fwd("rg_conv_fwd", f_rg_conv, rgc_grid, rgc_ins, rgc_outs, rgc_car, transposed=[0])
    gates = mm(xc, w_bd, name="rg_gates_fwd")
    lru_ins = [act(xc, cb_r), act(gates, cb_r, 0, d_rnn), act(gates, cb_r, d_rnn, d_rnn), act(proj, cb_r, o_gr, d_rnn),
               prm(row1(W['rg_b_a']), cb_r), prm(row1(W['rg_b_x']), cb_r), prm(row1(W['rg_lambda']), cb_r)]
    lru_grid, lru_car = (d_rnn // cb_r, nt), [(1, cb_r)]
    (rec,), lru_hist, (rec_t,) = stage_fwd("rglru_fwd", f_rglru, lru_grid, lru_ins, [out(d_rnn, cb_r)], lru_car,
                                           transposed=[0])
    y_a = mm(rec, w_pa, name="proj_a_fwd")

    cb_q = _tile(math.gcd(math.gcd(qk, o_q), o_k), 256)
    dnc_car, dnc = [(SUBLANES, cb_q)], {}
    for nm, col0, width, w0 in (("q", o_q, qk, 0), ("k", o_k, qk, qk), ("v", o_v, vdim, 2 * qk)):
        ins_ = [act(proj, cb_q, col0, width, rows=tbc), prm(dcw[:, w0:w0 + width], cb_q, krows(4))]
        grid_, outs_ = (width // cb_q, ntc), [out(width, cb_q, tbc)]
        (y_,), hist_ = stage_fwd("dn_conv_%s_fwd" % nm, f_dn_conv, grid_, ins_, outs_, dnc_car)
        dnc[nm] = (y_, ins_, grid_, outs_, hist_)
    q_c, k_c, v_c = dnc["q"][0], dnc["k"][0], dnc["v"][0]
    f_qnorm, f_knorm = functools.partial(f_l2norm, LANES ** -0.5), functools.partial(f_l2norm, 1.0)
    qn_ins, kn_ins = [act(q_c, LANES, rows=tbh)], [act(k_c, LANES, rows=tbh)]
    nrm_grid, nrm_outs = (qk // LANES, nth), [out(qk, LANES, tbh)]
    (qn,), _ = stage_fwd("dn_qnorm_fwd", f_qnorm, nrm_grid, qn_ins, nrm_outs)
    (kn,), _ = stage_fwd("dn_knorm_fwd", f_knorm, nrm_grid, kn_ins, nrm_outs)
    gate_ins = [act(proj, LANES, o_a, LANES), act(proj, LANES, o_b, LANES),
                prm(padv(W['dn_a_log']), LANES), prm(padv(W['dn_dt_bias']), LANES)]
    gate_outs = [out(LANES, LANES), out(LANES, LANES)]
    (g_dn, beta_dn), _ = stage_fwd("dn_gates_fwd", f_dn_gates, (1, nt), gate_ins, gate_outs)
    n_ch = t // DN_CHUNK
    gt_dn = jnp.transpose(g_dn.reshape(n_ch, DN_CHUNK, LANES)[:, :, :n_vh], (0, 2, 1))
    dn_mid = delta_intra_fwd(qn, kn, v_c, 0, g_dn, gt_dn, beta_dn, n_vh)
    o_dn, dn_hist = delta_inter_fwd(*dn_mid, g_dn, n_vh)
    dno_ins = [act(o_dn, LANES, rows=tbh), act(proj, LANES, o_z, vdim, rows=tbh), prm(nw_t, LANES)]
    dno_grid, dno_outs = (n_vh, nth), [out(vdim, LANES, tbh)]
    (dn,), _, (dn_t,) = stage_fwd("dn_out_fwd", f_dn_out, dno_grid, dno_ins, dno_outs, transposed=[0])
    y_b = mm(dn, w_pb, name="proj_b_fwd")

    cb_m = _tile(math.gcd(math.gcd(d, o_ga), o_gb), 512)
    mrg_ins = [act(proj, cb_m, o_ga, d, rows=tbc), act(proj, cb_m, o_gb, d, rows=tbc), act(y_a, cb_m, rows=tbc),
               act(y_b, cb_m, rows=tbc)]
    mrg_grid, mrg_outs = (d // cb_m, ntc), [out(d, cb_m, tbc)]
    (merged,), _, (merged_t,) = stage_fwd("merge_fwd", f_merge, mrg_grid, mrg_ins, mrg_outs, transposed=[0])
    mix = mm(merged, w_o, name="w_out_fwd")
    ln1_ins = [act(xs, d), act(mix, d), prm(gt1, d), prm(row1(W['ln1_g']), d), prm(row1(W['ln1_b']), d)]
    (x1,), _ = stage_fwd("ln1_fwd", f_deepnorm, (1, nt), ln1_ins, [out(d, d)])

    mod2_ins = [act(x1, d), prm(sc2, d), prm(sh2, d)]
    (h2,), _, (h2_t,) = stage_fwd("mod2_fwd", f_modulate, (1, nt), mod2_ins, [out(d, d)], transposed=[0])
    gu = mm(h2, w_gu, name="ffn_in_fwd")
    cb_f = _tile(d_ff, 256)
    ffa_ins = [act(gu, cb_f, 0, d_ff, rows=tbc), act(gu, cb_f, d_ff, d_ff, rows=tbc), prm(fcw, cb_f, krows(3)),
               prm(row1(W['ffn_conv_b']), cb_f)]
    ffa_grid, ffa_car, ffa_outs = (d_ff // cb_f, ntc), [(SUBLANES, cb_f)], [out(d_ff, cb_f, tbc)]
    (act_ff,), ffa_hist, (act_t,) = stage_fwd("ffn_act_fwd", f_ffn_act, ffa_grid, ffa_ins, ffa_outs, ffa_car,
                                              transposed=[0])
    ff = mm(act_ff, w_fd, name="ffn_down_fwd")
    ln2_ins = [act(x1, d), act(ff, d), prm(gt2, d), prm(row1(W['ln2_g']), d), prm(row1(W['ln2_b']), d)]
    (x2,), _ = stage_fwd("ln2_fwd", f_deepnorm, (1, nt), ln2_ins, [out(d, d)])
    dy, loss_loc = loss_head(x2, tgt, tb)

    dx1_a, d_ff_o, d_gt2, d_ln2g, d_ln2b = stage_bwd("ln2_bwd", f_deepnorm, (1, nt), ln2_ins, [out(d, d)], [dy])
    d_act = mm(d_ff_o, w_fd, name="ffn_down_bwd_x", tb=True)
    gw_fd = mm(act_t, d_ff_o, name="ffn_down_bwd_w")
    d_gp, d_up, d_fcw, d_fcb = stage_bwd("ffn_act_bwd", f_ffn_act, ffa_grid, ffa_ins, ffa_outs, [d_act],
                                         ffa_car, ffa_hist, gdtypes={0: MXU_DTYPE, 1: MXU_DTYPE})
    col_blocks = lambda g: jnp.transpose(g.reshape(g.shape[0], N_DEV, -1), (1, 0, 2)).astype(WIRE_DTYPE)
    row_blocks = lambda g: g.reshape(N_DEV, -1, g.shape[1]).astype(WIRE_DTYPE)
    big_parts = {}
    d_h2, big_parts['ffn_w_down'] = mm([d_gp, d_up], [w_gate, w_up], name="ffn_in_bwd_x", tb=True,
                                       scatter=[row_blocks(gw_fd)])
    gw_gate, gw_up = mm(h2_t, d_gp, name="ffn_gate_bwd_w"), mm(h2_t, d_up, name="ffn_up_bwd_w")
    d_x1, d_sc2, d_sh2 = stage_bwd("mod2_bwd", f_modulate, (1, nt), mod2_ins, [out(d, d)], [d_h2], add_to={0: dx1_a})
    dx_a, d_mix, d_gt1, d_ln1g, d_ln1b = stage_bwd("ln1_bwd", f_deepnorm, (1, nt), ln1_ins, [out(d, d)], [d_x1])
    d_merged = mm(d_mix, w_o, name="w_out_bwd_x", tb=True)
    gw_o = mm(merged_t, d_mix, name="w_out_bwd_w")
    d_ga, d_gb, d_ya, d_yb = stage_bwd("merge_bwd", f_merge, mrg_grid, mrg_ins, mrg_outs, [d_merged],
                                       gdtypes={0: MXU_DTYPE, 1: MXU_DTYPE})
    d_rec = mm(d_ya, w_pa, name="proj_a_bwd_x", tb=True)
    gw_pa = mm(rec_t, d_ya, name="proj_a_bwd_w")
    d_dn = mm(d_yb, w_pb, name="proj_b_bwd_x", tb=True)
    gw_pb = mm(dn_t, d_yb, name="proj_b_bwd_w")

    d_o, d_z, d_nwt = stage_bwd("dn_out_bwd", f_dn_out, dno_grid, dno_ins, dno_outs, [d_dn], gdtypes={1: MXU_DTYPE})
    *d_mid, d_g_state = delta_inter_bwd(*dn_mid, g_dn, dn_hist, d_o, n_vh)
    d_qn, d_kn, d_v, d_g_col, d_gt, d_beta = delta_intra_bwd(qn, kn, v_c, 0, g_dn, gt_dn, beta_dn, d_mid, n_vh)
    d_g_row = jnp.pad(jnp.transpose(d_gt, (0, 2, 1)).reshape(t, n_vh), ((0, 0), (0, LANES - n_vh)))
    (d_qc,) = stage_bwd("dn_qnorm_bwd", f_qnorm, nrm_grid, qn_ins, nrm_outs, [d_qn])
    (d_kc,) = stage_bwd("dn_knorm_bwd", f_knorm, nrm_grid, kn_ins, nrm_outs, [d_kn])
    d_a, d_b, d_alog, d_dtb = stage_bwd("dn_gates_bwd", f_dn_gates, (1, nt), gate_ins, gate_outs,
                                        [(d_g_state, d_g_col, d_g_row), d_beta], gdtypes={0: MXU_DTYPE, 1: MXU_DTYPE})
    d_win, d_dcw = {}, []
    for nm, cot in (("q", d_qc), ("k", d_kc), ("v", d_v)):
        _, ins_, grid_, outs_, hist_ = dnc[nm]
        d_win[nm], dw_ = stage_bwd("dn_conv_%s_bwd" % nm, f_dn_conv, grid_, ins_, outs_, [cot], dnc_car, hist_,
                                   gdtypes={0: MXU_DTYPE})
        d_dcw.append(dw_)
    d_dcw = jnp.concatenate(d_dcw, axis=1)

    d_xc_a, d_pr, d_pi, d_gr, d_ba, d_bx, d_lam = stage_bwd(
        "rglru_bwd", f_rglru, lru_grid, lru_ins, [out(d_rnn, cb_r)], [d_rec], lru_car, lru_hist,
        gdtypes={1: MXU_DTYPE, 2: MXU_DTYPE, 3: MXU_DTYPE})
    (d_xc_b, big_parts['ffn_w_gate'], big_parts['ffn_w_up'], big_parts['w_out'], big_parts['w_proj_a'],
     big_parts['w_proj_b']) = mm([d_pr, d_pi], [w_bd[:, :d_rnn], w_bd[:, d_rnn:]], name="rg_gates_bwd_x", tb=True,
                                 scatter=[col_blocks(gw_gate), col_blocks(gw_up), row_blocks(gw_o),
                                          row_blocks(gw_pa), row_blocks(gw_pb)])
    gw_bd_a, gw_bd_x = mm(xc_t, d_pr, name="rg_gate_a_bwd_w"), mm(xc_t, d_pi, name="rg_gate_x_bwd_w")
    d_xr, d_rcw, d_rcb = stage_bwd("rg_conv_bwd", f_rg_conv, rgc_grid, rgc_ins, rgc_outs, [(d_xc_a, d_xc_b)],
                                   rgc_car, rgc_hist, gdtypes={0: MXU_DTYPE})

    diag = lambda g: jnp.einsum('nimj,nm->nij', g.reshape(n_blk, d_rnn // n_blk, n_blk, d_rnn // n_blk), eye_b)
    small_names = ['rg_conv_w', 'rg_conv_b', 'rg_w_a', 'rg_b_a', 'rg_w_x', 'rg_b_x', 'rg_lambda', 'dn_conv_w',
                   'dn_a_log', 'dn_dt_bias', 'dn_norm_w', 'ln1_g', 'ln1_b', 'ffn_conv_w', 'ffn_conv_b', 'ln2_g', 'ln2_b']
    small_loc = {
        'rg_conv_w': d_rcw, 'rg_conv_b': d_rcb,
        'rg_w_a': diag(gw_bd_a), 'rg_b_a': d_ba, 'rg_w_x': diag(gw_bd_x), 'rg_b_x': d_bx,
        'rg_lambda': d_lam, 'dn_conv_w': d_dcw, 'dn_a_log': d_alog[:, :n_vh], 'dn_dt_bias': d_dtb[:, :n_vh],
        'dn_norm_w': jnp.sum(d_nwt.reshape(n_vh, LANES), axis=0), 'ln1_g': d_ln1g, 'ln1_b': d_ln1b,
        'ffn_conv_w': d_fcw, 'ffn_conv_b': d_fcb, 'ln2_g': d_ln2g, 'ln2_b': d_ln2b}
    small_list = [small_loc[n] for n in small_names]

    d_segs = [d_xr, d_gr, d_win["q"], d_win["k"], d_win["v"], d_z, d_a, d_b, d_ga, d_gb]
    gw_segs = [mm(h1_t, dg, name="proj_bwd_w%d" % i) for i, dg in enumerate(d_segs)]
    gw_in = jnp.concatenate([g_[:, :splits[i]] for i, g_ in enumerate(gw_segs)], axis=1)
    half = len(d_segs) // 2
    d_h1_a, small_all = mm(d_segs[:half], groups[:half], name="proj_bwd_x0", tb=True,
                           gather=[_pack(small_list)], **MM_SPLIT_CAPS)
    d_h1_b, big_parts['w_in'] = mm(d_segs[half:], groups[half:], name="proj_bwd_x1", tb=True,
                                   scatter=[col_blocks(gw_in)], **MM_SPLIT_CAPS)
    grad_x, d_sc1, d_sh1 = stage_bwd("mod1_bwd", f_modulate, (1, nt), mod1_ins, [out(d, d)], [(d_h1_a, d_h1_b)],
                                     add_to={0: dx_a})

    g_small = dict(zip(small_names, _unpack(sum_parts("sum_small_grads", small_all), small_list)))
    d_ada_me = jnp.concatenate([d_sh1, d_sc1, d_gt1, d_sh2, d_sc2, d_gt2], axis=1)
    (d_ada_all,) = all_gather("gather_d_ada", [d_ada_me.reshape(-1, LANES)])
    g_small['b_ada'] = sum_parts("sum_d_ada", d_ada_all)
    small_names = ['b_ada'] + small_names
    d_ada_cols = lax.dynamic_slice(d_ada_all.reshape(N_DEV, 6 * d), (0, me * ada_w), (N_DEV, ada_w))
    d_ada_pad = jnp.pad(d_ada_cols, ((0, LANES - N_DEV), (0, 0)))
    gw_ada = mm(c_pad, d_ada_pad, name="ada_bwd_w", ta=True, a_act="silu")

    res = {}
    big_parts['w_ada'] = gw_ada[None]
    for n in ['w_ada'] + big:
        res[n] = adamw("adamw_" + n, W[n], big_parts[n], M[n], V[n])
    for n in small_sh:
        w_ = W[n].shape[1]
        g_small[n] = lax.dynamic_slice(g_small[n], (0, me * w_), (W[n].shape[0], w_))
    for n in small_names:
        g_small[n] = g_small[n].reshape(W[n].shape)
    pk = lambda dct: _pack([dct[n] for n in small_names])
    s_g, s_d, s_m, s_v = adamw("adamw_small", pk(W), pk(g_small)[None], pk(M), pk(V))
    like = [W[n] for n in small_names]
    for n, g_, d_, m_, v_ in zip(small_names, _unpack(s_g, like), _unpack(s_d, like), _unpack(s_m, like), _unpack(s_v, like)):
        res[n] = (g_, d_, m_, v_)

    loss = lax.psum(loss_loc[0, 0], ("x", "y", "c"))
    outs = [loss, grad_x[None]]
    for j in range(4):
        outs += [res[n][j].reshape(loc[n].shape) for n in names]
    return tuple(outs)
```

```python
import functools
import math

import jax
import jax.numpy as jnp
from jax import lax
from jax.experimental import pallas as pl
from jax.experimental.pallas import tpu as pltpu

F32 = jnp.float32
BF16 = jnp.bfloat16
MXU_DTYPE = BF16
WIRE_DTYPE = BF16
DN_DTYPE = BF16
HI = lax.Precision.HIGHEST
MESH = pl.DeviceIdType.MESH

N_DEV = 8
LANES = 128
SUBLANES = 8
VMEM_LIMIT = 56 * 1024 * 1024
MM_TM_CAP, MM_TN_CAP, MM_TK_CAP = 1024, 1536, 2048
MM_SPLIT_CAPS = dict(tm_cap=1024, tn_cap=1024, tk_cap=1024)

RG_C = 8.0
DN_CHUNK = 64
DN_HEAD_GROUP = 16
LN_EPS = 1e-5
RMS_EPS = 1e-6
L2_EPS = 1e-6
DEPTH = 1
DEEPNORM_ALPHA = (2 * DEPTH) ** 0.25
ADAM_LR = 0.001
ADAM_B1 = 0.9
ADAM_B2 = 0.999
ADAM_EPS = 1e-08
ADAM_WD = 0.01
ADAM_STEP = 10


def _tile(n, cap, unit=LANES):
    best = None
    for t in range(unit, min(n, cap) + 1, unit):
        if n % t == 0:
            best = t
    return best if best is not None else n


def _round_up(n, m):
    return (n + m - 1) // m * m


_HBM = pl.BlockSpec(memory_space=pl.ANY)


def _exchange_sems(n):
    return [pltpu.SemaphoreType.DMA((n, N_DEV - 1)), pltpu.SemaphoreType.DMA((n, N_DEV - 1)),
            pltpu.SemaphoreType.DMA((n,))]


def _exchange_out_shape(arrs, scatter):
    return [jax.ShapeDtypeStruct(a.shape if scatter else (N_DEV,) + a.shape, a.dtype) for a in arrs]


def _exchange_copies(in_refs, out_refs, sems, scatter, phase):
    send_sems, recv_sems, local_sems = sems
    x, y, c = lax.axis_index("x"), lax.axis_index("y"), lax.axis_index("c")
    me = 4 * x + 2 * y + c
    peers = [(x ^ ((k >> 2) & 1), y ^ ((k >> 1) & 1), c ^ (k & 1)) for k in range(N_DEV)]
    row = [4 * p[0] + 2 * p[1] + p[2] for p in peers]
    n = len(in_refs)

    def local(i):
        return pltpu.make_async_copy(in_refs[i].at[me] if scatter else in_refs[i], out_refs[i].at[me], local_sems.at[i])

    def remote(i, k, src, dst_row, to):
        return pltpu.make_async_remote_copy(src_ref=src, dst_ref=out_refs[i].at[dst_row],
                                            send_sem=send_sems.at[i, k - 1], recv_sem=recv_sems.at[i, k - 1],
                                            device_id=to, device_id_type=MESH)

    if scatter:
        sends = [(i, k, in_refs[i].at[row[k]], me, peers[k]) for k in range(1, N_DEV) for i in range(n)]
        passed = []
    else:
        sends = [(i, k, in_refs[i], me, peers[k]) for k in (1, 2, 4, 6) for i in range(n)]
        passed = [(i, k + 1, out_refs[i].at[row[k]], row[k], peers[1]) for k in (2, 4, 6) for i in range(n)]
    arrival = lambda i, k: remote(i, k, in_refs[i].at[me] if scatter else in_refs[i], row[k], peers[k])

    if phase == "start":
        for i in range(n):
            local(i).start()
        for cp in sends:
            remote(*cp).start()
    else:
        for cp in passed:
            arrival(cp[0], cp[1] - 1).wait_recv()
            remote(*cp).start()
        waited = {(cp[0], cp[1] - 1) for cp in passed}
        for k in range(1, N_DEV):
            for i in range(n):
                if (i, k) not in waited:
                    arrival(i, k).wait_recv()
        for cp in sends + passed:
            remote(*cp).wait_send()
        for i in range(n):
            local(i).wait()


def mm(a, b, *, name, ta=False, tb=False, a_act=None, bias=None, out_dtype=F32,
       tm_cap=MM_TM_CAP, tn_cap=MM_TN_CAP, tk_cap=MM_TK_CAP, gather=(), scatter=()):
    a_segs = list(a) if isinstance(a, (list, tuple)) else [a]
    b_segs = list(b) if isinstance(b, (list, tuple)) else [b]
    ns = len(a_segs)
    assert ns == len(b_segs) and (ns == 1 or a_act is None)
    m = a_segs[0].shape[1] if ta else a_segs[0].shape[0]
    n = b_segs[0].shape[0] if tb else b_segs[0].shape[1]
    ks = [x.shape[0] if ta else x.shape[1] for x in a_segs]
    assert ks == [y.shape[1] if tb else y.shape[0] for y in b_segs], (ks, ta, tb)
    tm, tn = _tile(m, tm_cap), _tile(n, tn_cap)
    tks = [_tile(k_, tk_cap) for k_ in ks]
    cnt = [k_ // t_ for k_, t_ in zip(ks, tks)]
    lo = [sum(cnt[:s]) for s in range(ns)]
    nk = sum(cnt)
    grid = (m // tm, n // tn, nk)
    dims = (((0 if ta else 1,), (1 if tb else 0,)), ((), ()))
    xch = list(gather) + list(scatter)
    nx, ng = len(xch), len(gather)
    n_main = 2 * ns + (bias is not None)

    def body(*refs):
        a_refs, b_refs = refs[:ns], refs[ns:2 * ns]
        bias_ref = refs[2 * ns] if bias is not None else None
        x_in, o_ref, x_out = refs[n_main:n_main + nx], refs[n_main + nx], refs[n_main + nx + 1:n_main + 2 * nx + 1]
        rest = refs[n_main + 2 * nx + 1:]
        acc_ref = rest[0] if nk > 1 else None
        sems = rest[1 if nk > 1 else 0:]
        groups = []
        if ng:
            groups.append((x_in[:ng], x_out[:ng], sems[:3], False))
        if nx > ng:
            groups.append((x_in[ng:], x_out[ng:], sems[-3:], True))
        if nx:
            step = (pl.program_id(0) * grid[1] + pl.program_id(1)) * grid[2] + pl.program_id(2)

            @pl.when(step == 0)
            def _():
                for gi, go_, gs, sc in groups:
                    _exchange_copies(gi, go_, gs, sc, "start")
        kk = pl.program_id(2)

        def finish(r):
            if bias is not None:
                r = r + bias_ref[...]
            o_ref[...] = r.astype(o_ref.dtype)

        def segment(s):
            av = a_refs[s][...]
            if a_act == "silu":
                av = jax.nn.silu(av.astype(F32))
            prod = lax.dot_general(av.astype(MXU_DTYPE), b_refs[s][...].astype(MXU_DTYPE), dims,
                                   preferred_element_type=F32)
            if nk == 1:
                finish(prod)
                return
            opens, closes = lo[s] == 0, lo[s] + cnt[s] == nk
            if opens:
                @pl.when(kk == 0)
                def _():
                    acc_ref[...] = prod
            inner = [kk > 0] * opens + [kk < nk - 1] * closes
            if inner:
                @pl.when(functools.reduce(lambda p, q: p & q, inner))
                def _():
                    acc_ref[...] += prod
            else:
                acc_ref[...] += prod
            if closes:
                @pl.when(kk == nk - 1)
                def _():
                    finish(acc_ref[...] + prod)

        for s in range(ns):
            if ns == 1:
                segment(s)
            else:
                pl.when((kk >= lo[s]) & (kk < lo[s] + cnt[s]))(functools.partial(segment, s))

        if nx:
            @pl.when(step == grid[0] * grid[1] * grid[2] - 1)
            def _():
                for gi, go_, gs, sc in groups:
                    _exchange_copies(gi, go_, gs, sc, "wait")

    def seg_index(s):
        return lambda q: jnp.clip(q - lo[s], 0, cnt[s] - 1) if ns > 1 else q

    a_specs, b_specs = [], []
    for s in range(ns):
        qi, tk = seg_index(s), tks[s]
        a_specs.append(pl.BlockSpec((tk, tm), (lambda qi: lambda i, j, q: (qi(q), i))(qi)) if ta
                       else pl.BlockSpec((tm, tk), (lambda qi: lambda i, j, q: (i, qi(q)))(qi)))
        b_specs.append(pl.BlockSpec((tn, tk), (lambda qi: lambda i, j, q: (j, qi(q)))(qi)) if tb
                       else pl.BlockSpec((tk, tn), (lambda qi: lambda i, j, q: (qi(q), j))(qi)))
    in_specs, args = a_specs + b_specs, a_segs + b_segs
    if bias is not None:
        in_specs.append(pl.BlockSpec((1, tn), lambda i, j, q: (0, j)))
        args.append(bias)
    o_spec, o_shape = pl.BlockSpec((tm, tn), lambda i, j, q: (i, j)), jax.ShapeDtypeStruct((m, n), out_dtype)
    acc = [pltpu.VMEM((tm, tn), F32)] if nk > 1 else []
    if not nx:
        return pl.pallas_call(
            body, name=name, grid=grid, in_specs=in_specs, out_specs=o_spec, out_shape=o_shape, scratch_shapes=acc,
            compiler_params=pltpu.CompilerParams(dimension_semantics=("parallel", "parallel", "arbitrary"),
                                                 vmem_limit_bytes=VMEM_LIMIT),
        )(*args)
    return pl.pallas_call(
        body, name=name, grid=grid, in_specs=in_specs + [_HBM] * nx, out_specs=[o_spec] + [_HBM] * nx,
        out_shape=[o_shape] + _exchange_out_shape(list(gather), False) + _exchange_out_shape(list(scatter), True),
        scratch_shapes=acc + (_exchange_sems(ng) if ng else []) + (_exchange_sems(nx - ng) if nx > ng else []),
        compiler_params=pltpu.CompilerParams(dimension_semantics=("arbitrary", "arbitrary", "arbitrary"),
                                             vmem_limit_bytes=VMEM_LIMIT, has_side_effects=True),
    )(*args, *xch)


class In:
    def __init__(self, arr, block, imap, acc=False, grad=True, parts=None, gshape=None, gimap=None):
        self.arr, self.block, self.imap, self.acc, self.grad, self.parts = arr, block, imap, acc, grad, parts
        self.gshape = arr.shape if gshape is None else gshape
        self.gimap = imap if gimap is None else gimap


class Out:
    def __init__(self, shape, block, imap, dtype=F32):
        self.shape, self.block, self.imap, self.dtype = shape, block, imap, dtype


def _load(in_refs, ins):
    vals = []
    for r, i in zip(in_refs, ins):
        if i.parts is None:
            vals.append(r[...])
        else:
            vals.extend(r[p] for p in i.parts)
    return vals


def _stage_params():
    return pltpu.CompilerParams(dimension_semantics=("parallel", "arbitrary"), vmem_limit_bytes=VMEM_LIMIT)


def stage_fwd(name, f, grid, ins, outs, carries=(), transposed=()):
    n_in, n_out, n_c, n_t = len(ins), len(outs), len(carries), len(transposed)

    def body(*refs):
        in_refs, out_refs = refs[:n_in], refs[n_in:n_in + n_out]
        hist_refs = refs[n_in + n_out:n_in + n_out + n_c]
        t_refs = refs[n_in + n_out + n_c:n_in + n_out + n_c + n_t]
        c_refs = refs[n_in + n_out + n_c + n_t:]
        if n_c:
            @pl.when(pl.program_id(1) == 0)
            def _():
                for c in c_refs:
                    c[...] = jnp.zeros_like(c)
        cin = [c[...] for c in c_refs]
        for h, c in zip(hist_refs, cin):
            h[...] = c
        o, cout = f(*_load(in_refs, ins), *cin)
        for r, v in zip(out_refs, o):
            r[...] = v.astype(r.dtype)
        for r, k in zip(t_refs, transposed):
            r[...] = o[k].T.astype(r.dtype)
        for c, v in zip(c_refs, cout):
            c[...] = v

    hist_spec = lambda c: pl.BlockSpec((None, None) + tuple(c), lambda o, s: (o, s) + (0,) * len(c))
    flip = lambda o_: pl.BlockSpec(o_.block[::-1], (lambda im: lambda o, s: im(o, s)[::-1])(o_.imap))
    res = pl.pallas_call(
        body, name=name, grid=grid,
        in_specs=[pl.BlockSpec(i.block, i.imap) for i in ins],
        out_specs=[pl.BlockSpec(o.block, o.imap) for o in outs] + [hist_spec(c) for c in carries]
        + [flip(outs[k]) for k in transposed],
        out_shape=[jax.ShapeDtypeStruct(o.shape, o.dtype) for o in outs]
        + [jax.ShapeDtypeStruct(tuple(grid) + tuple(c), F32) for c in carries]
        + [jax.ShapeDtypeStruct(outs[k].shape[::-1], MXU_DTYPE) for k in transposed],
        scratch_shapes=[pltpu.VMEM(tuple(c), F32) for c in carries],
        compiler_params=_stage_params(),
    )(*[i.arr for i in ins])
    res = list(res)
    if transposed:
        return res[:n_out], res[n_out:n_out + n_c], res[n_out + n_c:]
    return res[:n_out], res[n_out:]


def stage_bwd(name, f, grid, ins, outs, cots, carries=(), hists=(), add_to=None, gdtypes=None):
    n_in, n_out, n_c = len(ins), len(outs), len(carries)
    ns = grid[1]
    add_to = add_to or {}
    gdtypes = gdtypes or {}
    add_idx = sorted(add_to)
    g_idx = [k for k, i in enumerate(ins) if i.grad]
    cots = [c if isinstance(c, (tuple, list)) else (c,) for c in cots]
    n_cot = [len(c) for c in cots]
    rev = lambda imap: (lambda o, s: imap(o, ns - 1 - s))

    def body(*refs):
        p = 0
        in_refs = refs[p:p + n_in]; p += n_in
        cot_refs = []
        for cnt in n_cot:
            cot_refs.append(refs[p:p + cnt]); p += cnt
        hist_refs = refs[p:p + n_c]; p += n_c
        add_refs = refs[p:p + len(add_idx)]; p += len(add_idx)
        g_refs = refs[p:p + len(g_idx)]; p += len(g_idx)
        dc_refs = refs[p:]
        first = pl.program_id(1) == 0
        if n_c:
            @pl.when(first)
            def _():
                for c in dc_refs:
                    c[...] = jnp.zeros_like(c)
        vals = _load(in_refs, ins)
        cin = [h[...] for h in hist_refs]
        (o, cout), vjp = jax.vjp(lambda *a: f(*a), *vals, *cin)
        cot_o = []
        for crs, v in zip(cot_refs, o):
            c = crs[0][...].astype(v.dtype)
            for extra in crs[1:]:
                c = c + extra[...].astype(v.dtype)
            cot_o.append(c)
        cot_c = tuple(c[...] for c in dc_refs)
        grads = vjp((tuple(cot_o), cot_c))
        pos, per_in = 0, []
        for i in ins:
            cnt = 1 if i.parts is None else len(i.parts)
            per_in.append(grads[pos:pos + cnt])
            pos += cnt
        dcin = grads[pos:]
        for gr, k in zip(g_refs, g_idx):
            i, gs = ins[k], per_in[k]
            if i.acc:
                @pl.when(first)
                def _(gr=gr):
                    gr[...] = jnp.zeros_like(gr)
                if i.parts is None:
                    gr[...] += gs[0].astype(gr.dtype)
                else:
                    for pt, g in zip(i.parts, gs):
                        gr[pt] += g.astype(gr.dtype)
            else:
                g = gs[0]
                if k in add_to:
                    g = g + add_refs[add_idx.index(k)][...].astype(g.dtype)
                gr[...] = g.astype(gr.dtype)
        for c, v in zip(dc_refs, dcin):
            c[...] = v

    in_specs = [pl.BlockSpec(i.block, rev(i.imap)) for i in ins]
    for o_, cnt in zip(outs, n_cot):
        in_specs += [pl.BlockSpec(o_.block, rev(o_.imap))] * cnt
    in_specs += [pl.BlockSpec((None, None) + tuple(c), (lambda c: (lambda o, s: (o, ns - 1 - s) + (0,) * len(c)))(c))
                 for c in carries]
    in_specs += [pl.BlockSpec(ins[k].block, rev(ins[k].gimap)) for k in add_idx]
    out_specs, out_shape = [], []
    for k in g_idx:
        i = ins[k]
        if i.acc:
            out_specs.append(pl.BlockSpec(i.block, (lambda im: (lambda o, s: im(o, 0)))(i.imap)))
        else:
            out_specs.append(pl.BlockSpec(i.block, rev(i.gimap)))
        out_shape.append(jax.ShapeDtypeStruct(i.gshape, gdtypes.get(k, F32)))
    res = pl.pallas_call(
        body, name=name, grid=grid, in_specs=in_specs, out_specs=out_specs, out_shape=out_shape,
        scratch_shapes=[pltpu.VMEM(tuple(c), F32) for c in carries],
        compiler_params=_stage_params(),
    )(*[i.arr for i in ins], *[a for c in cots for a in c], *hists, *[add_to[k] for k in add_idx])
    return list(res)


def _iota_rows(shape):
    return lax.broadcasted_iota(jnp.int32, shape, 0)


@functools.partial(jax.custom_vjp, nondiff_argnums=(1,))
def _roll_rows(x, s):
    return pltpu.roll(x, s % x.shape[0], 0)


def _roll_rows_fwd(x, s):
    return _roll_rows(x, s), None


def _roll_rows_bwd(s, _, g):
    return (_roll_rows(g, -s),)


_roll_rows.defvjp(_roll_rows_fwd, _roll_rows_bwd)


@jax.custom_vjp
def _drop_head(xx):
    return xx[SUBLANES:]


def _drop_head_fwd(xx):
    return xx[SUBLANES:], None


def _drop_head_bwd(_, g):
    return (jnp.concatenate([jnp.zeros((SUBLANES, g.shape[1]), g.dtype), g], axis=0),)


_drop_head.defvjp(_drop_head_fwd, _drop_head_bwd)


@jax.custom_vjp
def _last_rows(x):
    return x[x.shape[0] - SUBLANES:]


def _last_rows_fwd(x):
    return x[x.shape[0] - SUBLANES:], x.shape[0]


def _last_rows_bwd(n, g):
    return (jnp.concatenate([jnp.zeros((n - SUBLANES, g.shape[1]), g.dtype), g], axis=0),)


_last_rows.defvjp(_last_rows_fwd, _last_rows_bwd)


def _last_row(x):
    n = x.shape[0]
    return jnp.sum(jnp.where(_iota_rows(x.shape) == n - 1, x, 0.0), axis=0, keepdims=True)


def _scan_steps(n):
    s = 1
    while s < n:
        yield s
        s *= 2


def _block_scan_impl(a, u, h0):
    n = a.shape[0]
    row = _iota_rows(a.shape)
    for s in _scan_steps(n):
        keep = row >= s
        a_s = jnp.where(keep, pltpu.roll(a, s, 0), 1.0)
        u_s = jnp.where(keep, pltpu.roll(u, s, 0), 0.0)
        u = u + a * u_s
        a = a * a_s
    return u + a * h0


@jax.custom_vjp
def _block_scan(a, u, h0):
    return _block_scan_impl(a, u, h0)


def _block_scan_fwd(a, u, h0):
    h = _block_scan_impl(a, u, h0)
    return h, (a, h, h0)


def _block_scan_bwd(res, dh):
    a, h, h0 = res
    n = a.shape[0]
    row = _iota_rows(a.shape)
    b = jnp.where(row < n - 1, pltpu.roll(a, n - 1, 0), 0.0)
    lam = dh
    for s in _scan_steps(n):
        keep = row < n - s
        b_s = jnp.where(keep, pltpu.roll(b, n - s, 0), 1.0)
        l_s = jnp.where(keep, pltpu.roll(lam, n - s, 0), 0.0)
        lam = lam + b * l_s
        b = b * b_s
    h_prev = jnp.where(row >= 1, pltpu.roll(h, 1, 0), jnp.broadcast_to(h0, h.shape))
    d_h0 = jnp.sum(jnp.where(row == 0, a * lam, 0.0), axis=0, keepdims=True)
    return lam * h_prev, lam, d_h0


_block_scan.defvjp(_block_scan_fwd, _block_scan_bwd)


def _dot_hi(a, b, dims=(((1,), (0,)), ((), ()))):
    return lax.dot_general(a, b, dims, precision=HI, preferred_element_type=F32)


_NN, _NT, _TN = "nn", "nt", "tn"
_CONTRACT = {_NN: (1, 0), _NT: (1, 1), _TN: (0, 0)}


def _raw_dot(a, b, kind):
    ca, cb = _CONTRACT[kind]
    lead = a.ndim - 2
    dims = (((ca + lead,), (cb + lead,)), (tuple(range(lead)), tuple(range(lead))))
    return lax.dot_general(a.astype(DN_DTYPE), b.astype(DN_DTYPE), dims, preferred_element_type=F32)


@jax.custom_vjp
def _nn(a, b):
    return _raw_dot(a, b, _NN)


_nn.defvjp(lambda a, b: (_raw_dot(a, b, _NN), (a, b)),
           lambda r, g: (_raw_dot(g, r[1], _NT), _raw_dot(r[0], g, _TN)))


@jax.custom_vjp
def _nt(a, b):
    return _raw_dot(a, b, _NT)


_nt.defvjp(lambda a, b: (_raw_dot(a, b, _NT), (a, b)),
           lambda r, g: (_raw_dot(g, r[1], _NN), _raw_dot(g, r[0], _TN)))


@jax.custom_vjp
def _tn(a, b):
    return _raw_dot(a, b, _TN)


_tn.defvjp(lambda a, b: (_raw_dot(a, b, _TN), (a, b)),
           lambda r, g: (_raw_dot(r[1], g, _NT), _raw_dot(r[0], g, _NN)))


def _neumann_inverse(a):
    n = a.shape[-1]
    eye = (lax.broadcasted_iota(jnp.int32, (n, n), 0) == lax.broadcasted_iota(jnp.int32, (n, n), 1)).astype(F32)
    p = _raw_dot(a, a, _NN)
    e = p
    for _ in range(int(math.log2(n)) - 2):
        p = _raw_dot(p, p, _NN)
        e = e + p + _raw_dot(e, p, _NN)
    return eye - a + e - _raw_dot(a, e, _NN)


@jax.custom_vjp
def _unit_lower_inverse(a):
    return _neumann_inverse(a)


def _unit_lower_inverse_fwd(a):
    x = _neumann_inverse(a)
    return x, x


def _unit_lower_inverse_bwd(x, g):
    return (-_raw_dot(_raw_dot(x, g, _TN), x, _NT),)


_unit_lower_inverse.defvjp(_unit_lower_inverse_fwd, _unit_lower_inverse_bwd)


def _softplus(x):
    return jnp.maximum(x, 0.0) + jnp.log1p(jnp.exp(-jnp.abs(x)))


def _neg_expm1(x):
    series = -x * (1.0 + x * (0.5 + x * (1.0 / 6.0 + x * (1.0 / 24.0 + x * (1.0 / 120.0)))))
    return jnp.where(x > -0.03, series, 1.0 - jnp.exp(x))


def f_modulate(x, sc, sh):
    return (x * (1.0 + sc) + sh,), ()


def f_deepnorm(x, y, gt, g, b):
    v = DEEPNORM_ALPHA * x + (1.0 + gt) * y
    mu = jnp.mean(v, axis=-1, keepdims=True)
    vc = v - mu
    var = jnp.mean(vc * vc, axis=-1, keepdims=True)
    return (vc * lax.rsqrt(var + LN_EPS) * g + b,), ()


def _causal_conv(x, prev, ws):
    xx = jnp.concatenate([prev, x], axis=0)
    k = len(ws)
    y = ws[k - 1] * x
    for j in range(k - 1):
        y = y + ws[j] * _drop_head(_roll_rows(xx, k - 1 - j))
    return y


def f_rg_conv(x, w0, w1, w2, w3, b, prev):
    return (_causal_conv(x, prev, (w0, w1, w2, w3)) + b,), (_last_rows(x),)


def f_dn_conv(x, w0, w1, w2, w3, prev):
    return (jax.nn.silu(_causal_conv(x, prev, (w0, w1, w2, w3))),), (_last_rows(x),)


def f_ffn_act(gp, up, w0, w1, w2, b, prev):
    return (jax.nn.gelu(_causal_conv(gp, prev, (w0, w1, w2)) + b) * up,), (_last_rows(gp),)


def f_rglru(xc, pre_r, pre_i, gr, b_a, b_x, lam, h0):
    gate_r = jax.nn.sigmoid(pre_r + b_a)
    gate_i = jax.nn.sigmoid(pre_i + b_x)
    log_a = -RG_C * gate_r * _softplus(-lam)
    a = jnp.exp(log_a)
    mult = jnp.sqrt(_neg_expm1(2.0 * log_a))
    h = _block_scan(a, mult * gate_i * xc, h0)
    return (h * jax.nn.gelu(gr),), (_last_row(h),)


def f_dn_conv_norm(scale, x, w0, w1, w2, w3, prev):
    y = jax.nn.silu(_causal_conv(x, prev, (w0, w1, w2, w3)))
    return (y * lax.rsqrt(jnp.sum(y * y, axis=-1, keepdims=True) + L2_EPS) * scale,), (_last_rows(x),)


def f_dn_gates(a_in, b_in, a_log, dt_bias):
    g = -jnp.exp(a_log) * _softplus(a_in + dt_bias)
    n = g.shape[0]
    shift = int(math.log2(DN_CHUNK))
    ri = lax.broadcasted_iota(jnp.int32, (n, n), 0)
    ci = lax.broadcasted_iota(jnp.int32, (n, n), 1)
    tri = ((lax.shift_right_logical(ri, shift) == lax.shift_right_logical(ci, shift)) & (ri >= ci)).astype(F32)
    return (_dot_hi(tri, g), jax.nn.sigmoid(b_in)), ()


def f_dn_out(o, z, nw):
    r = lax.rsqrt(jnp.mean(o * o, axis=-1, keepdims=True) + RMS_EPS)
    return (o * r * nw * jax.nn.silu(z),), ()


def f_merge(ga, gb, ya, yb):
    return (jax.nn.sigmoid(ga) * ya + jax.nn.sigmoid(gb) * yb,), ()


def _delta_intra(q, k, v, g_i, g_j, beta):
    c = q.shape[-2]
    ri = lax.broadcasted_iota(jnp.int32, (c, c), 0)
    ci = lax.broadcasted_iota(jnp.int32, (c, c), 1)
    decay = jnp.exp(jnp.where(ri >= ci, g_i - g_j, -jnp.inf))
    g_last = jnp.sum(jnp.where(_iota_rows((c, 1)) == c - 1, g_i, 0.0), axis=-2, keepdims=True)
    exp_g = jnp.exp(g_i)
    kb = k * beta
    t_inv = _unit_lower_inverse(jnp.where(ri > ci, _nt(kb, k) * decay, 0.0))
    u = _nn(t_inv, v * beta)
    w = _nn(t_inv, kb * exp_g)
    return u, w, _nt(q, k) * decay, q * exp_g, k * jnp.exp(g_last - g_i)


def _delta_inter(u, w, qk, q_dec, k_dec, g_last, state):
    v_new = u - _nn(w, state)
    o = _nn(q_dec, state) + _nn(qk, v_new)
    return o, jnp.exp(g_last) * state + _tn(k_dec, v_new)


def _chunk_spec(width, nc=None, col=0):
    if nc is None:
        return pl.BlockSpec((DN_CHUNK, width), lambda s: (s, col))
    return pl.BlockSpec((DN_CHUNK, width), lambda s: (nc - 1 - s, col))


def _delta_params(sem):
    return pltpu.CompilerParams(dimension_semantics=(sem,), vmem_limit_bytes=VMEM_LIMIT)


def _head(ref, h, width=LANES):
    return ref[:, h * LANES:h * LANES + width]


def _head_groups(n_vh):
    hb = min(DN_HEAD_GROUP, n_vh)
    return [range(h0, h0 + hb) for h0 in range(0, n_vh, hb)]


def _stack(hs, f):
    return jnp.stack([f(h) for h in hs])


def _intra_operands(hs, rep, q_ref, k_ref, v_ref, g_ref, gt_ref, b_ref):
    return (_stack(hs, lambda h: _head(q_ref, h // rep)), _stack(hs, lambda h: _head(k_ref, h // rep)),
            _stack(hs, lambda h: _head(v_ref, h)), _stack(hs, lambda h: g_ref[:, h:h + 1]),
            _stack(hs, lambda h: gt_ref[h:h + 1, :]), _stack(hs, lambda h: b_ref[:, h:h + 1]))


def _inter_operands(hs, u_ref, w_ref, qk_ref, qd_ref, kd_ref, g_ref):
    f32 = lambda ref, width=LANES: _stack(hs, lambda h: _head(ref, h, width).astype(F32))
    return (f32(u_ref), f32(w_ref), f32(qk_ref, DN_CHUNK), f32(qd_ref), f32(kd_ref),
            _stack(hs, lambda h: g_ref[DN_CHUNK - 1:DN_CHUNK, h:h + 1]))


def delta_intra_fwd(qn, kn, qkv, v_blk, big_g, big_gt, beta, n_vh):
    t, qk_w = qn.shape
    vdim = n_vh * LANES
    rep = vdim // qk_w
    nc = t // DN_CHUNK

    def body(q_ref, k_ref, v_ref, g_ref, gt_ref, b_ref, u_ref, w_ref, qk_ref, qd_ref, kd_ref):
        for hs in _head_groups(n_vh):
            u, w, qk, qd, kd = _delta_intra(*_intra_operands(hs, rep, q_ref, k_ref, v_ref, g_ref, gt_ref, b_ref))
            for i, h in enumerate(hs):
                sl = slice(h * LANES, (h + 1) * LANES)
                u_ref[:, sl] = u[i]
                w_ref[:, sl] = w[i].astype(w_ref.dtype)
                qk_ref[:, sl] = jnp.concatenate([qk[i], jnp.zeros_like(qk[i])], axis=1).astype(qk_ref.dtype)
                qd_ref[:, sl] = qd[i].astype(qd_ref.dtype)
                kd_ref[:, sl] = kd[i].astype(kd_ref.dtype)

    return pl.pallas_call(
        body, name="delta_intra_fwd", grid=(nc,),
        in_specs=[_chunk_spec(qk_w), _chunk_spec(qk_w), _chunk_spec(vdim, col=v_blk), _chunk_spec(LANES),
                  pl.BlockSpec((None, n_vh, DN_CHUNK), lambda s: (s, 0, 0)), _chunk_spec(LANES)],
        out_specs=[_chunk_spec(vdim)] * 5,
        out_shape=[jax.ShapeDtypeStruct((t, vdim), F32)] + [jax.ShapeDtypeStruct((t, vdim), DN_DTYPE)] * 4,
        compiler_params=_delta_params("parallel"),
    )(qn, kn, qkv, big_g, big_gt, beta)


def delta_inter_fwd(u, w, qk, q_dec, k_dec, big_g, n_vh):
    t, vdim = u.shape
    nc = t // DN_CHUNK

    def body(u_ref, w_ref, qk_ref, qd_ref, kd_ref, g_ref, o_ref, hist_ref, s_ref):
        @pl.when(pl.program_id(0) == 0)
        def _():
            s_ref[...] = jnp.zeros_like(s_ref)
        for hs in _head_groups(n_vh):
            grp = slice(hs[0], hs[-1] + 1)
            st = s_ref[grp]
            hist_ref[grp] = st
            o, ns = _delta_inter(*_inter_operands(hs, u_ref, w_ref, qk_ref, qd_ref, kd_ref, g_ref), st)
            for i, h in enumerate(hs):
                o_ref[:, h * LANES:(h + 1) * LANES] = o[i]
            s_ref[grp] = ns

    return pl.pallas_call(
        body, name="delta_inter_fwd", grid=(nc,),
        in_specs=[_chunk_spec(vdim)] * 5 + [_chunk_spec(LANES)],
        out_specs=[_chunk_spec(vdim), pl.BlockSpec((None, n_vh, LANES, LANES), lambda s: (s, 0, 0, 0))],
        out_shape=[jax.ShapeDtypeStruct((t, vdim), F32), jax.ShapeDtypeStruct((nc, n_vh, LANES, LANES), F32)],
        scratch_shapes=[pltpu.VMEM((n_vh, LANES, LANES), F32)],
        compiler_params=_delta_params("arbitrary"),
    )(u, w, qk, q_dec, k_dec, big_g)


def delta_inter_bwd(u, w, qk, q_dec, k_dec, big_g, hist, d_o, n_vh):
    t, vdim = u.shape
    nc = t // DN_CHUNK

    def body(u_ref, w_ref, qk_ref, qd_ref, kd_ref, g_ref, hist_ref, do_ref,
             du_ref, dw_ref, dqk_ref, dqd_ref, dkd_ref, dg_ref, ds_ref):
        @pl.when(pl.program_id(0) == 0)
        def _():
            ds_ref[...] = jnp.zeros_like(ds_ref)
        lane = lax.broadcasted_iota(jnp.int32, (1, LANES), 1)
        dgl_all = jnp.zeros((1, LANES), F32)
        for hs in _head_groups(n_vh):
            grp = slice(hs[0], hs[-1] + 1)
            prim = _inter_operands(hs, u_ref, w_ref, qk_ref, qd_ref, kd_ref, g_ref) + (hist_ref[grp],)
            _, vjp = jax.vjp(_delta_inter, *prim)
            du, dw, dqk, dqd, dkd, dgl, dst = vjp((_stack(hs, lambda h: _head(do_ref, h)), ds_ref[grp]))
            ds_ref[grp] = dst
            for i, h in enumerate(hs):
                sl = slice(h * LANES, (h + 1) * LANES)
                du_ref[:, sl] = du[i]
                dw_ref[:, sl] = dw[i]
                dqk_ref[:, sl] = jnp.concatenate([dqk[i], jnp.zeros_like(dqk[i])], axis=1)
                dqd_ref[:, sl] = dqd[i]
                dkd_ref[:, sl] = dkd[i]
                dgl_all = dgl_all + dgl[i] * (lane == h).astype(F32)
        last = _iota_rows((DN_CHUNK, LANES)) == DN_CHUNK - 1
        dg_ref[...] = jnp.where(last, jnp.broadcast_to(dgl_all, (DN_CHUNK, LANES)), 0.0)

    rv = lambda w_: _chunk_spec(w_, nc)
    return pl.pallas_call(
        body, name="delta_inter_bwd", grid=(nc,),
        in_specs=[rv(vdim)] * 5 + [rv(LANES), pl.BlockSpec((None, n_vh, LANES, LANES), lambda s: (nc - 1 - s, 0, 0, 0)),
                                   rv(vdim)],
        out_specs=[rv(vdim)] * 5 + [rv(LANES)],
        out_shape=[jax.ShapeDtypeStruct((t, vdim), F32)] * 5 + [jax.ShapeDtypeStruct((t, LANES), F32)],
        scratch_shapes=[pltpu.VMEM((n_vh, LANES, LANES), F32)],
        compiler_params=_delta_params("arbitrary"),
    )(u, w, qk, q_dec, k_dec, big_g, hist, d_o)


def delta_intra_bwd(qn, kn, qkv, v_blk, big_g, big_gt, beta, cots, n_vh):
    t, qk_w = qn.shape
    vdim = n_vh * LANES
    rep = vdim // qk_w
    nc = t // DN_CHUNK

    def body(q_ref, k_ref, v_ref, g_ref, gt_ref, b_ref, du_ref, dw_ref, dqk_ref, dqd_ref, dkd_ref,
             dq_ref, dk_ref, dv_ref, dg_ref, dgt_ref, db_ref):
        lane = lax.broadcasted_iota(jnp.int32, (1, LANES), 1)
        dg_all = jnp.zeros((DN_CHUNK, LANES), F32)
        db_all = jnp.zeros((DN_CHUNK, LANES), F32)
        dq_acc, dk_acc = None, None
        for hs in _head_groups(n_vh):
            _, vjp = jax.vjp(_delta_intra, *_intra_operands(hs, rep, q_ref, k_ref, v_ref, g_ref, gt_ref, b_ref))
            cot = lambda ref, width=LANES: _stack(hs, lambda h: _head(ref, h, width))
            dq, dk, dv, dgi, dgj, db = vjp((cot(du_ref), cot(dw_ref), cot(dqk_ref, DN_CHUNK), cot(dqd_ref), cot(dkd_ref)))
            for i, h in enumerate(hs):
                j = h // rep
                dv_ref[:, h * LANES:(h + 1) * LANES] = dv[i]
                dgt_ref[h:h + 1, :] = dgj[i]
                onehot = (lane == h).astype(F32)
                dg_all = dg_all + dgi[i] * onehot
                db_all = db_all + db[i] * onehot
                dq_acc = dq[i] if h % rep == 0 else dq_acc + dq[i]
                dk_acc = dk[i] if h % rep == 0 else dk_acc + dk[i]
                if h % rep == rep - 1:
                    dq_ref[:, j * LANES:(j + 1) * LANES] = dq_acc
                    dk_ref[:, j * LANES:(j + 1) * LANES] = dk_acc
        dg_ref[...] = dg_all
        db_ref[...] = db_all

    gt_spec = pl.BlockSpec((None, n_vh, DN_CHUNK), lambda s: (s, 0, 0))
    return pl.pallas_call(
        body, name="delta_intra_bwd", grid=(nc,),
        in_specs=[_chunk_spec(qk_w), _chunk_spec(qk_w), _chunk_spec(vdim, col=v_blk), _chunk_spec(LANES), gt_spec,
                  _chunk_spec(LANES)] + [_chunk_spec(vdim)] * 5,
        out_specs=[_chunk_spec(qk_w), _chunk_spec(qk_w), _chunk_spec(vdim), _chunk_spec(LANES), gt_spec,
                   _chunk_spec(LANES)],
        out_shape=[jax.ShapeDtypeStruct((t, qk_w), F32), jax.ShapeDtypeStruct((t, qk_w), F32),
                   jax.ShapeDtypeStruct((t, vdim), F32), jax.ShapeDtypeStruct((t, LANES), F32),
                   jax.ShapeDtypeStruct((nc, n_vh, DN_CHUNK), F32), jax.ShapeDtypeStruct((t, LANES), F32)],
        compiler_params=_delta_params("parallel"),
    )(qn, kn, qkv, big_g, big_gt, beta, *cots)


def loss_head(y, target, tb):
    t, d = y.shape

    def body(y_ref, t_ref, dy_ref, loss_ref):
        @pl.when(pl.program_id(0) == 0)
        def _():
            loss_ref[...] = jnp.zeros_like(loss_ref)
        err = y_ref[...] - t_ref[...]
        dy_ref[...] = err * (1.0 / d)
        loss_ref[...] += 0.5 * jnp.sum(jnp.sum(err * err, axis=1, keepdims=True), axis=0, keepdims=True) * (1.0 / d)

    return pl.pallas_call(
        body, name="loss_head", grid=(t // tb,),
        in_specs=[pl.BlockSpec((tb, d), lambda s: (s, 0))] * 2,
        out_specs=[pl.BlockSpec((tb, d), lambda s: (s, 0)), pl.BlockSpec((1, 1), lambda s: (0, 0))],
        out_shape=[jax.ShapeDtypeStruct((t, d), F32), jax.ShapeDtypeStruct((1, 1), F32)],
        compiler_params=pltpu.CompilerParams(dimension_semantics=("arbitrary",), vmem_limit_bytes=VMEM_LIMIT),
    )(y, target)


def all_gather(name, arrs):
    n = len(arrs)

    def body(*refs):
        in_refs, out_refs, sems = refs[:n], refs[n:2 * n], refs[2 * n:]
        _exchange_copies(in_refs, out_refs, sems, False, "start")
        _exchange_copies(in_refs, out_refs, sems, False, "wait")

    res = pl.pallas_call(
        body, name=name,
        in_specs=[_HBM] * n, out_specs=[_HBM] * n,
        out_shape=_exchange_out_shape(arrs, False), scratch_shapes=_exchange_sems(n),
        compiler_params=pltpu.CompilerParams(has_side_effects=True),
    )(*arrs)
    return list(res)


def _adamw_math(w, g, m, v):
    m = ADAM_B1 * m + (1.0 - ADAM_B1) * g
    v = ADAM_B2 * v + (1.0 - ADAM_B2) * (g * g)
    m_hat = m / (1.0 - ADAM_B1 ** ADAM_STEP)
    v_hat = v / (1.0 - ADAM_B2 ** ADAM_STEP)
    delta = -ADAM_LR * (m_hat / (jnp.sqrt(v_hat) + ADAM_EPS) + ADAM_WD * w)
    return delta, m, v


def adamw(name, w, parts, m, v, rows_cap=128):
    r, c = w.shape
    np_ = parts.shape[0]
    tr = _tile(r, rows_cap, SUBLANES * (4 // parts.dtype.itemsize))

    def body(w_ref, p_ref, m_ref, v_ref, g_ref, d_ref, nm_ref, nv_ref):
        g = p_ref[0].astype(F32)
        for k in range(1, np_):
            g = g + p_ref[k].astype(F32)
        delta, nm, nv = _adamw_math(w_ref[...], g, m_ref[...], v_ref[...])
        g_ref[...] = g
        d_ref[...] = delta
        nm_ref[...] = nm
        nv_ref[...] = nv

    spec = pl.BlockSpec((tr, c), lambda i: (i, 0))
    return pl.pallas_call(
        body, name=name, grid=(r // tr,),
        in_specs=[spec, pl.BlockSpec((np_, tr, c), lambda i: (0, i, 0)), spec, spec],
        out_specs=[spec] * 4, out_shape=[jax.ShapeDtypeStruct((r, c), F32)] * 4,
        compiler_params=pltpu.CompilerParams(dimension_semantics=("parallel",), vmem_limit_bytes=VMEM_LIMIT),
    )(w, parts, m, v)


def sum_parts(name, parts, rows_cap=256):
    np_, r, c = parts.shape
    tr = _tile(r, rows_cap, SUBLANES)

    def body(p_ref, o_ref):
        g = p_ref[0].astype(F32)
        for k in range(1, np_):
            g = g + p_ref[k].astype(F32)
        o_ref[...] = g

    return pl.pallas_call(
        body, name=name, grid=(r // tr,),
        in_specs=[pl.BlockSpec((np_, tr, c), lambda i: (0, i, 0))],
        out_specs=pl.BlockSpec((tr, c), lambda i: (i, 0)),
        out_shape=jax.ShapeDtypeStruct((r, c), F32),
        compiler_params=pltpu.CompilerParams(dimension_semantics=("parallel",), vmem_limit_bytes=VMEM_LIMIT),
    )(parts)


def _pack(arrs):
    flat = jnp.concatenate([a.reshape(-1).astype(F32) for a in arrs])
    n = flat.shape[0]
    return jnp.pad(flat, (0, _round_up(n, LANES * SUBLANES) - n)).reshape(-1, LANES)


def _unpack(packed, like):
    flat, out, pos = packed.reshape(-1), [], 0
    for a in like:
        out.append(flat[pos:pos + a.size].reshape(a.shape))
        pos += a.size
    return out


def kernel(x, c, w_ada, b_ada, w_in, rg_conv_w, rg_conv_b, rg_w_a, rg_b_a, rg_w_x, rg_b_x, rg_lambda, dn_conv_w, dn_a_log, dn_dt_bias, dn_norm_w, w_proj_a, w_proj_b, w_out, ln1_g, ln1_b, ffn_w_gate, ffn_w_up, ffn_conv_w, ffn_conv_b, ffn_w_down, ln2_g, ln2_b, loss_target, m_w_ada, m_b_ada, m_w_in, m_rg_conv_w, m_rg_conv_b, m_rg_w_a, m_rg_b_a, m_rg_w_x, m_rg_b_x, m_rg_lambda, m_dn_conv_w, m_dn_a_log, m_dn_dt_bias, m_dn_norm_w, m_w_proj_a, m_w_proj_b, m_w_out, m_ln1_g, m_ln1_b, m_ffn_w_gate, m_ffn_w_up, m_ffn_conv_w, m_ffn_conv_b, m_ffn_w_down, m_ln2_g, m_ln2_b, v_w_ada, v_b_ada, v_w_in, v_rg_conv_w, v_rg_conv_b, v_rg_w_a, v_rg_b_a, v_rg_w_x, v_rg_b_x, v_rg_lambda, v_dn_conv_w, v_dn_a_log, v_dn_dt_bias, v_dn_norm_w, v_w_proj_a, v_w_proj_b, v_w_out, v_ln1_g, v_ln1_b, v_ffn_w_gate, v_ffn_w_up, v_ffn_conv_w, v_ffn_conv_b, v_ffn_w_down, v_ln2_g, v_ln2_b):
    names = ['w_ada', 'b_ada', 'w_in', 'rg_conv_w', 'rg_conv_b', 'rg_w_a', 'rg_b_a', 'rg_w_x', 'rg_b_x', 'rg_lambda',
             'dn_conv_w', 'dn_a_log', 'dn_dt_bias', 'dn_norm_w', 'w_proj_a', 'w_proj_b', 'w_out', 'ln1_g', 'ln1_b',
             'ffn_w_gate', 'ffn_w_up', 'ffn_conv_w', 'ffn_conv_b', 'ffn_w_down', 'ln2_g', 'ln2_b']
    loc = locals()
    W = {n: loc[n][0] for n in names}
    M = {n: loc['m_' + n][0] for n in names}
    V = {n: loc['v_' + n][0] for n in names}

    me = 4 * lax.axis_index("x") + 2 * lax.axis_index("y") + lax.axis_index("c")
    xs, tgt = x[0], loss_target[0]
    t, d = xs.shape
    d_rnn = W['rg_conv_b'].shape[0]
    n_blk = W['rg_w_a'].shape[0]
    n_vh = W['dn_a_log'].shape[0]
    assert W['dn_norm_w'].shape[0] == LANES
    vdim = n_vh * LANES
    d_ff = W['ffn_conv_b'].shape[0]
    d_in = W['w_in'].shape[1] * N_DEV
    qk = (d_in - 2 * d_rnn - 2 * vdim - 2 * n_vh - 2 * d) // 2
    assert vdim == 2 * qk and qk % LANES == 0 and n_vh <= LANES
    splits = (d_rnn, d_rnn, qk, qk, vdim, vdim, n_vh, n_vh, d, d)
    offs = [0]
    for s_ in splits:
        offs.append(offs[-1] + s_)

    tb = _tile(t, 256, SUBLANES)

    big = ['w_in', 'w_proj_a', 'w_proj_b', 'w_out', 'ffn_w_gate', 'ffn_w_up', 'ffn_w_down']
    small_sh = ['rg_conv_w', 'dn_conv_w', 'ffn_conv_w']
    first = all_gather("gather_first", [W['w_in'].astype(WIRE_DTYPE)] + [W[n] for n in small_sh] + [c])
    g_in, g_rcw, g_dcw, g_fcw, c_all = first
    cols = lambda g: jnp.transpose(g, (1, 0, 2)).reshape(g.shape[1], -1)
    rows = lambda g: g.reshape(-1, g.shape[2])
    w_in_f = cols(g_in)
    padl = lambda a: jnp.pad(a, ((0, 0), (0, LANES - a.shape[1])))
    groups = [w_in_f[:, offs[i]:offs[i + 1]] for i in range(10)]
    groups[6], groups[7] = padl(groups[6]), padl(groups[7])
    go = [0]
    for g_ in groups:
        go.append(go[-1] + g_.shape[1])
    n_pad = _round_up(go[-1], 512)
    wp = jnp.pad(jnp.concatenate(groups, axis=1), ((0, 0), (0, n_pad - go[-1])))
    o_xr, o_gr, o_q, o_k, o_v, o_z, o_a, o_b, o_ga, o_gb = go[:10]
    rcw, dcw, fcw = cols(g_rcw), cols(g_dcw), cols(g_fcw)
    eye_b = jnp.eye(n_blk, dtype=F32)
    bd = lambda w: (w[:, :, None, :] * eye_b[:, None, :, None]).reshape(d_rnn, d_rnn)
    w_bd = jnp.concatenate([bd(W['rg_w_a']), bd(W['rg_w_x'])], axis=1)
    row1 = lambda a: a.reshape(1, -1)
    padv = lambda a: jnp.pad(row1(a), ((0, 0), (0, LANES - a.shape[0])))
    nw_t = jnp.tile(row1(W['dn_norm_w']), (1, n_vh))

    c_pad =jnp.pad(c_all.reshape(N_DEV, d), ((0, LANES - N_DEV), (0, 0)))
    ada_w = W['w_ada'].shape[1]
    b_ada_me = lax.dynamic_slice(W['b_ada'], (me * ada_w,), (ada_w,)).reshape(1, ada_w)
    ada_sh = mm(c_pad, W['w_ada'], name="ada_fwd", a_act="silu", bias=b_ada_me)
    (ada_all,) = all_gather("gather_ada", [ada_sh[:N_DEV]])
    ada_me = lax.dynamic_slice(ada_all, (0, me, 0), (N_DEV, 1, ada_w)).reshape(6, 1, d)
    sh1, sc1, gt1, sh2, sc2, gt2 = [ada_me[i] for i in range(6)]

    nt = t // tb

    def act(a, bw, col0=0, width=None, grad=True, rows=tb):
        width = a.shape[1] if width is None else width
        assert col0 % bw == 0 and width % bw == 0
        c0 = col0 // bw
        return In(a, (rows, bw), lambda o, s: (s, c0 + o), grad=grad, gshape=(t, width), gimap=lambda o, s: (s, o))

    def prm(a, bw, parts=None):
        return In(a, (a.shape[0], bw), lambda o, s: (0, o), acc=True, parts=parts)

    def out(width, bw, rows=tb):
        return Out((t, width), (rows, bw), lambda o, s: (s, o))

    tbh = _tile(t, 1024, SUBLANES)
    nth = t // tbh
    tbc = _tile(t, 1024, SUBLANES)
    ntc = t // tbc

    krows = lambda k_: [(slice(j, j + 1), slice(None)) for j in range(k_)]

    mod1_ins = [act(xs, d), prm(sc1, d), prm(sh1, d)]
    (h1,), _, (h1_t,) = stage_fwd("mod1_fwd", f_modulate, (1, nt), mod1_ins, [out(d, d)], transposed=[0])
    proj, g_pa, g_pb, g_out, g_fg, g_fu, g_fd = mm(h1, wp, name="proj_fwd",
                                                   gather=[W[n].astype(WIRE_DTYPE) for n in big[1:]])
    w_pa, w_pb, w_o, w_fd = rows(g_pa), rows(g_pb), rows(g_out), rows(g_fd)
    w_gate, w_up = cols(g_fg), cols(g_fu)
    w_gu = jnp.concatenate([w_gate, w_up], axis=1)

    cb_r = _tile(math.gcd(d_rnn, o_gr), 256)
    rgc_ins = [act(proj, cb_r, o_xr, d_rnn, rows=tbc), prm(rcw, cb_r, krows(4)), prm(row1(W['rg_conv_b']), cb_r)]
    rgc_grid, rgc_car, rgc_outs = (d_rnn // cb_r, ntc), [(SUBLANES, cb_r)], [out(d_rnn, cb_r, tbc)]
    (xc,), rgc_hist, (xc_t,) = stage_fwd("rg_conv_fwd", f_rg_conv, rgc_grid, rgc_ins, rgc_outs, rgc_car, transposed=[0])
    gates = mm(xc, w_bd, name="rg_gates_fwd")
    lru_ins = [act(xc, cb_r), act(gates, cb_r, 0, d_rnn), act(gates, cb_r, d_rnn, d_rnn), act(proj, cb_r, o_gr, d_rnn),
               prm(row1(W['rg_b_a']), cb_r), prm(row1(W['rg_b_x']), cb_r), prm(row1(W['rg_lambda']), cb_r)]
    lru_grid, lru_car = (d_rnn // cb_r, nt), [(1, cb_r)]
    (rec,), lru_hist, (rec_t,) = stage_fwd("rglru_fwd", f_rglru, lru_grid, lru_ins, [out(d_rnn, cb_r)], lru_car,
                                           transposed=[0])
    y_a = mm(rec, w_pa, name="proj_a_fwd")

    dnc = {}
    for nm, col0, width, w0, cb_, f_ in (("q", o_q, qk, 0, LANES, functools.partial(f_dn_conv_norm, LANES ** -0.5)),
                                         ("k", o_k, qk, qk, LANES, functools.partial(f_dn_conv_norm, 1.0)),
                                         ("v", o_v, vdim, 2 * qk, _tile(math.gcd(vdim, o_v), 256), f_dn_conv)):
        ins_ = [act(proj, cb_, col0, width, rows=tbc), prm(dcw[:, w0:w0 + width], cb_, krows(4))]
        grid_, outs_, car_ = (width // cb_, ntc), [out(width, cb_, tbc)], [(SUBLANES, cb_)]
        (y_,), hist_ = stage_fwd("dn_conv_%s_fwd" % nm, f_, grid_, ins_, outs_, car_)
        dnc[nm] = (y_, f_, ins_, grid_, outs_, car_, hist_)
    qn, kn, v_c = dnc["q"][0], dnc["k"][0], dnc["v"][0]
    gate_ins = [act(proj, LANES, o_a, LANES), act(proj, LANES, o_b, LANES),
                prm(padv(W['dn_a_log']), LANES), prm(padv(W['dn_dt_bias']), LANES)]
    gate_outs = [out(LANES, LANES), out(LANES, LANES)]
    (g_dn, beta_dn), _ = stage_fwd("dn_gates_fwd", f_dn_gates, (1, nt), gate_ins, gate_outs)
    n_ch = t // DN_CHUNK
    gt_dn = jnp.transpose(g_dn.reshape(n_ch, DN_CHUNK, LANES)[:, :, :n_vh], (0, 2, 1))
    dn_mid = delta_intra_fwd(qn, kn, v_c, 0, g_dn, gt_dn, beta_dn, n_vh)
    o_dn, dn_hist = delta_inter_fwd(*dn_mid, g_dn, n_vh)
    dno_ins = [act(o_dn, LANES, rows=tbh), act(proj, LANES, o_z, vdim, rows=tbh), prm(nw_t, LANES)]
    dno_grid, dno_outs = (n_vh, nth), [out(vdim, LANES, tbh)]
    (dn,), _, (dn_t,) = stage_fwd("dn_out_fwd", f_dn_out, dno_grid, dno_ins, dno_outs, transposed=[0])
    y_b = mm(dn, w_pb, name="proj_b_fwd")

    cb_m = _tile(math.gcd(math.gcd(d, o_ga), o_gb), 512)
    mrg_ins = [act(proj, cb_m, o_ga, d, rows=tbc), act(proj, cb_m, o_gb, d, rows=tbc), act(y_a, cb_m, rows=tbc),
               act(y_b, cb_m, rows=tbc)]
    mrg_grid, mrg_outs = (d // cb_m, ntc), [out(d, cb_m, tbc)]
    (merged,), _, (merged_t,) = stage_fwd("merge_fwd", f_merge, mrg_grid, mrg_ins, mrg_outs, transposed=[0])
    mix = mm(merged, w_o, name="w_out_fwd")
    ln1_ins = [act(xs, d), act(mix, d), prm(gt1, d), prm(row1(W['ln1_g']), d), prm(row1(W['ln1_b']), d)]
    (x1,), _ = stage_fwd("ln1_fwd", f_deepnorm, (1, nt), ln1_ins, [out(d, d)])

    mod2_ins = [act(x1, d), prm(sc2, d), prm(sh2, d)]
    (h2,), _, (h2_t,) = stage_fwd("mod2_fwd", f_modulate, (1, nt), mod2_ins, [out(d, d)], transposed=[0])
    gu = mm(h2, w_gu, name="ffn_in_fwd")
    cb_f = _tile(d_ff, 256)
    ffa_ins = [act(gu, cb_f, 0, d_ff, rows=tbc), act(gu, cb_f, d_ff, d_ff, rows=tbc), prm(fcw, cb_f, krows(3)),
               prm(row1(W['ffn_conv_b']), cb_f)]
    ffa_grid, ffa_car, ffa_outs = (d_ff // cb_f, ntc), [(SUBLANES, cb_f)], [out(d_ff, cb_f, tbc)]
    (act_ff,), ffa_hist, (act_t,) = stage_fwd("ffn_act_fwd", f_ffn_act, ffa_grid, ffa_ins, ffa_outs, ffa_car,
                                              transposed=[0])
    ff = mm(act_ff, w_fd, name="ffn_down_fwd")
    ln2_ins = [act(x1, d), act(ff, d), prm(gt2, d), prm(row1(W['ln2_g']), d), prm(row1(W['ln2_b']), d)]
    (x2,), _ = stage_fwd("ln2_fwd", f_deepnorm, (1, nt), ln2_ins, [out(d, d)])
    dy, loss_loc = loss_head(x2, tgt, tb)

    dx1_a, d_ff_o, d_gt2, d_ln2g, d_ln2b = stage_bwd("ln2_bwd", f_deepnorm, (1, nt), ln2_ins, [out(d, d)], [dy])
    d_act = mm(d_ff_o, w_fd, name="ffn_down_bwd_x", tb=True)
    gw_fd = mm(act_t, d_ff_o, name="ffn_down_bwd_w")
    d_gp, d_up, d_fcw, d_fcb = stage_bwd("ffn_act_bwd", f_ffn_act, ffa_grid, ffa_ins, ffa_outs, [d_act],
                                         ffa_car, ffa_hist, gdtypes={0: MXU_DTYPE, 1: MXU_DTYPE})
    col_blocks = lambda g: jnp.transpose(g.reshape(g.shape[0], N_DEV, -1), (1, 0, 2)).astype(WIRE_DTYPE)
    row_blocks = lambda g: g.reshape(N_DEV, -1, g.shape[1]).astype(WIRE_DTYPE)
    big_parts = {}
    d_h2, big_parts['ffn_w_down'] = mm([d_gp, d_up], [w_gate, w_up], name="ffn_in_bwd_x", tb=True,
                                       scatter=[row_blocks(gw_fd)])
    gw_gate, gw_up = mm(h2_t, d_gp, name="ffn_gate_bwd_w"), mm(h2_t, d_up, name="ffn_up_bwd_w")
    d_x1, d_sc2, d_sh2 = stage_bwd("mod2_bwd", f_modulate, (1, nt), mod2_ins, [out(d, d)], [d_h2], add_to={0: dx1_a})
    dx_a, d_mix, d_gt1, d_ln1g, d_ln1b = stage_bwd("ln1_bwd", f_deepnorm, (1, nt), ln1_ins, [out(d, d)], [d_x1])
    d_merged = mm(d_mix, w_o, name="w_out_bwd_x", tb=True)
    gw_o = mm(merged_t, d_mix, name="w_out_bwd_w")
    d_ga, d_gb, d_ya, d_yb = stage_bwd("merge_bwd", f_merge, mrg_grid, mrg_ins, mrg_outs, [d_merged],
                                       gdtypes={0: MXU_DTYPE, 1: MXU_DTYPE})
    d_rec = mm(d_ya, w_pa, name="proj_a_bwd_x", tb=True)
    gw_pa = mm(rec_t, d_ya, name="proj_a_bwd_w")
    d_dn = mm(d_yb, w_pb, name="proj_b_bwd_x", tb=True)
    gw_pb = mm(dn_t, d_yb, name="proj_b_bwd_w")

    d_o, d_z, d_nwt = stage_bwd("dn_out_bwd", f_dn_out, dno_grid, dno_ins, dno_outs, [d_dn], gdtypes={1: MXU_DTYPE})
    *d_mid, d_g_state = delta_inter_bwd(*dn_mid, g_dn, dn_hist, d_o, n_vh)
    d_qn, d_kn, d_v, d_g_col, d_gt, d_beta = delta_intra_bwd(qn, kn, v_c, 0, g_dn, gt_dn, beta_dn, d_mid, n_vh)
    d_g_row = jnp.pad(jnp.transpose(d_gt, (0, 2, 1)).reshape(t, n_vh), ((0, 0), (0, LANES - n_vh)))
    d_a, d_b, d_alog, d_dtb = stage_bwd("dn_gates_bwd", f_dn_gates, (1, nt), gate_ins, gate_outs,
                                        [(d_g_state, d_g_col, d_g_row), d_beta], gdtypes={0: MXU_DTYPE, 1: MXU_DTYPE})
    d_win, d_dcw = {}, []
    for nm, cot in (("q", d_qn), ("k", d_kn), ("v", d_v)):
        _, f_, ins_, grid_, outs_, car_, hist_ = dnc[nm]
        d_win[nm], dw_ = stage_bwd("dn_conv_%s_bwd" % nm, f_, grid_, ins_, outs_, [cot], car_, hist_,
                                   gdtypes={0: MXU_DTYPE})
        d_dcw.append(dw_)
    d_dcw = jnp.concatenate(d_dcw, axis=1)

    d_xc_a, d_pr, d_pi, d_gr, d_ba, d_bx, d_lam = stage_bwd(
        "rglru_bwd", f_rglru, lru_grid, lru_ins, [out(d_rnn, cb_r)], [d_rec], lru_car, lru_hist,
        gdtypes={1: MXU_DTYPE, 2: MXU_DTYPE, 3: MXU_DTYPE})
    (d_xc_b, big_parts['ffn_w_gate'], big_parts['ffn_w_up'], big_parts['w_out'], big_parts['w_proj_a'],
     big_parts['w_proj_b']) = mm([d_pr, d_pi], [w_bd[:, :d_rnn], w_bd[:, d_rnn:]], name="rg_gates_bwd_x", tb=True,
                                 scatter=[col_blocks(gw_gate), col_blocks(gw_up), row_blocks(gw_o),
                                          row_blocks(gw_pa), row_blocks(gw_pb)])
    gw_bd_a, gw_bd_x = mm(xc_t, d_pr, name="rg_gate_a_bwd_w"), mm(xc_t, d_pi, name="rg_gate_x_bwd_w")
    d_xr, d_rcw, d_rcb = stage_bwd("rg_conv_bwd", f_rg_conv, rgc_grid, rgc_ins, rgc_outs, [(d_xc_a, d_xc_b)],
                                   rgc_car, rgc_hist, gdtypes={0: MXU_DTYPE})

    diag = lambda g: jnp.einsum('nimj,nm->nij', g.reshape(n_blk, d_rnn // n_blk, n_blk, d_rnn // n_blk), eye_b)
    small_names = ['rg_conv_w', 'rg_conv_b', 'rg_w_a', 'rg_b_a', 'rg_w_x', 'rg_b_x', 'rg_lambda', 'dn_conv_w',
                   'dn_a_log', 'dn_dt_bias', 'dn_norm_w', 'ln1_g', 'ln1_b', 'ffn_conv_w', 'ffn_conv_b', 'ln2_g', 'ln2_b']
    small_loc = {
        'rg_conv_w': d_rcw, 'rg_conv_b': d_rcb,
        'rg_w_a': diag(gw_bd_a), 'rg_b_a': d_ba, 'rg_w_x': diag(gw_bd_x), 'rg_b_x': d_bx,
        'rg_lambda': d_lam, 'dn_conv_w': d_dcw, 'dn_a_log': d_alog[:, :n_vh], 'dn_dt_bias': d_dtb[:, :n_vh],
        'dn_norm_w': jnp.sum(d_nwt.reshape(n_vh, LANES), axis=0), 'ln1_g': d_ln1g, 'ln1_b': d_ln1b,
        'ffn_conv_w': d_fcw, 'ffn_conv_b': d_fcb, 'ln2_g': d_ln2g, 'ln2_b': d_ln2b}
    small_list = [small_loc[n] for n in small_names]

    d_segs = [d_xr, d_gr, d_win["q"], d_win["k"], d_win["v"], d_z, d_a, d_b, d_ga, d_gb]
    gw_segs = [mm(h1_t, dg, name="proj_bwd_w%d" % i) for i, dg in enumerate(d_segs)]
    gw_in = jnp.concatenate([g_[:, :splits[i]] for i, g_ in enumerate(gw_segs)], axis=1)
    half = len(d_segs) // 2
    d_h1_a, small_all = mm(d_segs[:half], groups[:half], name="proj_bwd_x0", tb=True,
                           gather=[_pack(small_list)], **MM_SPLIT_CAPS)
    d_h1_b, big_parts['w_in'] = mm(d_segs[half:], groups[half:], name="proj_bwd_x1", tb=True,
                                   scatter=[col_blocks(gw_in)], **MM_SPLIT_CAPS)
    grad_x, d_sc1, d_sh1 = stage_bwd("mod1_bwd", f_modulate, (1, nt), mod1_ins, [out(d, d)], [(d_h1_a, d_h1_b)],
                                     add_to={0: dx_a})

    g_small = dict(zip(small_names, _unpack(sum_parts("sum_small_grads", small_all), small_list)))
    d_ada_me = jnp.concatenate([d_sh1, d_sc1, d_gt1, d_sh2, d_sc2, d_gt2], axis=1)
    (d_ada_all,) = all_gather("gather_d_ada", [d_ada_me.reshape(-1, LANES)])
    g_small['b_ada'] = sum_parts("sum_d_ada", d_ada_all)
    small_names = ['b_ada'] + small_names
    d_ada_cols = lax.dynamic_slice(d_ada_all.reshape(N_DEV, 6 * d), (0, me * ada_w), (N_DEV, ada_w))
    d_ada_pad = jnp.pad(d_ada_cols, ((0, LANES - N_DEV), (0, 0)))
    gw_ada = mm(c_pad, d_ada_pad, name="ada_bwd_w", ta=True, a_act="silu")

    res = {}
    big_parts['w_ada'] = gw_ada[None]
    for n in ['w_ada'] + big:
        res[n] = adamw("adamw_" + n, W[n], big_parts[n], M[n], V[n])
    for n in small_sh:
        w_ = W[n].shape[1]
        g_small[n] = lax.dynamic_slice(g_small[n], (0, me * w_), (W[n].shape[0], w_))
    for n in small_names:
        g_small[n] = g_small[n].reshape(W[n].shape)
    pk = lambda dct: _pack([dct[n] for n in small_names])
    s_g, s_d, s_m, s_v = adamw("adamw_small", pk(W), pk(g_small)[None], pk(M), pk(V))
    like = [W[n] for n in small_names]
    for n, g_, d_, m_, v_ in zip(small_names, _unpack(s_g, like), _unpack(s_d, like), _unpack(s_m, like), _unpack(s_v, like)):
        res[n] = (g_, d_, m_, v_)

    loss = lax.psum(loss_loc[0, 0], ("x", "y", "c"))
    outs = [loss, grad_x[None]]
    for j in range(4):
        outs += [res[n][j].reshape(loc[n].shape) for n in names]
    return tuple(outs)
```

```python
import functools
import math

import jax
import jax.numpy as jnp
from jax import lax
from jax.experimental import pallas as pl
from jax.experimental.pallas import tpu as pltpu

F32 = jnp.float32
BF16 = jnp.bfloat16
MXU_DTYPE = BF16
WIRE_DTYPE = BF16
DN_DTYPE = BF16
HI = lax.Precision.HIGHEST
MESH = pl.DeviceIdType.MESH

N_DEV = 8
LANES = 128
SUBLANES = 8
VMEM_LIMIT = 56 * 1024 * 1024
MM_TM_CAP, MM_TN_CAP, MM_TK_CAP = 1024, 1536, 2048
MM_SPLIT_CAPS = dict(tm_cap=1024, tn_cap=1024, tk_cap=1024)

RG_C = 8.0
DN_CHUNK = 64
DN_HEAD_GROUP = 16
LN_EPS = 1e-5
RMS_EPS = 1e-6
L2_EPS = 1e-6
DEPTH = 1
DEEPNORM_ALPHA = (2 * DEPTH) ** 0.25
ADAM_LR = 0.001
ADAM_B1 = 0.9
ADAM_B2 = 0.999
ADAM_EPS = 1e-08
ADAM_WD = 0.01
ADAM_STEP = 10


def _tile(n, cap, unit=LANES):
    best = None
    for t in range(unit, min(n, cap) + 1, unit):
        if n % t == 0:
            best = t
    return best if best is not None else n


def _round_up(n, m):
    return (n + m - 1) // m * m


_HBM = pl.BlockSpec(memory_space=pl.ANY)


def _exchange_sems(n):
    return [pltpu.SemaphoreType.DMA((n, N_DEV - 1)), pltpu.SemaphoreType.DMA((n, N_DEV - 1)),
            pltpu.SemaphoreType.DMA((n,))]


def _exchange_out_shape(arrs, scatter):
    return [jax.ShapeDtypeStruct(a.shape if scatter else (N_DEV,) + a.shape, a.dtype) for a in arrs]


def _exchange_copies(in_refs, out_refs, sems, scatter, phase):
    send_sems, recv_sems, local_sems = sems
    x, y, c = lax.axis_index("x"), lax.axis_index("y"), lax.axis_index("c")
    me = 4 * x + 2 * y + c
    peers = [(x ^ ((k >> 2) & 1), y ^ ((k >> 1) & 1), c ^ (k & 1)) for k in range(N_DEV)]
    row = [4 * p[0] + 2 * p[1] + p[2] for p in peers]
    n = len(in_refs)

    def local(i):
        return pltpu.make_async_copy(in_refs[i].at[me] if scatter else in_refs[i], out_refs[i].at[me], local_sems.at[i])

    def remote(i, k, src, dst_row, to):
        return pltpu.make_async_remote_copy(src_ref=src, dst_ref=out_refs[i].at[dst_row],
                                            send_sem=send_sems.at[i, k - 1], recv_sem=recv_sems.at[i, k - 1],
                                            device_id=to, device_id_type=MESH)

    if scatter:
        sends = [(i, k, in_refs[i].at[row[k]], me, peers[k]) for k in range(1, N_DEV) for i in range(n)]
        passed = []
    else:
        sends = [(i, k, in_refs[i], me, peers[k]) for k in (1, 2, 4, 6) for i in range(n)]
        passed = [(i, k + 1, out_refs[i].at[row[k]], row[k], peers[1]) for k in (2, 4, 6) for i in range(n)]
    arrival = lambda i, k: remote(i, k, in_refs[i].at[me] if scatter else in_refs[i], row[k], peers[k])

    if phase == "start":
        for i in range(n):
            local(i).start()
        for cp in sends:
            remote(*cp).start()
    else:
        for cp in passed:
            arrival(cp[0], cp[1] - 1).wait_recv()
            remote(*cp).start()
        waited = {(cp[0], cp[1] - 1) for cp in passed}
        for k in range(1, N_DEV):
            for i in range(n):
                if (i, k) not in waited:
                    arrival(i, k).wait_recv()
        for cp in sends + passed:
            remote(*cp).wait_send()
        for i in range(n):
            local(i).wait()


def mm(a, b, *, name, ta=False, tb=False, a_act=None, bias=None, out_dtype=F32,
       tm_cap=MM_TM_CAP, tn_cap=MM_TN_CAP, tk_cap=MM_TK_CAP, gather=(), scatter=()):
    a_segs = list(a) if isinstance(a, (list, tuple)) else [a]
    b_segs = list(b) if isinstance(b, (list, tuple)) else [b]
    ns = len(a_segs)
    assert ns == len(b_segs) and (ns == 1 or a_act is None)
    m = a_segs[0].shape[1] if ta else a_segs[0].shape[0]
    n = b_segs[0].shape[0] if tb else b_segs[0].shape[1]
    ks = [x.shape[0] if ta else x.shape[1] for x in a_segs]
    assert ks == [y.shape[1] if tb else y.shape[0] for y in b_segs], (ks, ta, tb)
    tm, tn = _tile(m, tm_cap), _tile(n, tn_cap)
    tks = [_tile(k_, tk_cap) for k_ in ks]
    cnt = [k_ // t_ for k_, t_ in zip(ks, tks)]
    lo = [sum(cnt[:s]) for s in range(ns)]
    nk = sum(cnt)
    grid = (m // tm, n // tn, nk)
    dims = (((0 if ta else 1,), (1 if tb else 0,)), ((), ()))
    xch = list(gather) + list(scatter)
    nx, ng = len(xch), len(gather)
    n_main = 2 * ns + (bias is not None)

    def body(*refs):
        a_refs, b_refs = refs[:ns], refs[ns:2 * ns]
        bias_ref = refs[2 * ns] if bias is not None else None
        x_in, o_ref, x_out = refs[n_main:n_main + nx], refs[n_main + nx], refs[n_main + nx + 1:n_main + 2 * nx + 1]
        rest = refs[n_main + 2 * nx + 1:]
        acc_ref = rest[0] if nk > 1 else None
        sems = rest[1 if nk > 1 else 0:]
        groups = []
        if ng:
            groups.append((x_in[:ng], x_out[:ng], sems[:3], False))
        if nx > ng:
            groups.append((x_in[ng:], x_out[ng:], sems[-3:], True))
        if nx:
            step = (pl.program_id(0) * grid[1] + pl.program_id(1)) * grid[2] + pl.program_id(2)

            @pl.when(step == 0)
            def _():
                for gi, go_, gs, sc in groups:
                    _exchange_copies(gi, go_, gs, sc, "start")
        kk = pl.program_id(2)

        def finish(r):
            if bias is not None:
                r = r + bias_ref[...]
            o_ref[...] = r.astype(o_ref.dtype)

        def segment(s):
            av = a_refs[s][...]
            if a_act == "silu":
                av = jax.nn.silu(av.astype(F32))
            prod = lax.dot_general(av.astype(MXU_DTYPE), b_refs[s][...].astype(MXU_DTYPE), dims,
                                   preferred_element_type=F32)
            if nk == 1:
                finish(prod)
                return
            opens, closes = lo[s] == 0, lo[s] + cnt[s] == nk
            if opens:
                @pl.when(kk == 0)
                def _():
                    acc_ref[...] = prod
            inner = [kk > 0] * opens + [kk < nk - 1] * closes
            if inner:
                @pl.when(functools.reduce(lambda p, q: p & q, inner))
                def _():
                    acc_ref[...] += prod
            else:
                acc_ref[...] += prod
            if closes:
                @pl.when(kk == nk - 1)
                def _():
                    finish(acc_ref[...] + prod)

        for s in range(ns):
            if ns == 1:
                segment(s)
            else:
                pl.when((kk >= lo[s]) & (kk < lo[s] + cnt[s]))(functools.partial(segment, s))

        if nx:
            @pl.when(step == grid[0] * grid[1] * grid[2] - 1)
            def _():
                for gi, go_, gs, sc in groups:
                    _exchange_copies(gi, go_, gs, sc, "wait")

    def seg_index(s):
        return lambda q: jnp.clip(q - lo[s], 0, cnt[s] - 1) if ns > 1 else q

    a_specs, b_specs = [], []
    for s in range(ns):
        qi, tk = seg_index(s), tks[s]
        a_specs.append(pl.BlockSpec((tk, tm), (lambda qi: lambda i, j, q: (qi(q), i))(qi)) if ta
                       else pl.BlockSpec((tm, tk), (lambda qi: lambda i, j, q: (i, qi(q)))(qi)))
        b_specs.append(pl.BlockSpec((tn, tk), (lambda qi: lambda i, j, q: (j, qi(q)))(qi)) if tb
                       else pl.BlockSpec((tk, tn), (lambda qi: lambda i, j, q: (qi(q), j))(qi)))
    in_specs, args = a_specs + b_specs, a_segs + b_segs
    if bias is not None:
        in_specs.append(pl.BlockSpec((1, tn), lambda i, j, q: (0, j)))
        args.append(bias)
    o_spec, o_shape = pl.BlockSpec((tm, tn), lambda i, j, q: (i, j)), jax.ShapeDtypeStruct((m, n), out_dtype)
    acc = [pltpu.VMEM((tm, tn), F32)] if nk > 1 else []
    if not nx:
        return pl.pallas_call(
            body, name=name, grid=grid, in_specs=in_specs, out_specs=o_spec, out_shape=o_shape, scratch_shapes=acc,
            compiler_params=pltpu.CompilerParams(dimension_semantics=("parallel", "parallel", "arbitrary"),
                                                 vmem_limit_bytes=VMEM_LIMIT),
        )(*args)
    return pl.pallas_call(
        body, name=name, grid=grid, in_specs=in_specs + [_HBM] * nx, out_specs=[o_spec] + [_HBM] * nx,
        out_shape=[o_shape] + _exchange_out_shape(list(gather), False) + _exchange_out_shape(list(scatter), True),
        scratch_shapes=acc + (_exchange_sems(ng) if ng else []) + (_exchange_sems(nx - ng) if nx > ng else []),
        compiler_params=pltpu.CompilerParams(dimension_semantics=("arbitrary", "arbitrary", "arbitrary"),
                                             vmem_limit_bytes=VMEM_LIMIT, has_side_effects=True),
    )(*args, *xch)


class In:
    def __init__(self, arr, block, imap, acc=False, grad=True, parts=None, gshape=None, gimap=None):
        self.arr, self.block, self.imap, self.acc, self.grad, self.parts = arr, block, imap, acc, grad, parts
        self.gshape = arr.shape if gshape is None else gshape
        self.gimap = imap if gimap is None else gimap


class Out:
    def __init__(self, shape, block, imap, dtype=F32):
        self.shape, self.block, self.imap, self.dtype = shape, block, imap, dtype


def _load(in_refs, ins):
    vals = []
    for r, i in zip(in_refs, ins):
        if i.parts is None:
            vals.append(r[...])
        else:
            vals.extend(r[p] for p in i.parts)
    return vals


def _stage_params():
    return pltpu.CompilerParams(dimension_semantics=("parallel", "arbitrary"), vmem_limit_bytes=VMEM_LIMIT)


def stage_fwd(name, f, grid, ins, outs, carries=(), transposed=()):
    n_in, n_out, n_c, n_t = len(ins), len(outs), len(carries), len(transposed)

    def body(*refs):
        in_refs, out_refs = refs[:n_in], refs[n_in:n_in + n_out]
        hist_refs = refs[n_in + n_out:n_in + n_out + n_c]
        t_refs = refs[n_in + n_out + n_c:n_in + n_out + n_c + n_t]
        c_refs = refs[n_in + n_out + n_c + n_t:]
        if n_c:
            @pl.when(pl.program_id(1) == 0)
            def _():
                for c in c_refs:
                    c[...] = jnp.zeros_like(c)
        cin = [c[...] for c in c_refs]
        for h, c in zip(hist_refs, cin):
            h[...] = c
        o, cout = f(*_load(in_refs, ins), *cin)
        for r, v in zip(out_refs, o):
            r[...] = v.astype(r.dtype)
        for r, k in zip(t_refs, transposed):
            r[...] = o[k].T.astype(r.dtype)
        for c, v in zip(c_refs, cout):
            c[...] = v

    hist_spec = lambda c: pl.BlockSpec((None, None) + tuple(c), lambda o, s: (o, s) + (0,) * len(c))
    flip = lambda o_: pl.BlockSpec(o_.block[::-1], (lambda im: lambda o, s: im(o, s)[::-1])(o_.imap))
    res = pl.pallas_call(
        body, name=name, grid=grid,
        in_specs=[pl.BlockSpec(i.block, i.imap) for i in ins],
        out_specs=[pl.BlockSpec(o.block, o.imap) for o in outs] + [hist_spec(c) for c in carries]
        + [flip(outs[k]) for k in transposed],
        out_shape=[jax.ShapeDtypeStruct(o.shape, o.dtype) for o in outs]
        + [jax.ShapeDtypeStruct(tuple(grid) + tuple(c), F32) for c in carries]
        + [jax.ShapeDtypeStruct(outs[k].shape[::-1], MXU_DTYPE) for k in transposed],
        scratch_shapes=[pltpu.VMEM(tuple(c), F32) for c in carries],
        compiler_params=_stage_params(),
    )(*[i.arr for i in ins])
    res = list(res)
    if transposed:
        return res[:n_out], res[n_out:n_out + n_c], res[n_out + n_c:]
    return res[:n_out], res[n_out:]


def stage_bwd(name, f, grid, ins, outs, cots, carries=(), hists=(), add_to=None, gdtypes=None):
    n_in, n_out, n_c = len(ins), len(outs), len(carries)
    ns = grid[1]
    add_to = add_to or {}
    gdtypes = gdtypes or {}
    add_idx = sorted(add_to)
    g_idx = [k for k, i in enumerate(ins) if i.grad]
    cots = [c if isinstance(c, (tuple, list)) else (c,) for c in cots]
    n_cot = [len(c) for c in cots]
    rev = lambda imap: (lambda o, s: imap(o, ns - 1 - s))

    def body(*refs):
        p = 0
        in_refs = refs[p:p + n_in]; p += n_in
        cot_refs = []
        for cnt in n_cot:
            cot_refs.append(refs[p:p + cnt]); p += cnt
        hist_refs = refs[p:p + n_c]; p += n_c
        add_refs = refs[p:p + len(add_idx)]; p += len(add_idx)
        g_refs = refs[p:p + len(g_idx)]; p += len(g_idx)
        dc_refs = refs[p:]
        first = pl.program_id(1) == 0
        if n_c:
            @pl.when(first)
            def _():
                for c in dc_refs:
                    c[...] = jnp.zeros_like(c)
        vals = _load(in_refs, ins)
        cin = [h[...] for h in hist_refs]
        (o, cout), vjp = jax.vjp(lambda *a: f(*a), *vals, *cin)
        cot_o = []
        for crs, v in zip(cot_refs, o):
            c = crs[0][...].astype(v.dtype)
            for extra in crs[1:]:
                c = c + extra[...].astype(v.dtype)
            cot_o.append(c)
        cot_c = tuple(c[...] for c in dc_refs)
        grads = vjp((tuple(cot_o), cot_c))
        pos, per_in = 0, []
        for i in ins:
            cnt = 1 if i.parts is None else len(i.parts)
            per_in.append(grads[pos:pos + cnt])
            pos += cnt
        dcin = grads[pos:]
        for gr, k in zip(g_refs, g_idx):
            i, gs = ins[k], per_in[k]
            if i.acc:
                @pl.when(first)
                def _(gr=gr):
                    gr[...] = jnp.zeros_like(gr)
                if i.parts is None:
                    gr[...] += gs[0].astype(gr.dtype)
                else:
                    for pt, g in zip(i.parts, gs):
                        gr[pt] += g.astype(gr.dtype)
            else:
                g = gs[0]
                if k in add_to:
                    g = g + add_refs[add_idx.index(k)][...].astype(g.dtype)
                gr[...] = g.astype(gr.dtype)
        for c, v in zip(dc_refs, dcin):
            c[...] = v

    in_specs = [pl.BlockSpec(i.block, rev(i.imap)) for i in ins]
    for o_, cnt in zip(outs, n_cot):
        in_specs += [pl.BlockSpec(o_.block, rev(o_.imap))] * cnt
    in_specs += [pl.BlockSpec((None, None) + tuple(c), (lambda c: (lambda o, s: (o, ns - 1 - s) + (0,) * len(c)))(c))
                 for c in carries]
    in_specs += [pl.BlockSpec(ins[k].block, rev(ins[k].gimap)) for k in add_idx]
    out_specs, out_shape = [], []
    for k in g_idx:
        i = ins[k]
        if i.acc:
            out_specs.append(pl.BlockSpec(i.block, (lambda im: (lambda o, s: im(o, 0)))(i.imap)))
        else:
            out_specs.append(pl.BlockSpec(i.block, rev(i.gimap)))
        out_shape.append(jax.ShapeDtypeStruct(i.gshape, gdtypes.get(k, F32)))
    res = pl.pallas_call(
        body, name=name, grid=grid, in_specs=in_specs, out_specs=out_specs, out_shape=out_shape,
        scratch_shapes=[pltpu.VMEM(tuple(c), F32) for c in carries],
        compiler_params=_stage_params(),
    )(*[i.arr for i in ins], *[a for c in cots for a in c], *hists, *[add_to[k] for k in add_idx])
    return list(res)


def _iota_rows(shape):
    return lax.broadcasted_iota(jnp.int32, shape, 0)


@functools.partial(jax.custom_vjp, nondiff_argnums=(1,))
def _roll_rows(x, s):
    return pltpu.roll(x, s % x.shape[0], 0)


def _roll_rows_fwd(x, s):
    return _roll_rows(x, s), None


def _roll_rows_bwd(s, _, g):
    return (_roll_rows(g, -s),)


_roll_rows.defvjp(_roll_rows_fwd, _roll_rows_bwd)


@jax.custom_vjp
def _drop_head(xx):
    return xx[SUBLANES:]


def _drop_head_fwd(xx):
    return xx[SUBLANES:], None


def _drop_head_bwd(_, g):
    return (jnp.concatenate([jnp.zeros((SUBLANES, g.shape[1]), g.dtype), g], axis=0),)


_drop_head.defvjp(_drop_head_fwd, _drop_head_bwd)


@jax.custom_vjp
def _last_rows(x):
    return x[x.shape[0] - SUBLANES:]


def _last_rows_fwd(x):
    return x[x.shape[0] - SUBLANES:], x.shape[0]


def _last_rows_bwd(n, g):
    return (jnp.concatenate([jnp.zeros((n - SUBLANES, g.shape[1]), g.dtype), g], axis=0),)


_last_rows.defvjp(_last_rows_fwd, _last_rows_bwd)


def _last_row(x):
    n = x.shape[0]
    return jnp.sum(jnp.where(_iota_rows(x.shape) == n - 1, x, 0.0), axis=0, keepdims=True)


def _scan_steps(n):
    s = 1
    while s < n:
        yield s
        s *= 2


def _block_scan_impl(a, u, h0):
    n = a.shape[0]
    row = _iota_rows(a.shape)
    for s in _scan_steps(n):
        keep = row >= s
        a_s = jnp.where(keep, pltpu.roll(a, s, 0), 1.0)
        u_s = jnp.where(keep, pltpu.roll(u, s, 0), 0.0)
        u = u + a * u_s
        a = a * a_s
    return u + a * h0


@jax.custom_vjp
def _block_scan(a, u, h0):
    return _block_scan_impl(a, u, h0)


def _block_scan_fwd(a, u, h0):
    h = _block_scan_impl(a, u, h0)
    return h, (a, h, h0)


def _block_scan_bwd(res, dh):
    a, h, h0 = res
    n = a.shape[0]
    row = _iota_rows(a.shape)
    b = jnp.where(row < n - 1, pltpu.roll(a, n - 1, 0), 0.0)
    lam = dh
    for s in _scan_steps(n):
        keep = row < n - s
        b_s = jnp.where(keep, pltpu.roll(b, n - s, 0), 1.0)
        l_s = jnp.where(keep, pltpu.roll(lam, n - s, 0), 0.0)
        lam = lam + b * l_s
        b = b * b_s
    h_prev = jnp.where(row >= 1, pltpu.roll(h, 1, 0), jnp.broadcast_to(h0, h.shape))
    d_h0 = jnp.sum(jnp.where(row == 0, a * lam, 0.0), axis=0, keepdims=True)
    return lam * h_prev, lam, d_h0


_block_scan.defvjp(_block_scan_fwd, _block_scan_bwd)


def _dot_hi(a, b, dims=(((1,), (0,)), ((), ()))):
    return lax.dot_general(a, b, dims, precision=HI, preferred_element_type=F32)


_NN, _NT, _TN = "nn", "nt", "tn"
_CONTRACT = {_NN: (1, 0), _NT: (1, 1), _TN: (0, 0)}


def _raw_dot(a, b, kind):
    ca, cb = _CONTRACT[kind]
    lead = a.ndim - 2
    dims = (((ca + lead,), (cb + lead,)), (tuple(range(lead)), tuple(range(lead))))
    return lax.dot_general(a.astype(DN_DTYPE), b.astype(DN_DTYPE), dims, preferred_element_type=F32)


@jax.custom_vjp
def _nn(a, b):
    return _raw_dot(a, b, _NN)


_nn.defvjp(lambda a, b: (_raw_dot(a, b, _NN), (a, b)),
           lambda r, g: (_raw_dot(g, r[1], _NT), _raw_dot(r[0], g, _TN)))


@jax.custom_vjp
def _nt(a, b):
    return _raw_dot(a, b, _NT)


_nt.defvjp(lambda a, b: (_raw_dot(a, b, _NT), (a, b)),
           lambda r, g: (_raw_dot(g, r[1], _NN), _raw_dot(g, r[0], _TN)))


@jax.custom_vjp
def _tn(a, b):
    return _raw_dot(a, b, _TN)


_tn.defvjp(lambda a, b: (_raw_dot(a, b, _TN), (a, b)),
           lambda r, g: (_raw_dot(r[1], g, _NT), _raw_dot(r[0], g, _NN)))


def _neumann_inverse(a):
    n = a.shape[-1]
    eye = (lax.broadcasted_iota(jnp.int32, (n, n), 0) == lax.broadcasted_iota(jnp.int32, (n, n), 1)).astype(F32)
    p = _raw_dot(a, a, _NN)
    e = p
    for _ in range(int(math.log2(n)) - 2):
        p = _raw_dot(p, p, _NN)
        e = e + p + _raw_dot(e, p, _NN)
    return eye - a + e - _raw_dot(a, e, _NN)


@jax.custom_vjp
def _unit_lower_inverse(a):
    return _neumann_inverse(a)


def _unit_lower_inverse_fwd(a):
    x = _neumann_inverse(a)
    return x, x


def _unit_lower_inverse_bwd(x, g):
    return (-_raw_dot(_raw_dot(x, g, _TN), x, _NT),)


_unit_lower_inverse.defvjp(_unit_lower_inverse_fwd, _unit_lower_inverse_bwd)


def _softplus(x):
    return jnp.maximum(x, 0.0) + jnp.log1p(jnp.exp(-jnp.abs(x)))


def _neg_expm1(x):
    series = -x * (1.0 + x * (0.5 + x * (1.0 / 6.0 + x * (1.0 / 24.0 + x * (1.0 / 120.0)))))
    return jnp.where(x > -0.03, series, 1.0 - jnp.exp(x))


def f_modulate(x, sc, sh):
    return (x * (1.0 + sc) + sh,), ()


def _deepnorm(x, y, gt, g, b):
    v = DEEPNORM_ALPHA * x + (1.0 + gt) * y
    mu = jnp.mean(v, axis=-1, keepdims=True)
    vc = v - mu
    var = jnp.mean(vc * vc, axis=-1, keepdims=True)
    return vc * lax.rsqrt(var + LN_EPS) * g + b


def f_deepnorm_mod(x, y, gt, g, b, sc, sh):
    x1 = _deepnorm(x, y, gt, g, b)
    return (x1, x1 * (1.0 + sc) + sh), ()


def f_deepnorm_loss(x, y, gt, g, b, target):
    err = _deepnorm(x, y, gt, g, b) - target
    return (0.5 * jnp.mean(err * err, axis=-1, keepdims=True),), ()


def _causal_conv(x, prev, ws):
    xx = jnp.concatenate([prev, x], axis=0)
    k = len(ws)
    y = ws[k - 1] * x
    for j in range(k - 1):
        y = y + ws[j] * _drop_head(_roll_rows(xx, k - 1 - j))
    return y


def f_rg_conv(x, w0, w1, w2, w3, b, prev):
    return (_causal_conv(x, prev, (w0, w1, w2, w3)) + b,), (_last_rows(x),)


def f_dn_conv(x, w0, w1, w2, w3, prev):
    return (jax.nn.silu(_causal_conv(x, prev, (w0, w1, w2, w3))),), (_last_rows(x),)


def f_ffn_act(gp, up, w0, w1, w2, b, prev):
    return (jax.nn.gelu(_causal_conv(gp, prev, (w0, w1, w2)) + b) * up,), (_last_rows(gp),)


def f_rglru(xc, pre_r, pre_i, gr, b_a, b_x, lam, h0):
    gate_r = jax.nn.sigmoid(pre_r + b_a)
    gate_i = jax.nn.sigmoid(pre_i + b_x)
    log_a = -RG_C * gate_r * _softplus(-lam)
    a = jnp.exp(log_a)
    mult = jnp.sqrt(_neg_expm1(2.0 * log_a))
    h = _block_scan(a, mult * gate_i * xc, h0)
    return (h * jax.nn.gelu(gr),), (_last_row(h),)


def f_dn_conv_norm(scale, x, w0, w1, w2, w3, prev):
    y = jax.nn.silu(_causal_conv(x, prev, (w0, w1, w2, w3)))
    return (y * lax.rsqrt(jnp.sum(y * y, axis=-1, keepdims=True) + L2_EPS) * scale,), (_last_rows(x),)


def f_dn_gates(a_in, b_in, a_log, dt_bias):
    g = -jnp.exp(a_log) * _softplus(a_in + dt_bias)
    n = g.shape[0]
    shift = int(math.log2(DN_CHUNK))
    ri = lax.broadcasted_iota(jnp.int32, (n, n), 0)
    ci = lax.broadcasted_iota(jnp.int32, (n, n), 1)
    tri = ((lax.shift_right_logical(ri, shift) == lax.shift_right_logical(ci, shift)) & (ri >= ci)).astype(F32)
    return (_dot_hi(tri, g), jax.nn.sigmoid(b_in)), ()


def f_dn_out(o, z, nw):
    r = lax.rsqrt(jnp.mean(o * o, axis=-1, keepdims=True) + RMS_EPS)
    return (o * r * nw * jax.nn.silu(z),), ()


def f_merge(ga, gb, ya, yb):
    return (jax.nn.sigmoid(ga) * ya + jax.nn.sigmoid(gb) * yb,), ()


def _delta_intra(q, k, v, g_i, g_j, beta):
    c = q.shape[-2]
    ri = lax.broadcasted_iota(jnp.int32, (c, c), 0)
    ci = lax.broadcasted_iota(jnp.int32, (c, c), 1)
    decay = jnp.exp(jnp.where(ri >= ci, g_i - g_j, -jnp.inf))
    g_last = jnp.sum(jnp.where(_iota_rows((c, 1)) == c - 1, g_i, 0.0), axis=-2, keepdims=True)
    exp_g = jnp.exp(g_i)
    kb = k * beta
    t_inv = _unit_lower_inverse(jnp.where(ri > ci, _nt(kb, k) * decay, 0.0))
    u = _nn(t_inv, v * beta)
    w = _nn(t_inv, kb * exp_g)
    return u, w, _nt(q, k) * decay, q * exp_g, k * jnp.exp(g_last - g_i)


def _delta_inter(u, w, qk, q_dec, k_dec, g_last, state):
    v_new = u - _nn(w, state)
    o = _nn(q_dec, state) + _nn(qk, v_new)
    return o, jnp.exp(g_last) * state + _tn(k_dec, v_new)


def _chunk_spec(width, nc=None, col=0):
    if nc is None:
        return pl.BlockSpec((DN_CHUNK, width), lambda s: (s, col))
    return pl.BlockSpec((DN_CHUNK, width), lambda s: (nc - 1 - s, col))


def _delta_params(sem):
    return pltpu.CompilerParams(dimension_semantics=(sem,), vmem_limit_bytes=VMEM_LIMIT)


def _head(ref, h, width=LANES):
    return ref[:, h * LANES:h * LANES + width]


def _head_groups(n_vh):
    hb = min(DN_HEAD_GROUP, n_vh)
    return [range(h0, h0 + hb) for h0 in range(0, n_vh, hb)]


def _stack(hs, f):
    return jnp.stack([f(h) for h in hs])


def _intra_operands(hs, rep, q_ref, k_ref, v_ref, g_ref, gt_ref, b_ref):
    return (_stack(hs, lambda h: _head(q_ref, h // rep)), _stack(hs, lambda h: _head(k_ref, h // rep)),
            _stack(hs, lambda h: _head(v_ref, h)), _stack(hs, lambda h: g_ref[:, h:h + 1]),
            _stack(hs, lambda h: gt_ref[h:h + 1, :]), _stack(hs, lambda h: b_ref[:, h:h + 1]))


def _inter_operands(hs, u_ref, w_ref, qk_ref, qd_ref, kd_ref, g_ref):
    f32 = lambda ref, width=LANES: _stack(hs, lambda h: _head(ref, h, width).astype(F32))
    return (f32(u_ref), f32(w_ref), f32(qk_ref, DN_CHUNK), f32(qd_ref), f32(kd_ref),
            _stack(hs, lambda h: g_ref[DN_CHUNK - 1:DN_CHUNK, h:h + 1]))


def delta_intra_fwd(qn, kn, qkv, v_blk, big_g, big_gt, beta, n_vh):
    t, qk_w = qn.shape
    vdim = n_vh * LANES
    rep = vdim // qk_w
    nc = t // DN_CHUNK

    def body(q_ref, k_ref, v_ref, g_ref, gt_ref, b_ref, u_ref, w_ref, qk_ref, qd_ref, kd_ref):
        for hs in _head_groups(n_vh):
            u, w, qk, qd, kd = _delta_intra(*_intra_operands(hs, rep, q_ref, k_ref, v_ref, g_ref, gt_ref, b_ref))
            for i, h in enumerate(hs):
                sl = slice(h * LANES, (h + 1) * LANES)
                u_ref[:, sl] = u[i]
                w_ref[:, sl] = w[i].astype(w_ref.dtype)
                qk_ref[:, sl] = jnp.concatenate([qk[i], jnp.zeros_like(qk[i])], axis=1).astype(qk_ref.dtype)
                qd_ref[:, sl] = qd[i].astype(qd_ref.dtype)
                kd_ref[:, sl] = kd[i].astype(kd_ref.dtype)

    return pl.pallas_call(
        body, name="delta_intra_fwd", grid=(nc,),
        in_specs=[_chunk_spec(qk_w), _chunk_spec(qk_w), _chunk_spec(vdim, col=v_blk), _chunk_spec(LANES),
                  pl.BlockSpec((None, n_vh, DN_CHUNK), lambda s: (s, 0, 0)), _chunk_spec(LANES)],
        out_specs=[_chunk_spec(vdim)] * 5,
        out_shape=[jax.ShapeDtypeStruct((t, vdim), F32)] + [jax.ShapeDtypeStruct((t, vdim), DN_DTYPE)] * 4,
        compiler_params=_delta_params("parallel"),
    )(qn, kn, qkv, big_g, big_gt, beta)


def delta_inter_fwd(u, w, qk, q_dec, k_dec, big_g, n_vh):
    t, vdim = u.shape
    nc = t // DN_CHUNK

    def body(u_ref, w_ref, qk_ref, qd_ref, kd_ref, g_ref, o_ref, hist_ref, s_ref):
        @pl.when(pl.program_id(0) == 0)
        def _():
            s_ref[...] = jnp.zeros_like(s_ref)
        for hs in _head_groups(n_vh):
            grp = slice(hs[0], hs[-1] + 1)
            st = s_ref[grp]
            hist_ref[grp] = st
            o, ns = _delta_inter(*_inter_operands(hs, u_ref, w_ref, qk_ref, qd_ref, kd_ref, g_ref), st)
            for i, h in enumerate(hs):
                o_ref[:, h * LANES:(h + 1) * LANES] = o[i]
            s_ref[grp] = ns

    return pl.pallas_call(
        body, name="delta_inter_fwd", grid=(nc,),
        in_specs=[_chunk_spec(vdim)] * 5 + [_chunk_spec(LANES)],
        out_specs=[_chunk_spec(vdim), pl.BlockSpec((None, n_vh, LANES, LANES), lambda s: (s, 0, 0, 0))],
        out_shape=[jax.ShapeDtypeStruct((t, vdim), F32), jax.ShapeDtypeStruct((nc, n_vh, LANES, LANES), F32)],
        scratch_shapes=[pltpu.VMEM((n_vh, LANES, LANES), F32)],
        compiler_params=_delta_params("arbitrary"),
    )(u, w, qk, q_dec, k_dec, big_g)


def delta_inter_bwd(u, w, qk, q_dec, k_dec, big_g, hist, d_o, n_vh):
    t, vdim = u.shape
    nc = t // DN_CHUNK

    def body(u_ref, w_ref, qk_ref, qd_ref, kd_ref, g_ref, hist_ref, do_ref,
             du_ref, dw_ref, dqk_ref, dqd_ref, dkd_ref, dg_ref, ds_ref):
        @pl.when(pl.program_id(0) == 0)
        def _():
            ds_ref[...] = jnp.zeros_like(ds_ref)
        lane = lax.broadcasted_iota(jnp.int32, (1, LANES), 1)
        dgl_all = jnp.zeros((1, LANES), F32)
        for hs in _head_groups(n_vh):
            grp = slice(hs[0], hs[-1] + 1)
            prim = _inter_operands(hs, u_ref, w_ref, qk_ref, qd_ref, kd_ref, g_ref) + (hist_ref[grp],)
            _, vjp = jax.vjp(_delta_inter, *prim)
            du, dw, dqk, dqd, dkd, dgl, dst = vjp((_stack(hs, lambda h: _head(do_ref, h)), ds_ref[grp]))
            ds_ref[grp] = dst
            for i, h in enumerate(hs):
                sl = slice(h * LANES, (h + 1) * LANES)
                du_ref[:, sl] = du[i]
                dw_ref[:, sl] = dw[i]
                dqk_ref[:, sl] = jnp.concatenate([dqk[i], jnp.zeros_like(dqk[i])], axis=1)
                dqd_ref[:, sl] = dqd[i]
                dkd_ref[:, sl] = dkd[i]
                dgl_all = dgl_all + dgl[i] * (lane == h).astype(F32)
        last = _iota_rows((DN_CHUNK, LANES)) == DN_CHUNK - 1
        dg_ref[...] = jnp.where(last, jnp.broadcast_to(dgl_all, (DN_CHUNK, LANES)), 0.0)

    rv = lambda w_: _chunk_spec(w_, nc)
    return pl.pallas_call(
        body, name="delta_inter_bwd", grid=(nc,),
        in_specs=[rv(vdim)] * 5 + [rv(LANES), pl.BlockSpec((None, n_vh, LANES, LANES), lambda s: (nc - 1 - s, 0, 0, 0)),
                                   rv(vdim)],
        out_specs=[rv(vdim)] * 5 + [rv(LANES)],
        out_shape=[jax.ShapeDtypeStruct((t, vdim), F32)] * 5 + [jax.ShapeDtypeStruct((t, LANES), F32)],
        scratch_shapes=[pltpu.VMEM((n_vh, LANES, LANES), F32)],
        compiler_params=_delta_params("arbitrary"),
    )(u, w, qk, q_dec, k_dec, big_g, hist, d_o)


def delta_intra_bwd(qn, kn, qkv, v_blk, big_g, big_gt, beta, cots, n_vh):
    t, qk_w = qn.shape
    vdim = n_vh * LANES
    rep = vdim // qk_w
    nc = t // DN_CHUNK

    def body(q_ref, k_ref, v_ref, g_ref, gt_ref, b_ref, du_ref, dw_ref, dqk_ref, dqd_ref, dkd_ref,
             dq_ref, dk_ref, dv_ref, dg_ref, dgt_ref, db_ref):
        lane = lax.broadcasted_iota(jnp.int32, (1, LANES), 1)
        dg_all = jnp.zeros((DN_CHUNK, LANES), F32)
        db_all = jnp.zeros((DN_CHUNK, LANES), F32)
        dq_acc, dk_acc = None, None
        for hs in _head_groups(n_vh):
            _, vjp = jax.vjp(_delta_intra, *_intra_operands(hs, rep, q_ref, k_ref, v_ref, g_ref, gt_ref, b_ref))
            cot = lambda ref, width=LANES: _stack(hs, lambda h: _head(ref, h, width))
            dq, dk, dv, dgi, dgj, db = vjp((cot(du_ref), cot(dw_ref), cot(dqk_ref, DN_CHUNK), cot(dqd_ref), cot(dkd_ref)))
            for i, h in enumerate(hs):
                j = h // rep
                dv_ref[:, h * LANES:(h + 1) * LANES] = dv[i]
                dgt_ref[h:h + 1, :] = dgj[i]
                onehot = (lane == h).astype(F32)
                dg_all = dg_all + dgi[i] * onehot
                db_all = db_all + db[i] * onehot
                dq_acc = dq[i] if h % rep == 0 else dq_acc + dq[i]
                dk_acc = dk[i] if h % rep == 0 else dk_acc + dk[i]
                if h % rep == rep - 1:
                    dq_ref[:, j * LANES:(j + 1) * LANES] = dq_acc
                    dk_ref[:, j * LANES:(j + 1) * LANES] = dk_acc
        dg_ref[...] = dg_all
        db_ref[...] = db_all

    gt_spec = pl.BlockSpec((None, n_vh, DN_CHUNK), lambda s: (s, 0, 0))
    return pl.pallas_call(
        body, name="delta_intra_bwd", grid=(nc,),
        in_specs=[_chunk_spec(qk_w), _chunk_spec(qk_w), _chunk_spec(vdim, col=v_blk), _chunk_spec(LANES), gt_spec,
                  _chunk_spec(LANES)] + [_chunk_spec(vdim)] * 5,
        out_specs=[_chunk_spec(qk_w), _chunk_spec(qk_w), _chunk_spec(vdim), _chunk_spec(LANES), gt_spec,
                   _chunk_spec(LANES)],
        out_shape=[jax.ShapeDtypeStruct((t, qk_w), F32), jax.ShapeDtypeStruct((t, qk_w), F32),
                   jax.ShapeDtypeStruct((t, vdim), F32), jax.ShapeDtypeStruct((t, LANES), F32),
                   jax.ShapeDtypeStruct((nc, n_vh, DN_CHUNK), F32), jax.ShapeDtypeStruct((t, LANES), F32)],
        compiler_params=_delta_params("parallel"),
    )(qn, kn, qkv, big_g, big_gt, beta, *cots)


def all_gather(name, arrs):
    n = len(arrs)

    def body(*refs):
        in_refs, out_refs, sems = refs[:n], refs[n:2 * n], refs[2 * n:]
        _exchange_copies(in_refs, out_refs, sems, False, "start")
        _exchange_copies(in_refs, out_refs, sems, False, "wait")

    res = pl.pallas_call(
        body, name=name,
        in_specs=[_HBM] * n, out_specs=[_HBM] * n,
        out_shape=_exchange_out_shape(arrs, False), scratch_shapes=_exchange_sems(n),
        compiler_params=pltpu.CompilerParams(has_side_effects=True),
    )(*arrs)
    return list(res)


def _adamw_math(w, g, m, v):
    m = ADAM_B1 * m + (1.0 - ADAM_B1) * g
    v = ADAM_B2 * v + (1.0 - ADAM_B2) * (g * g)
    m_hat = m / (1.0 - ADAM_B1 ** ADAM_STEP)
    v_hat = v / (1.0 - ADAM_B2 ** ADAM_STEP)
    delta = -ADAM_LR * (m_hat / (jnp.sqrt(v_hat) + ADAM_EPS) + ADAM_WD * w)
    return delta, m, v


def adamw(name, w, parts, m, v, rows_cap=128):
    r, c = w.shape
    np_ = parts.shape[0]
    tr = _tile(r, rows_cap, SUBLANES * (4 // parts.dtype.itemsize))

    def body(w_ref, p_ref, m_ref, v_ref, g_ref, d_ref, nm_ref, nv_ref):
        g = p_ref[0].astype(F32)
        for k in range(1, np_):
            g = g + p_ref[k].astype(F32)
        delta, nm, nv = _adamw_math(w_ref[...], g, m_ref[...], v_ref[...])
        g_ref[...] = g
        d_ref[...] = delta
        nm_ref[...] = nm
        nv_ref[...] = nv

    spec = pl.BlockSpec((tr, c), lambda i: (i, 0))
    return pl.pallas_call(
        body, name=name, grid=(r // tr,),
        in_specs=[spec, pl.BlockSpec((np_, tr, c), lambda i: (0, i, 0)), spec, spec],
        out_specs=[spec] * 4, out_shape=[jax.ShapeDtypeStruct((r, c), F32)] * 4,
        compiler_params=pltpu.CompilerParams(dimension_semantics=("parallel",), vmem_limit_bytes=VMEM_LIMIT),
    )(w, parts, m, v)


def sum_parts(name, parts, rows_cap=256):
    np_, r, c = parts.shape
    tr = _tile(r, rows_cap, SUBLANES)

    def body(p_ref, o_ref):
        g = p_ref[0].astype(F32)
        for k in range(1, np_):
            g = g + p_ref[k].astype(F32)
        o_ref[...] = g

    return pl.pallas_call(
        body, name=name, grid=(r // tr,),
        in_specs=[pl.BlockSpec((np_, tr, c), lambda i: (0, i, 0))],
        out_specs=pl.BlockSpec((tr, c), lambda i: (i, 0)),
        out_shape=jax.ShapeDtypeStruct((r, c), F32),
        compiler_params=pltpu.CompilerParams(dimension_semantics=("parallel",), vmem_limit_bytes=VMEM_LIMIT),
    )(parts)


def _pack(arrs):
    flat = jnp.concatenate([a.reshape(-1).astype(F32) for a in arrs])
    n = flat.shape[0]
    return jnp.pad(flat, (0, _round_up(n, LANES * SUBLANES) - n)).reshape(-1, LANES)


def _unpack(packed, like):
    flat, out, pos = packed.reshape(-1), [], 0
    for a in like:
        out.append(flat[pos:pos + a.size].reshape(a.shape))
        pos += a.size
    return out


def kernel(x, c, w_ada, b_ada, w_in, rg_conv_w, rg_conv_b, rg_w_a, rg_b_a, rg_w_x, rg_b_x, rg_lambda, dn_conv_w, dn_a_log, dn_dt_bias, dn_norm_w, w_proj_a, w_proj_b, w_out, ln1_g, ln1_b, ffn_w_gate, ffn_w_up, ffn_conv_w, ffn_conv_b, ffn_w_down, ln2_g, ln2_b, loss_target, m_w_ada, m_b_ada, m_w_in, m_rg_conv_w, m_rg_conv_b, m_rg_w_a, m_rg_b_a, m_rg_w_x, m_rg_b_x, m_rg_lambda, m_dn_conv_w, m_dn_a_log, m_dn_dt_bias, m_dn_norm_w, m_w_proj_a, m_w_proj_b, m_w_out, m_ln1_g, m_ln1_b, m_ffn_w_gate, m_ffn_w_up, m_ffn_conv_w, m_ffn_conv_b, m_ffn_w_down, m_ln2_g, m_ln2_b, v_w_ada, v_b_ada, v_w_in, v_rg_conv_w, v_rg_conv_b, v_rg_w_a, v_rg_b_a, v_rg_w_x, v_rg_b_x, v_rg_lambda, v_dn_conv_w, v_dn_a_log, v_dn_dt_bias, v_dn_norm_w, v_w_proj_a, v_w_proj_b, v_w_out, v_ln1_g, v_ln1_b, v_ffn_w_gate, v_ffn_w_up, v_ffn_conv_w, v_ffn_conv_b, v_ffn_w_down, v_ln2_g, v_ln2_b):
    names = ['w_ada', 'b_ada', 'w_in', 'rg_conv_w', 'rg_conv_b', 'rg_w_a', 'rg_b_a', 'rg_w_x', 'rg_b_x', 'rg_lambda',
             'dn_conv_w', 'dn_a_log', 'dn_dt_bias', 'dn_norm_w', 'w_proj_a', 'w_proj_b', 'w_out', 'ln1_g', 'ln1_b',
             'ffn_w_gate', 'ffn_w_up', 'ffn_conv_w', 'ffn_conv_b', 'ffn_w_down', 'ln2_g', 'ln2_b']
    loc = locals()
    W = {n: loc[n][0] for n in names}
    M = {n: loc['m_' + n][0] for n in names}
    V = {n: loc['v_' + n][0] for n in names}

    me = 4 * lax.axis_index("x") + 2 * lax.axis_index("y") + lax.axis_index("c")
    xs, tgt = x[0], loss_target[0]
    t, d = xs.shape
    d_rnn = W['rg_conv_b'].shape[0]
    n_blk = W['rg_w_a'].shape[0]
    n_vh = W['dn_a_log'].shape[0]
    assert W['dn_norm_w'].shape[0] == LANES
    vdim = n_vh * LANES
    d_ff = W['ffn_conv_b'].shape[0]
    d_in = W['w_in'].shape[1] * N_DEV
    qk = (d_in - 2 * d_rnn - 2 * vdim - 2 * n_vh - 2 * d) // 2
    assert vdim == 2 * qk and qk % LANES == 0 and n_vh <= LANES
    splits = (d_rnn, d_rnn, qk, qk, vdim, vdim, n_vh, n_vh, d, d)
    offs = [0]
    for s_ in splits:
        offs.append(offs[-1] + s_)

    tb = _tile(t, 256, SUBLANES)

    big = ['w_in', 'w_proj_a', 'w_proj_b', 'w_out', 'ffn_w_gate', 'ffn_w_up', 'ffn_w_down']
    small_sh = ['rg_conv_w', 'dn_conv_w', 'ffn_conv_w']
    first = all_gather("gather_first", [W['w_in'].astype(WIRE_DTYPE)] + [W[n] for n in small_sh] + [c])
    g_in, g_rcw, g_dcw, g_fcw, c_all = first
    cols = lambda g: jnp.transpose(g, (1, 0, 2)).reshape(g.shape[1], -1)
    rows = lambda g: g.reshape(-1, g.shape[2])
    w_in_f = cols(g_in)
    padl = lambda a: jnp.pad(a, ((0, 0), (0, LANES - a.shape[1])))
    groups = [w_in_f[:, offs[i]:offs[i + 1]] for i in range(10)]
    groups[6], groups[7] = padl(groups[6]), padl(groups[7])
    go = [0]
    for g_ in groups:
        go.append(go[-1] + g_.shape[1])
    n_pad = _round_up(go[-1], 512)
    wp = jnp.pad(jnp.concatenate(groups, axis=1), ((0, 0), (0, n_pad - go[-1])))
    o_xr, o_gr, o_q, o_k, o_v, o_z, o_a, o_b, o_ga, o_gb = go[:10]
    rcw, dcw, fcw = cols(g_rcw), cols(g_dcw), cols(g_fcw)
    eye_b = jnp.eye(n_blk, dtype=F32)
    bd = lambda w: (w[:, :, None, :] * eye_b[:, None, :, None]).reshape(d_rnn, d_rnn)
    w_bd = jnp.concatenate([bd(W['rg_w_a']), bd(W['rg_w_x'])], axis=1)
    row1 = lambda a: a.reshape(1, -1)
    padv = lambda a: jnp.pad(row1(a), ((0, 0), (0, LANES - a.shape[0])))
    nw_t = jnp.tile(row1(W['dn_norm_w']), (1, n_vh))

    c_pad =jnp.pad(c_all.reshape(N_DEV, d), ((0, LANES - N_DEV), (0, 0)))
    ada_w = W['w_ada'].shape[1]
    b_ada_me = lax.dynamic_slice(W['b_ada'], (me * ada_w,), (ada_w,)).reshape(1, ada_w)
    ada_sh = mm(c_pad, W['w_ada'], name="ada_fwd", a_act="silu", bias=b_ada_me)
    (ada_all,) = all_gather("gather_ada", [ada_sh[:N_DEV]])
    ada_me = lax.dynamic_slice(ada_all, (0, me, 0), (N_DEV, 1, ada_w)).reshape(6, 1, d)
    sh1, sc1, gt1, sh2, sc2, gt2 = [ada_me[i] for i in range(6)]

    nt = t // tb

    def act(a, bw, col0=0, width=None, grad=True, rows=tb):
        width = a.shape[1] if width is None else width
        assert col0 % bw == 0 and width % bw == 0
        c0 = col0 // bw
        return In(a, (rows, bw), lambda o, s: (s, c0 + o), grad=grad, gshape=(t, width), gimap=lambda o, s: (s, o))

    def prm(a, bw, parts=None):
        return In(a, (a.shape[0], bw), lambda o, s: (0, o), acc=True, parts=parts)

    def out(width, bw, rows=tb):
        return Out((t, width), (rows, bw), lambda o, s: (s, o))

    tbh = _tile(t, 1024, SUBLANES)
    nth = t // tbh
    tbc = _tile(t, 1024, SUBLANES)
    ntc = t // tbc

    krows = lambda k_: [(slice(j, j + 1), slice(None)) for j in range(k_)]

    mod1_ins = [act(xs, d), prm(sc1, d), prm(sh1, d)]
    (h1,), _, (h1_t,) = stage_fwd("mod1_fwd", f_modulate, (1, nt), mod1_ins, [out(d, d)], transposed=[0])
    proj, g_pa, g_pb, g_out, g_fg, g_fu, g_fd = mm(h1, wp, name="proj_fwd",
                                                   gather=[W[n].astype(WIRE_DTYPE) for n in big[1:]])
    w_pa, w_pb, w_o, w_fd = rows(g_pa), rows(g_pb), rows(g_out), rows(g_fd)
    w_gate, w_up = cols(g_fg), cols(g_fu)
    w_gu = jnp.concatenate([w_gate, w_up], axis=1)

    cb_r = _tile(math.gcd(d_rnn, o_gr), 256)
    rgc_ins = [act(proj, cb_r, o_xr, d_rnn, rows=tbc), prm(rcw, cb_r, krows(4)), prm(row1(W['rg_conv_b']), cb_r)]
    rgc_grid, rgc_car, rgc_outs = (d_rnn // cb_r, ntc), [(SUBLANES, cb_r)], [out(d_rnn, cb_r, tbc)]
    (xc,), rgc_hist, (xc_t,) = stage_fwd("rg_conv_fwd", f_rg_conv, rgc_grid, rgc_ins, rgc_outs, rgc_car, transposed=[0])
    gates = mm(xc, w_bd, name="rg_gates_fwd")
    lru_ins = [act(xc, cb_r), act(gates, cb_r, 0, d_rnn), act(gates, cb_r, d_rnn, d_rnn), act(proj, cb_r, o_gr, d_rnn),
               prm(row1(W['rg_b_a']), cb_r), prm(row1(W['rg_b_x']), cb_r), prm(row1(W['rg_lambda']), cb_r)]
    lru_grid, lru_car = (d_rnn // cb_r, nt), [(1, cb_r)]
    (rec,), lru_hist, (rec_t,) = stage_fwd("rglru_fwd", f_rglru, lru_grid, lru_ins, [out(d_rnn, cb_r)], lru_car,
                                           transposed=[0])
    y_a = mm(rec, w_pa, name="proj_a_fwd")

    dnc = {}
    for nm, col0, width, w0, cb_, f_ in (("q", o_q, qk, 0, LANES, functools.partial(f_dn_conv_norm, LANES ** -0.5)),
                                         ("k", o_k, qk, qk, LANES, functools.partial(f_dn_conv_norm, 1.0)),
                                         ("v", o_v, vdim, 2 * qk, _tile(math.gcd(vdim, o_v), 256), f_dn_conv)):
        ins_ = [act(proj, cb_, col0, width, rows=tbc), prm(dcw[:, w0:w0 + width], cb_, krows(4))]
        grid_, outs_, car_ = (width // cb_, ntc), [out(width, cb_, tbc)], [(SUBLANES, cb_)]
        (y_,), hist_ = stage_fwd("dn_conv_%s_fwd" % nm, f_, grid_, ins_, outs_, car_)
        dnc[nm] = (y_, f_, ins_, grid_, outs_, car_, hist_)
    qn, kn, v_c = dnc["q"][0], dnc["k"][0], dnc["v"][0]
    gate_ins = [act(proj, LANES, o_a, LANES), act(proj, LANES, o_b, LANES),
                prm(padv(W['dn_a_log']), LANES), prm(padv(W['dn_dt_bias']), LANES)]
    gate_outs = [out(LANES, LANES), out(LANES, LANES)]
    (g_dn, beta_dn), _ = stage_fwd("dn_gates_fwd", f_dn_gates, (1, nt), gate_ins, gate_outs)
    n_ch = t // DN_CHUNK
    gt_dn = jnp.transpose(g_dn.reshape(n_ch, DN_CHUNK, LANES)[:, :, :n_vh], (0, 2, 1))
    dn_mid = delta_intra_fwd(qn, kn, v_c, 0, g_dn, gt_dn, beta_dn, n_vh)
    o_dn, dn_hist = delta_inter_fwd(*dn_mid, g_dn, n_vh)
    dno_ins = [act(o_dn, LANES, rows=tbh), act(proj, LANES, o_z, vdim, rows=tbh), prm(nw_t, LANES)]
    dno_grid, dno_outs = (n_vh, nth), [out(vdim, LANES, tbh)]
    (dn,), _, (dn_t,) = stage_fwd("dn_out_fwd", f_dn_out, dno_grid, dno_ins, dno_outs, transposed=[0])
    y_b = mm(dn, w_pb, name="proj_b_fwd")

    cb_m = _tile(math.gcd(math.gcd(d, o_ga), o_gb), 512)
    mrg_ins = [act(proj, cb_m, o_ga, d, rows=tbc), act(proj, cb_m, o_gb, d, rows=tbc), act(y_a, cb_m, rows=tbc),
               act(y_b, cb_m, rows=tbc)]
    mrg_grid, mrg_outs = (d // cb_m, ntc), [out(d, cb_m, tbc)]
    (merged,), _, (merged_t,) = stage_fwd("merge_fwd", f_merge, mrg_grid, mrg_ins, mrg_outs, transposed=[0])
    mix = mm(merged, w_o, name="w_out_fwd")
    ln1_ins = [act(xs, d), act(mix, d), prm(gt1, d), prm(row1(W['ln1_g']), d), prm(row1(W['ln1_b']), d),
               prm(sc2, d), prm(sh2, d)]
    ln1_outs = [out(d, d), out(d, d)]
    (x1, h2), _, (h2_t,) = stage_fwd("ln1_mod2_fwd", f_deepnorm_mod, (1, nt), ln1_ins, ln1_outs, transposed=[1])

    gu = mm(h2, w_gu, name="ffn_in_fwd")
    cb_f = _tile(d_ff, 256)
    ffa_ins = [act(gu, cb_f, 0, d_ff, rows=tbc), act(gu, cb_f, d_ff, d_ff, rows=tbc), prm(fcw, cb_f, krows(3)),
               prm(row1(W['ffn_conv_b']), cb_f)]
    ffa_grid, ffa_car, ffa_outs = (d_ff // cb_f, ntc), [(SUBLANES, cb_f)], [out(d_ff, cb_f, tbc)]
    (act_ff,), ffa_hist, (act_t,) = stage_fwd("ffn_act_fwd", f_ffn_act, ffa_grid, ffa_ins, ffa_outs, ffa_car,
                                              transposed=[0])
    ff = mm(act_ff, w_fd, name="ffn_down_fwd")
    ln2_ins = [act(x1, d), act(ff, d), prm(gt2, d), prm(row1(W['ln2_g']), d), prm(row1(W['ln2_b']), d),
               act(tgt, d, grad=False)]
    ln2_outs = [Out((t, 1), (tb, 1), lambda o, s: (s, 0))]
    (loss_rows,), _ = stage_fwd("ln2_loss_fwd", f_deepnorm_loss, (1, nt), ln2_ins, ln2_outs)

    dx1_a, d_ff_o, d_gt2, d_ln2g, d_ln2b = stage_bwd("ln2_loss_bwd", f_deepnorm_loss, (1, nt), ln2_ins, ln2_outs,
                                                     [jnp.ones((t, 1), F32)])
    d_act = mm(d_ff_o, w_fd, name="ffn_down_bwd_x", tb=True)
    gw_fd = mm(act_t, d_ff_o, name="ffn_down_bwd_w")
    d_gp, d_up, d_fcw, d_fcb = stage_bwd("ffn_act_bwd", f_ffn_act, ffa_grid, ffa_ins, ffa_outs, [d_act],
                                         ffa_car, ffa_hist, gdtypes={0: MXU_DTYPE, 1: MXU_DTYPE})
    col_blocks = lambda g: jnp.transpose(g.reshape(g.shape[0], N_DEV, -1), (1, 0, 2)).astype(WIRE_DTYPE)
    row_blocks = lambda g: g.reshape(N_DEV, -1, g.shape[1]).astype(WIRE_DTYPE)
    big_parts = {}
    d_h2, big_parts['ffn_w_down'] = mm([d_gp, d_up], [w_gate, w_up], name="ffn_in_bwd_x", tb=True,
                                       scatter=[row_blocks(gw_fd)])
    gw_gate, gw_up = mm(h2_t, d_gp, name="ffn_gate_bwd_w"), mm(h2_t, d_up, name="ffn_up_bwd_w")
    dx_a, d_mix, d_gt1, d_ln1g, d_ln1b, d_sc2, d_sh2 = stage_bwd("ln1_mod2_bwd", f_deepnorm_mod, (1, nt), ln1_ins,
                                                                 ln1_outs, [dx1_a, d_h2])
    d_merged = mm(d_mix, w_o, name="w_out_bwd_x", tb=True)
    gw_o = mm(merged_t, d_mix, name="w_out_bwd_w")
    d_ga, d_gb, d_ya, d_yb = stage_bwd("merge_bwd", f_merge, mrg_grid, mrg_ins, mrg_outs, [d_merged],
                                       gdtypes={0: MXU_DTYPE, 1: MXU_DTYPE})
    d_rec = mm(d_ya, w_pa, name="proj_a_bwd_x", tb=True)
    gw_pa = mm(rec_t, d_ya, name="proj_a_bwd_w")
    d_dn = mm(d_yb, w_pb, name="proj_b_bwd_x", tb=True)
    gw_pb = mm(dn_t, d_yb, name="proj_b_bwd_w")

    d_o, d_z, d_nwt = stage_bwd("dn_out_bwd", f_dn_out, dno_grid, dno_ins, dno_outs, [d_dn], gdtypes={1: MXU_DTYPE})
    *d_mid, d_g_state = delta_inter_bwd(*dn_mid, g_dn, dn_hist, d_o, n_vh)
    d_qn, d_kn, d_v, d_g_col, d_gt, d_beta = delta_intra_bwd(qn, kn, v_c, 0, g_dn, gt_dn, beta_dn, d_mid, n_vh)
    d_g_row = jnp.pad(jnp.transpose(d_gt, (0, 2, 1)).reshape(t, n_vh), ((0, 0), (0, LANES - n_vh)))
    d_a, d_b, d_alog, d_dtb = stage_bwd("dn_gates_bwd", f_dn_gates, (1, nt), gate_ins, gate_outs,
                                        [(d_g_state, d_g_col, d_g_row), d_beta], gdtypes={0: MXU_DTYPE, 1: MXU_DTYPE})
    d_win, d_dcw = {}, []
    for nm, cot in (("q", d_qn), ("k", d_kn), ("v", d_v)):
        _, f_, ins_, grid_, outs_, car_, hist_ = dnc[nm]
        d_win[nm], dw_ = stage_bwd("dn_conv_%s_bwd" % nm, f_, grid_, ins_, outs_, [cot], car_, hist_,
                                   gdtypes={0: MXU_DTYPE})
        d_dcw.append(dw_)
    d_dcw = jnp.concatenate(d_dcw, axis=1)

    d_xc_a, d_pr, d_pi, d_gr, d_ba, d_bx, d_lam = stage_bwd(
        "rglru_bwd", f_rglru, lru_grid, lru_ins, [out(d_rnn, cb_r)], [d_rec], lru_car, lru_hist,
        gdtypes={1: MXU_DTYPE, 2: MXU_DTYPE, 3: MXU_DTYPE})
    d_xc_b, big_parts['w_out'], big_parts['w_proj_a'], big_parts['w_proj_b'] = mm(
        [d_pr, d_pi], [w_bd[:, :d_rnn], w_bd[:, d_rnn:]], name="rg_gates_bwd_x", tb=True,
        scatter=[row_blocks(gw_o), row_blocks(gw_pa), row_blocks(gw_pb)])
    gw_bd_a, gw_bd_x = mm(xc_t, d_pr, name="rg_gate_a_bwd_w"), mm(xc_t, d_pi, name="rg_gate_x_bwd_w")
    d_xr, d_rcw, d_rcb = stage_bwd("rg_conv_bwd", f_rg_conv, rgc_grid, rgc_ins, rgc_outs, [(d_xc_a, d_xc_b)],
                                   rgc_car, rgc_hist, gdtypes={0: MXU_DTYPE})

    diag = lambda g: jnp.einsum('nimj,nm->nij', g.reshape(n_blk, d_rnn // n_blk, n_blk, d_rnn // n_blk), eye_b)
    small_names = ['rg_conv_w', 'rg_conv_b', 'rg_w_a', 'rg_b_a', 'rg_w_x', 'rg_b_x', 'rg_lambda', 'dn_conv_w',
                   'dn_a_log', 'dn_dt_bias', 'dn_norm_w', 'ln1_g', 'ln1_b', 'ffn_conv_w', 'ffn_conv_b', 'ln2_g', 'ln2_b']
    small_loc = {
        'rg_conv_w': d_rcw, 'rg_conv_b': d_rcb,
        'rg_w_a': diag(gw_bd_a), 'rg_b_a': d_ba, 'rg_w_x': diag(gw_bd_x), 'rg_b_x': d_bx,
        'rg_lambda': d_lam, 'dn_conv_w': d_dcw, 'dn_a_log': d_alog[:, :n_vh], 'dn_dt_bias': d_dtb[:, :n_vh],
        'dn_norm_w': jnp.sum(d_nwt.reshape(n_vh, LANES), axis=0), 'ln1_g': d_ln1g, 'ln1_b': d_ln1b,
        'ffn_conv_w': d_fcw, 'ffn_conv_b': d_fcb, 'ln2_g': d_ln2g, 'ln2_b': d_ln2b}
    small_list = [small_loc[n] for n in small_names]

    d_segs = [d_xr, d_gr, d_win["q"], d_win["k"], d_win["v"], d_z, d_a, d_b, d_ga, d_gb]
    riders = {4: ('ffn_w_gate', gw_gate), 5: ('ffn_w_up', gw_up)}
    gw_segs = []
    for i, dg in enumerate(d_segs):
        if i in riders:
            g_, big_parts[riders[i][0]] = mm(h1_t, dg, name="proj_bwd_w%d" % i, scatter=[col_blocks(riders[i][1])])
        else:
            g_ = mm(h1_t, dg, name="proj_bwd_w%d" % i)
        gw_segs.append(g_)
    gw_in = jnp.concatenate([g_[:, :splits[i]] for i, g_ in enumerate(gw_segs)], axis=1)
    half = len(d_segs) // 2
    d_h1_a, big_parts['w_in'] = mm(d_segs[:half], groups[:half], name="proj_bwd_x0", tb=True,
                                   scatter=[col_blocks(gw_in)], **MM_SPLIT_CAPS)
    d_h1_b, small_all = mm(d_segs[half:], groups[half:], name="proj_bwd_x1", tb=True,
                           gather=[_pack(small_list)], **MM_SPLIT_CAPS)
    grad_x, d_sc1, d_sh1 = stage_bwd("mod1_bwd", f_modulate, (1, nt), mod1_ins, [out(d, d)], [(d_h1_a, d_h1_b)],
                                     add_to={0: dx_a})

    g_small = dict(zip(small_names, _unpack(sum_parts("sum_small_grads", small_all), small_list)))
    d_ada_me = jnp.concatenate([d_sh1, d_sc1, d_gt1, d_sh2, d_sc2, d_gt2], axis=1)
    (d_ada_all,) = all_gather("gather_d_ada", [d_ada_me.reshape(-1, LANES)])
    g_small['b_ada'] = sum_parts("sum_d_ada", d_ada_all)
    small_names = ['b_ada'] + small_names
    d_ada_cols = lax.dynamic_slice(d_ada_all.reshape(N_DEV, 6 * d), (0, me * ada_w), (N_DEV, ada_w))
    d_ada_pad = jnp.pad(d_ada_cols, ((0, LANES - N_DEV), (0, 0)))
    gw_ada = mm(c_pad, d_ada_pad, name="ada_bwd_w", ta=True, a_act="silu")

    res = {}
    big_parts['w_ada'] = gw_ada[None]
    for n in ['w_ada'] + big:
        res[n] = adamw("adamw_" + n, W[n], big_parts[n], M[n], V[n])
    for n in small_sh:
        w_ = W[n].shape[1]
        g_small[n] = lax.dynamic_slice(g_small[n], (0, me * w_), (W[n].shape[0], w_))
    for n in small_names:
        g_small[n] = g_small[n].reshape(W[n].shape)
    pk = lambda dct: _pack([dct[n] for n in small_names])
    s_g, s_d, s_m, s_v = adamw("adamw_small", pk(W), pk(g_small)[None], pk(M), pk(V))
    like = [W[n] for n in small_names]
    for n, g_, d_, m_, v_ in zip(small_names, _unpack(s_g, like), _unpack(s_d, like), _unpack(s_m, like), _unpack(s_v, like)):
        res[n] = (g_, d_, m_, v_)

    loss = lax.psum(jnp.sum(loss_rows), ("x", "y", "c"))
    outs = [loss, grad_x[None]]
    for j in range(4):
        outs += [res[n][j].reshape(loc[n].shape) for n in names]
    return tuple(outs)
```

```python
import functools
import math

import jax
import jax.numpy as jnp
from jax import lax
from jax.experimental import pallas as pl
from jax.experimental.pallas import tpu as pltpu

F32 = jnp.float32
BF16 = jnp.bfloat16
MXU_DTYPE = BF16
WIRE_DTYPE = BF16
DN_DTYPE = BF16
HI = lax.Precision.HIGHEST
MESH = pl.DeviceIdType.MESH

N_DEV = 8
LANES = 128
SUBLANES = 8
VMEM_LIMIT = 56 * 1024 * 1024
MM_TM_CAP, MM_TN_CAP, MM_TK_CAP = 1024, 1536, 2048
MM_SPLIT_CAPS = dict(tm_cap=1024, tn_cap=1024, tk_cap=1024)

RG_C = 8.0
DN_CHUNK = 64
DN_HEAD_GROUP = 16
DN_INTER_CHUNKS = 4
LN_EPS = 1e-5
RMS_EPS = 1e-6
L2_EPS = 1e-6
DEPTH = 1
DEEPNORM_ALPHA = (2 * DEPTH) ** 0.25
ADAM_LR = 0.001
ADAM_B1 = 0.9
ADAM_B2 = 0.999
ADAM_EPS = 1e-08
ADAM_WD = 0.01
ADAM_STEP = 10


def _tile(n, cap, unit=LANES):
    best = None
    for t in range(unit, min(n, cap) + 1, unit):
        if n % t == 0:
            best = t
    return best if best is not None else n


def _round_up(n, m):
    return (n + m - 1) // m * m


_HBM = pl.BlockSpec(memory_space=pl.ANY)


def _exchange_sems(n):
    return [pltpu.SemaphoreType.DMA((n, N_DEV - 1)), pltpu.SemaphoreType.DMA((n, N_DEV - 1)),
            pltpu.SemaphoreType.DMA((n,))]


def _exchange_out_shape(arrs, scatter):
    return [jax.ShapeDtypeStruct(a.shape if scatter else (N_DEV,) + a.shape, a.dtype) for a in arrs]


def _exchange_copies(in_refs, out_refs, sems, scatter, phase):
    send_sems, recv_sems, local_sems = sems
    x, y, c = lax.axis_index("x"), lax.axis_index("y"), lax.axis_index("c")
    me = 4 * x + 2 * y + c
    peers = [(x ^ ((k >> 2) & 1), y ^ ((k >> 1) & 1), c ^ (k & 1)) for k in range(N_DEV)]
    row = [4 * p[0] + 2 * p[1] + p[2] for p in peers]
    n = len(in_refs)

    def local(i):
        return pltpu.make_async_copy(in_refs[i].at[me] if scatter else in_refs[i], out_refs[i].at[me], local_sems.at[i])

    def remote(i, k, src, dst_row, to):
        return pltpu.make_async_remote_copy(src_ref=src, dst_ref=out_refs[i].at[dst_row],
                                            send_sem=send_sems.at[i, k - 1], recv_sem=recv_sems.at[i, k - 1],
                                            device_id=to, device_id_type=MESH)

    if scatter:
        sends = [(i, k, in_refs[i].at[row[k]], me, peers[k]) for k in range(1, N_DEV) for i in range(n)]
        passed = []
    else:
        sends = [(i, k, in_refs[i], me, peers[k]) for k in (1, 2, 4, 6) for i in range(n)]
        passed = [(i, k + 1, out_refs[i].at[row[k]], row[k], peers[1]) for k in (2, 4, 6) for i in range(n)]
    arrival = lambda i, k: remote(i, k, in_refs[i].at[me] if scatter else in_refs[i], row[k], peers[k])

    if phase == "start":
        for i in range(n):
            local(i).start()
        for cp in sends:
            remote(*cp).start()
    else:
        for cp in passed:
            arrival(cp[0], cp[1] - 1).wait_recv()
            remote(*cp).start()
        waited = {(cp[0], cp[1] - 1) for cp in passed}
        for k in range(1, N_DEV):
            for i in range(n):
                if (i, k) not in waited:
                    arrival(i, k).wait_recv()
        for cp in sends + passed:
            remote(*cp).wait_send()
        for i in range(n):
            local(i).wait()


def mm(a, b, *, name, ta=False, tb=False, a_act=None, bias=None, out_dtype=F32,
       tm_cap=MM_TM_CAP, tn_cap=MM_TN_CAP, tk_cap=MM_TK_CAP, gather=(), scatter=()):
    a_segs = list(a) if isinstance(a, (list, tuple)) else [a]
    b_segs = list(b) if isinstance(b, (list, tuple)) else [b]
    ns = len(a_segs)
    assert ns == len(b_segs) and (ns == 1 or a_act is None)
    m = a_segs[0].shape[1] if ta else a_segs[0].shape[0]
    n = b_segs[0].shape[0] if tb else b_segs[0].shape[1]
    ks = [x.shape[0] if ta else x.shape[1] for x in a_segs]
    assert ks == [y.shape[1] if tb else y.shape[0] for y in b_segs], (ks, ta, tb)
    tm, tn = _tile(m, tm_cap), _tile(n, tn_cap)
    tks = [_tile(k_, tk_cap) for k_ in ks]
    cnt = [k_ // t_ for k_, t_ in zip(ks, tks)]
    lo = [sum(cnt[:s]) for s in range(ns)]
    nk = sum(cnt)
    grid = (m // tm, n // tn, nk)
    dims = (((0 if ta else 1,), (1 if tb else 0,)), ((), ()))
    xch = list(gather) + list(scatter)
    nx, ng = len(xch), len(gather)
    n_main = 2 * ns + (bias is not None)

    def body(*refs):
        a_refs, b_refs = refs[:ns], refs[ns:2 * ns]
        bias_ref = refs[2 * ns] if bias is not None else None
        x_in, o_ref, x_out = refs[n_main:n_main + nx], refs[n_main + nx], refs[n_main + nx + 1:n_main + 2 * nx + 1]
        rest = refs[n_main + 2 * nx + 1:]
        acc_ref = rest[0] if nk > 1 else None
        sems = rest[1 if nk > 1 else 0:]
        groups = []
        if ng:
            groups.append((x_in[:ng], x_out[:ng], sems[:3], False))
        if nx > ng:
            groups.append((x_in[ng:], x_out[ng:], sems[-3:], True))
        if nx:
            step = (pl.program_id(0) * grid[1] + pl.program_id(1)) * grid[2] + pl.program_id(2)

            @pl.when(step == 0)
            def _():
                for gi, go_, gs, sc in groups:
                    _exchange_copies(gi, go_, gs, sc, "start")
        kk = pl.program_id(2)

        def finish(r):
            if bias is not None:
                r = r + bias_ref[...]
            o_ref[...] = r.astype(o_ref.dtype)

        def segment(s):
            av = a_refs[s][...]
            if a_act == "silu":
                av = jax.nn.silu(av.astype(F32))
            prod = lax.dot_general(av.astype(MXU_DTYPE), b_refs[s][...].astype(MXU_DTYPE), dims,
                                   preferred_element_type=F32)
            if nk == 1:
                finish(prod)
                return
            opens, closes = lo[s] == 0, lo[s] + cnt[s] == nk
            if opens:
                @pl.when(kk == 0)
                def _():
                    acc_ref[...] = prod
            inner = [kk > 0] * opens + [kk < nk - 1] * closes
            if inner:
                @pl.when(functools.reduce(lambda p, q: p & q, inner))
                def _():
                    acc_ref[...] += prod
            else:
                acc_ref[...] += prod
            if closes:
                @pl.when(kk == nk - 1)
                def _():
                    finish(acc_ref[...] + prod)

        for s in range(ns):
            if ns == 1:
                segment(s)
            else:
                pl.when((kk >= lo[s]) & (kk < lo[s] + cnt[s]))(functools.partial(segment, s))

        if nx:
            @pl.when(step == grid[0] * grid[1] * grid[2] - 1)
            def _():
                for gi, go_, gs, sc in groups:
                    _exchange_copies(gi, go_, gs, sc, "wait")

    def seg_index(s):
        return lambda q: jnp.clip(q - lo[s], 0, cnt[s] - 1) if ns > 1 else q

    a_specs, b_specs = [], []
    for s in range(ns):
        qi, tk = seg_index(s), tks[s]
        a_specs.append(pl.BlockSpec((tk, tm), (lambda qi: lambda i, j, q: (qi(q), i))(qi)) if ta
                       else pl.BlockSpec((tm, tk), (lambda qi: lambda i, j, q: (i, qi(q)))(qi)))
        b_specs.append(pl.BlockSpec((tn, tk), (lambda qi: lambda i, j, q: (j, qi(q)))(qi)) if tb
                       else pl.BlockSpec((tk, tn), (lambda qi: lambda i, j, q: (qi(q), j))(qi)))
    in_specs, args = a_specs + b_specs, a_segs + b_segs
    if bias is not None:
        in_specs.append(pl.BlockSpec((1, tn), lambda i, j, q: (0, j)))
        args.append(bias)
    o_spec, o_shape = pl.BlockSpec((tm, tn), lambda i, j, q: (i, j)), jax.ShapeDtypeStruct((m, n), out_dtype)
    acc = [pltpu.VMEM((tm, tn), F32)] if nk > 1 else []
    if not nx:
        return pl.pallas_call(
            body, name=name, grid=grid, in_specs=in_specs, out_specs=o_spec, out_shape=o_shape, scratch_shapes=acc,
            compiler_params=pltpu.CompilerParams(dimension_semantics=("parallel", "parallel", "arbitrary"),
                                                 vmem_limit_bytes=VMEM_LIMIT),
        )(*args)
    return pl.pallas_call(
        body, name=name, grid=grid, in_specs=in_specs + [_HBM] * nx, out_specs=[o_spec] + [_HBM] * nx,
        out_shape=[o_shape] + _exchange_out_shape(list(gather), False) + _exchange_out_shape(list(scatter), True),
        scratch_shapes=acc + (_exchange_sems(ng) if ng else []) + (_exchange_sems(nx - ng) if nx > ng else []),
        compiler_params=pltpu.CompilerParams(dimension_semantics=("arbitrary", "arbitrary", "arbitrary"),
                                             vmem_limit_bytes=VMEM_LIMIT, has_side_effects=True),
    )(*args, *xch)


class In:
    def __init__(self, arr, block, imap, acc=False, grad=True, parts=None, gshape=None, gimap=None):
        self.arr, self.block, self.imap, self.acc, self.grad, self.parts = arr, block, imap, acc, grad, parts
        self.gshape = arr.shape if gshape is None else gshape
        self.gimap = imap if gimap is None else gimap


class Out:
    def __init__(self, shape, block, imap, dtype=F32):
        self.shape, self.block, self.imap, self.dtype = shape, block, imap, dtype


def _load(in_refs, ins):
    vals = []
    for r, i in zip(in_refs, ins):
        if i.parts is None:
            vals.append(r[...])
        else:
            vals.extend(r[p] for p in i.parts)
    return vals


def _stage_params():
    return pltpu.CompilerParams(dimension_semantics=("parallel", "arbitrary"), vmem_limit_bytes=VMEM_LIMIT)


def stage_fwd(name, f, grid, ins, outs, carries=(), transposed=()):
    n_in, n_out, n_c, n_t = len(ins), len(outs), len(carries), len(transposed)

    def body(*refs):
        in_refs, out_refs = refs[:n_in], refs[n_in:n_in + n_out]
        hist_refs = refs[n_in + n_out:n_in + n_out + n_c]
        t_refs = refs[n_in + n_out + n_c:n_in + n_out + n_c + n_t]
        c_refs = refs[n_in + n_out + n_c + n_t:]
        if n_c:
            @pl.when(pl.program_id(1) == 0)
            def _():
                for c in c_refs:
                    c[...] = jnp.zeros_like(c)
        cin = [c[...] for c in c_refs]
        for h, c in zip(hist_refs, cin):
            h[...] = c
        o, cout = f(*_load(in_refs, ins), *cin)
        for r, v in zip(out_refs, o):
            r[...] = v.astype(r.dtype)
        for r, k in zip(t_refs, transposed):
            r[...] = o[k].T.astype(r.dtype)
        for c, v in zip(c_refs, cout):
            c[...] = v

    hist_spec = lambda c: pl.BlockSpec((None, None) + tuple(c), lambda o, s: (o, s) + (0,) * len(c))
    flip = lambda o_: pl.BlockSpec(o_.block[::-1], (lambda im: lambda o, s: im(o, s)[::-1])(o_.imap))
    res = pl.pallas_call(
        body, name=name, grid=grid,
        in_specs=[pl.BlockSpec(i.block, i.imap) for i in ins],
        out_specs=[pl.BlockSpec(o.block, o.imap) for o in outs] + [hist_spec(c) for c in carries]
        + [flip(outs[k]) for k in transposed],
        out_shape=[jax.ShapeDtypeStruct(o.shape, o.dtype) for o in outs]
        + [jax.ShapeDtypeStruct(tuple(grid) + tuple(c), F32) for c in carries]
        + [jax.ShapeDtypeStruct(outs[k].shape[::-1], MXU_DTYPE) for k in transposed],
        scratch_shapes=[pltpu.VMEM(tuple(c), F32) for c in carries],
        compiler_params=_stage_params(),
    )(*[i.arr for i in ins])
    res = list(res)
    if transposed:
        return res[:n_out], res[n_out:n_out + n_c], res[n_out + n_c:]
    return res[:n_out], res[n_out:]


def stage_bwd(name, f, grid, ins, outs, cots, carries=(), hists=(), add_to=None, gdtypes=None):
    n_in, n_out, n_c = len(ins), len(outs), len(carries)
    ns = grid[1]
    add_to = add_to or {}
    gdtypes = gdtypes or {}
    add_idx = sorted(add_to)
    g_idx = [k for k, i in enumerate(ins) if i.grad]
    cots = [c if isinstance(c, (tuple, list)) else (c,) for c in cots]
    n_cot = [len(c) for c in cots]
    rev = lambda imap: (lambda o, s: imap(o, ns - 1 - s))

    def body(*refs):
        p = 0
        in_refs = refs[p:p + n_in]; p += n_in
        cot_refs = []
        for cnt in n_cot:
            cot_refs.append(refs[p:p + cnt]); p += cnt
        hist_refs = refs[p:p + n_c]; p += n_c
        add_refs = refs[p:p + len(add_idx)]; p += len(add_idx)
        g_refs = refs[p:p + len(g_idx)]; p += len(g_idx)
        dc_refs = refs[p:]
        first = pl.program_id(1) == 0
        if n_c:
            @pl.when(first)
            def _():
                for c in dc_refs:
                    c[...] = jnp.zeros_like(c)
        vals = _load(in_refs, ins)
        cin = [h[...] for h in hist_refs]
        (o, cout), vjp = jax.vjp(lambda *a: f(*a), *vals, *cin)
        cot_o = []
        for crs, v in zip(cot_refs, o):
            c = crs[0][...].astype(v.dtype)
            for extra in crs[1:]:
                c = c + extra[...].astype(v.dtype)
            cot_o.append(c)
        cot_c = tuple(c[...] for c in dc_refs)
        grads = vjp((tuple(cot_o), cot_c))
        pos, per_in = 0, []
        for i in ins:
            cnt = 1 if i.parts is None else len(i.parts)
            per_in.append(grads[pos:pos + cnt])
            pos += cnt
        dcin = grads[pos:]
        for gr, k in zip(g_refs, g_idx):
            i, gs = ins[k], per_in[k]
            if i.acc:
                @pl.when(first)
                def _(gr=gr):
                    gr[...] = jnp.zeros_like(gr)
                if i.parts is None:
                    gr[...] += gs[0].astype(gr.dtype)
                else:
                    for pt, g in zip(i.parts, gs):
                        gr[pt] += g.astype(gr.dtype)
            else:
                g = gs[0]
                if k in add_to:
                    g = g + add_refs[add_idx.index(k)][...].astype(g.dtype)
                gr[...] = g.astype(gr.dtype)
        for c, v in zip(dc_refs, dcin):
            c[...] = v

    in_specs = [pl.BlockSpec(i.block, rev(i.imap)) for i in ins]
    for o_, cnt in zip(outs, n_cot):
        in_specs += [pl.BlockSpec(o_.block, rev(o_.imap))] * cnt
    in_specs += [pl.BlockSpec((None, None) + tuple(c), (lambda c: (lambda o, s: (o, ns - 1 - s) + (0,) * len(c)))(c))
                 for c in carries]
    in_specs += [pl.BlockSpec(ins[k].block, rev(ins[k].gimap)) for k in add_idx]
    out_specs, out_shape = [], []
    for k in g_idx:
        i = ins[k]
        if i.acc:
            out_specs.append(pl.BlockSpec(i.block, (lambda im: (lambda o, s: im(o, 0)))(i.imap)))
        else:
            out_specs.append(pl.BlockSpec(i.block, rev(i.gimap)))
        out_shape.append(jax.ShapeDtypeStruct(i.gshape, gdtypes.get(k, F32)))
    res = pl.pallas_call(
        body, name=name, grid=grid, in_specs=in_specs, out_specs=out_specs, out_shape=out_shape,
        scratch_shapes=[pltpu.VMEM(tuple(c), F32) for c in carries],
        compiler_params=_stage_params(),
    )(*[i.arr for i in ins], *[a for c in cots for a in c], *hists, *[add_to[k] for k in add_idx])
    return list(res)


def _iota_rows(shape):
    return lax.broadcasted_iota(jnp.int32, shape, 0)


@functools.partial(jax.custom_vjp, nondiff_argnums=(1,))
def _roll_rows(x, s):
    return pltpu.roll(x, s % x.shape[0], 0)


def _roll_rows_fwd(x, s):
    return _roll_rows(x, s), None


def _roll_rows_bwd(s, _, g):
    return (_roll_rows(g, -s),)


_roll_rows.defvjp(_roll_rows_fwd, _roll_rows_bwd)


@jax.custom_vjp
def _drop_head(xx):
    return xx[SUBLANES:]


def _drop_head_fwd(xx):
    return xx[SUBLANES:], None


def _drop_head_bwd(_, g):
    return (jnp.concatenate([jnp.zeros((SUBLANES, g.shape[1]), g.dtype), g], axis=0),)


_drop_head.defvjp(_drop_head_fwd, _drop_head_bwd)


@jax.custom_vjp
def _last_rows(x):
    return x[x.shape[0] - SUBLANES:]


def _last_rows_fwd(x):
    return x[x.shape[0] - SUBLANES:], x.shape[0]


def _last_rows_bwd(n, g):
    return (jnp.concatenate([jnp.zeros((n - SUBLANES, g.shape[1]), g.dtype), g], axis=0),)


_last_rows.defvjp(_last_rows_fwd, _last_rows_bwd)


def _last_row(x):
    n = x.shape[0]
    return jnp.sum(jnp.where(_iota_rows(x.shape) == n - 1, x, 0.0), axis=0, keepdims=True)


def _scan_steps(n):
    s = 1
    while s < n:
        yield s
        s *= 2


def _block_scan_impl(a, u, h0):
    n = a.shape[0]
    row = _iota_rows(a.shape)
    for s in _scan_steps(n):
        keep = row >= s
        a_s = jnp.where(keep, pltpu.roll(a, s, 0), 1.0)
        u_s = jnp.where(keep, pltpu.roll(u, s, 0), 0.0)
        u = u + a * u_s
        a = a * a_s
    return u + a * h0


@jax.custom_vjp
def _block_scan(a, u, h0):
    return _block_scan_impl(a, u, h0)


def _block_scan_fwd(a, u, h0):
    h = _block_scan_impl(a, u, h0)
    return h, (a, h, h0)


def _block_scan_bwd(res, dh):
    a, h, h0 = res
    n = a.shape[0]
    row = _iota_rows(a.shape)
    b = jnp.where(row < n - 1, pltpu.roll(a, n - 1, 0), 0.0)
    lam = dh
    for s in _scan_steps(n):
        keep = row < n - s
        b_s = jnp.where(keep, pltpu.roll(b, n - s, 0), 1.0)
        l_s = jnp.where(keep, pltpu.roll(lam, n - s, 0), 0.0)
        lam = lam + b * l_s
        b = b * b_s
    h_prev = jnp.where(row >= 1, pltpu.roll(h, 1, 0), jnp.broadcast_to(h0, h.shape))
    d_h0 = jnp.sum(jnp.where(row == 0, a * lam, 0.0), axis=0, keepdims=True)
    return lam * h_prev, lam, d_h0


_block_scan.defvjp(_block_scan_fwd, _block_scan_bwd)


def _dot_hi(a, b, dims=(((1,), (0,)), ((), ()))):
    return lax.dot_general(a, b, dims, precision=HI, preferred_element_type=F32)


_NN, _NT, _TN = "nn", "nt", "tn"
_CONTRACT = {_NN: (1, 0), _NT: (1, 1), _TN: (0, 0)}


def _raw_dot(a, b, kind):
    ca, cb = _CONTRACT[kind]
    lead = a.ndim - 2
    dims = (((ca + lead,), (cb + lead,)), (tuple(range(lead)), tuple(range(lead))))
    return lax.dot_general(a.astype(DN_DTYPE), b.astype(DN_DTYPE), dims, preferred_element_type=F32)


@jax.custom_vjp
def _nn(a, b):
    return _raw_dot(a, b, _NN)


_nn.defvjp(lambda a, b: (_raw_dot(a, b, _NN), (a, b)),
           lambda r, g: (_raw_dot(g, r[1], _NT), _raw_dot(r[0], g, _TN)))


@jax.custom_vjp
def _nt(a, b):
    return _raw_dot(a, b, _NT)


_nt.defvjp(lambda a, b: (_raw_dot(a, b, _NT), (a, b)),
           lambda r, g: (_raw_dot(g, r[1], _NN), _raw_dot(g, r[0], _TN)))


@jax.custom_vjp
def _tn(a, b):
    return _raw_dot(a, b, _TN)


_tn.defvjp(lambda a, b: (_raw_dot(a, b, _TN), (a, b)),
           lambda r, g: (_raw_dot(r[1], g, _NT), _raw_dot(r[0], g, _NN)))


def _neumann_inverse(a):
    n = a.shape[-1]
    eye = (lax.broadcasted_iota(jnp.int32, (n, n), 0) == lax.broadcasted_iota(jnp.int32, (n, n), 1)).astype(F32)
    p = _raw_dot(a, a, _NN)
    e = p
    for _ in range(int(math.log2(n)) - 2):
        p = _raw_dot(p, p, _NN)
        e = e + p + _raw_dot(e, p, _NN)
    return eye - a + e - _raw_dot(a, e, _NN)


@jax.custom_vjp
def _unit_lower_inverse(a):
    return _neumann_inverse(a)


def _unit_lower_inverse_fwd(a):
    x = _neumann_inverse(a)
    return x, x


def _unit_lower_inverse_bwd(x, g):
    return (-_raw_dot(_raw_dot(x, g, _TN), x, _NT),)


_unit_lower_inverse.defvjp(_unit_lower_inverse_fwd, _unit_lower_inverse_bwd)


def _softplus(x):
    return jnp.maximum(x, 0.0) + jnp.log1p(jnp.exp(-jnp.abs(x)))


def _neg_expm1(x):
    series = -x * (1.0 + x * (0.5 + x * (1.0 / 6.0 + x * (1.0 / 24.0 + x * (1.0 / 120.0)))))
    return jnp.where(x > -0.03, series, 1.0 - jnp.exp(x))


def f_modulate(x, sc, sh):
    return (x * (1.0 + sc) + sh,), ()


def _deepnorm(x, y, gt, g, b):
    v = DEEPNORM_ALPHA * x + (1.0 + gt) * y
    mu = jnp.mean(v, axis=-1, keepdims=True)
    vc = v - mu
    var = jnp.mean(vc * vc, axis=-1, keepdims=True)
    return vc * lax.rsqrt(var + LN_EPS) * g + b


def f_deepnorm_mod(x, y, gt, g, b, sc, sh):
    x1 = _deepnorm(x, y, gt, g, b)
    return (x1, x1 * (1.0 + sc) + sh), ()


def f_deepnorm_loss(x, y, gt, g, b, target):
    err = _deepnorm(x, y, gt, g, b) - target
    return (0.5 * jnp.mean(err * err, axis=-1, keepdims=True),), ()


def _causal_conv(x, prev, ws):
    xx = jnp.concatenate([prev, x], axis=0)
    k = len(ws)
    y = ws[k - 1] * x
    for j in range(k - 1):
        y = y + ws[j] * _drop_head(_roll_rows(xx, k - 1 - j))
    return y


def f_rg_conv(x, w0, w1, w2, w3, b, prev):
    return (_causal_conv(x, prev, (w0, w1, w2, w3)) + b,), (_last_rows(x),)


def f_dn_conv(x, w0, w1, w2, w3, prev):
    return (jax.nn.silu(_causal_conv(x, prev, (w0, w1, w2, w3))),), (_last_rows(x),)


def f_ffn_act(gp, up, w0, w1, w2, b, prev):
    return (jax.nn.gelu(_causal_conv(gp, prev, (w0, w1, w2)) + b) * up,), (_last_rows(gp),)


def f_rglru(xc, pre_r, pre_i, gr, b_a, b_x, lam, h0):
    gate_r = jax.nn.sigmoid(pre_r + b_a)
    gate_i = jax.nn.sigmoid(pre_i + b_x)
    log_a = -RG_C * gate_r * _softplus(-lam)
    a = jnp.exp(log_a)
    mult = jnp.sqrt(_neg_expm1(2.0 * log_a))
    h = _block_scan(a, mult * gate_i * xc, h0)
    return (h * jax.nn.gelu(gr),), (_last_row(h),)


def f_dn_conv_norm(scale, x, w0, w1, w2, w3, prev):
    y = jax.nn.silu(_causal_conv(x, prev, (w0, w1, w2, w3)))
    return (y * lax.rsqrt(jnp.sum(y * y, axis=-1, keepdims=True) + L2_EPS) * scale,), (_last_rows(x),)


def f_dn_gates(a_in, b_in, a_log, dt_bias):
    g = -jnp.exp(a_log) * _softplus(a_in + dt_bias)
    n = g.shape[0]
    shift = int(math.log2(DN_CHUNK))
    ri = lax.broadcasted_iota(jnp.int32, (n, n), 0)
    ci = lax.broadcasted_iota(jnp.int32, (n, n), 1)
    tri = ((lax.shift_right_logical(ri, shift) == lax.shift_right_logical(ci, shift)) & (ri >= ci)).astype(F32)
    return (_dot_hi(tri, g), jax.nn.sigmoid(b_in)), ()


def f_dn_out(o, z, nw):
    r = lax.rsqrt(jnp.mean(o * o, axis=-1, keepdims=True) + RMS_EPS)
    return (o * r * nw * jax.nn.silu(z),), ()


def f_merge(ga, gb, ya, yb):
    return (jax.nn.sigmoid(ga) * ya + jax.nn.sigmoid(gb) * yb,), ()


def _delta_intra(q, k, v, g_i, g_j, beta):
    c = q.shape[-2]
    ri = lax.broadcasted_iota(jnp.int32, (c, c), 0)
    ci = lax.broadcasted_iota(jnp.int32, (c, c), 1)
    decay = jnp.exp(jnp.where(ri >= ci, g_i - g_j, -jnp.inf))
    g_last = jnp.sum(jnp.where(_iota_rows((c, 1)) == c - 1, g_i, 0.0), axis=-2, keepdims=True)
    exp_g = jnp.exp(g_i)
    kb = k * beta
    t_inv = _unit_lower_inverse(jnp.where(ri > ci, _nt(kb, k) * decay, 0.0))
    u = _nn(t_inv, v * beta)
    w = _nn(t_inv, kb * exp_g)
    return u, w, _nt(q, k) * decay, q * exp_g, k * jnp.exp(g_last - g_i)


def _delta_inter(u, w, qk, q_dec, k_dec, g_last, state):
    v_new = u - _nn(w, state)
    o = _nn(q_dec, state) + _nn(qk, v_new)
    return o, jnp.exp(g_last) * state + _tn(k_dec, v_new)


def _chunk_spec(width, nc=None, col=0):
    if nc is None:
        return pl.BlockSpec((DN_CHUNK, width), lambda s: (s, col))
    return pl.BlockSpec((DN_CHUNK, width), lambda s: (nc - 1 - s, col))


def _delta_params(sem):
    return pltpu.CompilerParams(dimension_semantics=(sem,), vmem_limit_bytes=VMEM_LIMIT)


def _head(ref, h, width=LANES):
    return ref[:, h * LANES:h * LANES + width]


def _head_groups(n_vh):
    hb = min(DN_HEAD_GROUP, n_vh)
    return [range(h0, h0 + hb) for h0 in range(0, n_vh, hb)]


def _stack(hs, f):
    return jnp.stack([f(h) for h in hs])


def _intra_operands(hs, rep, q_ref, k_ref, v_ref, g_ref, gt_ref, b_ref):
    return (_stack(hs, lambda h: _head(q_ref, h // rep)), _stack(hs, lambda h: _head(k_ref, h // rep)),
            _stack(hs, lambda h: _head(v_ref, h)), _stack(hs, lambda h: g_ref[:, h:h + 1]),
            _stack(hs, lambda h: gt_ref[h:h + 1, :]), _stack(hs, lambda h: b_ref[:, h:h + 1]))


def _rows(ci):
    return slice(ci * DN_CHUNK, (ci + 1) * DN_CHUNK)


def _inter_operands(hs, ci, u_ref, w_ref, qk_ref, qd_ref, kd_ref, g_ref):
    f32 = lambda ref, width=LANES: _stack(hs, lambda h: ref[_rows(ci), h * LANES:h * LANES + width].astype(F32))
    last = (ci + 1) * DN_CHUNK - 1
    return (f32(u_ref), f32(w_ref), f32(qk_ref, DN_CHUNK), f32(qd_ref), f32(kd_ref),
            _stack(hs, lambda h: g_ref[last:last + 1, h:h + 1]))


def _inter_spec(width, steps, reverse=False):
    rows = DN_INTER_CHUNKS * DN_CHUNK
    return pl.BlockSpec((rows, width), (lambda s: (steps - 1 - s, 0)) if reverse else (lambda s: (s, 0)))


def delta_intra_fwd(qn, kn, qkv, v_blk, big_g, big_gt, beta, n_vh):
    t, qk_w = qn.shape
    vdim = n_vh * LANES
    rep = vdim // qk_w
    nc = t // DN_CHUNK

    def body(q_ref, k_ref, v_ref, g_ref, gt_ref, b_ref, u_ref, w_ref, qk_ref, qd_ref, kd_ref):
        for hs in _head_groups(n_vh):
            u, w, qk, qd, kd = _delta_intra(*_intra_operands(hs, rep, q_ref, k_ref, v_ref, g_ref, gt_ref, b_ref))
            for i, h in enumerate(hs):
                sl = slice(h * LANES, (h + 1) * LANES)
                u_ref[:, sl] = u[i]
                w_ref[:, sl] = w[i].astype(w_ref.dtype)
                qk_ref[:, sl] = jnp.concatenate([qk[i], jnp.zeros_like(qk[i])], axis=1).astype(qk_ref.dtype)
                qd_ref[:, sl] = qd[i].astype(qd_ref.dtype)
                kd_ref[:, sl] = kd[i].astype(kd_ref.dtype)

    return pl.pallas_call(
        body, name="delta_intra_fwd", grid=(nc,),
        in_specs=[_chunk_spec(qk_w), _chunk_spec(qk_w), _chunk_spec(vdim, col=v_blk), _chunk_spec(LANES),
                  pl.BlockSpec((None, n_vh, DN_CHUNK), lambda s: (s, 0, 0)), _chunk_spec(LANES)],
        out_specs=[_chunk_spec(vdim)] * 5,
        out_shape=[jax.ShapeDtypeStruct((t, vdim), F32)] + [jax.ShapeDtypeStruct((t, vdim), DN_DTYPE)] * 4,
        compiler_params=_delta_params("parallel"),
    )(qn, kn, qkv, big_g, big_gt, beta)


def delta_inter_fwd(u, w, qk, q_dec, k_dec, big_g, n_vh):
    t, vdim = u.shape
    nc = t // DN_CHUNK

    cpb = DN_INTER_CHUNKS
    steps = nc // cpb

    def body(u_ref, w_ref, qk_ref, qd_ref, kd_ref, g_ref, o_ref, hist_ref, s_ref):
        @pl.when(pl.program_id(0) == 0)
        def _():
            s_ref[...] = jnp.zeros_like(s_ref)
        for ci in range(cpb):
            for hs in _head_groups(n_vh):
                grp = slice(hs[0], hs[-1] + 1)
                st = s_ref[grp]
                hist_ref[ci, grp] = st
                o, ns = _delta_inter(*_inter_operands(hs, ci, u_ref, w_ref, qk_ref, qd_ref, kd_ref, g_ref), st)
                for i, h in enumerate(hs):
                    o_ref[_rows(ci), h * LANES:(h + 1) * LANES] = o[i]
                s_ref[grp] = ns

    return pl.pallas_call(
        body, name="delta_inter_fwd", grid=(steps,),
        in_specs=[_inter_spec(vdim, steps)] * 5 + [_inter_spec(LANES, steps)],
        out_specs=[_inter_spec(vdim, steps), pl.BlockSpec((cpb, n_vh, LANES, LANES), lambda s: (s, 0, 0, 0))],
        out_shape=[jax.ShapeDtypeStruct((t, vdim), F32), jax.ShapeDtypeStruct((nc, n_vh, LANES, LANES), F32)],
        scratch_shapes=[pltpu.VMEM((n_vh, LANES, LANES), F32)],
        compiler_params=_delta_params("arbitrary"),
    )(u, w, qk, q_dec, k_dec, big_g)


def delta_inter_bwd(u, w, qk, q_dec, k_dec, big_g, hist, d_o, n_vh):
    t, vdim = u.shape
    nc = t // DN_CHUNK

    cpb = DN_INTER_CHUNKS
    steps = nc // cpb

    def body(u_ref, w_ref, qk_ref, qd_ref, kd_ref, g_ref, hist_ref, do_ref,
             du_ref, dw_ref, dqk_ref, dqd_ref, dkd_ref, dg_ref, ds_ref):
        @pl.when(pl.program_id(0) == 0)
        def _():
            ds_ref[...] = jnp.zeros_like(ds_ref)
        lane = lax.broadcasted_iota(jnp.int32, (1, LANES), 1)
        last = _iota_rows((DN_CHUNK, LANES)) == DN_CHUNK - 1
        for ci in reversed(range(cpb)):
            dgl_all = jnp.zeros((1, LANES), F32)
            for hs in _head_groups(n_vh):
                grp = slice(hs[0], hs[-1] + 1)
                prim = _inter_operands(hs, ci, u_ref, w_ref, qk_ref, qd_ref, kd_ref, g_ref) + (hist_ref[ci, grp],)
                _, vjp = jax.vjp(_delta_inter, *prim)
                cot_o = _stack(hs, lambda h: do_ref[_rows(ci), h * LANES:(h + 1) * LANES])
                du, dw, dqk, dqd, dkd, dgl, dst = vjp((cot_o, ds_ref[grp]))
                ds_ref[grp] = dst
                for i, h in enumerate(hs):
                    sl = slice(h * LANES, (h + 1) * LANES)
                    du_ref[_rows(ci), sl] = du[i]
                    dw_ref[_rows(ci), sl] = dw[i]
                    dqk_ref[_rows(ci), sl] = jnp.concatenate([dqk[i], jnp.zeros_like(dqk[i])], axis=1)
                    dqd_ref[_rows(ci), sl] = dqd[i]
                    dkd_ref[_rows(ci), sl] = dkd[i]
                    dgl_all = dgl_all + dgl[i] * (lane == h).astype(F32)
            dg_ref[_rows(ci), :] = jnp.where(last, jnp.broadcast_to(dgl_all, (DN_CHUNK, LANES)), 0.0)

    rv = lambda w_: _inter_spec(w_, steps, reverse=True)
    return pl.pallas_call(
        body, name="delta_inter_bwd", grid=(steps,),
        in_specs=[rv(vdim)] * 5 + [rv(LANES), pl.BlockSpec((cpb, n_vh, LANES, LANES), lambda s: (steps - 1 - s, 0, 0, 0)),
                                   rv(vdim)],
        out_specs=[rv(vdim)] * 5 + [rv(LANES)],
        out_shape=[jax.ShapeDtypeStruct((t, vdim), F32)] * 5 + [jax.ShapeDtypeStruct((t, LANES), F32)],
        scratch_shapes=[pltpu.VMEM((n_vh, LANES, LANES), F32)],
        compiler_params=_delta_params("arbitrary"),
    )(u, w, qk, q_dec, k_dec, big_g, hist, d_o)


def delta_intra_bwd(qn, kn, qkv, v_blk, big_g, big_gt, beta, cots, n_vh):
    t, qk_w = qn.shape
    vdim = n_vh * LANES
    rep = vdim // qk_w
    nc = t // DN_CHUNK

    def body(q_ref, k_ref, v_ref, g_ref, gt_ref, b_ref, du_ref, dw_ref, dqk_ref, dqd_ref, dkd_ref,
             dq_ref, dk_ref, dv_ref, dg_ref, dgt_ref, db_ref):
        lane = lax.broadcasted_iota(jnp.int32, (1, LANES), 1)
        dg_all = jnp.zeros((DN_CHUNK, LANES), F32)
        db_all = jnp.zeros((DN_CHUNK, LANES), F32)
        dq_acc, dk_acc = None, None
        for hs in _head_groups(n_vh):
            _, vjp = jax.vjp(_delta_intra, *_intra_operands(hs, rep, q_ref, k_ref, v_ref, g_ref, gt_ref, b_ref))
            cot = lambda ref, width=LANES: _stack(hs, lambda h: _head(ref, h, width))
            dq, dk, dv, dgi, dgj, db = vjp((cot(du_ref), cot(dw_ref), cot(dqk_ref, DN_CHUNK), cot(dqd_ref), cot(dkd_ref)))
            for i, h in enumerate(hs):
                j = h // rep
                dv_ref[:, h * LANES:(h + 1) * LANES] = dv[i]
                dgt_ref[h:h + 1, :] = dgj[i]
                onehot = (lane == h).astype(F32)
                dg_all = dg_all + dgi[i] * onehot
                db_all = db_all + db[i] * onehot
                dq_acc = dq[i] if h % rep == 0 else dq_acc + dq[i]
                dk_acc = dk[i] if h % rep == 0 else dk_acc + dk[i]
                if h % rep == rep - 1:
                    dq_ref[:, j * LANES:(j + 1) * LANES] = dq_acc
                    dk_ref[:, j * LANES:(j + 1) * LANES] = dk_acc
        dg_ref[...] = dg_all
        db_ref[...] = db_all

    gt_spec = pl.BlockSpec((None, n_vh, DN_CHUNK), lambda s: (s, 0, 0))
    return pl.pallas_call(
        body, name="delta_intra_bwd", grid=(nc,),
        in_specs=[_chunk_spec(qk_w), _chunk_spec(qk_w), _chunk_spec(vdim, col=v_blk), _chunk_spec(LANES), gt_spec,
                  _chunk_spec(LANES)] + [_chunk_spec(vdim)] * 5,
        out_specs=[_chunk_spec(qk_w), _chunk_spec(qk_w), _chunk_spec(vdim), _chunk_spec(LANES), gt_spec,
                   _chunk_spec(LANES)],
        out_shape=[jax.ShapeDtypeStruct((t, qk_w), F32), jax.ShapeDtypeStruct((t, qk_w), F32),
                   jax.ShapeDtypeStruct((t, vdim), F32), jax.ShapeDtypeStruct((t, LANES), F32),
                   jax.ShapeDtypeStruct((nc, n_vh, DN_CHUNK), F32), jax.ShapeDtypeStruct((t, LANES), F32)],
        compiler_params=_delta_params("parallel"),
    )(qn, kn, qkv, big_g, big_gt, beta, *cots)


def all_gather(name, arrs):
    n = len(arrs)

    def body(*refs):
        in_refs, out_refs, sems = refs[:n], refs[n:2 * n], refs[2 * n:]
        _exchange_copies(in_refs, out_refs, sems, False, "start")
        _exchange_copies(in_refs, out_refs, sems, False, "wait")

    res = pl.pallas_call(
        body, name=name,
        in_specs=[_HBM] * n, out_specs=[_HBM] * n,
        out_shape=_exchange_out_shape(arrs, False), scratch_shapes=_exchange_sems(n),
        compiler_params=pltpu.CompilerParams(has_side_effects=True),
    )(*arrs)
    return list(res)


def _adamw_math(w, g, m, v):
    m = ADAM_B1 * m + (1.0 - ADAM_B1) * g
    v = ADAM_B2 * v + (1.0 - ADAM_B2) * (g * g)
    m_hat = m / (1.0 - ADAM_B1 ** ADAM_STEP)
    v_hat = v / (1.0 - ADAM_B2 ** ADAM_STEP)
    delta = -ADAM_LR * (m_hat / (jnp.sqrt(v_hat) + ADAM_EPS) + ADAM_WD * w)
    return delta, m, v


def adamw(name, w, parts, m, v, rows_cap=128):
    r, c = w.shape
    np_ = parts.shape[0]
    tr = _tile(r, rows_cap, SUBLANES * (4 // parts.dtype.itemsize))

    def body(w_ref, p_ref, m_ref, v_ref, g_ref, d_ref, nm_ref, nv_ref):
        g = p_ref[0].astype(F32)
        for k in range(1, np_):
            g = g + p_ref[k].astype(F32)
        delta, nm, nv = _adamw_math(w_ref[...], g, m_ref[...], v_ref[...])
        g_ref[...] = g
        d_ref[...] = delta
        nm_ref[...] = nm
        nv_ref[...] = nv

    spec = pl.BlockSpec((tr, c), lambda i: (i, 0))
    return pl.pallas_call(
        body, name=name, grid=(r // tr,),
        in_specs=[spec, pl.BlockSpec((np_, tr, c), lambda i: (0, i, 0)), spec, spec],
        out_specs=[spec] * 4, out_shape=[jax.ShapeDtypeStruct((r, c), F32)] * 4,
        compiler_params=pltpu.CompilerParams(dimension_semantics=("parallel",), vmem_limit_bytes=VMEM_LIMIT),
    )(w, parts, m, v)


def sum_parts(name, parts, rows_cap=256):
    np_, r, c = parts.shape
    tr = _tile(r, rows_cap, SUBLANES)

    def body(p_ref, o_ref):
        g = p_ref[0].astype(F32)
        for k in range(1, np_):
            g = g + p_ref[k].astype(F32)
        o_ref[...] = g

    return pl.pallas_call(
        body, name=name, grid=(r // tr,),
        in_specs=[pl.BlockSpec((np_, tr, c), lambda i: (0, i, 0))],
        out_specs=pl.BlockSpec((tr, c), lambda i: (i, 0)),
        out_shape=jax.ShapeDtypeStruct((r, c), F32),
        compiler_params=pltpu.CompilerParams(dimension_semantics=("parallel",), vmem_limit_bytes=VMEM_LIMIT),
    )(parts)


def _pack(arrs):
    flat = jnp.concatenate([a.reshape(-1).astype(F32) for a in arrs])
    n = flat.shape[0]
    return jnp.pad(flat, (0, _round_up(n, LANES * SUBLANES) - n)).reshape(-1, LANES)


def _unpack(packed, like):
    flat, out, pos = packed.reshape(-1), [], 0
    for a in like:
        out.append(flat[pos:pos + a.size].reshape(a.shape))
        pos += a.size
    return out


def kernel(x, c, w_ada, b_ada, w_in, rg_conv_w, rg_conv_b, rg_w_a, rg_b_a, rg_w_x, rg_b_x, rg_lambda, dn_conv_w, dn_a_log, dn_dt_bias, dn_norm_w, w_proj_a, w_proj_b, w_out, ln1_g, ln1_b, ffn_w_gate, ffn_w_up, ffn_conv_w, ffn_conv_b, ffn_w_down, ln2_g, ln2_b, loss_target, m_w_ada, m_b_ada, m_w_in, m_rg_conv_w, m_rg_conv_b, m_rg_w_a, m_rg_b_a, m_rg_w_x, m_rg_b_x, m_rg_lambda, m_dn_conv_w, m_dn_a_log, m_dn_dt_bias, m_dn_norm_w, m_w_proj_a, m_w_proj_b, m_w_out, m_ln1_g, m_ln1_b, m_ffn_w_gate, m_ffn_w_up, m_ffn_conv_w, m_ffn_conv_b, m_ffn_w_down, m_ln2_g, m_ln2_b, v_w_ada, v_b_ada, v_w_in, v_rg_conv_w, v_rg_conv_b, v_rg_w_a, v_rg_b_a, v_rg_w_x, v_rg_b_x, v_rg_lambda, v_dn_conv_w, v_dn_a_log, v_dn_dt_bias, v_dn_norm_w, v_w_proj_a, v_w_proj_b, v_w_out, v_ln1_g, v_ln1_b, v_ffn_w_gate, v_ffn_w_up, v_ffn_conv_w, v_ffn_conv_b, v_ffn_w_down, v_ln2_g, v_ln2_b):
    names = ['w_ada', 'b_ada', 'w_in', 'rg_conv_w', 'rg_conv_b', 'rg_w_a', 'rg_b_a', 'rg_w_x', 'rg_b_x', 'rg_lambda',
             'dn_conv_w', 'dn_a_log', 'dn_dt_bias', 'dn_norm_w', 'w_proj_a', 'w_proj_b', 'w_out', 'ln1_g', 'ln1_b',
             'ffn_w_gate', 'ffn_w_up', 'ffn_conv_w', 'ffn_conv_b', 'ffn_w_down', 'ln2_g', 'ln2_b']
    loc = locals()
    W = {n: loc[n][0] for n in names}
    M = {n: loc['m_' + n][0] for n in names}
    V = {n: loc['v_' + n][0] for n in names}

    me = 4 * lax.axis_index("x") + 2 * lax.axis_index("y") + lax.axis_index("c")
    xs, tgt = x[0], loss_target[0]
    t, d = xs.shape
    d_rnn = W['rg_conv_b'].shape[0]
    n_blk = W['rg_w_a'].shape[0]
    n_vh = W['dn_a_log'].shape[0]
    assert W['dn_norm_w'].shape[0] == LANES
    vdim = n_vh * LANES
    d_ff = W['ffn_conv_b'].shape[0]
    d_in = W['w_in'].shape[1] * N_DEV
    qk = (d_in - 2 * d_rnn - 2 * vdim - 2 * n_vh - 2 * d) // 2
    assert vdim == 2 * qk and qk % LANES == 0 and n_vh <= LANES
    splits = (d_rnn, d_rnn, qk, qk, vdim, vdim, n_vh, n_vh, d, d)
    offs = [0]
    for s_ in splits:
        offs.append(offs[-1] + s_)

    tb = _tile(t, 256, SUBLANES)

    big = ['w_in', 'w_proj_a', 'w_proj_b', 'w_out', 'ffn_w_gate', 'ffn_w_up', 'ffn_w_down']
    small_sh = ['rg_conv_w', 'dn_conv_w', 'ffn_conv_w']
    first = all_gather("gather_first", [W['w_in'].astype(WIRE_DTYPE)] + [W[n] for n in small_sh] + [c])
    g_in, g_rcw, g_dcw, g_fcw, c_all = first
    cols = lambda g: jnp.transpose(g, (1, 0, 2)).reshape(g.shape[1], -1)
    rows = lambda g: g.reshape(-1, g.shape[2])
    w_in_f = cols(g_in)
    padl = lambda a: jnp.pad(a, ((0, 0), (0, LANES - a.shape[1])))
    groups = [w_in_f[:, offs[i]:offs[i + 1]] for i in range(10)]
    groups[6], groups[7] = padl(groups[6]), padl(groups[7])
    go = [0]
    for g_ in groups:
        go.append(go[-1] + g_.shape[1])
    n_pad = _round_up(go[-1], 512)
    wp = jnp.pad(jnp.concatenate(groups, axis=1), ((0, 0), (0, n_pad - go[-1])))
    o_xr, o_gr, o_q, o_k, o_v, o_z, o_a, o_b, o_ga, o_gb = go[:10]
    rcw, dcw, fcw = cols(g_rcw), cols(g_dcw), cols(g_fcw)
    eye_b = jnp.eye(n_blk, dtype=F32)
    bd = lambda w: (w[:, :, None, :] * eye_b[:, None, :, None]).reshape(d_rnn, d_rnn)
    w_bd = jnp.concatenate([bd(W['rg_w_a']), bd(W['rg_w_x'])], axis=1)
    row1 = lambda a: a.reshape(1, -1)
    padv = lambda a: jnp.pad(row1(a), ((0, 0), (0, LANES - a.shape[0])))
    nw_t = jnp.tile(row1(W['dn_norm_w']), (1, n_vh))

    c_pad =jnp.pad(c_all.reshape(N_DEV, d), ((0, LANES - N_DEV), (0, 0)))
    ada_w = W['w_ada'].shape[1]
    b_ada_me = lax.dynamic_slice(W['b_ada'], (me * ada_w,), (ada_w,)).reshape(1, ada_w)
    ada_sh = mm(c_pad, W['w_ada'], name="ada_fwd", a_act="silu", bias=b_ada_me)
    (ada_all,) = all_gather("gather_ada", [ada_sh[:N_DEV]])
    ada_me = lax.dynamic_slice(ada_all, (0, me, 0), (N_DEV, 1, ada_w)).reshape(6, 1, d)
    sh1, sc1, gt1, sh2, sc2, gt2 = [ada_me[i] for i in range(6)]

    nt = t // tb

    def act(a, bw, col0=0, width=None, grad=True, rows=tb):
        width = a.shape[1] if width is None else width
        assert col0 % bw == 0 and width % bw == 0
        c0 = col0 // bw
        return In(a, (rows, bw), lambda o, s: (s, c0 + o), grad=grad, gshape=(t, width), gimap=lambda o, s: (s, o))

    def prm(a, bw, parts=None):
        return In(a, (a.shape[0], bw), lambda o, s: (0, o), acc=True, parts=parts)

    def out(width, bw, rows=tb):
        return Out((t, width), (rows, bw), lambda o, s: (s, o))

    tbh = _tile(t, 2048, SUBLANES)
    nth = t // tbh
    tbc = _tile(t, 1024, SUBLANES)
    ntc = t // tbc

    krows = lambda k_: [(slice(j, j + 1), slice(None)) for j in range(k_)]

    mod1_ins = [act(xs, d), prm(sc1, d), prm(sh1, d)]
    (h1,), _, (h1_t,) = stage_fwd("mod1_fwd", f_modulate, (1, nt), mod1_ins, [out(d, d)], transposed=[0])
    proj, g_pa, g_pb, g_out, g_fg, g_fu, g_fd = mm(h1, wp, name="proj_fwd",
                                                   gather=[W[n].astype(WIRE_DTYPE) for n in big[1:]])
    w_pa, w_pb, w_o, w_fd = rows(g_pa), rows(g_pb), rows(g_out), rows(g_fd)
    w_gate, w_up = cols(g_fg), cols(g_fu)
    w_gu = jnp.concatenate([w_gate, w_up], axis=1)

    cb_r = _tile(math.gcd(d_rnn, o_gr), 256)
    rgc_ins = [act(proj, cb_r, o_xr, d_rnn, rows=tbc), prm(rcw, cb_r, krows(4)), prm(row1(W['rg_conv_b']), cb_r)]
    rgc_grid, rgc_car, rgc_outs = (d_rnn // cb_r, ntc), [(SUBLANES, cb_r)], [out(d_rnn, cb_r, tbc)]
    (xc,), rgc_hist, (xc_t,) = stage_fwd("rg_conv_fwd", f_rg_conv, rgc_grid, rgc_ins, rgc_outs, rgc_car, transposed=[0])
    gates = mm(xc, w_bd, name="rg_gates_fwd")
    lru_ins = [act(xc, cb_r), act(gates, cb_r, 0, d_rnn), act(gates, cb_r, d_rnn, d_rnn), act(proj, cb_r, o_gr, d_rnn),
               prm(row1(W['rg_b_a']), cb_r), prm(row1(W['rg_b_x']), cb_r), prm(row1(W['rg_lambda']), cb_r)]
    lru_grid, lru_car = (d_rnn // cb_r, nt), [(1, cb_r)]
    (rec,), lru_hist, (rec_t,) = stage_fwd("rglru_fwd", f_rglru, lru_grid, lru_ins, [out(d_rnn, cb_r)], lru_car,
                                           transposed=[0])
    y_a = mm(rec, w_pa, name="proj_a_fwd")

    dnc = {}
    for nm, col0, width, w0, cb_, f_ in (("q", o_q, qk, 0, LANES, functools.partial(f_dn_conv_norm, LANES ** -0.5)),
                                         ("k", o_k, qk, qk, LANES, functools.partial(f_dn_conv_norm, 1.0)),
                                         ("v", o_v, vdim, 2 * qk, _tile(math.gcd(vdim, o_v), 256), f_dn_conv)):
        ins_ = [act(proj, cb_, col0, width, rows=tbc), prm(dcw[:, w0:w0 + width], cb_, krows(4))]
        grid_, outs_, car_ = (width // cb_, ntc), [out(width, cb_, tbc)], [(SUBLANES, cb_)]
        (y_,), hist_ = stage_fwd("dn_conv_%s_fwd" % nm, f_, grid_, ins_, outs_, car_)
        dnc[nm] = (y_, f_, ins_, grid_, outs_, car_, hist_)
    qn, kn, v_c = dnc["q"][0], dnc["k"][0], dnc["v"][0]
    gate_ins = [act(proj, LANES, o_a, LANES), act(proj, LANES, o_b, LANES),
                prm(padv(W['dn_a_log']), LANES), prm(padv(W['dn_dt_bias']), LANES)]
    gate_outs = [out(LANES, LANES), out(LANES, LANES)]
    (g_dn, beta_dn), _ = stage_fwd("dn_gates_fwd", f_dn_gates, (1, nt), gate_ins, gate_outs)
    n_ch = t // DN_CHUNK
    gt_dn = jnp.transpose(g_dn.reshape(n_ch, DN_CHUNK, LANES)[:, :, :n_vh], (0, 2, 1))
    dn_mid = delta_intra_fwd(qn, kn, v_c, 0, g_dn, gt_dn, beta_dn, n_vh)
    o_dn, dn_hist = delta_inter_fwd(*dn_mid, g_dn, n_vh)
    dno_ins = [act(o_dn, LANES, rows=tbh), act(proj, LANES, o_z, vdim, rows=tbh), prm(nw_t, LANES)]
    dno_grid, dno_outs = (n_vh, nth), [out(vdim, LANES, tbh)]
    (dn,), _, (dn_t,) = stage_fwd("dn_out_fwd", f_dn_out, dno_grid, dno_ins, dno_outs, transposed=[0])
    y_b = mm(dn, w_pb, name="proj_b_fwd")

    cb_m = _tile(math.gcd(math.gcd(d, o_ga), o_gb), 512)
    mrg_ins = [act(proj, cb_m, o_ga, d, rows=tbc), act(proj, cb_m, o_gb, d, rows=tbc), act(y_a, cb_m, rows=tbc),
               act(y_b, cb_m, rows=tbc)]
    mrg_grid, mrg_outs = (d // cb_m, ntc), [out(d, cb_m, tbc)]
    (merged,), _, (merged_t,) = stage_fwd("merge_fwd", f_merge, mrg_grid, mrg_ins, mrg_outs, transposed=[0])
    mix = mm(merged, w_o, name="w_out_fwd")
    ln1_ins = [act(xs, d), act(mix, d), prm(gt1, d), prm(row1(W['ln1_g']), d), prm(row1(W['ln1_b']), d),
               prm(sc2, d), prm(sh2, d)]
    ln1_outs = [out(d, d), out(d, d)]
    (x1, h2), _, (h2_t,) = stage_fwd("ln1_mod2_fwd", f_deepnorm_mod, (1, nt), ln1_ins, ln1_outs, transposed=[1])

    gu = mm(h2, w_gu, name="ffn_in_fwd")
    cb_f = _tile(d_ff, 256)
    ffa_ins = [act(gu, cb_f, 0, d_ff, rows=tbc), act(gu, cb_f, d_ff, d_ff, rows=tbc), prm(fcw, cb_f, krows(3)),
               prm(row1(W['ffn_conv_b']), cb_f)]
    ffa_grid, ffa_car, ffa_outs = (d_ff // cb_f, ntc), [(SUBLANES, cb_f)], [out(d_ff, cb_f, tbc)]
    (act_ff,), ffa_hist, (act_t,) = stage_fwd("ffn_act_fwd", f_ffn_act, ffa_grid, ffa_ins, ffa_outs, ffa_car,
                                              transposed=[0])
    ff = mm(act_ff, w_fd, name="ffn_down_fwd")
    ln2_ins = [act(x1, d), act(ff, d), prm(gt2, d), prm(row1(W['ln2_g']), d), prm(row1(W['ln2_b']), d),
               act(tgt, d, grad=False)]
    ln2_outs = [Out((t, 1), (tb, 1), lambda o, s: (s, 0))]
    (loss_rows,), _ = stage_fwd("ln2_loss_fwd", f_deepnorm_loss, (1, nt), ln2_ins, ln2_outs)

    dx1_a, d_ff_o, d_gt2, d_ln2g, d_ln2b = stage_bwd("ln2_loss_bwd", f_deepnorm_loss, (1, nt), ln2_ins, ln2_outs,
                                                     [jnp.ones((t, 1), F32)])
    d_act = mm(d_ff_o, w_fd, name="ffn_down_bwd_x", tb=True)
    gw_fd = mm(act_t, d_ff_o, name="ffn_down_bwd_w")
    d_gp, d_up, d_fcw, d_fcb = stage_bwd("ffn_act_bwd", f_ffn_act, ffa_grid, ffa_ins, ffa_outs, [d_act],
                                         ffa_car, ffa_hist, gdtypes={0: MXU_DTYPE, 1: MXU_DTYPE})
    col_blocks = lambda g: jnp.transpose(g.reshape(g.shape[0], N_DEV, -1), (1, 0, 2)).astype(WIRE_DTYPE)
    row_blocks = lambda g: g.reshape(N_DEV, -1, g.shape[1]).astype(WIRE_DTYPE)
    big_parts = {}
    d_h2, big_parts['ffn_w_down'] = mm([d_gp, d_up], [w_gate, w_up], name="ffn_in_bwd_x", tb=True,
                                       scatter=[row_blocks(gw_fd)])
    gw_gate, gw_up = mm(h2_t, d_gp, name="ffn_gate_bwd_w"), mm(h2_t, d_up, name="ffn_up_bwd_w")
    dx_a, d_mix, d_gt1, d_ln1g, d_ln1b, d_sc2, d_sh2 = stage_bwd("ln1_mod2_bwd", f_deepnorm_mod, (1, nt), ln1_ins,
                                                                 ln1_outs, [dx1_a, d_h2])
    d_merged = mm(d_mix, w_o, name="w_out_bwd_x", tb=True)
    gw_o = mm(merged_t, d_mix, name="w_out_bwd_w")
    d_ga, d_gb, d_ya, d_yb = stage_bwd("merge_bwd", f_merge, mrg_grid, mrg_ins, mrg_outs, [d_merged],
                                       gdtypes={0: MXU_DTYPE, 1: MXU_DTYPE})
    d_rec = mm(d_ya, w_pa, name="proj_a_bwd_x", tb=True)
    gw_pa = mm(rec_t, d_ya, name="proj_a_bwd_w")
    d_dn = mm(d_yb, w_pb, name="proj_b_bwd_x", tb=True)
    gw_pb = mm(dn_t, d_yb, name="proj_b_bwd_w")

    d_o, d_z, d_nwt = stage_bwd("dn_out_bwd", f_dn_out, dno_grid, dno_ins, dno_outs, [d_dn], gdtypes={1: MXU_DTYPE})
    *d_mid, d_g_state = delta_inter_bwd(*dn_mid, g_dn, dn_hist, d_o, n_vh)
    d_qn, d_kn, d_v, d_g_col, d_gt, d_beta = delta_intra_bwd(qn, kn, v_c, 0, g_dn, gt_dn, beta_dn, d_mid, n_vh)
    d_g_row = jnp.pad(jnp.transpose(d_gt, (0, 2, 1)).reshape(t, n_vh), ((0, 0), (0, LANES - n_vh)))
    d_a, d_b, d_alog, d_dtb = stage_bwd("dn_gates_bwd", f_dn_gates, (1, nt), gate_ins, gate_outs,
                                        [(d_g_state, d_g_col, d_g_row), d_beta], gdtypes={0: MXU_DTYPE, 1: MXU_DTYPE})
    d_win, d_dcw = {}, []
    for nm, cot in (("q", d_qn), ("k", d_kn), ("v", d_v)):
        _, f_, ins_, grid_, outs_, car_, hist_ = dnc[nm]
        d_win[nm], dw_ = stage_bwd("dn_conv_%s_bwd" % nm, f_, grid_, ins_, outs_, [cot], car_, hist_,
                                   gdtypes={0: MXU_DTYPE})
        d_dcw.append(dw_)
    d_dcw = jnp.concatenate(d_dcw, axis=1)

    d_xc_a, d_pr, d_pi, d_gr, d_ba, d_bx, d_lam = stage_bwd(
        "rglru_bwd", f_rglru, lru_grid, lru_ins, [out(d_rnn, cb_r)], [d_rec], lru_car, lru_hist,
        gdtypes={1: MXU_DTYPE, 2: MXU_DTYPE, 3: MXU_DTYPE})
    d_xc_b, big_parts['w_out'], big_parts['w_proj_a'], big_parts['w_proj_b'] = mm(
        [d_pr, d_pi], [w_bd[:, :d_rnn], w_bd[:, d_rnn:]], name="rg_gates_bwd_x", tb=True,
        scatter=[row_blocks(gw_o), row_blocks(gw_pa), row_blocks(gw_pb)])
    gw_bd_a, gw_bd_x = mm(xc_t, d_pr, name="rg_gate_a_bwd_w"), mm(xc_t, d_pi, name="rg_gate_x_bwd_w")
    d_xr, d_rcw, d_rcb = stage_bwd("rg_conv_bwd", f_rg_conv, rgc_grid, rgc_ins, rgc_outs, [(d_xc_a, d_xc_b)],
                                   rgc_car, rgc_hist, gdtypes={0: MXU_DTYPE})

    diag = lambda g: jnp.einsum('nimj,nm->nij', g.reshape(n_blk, d_rnn // n_blk, n_blk, d_rnn // n_blk), eye_b)
    small_names = ['rg_conv_w', 'rg_conv_b', 'rg_w_a', 'rg_b_a', 'rg_w_x', 'rg_b_x', 'rg_lambda', 'dn_conv_w',
                   'dn_a_log', 'dn_dt_bias', 'dn_norm_w', 'ln1_g', 'ln1_b', 'ffn_conv_w', 'ffn_conv_b', 'ln2_g', 'ln2_b']
    small_loc = {
        'rg_conv_w': d_rcw, 'rg_conv_b': d_rcb,
        'rg_w_a': diag(gw_bd_a), 'rg_b_a': d_ba, 'rg_w_x': diag(gw_bd_x), 'rg_b_x': d_bx,
        'rg_lambda': d_lam, 'dn_conv_w': d_dcw, 'dn_a_log': d_alog[:, :n_vh], 'dn_dt_bias': d_dtb[:, :n_vh],
        'dn_norm_w': jnp.sum(d_nwt.reshape(n_vh, LANES), axis=0), 'ln1_g': d_ln1g, 'ln1_b': d_ln1b,
        'ffn_conv_w': d_fcw, 'ffn_conv_b': d_fcb, 'ln2_g': d_ln2g, 'ln2_b': d_ln2b}
    small_list = [small_loc[n] for n in small_names]

    d_segs = [d_xr, d_gr, d_win["q"], d_win["k"], d_win["v"], d_z, d_a, d_b, d_ga, d_gb]
    riders = {4: ('ffn_w_gate', gw_gate), 5: ('ffn_w_up', gw_up)}
    gw_segs = []
    for i, dg in enumerate(d_segs):
        if i in riders:
            g_, big_parts[riders[i][0]] = mm(h1_t, dg, name="proj_bwd_w%d" % i, scatter=[col_blocks(riders[i][1])])
        else:
            g_ = mm(h1_t, dg, name="proj_bwd_w%d" % i)
        gw_segs.append(g_)
    gw_in = jnp.concatenate([g_[:, :splits[i]] for i, g_ in enumerate(gw_segs)], axis=1)
    half = len(d_segs) // 2
    d_h1_a, big_parts['w_in'] = mm(d_segs[:half], groups[:half], name="proj_bwd_x0", tb=True,
                                   scatter=[col_blocks(gw_in)], **MM_SPLIT_CAPS)
    d_h1_b, small_all = mm(d_segs[half:], groups[half:], name="proj_bwd_x1", tb=True,
                           gather=[_pack(small_list)], **MM_SPLIT_CAPS)
    grad_x, d_sc1, d_sh1 = stage_bwd("mod1_bwd", f_modulate, (1, nt), mod1_ins, [out(d, d)], [(d_h1_a, d_h1_b)],
                                     add_to={0: dx_a})

    g_small = dict(zip(small_names, _unpack(sum_parts("sum_small_grads", small_all), small_list)))
    d_ada_me = jnp.concatenate([d_sh1, d_sc1, d_gt1, d_sh2, d_sc2, d_gt2], axis=1)
    (d_ada_all,) = all_gather("gather_d_ada", [d_ada_me.reshape(-1, LANES)])
    g_small['b_ada'] = sum_parts("sum_d_ada", d_ada_all)
    small_names = ['b_ada'] + small_names
    d_ada_cols = lax.dynamic_slice(d_ada_all.reshape(N_DEV, 6 * d), (0, me * ada_w), (N_DEV, ada_w))
    d_ada_pad = jnp.pad(d_ada_cols, ((0, LANES - N_DEV), (0, 0)))
    gw_ada = mm(c_pad, d_ada_pad, name="ada_bwd_w", ta=True, a_act="silu")

    res = {}
    big_parts['w_ada'] = gw_ada[None]
    for n in ['w_ada'] + big:
        res[n] = adamw("adamw_" + n, W[n], big_parts[n], M[n], V[n])
    for n in small_sh:
        w_ = W[n].shape[1]
        g_small[n] = lax.dynamic_slice(g_small[n], (0, me * w_), (W[n].shape[0], w_))
    for n in small_names:
        g_small[n] = g_small[n].reshape(W[n].shape)
    pk = lambda dct: _pack([dct[n] for n in small_names])
    s_g, s_d, s_m, s_v = adamw("adamw_small", pk(W), pk(g_small)[None], pk(M), pk(V))
    like = [W[n] for n in small_names]
    for n, g_, d_, m_, v_ in zip(small_names, _unpack(s_g, like), _unpack(s_d, like), _unpack(s_m, like), _unpack(s_v, like)):
        res[n] = (g_, d_, m_, v_)

    loss = lax.psum(jnp.sum(loss_rows), ("x", "y", "c"))
    outs = [loss, grad_x[None]]
    for j in range(4):
        outs += [res[n][j].reshape(loc[n].shape) for n in names]
    return tuple(outs)
```

```python
import functools
import math

import jax
import jax.numpy as jnp
from jax import lax
from jax.experimental import pallas as pl
from jax.experimental.pallas import tpu as pltpu

F32 = jnp.float32
BF16 = jnp.bfloat16
MXU_DTYPE = BF16
WIRE_DTYPE = BF16
DN_DTYPE = BF16
HI = lax.Precision.HIGHEST
MESH = pl.DeviceIdType.MESH

N_DEV = 8
LANES = 128
SUBLANES = 8
VMEM_LIMIT = 56 * 1024 * 1024
MM_TM_CAP, MM_TN_CAP, MM_TK_CAP = 1024, 1536, 2048
MM_SPLIT_CAPS = dict(tm_cap=1024, tn_cap=1024, tk_cap=1024)

RG_C = 8.0
DN_CHUNK = 64
DN_HEAD_GROUP = 16
DN_INTER_CHUNKS = 4
DN_INTRA_CHUNKS = 2
LN_EPS = 1e-5
RMS_EPS = 1e-6
L2_EPS = 1e-6
DEPTH = 1
DEEPNORM_ALPHA = (2 * DEPTH) ** 0.25
ADAM_LR = 0.001
ADAM_B1 = 0.9
ADAM_B2 = 0.999
ADAM_EPS = 1e-08
ADAM_WD = 0.01
ADAM_STEP = 10


def _tile(n, cap, unit=LANES):
    best = None
    for t in range(unit, min(n, cap) + 1, unit):
        if n % t == 0:
            best = t
    return best if best is not None else n


def _round_up(n, m):
    return (n + m - 1) // m * m


_HBM = pl.BlockSpec(memory_space=pl.ANY)


def _exchange_sems(n):
    return [pltpu.SemaphoreType.DMA((n, N_DEV - 1)), pltpu.SemaphoreType.DMA((n, N_DEV - 1)),
            pltpu.SemaphoreType.DMA((n,))]


def _exchange_out_shape(arrs, scatter):
    return [jax.ShapeDtypeStruct(a.shape if scatter else (N_DEV,) + a.shape, a.dtype) for a in arrs]


def _exchange_copies(in_refs, out_refs, sems, scatter, phase):
    send_sems, recv_sems, local_sems = sems
    x, y, c = lax.axis_index("x"), lax.axis_index("y"), lax.axis_index("c")
    me = 4 * x + 2 * y + c
    peers = [(x ^ ((k >> 2) & 1), y ^ ((k >> 1) & 1), c ^ (k & 1)) for k in range(N_DEV)]
    row = [4 * p[0] + 2 * p[1] + p[2] for p in peers]
    n = len(in_refs)

    def local(i):
        return pltpu.make_async_copy(in_refs[i].at[me] if scatter else in_refs[i], out_refs[i].at[me], local_sems.at[i])

    def remote(i, k, src, dst_row, to):
        return pltpu.make_async_remote_copy(src_ref=src, dst_ref=out_refs[i].at[dst_row],
                                            send_sem=send_sems.at[i, k - 1], recv_sem=recv_sems.at[i, k - 1],
                                            device_id=to, device_id_type=MESH)

    if scatter:
        sends = [(i, k, in_refs[i].at[row[k]], me, peers[k]) for k in range(1, N_DEV) for i in range(n)]
        passed = []
    else:
        sends = [(i, k, in_refs[i], me, peers[k]) for k in (1, 2, 4, 6) for i in range(n)]
        passed = [(i, k + 1, out_refs[i].at[row[k]], row[k], peers[1]) for k in (2, 4, 6) for i in range(n)]
    arrival = lambda i, k: remote(i, k, in_refs[i].at[me] if scatter else in_refs[i], row[k], peers[k])

    if phase == "start":
        for i in range(n):
            local(i).start()
        for cp in sends:
            remote(*cp).start()
    else:
        for cp in passed:
            arrival(cp[0], cp[1] - 1).wait_recv()
            remote(*cp).start()
        waited = {(cp[0], cp[1] - 1) for cp in passed}
        for k in range(1, N_DEV):
            for i in range(n):
                if (i, k) not in waited:
                    arrival(i, k).wait_recv()
        for cp in sends + passed:
            remote(*cp).wait_send()
        for i in range(n):
            local(i).wait()


def mm(a, b, *, name, ta=False, tb=False, a_act=None, bias=None, out_dtype=F32,
       tm_cap=MM_TM_CAP, tn_cap=MM_TN_CAP, tk_cap=MM_TK_CAP, gather=(), scatter=()):
    a_segs = list(a) if isinstance(a, (list, tuple)) else [a]
    b_segs = list(b) if isinstance(b, (list, tuple)) else [b]
    ns = len(a_segs)
    assert ns == len(b_segs) and (ns == 1 or a_act is None)
    m = a_segs[0].shape[1] if ta else a_segs[0].shape[0]
    n = b_segs[0].shape[0] if tb else b_segs[0].shape[1]
    ks = [x.shape[0] if ta else x.shape[1] for x in a_segs]
    assert ks == [y.shape[1] if tb else y.shape[0] for y in b_segs], (ks, ta, tb)
    tm, tn = _tile(m, tm_cap), _tile(n, tn_cap)
    tks = [_tile(k_, tk_cap) for k_ in ks]
    cnt = [k_ // t_ for k_, t_ in zip(ks, tks)]
    lo = [sum(cnt[:s]) for s in range(ns)]
    nk = sum(cnt)
    grid = (m // tm, n // tn, nk)
    dims = (((0 if ta else 1,), (1 if tb else 0,)), ((), ()))
    xch = list(gather) + list(scatter)
    nx, ng = len(xch), len(gather)
    n_main = 2 * ns + (bias is not None)

    def body(*refs):
        a_refs, b_refs = refs[:ns], refs[ns:2 * ns]
        bias_ref = refs[2 * ns] if bias is not None else None
        x_in, o_ref, x_out = refs[n_main:n_main + nx], refs[n_main + nx], refs[n_main + nx + 1:n_main + 2 * nx + 1]
        rest = refs[n_main + 2 * nx + 1:]
        acc_ref = rest[0] if nk > 1 else None
        sems = rest[1 if nk > 1 else 0:]
        groups = []
        if ng:
            groups.append((x_in[:ng], x_out[:ng], sems[:3], False))
        if nx > ng:
            groups.append((x_in[ng:], x_out[ng:], sems[-3:], True))
        if nx:
            step = (pl.program_id(0) * grid[1] + pl.program_id(1)) * grid[2] + pl.program_id(2)

            @pl.when(step == 0)
            def _():
                for gi, go_, gs, sc in groups:
                    _exchange_copies(gi, go_, gs, sc, "start")
        kk = pl.program_id(2)

        def finish(r):
            if bias is not None:
                r = r + bias_ref[...]
            o_ref[...] = r.astype(o_ref.dtype)

        def segment(s):
            av = a_refs[s][...]
            if a_act == "silu":
                av = jax.nn.silu(av.astype(F32))
            prod = lax.dot_general(av.astype(MXU_DTYPE), b_refs[s][...].astype(MXU_DTYPE), dims,
                                   preferred_element_type=F32)
            if nk == 1:
                finish(prod)
                return
            opens, closes = lo[s] == 0, lo[s] + cnt[s] == nk
            if opens:
                @pl.when(kk == 0)
                def _():
                    acc_ref[...] = prod
            inner = [kk > 0] * opens + [kk < nk - 1] * closes
            if inner:
                @pl.when(functools.reduce(lambda p, q: p & q, inner))
                def _():
                    acc_ref[...] += prod
            else:
                acc_ref[...] += prod
            if closes:
                @pl.when(kk == nk - 1)
                def _():
                    finish(acc_ref[...] + prod)

        for s in range(ns):
            if ns == 1:
                segment(s)
            else:
                pl.when((kk >= lo[s]) & (kk < lo[s] + cnt[s]))(functools.partial(segment, s))

        if nx:
            @pl.when(step == grid[0] * grid[1] * grid[2] - 1)
            def _():
                for gi, go_, gs, sc in groups:
                    _exchange_copies(gi, go_, gs, sc, "wait")

    def seg_index(s):
        return lambda q: jnp.clip(q - lo[s], 0, cnt[s] - 1) if ns > 1 else q

    a_specs, b_specs = [], []
    for s in range(ns):
        qi, tk = seg_index(s), tks[s]
        a_specs.append(pl.BlockSpec((tk, tm), (lambda qi: lambda i, j, q: (qi(q), i))(qi)) if ta
                       else pl.BlockSpec((tm, tk), (lambda qi: lambda i, j, q: (i, qi(q)))(qi)))
        b_specs.append(pl.BlockSpec((tn, tk), (lambda qi: lambda i, j, q: (j, qi(q)))(qi)) if tb
                       else pl.BlockSpec((tk, tn), (lambda qi: lambda i, j, q: (qi(q), j))(qi)))
    in_specs, args = a_specs + b_specs, a_segs + b_segs
    if bias is not None:
        in_specs.append(pl.BlockSpec((1, tn), lambda i, j, q: (0, j)))
        args.append(bias)
    o_spec, o_shape = pl.BlockSpec((tm, tn), lambda i, j, q: (i, j)), jax.ShapeDtypeStruct((m, n), out_dtype)
    acc = [pltpu.VMEM((tm, tn), F32)] if nk > 1 else []
    if not nx:
        return pl.pallas_call(
            body, name=name, grid=grid, in_specs=in_specs, out_specs=o_spec, out_shape=o_shape, scratch_shapes=acc,
            compiler_params=pltpu.CompilerParams(dimension_semantics=("parallel", "parallel", "arbitrary"),
                                                 vmem_limit_bytes=VMEM_LIMIT),
        )(*args)
    return pl.pallas_call(
        body, name=name, grid=grid, in_specs=in_specs + [_HBM] * nx, out_specs=[o_spec] + [_HBM] * nx,
        out_shape=[o_shape] + _exchange_out_shape(list(gather), False) + _exchange_out_shape(list(scatter), True),
        scratch_shapes=acc + (_exchange_sems(ng) if ng else []) + (_exchange_sems(nx - ng) if nx > ng else []),
        compiler_params=pltpu.CompilerParams(dimension_semantics=("arbitrary", "arbitrary", "arbitrary"),
                                             vmem_limit_bytes=VMEM_LIMIT, has_side_effects=True),
    )(*args, *xch)


class In:
    def __init__(self, arr, block, imap, acc=False, grad=True, parts=None, gshape=None, gimap=None):
        self.arr, self.block, self.imap, self.acc, self.grad, self.parts = arr, block, imap, acc, grad, parts
        self.gshape = arr.shape if gshape is None else gshape
        self.gimap = imap if gimap is None else gimap


class Out:
    def __init__(self, shape, block, imap, dtype=F32):
        self.shape, self.block, self.imap, self.dtype = shape, block, imap, dtype


def _load(in_refs, ins):
    vals = []
    for r, i in zip(in_refs, ins):
        if i.parts is None:
            vals.append(r[...])
        else:
            vals.extend(r[p] for p in i.parts)
    return vals


def _stage_params():
    return pltpu.CompilerParams(dimension_semantics=("parallel", "arbitrary"), vmem_limit_bytes=VMEM_LIMIT)


def stage_fwd(name, f, grid, ins, outs, carries=(), transposed=()):
    n_in, n_out, n_c, n_t = len(ins), len(outs), len(carries), len(transposed)

    def body(*refs):
        in_refs, out_refs = refs[:n_in], refs[n_in:n_in + n_out]
        hist_refs = refs[n_in + n_out:n_in + n_out + n_c]
        t_refs = refs[n_in + n_out + n_c:n_in + n_out + n_c + n_t]
        c_refs = refs[n_in + n_out + n_c + n_t:]
        if n_c:
            @pl.when(pl.program_id(1) == 0)
            def _():
                for c in c_refs:
                    c[...] = jnp.zeros_like(c)
        cin = [c[...] for c in c_refs]
        for h, c in zip(hist_refs, cin):
            h[...] = c
        o, cout = f(*_load(in_refs, ins), *cin)
        for r, v in zip(out_refs, o):
            r[...] = v.astype(r.dtype)
        for r, k in zip(t_refs, transposed):
            r[...] = o[k].T.astype(r.dtype)
        for c, v in zip(c_refs, cout):
            c[...] = v

    hist_spec = lambda c: pl.BlockSpec((None, None) + tuple(c), lambda o, s: (o, s) + (0,) * len(c))
    flip = lambda o_: pl.BlockSpec(o_.block[::-1], (lambda im: lambda o, s: im(o, s)[::-1])(o_.imap))
    res = pl.pallas_call(
        body, name=name, grid=grid,
        in_specs=[pl.BlockSpec(i.block, i.imap) for i in ins],
        out_specs=[pl.BlockSpec(o.block, o.imap) for o in outs] + [hist_spec(c) for c in carries]
        + [flip(outs[k]) for k in transposed],
        out_shape=[jax.ShapeDtypeStruct(o.shape, o.dtype) for o in outs]
        + [jax.ShapeDtypeStruct(tuple(grid) + tuple(c), F32) for c in carries]
        + [jax.ShapeDtypeStruct(outs[k].shape[::-1], MXU_DTYPE) for k in transposed],
        scratch_shapes=[pltpu.VMEM(tuple(c), F32) for c in carries],
        compiler_params=_stage_params(),
    )(*[i.arr for i in ins])
    res = list(res)
    if transposed:
        return res[:n_out], res[n_out:n_out + n_c], res[n_out + n_c:]
    return res[:n_out], res[n_out:]


def stage_bwd(name, f, grid, ins, outs, cots, carries=(), hists=(), add_to=None, gdtypes=None):
    n_in, n_out, n_c = len(ins), len(outs), len(carries)
    ns = grid[1]
    add_to = add_to or {}
    gdtypes = gdtypes or {}
    add_idx = sorted(add_to)
    g_idx = [k for k, i in enumerate(ins) if i.grad]
    cots = [c if isinstance(c, (tuple, list)) else (c,) for c in cots]
    n_cot = [len(c) for c in cots]
    rev = lambda imap: (lambda o, s: imap(o, ns - 1 - s))

    def body(*refs):
        p = 0
        in_refs = refs[p:p + n_in]; p += n_in
        cot_refs = []
        for cnt in n_cot:
            cot_refs.append(refs[p:p + cnt]); p += cnt
        hist_refs = refs[p:p + n_c]; p += n_c
        add_refs = refs[p:p + len(add_idx)]; p += len(add_idx)
        g_refs = refs[p:p + len(g_idx)]; p += len(g_idx)
        dc_refs = refs[p:]
        first = pl.program_id(1) == 0
        if n_c:
            @pl.when(first)
            def _():
                for c in dc_refs:
                    c[...] = jnp.zeros_like(c)
        vals = _load(in_refs, ins)
        cin = [h[...] for h in hist_refs]
        (o, cout), vjp = jax.vjp(lambda *a: f(*a), *vals, *cin)
        cot_o = []
        for crs, v in zip(cot_refs, o):
            c = crs[0][...].astype(v.dtype)
            for extra in crs[1:]:
                c = c + extra[...].astype(v.dtype)
            cot_o.append(c)
        cot_c = tuple(c[...] for c in dc_refs)
        grads = vjp((tuple(cot_o), cot_c))
        pos, per_in = 0, []
        for i in ins:
            cnt = 1 if i.parts is None else len(i.parts)
            per_in.append(grads[pos:pos + cnt])
            pos += cnt
        dcin = grads[pos:]
        for gr, k in zip(g_refs, g_idx):
            i, gs = ins[k], per_in[k]
            if i.acc:
                @pl.when(first)
                def _(gr=gr):
                    gr[...] = jnp.zeros_like(gr)
                if i.parts is None:
                    gr[...] += gs[0].astype(gr.dtype)
                else:
                    for pt, g in zip(i.parts, gs):
                        gr[pt] += g.astype(gr.dtype)
            else:
                g = gs[0]
                if k in add_to:
                    g = g + add_refs[add_idx.index(k)][...].astype(g.dtype)
                gr[...] = g.astype(gr.dtype)
        for c, v in zip(dc_refs, dcin):
            c[...] = v

    in_specs = [pl.BlockSpec(i.block, rev(i.imap)) for i in ins]
    for o_, cnt in zip(outs, n_cot):
        in_specs += [pl.BlockSpec(o_.block, rev(o_.imap))] * cnt
    in_specs += [pl.BlockSpec((None, None) + tuple(c), (lambda c: (lambda o, s: (o, ns - 1 - s) + (0,) * len(c)))(c))
                 for c in carries]
    in_specs += [pl.BlockSpec(ins[k].block, rev(ins[k].gimap)) for k in add_idx]
    out_specs, out_shape = [], []
    for k in g_idx:
        i = ins[k]
        if i.acc:
            out_specs.append(pl.BlockSpec(i.block, (lambda im: (lambda o, s: im(o, 0)))(i.imap)))
        else:
            out_specs.append(pl.BlockSpec(i.block, rev(i.gimap)))
        out_shape.append(jax.ShapeDtypeStruct(i.gshape, gdtypes.get(k, F32)))
    res = pl.pallas_call(
        body, name=name, grid=grid, in_specs=in_specs, out_specs=out_specs, out_shape=out_shape,
        scratch_shapes=[pltpu.VMEM(tuple(c), F32) for c in carries],
        compiler_params=_stage_params(),
    )(*[i.arr for i in ins], *[a for c in cots for a in c], *hists, *[add_to[k] for k in add_idx])
    return list(res)


def _iota_rows(shape):
    return lax.broadcasted_iota(jnp.int32, shape, 0)


@functools.partial(jax.custom_vjp, nondiff_argnums=(1,))
def _roll_rows(x, s):
    return pltpu.roll(x, s % x.shape[0], 0)


def _roll_rows_fwd(x, s):
    return _roll_rows(x, s), None


def _roll_rows_bwd(s, _, g):
    return (_roll_rows(g, -s),)


_roll_rows.defvjp(_roll_rows_fwd, _roll_rows_bwd)


@jax.custom_vjp
def _drop_head(xx):
    return xx[SUBLANES:]


def _drop_head_fwd(xx):
    return xx[SUBLANES:], None


def _drop_head_bwd(_, g):
    return (jnp.concatenate([jnp.zeros((SUBLANES, g.shape[1]), g.dtype), g], axis=0),)


_drop_head.defvjp(_drop_head_fwd, _drop_head_bwd)


@jax.custom_vjp
def _last_rows(x):
    return x[x.shape[0] - SUBLANES:]


def _last_rows_fwd(x):
    return x[x.shape[0] - SUBLANES:], x.shape[0]


def _last_rows_bwd(n, g):
    return (jnp.concatenate([jnp.zeros((n - SUBLANES, g.shape[1]), g.dtype), g], axis=0),)


_last_rows.defvjp(_last_rows_fwd, _last_rows_bwd)


def _last_row(x):
    n = x.shape[0]
    return jnp.sum(jnp.where(_iota_rows(x.shape) == n - 1, x, 0.0), axis=0, keepdims=True)


def _scan_steps(n):
    s = 1
    while s < n:
        yield s
        s *= 2


def _block_scan_impl(a, u, h0):
    n = a.shape[0]
    row = _iota_rows(a.shape)
    for s in _scan_steps(n):
        keep = row >= s
        a_s = jnp.where(keep, pltpu.roll(a, s, 0), 1.0)
        u_s = jnp.where(keep, pltpu.roll(u, s, 0), 0.0)
        u = u + a * u_s
        a = a * a_s
    return u + a * h0


@jax.custom_vjp
def _block_scan(a, u, h0):
    return _block_scan_impl(a, u, h0)


def _block_scan_fwd(a, u, h0):
    h = _block_scan_impl(a, u, h0)
    return h, (a, h, h0)


def _block_scan_bwd(res, dh):
    a, h, h0 = res
    n = a.shape[0]
    row = _iota_rows(a.shape)
    b = jnp.where(row < n - 1, pltpu.roll(a, n - 1, 0), 0.0)
    lam = dh
    for s in _scan_steps(n):
        keep = row < n - s
        b_s = jnp.where(keep, pltpu.roll(b, n - s, 0), 1.0)
        l_s = jnp.where(keep, pltpu.roll(lam, n - s, 0), 0.0)
        lam = lam + b * l_s
        b = b * b_s
    h_prev = jnp.where(row >= 1, pltpu.roll(h, 1, 0), jnp.broadcast_to(h0, h.shape))
    d_h0 = jnp.sum(jnp.where(row == 0, a * lam, 0.0), axis=0, keepdims=True)
    return lam * h_prev, lam, d_h0


_block_scan.defvjp(_block_scan_fwd, _block_scan_bwd)


def _dot_hi(a, b, dims=(((1,), (0,)), ((), ()))):
    return lax.dot_general(a, b, dims, precision=HI, preferred_element_type=F32)


_NN, _NT, _TN = "nn", "nt", "tn"
_CONTRACT = {_NN: (1, 0), _NT: (1, 1), _TN: (0, 0)}


def _raw_dot(a, b, kind):
    ca, cb = _CONTRACT[kind]
    lead = a.ndim - 2
    dims = (((ca + lead,), (cb + lead,)), (tuple(range(lead)), tuple(range(lead))))
    return lax.dot_general(a.astype(DN_DTYPE), b.astype(DN_DTYPE), dims, preferred_element_type=F32)


@jax.custom_vjp
def _nn(a, b):
    return _raw_dot(a, b, _NN)


_nn.defvjp(lambda a, b: (_raw_dot(a, b, _NN), (a, b)),
           lambda r, g: (_raw_dot(g, r[1], _NT), _raw_dot(r[0], g, _TN)))


@jax.custom_vjp
def _nt(a, b):
    return _raw_dot(a, b, _NT)


_nt.defvjp(lambda a, b: (_raw_dot(a, b, _NT), (a, b)),
           lambda r, g: (_raw_dot(g, r[1], _NN), _raw_dot(g, r[0], _TN)))


@jax.custom_vjp
def _tn(a, b):
    return _raw_dot(a, b, _TN)


_tn.defvjp(lambda a, b: (_raw_dot(a, b, _TN), (a, b)),
           lambda r, g: (_raw_dot(r[1], g, _NT), _raw_dot(r[0], g, _NN)))


def _neumann_inverse(a):
    n = a.shape[-1]
    eye = (lax.broadcasted_iota(jnp.int32, (n, n), 0) == lax.broadcasted_iota(jnp.int32, (n, n), 1)).astype(F32)
    p = _raw_dot(a, a, _NN)
    e = p
    for _ in range(int(math.log2(n)) - 2):
        p = _raw_dot(p, p, _NN)
        e = e + p + _raw_dot(e, p, _NN)
    return eye - a + e - _raw_dot(a, e, _NN)


@jax.custom_vjp
def _unit_lower_inverse(a):
    return _neumann_inverse(a)


def _unit_lower_inverse_fwd(a):
    x = _neumann_inverse(a)
    return x, x


def _unit_lower_inverse_bwd(x, g):
    return (-_raw_dot(_raw_dot(x, g, _TN), x, _NT),)


_unit_lower_inverse.defvjp(_unit_lower_inverse_fwd, _unit_lower_inverse_bwd)


def _softplus(x):
    return jnp.maximum(x, 0.0) + jnp.log1p(jnp.exp(-jnp.abs(x)))


def _neg_expm1(x):
    series = -x * (1.0 + x * (0.5 + x * (1.0 / 6.0 + x * (1.0 / 24.0 + x * (1.0 / 120.0)))))
    return jnp.where(x > -0.03, series, 1.0 - jnp.exp(x))


def f_modulate(x, sc, sh):
    return (x * (1.0 + sc) + sh,), ()


def _deepnorm(x, y, gt, g, b):
    v = DEEPNORM_ALPHA * x + (1.0 + gt) * y
    mu = jnp.mean(v, axis=-1, keepdims=True)
    vc = v - mu
    var = jnp.mean(vc * vc, axis=-1, keepdims=True)
    return vc * lax.rsqrt(var + LN_EPS) * g + b


def f_deepnorm_mod(x, y, gt, g, b, sc, sh):
    x1 = _deepnorm(x, y, gt, g, b)
    return (x1, x1 * (1.0 + sc) + sh), ()


def f_deepnorm_loss(x, y, gt, g, b, target):
    err = _deepnorm(x, y, gt, g, b) - target
    return (0.5 * jnp.mean(err * err, axis=-1, keepdims=True),), ()


def _causal_conv(x, prev, ws):
    xx = jnp.concatenate([prev, x], axis=0)
    k = len(ws)
    y = ws[k - 1] * x
    for j in range(k - 1):
        y = y + ws[j] * _drop_head(_roll_rows(xx, k - 1 - j))
    return y


def f_rg_conv(x, w0, w1, w2, w3, b, prev):
    return (_causal_conv(x, prev, (w0, w1, w2, w3)) + b,), (_last_rows(x),)


def f_dn_conv(x, w0, w1, w2, w3, prev):
    return (jax.nn.silu(_causal_conv(x, prev, (w0, w1, w2, w3))),), (_last_rows(x),)


def f_ffn_act(gp, up, w0, w1, w2, b, prev):
    return (jax.nn.gelu(_causal_conv(gp, prev, (w0, w1, w2)) + b) * up,), (_last_rows(gp),)


def f_rglru(xc, pre_r, pre_i, gr, b_a, b_x, lam, h0):
    gate_r = jax.nn.sigmoid(pre_r + b_a)
    gate_i = jax.nn.sigmoid(pre_i + b_x)
    log_a = -RG_C * gate_r * _softplus(-lam)
    a = jnp.exp(log_a)
    mult = jnp.sqrt(_neg_expm1(2.0 * log_a))
    h = _block_scan(a, mult * gate_i * xc, h0)
    return (h * jax.nn.gelu(gr),), (_last_row(h),)


def f_dn_conv_norm(scale, x, w0, w1, w2, w3, prev):
    y = jax.nn.silu(_causal_conv(x, prev, (w0, w1, w2, w3)))
    return (y * lax.rsqrt(jnp.sum(y * y, axis=-1, keepdims=True) + L2_EPS) * scale,), (_last_rows(x),)


def f_dn_gates(a_in, b_in, a_log, dt_bias):
    g = -jnp.exp(a_log) * _softplus(a_in + dt_bias)
    n = g.shape[0]
    shift = int(math.log2(DN_CHUNK))
    ri = lax.broadcasted_iota(jnp.int32, (n, n), 0)
    ci = lax.broadcasted_iota(jnp.int32, (n, n), 1)
    tri = ((lax.shift_right_logical(ri, shift) == lax.shift_right_logical(ci, shift)) & (ri >= ci)).astype(F32)
    return (_dot_hi(tri, g), jax.nn.sigmoid(b_in)), ()


def f_dn_out(o, z, nw):
    r = lax.rsqrt(jnp.mean(o * o, axis=-1, keepdims=True) + RMS_EPS)
    return (o * r * nw * jax.nn.silu(z),), ()


def f_merge(ga, gb, ya, yb):
    return (jax.nn.sigmoid(ga) * ya + jax.nn.sigmoid(gb) * yb,), ()


def _delta_intra(q, k, v, g_i, g_j, beta):
    c = q.shape[-2]
    ri = lax.broadcasted_iota(jnp.int32, (c, c), 0)
    ci = lax.broadcasted_iota(jnp.int32, (c, c), 1)
    decay = jnp.exp(jnp.where(ri >= ci, g_i - g_j, -jnp.inf))
    g_last = jnp.sum(jnp.where(_iota_rows((c, 1)) == c - 1, g_i, 0.0), axis=-2, keepdims=True)
    exp_g = jnp.exp(g_i)
    kb = k * beta
    t_inv = _unit_lower_inverse(jnp.where(ri > ci, _nt(kb, k) * decay, 0.0))
    u = _nn(t_inv, v * beta)
    w = _nn(t_inv, kb * exp_g)
    return u, w, _nt(q, k) * decay, q * exp_g, k * jnp.exp(g_last - g_i)


def _delta_inter(u, w, qk, q_dec, k_dec, g_last, state):
    v_new = u - _nn(w, state)
    o = _nn(q_dec, state) + _nn(qk, v_new)
    return o, jnp.exp(g_last) * state + _tn(k_dec, v_new)


def _delta_params(sem):
    return pltpu.CompilerParams(dimension_semantics=(sem,), vmem_limit_bytes=VMEM_LIMIT)


def _head_groups(n_vh):
    hb = min(DN_HEAD_GROUP, n_vh)
    return [range(h0, h0 + hb) for h0 in range(0, n_vh, hb)]


def _stack(hs, f):
    return jnp.stack([f(h) for h in hs])


def _rows(ci):
    return slice(ci * DN_CHUNK, (ci + 1) * DN_CHUNK)


def _intra_pairs(n_vh):
    return [(ci, h) for ci in range(DN_INTRA_CHUNKS) for h in range(n_vh)]


def _pair_stack(pairs, ref, width=LANES, head_of=lambda h: h):
    return jnp.stack([ref[_rows(ci), head_of(h) * LANES:head_of(h) * LANES + width] for ci, h in pairs])


def _intra_operands(pairs, rep, q_ref, k_ref, v_ref, g_ref, gt_ref, b_ref):
    qk_head = lambda h: h // rep
    return (_pair_stack(pairs, q_ref, head_of=qk_head), _pair_stack(pairs, k_ref, head_of=qk_head),
            _pair_stack(pairs, v_ref), jnp.stack([g_ref[_rows(ci), h:h + 1] for ci, h in pairs]),
            jnp.stack([gt_ref[ci, h:h + 1, :] for ci, h in pairs]),
            jnp.stack([b_ref[_rows(ci), h:h + 1] for ci, h in pairs]))


def _intra_spec(width, col=0):
    return pl.BlockSpec((DN_INTRA_CHUNKS * DN_CHUNK, width), lambda s: (s, col))


def _inter_operands(hs, ci, u_ref, w_ref, qk_ref, qd_ref, kd_ref, g_ref):
    f32 = lambda ref, width=LANES: _stack(hs, lambda h: ref[_rows(ci), h * LANES:h * LANES + width].astype(F32))
    last = (ci + 1) * DN_CHUNK - 1
    return (f32(u_ref), f32(w_ref), f32(qk_ref, DN_CHUNK), f32(qd_ref), f32(kd_ref),
            _stack(hs, lambda h: g_ref[last:last + 1, h:h + 1]))


def _inter_spec(width, steps, reverse=False):
    rows = DN_INTER_CHUNKS * DN_CHUNK
    return pl.BlockSpec((rows, width), (lambda s: (steps - 1 - s, 0)) if reverse else (lambda s: (s, 0)))


def delta_intra_fwd(qn, kn, qkv, v_blk, big_g, big_gt, beta, n_vh):
    t, qk_w = qn.shape
    vdim = n_vh * LANES
    rep = vdim // qk_w
    nc = t // DN_CHUNK

    pairs = _intra_pairs(n_vh)

    def body(q_ref, k_ref, v_ref, g_ref, gt_ref, b_ref, u_ref, w_ref, qk_ref, qd_ref, kd_ref):
        u, w, qk, qd, kd = _delta_intra(*_intra_operands(pairs, rep, q_ref, k_ref, v_ref, g_ref, gt_ref, b_ref))
        for i, (ci, h) in enumerate(pairs):
            at = (_rows(ci), slice(h * LANES, (h + 1) * LANES))
            u_ref[at] = u[i]
            w_ref[at] = w[i].astype(w_ref.dtype)
            qk_ref[at] = jnp.concatenate([qk[i], jnp.zeros_like(qk[i])], axis=1).astype(qk_ref.dtype)
            qd_ref[at] = qd[i].astype(qd_ref.dtype)
            kd_ref[at] = kd[i].astype(kd_ref.dtype)

    return pl.pallas_call(
        body, name="delta_intra_fwd", grid=(nc // DN_INTRA_CHUNKS,),
        in_specs=[_intra_spec(qk_w), _intra_spec(qk_w), _intra_spec(vdim, v_blk), _intra_spec(LANES),
                  pl.BlockSpec((DN_INTRA_CHUNKS, n_vh, DN_CHUNK), lambda s: (s, 0, 0)), _intra_spec(LANES)],
        out_specs=[_intra_spec(vdim)] * 5,
        out_shape=[jax.ShapeDtypeStruct((t, vdim), F32)] + [jax.ShapeDtypeStruct((t, vdim), DN_DTYPE)] * 4,
        compiler_params=_delta_params("parallel"),
    )(qn, kn, qkv, big_g, big_gt, beta)


def delta_inter_fwd(u, w, qk, q_dec, k_dec, big_g, n_vh):
    t, vdim = u.shape
    nc = t // DN_CHUNK

    cpb = DN_INTER_CHUNKS
    steps = nc // cpb

    def body(u_ref, w_ref, qk_ref, qd_ref, kd_ref, g_ref, o_ref, hist_ref, s_ref):
        @pl.when(pl.program_id(0) == 0)
        def _():
            s_ref[...] = jnp.zeros_like(s_ref)
        for ci in range(cpb):
            for hs in _head_groups(n_vh):
                grp = slice(hs[0], hs[-1] + 1)
                st = s_ref[grp]
                hist_ref[ci, grp] = st
                o, ns = _delta_inter(*_inter_operands(hs, ci, u_ref, w_ref, qk_ref, qd_ref, kd_ref, g_ref), st)
                for i, h in enumerate(hs):
                    o_ref[_rows(ci), h * LANES:(h + 1) * LANES] = o[i]
                s_ref[grp] = ns

    return pl.pallas_call(
        body, name="delta_inter_fwd", grid=(steps,),
        in_specs=[_inter_spec(vdim, steps)] * 5 + [_inter_spec(LANES, steps)],
        out_specs=[_inter_spec(vdim, steps), pl.BlockSpec((cpb, n_vh, LANES, LANES), lambda s: (s, 0, 0, 0))],
        out_shape=[jax.ShapeDtypeStruct((t, vdim), F32), jax.ShapeDtypeStruct((nc, n_vh, LANES, LANES), F32)],
        scratch_shapes=[pltpu.VMEM((n_vh, LANES, LANES), F32)],
        compiler_params=_delta_params("arbitrary"),
    )(u, w, qk, q_dec, k_dec, big_g)


def delta_inter_bwd(u, w, qk, q_dec, k_dec, big_g, hist, d_o, n_vh):
    t, vdim = u.shape
    nc = t // DN_CHUNK

    cpb = DN_INTER_CHUNKS
    steps = nc // cpb

    def body(u_ref, w_ref, qk_ref, qd_ref, kd_ref, g_ref, hist_ref, do_ref,
             du_ref, dw_ref, dqk_ref, dqd_ref, dkd_ref, dg_ref, ds_ref):
        @pl.when(pl.program_id(0) == 0)
        def _():
            ds_ref[...] = jnp.zeros_like(ds_ref)
        lane = lax.broadcasted_iota(jnp.int32, (1, LANES), 1)
        last = _iota_rows((DN_CHUNK, LANES)) == DN_CHUNK - 1
        for ci in reversed(range(cpb)):
            dgl_all = jnp.zeros((1, LANES), F32)
            for hs in _head_groups(n_vh):
                grp = slice(hs[0], hs[-1] + 1)
                prim = _inter_operands(hs, ci, u_ref, w_ref, qk_ref, qd_ref, kd_ref, g_ref) + (hist_ref[ci, grp],)
                _, vjp = jax.vjp(_delta_inter, *prim)
                cot_o = _stack(hs, lambda h: do_ref[_rows(ci), h * LANES:(h + 1) * LANES])
                du, dw, dqk, dqd, dkd, dgl, dst = vjp((cot_o, ds_ref[grp]))
                ds_ref[grp] = dst
                for i, h in enumerate(hs):
                    sl = slice(h * LANES, (h + 1) * LANES)
                    du_ref[_rows(ci), sl] = du[i]
                    dw_ref[_rows(ci), sl] = dw[i]
                    dqk_ref[_rows(ci), sl] = jnp.concatenate([dqk[i], jnp.zeros_like(dqk[i])], axis=1)
                    dqd_ref[_rows(ci), sl] = dqd[i]
                    dkd_ref[_rows(ci), sl] = dkd[i]
                    dgl_all = dgl_all + dgl[i] * (lane == h).astype(F32)
            dg_ref[_rows(ci), :] = jnp.where(last, jnp.broadcast_to(dgl_all, (DN_CHUNK, LANES)), 0.0)

    rv = lambda w_: _inter_spec(w_, steps, reverse=True)
    return pl.pallas_call(
        body, name="delta_inter_bwd", grid=(steps,),
        in_specs=[rv(vdim)] * 5 + [rv(LANES), pl.BlockSpec((cpb, n_vh, LANES, LANES), lambda s: (steps - 1 - s, 0, 0, 0)),
                                   rv(vdim)],
        out_specs=[rv(vdim)] * 5 + [rv(LANES)],
        out_shape=[jax.ShapeDtypeStruct((t, vdim), F32)] * 5 + [jax.ShapeDtypeStruct((t, LANES), F32)],
        scratch_shapes=[pltpu.VMEM((n_vh, LANES, LANES), F32)],
        compiler_params=_delta_params("arbitrary"),
    )(u, w, qk, q_dec, k_dec, big_g, hist, d_o)


def delta_intra_bwd(qn, kn, qkv, v_blk, big_g, big_gt, beta, cots, n_vh):
    t, qk_w = qn.shape
    vdim = n_vh * LANES
    rep = vdim // qk_w
    nc = t // DN_CHUNK

    pairs = _intra_pairs(n_vh)

    def body(q_ref, k_ref, v_ref, g_ref, gt_ref, b_ref, du_ref, dw_ref, dqk_ref, dqd_ref, dkd_ref,
             dq_ref, dk_ref, dv_ref, dg_ref, dgt_ref, db_ref):
        lane = lax.broadcasted_iota(jnp.int32, (1, LANES), 1)
        _, vjp = jax.vjp(_delta_intra, *_intra_operands(pairs, rep, q_ref, k_ref, v_ref, g_ref, gt_ref, b_ref))
        dq, dk, dv, dgi, dgj, db = vjp((_pair_stack(pairs, du_ref), _pair_stack(pairs, dw_ref),
                                        _pair_stack(pairs, dqk_ref, DN_CHUNK), _pair_stack(pairs, dqd_ref),
                                        _pair_stack(pairs, dkd_ref)))
        dg_all, db_all = {}, {}
        dq_acc, dk_acc = None, None
        for i, (ci, h) in enumerate(pairs):
            j = h // rep
            dv_ref[_rows(ci), h * LANES:(h + 1) * LANES] = dv[i]
            dgt_ref[ci, h:h + 1, :] = dgj[i]
            onehot = (lane == h).astype(F32)
            dg_all[ci] = dgi[i] * onehot + dg_all.get(ci, 0.0)
            db_all[ci] = db[i] * onehot + db_all.get(ci, 0.0)
            dq_acc = dq[i] if h % rep == 0 else dq_acc + dq[i]
            dk_acc = dk[i] if h % rep == 0 else dk_acc + dk[i]
            if h % rep == rep - 1:
                dq_ref[_rows(ci), j * LANES:(j + 1) * LANES] = dq_acc
                dk_ref[_rows(ci), j * LANES:(j + 1) * LANES] = dk_acc
        for ci in dg_all:
            dg_ref[_rows(ci), :] = dg_all[ci]
            db_ref[_rows(ci), :] = db_all[ci]

    gt_spec = pl.BlockSpec((DN_INTRA_CHUNKS, n_vh, DN_CHUNK), lambda s: (s, 0, 0))
    return pl.pallas_call(
        body, name="delta_intra_bwd", grid=(nc // DN_INTRA_CHUNKS,),
        in_specs=[_intra_spec(qk_w), _intra_spec(qk_w), _intra_spec(vdim, v_blk), _intra_spec(LANES), gt_spec,
                  _intra_spec(LANES)] + [_intra_spec(vdim)] * 5,
        out_specs=[_intra_spec(qk_w), _intra_spec(qk_w), _intra_spec(vdim), _intra_spec(LANES), gt_spec,
                   _intra_spec(LANES)],
        out_shape=[jax.ShapeDtypeStruct((t, qk_w), F32), jax.ShapeDtypeStruct((t, qk_w), F32),
                   jax.ShapeDtypeStruct((t, vdim), F32), jax.ShapeDtypeStruct((t, LANES), F32),
                   jax.ShapeDtypeStruct((nc, n_vh, DN_CHUNK), F32), jax.ShapeDtypeStruct((t, LANES), F32)],
        compiler_params=_delta_params("parallel"),
    )(qn, kn, qkv, big_g, big_gt, beta, *cots)


def all_gather(name, arrs):
    n = len(arrs)

    def body(*refs):
        in_refs, out_refs, sems = refs[:n], refs[n:2 * n], refs[2 * n:]
        _exchange_copies(in_refs, out_refs, sems, False, "start")
        _exchange_copies(in_refs, out_refs, sems, False, "wait")

    res = pl.pallas_call(
        body, name=name,
        in_specs=[_HBM] * n, out_specs=[_HBM] * n,
        out_shape=_exchange_out_shape(arrs, False), scratch_shapes=_exchange_sems(n),
        compiler_params=pltpu.CompilerParams(has_side_effects=True),
    )(*arrs)
    return list(res)


def _adamw_math(w, g, m, v):
    m = ADAM_B1 * m + (1.0 - ADAM_B1) * g
    v = ADAM_B2 * v + (1.0 - ADAM_B2) * (g * g)
    m_hat = m / (1.0 - ADAM_B1 ** ADAM_STEP)
    v_hat = v / (1.0 - ADAM_B2 ** ADAM_STEP)
    delta = -ADAM_LR * (m_hat / (jnp.sqrt(v_hat) + ADAM_EPS) + ADAM_WD * w)
    return delta, m, v


def adamw(name, w, parts, m, v, rows_cap=128):
    r, c = w.shape
    np_ = parts.shape[0]
    tr = _tile(r, rows_cap, SUBLANES * (4 // parts.dtype.itemsize))

    def body(w_ref, p_ref, m_ref, v_ref, g_ref, d_ref, nm_ref, nv_ref):
        g = p_ref[0].astype(F32)
        for k in range(1, np_):
            g = g + p_ref[k].astype(F32)
        delta, nm, nv = _adamw_math(w_ref[...], g, m_ref[...], v_ref[...])
        g_ref[...] = g
        d_ref[...] = delta
        nm_ref[...] = nm
        nv_ref[...] = nv

    spec = pl.BlockSpec((tr, c), lambda i: (i, 0))
    return pl.pallas_call(
        body, name=name, grid=(r // tr,),
        in_specs=[spec, pl.BlockSpec((np_, tr, c), lambda i: (0, i, 0)), spec, spec],
        out_specs=[spec] * 4, out_shape=[jax.ShapeDtypeStruct((r, c), F32)] * 4,
        compiler_params=pltpu.CompilerParams(dimension_semantics=("parallel",), vmem_limit_bytes=VMEM_LIMIT),
    )(w, parts, m, v)


def sum_parts(name, parts, rows_cap=256):
    np_, r, c = parts.shape
    tr = _tile(r, rows_cap, SUBLANES)

    def body(p_ref, o_ref):
        g = p_ref[0].astype(F32)
        for k in range(1, np_):
            g = g + p_ref[k].astype(F32)
        o_ref[...] = g

    return pl.pallas_call(
        body, name=name, grid=(r // tr,),
        in_specs=[pl.BlockSpec((np_, tr, c), lambda i: (0, i, 0))],
        out_specs=pl.BlockSpec((tr, c), lambda i: (i, 0)),
        out_shape=jax.ShapeDtypeStruct((r, c), F32),
        compiler_params=pltpu.CompilerParams(dimension_semantics=("parallel",), vmem_limit_bytes=VMEM_LIMIT),
    )(parts)


def _pack(arrs):
    flat = jnp.concatenate([a.reshape(-1).astype(F32) for a in arrs])
    n = flat.shape[0]
    return jnp.pad(flat, (0, _round_up(n, LANES * SUBLANES) - n)).reshape(-1, LANES)


def _unpack(packed, like):
    flat, out, pos = packed.reshape(-1), [], 0
    for a in like:
        out.append(flat[pos:pos + a.size].reshape(a.shape))
        pos += a.size
    return out


def kernel(x, c, w_ada, b_ada, w_in, rg_conv_w, rg_conv_b, rg_w_a, rg_b_a, rg_w_x, rg_b_x, rg_lambda, dn_conv_w, dn_a_log, dn_dt_bias, dn_norm_w, w_proj_a, w_proj_b, w_out, ln1_g, ln1_b, ffn_w_gate, ffn_w_up, ffn_conv_w, ffn_conv_b, ffn_w_down, ln2_g, ln2_b, loss_target, m_w_ada, m_b_ada, m_w_in, m_rg_conv_w, m_rg_conv_b, m_rg_w_a, m_rg_b_a, m_rg_w_x, m_rg_b_x, m_rg_lambda, m_dn_conv_w, m_dn_a_log, m_dn_dt_bias, m_dn_norm_w, m_w_proj_a, m_w_proj_b, m_w_out, m_ln1_g, m_ln1_b, m_ffn_w_gate, m_ffn_w_up, m_ffn_conv_w, m_ffn_conv_b, m_ffn_w_down, m_ln2_g, m_ln2_b, v_w_ada, v_b_ada, v_w_in, v_rg_conv_w, v_rg_conv_b, v_rg_w_a, v_rg_b_a, v_rg_w_x, v_rg_b_x, v_rg_lambda, v_dn_conv_w, v_dn_a_log, v_dn_dt_bias, v_dn_norm_w, v_w_proj_a, v_w_proj_b, v_w_out, v_ln1_g, v_ln1_b, v_ffn_w_gate, v_ffn_w_up, v_ffn_conv_w, v_ffn_conv_b, v_ffn_w_down, v_ln2_g, v_ln2_b):
    names = ['w_ada', 'b_ada', 'w_in', 'rg_conv_w', 'rg_conv_b', 'rg_w_a', 'rg_b_a', 'rg_w_x', 'rg_b_x', 'rg_lambda',
             'dn_conv_w', 'dn_a_log', 'dn_dt_bias', 'dn_norm_w', 'w_proj_a', 'w_proj_b', 'w_out', 'ln1_g', 'ln1_b',
             'ffn_w_gate', 'ffn_w_up', 'ffn_conv_w', 'ffn_conv_b', 'ffn_w_down', 'ln2_g', 'ln2_b']
    loc = locals()
    W = {n: loc[n][0] for n in names}
    M = {n: loc['m_' + n][0] for n in names}
    V = {n: loc['v_' + n][0] for n in names}

    me = 4 * lax.axis_index("x") + 2 * lax.axis_index("y") + lax.axis_index("c")
    xs, tgt = x[0], loss_target[0]
    t, d = xs.shape
    d_rnn = W['rg_conv_b'].shape[0]
    n_blk = W['rg_w_a'].shape[0]
    n_vh = W['dn_a_log'].shape[0]
    assert W['dn_norm_w'].shape[0] == LANES
    vdim = n_vh * LANES
    d_ff = W['ffn_conv_b'].shape[0]
    d_in = W['w_in'].shape[1] * N_DEV
    qk = (d_in - 2 * d_rnn - 2 * vdim - 2 * n_vh - 2 * d) // 2
    assert vdim == 2 * qk and qk % LANES == 0 and n_vh <= LANES
    splits = (d_rnn, d_rnn, qk, qk, vdim, vdim, n_vh, n_vh, d, d)
    offs = [0]
    for s_ in splits:
        offs.append(offs[-1] + s_)

    tb = _tile(t, 256, SUBLANES)

    big = ['w_in', 'w_proj_a', 'w_proj_b', 'w_out', 'ffn_w_gate', 'ffn_w_up', 'ffn_w_down']
    small_sh = ['rg_conv_w', 'dn_conv_w', 'ffn_conv_w']
    first = all_gather("gather_first", [W['w_in'].astype(WIRE_DTYPE)] + [W[n] for n in small_sh] + [c])
    g_in, g_rcw, g_dcw, g_fcw, c_all = first
    cols = lambda g: jnp.transpose(g, (1, 0, 2)).reshape(g.shape[1], -1)
    rows = lambda g: g.reshape(-1, g.shape[2])
    w_in_f = cols(g_in)
    padl = lambda a: jnp.pad(a, ((0, 0), (0, LANES - a.shape[1])))
    groups = [w_in_f[:, offs[i]:offs[i + 1]] for i in range(10)]
    groups[6], groups[7] = padl(groups[6]), padl(groups[7])
    go = [0]
    for g_ in groups:
        go.append(go[-1] + g_.shape[1])
    n_pad = _round_up(go[-1], 512)
    wp = jnp.pad(jnp.concatenate(groups, axis=1), ((0, 0), (0, n_pad - go[-1])))
    o_xr, o_gr, o_q, o_k, o_v, o_z, o_a, o_b, o_ga, o_gb = go[:10]
    rcw, dcw, fcw = cols(g_rcw), cols(g_dcw), cols(g_fcw)
    eye_b = jnp.eye(n_blk, dtype=F32)
    bd = lambda w: (w[:, :, None, :] * eye_b[:, None, :, None]).reshape(d_rnn, d_rnn)
    w_bd = jnp.concatenate([bd(W['rg_w_a']), bd(W['rg_w_x'])], axis=1)
    row1 = lambda a: a.reshape(1, -1)
    padv = lambda a: jnp.pad(row1(a), ((0, 0), (0, LANES - a.shape[0])))
    nw_t = jnp.tile(row1(W['dn_norm_w']), (1, n_vh))

    c_pad =jnp.pad(c_all.reshape(N_DEV, d), ((0, LANES - N_DEV), (0, 0)))
    ada_w = W['w_ada'].shape[1]
    b_ada_me = lax.dynamic_slice(W['b_ada'], (me * ada_w,), (ada_w,)).reshape(1, ada_w)
    ada_sh = mm(c_pad, W['w_ada'], name="ada_fwd", a_act="silu", bias=b_ada_me)
    (ada_all,) = all_gather("gather_ada", [ada_sh[:N_DEV]])
    ada_me = lax.dynamic_slice(ada_all, (0, me, 0), (N_DEV, 1, ada_w)).reshape(6, 1, d)
    sh1, sc1, gt1, sh2, sc2, gt2 = [ada_me[i] for i in range(6)]

    nt = t // tb

    def act(a, bw, col0=0, width=None, grad=True, rows=tb):
        width = a.shape[1] if width is None else width
        assert col0 % bw == 0 and width % bw == 0
        c0 = col0 // bw
        return In(a, (rows, bw), lambda o, s: (s, c0 + o), grad=grad, gshape=(t, width), gimap=lambda o, s: (s, o))

    def prm(a, bw, parts=None):
        return In(a, (a.shape[0], bw), lambda o, s: (0, o), acc=True, parts=parts)

    def out(width, bw, rows=tb):
        return Out((t, width), (rows, bw), lambda o, s: (s, o))

    tbh = _tile(t, 2048, SUBLANES)
    nth = t // tbh
    tbc = _tile(t, 1024, SUBLANES)
    ntc = t // tbc

    krows = lambda k_: [(slice(j, j + 1), slice(None)) for j in range(k_)]

    mod1_ins = [act(xs, d), prm(sc1, d), prm(sh1, d)]
    (h1,), _, (h1_t,) = stage_fwd("mod1_fwd", f_modulate, (1, nt), mod1_ins, [out(d, d)], transposed=[0])
    proj, g_pa, g_pb, g_out, g_fg, g_fu, g_fd = mm(h1, wp, name="proj_fwd",
                                                   gather=[W[n].astype(WIRE_DTYPE) for n in big[1:]])
    w_pa, w_pb, w_o, w_fd = rows(g_pa), rows(g_pb), rows(g_out), rows(g_fd)
    w_gate, w_up = cols(g_fg), cols(g_fu)
    w_gu = jnp.concatenate([w_gate, w_up], axis=1)

    cb_r = _tile(math.gcd(d_rnn, o_gr), 256)
    rgc_ins = [act(proj, cb_r, o_xr, d_rnn, rows=tbc), prm(rcw, cb_r, krows(4)), prm(row1(W['rg_conv_b']), cb_r)]
    rgc_grid, rgc_car, rgc_outs = (d_rnn // cb_r, ntc), [(SUBLANES, cb_r)], [out(d_rnn, cb_r, tbc)]
    (xc,), rgc_hist, (xc_t,) = stage_fwd("rg_conv_fwd", f_rg_conv, rgc_grid, rgc_ins, rgc_outs, rgc_car, transposed=[0])
    gates = mm(xc, w_bd, name="rg_gates_fwd")
    lru_ins = [act(xc, cb_r), act(gates, cb_r, 0, d_rnn), act(gates, cb_r, d_rnn, d_rnn), act(proj, cb_r, o_gr, d_rnn),
               prm(row1(W['rg_b_a']), cb_r), prm(row1(W['rg_b_x']), cb_r), prm(row1(W['rg_lambda']), cb_r)]
    lru_grid, lru_car = (d_rnn // cb_r, nt), [(1, cb_r)]
    (rec,), lru_hist, (rec_t,) = stage_fwd("rglru_fwd", f_rglru, lru_grid, lru_ins, [out(d_rnn, cb_r)], lru_car,
                                           transposed=[0])
    y_a = mm(rec, w_pa, name="proj_a_fwd")

    dnc = {}
    for nm, col0, width, w0, cb_, f_ in (("q", o_q, qk, 0, LANES, functools.partial(f_dn_conv_norm, LANES ** -0.5)),
                                         ("k", o_k, qk, qk, LANES, functools.partial(f_dn_conv_norm, 1.0)),
                                         ("v", o_v, vdim, 2 * qk, _tile(math.gcd(vdim, o_v), 256), f_dn_conv)):
        ins_ = [act(proj, cb_, col0, width, rows=tbc), prm(dcw[:, w0:w0 + width], cb_, krows(4))]
        grid_, outs_, car_ = (width // cb_, ntc), [out(width, cb_, tbc)], [(SUBLANES, cb_)]
        (y_,), hist_ = stage_fwd("dn_conv_%s_fwd" % nm, f_, grid_, ins_, outs_, car_)
        dnc[nm] = (y_, f_, ins_, grid_, outs_, car_, hist_)
    qn, kn, v_c = dnc["q"][0], dnc["k"][0], dnc["v"][0]
    gate_ins = [act(proj, LANES, o_a, LANES), act(proj, LANES, o_b, LANES),
                prm(padv(W['dn_a_log']), LANES), prm(padv(W['dn_dt_bias']), LANES)]
    gate_outs = [out(LANES, LANES), out(LANES, LANES)]
    (g_dn, beta_dn), _ = stage_fwd("dn_gates_fwd", f_dn_gates, (1, nt), gate_ins, gate_outs)
    n_ch = t // DN_CHUNK
    gt_dn = jnp.transpose(g_dn.reshape(n_ch, DN_CHUNK, LANES)[:, :, :n_vh], (0, 2, 1))
    dn_mid = delta_intra_fwd(qn, kn, v_c, 0, g_dn, gt_dn, beta_dn, n_vh)
    o_dn, dn_hist = delta_inter_fwd(*dn_mid, g_dn, n_vh)
    dno_ins = [act(o_dn, LANES, rows=tbh), act(proj, LANES, o_z, vdim, rows=tbh), prm(nw_t, LANES)]
    dno_grid, dno_outs = (n_vh, nth), [out(vdim, LANES, tbh)]
    (dn,), _, (dn_t,) = stage_fwd("dn_out_fwd", f_dn_out, dno_grid, dno_ins, dno_outs, transposed=[0])
    y_b = mm(dn, w_pb, name="proj_b_fwd")

    cb_m = _tile(math.gcd(math.gcd(d, o_ga), o_gb), 512)
    mrg_ins = [act(proj, cb_m, o_ga, d, rows=tbc), act(proj, cb_m, o_gb, d, rows=tbc), act(y_a, cb_m, rows=tbc),
               act(y_b, cb_m, rows=tbc)]
    mrg_grid, mrg_outs = (d // cb_m, ntc), [out(d, cb_m, tbc)]
    (merged,), _, (merged_t,) = stage_fwd("merge_fwd", f_merge, mrg_grid, mrg_ins, mrg_outs, transposed=[0])
    mix = mm(merged, w_o, name="w_out_fwd")
    ln1_ins = [act(xs, d), act(mix, d), prm(gt1, d), prm(row1(W['ln1_g']), d), prm(row1(W['ln1_b']), d),
               prm(sc2, d), prm(sh2, d)]
    ln1_outs = [out(d, d), out(d, d)]
    (x1, h2), _, (h2_t,) = stage_fwd("ln1_mod2_fwd", f_deepnorm_mod, (1, nt), ln1_ins, ln1_outs, transposed=[1])

    gu = mm(h2, w_gu, name="ffn_in_fwd")
    cb_f = _tile(d_ff, 256)
    ffa_ins = [act(gu, cb_f, 0, d_ff, rows=tbc), act(gu, cb_f, d_ff, d_ff, rows=tbc), prm(fcw, cb_f, krows(3)),
               prm(row1(W['ffn_conv_b']), cb_f)]
    ffa_grid, ffa_car, ffa_outs = (d_ff // cb_f, ntc), [(SUBLANES, cb_f)], [out(d_ff, cb_f, tbc)]
    (act_ff,), ffa_hist, (act_t,) = stage_fwd("ffn_act_fwd", f_ffn_act, ffa_grid, ffa_ins, ffa_outs, ffa_car,
                                              transposed=[0])
    ff = mm(act_ff, w_fd, name="ffn_down_fwd")
    ln2_ins = [act(x1, d), act(ff, d), prm(gt2, d), prm(row1(W['ln2_g']), d), prm(row1(W['ln2_b']), d),
               act(tgt, d, grad=False)]
    ln2_outs = [Out((t, 1), (tb, 1), lambda o, s: (s, 0))]
    (loss_rows,), _ = stage_fwd("ln2_loss_fwd", f_deepnorm_loss, (1, nt), ln2_ins, ln2_outs)

    dx1_a, d_ff_o, d_gt2, d_ln2g, d_ln2b = stage_bwd("ln2_loss_bwd", f_deepnorm_loss, (1, nt), ln2_ins, ln2_outs,
                                                     [jnp.ones((t, 1), F32)])
    d_act = mm(d_ff_o, w_fd, name="ffn_down_bwd_x", tb=True)
    gw_fd = mm(act_t, d_ff_o, name="ffn_down_bwd_w")
    d_gp, d_up, d_fcw, d_fcb = stage_bwd("ffn_act_bwd", f_ffn_act, ffa_grid, ffa_ins, ffa_outs, [d_act],
                                         ffa_car, ffa_hist, gdtypes={0: MXU_DTYPE, 1: MXU_DTYPE})
    col_blocks = lambda g: jnp.transpose(g.reshape(g.shape[0], N_DEV, -1), (1, 0, 2)).astype(WIRE_DTYPE)
    row_blocks = lambda g: g.reshape(N_DEV, -1, g.shape[1]).astype(WIRE_DTYPE)
    big_parts = {}
    d_h2, big_parts['ffn_w_down'] = mm([d_gp, d_up], [w_gate, w_up], name="ffn_in_bwd_x", tb=True,
                                       scatter=[row_blocks(gw_fd)])
    gw_gate, gw_up = mm(h2_t, d_gp, name="ffn_gate_bwd_w"), mm(h2_t, d_up, name="ffn_up_bwd_w")
    dx_a, d_mix, d_gt1, d_ln1g, d_ln1b, d_sc2, d_sh2 = stage_bwd("ln1_mod2_bwd", f_deepnorm_mod, (1, nt), ln1_ins,
                                                                 ln1_outs, [dx1_a, d_h2])
    d_merged = mm(d_mix, w_o, name="w_out_bwd_x", tb=True)
    gw_o = mm(merged_t, d_mix, name="w_out_bwd_w")
    d_ga, d_gb, d_ya, d_yb = stage_bwd("merge_bwd", f_merge, mrg_grid, mrg_ins, mrg_outs, [d_merged],
                                       gdtypes={0: MXU_DTYPE, 1: MXU_DTYPE})
    d_rec = mm(d_ya, w_pa, name="proj_a_bwd_x", tb=True)
    gw_pa = mm(rec_t, d_ya, name="proj_a_bwd_w")
    d_dn = mm(d_yb, w_pb, name="proj_b_bwd_x", tb=True)
    gw_pb = mm(dn_t, d_yb, name="proj_b_bwd_w")

    d_o, d_z, d_nwt = stage_bwd("dn_out_bwd", f_dn_out, dno_grid, dno_ins, dno_outs, [d_dn], gdtypes={1: MXU_DTYPE})
    *d_mid, d_g_state = delta_inter_bwd(*dn_mid, g_dn, dn_hist, d_o, n_vh)
    d_qn, d_kn, d_v, d_g_col, d_gt, d_beta = delta_intra_bwd(qn, kn, v_c, 0, g_dn, gt_dn, beta_dn, d_mid, n_vh)
    d_g_row = jnp.pad(jnp.transpose(d_gt, (0, 2, 1)).reshape(t, n_vh), ((0, 0), (0, LANES - n_vh)))
    d_a, d_b, d_alog, d_dtb = stage_bwd("dn_gates_bwd", f_dn_gates, (1, nt), gate_ins, gate_outs,
                                        [(d_g_state, d_g_col, d_g_row), d_beta], gdtypes={0: MXU_DTYPE, 1: MXU_DTYPE})
    d_win, d_dcw = {}, []
    for nm, cot in (("q", d_qn), ("k", d_kn), ("v", d_v)):
        _, f_, ins_, grid_, outs_, car_, hist_ = dnc[nm]
        d_win[nm], dw_ = stage_bwd("dn_conv_%s_bwd" % nm, f_, grid_, ins_, outs_, [cot], car_, hist_,
                                   gdtypes={0: MXU_DTYPE})
        d_dcw.append(dw_)
    d_dcw = jnp.concatenate(d_dcw, axis=1)

    d_xc_a, d_pr, d_pi, d_gr, d_ba, d_bx, d_lam = stage_bwd(
        "rglru_bwd", f_rglru, lru_grid, lru_ins, [out(d_rnn, cb_r)], [d_rec], lru_car, lru_hist,
        gdtypes={1: MXU_DTYPE, 2: MXU_DTYPE, 3: MXU_DTYPE})
    d_xc_b, big_parts['w_out'], big_parts['w_proj_a'], big_parts['w_proj_b'] = mm(
        [d_pr, d_pi], [w_bd[:, :d_rnn], w_bd[:, d_rnn:]], name="rg_gates_bwd_x", tb=True,
        scatter=[row_blocks(gw_o), row_blocks(gw_pa), row_blocks(gw_pb)])
    gw_bd_a, gw_bd_x = mm(xc_t, d_pr, name="rg_gate_a_bwd_w"), mm(xc_t, d_pi, name="rg_gate_x_bwd_w")
    d_xr, d_rcw, d_rcb = stage_bwd("rg_conv_bwd", f_rg_conv, rgc_grid, rgc_ins, rgc_outs, [(d_xc_a, d_xc_b)],
                                   rgc_car, rgc_hist, gdtypes={0: MXU_DTYPE})

    diag = lambda g: jnp.einsum('nimj,nm->nij', g.reshape(n_blk, d_rnn // n_blk, n_blk, d_rnn // n_blk), eye_b)
    small_names = ['rg_conv_w', 'rg_conv_b', 'rg_w_a', 'rg_b_a', 'rg_w_x', 'rg_b_x', 'rg_lambda', 'dn_conv_w',
                   'dn_a_log', 'dn_dt_bias', 'dn_norm_w', 'ln1_g', 'ln1_b', 'ffn_conv_w', 'ffn_conv_b', 'ln2_g', 'ln2_b']
    small_loc = {
        'rg_conv_w': d_rcw, 'rg_conv_b': d_rcb,
        'rg_w_a': diag(gw_bd_a), 'rg_b_a': d_ba, 'rg_w_x': diag(gw_bd_x), 'rg_b_x': d_bx,
        'rg_lambda': d_lam, 'dn_conv_w': d_dcw, 'dn_a_log': d_alog[:, :n_vh], 'dn_dt_bias': d_dtb[:, :n_vh],
        'dn_norm_w': jnp.sum(d_nwt.reshape(n_vh, LANES), axis=0), 'ln1_g': d_ln1g, 'ln1_b': d_ln1b,
        'ffn_conv_w': d_fcw, 'ffn_conv_b': d_fcb, 'ln2_g': d_ln2g, 'ln2_b': d_ln2b}
    small_list = [small_loc[n] for n in small_names]

    d_segs = [d_xr, d_gr, d_win["q"], d_win["k"], d_win["v"], d_z, d_a, d_b, d_ga, d_gb]
    riders = {4: ('ffn_w_gate', gw_gate), 5: ('ffn_w_up', gw_up)}
    gw_segs = []
    for i, dg in enumerate(d_segs):
        if i in riders:
            g_, big_parts[riders[i][0]] = mm(h1_t, dg, name="proj_bwd_w%d" % i, scatter=[col_blocks(riders[i][1])])
        else:
            g_ = mm(h1_t, dg, name="proj_bwd_w%d" % i)
        gw_segs.append(g_)
    gw_in = jnp.concatenate([g_[:, :splits[i]] for i, g_ in enumerate(gw_segs)], axis=1)
    half = len(d_segs) // 2
    d_h1_a, big_parts['w_in'] = mm(d_segs[:half], groups[:half], name="proj_bwd_x0", tb=True,
                                   scatter=[col_blocks(gw_in)], **MM_SPLIT_CAPS)
    d_h1_b, small_all = mm(d_segs[half:], groups[half:], name="proj_bwd_x1", tb=True,
                           gather=[_pack(small_list)], **MM_SPLIT_CAPS)
    grad_x, d_sc1, d_sh1 = stage_bwd("mod1_bwd", f_modulate, (1, nt), mod1_ins, [out(d, d)], [(d_h1_a, d_h1_b)],
                                     add_to={0: dx_a})

    g_small = dict(zip(small_names, _unpack(sum_parts("sum_small_grads", small_all), small_list)))
    d_ada_me = jnp.concatenate([d_sh1, d_sc1, d_gt1, d_sh2, d_sc2, d_gt2], axis=1)
    (d_ada_all,) = all_gather("gather_d_ada", [d_ada_me.reshape(-1, LANES)])
    g_small['b_ada'] = sum_parts("sum_d_ada", d_ada_all)
    small_names = ['b_ada'] + small_names
    d_ada_cols = lax.dynamic_slice(d_ada_all.reshape(N_DEV, 6 * d), (0, me * ada_w), (N_DEV, ada_w))
    d_ada_pad = jnp.pad(d_ada_cols, ((0, LANES - N_DEV), (0, 0)))
    gw_ada = mm(c_pad, d_ada_pad, name="ada_bwd_w", ta=True, a_act="silu")

    res = {}
    big_parts['w_ada'] = gw_ada[None]
    for n in ['w_ada'] + big:
        res[n] = adamw("adamw_" + n, W[n], big_parts[n], M[n], V[n])
    for n in small_sh:
        w_ = W[n].shape[1]
        g_small[n] = lax.dynamic_slice(g_small[n], (0, me * w_), (W[n].shape[0], w_))
    for n in small_names:
        g_small[n] = g_small[n].reshape(W[n].shape)
    pk = lambda dct: _pack([dct[n] for n in small_names])
    s_g, s_d, s_m, s_v = adamw("adamw_small", pk(W), pk(g_small)[None], pk(M), pk(V))
    like = [W[n] for n in small_names]
    for n, g_, d_, m_, v_ in zip(small_names, _unpack(s_g, like), _unpack(s_d, like), _unpack(s_m, like), _unpack(s_v, like)):
        res[n] = (g_, d_, m_, v_)

    loss = lax.psum(jnp.sum(loss_rows), ("x", "y", "c"))
    outs = [loss, grad_x[None]]
    for j in range(4):
        outs += [res[n][j].reshape(loc[n].shape) for n in names]
    return tuple(outs)
```

```python
import functools
import math

import jax
import jax.numpy as jnp
from jax import lax
from jax.experimental import pallas as pl
from jax.experimental.pallas import tpu as pltpu

F32 = jnp.float32
BF16 = jnp.bfloat16
MXU_DTYPE = BF16
WIRE_DTYPE = BF16
DN_DTYPE = BF16
HI = lax.Precision.HIGHEST
MESH = pl.DeviceIdType.MESH

N_DEV = 8
LANES = 128
SUBLANES = 8
VMEM_LIMIT = 56 * 1024 * 1024
MM_TM_CAP, MM_TN_CAP, MM_TK_CAP = 1536, 1536, 2048
MM_SPLIT_CAPS = dict(tm_cap=1024, tn_cap=1024, tk_cap=1024)

RG_C = 8.0
DN_CHUNK = 64
DN_HEAD_GROUP = 16
DN_INTER_CHUNKS = 4
DN_INTRA_CHUNKS = 2
LN_EPS = 1e-5
RMS_EPS = 1e-6
L2_EPS = 1e-6
DEPTH = 1
DEEPNORM_ALPHA = (2 * DEPTH) ** 0.25
ADAM_LR = 0.001
ADAM_B1 = 0.9
ADAM_B2 = 0.999
ADAM_EPS = 1e-08
ADAM_WD = 0.01
ADAM_STEP = 10


def _tile(n, cap, unit=LANES):
    best = None
    for t in range(unit, min(n, cap) + 1, unit):
        if n % t == 0:
            best = t
    return best if best is not None else n


def _round_up(n, m):
    return (n + m - 1) // m * m


_HBM = pl.BlockSpec(memory_space=pl.ANY)


def _exchange_sems(n):
    return [pltpu.SemaphoreType.DMA((n, N_DEV - 1)), pltpu.SemaphoreType.DMA((n, N_DEV - 1)),
            pltpu.SemaphoreType.DMA((n,))]


def _exchange_out_shape(arrs, scatter):
    return [jax.ShapeDtypeStruct(a.shape if scatter else (N_DEV,) + a.shape, a.dtype) for a in arrs]


def _exchange_copies(in_refs, out_refs, sems, scatter, phase):
    send_sems, recv_sems, local_sems = sems
    x, y, c = lax.axis_index("x"), lax.axis_index("y"), lax.axis_index("c")
    me = 4 * x + 2 * y + c
    peers = [(x ^ ((k >> 2) & 1), y ^ ((k >> 1) & 1), c ^ (k & 1)) for k in range(N_DEV)]
    row = [4 * p[0] + 2 * p[1] + p[2] for p in peers]
    n = len(in_refs)

    def local(i):
        return pltpu.make_async_copy(in_refs[i].at[me] if scatter else in_refs[i], out_refs[i].at[me], local_sems.at[i])

    def remote(i, k, src, dst_row, to):
        return pltpu.make_async_remote_copy(src_ref=src, dst_ref=out_refs[i].at[dst_row],
                                            send_sem=send_sems.at[i, k - 1], recv_sem=recv_sems.at[i, k - 1],
                                            device_id=to, device_id_type=MESH)

    if scatter:
        sends = [(i, k, in_refs[i].at[row[k]], me, peers[k]) for k in range(1, N_DEV) for i in range(n)]
        passed = []
    else:
        sends = [(i, k, in_refs[i], me, peers[k]) for k in (1, 2, 4, 6) for i in range(n)]
        passed = [(i, k + 1, out_refs[i].at[row[k]], row[k], peers[1]) for k in (2, 4, 6) for i in range(n)]
    arrival = lambda i, k: remote(i, k, in_refs[i].at[me] if scatter else in_refs[i], row[k], peers[k])

    if phase == "start":
        for i in range(n):
            local(i).start()
        for cp in sends:
            remote(*cp).start()
    else:
        for cp in passed:
            arrival(cp[0], cp[1] - 1).wait_recv()
            remote(*cp).start()
        waited = {(cp[0], cp[1] - 1) for cp in passed}
        for k in range(1, N_DEV):
            for i in range(n):
                if (i, k) not in waited:
                    arrival(i, k).wait_recv()
        for cp in sends + passed:
            remote(*cp).wait_send()
        for i in range(n):
            local(i).wait()


def mm(a, b, *, name, ta=False, tb=False, a_act=None, bias=None, out_dtype=F32,
       tm_cap=MM_TM_CAP, tn_cap=MM_TN_CAP, tk_cap=MM_TK_CAP, gather=(), scatter=()):
    a_segs = list(a) if isinstance(a, (list, tuple)) else [a]
    b_segs = list(b) if isinstance(b, (list, tuple)) else [b]
    ns = len(a_segs)
    assert ns == len(b_segs) and (ns == 1 or a_act is None)
    m = a_segs[0].shape[1] if ta else a_segs[0].shape[0]
    n = b_segs[0].shape[0] if tb else b_segs[0].shape[1]
    ks = [x.shape[0] if ta else x.shape[1] for x in a_segs]
    assert ks == [y.shape[1] if tb else y.shape[0] for y in b_segs], (ks, ta, tb)
    tm, tn = _tile(m, tm_cap), _tile(n, tn_cap)
    tks = [_tile(k_, tk_cap) for k_ in ks]
    cnt = [k_ // t_ for k_, t_ in zip(ks, tks)]
    lo = [sum(cnt[:s]) for s in range(ns)]
    nk = sum(cnt)
    grid = (m // tm, n // tn, nk)
    dims = (((0 if ta else 1,), (1 if tb else 0,)), ((), ()))
    xch = list(gather) + list(scatter)
    nx, ng = len(xch), len(gather)
    n_main = 2 * ns + (bias is not None)

    def body(*refs):
        a_refs, b_refs = refs[:ns], refs[ns:2 * ns]
        bias_ref = refs[2 * ns] if bias is not None else None
        x_in, o_ref, x_out = refs[n_main:n_main + nx], refs[n_main + nx], refs[n_main + nx + 1:n_main + 2 * nx + 1]
        rest = refs[n_main + 2 * nx + 1:]
        acc_ref = rest[0] if nk > 1 else None
        sems = rest[1 if nk > 1 else 0:]
        groups = []
        if ng:
            groups.append((x_in[:ng], x_out[:ng], sems[:3], False))
        if nx > ng:
            groups.append((x_in[ng:], x_out[ng:], sems[-3:], True))
        if nx:
            step = (pl.program_id(0) * grid[1] + pl.program_id(1)) * grid[2] + pl.program_id(2)

            @pl.when(step == 0)
            def _():
                for gi, go_, gs, sc in groups:
                    _exchange_copies(gi, go_, gs, sc, "start")
        kk = pl.program_id(2)

        def finish(r):
            if bias is not None:
                r = r + bias_ref[...]
            o_ref[...] = r.astype(o_ref.dtype)

        def segment(s):
            av = a_refs[s][...]
            if a_act == "silu":
                av = jax.nn.silu(av.astype(F32))
            prod = lax.dot_general(av.astype(MXU_DTYPE), b_refs[s][...].astype(MXU_DTYPE), dims,
                                   preferred_element_type=F32)
            if nk == 1:
                finish(prod)
                return
            opens, closes = lo[s] == 0, lo[s] + cnt[s] == nk
            if opens:
                @pl.when(kk == 0)
                def _():
                    acc_ref[...] = prod
            inner = [kk > 0] * opens + [kk < nk - 1] * closes
            if inner:
                @pl.when(functools.reduce(lambda p, q: p & q, inner))
                def _():
                    acc_ref[...] += prod
            else:
                acc_ref[...] += prod
            if closes:
                @pl.when(kk == nk - 1)
                def _():
                    finish(acc_ref[...] + prod)

        for s in range(ns):
            if ns == 1:
                segment(s)
            else:
                pl.when((kk >= lo[s]) & (kk < lo[s] + cnt[s]))(functools.partial(segment, s))

        if nx:
            @pl.when(step == grid[0] * grid[1] * grid[2] - 1)
            def _():
                for gi, go_, gs, sc in groups:
                    _exchange_copies(gi, go_, gs, sc, "wait")

    def seg_index(s):
        return lambda q: jnp.clip(q - lo[s], 0, cnt[s] - 1) if ns > 1 else q

    a_specs, b_specs = [], []
    for s in range(ns):
        qi, tk = seg_index(s), tks[s]
        a_specs.append(pl.BlockSpec((tk, tm), (lambda qi: lambda i, j, q: (qi(q), i))(qi)) if ta
                       else pl.BlockSpec((tm, tk), (lambda qi: lambda i, j, q: (i, qi(q)))(qi)))
        b_specs.append(pl.BlockSpec((tn, tk), (lambda qi: lambda i, j, q: (j, qi(q)))(qi)) if tb
                       else pl.BlockSpec((tk, tn), (lambda qi: lambda i, j, q: (qi(q), j))(qi)))
    in_specs, args = a_specs + b_specs, a_segs + b_segs
    if bias is not None:
        in_specs.append(pl.BlockSpec((1, tn), lambda i, j, q: (0, j)))
        args.append(bias)
    o_spec, o_shape = pl.BlockSpec((tm, tn), lambda i, j, q: (i, j)), jax.ShapeDtypeStruct((m, n), out_dtype)
    acc = [pltpu.VMEM((tm, tn), F32)] if nk > 1 else []
    if not nx:
        return pl.pallas_call(
            body, name=name, grid=grid, in_specs=in_specs, out_specs=o_spec, out_shape=o_shape, scratch_shapes=acc,
            compiler_params=pltpu.CompilerParams(dimension_semantics=("parallel", "parallel", "arbitrary"),
                                                 vmem_limit_bytes=VMEM_LIMIT),
        )(*args)
    return pl.pallas_call(
        body, name=name, grid=grid, in_specs=in_specs + [_HBM] * nx, out_specs=[o_spec] + [_HBM] * nx,
        out_shape=[o_shape] + _exchange_out_shape(list(gather), False) + _exchange_out_shape(list(scatter), True),
        scratch_shapes=acc + (_exchange_sems(ng) if ng else []) + (_exchange_sems(nx - ng) if nx > ng else []),
        compiler_params=pltpu.CompilerParams(dimension_semantics=("arbitrary", "arbitrary", "arbitrary"),
                                             vmem_limit_bytes=VMEM_LIMIT, has_side_effects=True),
    )(*args, *xch)


class In:
    def __init__(self, arr, block, imap, acc=False, grad=True, parts=None, gshape=None, gimap=None):
        self.arr, self.block, self.imap, self.acc, self.grad, self.parts = arr, block, imap, acc, grad, parts
        self.gshape = arr.shape if gshape is None else gshape
        self.gimap = imap if gimap is None else gimap


class Out:
    def __init__(self, shape, block, imap, dtype=F32):
        self.shape, self.block, self.imap, self.dtype = shape, block, imap, dtype


def _load(in_refs, ins):
    vals = []
    for r, i in zip(in_refs, ins):
        if i.parts is None:
            vals.append(r[...])
        else:
            vals.extend(r[p] for p in i.parts)
    return vals


def _stage_params():
    return pltpu.CompilerParams(dimension_semantics=("parallel", "arbitrary"), vmem_limit_bytes=VMEM_LIMIT)


def stage_fwd(name, f, grid, ins, outs, carries=(), transposed=()):
    n_in, n_out, n_c, n_t = len(ins), len(outs), len(carries), len(transposed)

    def body(*refs):
        in_refs, out_refs = refs[:n_in], refs[n_in:n_in + n_out]
        hist_refs = refs[n_in + n_out:n_in + n_out + n_c]
        t_refs = refs[n_in + n_out + n_c:n_in + n_out + n_c + n_t]
        c_refs = refs[n_in + n_out + n_c + n_t:]
        if n_c:
            @pl.when(pl.program_id(1) == 0)
            def _():
                for c in c_refs:
                    c[...] = jnp.zeros_like(c)
        cin = [c[...] for c in c_refs]
        for h, c in zip(hist_refs, cin):
            h[...] = c
        o, cout = f(*_load(in_refs, ins), *cin)
        for r, v in zip(out_refs, o):
            r[...] = v.astype(r.dtype)
        for r, k in zip(t_refs, transposed):
            r[...] = o[k].T.astype(r.dtype)
        for c, v in zip(c_refs, cout):
            c[...] = v

    hist_spec = lambda c: pl.BlockSpec((None, None) + tuple(c), lambda o, s: (o, s) + (0,) * len(c))
    flip = lambda o_: pl.BlockSpec(o_.block[::-1], (lambda im: lambda o, s: im(o, s)[::-1])(o_.imap))
    res = pl.pallas_call(
        body, name=name, grid=grid,
        in_specs=[pl.BlockSpec(i.block, i.imap) for i in ins],
        out_specs=[pl.BlockSpec(o.block, o.imap) for o in outs] + [hist_spec(c) for c in carries]
        + [flip(outs[k]) for k in transposed],
        out_shape=[jax.ShapeDtypeStruct(o.shape, o.dtype) for o in outs]
        + [jax.ShapeDtypeStruct(tuple(grid) + tuple(c), F32) for c in carries]
        + [jax.ShapeDtypeStruct(outs[k].shape[::-1], MXU_DTYPE) for k in transposed],
        scratch_shapes=[pltpu.VMEM(tuple(c), F32) for c in carries],
        compiler_params=_stage_params(),
    )(*[i.arr for i in ins])
    res = list(res)
    if transposed:
        return res[:n_out], res[n_out:n_out + n_c], res[n_out + n_c:]
    return res[:n_out], res[n_out:]


def stage_bwd(name, f, grid, ins, outs, cots, carries=(), hists=(), add_to=None, gdtypes=None):
    n_in, n_out, n_c = len(ins), len(outs), len(carries)
    ns = grid[1]
    add_to = add_to or {}
    gdtypes = gdtypes or {}
    add_idx = sorted(add_to)
    g_idx = [k for k, i in enumerate(ins) if i.grad]
    cots = [c if isinstance(c, (tuple, list)) else (c,) for c in cots]
    n_cot = [len(c) for c in cots]
    rev = lambda imap: (lambda o, s: imap(o, ns - 1 - s))

    def body(*refs):
        p = 0
        in_refs = refs[p:p + n_in]; p += n_in
        cot_refs = []
        for cnt in n_cot:
            cot_refs.append(refs[p:p + cnt]); p += cnt
        hist_refs = refs[p:p + n_c]; p += n_c
        add_refs = refs[p:p + len(add_idx)]; p += len(add_idx)
        g_refs = refs[p:p + len(g_idx)]; p += len(g_idx)
        dc_refs = refs[p:]
        first = pl.program_id(1) == 0
        if n_c:
            @pl.when(first)
            def _():
                for c in dc_refs:
                    c[...] = jnp.zeros_like(c)
        vals = _load(in_refs, ins)
        cin = [h[...] for h in hist_refs]
        (o, cout), vjp = jax.vjp(lambda *a: f(*a), *vals, *cin)
        cot_o = []
        for crs, v in zip(cot_refs, o):
            c = crs[0][...].astype(v.dtype)
            for extra in crs[1:]:
                c = c + extra[...].astype(v.dtype)
            cot_o.append(c)
        cot_c = tuple(c[...] for c in dc_refs)
        grads = vjp((tuple(cot_o), cot_c))
        pos, per_in = 0, []
        for i in ins:
            cnt = 1 if i.parts is None else len(i.parts)
            per_in.append(grads[pos:pos + cnt])
            pos += cnt
        dcin = grads[pos:]
        for gr, k in zip(g_refs, g_idx):
            i, gs = ins[k], per_in[k]
            if i.acc:
                @pl.when(first)
                def _(gr=gr):
                    gr[...] = jnp.zeros_like(gr)
                if i.parts is None:
                    gr[...] += gs[0].astype(gr.dtype)
                else:
                    for pt, g in zip(i.parts, gs):
                        gr[pt] += g.astype(gr.dtype)
            else:
                g = gs[0]
                if k in add_to:
                    g = g + add_refs[add_idx.index(k)][...].astype(g.dtype)
                gr[...] = g.astype(gr.dtype)
        for c, v in zip(dc_refs, dcin):
            c[...] = v

    in_specs = [pl.BlockSpec(i.block, rev(i.imap)) for i in ins]
    for o_, cnt in zip(outs, n_cot):
        in_specs += [pl.BlockSpec(o_.block, rev(o_.imap))] * cnt
    in_specs += [pl.BlockSpec((None, None) + tuple(c), (lambda c: (lambda o, s: (o, ns - 1 - s) + (0,) * len(c)))(c))
                 for c in carries]
    in_specs += [pl.BlockSpec(ins[k].block, rev(ins[k].gimap)) for k in add_idx]
    out_specs, out_shape = [], []
    for k in g_idx:
        i = ins[k]
        if i.acc:
            out_specs.append(pl.BlockSpec(i.block, (lambda im: (lambda o, s: im(o, 0)))(i.imap)))
        else:
            out_specs.append(pl.BlockSpec(i.block, rev(i.gimap)))
        out_shape.append(jax.ShapeDtypeStruct(i.gshape, gdtypes.get(k, F32)))
    res = pl.pallas_call(
        body, name=name, grid=grid, in_specs=in_specs, out_specs=out_specs, out_shape=out_shape,
        scratch_shapes=[pltpu.VMEM(tuple(c), F32) for c in carries],
        compiler_params=_stage_params(),
    )(*[i.arr for i in ins], *[a for c in cots for a in c], *hists, *[add_to[k] for k in add_idx])
    return list(res)


def _iota_rows(shape):
    return lax.broadcasted_iota(jnp.int32, shape, 0)


@functools.partial(jax.custom_vjp, nondiff_argnums=(1,))
def _roll_rows(x, s):
    return pltpu.roll(x, s % x.shape[0], 0)


def _roll_rows_fwd(x, s):
    return _roll_rows(x, s), None


def _roll_rows_bwd(s, _, g):
    return (_roll_rows(g, -s),)


_roll_rows.defvjp(_roll_rows_fwd, _roll_rows_bwd)


@jax.custom_vjp
def _drop_head(xx):
    return xx[SUBLANES:]


def _drop_head_fwd(xx):
    return xx[SUBLANES:], None


def _drop_head_bwd(_, g):
    return (jnp.concatenate([jnp.zeros((SUBLANES, g.shape[1]), g.dtype), g], axis=0),)


_drop_head.defvjp(_drop_head_fwd, _drop_head_bwd)


@jax.custom_vjp
def _last_rows(x):
    return x[x.shape[0] - SUBLANES:]


def _last_rows_fwd(x):
    return x[x.shape[0] - SUBLANES:], x.shape[0]


def _last_rows_bwd(n, g):
    return (jnp.concatenate([jnp.zeros((n - SUBLANES, g.shape[1]), g.dtype), g], axis=0),)


_last_rows.defvjp(_last_rows_fwd, _last_rows_bwd)


def _last_row(x):
    n = x.shape[0]
    return jnp.sum(jnp.where(_iota_rows(x.shape) == n - 1, x, 0.0), axis=0, keepdims=True)


def _scan_steps(n):
    s = 1
    while s < n:
        yield s
        s *= 2


def _block_scan_log(a, u, h0):
    n = a.shape[0]
    row = _iota_rows(a.shape)
    for s in _scan_steps(n):
        keep = row >= s
        a_s = jnp.where(keep, pltpu.roll(a, s, 0), 1.0)
        u_s = jnp.where(keep, pltpu.roll(u, s, 0), 0.0)
        u = u + a * u_s
        a = a * a_s
    return u + a * h0


def _block_scan_impl(a, u, edge, reverse=False):
    n, c = a.shape
    nt = n // SUBLANES
    a, u = a.reshape(nt, SUBLANES, c), u.reshape(nt, SUBLANES, c)
    row = lax.broadcasted_iota(jnp.int32, a.shape, 1)
    for s in _scan_steps(SUBLANES):
        keep, shift = (row < SUBLANES - s, SUBLANES - s) if reverse else (row >= s, s)
        a_s = jnp.where(keep, pltpu.roll(a, shift, 1), 1.0)
        u_s = jnp.where(keep, pltpu.roll(u, shift, 1), 0.0)
        u = u + a * u_s
        a = a * a_s
    carry = jnp.broadcast_to(edge, (SUBLANES, c))
    tiles = [None] * nt
    at = 0 if reverse else SUBLANES - 1
    for i in (reversed(range(nt)) if reverse else range(nt)):
        tiles[i] = u[i] + a[i] * carry
        carry = jnp.broadcast_to(tiles[i][at:at + 1, :], (SUBLANES, c))
    return jnp.stack(tiles).reshape(n, c)


@jax.custom_vjp
def _block_scan(a, u, h0):
    return _block_scan_log(a, u, h0)


def _block_scan_fwd(a, u, h0):
    h = _block_scan_impl(a, u, h0)
    return h, (a, h, h0)


def _block_scan_bwd(res, dh):
    a, h, h0 = res
    n = a.shape[0]
    row = _iota_rows(a.shape)
    lam = _block_scan_impl(pltpu.roll(a, n - 1, 0), dh, jnp.zeros_like(h0), reverse=True)
    h_prev = jnp.where(row >= 1, pltpu.roll(h, 1, 0), jnp.broadcast_to(h0, h.shape))
    d_h0 = jnp.sum(jnp.where(row == 0, a * lam, 0.0), axis=0, keepdims=True)
    return lam * h_prev, lam, d_h0


_block_scan.defvjp(_block_scan_fwd, _block_scan_bwd)


def _dot_hi(a, b, dims=(((1,), (0,)), ((), ()))):
    return lax.dot_general(a, b, dims, precision=HI, preferred_element_type=F32)


_NN, _NT, _TN = "nn", "nt", "tn"
_CONTRACT = {_NN: (1, 0), _NT: (1, 1), _TN: (0, 0)}


def _raw_dot(a, b, kind):
    ca, cb = _CONTRACT[kind]
    lead = a.ndim - 2
    dims = (((ca + lead,), (cb + lead,)), (tuple(range(lead)), tuple(range(lead))))
    return lax.dot_general(a.astype(DN_DTYPE), b.astype(DN_DTYPE), dims, preferred_element_type=F32)


@jax.custom_vjp
def _nn(a, b):
    return _raw_dot(a, b, _NN)


_nn.defvjp(lambda a, b: (_raw_dot(a, b, _NN), (a, b)),
           lambda r, g: (_raw_dot(g, r[1], _NT), _raw_dot(r[0], g, _TN)))


@jax.custom_vjp
def _nt(a, b):
    return _raw_dot(a, b, _NT)


_nt.defvjp(lambda a, b: (_raw_dot(a, b, _NT), (a, b)),
           lambda r, g: (_raw_dot(g, r[1], _NN), _raw_dot(g, r[0], _TN)))


@jax.custom_vjp
def _tn(a, b):
    return _raw_dot(a, b, _TN)


_tn.defvjp(lambda a, b: (_raw_dot(a, b, _TN), (a, b)),
           lambda r, g: (_raw_dot(r[1], g, _NT), _raw_dot(r[0], g, _NN)))


def _neumann_inverse(a):
    n = a.shape[-1]
    eye = (lax.broadcasted_iota(jnp.int32, (n, n), 0) == lax.broadcasted_iota(jnp.int32, (n, n), 1)).astype(F32)
    p = _raw_dot(a, a, _NN)
    e = p
    for _ in range(int(math.log2(n)) - 2):
        p = _raw_dot(p, p, _NN)
        e = e + p + _raw_dot(e, p, _NN)
    return eye - a + e - _raw_dot(a, e, _NN)


@jax.custom_vjp
def _unit_lower_inverse(a):
    return _neumann_inverse(a)


def _unit_lower_inverse_fwd(a):
    x = _neumann_inverse(a)
    return x, x


def _unit_lower_inverse_bwd(x, g):
    return (-_raw_dot(_raw_dot(x, g, _TN), x, _NT),)


_unit_lower_inverse.defvjp(_unit_lower_inverse_fwd, _unit_lower_inverse_bwd)


def _softplus(x):
    return jnp.maximum(x, 0.0) + jnp.log1p(jnp.exp(-jnp.abs(x)))


def _neg_expm1(x):
    series = -x * (1.0 + x * (0.5 + x * (1.0 / 6.0 + x * (1.0 / 24.0 + x * (1.0 / 120.0)))))
    return jnp.where(x > -0.03, series, 1.0 - jnp.exp(x))


def f_modulate(x, sc, sh):
    return (x * (1.0 + sc) + sh,), ()


def _deepnorm(x, y, gt, g, b):
    v = DEEPNORM_ALPHA * x + (1.0 + gt) * y
    mu = jnp.mean(v, axis=-1, keepdims=True)
    vc = v - mu
    var = jnp.mean(vc * vc, axis=-1, keepdims=True)
    return vc * lax.rsqrt(var + LN_EPS) * g + b


def f_deepnorm_mod(x, y, gt, g, b, sc, sh):
    x1 = _deepnorm(x, y, gt, g, b)
    return (x1, x1 * (1.0 + sc) + sh), ()


def f_deepnorm_loss(x, y, gt, g, b, target):
    err = _deepnorm(x, y, gt, g, b) - target
    return (0.5 * jnp.mean(err * err, axis=-1, keepdims=True),), ()


def _causal_conv(x, prev, ws):
    xx = jnp.concatenate([prev, x], axis=0)
    k = len(ws)
    y = ws[k - 1] * x
    for j in range(k - 1):
        y = y + ws[j] * _drop_head(_roll_rows(xx, k - 1 - j))
    return y


def f_rg_conv(x, w0, w1, w2, w3, b, prev):
    return (_causal_conv(x, prev, (w0, w1, w2, w3)) + b,), (_last_rows(x),)


def f_dn_conv(x, w0, w1, w2, w3, prev):
    return (jax.nn.silu(_causal_conv(x, prev, (w0, w1, w2, w3))),), (_last_rows(x),)


def f_ffn_act(gp, up, w0, w1, w2, b, prev):
    return (jax.nn.gelu(_causal_conv(gp, prev, (w0, w1, w2)) + b) * up,), (_last_rows(gp),)


def f_rglru(xc, pre_r, pre_i, gr, b_a, b_x, lam, h0):
    gate_r = jax.nn.sigmoid(pre_r + b_a)
    gate_i = jax.nn.sigmoid(pre_i + b_x)
    log_a = -RG_C * gate_r * _softplus(-lam)
    a = jnp.exp(log_a)
    mult = jnp.sqrt(_neg_expm1(2.0 * log_a))
    h = _block_scan(a, mult * gate_i * xc, h0)
    return (h * jax.nn.gelu(gr),), (_last_row(h),)


def f_dn_conv_norm(scale, x, w0, w1, w2, w3, prev):
    y = jax.nn.silu(_causal_conv(x, prev, (w0, w1, w2, w3)))
    return (y * lax.rsqrt(jnp.sum(y * y, axis=-1, keepdims=True) + L2_EPS) * scale,), (_last_rows(x),)


def f_dn_gates(a_in, b_in, a_log, dt_bias):
    g = -jnp.exp(a_log) * _softplus(a_in + dt_bias)
    n = g.shape[0]
    shift = int(math.log2(DN_CHUNK))
    ri = lax.broadcasted_iota(jnp.int32, (n, n), 0)
    ci = lax.broadcasted_iota(jnp.int32, (n, n), 1)
    tri = ((lax.shift_right_logical(ri, shift) == lax.shift_right_logical(ci, shift)) & (ri >= ci)).astype(F32)
    return (_dot_hi(tri, g), jax.nn.sigmoid(b_in)), ()


def f_dn_out(o, z, nw):
    r = lax.rsqrt(jnp.mean(o * o, axis=-1, keepdims=True) + RMS_EPS)
    return (o * r * nw * jax.nn.silu(z),), ()


def f_merge(ga, gb, ya, yb):
    return (jax.nn.sigmoid(ga) * ya + jax.nn.sigmoid(gb) * yb,), ()


def _delta_intra(q, k, v, g_i, g_j, beta):
    c = q.shape[-2]
    ri = lax.broadcasted_iota(jnp.int32, (c, c), 0)
    ci = lax.broadcasted_iota(jnp.int32, (c, c), 1)
    decay = jnp.exp(jnp.where(ri >= ci, g_i - g_j, -jnp.inf))
    g_last = jnp.sum(jnp.where(_iota_rows((c, 1)) == c - 1, g_i, 0.0), axis=-2, keepdims=True)
    exp_g = jnp.exp(g_i)
    kb = k * beta
    t_inv = _unit_lower_inverse(jnp.where(ri > ci, _nt(kb, k) * decay, 0.0))
    u = _nn(t_inv, v * beta)
    w = _nn(t_inv, kb * exp_g)
    return u, w, _nt(q, k) * decay, q * exp_g, k * jnp.exp(g_last - g_i)


def _delta_inter(u, w, qk, q_dec, k_dec, g_last, state):
    v_new = u - _nn(w, state)
    o = _nn(q_dec, state) + _nn(qk, v_new)
    return o, jnp.exp(g_last) * state + _tn(k_dec, v_new)


def _delta_params(sem):
    return pltpu.CompilerParams(dimension_semantics=(sem,), vmem_limit_bytes=VMEM_LIMIT)


def _head_groups(n_vh):
    hb = min(DN_HEAD_GROUP, n_vh)
    return [range(h0, h0 + hb) for h0 in range(0, n_vh, hb)]


def _stack(hs, f):
    return jnp.stack([f(h) for h in hs])


def _rows(ci):
    return slice(ci * DN_CHUNK, (ci + 1) * DN_CHUNK)


def _intra_pairs(n_vh):
    return [(ci, h) for ci in range(DN_INTRA_CHUNKS) for h in range(n_vh)]


def _pair_stack(pairs, ref, width=LANES, head_of=lambda h: h):
    return jnp.stack([ref[_rows(ci), head_of(h) * LANES:head_of(h) * LANES + width] for ci, h in pairs])


def _intra_operands(pairs, rep, q_ref, k_ref, v_ref, g_ref, gt_ref, b_ref):
    qk_head = lambda h: h // rep
    return (_pair_stack(pairs, q_ref, head_of=qk_head), _pair_stack(pairs, k_ref, head_of=qk_head),
            _pair_stack(pairs, v_ref), jnp.stack([g_ref[_rows(ci), h:h + 1] for ci, h in pairs]),
            jnp.stack([gt_ref[ci, h:h + 1, :] for ci, h in pairs]),
            jnp.stack([b_ref[_rows(ci), h:h + 1] for ci, h in pairs]))


def _intra_spec(width, col=0):
    return pl.BlockSpec((DN_INTRA_CHUNKS * DN_CHUNK, width), lambda s: (s, col))


def _inter_operands(hs, ci, u_ref, w_ref, qk_ref, qd_ref, kd_ref, g_ref):
    f32 = lambda ref, width=LANES: _stack(hs, lambda h: ref[_rows(ci), h * LANES:h * LANES + width].astype(F32))
    last = (ci + 1) * DN_CHUNK - 1
    return (f32(u_ref), f32(w_ref), f32(qk_ref, DN_CHUNK), f32(qd_ref), f32(kd_ref),
            _stack(hs, lambda h: g_ref[last:last + 1, h:h + 1]))


def _inter_spec(width, steps, reverse=False):
    rows = DN_INTER_CHUNKS * DN_CHUNK
    return pl.BlockSpec((rows, width), (lambda s: (steps - 1 - s, 0)) if reverse else (lambda s: (s, 0)))


def delta_intra_fwd(qn, kn, qkv, v_blk, big_g, big_gt, beta, n_vh):
    t, qk_w = qn.shape
    vdim = n_vh * LANES
    rep = vdim // qk_w
    nc = t // DN_CHUNK

    pairs = _intra_pairs(n_vh)

    def body(q_ref, k_ref, v_ref, g_ref, gt_ref, b_ref, u_ref, w_ref, qk_ref, qd_ref, kd_ref):
        u, w, qk, qd, kd = _delta_intra(*_intra_operands(pairs, rep, q_ref, k_ref, v_ref, g_ref, gt_ref, b_ref))
        for i, (ci, h) in enumerate(pairs):
            at = (_rows(ci), slice(h * LANES, (h + 1) * LANES))
            u_ref[at] = u[i]
            w_ref[at] = w[i].astype(w_ref.dtype)
            qk_ref[at] = jnp.concatenate([qk[i], jnp.zeros_like(qk[i])], axis=1).astype(qk_ref.dtype)
            qd_ref[at] = qd[i].astype(qd_ref.dtype)
            kd_ref[at] = kd[i].astype(kd_ref.dtype)

    return pl.pallas_call(
        body, name="delta_intra_fwd", grid=(nc // DN_INTRA_CHUNKS,),
        in_specs=[_intra_spec(qk_w), _intra_spec(qk_w), _intra_spec(vdim, v_blk), _intra_spec(LANES),
                  pl.BlockSpec((DN_INTRA_CHUNKS, n_vh, DN_CHUNK), lambda s: (s, 0, 0)), _intra_spec(LANES)],
        out_specs=[_intra_spec(vdim)] * 5,
        out_shape=[jax.ShapeDtypeStruct((t, vdim), F32)] + [jax.ShapeDtypeStruct((t, vdim), DN_DTYPE)] * 4,
        compiler_params=_delta_params("parallel"),
    )(qn, kn, qkv, big_g, big_gt, beta)


def delta_inter_fwd(u, w, qk, q_dec, k_dec, big_g, n_vh):
    t, vdim = u.shape
    nc = t // DN_CHUNK

    cpb = DN_INTER_CHUNKS
    steps = nc // cpb

    def body(u_ref, w_ref, qk_ref, qd_ref, kd_ref, g_ref, o_ref, hist_ref, s_ref):
        @pl.when(pl.program_id(0) == 0)
        def _():
            s_ref[...] = jnp.zeros_like(s_ref)
        for ci in range(cpb):
            for hs in _head_groups(n_vh):
                grp = slice(hs[0], hs[-1] + 1)
                st = s_ref[grp]
                hist_ref[ci, grp] = st
                o, ns = _delta_inter(*_inter_operands(hs, ci, u_ref, w_ref, qk_ref, qd_ref, kd_ref, g_ref), st)
                for i, h in enumerate(hs):
                    o_ref[_rows(ci), h * LANES:(h + 1) * LANES] = o[i]
                s_ref[grp] = ns

    return pl.pallas_call(
        body, name="delta_inter_fwd", grid=(steps,),
        in_specs=[_inter_spec(vdim, steps)] * 5 + [_inter_spec(LANES, steps)],
        out_specs=[_inter_spec(vdim, steps), pl.BlockSpec((cpb, n_vh, LANES, LANES), lambda s: (s, 0, 0, 0))],
        out_shape=[jax.ShapeDtypeStruct((t, vdim), F32), jax.ShapeDtypeStruct((nc, n_vh, LANES, LANES), F32)],
        scratch_shapes=[pltpu.VMEM((n_vh, LANES, LANES), F32)],
        compiler_params=_delta_params("arbitrary"),
    )(u, w, qk, q_dec, k_dec, big_g)


def delta_inter_bwd(u, w, qk, q_dec, k_dec, big_g, hist, d_o, n_vh):
    t, vdim = u.shape
    nc = t // DN_CHUNK

    cpb = DN_INTER_CHUNKS
    steps = nc // cpb

    def body(u_ref, w_ref, qk_ref, qd_ref, kd_ref, g_ref, hist_ref, do_ref,
             du_ref, dw_ref, dqk_ref, dqd_ref, dkd_ref, dg_ref, ds_ref):
        @pl.when(pl.program_id(0) == 0)
        def _():
            ds_ref[...] = jnp.zeros_like(ds_ref)
        lane = lax.broadcasted_iota(jnp.int32, (1, LANES), 1)
        last = _iota_rows((DN_CHUNK, LANES)) == DN_CHUNK - 1
        for ci in reversed(range(cpb)):
            dgl_all = jnp.zeros((1, LANES), F32)
            for hs in _head_groups(n_vh):
                grp = slice(hs[0], hs[-1] + 1)
                prim = _inter_operands(hs, ci, u_ref, w_ref, qk_ref, qd_ref, kd_ref, g_ref) + (hist_ref[ci, grp],)
                _, vjp = jax.vjp(_delta_inter, *prim)
                cot_o = _stack(hs, lambda h: do_ref[_rows(ci), h * LANES:(h + 1) * LANES])
                du, dw, dqk, dqd, dkd, dgl, dst = vjp((cot_o, ds_ref[grp]))
                ds_ref[grp] = dst
                for i, h in enumerate(hs):
                    sl = slice(h * LANES, (h + 1) * LANES)
                    du_ref[_rows(ci), sl] = du[i]
                    dw_ref[_rows(ci), sl] = dw[i]
                    dqk_ref[_rows(ci), sl] = jnp.concatenate([dqk[i], jnp.zeros_like(dqk[i])], axis=1)
                    dqd_ref[_rows(ci), sl] = dqd[i]
                    dkd_ref[_rows(ci), sl] = dkd[i]
                    dgl_all = dgl_all + dgl[i] * (lane == h).astype(F32)
            dg_ref[_rows(ci), :] = jnp.where(last, jnp.broadcast_to(dgl_all, (DN_CHUNK, LANES)), 0.0)

    rv = lambda w_: _inter_spec(w_, steps, reverse=True)
    return pl.pallas_call(
        body, name="delta_inter_bwd", grid=(steps,),
        in_specs=[rv(vdim)] * 5 + [rv(LANES), pl.BlockSpec((cpb, n_vh, LANES, LANES), lambda s: (steps - 1 - s, 0, 0, 0)),
                                   rv(vdim)],
        out_specs=[rv(vdim)] * 5 + [rv(LANES)],
        out_shape=[jax.ShapeDtypeStruct((t, vdim), F32)] * 5 + [jax.ShapeDtypeStruct((t, LANES), F32)],
        scratch_shapes=[pltpu.VMEM((n_vh, LANES, LANES), F32)],
        compiler_params=_delta_params("arbitrary"),
    )(u, w, qk, q_dec, k_dec, big_g, hist, d_o)


def delta_intra_bwd(qn, kn, qkv, v_blk, big_g, big_gt, beta, cots, n_vh):
    t, qk_w = qn.shape
    vdim = n_vh * LANES
    rep = vdim // qk_w
    nc = t // DN_CHUNK

    pairs = _intra_pairs(n_vh)

    def body(q_ref, k_ref, v_ref, g_ref, gt_ref, b_ref, du_ref, dw_ref, dqk_ref, dqd_ref, dkd_ref,
             dq_ref, dk_ref, dv_ref, dg_ref, dgt_ref, db_ref):
        lane = lax.broadcasted_iota(jnp.int32, (1, LANES), 1)
        _, vjp = jax.vjp(_delta_intra, *_intra_operands(pairs, rep, q_ref, k_ref, v_ref, g_ref, gt_ref, b_ref))
        dq, dk, dv, dgi, dgj, db = vjp((_pair_stack(pairs, du_ref), _pair_stack(pairs, dw_ref),
                                        _pair_stack(pairs, dqk_ref, DN_CHUNK), _pair_stack(pairs, dqd_ref),
                                        _pair_stack(pairs, dkd_ref)))
        dg_all, db_all = {}, {}
        dq_acc, dk_acc = None, None
        for i, (ci, h) in enumerate(pairs):
            j = h // rep
            dv_ref[_rows(ci), h * LANES:(h + 1) * LANES] = dv[i]
            dgt_ref[ci, h:h + 1, :] = dgj[i]
            onehot = (lane == h).astype(F32)
            dg_all[ci] = dgi[i] * onehot + dg_all.get(ci, 0.0)
            db_all[ci] = db[i] * onehot + db_all.get(ci, 0.0)
            dq_acc = dq[i] if h % rep == 0 else dq_acc + dq[i]
            dk_acc = dk[i] if h % rep == 0 else dk_acc + dk[i]
            if h % rep == rep - 1:
                dq_ref[_rows(ci), j * LANES:(j + 1) * LANES] = dq_acc
                dk_ref[_rows(ci), j * LANES:(j + 1) * LANES] = dk_acc
        for ci in dg_all:
            dg_ref[_rows(ci), :] = dg_all[ci]
            db_ref[_rows(ci), :] = db_all[ci]

    gt_spec = pl.BlockSpec((DN_INTRA_CHUNKS, n_vh, DN_CHUNK), lambda s: (s, 0, 0))
    return pl.pallas_call(
        body, name="delta_intra_bwd", grid=(nc // DN_INTRA_CHUNKS,),
        in_specs=[_intra_spec(qk_w), _intra_spec(qk_w), _intra_spec(vdim, v_blk), _intra_spec(LANES), gt_spec,
                  _intra_spec(LANES)] + [_intra_spec(vdim)] * 5,
        out_specs=[_intra_spec(qk_w), _intra_spec(qk_w), _intra_spec(vdim), _intra_spec(LANES), gt_spec,
                   _intra_spec(LANES)],
        out_shape=[jax.ShapeDtypeStruct((t, qk_w), F32), jax.ShapeDtypeStruct((t, qk_w), F32),
                   jax.ShapeDtypeStruct((t, vdim), F32), jax.ShapeDtypeStruct((t, LANES), F32),
                   jax.ShapeDtypeStruct((nc, n_vh, DN_CHUNK), F32), jax.ShapeDtypeStruct((t, LANES), F32)],
        compiler_params=_delta_params("parallel"),
    )(qn, kn, qkv, big_g, big_gt, beta, *cots)


def all_gather(name, arrs):
    n = len(arrs)

    def body(*refs):
        in_refs, out_refs, sems = refs[:n], refs[n:2 * n], refs[2 * n:]
        _exchange_copies(in_refs, out_refs, sems, False, "start")
        _exchange_copies(in_refs, out_refs, sems, False, "wait")

    res = pl.pallas_call(
        body, name=name,
        in_specs=[_HBM] * n, out_specs=[_HBM] * n,
        out_shape=_exchange_out_shape(arrs, False), scratch_shapes=_exchange_sems(n),
        compiler_params=pltpu.CompilerParams(has_side_effects=True),
    )(*arrs)
    return list(res)


def _adamw_math(w, g, m, v):
    m = ADAM_B1 * m + (1.0 - ADAM_B1) * g
    v = ADAM_B2 * v + (1.0 - ADAM_B2) * (g * g)
    m_hat = m / (1.0 - ADAM_B1 ** ADAM_STEP)
    v_hat = v / (1.0 - ADAM_B2 ** ADAM_STEP)
    delta = -ADAM_LR * (m_hat / (jnp.sqrt(v_hat) + ADAM_EPS) + ADAM_WD * w)
    return delta, m, v


def adamw(name, w, parts, m, v, rows_cap=128):
    r, c = w.shape
    np_ = parts.shape[0]
    tr = _tile(r, rows_cap, SUBLANES * (4 // parts.dtype.itemsize))

    def body(w_ref, p_ref, m_ref, v_ref, g_ref, d_ref, nm_ref, nv_ref):
        g = p_ref[0].astype(F32)
        for k in range(1, np_):
            g = g + p_ref[k].astype(F32)
        delta, nm, nv = _adamw_math(w_ref[...], g, m_ref[...], v_ref[...])
        g_ref[...] = g
        d_ref[...] = delta
        nm_ref[...] = nm
        nv_ref[...] = nv

    spec = pl.BlockSpec((tr, c), lambda i: (i, 0))
    return pl.pallas_call(
        body, name=name, grid=(r // tr,),
        in_specs=[spec, pl.BlockSpec((np_, tr, c), lambda i: (0, i, 0)), spec, spec],
        out_specs=[spec] * 4, out_shape=[jax.ShapeDtypeStruct((r, c), F32)] * 4,
        compiler_params=pltpu.CompilerParams(dimension_semantics=("parallel",), vmem_limit_bytes=VMEM_LIMIT),
    )(w, parts, m, v)


def sum_parts(name, parts, rows_cap=256):
    np_, r, c = parts.shape
    tr = _tile(r, rows_cap, SUBLANES)

    def body(p_ref, o_ref):
        g = p_ref[0].astype(F32)
        for k in range(1, np_):
            g = g + p_ref[k].astype(F32)
        o_ref[...] = g

    return pl.pallas_call(
        body, name=name, grid=(r // tr,),
        in_specs=[pl.BlockSpec((np_, tr, c), lambda i: (0, i, 0))],
        out_specs=pl.BlockSpec((tr, c), lambda i: (i, 0)),
        out_shape=jax.ShapeDtypeStruct((r, c), F32),
        compiler_params=pltpu.CompilerParams(dimension_semantics=("parallel",), vmem_limit_bytes=VMEM_LIMIT),
    )(parts)


def _pack(arrs):
    flat = jnp.concatenate([a.reshape(-1).astype(F32) for a in arrs])
    n = flat.shape[0]
    return jnp.pad(flat, (0, _round_up(n, LANES * SUBLANES) - n)).reshape(-1, LANES)


def _unpack(packed, like):
    flat, out, pos = packed.reshape(-1), [], 0
    for a in like:
        out.append(flat[pos:pos + a.size].reshape(a.shape))
        pos += a.size
    return out


def kernel(x, c, w_ada, b_ada, w_in, rg_conv_w, rg_conv_b, rg_w_a, rg_b_a, rg_w_x, rg_b_x, rg_lambda, dn_conv_w, dn_a_log, dn_dt_bias, dn_norm_w, w_proj_a, w_proj_b, w_out, ln1_g, ln1_b, ffn_w_gate, ffn_w_up, ffn_conv_w, ffn_conv_b, ffn_w_down, ln2_g, ln2_b, loss_target, m_w_ada, m_b_ada, m_w_in, m_rg_conv_w, m_rg_conv_b, m_rg_w_a, m_rg_b_a, m_rg_w_x, m_rg_b_x, m_rg_lambda, m_dn_conv_w, m_dn_a_log, m_dn_dt_bias, m_dn_norm_w, m_w_proj_a, m_w_proj_b, m_w_out, m_ln1_g, m_ln1_b, m_ffn_w_gate, m_ffn_w_up, m_ffn_conv_w, m_ffn_conv_b, m_ffn_w_down, m_ln2_g, m_ln2_b, v_w_ada, v_b_ada, v_w_in, v_rg_conv_w, v_rg_conv_b, v_rg_w_a, v_rg_b_a, v_rg_w_x, v_rg_b_x, v_rg_lambda, v_dn_conv_w, v_dn_a_log, v_dn_dt_bias, v_dn_norm_w, v_w_proj_a, v_w_proj_b, v_w_out, v_ln1_g, v_ln1_b, v_ffn_w_gate, v_ffn_w_up, v_ffn_conv_w, v_ffn_conv_b, v_ffn_w_down, v_ln2_g, v_ln2_b):
    names = ['w_ada', 'b_ada', 'w_in', 'rg_conv_w', 'rg_conv_b', 'rg_w_a', 'rg_b_a', 'rg_w_x', 'rg_b_x', 'rg_lambda',
             'dn_conv_w', 'dn_a_log', 'dn_dt_bias', 'dn_norm_w', 'w_proj_a', 'w_proj_b', 'w_out', 'ln1_g', 'ln1_b',
             'ffn_w_gate', 'ffn_w_up', 'ffn_conv_w', 'ffn_conv_b', 'ffn_w_down', 'ln2_g', 'ln2_b']
    loc = locals()
    W = {n: loc[n][0] for n in names}
    M = {n: loc['m_' + n][0] for n in names}
    V = {n: loc['v_' + n][0] for n in names}

    me = 4 * lax.axis_index("x") + 2 * lax.axis_index("y") + lax.axis_index("c")
    xs, tgt = x[0], loss_target[0]
    t, d = xs.shape
    d_rnn = W['rg_conv_b'].shape[0]
    n_blk = W['rg_w_a'].shape[0]
    n_vh = W['dn_a_log'].shape[0]
    assert W['dn_norm_w'].shape[0] == LANES
    vdim = n_vh * LANES
    d_ff = W['ffn_conv_b'].shape[0]
    d_in = W['w_in'].shape[1] * N_DEV
    qk = (d_in - 2 * d_rnn - 2 * vdim - 2 * n_vh - 2 * d) // 2
    assert vdim == 2 * qk and qk % LANES == 0 and n_vh <= LANES
    splits = (d_rnn, d_rnn, qk, qk, vdim, vdim, n_vh, n_vh, d, d)
    offs = [0]
    for s_ in splits:
        offs.append(offs[-1] + s_)

    tb = _tile(t, 256, SUBLANES)

    big = ['w_in', 'w_proj_a', 'w_proj_b', 'w_out', 'ffn_w_gate', 'ffn_w_up', 'ffn_w_down']
    small_sh = ['rg_conv_w', 'dn_conv_w', 'ffn_conv_w']
    first = all_gather("gather_first", [W['w_in'].astype(WIRE_DTYPE)] + [W[n] for n in small_sh] + [c])
    g_in, g_rcw, g_dcw, g_fcw, c_all = first
    cols = lambda g: jnp.transpose(g, (1, 0, 2)).reshape(g.shape[1], -1)
    rows = lambda g: g.reshape(-1, g.shape[2])
    w_in_f = cols(g_in)
    padl = lambda a: jnp.pad(a, ((0, 0), (0, LANES - a.shape[1])))
    groups = [w_in_f[:, offs[i]:offs[i + 1]] for i in range(10)]
    groups[6], groups[7] = padl(groups[6]), padl(groups[7])
    go = [0]
    for g_ in groups:
        go.append(go[-1] + g_.shape[1])
    n_pad = _round_up(go[-1], 512)
    wp = jnp.pad(jnp.concatenate(groups, axis=1), ((0, 0), (0, n_pad - go[-1])))
    o_xr, o_gr, o_q, o_k, o_v, o_z, o_a, o_b, o_ga, o_gb = go[:10]
    rcw, dcw, fcw = cols(g_rcw), cols(g_dcw), cols(g_fcw)
    eye_b = jnp.eye(n_blk, dtype=F32)
    bd = lambda w: (w[:, :, None, :] * eye_b[:, None, :, None]).reshape(d_rnn, d_rnn)
    w_bd = jnp.concatenate([bd(W['rg_w_a']), bd(W['rg_w_x'])], axis=1)
    row1 = lambda a: a.reshape(1, -1)
    padv = lambda a: jnp.pad(row1(a), ((0, 0), (0, LANES - a.shape[0])))
    nw_t = jnp.tile(row1(W['dn_norm_w']), (1, n_vh))

    c_pad =jnp.pad(c_all.reshape(N_DEV, d), ((0, LANES - N_DEV), (0, 0)))
    ada_w = W['w_ada'].shape[1]
    b_ada_me = lax.dynamic_slice(W['b_ada'], (me * ada_w,), (ada_w,)).reshape(1, ada_w)
    ada_sh = mm(c_pad, W['w_ada'], name="ada_fwd", a_act="silu", bias=b_ada_me)
    (ada_all,) = all_gather("gather_ada", [ada_sh[:N_DEV]])
    ada_me = lax.dynamic_slice(ada_all, (0, me, 0), (N_DEV, 1, ada_w)).reshape(6, 1, d)
    sh1, sc1, gt1, sh2, sc2, gt2 = [ada_me[i] for i in range(6)]

    nt = t // tb

    def act(a, bw, col0=0, width=None, grad=True, rows=tb):
        width = a.shape[1] if width is None else width
        assert col0 % bw == 0 and width % bw == 0
        c0 = col0 // bw
        return In(a, (rows, bw), lambda o, s: (s, c0 + o), grad=grad, gshape=(t, width), gimap=lambda o, s: (s, o))

    def prm(a, bw, parts=None):
        return In(a, (a.shape[0], bw), lambda o, s: (0, o), acc=True, parts=parts)

    def out(width, bw, rows=tb):
        return Out((t, width), (rows, bw), lambda o, s: (s, o))

    tbh = _tile(t, 2048, SUBLANES)
    nth = t // tbh
    tbc = _tile(t, 1024, SUBLANES)
    ntc = t // tbc

    krows = lambda k_: [(slice(j, j + 1), slice(None)) for j in range(k_)]

    mod1_ins = [act(xs, d), prm(sc1, d), prm(sh1, d)]
    (h1,), _, (h1_t,) = stage_fwd("mod1_fwd", f_modulate, (1, nt), mod1_ins, [out(d, d)], transposed=[0])
    proj, g_pa, g_pb, g_out, g_fg, g_fu, g_fd = mm(h1, wp, name="proj_fwd",
                                                   gather=[W[n].astype(WIRE_DTYPE) for n in big[1:]])
    w_pa, w_pb, w_o, w_fd = rows(g_pa), rows(g_pb), rows(g_out), rows(g_fd)
    w_gate, w_up = cols(g_fg), cols(g_fu)
    w_gu = jnp.concatenate([w_gate, w_up], axis=1)

    cb_r = _tile(math.gcd(d_rnn, o_gr), 256)
    rgc_ins = [act(proj, cb_r, o_xr, d_rnn, rows=tbc), prm(rcw, cb_r, krows(4)), prm(row1(W['rg_conv_b']), cb_r)]
    rgc_grid, rgc_car, rgc_outs = (d_rnn // cb_r, ntc), [(SUBLANES, cb_r)], [out(d_rnn, cb_r, tbc)]
    (xc,), rgc_hist, (xc_t,) = stage_fwd("rg_conv_fwd", f_rg_conv, rgc_grid, rgc_ins, rgc_outs, rgc_car, transposed=[0])
    gates = mm(xc, w_bd, name="rg_gates_fwd")
    lru_ins = [act(xc, cb_r), act(gates, cb_r, 0, d_rnn), act(gates, cb_r, d_rnn, d_rnn), act(proj, cb_r, o_gr, d_rnn),
               prm(row1(W['rg_b_a']), cb_r), prm(row1(W['rg_b_x']), cb_r), prm(row1(W['rg_lambda']), cb_r)]
    lru_grid, lru_car = (d_rnn // cb_r, nt), [(1, cb_r)]
    (rec,), lru_hist, (rec_t,) = stage_fwd("rglru_fwd", f_rglru, lru_grid, lru_ins, [out(d_rnn, cb_r)], lru_car,
                                           transposed=[0])
    y_a = mm(rec, w_pa, name="proj_a_fwd")

    dnc = {}
    for nm, col0, width, w0, cb_, f_ in (("q", o_q, qk, 0, LANES, functools.partial(f_dn_conv_norm, LANES ** -0.5)),
                                         ("k", o_k, qk, qk, LANES, functools.partial(f_dn_conv_norm, 1.0)),
                                         ("v", o_v, vdim, 2 * qk, _tile(math.gcd(vdim, o_v), 256), f_dn_conv)):
        ins_ = [act(proj, cb_, col0, width, rows=tbc), prm(dcw[:, w0:w0 + width], cb_, krows(4))]
        grid_, outs_, car_ = (width // cb_, ntc), [out(width, cb_, tbc)], [(SUBLANES, cb_)]
        (y_,), hist_ = stage_fwd("dn_conv_%s_fwd" % nm, f_, grid_, ins_, outs_, car_)
        dnc[nm] = (y_, f_, ins_, grid_, outs_, car_, hist_)
    qn, kn, v_c = dnc["q"][0], dnc["k"][0], dnc["v"][0]
    gate_ins = [act(proj, LANES, o_a, LANES), act(proj, LANES, o_b, LANES),
                prm(padv(W['dn_a_log']), LANES), prm(padv(W['dn_dt_bias']), LANES)]
    gate_outs = [out(LANES, LANES), out(LANES, LANES)]
    (g_dn, beta_dn), _ = stage_fwd("dn_gates_fwd", f_dn_gates, (1, nt), gate_ins, gate_outs)
    n_ch = t // DN_CHUNK
    gt_dn = jnp.transpose(g_dn.reshape(n_ch, DN_CHUNK, LANES)[:, :, :n_vh], (0, 2, 1))
    dn_mid = delta_intra_fwd(qn, kn, v_c, 0, g_dn, gt_dn, beta_dn, n_vh)
    o_dn, dn_hist = delta_inter_fwd(*dn_mid, g_dn, n_vh)
    dno_ins = [act(o_dn, LANES, rows=tbh), act(proj, LANES, o_z, vdim, rows=tbh), prm(nw_t, LANES)]
    dno_grid, dno_outs = (n_vh, nth), [out(vdim, LANES, tbh)]
    (dn,), _, (dn_t,) = stage_fwd("dn_out_fwd", f_dn_out, dno_grid, dno_ins, dno_outs, transposed=[0])
    y_b = mm(dn, w_pb, name="proj_b_fwd")

    cb_m = _tile(math.gcd(math.gcd(d, o_ga), o_gb), 512)
    mrg_ins = [act(proj, cb_m, o_ga, d, rows=tbc), act(proj, cb_m, o_gb, d, rows=tbc), act(y_a, cb_m, rows=tbc),
               act(y_b, cb_m, rows=tbc)]
    mrg_grid, mrg_outs = (d // cb_m, ntc), [out(d, cb_m, tbc)]
    (merged,), _, (merged_t,) = stage_fwd("merge_fwd", f_merge, mrg_grid, mrg_ins, mrg_outs, transposed=[0])
    mix = mm(merged, w_o, name="w_out_fwd")
    ln1_ins = [act(xs, d), act(mix, d), prm(gt1, d), prm(row1(W['ln1_g']), d), prm(row1(W['ln1_b']), d),
               prm(sc2, d), prm(sh2, d)]
    ln1_outs = [out(d, d), out(d, d)]
    (x1, h2), _, (h2_t,) = stage_fwd("ln1_mod2_fwd", f_deepnorm_mod, (1, nt), ln1_ins, ln1_outs, transposed=[1])

    gu = mm(h2, w_gu, name="ffn_in_fwd")
    cb_f = _tile(d_ff, 256)
    ffa_ins = [act(gu, cb_f, 0, d_ff, rows=tbc), act(gu, cb_f, d_ff, d_ff, rows=tbc), prm(fcw, cb_f, krows(3)),
               prm(row1(W['ffn_conv_b']), cb_f)]
    ffa_grid, ffa_car, ffa_outs = (d_ff // cb_f, ntc), [(SUBLANES, cb_f)], [out(d_ff, cb_f, tbc)]
    (act_ff,), ffa_hist, (act_t,) = stage_fwd("ffn_act_fwd", f_ffn_act, ffa_grid, ffa_ins, ffa_outs, ffa_car,
                                              transposed=[0])
    ff = mm(act_ff, w_fd, name="ffn_down_fwd")
    ln2_ins = [act(x1, d), act(ff, d), prm(gt2, d), prm(row1(W['ln2_g']), d), prm(row1(W['ln2_b']), d),
               act(tgt, d, grad=False)]
    ln2_outs = [Out((t, 1), (tb, 1), lambda o, s: (s, 0))]
    (loss_rows,), _ = stage_fwd("ln2_loss_fwd", f_deepnorm_loss, (1, nt), ln2_ins, ln2_outs)

    dx1_a, d_ff_o, d_gt2, d_ln2g, d_ln2b = stage_bwd("ln2_loss_bwd", f_deepnorm_loss, (1, nt), ln2_ins, ln2_outs,
                                                     [jnp.ones((t, 1), F32)])
    d_act = mm(d_ff_o, w_fd, name="ffn_down_bwd_x", tb=True)
    gw_fd = mm(act_t, d_ff_o, name="ffn_down_bwd_w")
    d_gp, d_up, d_fcw, d_fcb = stage_bwd("ffn_act_bwd", f_ffn_act, ffa_grid, ffa_ins, ffa_outs, [d_act],
                                         ffa_car, ffa_hist, gdtypes={0: MXU_DTYPE, 1: MXU_DTYPE})
    col_blocks = lambda g: jnp.transpose(g.reshape(g.shape[0], N_DEV, -1), (1, 0, 2)).astype(WIRE_DTYPE)
    row_blocks = lambda g: g.reshape(N_DEV, -1, g.shape[1]).astype(WIRE_DTYPE)
    big_parts = {}
    d_h2, big_parts['ffn_w_down'] = mm([d_gp, d_up], [w_gate, w_up], name="ffn_in_bwd_x", tb=True,
                                       scatter=[row_blocks(gw_fd)])
    gw_gate, gw_up = mm(h2_t, d_gp, name="ffn_gate_bwd_w"), mm(h2_t, d_up, name="ffn_up_bwd_w")
    dx_a, d_mix, d_gt1, d_ln1g, d_ln1b, d_sc2, d_sh2 = stage_bwd("ln1_mod2_bwd", f_deepnorm_mod, (1, nt), ln1_ins,
                                                                 ln1_outs, [dx1_a, d_h2])
    d_merged = mm(d_mix, w_o, name="w_out_bwd_x", tb=True)
    gw_o = mm(merged_t, d_mix, name="w_out_bwd_w")
    d_ga, d_gb, d_ya, d_yb = stage_bwd("merge_bwd", f_merge, mrg_grid, mrg_ins, mrg_outs, [d_merged],
                                       gdtypes={0: MXU_DTYPE, 1: MXU_DTYPE})
    d_rec = mm(d_ya, w_pa, name="proj_a_bwd_x", tb=True)
    gw_pa = mm(rec_t, d_ya, name="proj_a_bwd_w")
    d_dn = mm(d_yb, w_pb, name="proj_b_bwd_x", tb=True)
    gw_pb = mm(dn_t, d_yb, name="proj_b_bwd_w")

    d_o, d_z, d_nwt = stage_bwd("dn_out_bwd", f_dn_out, dno_grid, dno_ins, dno_outs, [d_dn], gdtypes={1: MXU_DTYPE})
    *d_mid, d_g_state = delta_inter_bwd(*dn_mid, g_dn, dn_hist, d_o, n_vh)
    d_qn, d_kn, d_v, d_g_col, d_gt, d_beta = delta_intra_bwd(qn, kn, v_c, 0, g_dn, gt_dn, beta_dn, d_mid, n_vh)
    d_g_row = jnp.pad(jnp.transpose(d_gt, (0, 2, 1)).reshape(t, n_vh), ((0, 0), (0, LANES - n_vh)))
    d_a, d_b, d_alog, d_dtb = stage_bwd("dn_gates_bwd", f_dn_gates, (1, nt), gate_ins, gate_outs,
                                        [(d_g_state, d_g_col, d_g_row), d_beta], gdtypes={0: MXU_DTYPE, 1: MXU_DTYPE})
    d_win, d_dcw = {}, []
    for nm, cot in (("q", d_qn), ("k", d_kn), ("v", d_v)):
        _, f_, ins_, grid_, outs_, car_, hist_ = dnc[nm]
        d_win[nm], dw_ = stage_bwd("dn_conv_%s_bwd" % nm, f_, grid_, ins_, outs_, [cot], car_, hist_,
                                   gdtypes={0: MXU_DTYPE})
        d_dcw.append(dw_)
    d_dcw = jnp.concatenate(d_dcw, axis=1)

    d_xc_a, d_pr, d_pi, d_gr, d_ba, d_bx, d_lam = stage_bwd(
        "rglru_bwd", f_rglru, lru_grid, lru_ins, [out(d_rnn, cb_r)], [d_rec], lru_car, lru_hist,
        gdtypes={1: MXU_DTYPE, 2: MXU_DTYPE, 3: MXU_DTYPE})
    d_xc_b, big_parts['w_out'], big_parts['w_proj_a'], big_parts['w_proj_b'] = mm(
        [d_pr, d_pi], [w_bd[:, :d_rnn], w_bd[:, d_rnn:]], name="rg_gates_bwd_x", tb=True,
        scatter=[row_blocks(gw_o), row_blocks(gw_pa), row_blocks(gw_pb)])
    gw_bd_a, gw_bd_x = mm(xc_t, d_pr, name="rg_gate_a_bwd_w"), mm(xc_t, d_pi, name="rg_gate_x_bwd_w")
    d_xr, d_rcw, d_rcb = stage_bwd("rg_conv_bwd", f_rg_conv, rgc_grid, rgc_ins, rgc_outs, [(d_xc_a, d_xc_b)],
                                   rgc_car, rgc_hist, gdtypes={0: MXU_DTYPE})

    diag = lambda g: jnp.einsum('nimj,nm->nij', g.reshape(n_blk, d_rnn // n_blk, n_blk, d_rnn // n_blk), eye_b)
    small_names = ['rg_conv_w', 'rg_conv_b', 'rg_w_a', 'rg_b_a', 'rg_w_x', 'rg_b_x', 'rg_lambda', 'dn_conv_w',
                   'dn_a_log', 'dn_dt_bias', 'dn_norm_w', 'ln1_g', 'ln1_b', 'ffn_conv_w', 'ffn_conv_b', 'ln2_g', 'ln2_b']
    small_loc = {
        'rg_conv_w': d_rcw, 'rg_conv_b': d_rcb,
        'rg_w_a': diag(gw_bd_a), 'rg_b_a': d_ba, 'rg_w_x': diag(gw_bd_x), 'rg_b_x': d_bx,
        'rg_lambda': d_lam, 'dn_conv_w': d_dcw, 'dn_a_log': d_alog[:, :n_vh], 'dn_dt_bias': d_dtb[:, :n_vh],
        'dn_norm_w': jnp.sum(d_nwt.reshape(n_vh, LANES), axis=0), 'ln1_g': d_ln1g, 'ln1_b': d_ln1b,
        'ffn_conv_w': d_fcw, 'ffn_conv_b': d_fcb, 'ln2_g': d_ln2g, 'ln2_b': d_ln2b}
    small_list = [small_loc[n] for n in small_names]

    d_segs = [d_xr, d_gr, d_win["q"], d_win["k"], d_win["v"], d_z, d_a, d_b, d_ga, d_gb]
    riders = {4: ('ffn_w_gate', gw_gate), 5: ('ffn_w_up', gw_up)}
    gw_segs = []
    for i, dg in enumerate(d_segs):
        if i in riders:
            g_, big_parts[riders[i][0]] = mm(h1_t, dg, name="proj_bwd_w%d" % i, scatter=[col_blocks(riders[i][1])])
        else:
            g_ = mm(h1_t, dg, name="proj_bwd_w%d" % i)
        gw_segs.append(g_)
    gw_in = jnp.concatenate([g_[:, :splits[i]] for i, g_ in enumerate(gw_segs)], axis=1)
    half = len(d_segs) // 2
    d_h1_a, big_parts['w_in'] = mm(d_segs[:half], groups[:half], name="proj_bwd_x0", tb=True,
                                   scatter=[col_blocks(gw_in)], **MM_SPLIT_CAPS)
    d_h1_b, small_all = mm(d_segs[half:], groups[half:], name="proj_bwd_x1", tb=True,
                           gather=[_pack(small_list)], **MM_SPLIT_CAPS)
    grad_x, d_sc1, d_sh1 = stage_bwd("mod1_bwd", f_modulate, (1, nt), mod1_ins, [out(d, d)], [(d_h1_a, d_h1_b)],
                                     add_to={0: dx_a})

    g_small = dict(zip(small_names, _unpack(sum_parts("sum_small_grads", small_all), small_list)))
    d_ada_me = jnp.concatenate([d_sh1, d_sc1, d_gt1, d_sh2, d_sc2, d_gt2], axis=1)
    (d_ada_all,) = all_gather("gather_d_ada", [d_ada_me.reshape(-1, LANES)])
    g_small['b_ada'] = sum_parts("sum_d_ada", d_ada_all)
    small_names = ['b_ada'] + small_names
    d_ada_cols = lax.dynamic_slice(d_ada_all.reshape(N_DEV, 6 * d), (0, me * ada_w), (N_DEV, ada_w))
    d_ada_pad = jnp.pad(d_ada_cols, ((0, LANES - N_DEV), (0, 0)))
    gw_ada = mm(c_pad, d_ada_pad, name="ada_bwd_w", ta=True, a_act="silu")

    res = {}
    big_parts['w_ada'] = gw_ada[None]
    for n in ['w_ada'] + big:
        res[n] = adamw("adamw_" + n, W[n], big_parts[n], M[n], V[n])
    for n in small_sh:
        w_ = W[n].shape[1]
        g_small[n] = lax.dynamic_slice(g_small[n], (0, me * w_), (W[n].shape[0], w_))
    for n in small_names:
        g_small[n] = g_small[n].reshape(W[n].shape)
    pk = lambda dct: _pack([dct[n] for n in small_names])
    s_g, s_d, s_m, s_v = adamw("adamw_small", pk(W), pk(g_small)[None], pk(M), pk(V))
    like = [W[n] for n in small_names]
    for n, g_, d_, m_, v_ in zip(small_names, _unpack(s_g, like), _unpack(s_d, like), _unpack(s_m, like), _unpack(s_v, like)):
        res[n] = (g_, d_, m_, v_)

    loss = lax.psum(jnp.sum(loss_rows), ("x", "y", "c"))
    outs = [loss, grad_x[None]]
    for j in range(4):
        outs += [res[n][j].reshape(loc[n].shape) for n in names]
    return tuple(outs)
```

```python
import functools
import math

import jax
import jax.numpy as jnp
from jax import lax
from jax.experimental import pallas as pl
from jax.experimental.pallas import tpu as pltpu

F32 = jnp.float32
BF16 = jnp.bfloat16
MXU_DTYPE = BF16
WIRE_DTYPE = BF16
DN_DTYPE = BF16
HI = lax.Precision.HIGHEST
MESH = pl.DeviceIdType.MESH

N_DEV = 8
LANES = 128
SUBLANES = 8
VMEM_LIMIT = 56 * 1024 * 1024
MM_TM_CAP, MM_TN_CAP, MM_TK_CAP = 1536, 1536, 2048
MM_SPLIT_CAPS = dict(tm_cap=1024, tn_cap=1024, tk_cap=1024)

RG_C = 8.0
DN_CHUNK = 64
DN_HEAD_GROUP = 16
DN_INTER_CHUNKS = 4
DN_INTRA_CHUNKS = 2
LN_EPS = 1e-5
RMS_EPS = 1e-6
L2_EPS = 1e-6
DEPTH = 1
DEEPNORM_ALPHA = (2 * DEPTH) ** 0.25
ADAM_LR = 0.001
ADAM_B1 = 0.9
ADAM_B2 = 0.999
ADAM_EPS = 1e-08
ADAM_WD = 0.01
ADAM_STEP = 10


def _tile(n, cap, unit=LANES):
    best = None
    for t in range(unit, min(n, cap) + 1, unit):
        if n % t == 0:
            best = t
    return best if best is not None else n


def _round_up(n, m):
    return (n + m - 1) // m * m


_HBM = pl.BlockSpec(memory_space=pl.ANY)


def _exchange_sems(n):
    return [pltpu.SemaphoreType.DMA((n, N_DEV - 1)), pltpu.SemaphoreType.DMA((n, N_DEV - 1)),
            pltpu.SemaphoreType.DMA((n,))]


def _exchange_out_shape(arrs, scatter):
    return [jax.ShapeDtypeStruct(a.shape if scatter else (N_DEV,) + a.shape, a.dtype) for a in arrs]


def _exchange_copies(in_refs, out_refs, sems, scatter, phase):
    send_sems, recv_sems, local_sems = sems
    x, y, c = lax.axis_index("x"), lax.axis_index("y"), lax.axis_index("c")
    me = 4 * x + 2 * y + c
    peers = [(x ^ ((k >> 2) & 1), y ^ ((k >> 1) & 1), c ^ (k & 1)) for k in range(N_DEV)]
    row = [4 * p[0] + 2 * p[1] + p[2] for p in peers]
    n = len(in_refs)

    def local(i):
        return pltpu.make_async_copy(in_refs[i].at[me] if scatter else in_refs[i], out_refs[i].at[me], local_sems.at[i])

    def remote(i, k, src, dst_row, to):
        return pltpu.make_async_remote_copy(src_ref=src, dst_ref=out_refs[i].at[dst_row],
                                            send_sem=send_sems.at[i, k - 1], recv_sem=recv_sems.at[i, k - 1],
                                            device_id=to, device_id_type=MESH)

    if scatter:
        sends = [(i, k, in_refs[i].at[row[k]], me, peers[k]) for k in range(1, N_DEV) for i in range(n)]
        passed = []
    else:
        sends = [(i, k, in_refs[i], me, peers[k]) for k in (1, 2, 4, 6) for i in range(n)]
        passed = [(i, k + 1, out_refs[i].at[row[k]], row[k], peers[1]) for k in (2, 4, 6) for i in range(n)]
    arrival = lambda i, k: remote(i, k, in_refs[i].at[me] if scatter else in_refs[i], row[k], peers[k])

    if phase == "start":
        for i in range(n):
            local(i).start()
        for cp in sends:
            remote(*cp).start()
    else:
        for cp in passed:
            arrival(cp[0], cp[1] - 1).wait_recv()
            remote(*cp).start()
        waited = {(cp[0], cp[1] - 1) for cp in passed}
        for k in range(1, N_DEV):
            for i in range(n):
                if (i, k) not in waited:
                    arrival(i, k).wait_recv()
        for cp in sends + passed:
            remote(*cp).wait_send()
        for i in range(n):
            local(i).wait()


def mm(a, b, *, name, ta=False, tb=False, a_act=None, bias=None, out_dtype=F32,
       tm_cap=MM_TM_CAP, tn_cap=MM_TN_CAP, tk_cap=MM_TK_CAP, gather=(), scatter=()):
    a_segs = list(a) if isinstance(a, (list, tuple)) else [a]
    b_segs = list(b) if isinstance(b, (list, tuple)) else [b]
    ns = len(a_segs)
    assert ns == len(b_segs) and (ns == 1 or a_act is None)
    m = a_segs[0].shape[1] if ta else a_segs[0].shape[0]
    n = b_segs[0].shape[0] if tb else b_segs[0].shape[1]
    ks = [x.shape[0] if ta else x.shape[1] for x in a_segs]
    assert ks == [y.shape[1] if tb else y.shape[0] for y in b_segs], (ks, ta, tb)
    tm, tn = _tile(m, tm_cap), _tile(n, tn_cap)
    tks = [_tile(k_, tk_cap) for k_ in ks]
    cnt = [k_ // t_ for k_, t_ in zip(ks, tks)]
    lo = [sum(cnt[:s]) for s in range(ns)]
    nk = sum(cnt)
    grid = (m // tm, n // tn, nk)
    dims = (((0 if ta else 1,), (1 if tb else 0,)), ((), ()))
    xch = list(gather) + list(scatter)
    nx, ng = len(xch), len(gather)
    n_main = 2 * ns + (bias is not None)

    def body(*refs):
        a_refs, b_refs = refs[:ns], refs[ns:2 * ns]
        bias_ref = refs[2 * ns] if bias is not None else None
        x_in, o_ref, x_out = refs[n_main:n_main + nx], refs[n_main + nx], refs[n_main + nx + 1:n_main + 2 * nx + 1]
        rest = refs[n_main + 2 * nx + 1:]
        acc_ref = rest[0] if nk > 1 else None
        sems = rest[1 if nk > 1 else 0:]
        groups = []
        if ng:
            groups.append((x_in[:ng], x_out[:ng], sems[:3], False))
        if nx > ng:
            groups.append((x_in[ng:], x_out[ng:], sems[-3:], True))
        if nx:
            step = (pl.program_id(0) * grid[1] + pl.program_id(1)) * grid[2] + pl.program_id(2)

            @pl.when(step == 0)
            def _():
                for gi, go_, gs, sc in groups:
                    _exchange_copies(gi, go_, gs, sc, "start")
        kk = pl.program_id(2)

        def finish(r):
            if bias is not None:
                r = r + bias_ref[...]
            o_ref[...] = r.astype(o_ref.dtype)

        def segment(s):
            av = a_refs[s][...]
            if a_act == "silu":
                av = jax.nn.silu(av.astype(F32))
            prod = lax.dot_general(av.astype(MXU_DTYPE), b_refs[s][...].astype(MXU_DTYPE), dims,
                                   preferred_element_type=F32)
            if nk == 1:
                finish(prod)
                return
            opens, closes = lo[s] == 0, lo[s] + cnt[s] == nk
            if opens:
                @pl.when(kk == 0)
                def _():
                    acc_ref[...] = prod
            inner = [kk > 0] * opens + [kk < nk - 1] * closes
            if inner:
                @pl.when(functools.reduce(lambda p, q: p & q, inner))
                def _():
                    acc_ref[...] += prod
            else:
                acc_ref[...] += prod
            if closes:
                @pl.when(kk == nk - 1)
                def _():
                    finish(acc_ref[...] + prod)

        for s in range(ns):
            if ns == 1:
                segment(s)
            else:
                pl.when((kk >= lo[s]) & (kk < lo[s] + cnt[s]))(functools.partial(segment, s))

        if nx:
            @pl.when(step == grid[0] * grid[1] * grid[2] - 1)
            def _():
                for gi, go_, gs, sc in groups:
                    _exchange_copies(gi, go_, gs, sc, "wait")

    def seg_index(s):
        return lambda q: jnp.clip(q - lo[s], 0, cnt[s] - 1) if ns > 1 else q

    a_specs, b_specs = [], []
    for s in range(ns):
        qi, tk = seg_index(s), tks[s]
        a_specs.append(pl.BlockSpec((tk, tm), (lambda qi: lambda i, j, q: (qi(q), i))(qi)) if ta
                       else pl.BlockSpec((tm, tk), (lambda qi: lambda i, j, q: (i, qi(q)))(qi)))
        b_specs.append(pl.BlockSpec((tn, tk), (lambda qi: lambda i, j, q: (j, qi(q)))(qi)) if tb
                       else pl.BlockSpec((tk, tn), (lambda qi: lambda i, j, q: (qi(q), j))(qi)))
    in_specs, args = a_specs + b_specs, a_segs + b_segs
    if bias is not None:
        in_specs.append(pl.BlockSpec((1, tn), lambda i, j, q: (0, j)))
        args.append(bias)
    o_spec, o_shape = pl.BlockSpec((tm, tn), lambda i, j, q: (i, j)), jax.ShapeDtypeStruct((m, n), out_dtype)
    acc = [pltpu.VMEM((tm, tn), F32)] if nk > 1 else []
    if not nx:
        return pl.pallas_call(
            body, name=name, grid=grid, in_specs=in_specs, out_specs=o_spec, out_shape=o_shape, scratch_shapes=acc,
            compiler_params=pltpu.CompilerParams(dimension_semantics=("parallel", "parallel", "arbitrary"),
                                                 vmem_limit_bytes=VMEM_LIMIT),
        )(*args)
    return pl.pallas_call(
        body, name=name, grid=grid, in_specs=in_specs + [_HBM] * nx, out_specs=[o_spec] + [_HBM] * nx,
        out_shape=[o_shape] + _exchange_out_shape(list(gather), False) + _exchange_out_shape(list(scatter), True),
        scratch_shapes=acc + (_exchange_sems(ng) if ng else []) + (_exchange_sems(nx - ng) if nx > ng else []),
        compiler_params=pltpu.CompilerParams(dimension_semantics=("arbitrary", "arbitrary", "arbitrary"),
                                             vmem_limit_bytes=VMEM_LIMIT, has_side_effects=True),
    )(*args, *xch)


class In:
    def __init__(self, arr, block, imap, acc=False, grad=True, parts=None, gshape=None, gimap=None):
        self.arr, self.block, self.imap, self.acc, self.grad, self.parts = arr, block, imap, acc, grad, parts
        self.gshape = arr.shape if gshape is None else gshape
        self.gimap = imap if gimap is None else gimap


class Out:
    def __init__(self, shape, block, imap, dtype=F32):
        self.shape, self.block, self.imap, self.dtype = shape, block, imap, dtype


def _load(in_refs, ins):
    vals = []
    for r, i in zip(in_refs, ins):
        if i.parts is None:
            vals.append(r[...])
        else:
            vals.extend(r[p] for p in i.parts)
    return vals


def _stage_params():
    return pltpu.CompilerParams(dimension_semantics=("parallel", "arbitrary"), vmem_limit_bytes=VMEM_LIMIT)


def stage_fwd(name, f, grid, ins, outs, carries=(), transposed=()):
    n_in, n_out, n_c, n_t = len(ins), len(outs), len(carries), len(transposed)

    def body(*refs):
        in_refs, out_refs = refs[:n_in], refs[n_in:n_in + n_out]
        hist_refs = refs[n_in + n_out:n_in + n_out + n_c]
        t_refs = refs[n_in + n_out + n_c:n_in + n_out + n_c + n_t]
        c_refs = refs[n_in + n_out + n_c + n_t:]
        if n_c:
            @pl.when(pl.program_id(1) == 0)
            def _():
                for c in c_refs:
                    c[...] = jnp.zeros_like(c)
        cin = [c[...] for c in c_refs]
        for h, c in zip(hist_refs, cin):
            h[...] = c
        o, cout = f(*_load(in_refs, ins), *cin)
        for r, v in zip(out_refs, o):
            r[...] = v.astype(r.dtype)
        for r, k in zip(t_refs, transposed):
            r[...] = o[k].T.astype(r.dtype)
        for c, v in zip(c_refs, cout):
            c[...] = v

    hist_spec = lambda c: pl.BlockSpec((None, None) + tuple(c), lambda o, s: (o, s) + (0,) * len(c))
    flip = lambda o_: pl.BlockSpec(o_.block[::-1], (lambda im: lambda o, s: im(o, s)[::-1])(o_.imap))
    res = pl.pallas_call(
        body, name=name, grid=grid,
        in_specs=[pl.BlockSpec(i.block, i.imap) for i in ins],
        out_specs=[pl.BlockSpec(o.block, o.imap) for o in outs] + [hist_spec(c) for c in carries]
        + [flip(outs[k]) for k in transposed],
        out_shape=[jax.ShapeDtypeStruct(o.shape, o.dtype) for o in outs]
        + [jax.ShapeDtypeStruct(tuple(grid) + tuple(c), F32) for c in carries]
        + [jax.ShapeDtypeStruct(outs[k].shape[::-1], MXU_DTYPE) for k in transposed],
        scratch_shapes=[pltpu.VMEM(tuple(c), F32) for c in carries],
        compiler_params=_stage_params(),
    )(*[i.arr for i in ins])
    res = list(res)
    if transposed:
        return res[:n_out], res[n_out:n_out + n_c], res[n_out + n_c:]
    return res[:n_out], res[n_out:]


def stage_bwd(name, f, grid, ins, outs, cots, carries=(), hists=(), add_to=None, gdtypes=None):
    n_in, n_out, n_c = len(ins), len(outs), len(carries)
    ns = grid[1]
    add_to = add_to or {}
    gdtypes = gdtypes or {}
    add_idx = sorted(add_to)
    g_idx = [k for k, i in enumerate(ins) if i.grad]
    cots = [c if isinstance(c, (tuple, list)) else (c,) for c in cots]
    n_cot = [len(c) for c in cots]
    rev = lambda imap: (lambda o, s: imap(o, ns - 1 - s))

    def body(*refs):
        p = 0
        in_refs = refs[p:p + n_in]; p += n_in
        cot_refs = []
        for cnt in n_cot:
            cot_refs.append(refs[p:p + cnt]); p += cnt
        hist_refs = refs[p:p + n_c]; p += n_c
        add_refs = refs[p:p + len(add_idx)]; p += len(add_idx)
        g_refs = refs[p:p + len(g_idx)]; p += len(g_idx)
        dc_refs = refs[p:]
        first = pl.program_id(1) == 0
        if n_c:
            @pl.when(first)
            def _():
                for c in dc_refs:
                    c[...] = jnp.zeros_like(c)
        vals = _load(in_refs, ins)
        cin = [h[...] for h in hist_refs]
        (o, cout), vjp = jax.vjp(lambda *a: f(*a), *vals, *cin)
        cot_o = []
        for crs, v in zip(cot_refs, o):
            c = crs[0][...].astype(v.dtype)
            for extra in crs[1:]:
                c = c + extra[...].astype(v.dtype)
            cot_o.append(c)
        cot_c = tuple(c[...] for c in dc_refs)
        grads = vjp((tuple(cot_o), cot_c))
        pos, per_in = 0, []
        for i in ins:
            cnt = 1 if i.parts is None else len(i.parts)
            per_in.append(grads[pos:pos + cnt])
            pos += cnt
        dcin = grads[pos:]
        for gr, k in zip(g_refs, g_idx):
            i, gs = ins[k], per_in[k]
            if i.acc:
                @pl.when(first)
                def _(gr=gr):
                    gr[...] = jnp.zeros_like(gr)
                if i.parts is None:
                    gr[...] += gs[0].astype(gr.dtype)
                else:
                    for pt, g in zip(i.parts, gs):
                        gr[pt] += g.astype(gr.dtype)
            else:
                g = gs[0]
                if k in add_to:
                    g = g + add_refs[add_idx.index(k)][...].astype(g.dtype)
                gr[...] = g.astype(gr.dtype)
        for c, v in zip(dc_refs, dcin):
            c[...] = v

    in_specs = [pl.BlockSpec(i.block, rev(i.imap)) for i in ins]
    for o_, cnt in zip(outs, n_cot):
        in_specs += [pl.BlockSpec(o_.block, rev(o_.imap))] * cnt
    in_specs += [pl.BlockSpec((None, None) + tuple(c), (lambda c: (lambda o, s: (o, ns - 1 - s) + (0,) * len(c)))(c))
                 for c in carries]
    in_specs += [pl.BlockSpec(ins[k].block, rev(ins[k].gimap)) for k in add_idx]
    out_specs, out_shape = [], []
    for k in g_idx:
        i = ins[k]
        if i.acc:
            out_specs.append(pl.BlockSpec(i.block, (lambda im: (lambda o, s: im(o, 0)))(i.imap)))
        else:
            out_specs.append(pl.BlockSpec(i.block, rev(i.gimap)))
        out_shape.append(jax.ShapeDtypeStruct(i.gshape, gdtypes.get(k, F32)))
    res = pl.pallas_call(
        body, name=name, grid=grid, in_specs=in_specs, out_specs=out_specs, out_shape=out_shape,
        scratch_shapes=[pltpu.VMEM(tuple(c), F32) for c in carries],
        compiler_params=_stage_params(),
    )(*[i.arr for i in ins], *[a for c in cots for a in c], *hists, *[add_to[k] for k in add_idx])
    return list(res)


def _iota_rows(shape):
    return lax.broadcasted_iota(jnp.int32, shape, 0)


@functools.partial(jax.custom_vjp, nondiff_argnums=(1,))
def _roll_rows(x, s):
    return pltpu.roll(x, s % x.shape[0], 0)


def _roll_rows_fwd(x, s):
    return _roll_rows(x, s), None


def _roll_rows_bwd(s, _, g):
    return (_roll_rows(g, -s),)


_roll_rows.defvjp(_roll_rows_fwd, _roll_rows_bwd)


@jax.custom_vjp
def _drop_head(xx):
    return xx[SUBLANES:]


def _drop_head_fwd(xx):
    return xx[SUBLANES:], None


def _drop_head_bwd(_, g):
    return (jnp.concatenate([jnp.zeros((SUBLANES, g.shape[1]), g.dtype), g], axis=0),)


_drop_head.defvjp(_drop_head_fwd, _drop_head_bwd)


@jax.custom_vjp
def _last_rows(x):
    return x[x.shape[0] - SUBLANES:]


def _last_rows_fwd(x):
    return x[x.shape[0] - SUBLANES:], x.shape[0]


def _last_rows_bwd(n, g):
    return (jnp.concatenate([jnp.zeros((n - SUBLANES, g.shape[1]), g.dtype), g], axis=0),)


_last_rows.defvjp(_last_rows_fwd, _last_rows_bwd)


def _last_row(x):
    n = x.shape[0]
    return jnp.sum(jnp.where(_iota_rows(x.shape) == n - 1, x, 0.0), axis=0, keepdims=True)


def _scan_steps(n):
    s = 1
    while s < n:
        yield s
        s *= 2


def _block_scan_log(a, u, h0):
    n = a.shape[0]
    row = _iota_rows(a.shape)
    for s in _scan_steps(n):
        keep = row >= s
        a_s = jnp.where(keep, pltpu.roll(a, s, 0), 1.0)
        u_s = jnp.where(keep, pltpu.roll(u, s, 0), 0.0)
        u = u + a * u_s
        a = a * a_s
    return u + a * h0


def _block_scan_impl(a, u, edge, reverse=False):
    n, c = a.shape
    nt = n // SUBLANES
    a, u = a.reshape(nt, SUBLANES, c), u.reshape(nt, SUBLANES, c)
    row = lax.broadcasted_iota(jnp.int32, a.shape, 1)
    for s in _scan_steps(SUBLANES):
        keep, shift = (row < SUBLANES - s, SUBLANES - s) if reverse else (row >= s, s)
        a_s = jnp.where(keep, pltpu.roll(a, shift, 1), 1.0)
        u_s = jnp.where(keep, pltpu.roll(u, shift, 1), 0.0)
        u = u + a * u_s
        a = a * a_s
    carry = jnp.broadcast_to(edge, (SUBLANES, c))
    tiles = [None] * nt
    at = 0 if reverse else SUBLANES - 1
    for i in (reversed(range(nt)) if reverse else range(nt)):
        tiles[i] = u[i] + a[i] * carry
        carry = jnp.broadcast_to(tiles[i][at:at + 1, :], (SUBLANES, c))
    return jnp.stack(tiles).reshape(n, c)


@jax.custom_vjp
def _block_scan(a, u, h0):
    return _block_scan_log(a, u, h0)


def _block_scan_fwd(a, u, h0):
    h = _block_scan_impl(a, u, h0)
    return h, (a, h, h0)


def _block_scan_bwd(res, dh):
    a, h, h0 = res
    n = a.shape[0]
    row = _iota_rows(a.shape)
    lam = _block_scan_impl(pltpu.roll(a, n - 1, 0), dh, jnp.zeros_like(h0), reverse=True)
    h_prev = jnp.where(row >= 1, pltpu.roll(h, 1, 0), jnp.broadcast_to(h0, h.shape))
    d_h0 = jnp.sum(jnp.where(row == 0, a * lam, 0.0), axis=0, keepdims=True)
    return lam * h_prev, lam, d_h0


_block_scan.defvjp(_block_scan_fwd, _block_scan_bwd)


def _dot_hi(a, b, dims=(((1,), (0,)), ((), ()))):
    return lax.dot_general(a, b, dims, precision=HI, preferred_element_type=F32)


_NN, _NT, _TN = "nn", "nt", "tn"
_CONTRACT = {_NN: (1, 0), _NT: (1, 1), _TN: (0, 0)}


def _raw_dot(a, b, kind):
    ca, cb = _CONTRACT[kind]
    lead = a.ndim - 2
    dims = (((ca + lead,), (cb + lead,)), (tuple(range(lead)), tuple(range(lead))))
    return lax.dot_general(a.astype(DN_DTYPE), b.astype(DN_DTYPE), dims, preferred_element_type=F32)


@jax.custom_vjp
def _nn(a, b):
    return _raw_dot(a, b, _NN)


_nn.defvjp(lambda a, b: (_raw_dot(a, b, _NN), (a, b)),
           lambda r, g: (_raw_dot(g, r[1], _NT), _raw_dot(r[0], g, _TN)))


@jax.custom_vjp
def _nt(a, b):
    return _raw_dot(a, b, _NT)


_nt.defvjp(lambda a, b: (_raw_dot(a, b, _NT), (a, b)),
           lambda r, g: (_raw_dot(g, r[1], _NN), _raw_dot(g, r[0], _TN)))


@jax.custom_vjp
def _tn(a, b):
    return _raw_dot(a, b, _TN)


_tn.defvjp(lambda a, b: (_raw_dot(a, b, _TN), (a, b)),
           lambda r, g: (_raw_dot(r[1], g, _NT), _raw_dot(r[0], g, _NN)))


def _neumann_inverse(a):
    n = a.shape[-1]
    eye = (lax.broadcasted_iota(jnp.int32, (n, n), 0) == lax.broadcasted_iota(jnp.int32, (n, n), 1)).astype(F32)
    p = _raw_dot(a, a, _NN)
    e = p
    for _ in range(int(math.log2(n)) - 2):
        p = _raw_dot(p, p, _NN)
        e = e + p + _raw_dot(e, p, _NN)
    return eye - a + e - _raw_dot(a, e, _NN)


@jax.custom_vjp
def _unit_lower_inverse(a):
    return _neumann_inverse(a)


def _unit_lower_inverse_fwd(a):
    x = _neumann_inverse(a)
    return x, x


def _unit_lower_inverse_bwd(x, g):
    return (-_raw_dot(_raw_dot(x, g, _TN), x, _NT),)


_unit_lower_inverse.defvjp(_unit_lower_inverse_fwd, _unit_lower_inverse_bwd)


def _softplus(x):
    return jnp.maximum(x, 0.0) + jnp.log1p(jnp.exp(-jnp.abs(x)))


def _neg_expm1(x):
    series = -x * (1.0 + x * (0.5 + x * (1.0 / 6.0 + x * (1.0 / 24.0 + x * (1.0 / 120.0)))))
    return jnp.where(x > -0.03, series, 1.0 - jnp.exp(x))


def f_modulate(x, sc, sh):
    return (x * (1.0 + sc) + sh,), ()


def _deepnorm(x, y, gt, g, b):
    v = DEEPNORM_ALPHA * x + (1.0 + gt) * y
    mu = jnp.mean(v, axis=-1, keepdims=True)
    vc = v - mu
    var = jnp.mean(vc * vc, axis=-1, keepdims=True)
    return vc * lax.rsqrt(var + LN_EPS) * g + b


def f_deepnorm_mod(x, y, gt, g, b, sc, sh):
    x1 = _deepnorm(x, y, gt, g, b)
    return (x1, x1 * (1.0 + sc) + sh), ()


def f_deepnorm_loss(x, y, gt, g, b, target):
    err = _deepnorm(x, y, gt, g, b) - target
    return (0.5 * jnp.mean(err * err, axis=-1, keepdims=True),), ()


def _causal_conv(x, prev, ws):
    xx = jnp.concatenate([prev, x], axis=0)
    k = len(ws)
    y = ws[k - 1] * x
    for j in range(k - 1):
        y = y + ws[j] * _drop_head(_roll_rows(xx, k - 1 - j))
    return y


def f_rg_conv(x, w0, w1, w2, w3, b, prev):
    return (_causal_conv(x, prev, (w0, w1, w2, w3)) + b,), (_last_rows(x),)


def f_dn_conv(x, w0, w1, w2, w3, prev):
    return (jax.nn.silu(_causal_conv(x, prev, (w0, w1, w2, w3))),), (_last_rows(x),)


def f_ffn_act(gp, up, w0, w1, w2, b, prev):
    return (jax.nn.gelu(_causal_conv(gp, prev, (w0, w1, w2)) + b) * up,), (_last_rows(gp),)


def f_rglru(xc, pre_r, pre_i, gr, b_a, b_x, lam, h0):
    gate_r = jax.nn.sigmoid(pre_r + b_a)
    gate_i = jax.nn.sigmoid(pre_i + b_x)
    log_a = -RG_C * gate_r * _softplus(-lam)
    a = jnp.exp(log_a)
    mult = jnp.sqrt(_neg_expm1(2.0 * log_a))
    h = _block_scan(a, mult * gate_i * xc, h0)
    return (h * jax.nn.gelu(gr),), (_last_row(h),)


def f_dn_conv_norm(scale, x, w0, w1, w2, w3, prev):
    y = jax.nn.silu(_causal_conv(x, prev, (w0, w1, w2, w3)))
    return (y * lax.rsqrt(jnp.sum(y * y, axis=-1, keepdims=True) + L2_EPS) * scale,), (_last_rows(x),)


def f_dn_gates(a_in, b_in, a_log, dt_bias):
    g = -jnp.exp(a_log) * _softplus(a_in + dt_bias)
    n = g.shape[0]
    shift = int(math.log2(DN_CHUNK))
    ri = lax.broadcasted_iota(jnp.int32, (n, n), 0)
    ci = lax.broadcasted_iota(jnp.int32, (n, n), 1)
    tri = ((lax.shift_right_logical(ri, shift) == lax.shift_right_logical(ci, shift)) & (ri >= ci)).astype(F32)
    return (_dot_hi(tri, g), jax.nn.sigmoid(b_in)), ()


def f_dn_out(o, z, nw):
    r = lax.rsqrt(jnp.mean(o * o, axis=-1, keepdims=True) + RMS_EPS)
    return (o * r * nw * jax.nn.silu(z),), ()


def f_merge(ga, gb, ya, yb):
    return (jax.nn.sigmoid(ga) * ya + jax.nn.sigmoid(gb) * yb,), ()


def _delta_intra(q, k, v, g_i, g_j, beta):
    c = q.shape[-2]
    ri = lax.broadcasted_iota(jnp.int32, (c, c), 0)
    ci = lax.broadcasted_iota(jnp.int32, (c, c), 1)
    decay = jnp.exp(jnp.where(ri >= ci, g_i - g_j, -jnp.inf))
    g_last = jnp.sum(jnp.where(_iota_rows((c, 1)) == c - 1, g_i, 0.0), axis=-2, keepdims=True)
    exp_g = jnp.exp(g_i)
    kb = k * beta
    t_inv = _unit_lower_inverse(jnp.where(ri > ci, _nt(kb, k) * decay, 0.0))
    u = _nn(t_inv, v * beta)
    w = _nn(t_inv, kb * exp_g)
    return u, w, _nt(q, k) * decay, q * exp_g, k * jnp.exp(g_last - g_i)


def _delta_inter(u, w, qk, q_dec, k_dec, g_last, state):
    v_new = u - _nn(w, state)
    o = _nn(q_dec, state) + _nn(qk, v_new)
    return o, jnp.exp(g_last) * state + _tn(k_dec, v_new)


def _delta_params(sem):
    return pltpu.CompilerParams(dimension_semantics=(sem,), vmem_limit_bytes=VMEM_LIMIT)


def _head_groups(n_vh):
    hb = min(DN_HEAD_GROUP, n_vh)
    return [range(h0, h0 + hb) for h0 in range(0, n_vh, hb)]


def _stack(hs, f):
    return jnp.stack([f(h) for h in hs])


def _rows(ci):
    return slice(ci * DN_CHUNK, (ci + 1) * DN_CHUNK)


def _intra_pairs(n_vh):
    return [(ci, h) for ci in range(DN_INTRA_CHUNKS) for h in range(n_vh)]


def _pair_stack(pairs, ref, width=LANES, head_of=lambda h: h):
    return jnp.stack([ref[_rows(ci), head_of(h) * LANES:head_of(h) * LANES + width] for ci, h in pairs])


def _intra_operands(pairs, rep, q_ref, k_ref, v_ref, g_ref, gt_ref, b_ref):
    qk_head = lambda h: h // rep
    return (_pair_stack(pairs, q_ref, head_of=qk_head), _pair_stack(pairs, k_ref, head_of=qk_head),
            _pair_stack(pairs, v_ref), jnp.stack([g_ref[_rows(ci), h:h + 1] for ci, h in pairs]),
            jnp.stack([gt_ref[ci, h:h + 1, :] for ci, h in pairs]),
            jnp.stack([b_ref[_rows(ci), h:h + 1] for ci, h in pairs]))


def _intra_spec(width, col=0):
    return pl.BlockSpec((DN_INTRA_CHUNKS * DN_CHUNK, width), lambda s: (s, col))


def _inter_operands(hs, ci, u_ref, w_ref, qk_ref, qd_ref, kd_ref, g_ref):
    f32 = lambda ref, width=LANES: _stack(hs, lambda h: ref[_rows(ci), h * LANES:h * LANES + width].astype(F32))
    last = (ci + 1) * DN_CHUNK - 1
    return (f32(u_ref), f32(w_ref), f32(qk_ref, DN_CHUNK), f32(qd_ref), f32(kd_ref),
            _stack(hs, lambda h: g_ref[last:last + 1, h:h + 1]))


def _inter_spec(width, steps, reverse=False):
    rows = DN_INTER_CHUNKS * DN_CHUNK
    return pl.BlockSpec((rows, width), (lambda s: (steps - 1 - s, 0)) if reverse else (lambda s: (s, 0)))


def delta_intra_fwd(qn, kn, qkv, v_blk, big_g, big_gt, beta, n_vh):
    t, qk_w = qn.shape
    vdim = n_vh * LANES
    rep = vdim // qk_w
    nc = t // DN_CHUNK

    pairs = _intra_pairs(n_vh)

    def body(q_ref, k_ref, v_ref, g_ref, gt_ref, b_ref, u_ref, w_ref, qk_ref, qd_ref, kd_ref):
        u, w, qk, qd, kd = _delta_intra(*_intra_operands(pairs, rep, q_ref, k_ref, v_ref, g_ref, gt_ref, b_ref))
        for i, (ci, h) in enumerate(pairs):
            at = (_rows(ci), slice(h * LANES, (h + 1) * LANES))
            u_ref[at] = u[i]
            w_ref[at] = w[i].astype(w_ref.dtype)
            qk_ref[at] = jnp.concatenate([qk[i], jnp.zeros_like(qk[i])], axis=1).astype(qk_ref.dtype)
            qd_ref[at] = qd[i].astype(qd_ref.dtype)
            kd_ref[at] = kd[i].astype(kd_ref.dtype)

    return pl.pallas_call(
        body, name="delta_intra_fwd", grid=(nc // DN_INTRA_CHUNKS,),
        in_specs=[_intra_spec(qk_w), _intra_spec(qk_w), _intra_spec(vdim, v_blk), _intra_spec(LANES),
                  pl.BlockSpec((DN_INTRA_CHUNKS, n_vh, DN_CHUNK), lambda s: (s, 0, 0)), _intra_spec(LANES)],
        out_specs=[_intra_spec(vdim)] * 5,
        out_shape=[jax.ShapeDtypeStruct((t, vdim), F32)] + [jax.ShapeDtypeStruct((t, vdim), DN_DTYPE)] * 4,
        compiler_params=_delta_params("parallel"),
    )(qn, kn, qkv, big_g, big_gt, beta)


def delta_inter_fwd(u, w, qk, q_dec, k_dec, big_g, n_vh):
    t, vdim = u.shape
    nc = t // DN_CHUNK

    cpb = DN_INTER_CHUNKS
    steps = nc // cpb

    def body(u_ref, w_ref, qk_ref, qd_ref, kd_ref, g_ref, o_ref, hist_ref, s_ref):
        @pl.when(pl.program_id(0) == 0)
        def _():
            s_ref[...] = jnp.zeros_like(s_ref)
        for ci in range(cpb):
            for hs in _head_groups(n_vh):
                grp = slice(hs[0], hs[-1] + 1)
                st = s_ref[grp]
                hist_ref[ci, grp] = st
                o, ns = _delta_inter(*_inter_operands(hs, ci, u_ref, w_ref, qk_ref, qd_ref, kd_ref, g_ref), st)
                for i, h in enumerate(hs):
                    o_ref[_rows(ci), h * LANES:(h + 1) * LANES] = o[i]
                s_ref[grp] = ns

    return pl.pallas_call(
        body, name="delta_inter_fwd", grid=(steps,),
        in_specs=[_inter_spec(vdim, steps)] * 5 + [_inter_spec(LANES, steps)],
        out_specs=[_inter_spec(vdim, steps), pl.BlockSpec((cpb, n_vh, LANES, LANES), lambda s: (s, 0, 0, 0))],
        out_shape=[jax.ShapeDtypeStruct((t, vdim), F32), jax.ShapeDtypeStruct((nc, n_vh, LANES, LANES), F32)],
        scratch_shapes=[pltpu.VMEM((n_vh, LANES, LANES), F32)],
        compiler_params=_delta_params("arbitrary"),
    )(u, w, qk, q_dec, k_dec, big_g)


def delta_inter_bwd(u, w, qk, q_dec, k_dec, big_g, hist, d_o, n_vh):
    t, vdim = u.shape
    nc = t // DN_CHUNK

    cpb = DN_INTER_CHUNKS
    steps = nc // cpb

    def body(u_ref, w_ref, qk_ref, qd_ref, kd_ref, g_ref, hist_ref, do_ref,
             du_ref, dw_ref, dqk_ref, dqd_ref, dkd_ref, dg_ref, ds_ref):
        @pl.when(pl.program_id(0) == 0)
        def _():
            ds_ref[...] = jnp.zeros_like(ds_ref)
        lane = lax.broadcasted_iota(jnp.int32, (1, LANES), 1)
        last = _iota_rows((DN_CHUNK, LANES)) == DN_CHUNK - 1
        for ci in reversed(range(cpb)):
            dgl_all = jnp.zeros((1, LANES), F32)
            for hs in _head_groups(n_vh):
                grp = slice(hs[0], hs[-1] + 1)
                prim = _inter_operands(hs, ci, u_ref, w_ref, qk_ref, qd_ref, kd_ref, g_ref) + (hist_ref[ci, grp],)
                _, vjp = jax.vjp(_delta_inter, *prim)
                cot_o = _stack(hs, lambda h: do_ref[_rows(ci), h * LANES:(h + 1) * LANES])
                du, dw, dqk, dqd, dkd, dgl, dst = vjp((cot_o, ds_ref[grp]))
                ds_ref[grp] = dst
                for i, h in enumerate(hs):
                    sl = slice(h * LANES, (h + 1) * LANES)
                    du_ref[_rows(ci), sl] = du[i]
                    dw_ref[_rows(ci), sl] = dw[i]
                    dqk_ref[_rows(ci), sl] = jnp.concatenate([dqk[i], jnp.zeros_like(dqk[i])], axis=1)
                    dqd_ref[_rows(ci), sl] = dqd[i]
                    dkd_ref[_rows(ci), sl] = dkd[i]
                    dgl_all = dgl_all + dgl[i] * (lane == h).astype(F32)
            dg_ref[_rows(ci), :] = jnp.where(last, jnp.broadcast_to(dgl_all, (DN_CHUNK, LANES)), 0.0)

    rv = lambda w_: _inter_spec(w_, steps, reverse=True)
    return pl.pallas_call(
        body, name="delta_inter_bwd", grid=(steps,),
        in_specs=[rv(vdim)] * 5 + [rv(LANES), pl.BlockSpec((cpb, n_vh, LANES, LANES), lambda s: (steps - 1 - s, 0, 0, 0)),
                                   rv(vdim)],
        out_specs=[rv(vdim)] * 5 + [rv(LANES)],
        out_shape=[jax.ShapeDtypeStruct((t, vdim), F32)] * 5 + [jax.ShapeDtypeStruct((t, LANES), F32)],
        scratch_shapes=[pltpu.VMEM((n_vh, LANES, LANES), F32)],
        compiler_params=_delta_params("arbitrary"),
    )(u, w, qk, q_dec, k_dec, big_g, hist, d_o)


def delta_intra_bwd(qn, kn, qkv, v_blk, big_g, big_gt, beta, cots, n_vh):
    t, qk_w = qn.shape
    vdim = n_vh * LANES
    rep = vdim // qk_w
    nc = t // DN_CHUNK

    pairs = _intra_pairs(n_vh)

    def body(q_ref, k_ref, v_ref, g_ref, gt_ref, b_ref, du_ref, dw_ref, dqk_ref, dqd_ref, dkd_ref,
             dq_ref, dk_ref, dv_ref, dg_ref, dgt_ref, db_ref):
        lane = lax.broadcasted_iota(jnp.int32, (1, LANES), 1)
        _, vjp = jax.vjp(_delta_intra, *_intra_operands(pairs, rep, q_ref, k_ref, v_ref, g_ref, gt_ref, b_ref))
        dq, dk, dv, dgi, dgj, db = vjp((_pair_stack(pairs, du_ref), _pair_stack(pairs, dw_ref),
                                        _pair_stack(pairs, dqk_ref, DN_CHUNK), _pair_stack(pairs, dqd_ref),
                                        _pair_stack(pairs, dkd_ref)))
        dg_all, db_all = {}, {}
        dq_acc, dk_acc = None, None
        for i, (ci, h) in enumerate(pairs):
            j = h // rep
            dv_ref[_rows(ci), h * LANES:(h + 1) * LANES] = dv[i]
            dgt_ref[ci, h:h + 1, :] = dgj[i]
            onehot = (lane == h).astype(F32)
            dg_all[ci] = dgi[i] * onehot + dg_all.get(ci, 0.0)
            db_all[ci] = db[i] * onehot + db_all.get(ci, 0.0)
            dq_acc = dq[i] if h % rep == 0 else dq_acc + dq[i]
            dk_acc = dk[i] if h % rep == 0 else dk_acc + dk[i]
            if h % rep == rep - 1:
                dq_ref[_rows(ci), j * LANES:(j + 1) * LANES] = dq_acc
                dk_ref[_rows(ci), j * LANES:(j + 1) * LANES] = dk_acc
        for ci in dg_all:
            dg_ref[_rows(ci), :] = dg_all[ci]
            db_ref[_rows(ci), :] = db_all[ci]

    gt_spec = pl.BlockSpec((DN_INTRA_CHUNKS, n_vh, DN_CHUNK), lambda s: (s, 0, 0))
    return pl.pallas_call(
        body, name="delta_intra_bwd", grid=(nc // DN_INTRA_CHUNKS,),
        in_specs=[_intra_spec(qk_w), _intra_spec(qk_w), _intra_spec(vdim, v_blk), _intra_spec(LANES), gt_spec,
                  _intra_spec(LANES)] + [_intra_spec(vdim)] * 5,
        out_specs=[_intra_spec(qk_w), _intra_spec(qk_w), _intra_spec(vdim), _intra_spec(LANES), gt_spec,
                   _intra_spec(LANES)],
        out_shape=[jax.ShapeDtypeStruct((t, qk_w), F32), jax.ShapeDtypeStruct((t, qk_w), F32),
                   jax.ShapeDtypeStruct((t, vdim), F32), jax.ShapeDtypeStruct((t, LANES), F32),
                   jax.ShapeDtypeStruct((nc, n_vh, DN_CHUNK), F32), jax.ShapeDtypeStruct((t, LANES), F32)],
        compiler_params=_delta_params("parallel"),
    )(qn, kn, qkv, big_g, big_gt, beta, *cots)


def all_gather(name, arrs):
    n = len(arrs)

    def body(*refs):
        in_refs, out_refs, sems = refs[:n], refs[n:2 * n], refs[2 * n:]
        _exchange_copies(in_refs, out_refs, sems, False, "start")
        _exchange_copies(in_refs, out_refs, sems, False, "wait")

    res = pl.pallas_call(
        body, name=name,
        in_specs=[_HBM] * n, out_specs=[_HBM] * n,
        out_shape=_exchange_out_shape(arrs, False), scratch_shapes=_exchange_sems(n),
        compiler_params=pltpu.CompilerParams(has_side_effects=True),
    )(*arrs)
    return list(res)


def _adamw_math(w, g, m, v):
    m = ADAM_B1 * m + (1.0 - ADAM_B1) * g
    v = ADAM_B2 * v + (1.0 - ADAM_B2) * (g * g)
    m_hat = m / (1.0 - ADAM_B1 ** ADAM_STEP)
    v_hat = v / (1.0 - ADAM_B2 ** ADAM_STEP)
    delta = -ADAM_LR * (m_hat / (jnp.sqrt(v_hat) + ADAM_EPS) + ADAM_WD * w)
    return delta, m, v


def adamw(name, w, parts, m, v, rows_cap=128):
    r, c = w.shape
    np_ = parts.shape[0]
    tr = _tile(r, rows_cap, SUBLANES * (4 // parts.dtype.itemsize))

    def body(w_ref, p_ref, m_ref, v_ref, g_ref, d_ref, nm_ref, nv_ref):
        g = p_ref[0].astype(F32)
        for k in range(1, np_):
            g = g + p_ref[k].astype(F32)
        delta, nm, nv = _adamw_math(w_ref[...], g, m_ref[...], v_ref[...])
        g_ref[...] = g
        d_ref[...] = delta
        nm_ref[...] = nm
        nv_ref[...] = nv

    spec = pl.BlockSpec((tr, c), lambda i: (i, 0))
    return pl.pallas_call(
        body, name=name, grid=(r // tr,),
        in_specs=[spec, pl.BlockSpec((np_, tr, c), lambda i: (0, i, 0)), spec, spec],
        out_specs=[spec] * 4, out_shape=[jax.ShapeDtypeStruct((r, c), F32)] * 4,
        compiler_params=pltpu.CompilerParams(dimension_semantics=("parallel",), vmem_limit_bytes=VMEM_LIMIT),
    )(w, parts, m, v)


def sum_parts(name, parts, rows_cap=256):
    np_, r, c = parts.shape
    tr = _tile(r, rows_cap, SUBLANES)

    def body(p_ref, o_ref):
        g = p_ref[0].astype(F32)
        for k in range(1, np_):
            g = g + p_ref[k].astype(F32)
        o_ref[...] = g

    return pl.pallas_call(
        body, name=name, grid=(r // tr,),
        in_specs=[pl.BlockSpec((np_, tr, c), lambda i: (0, i, 0))],
        out_specs=pl.BlockSpec((tr, c), lambda i: (i, 0)),
        out_shape=jax.ShapeDtypeStruct((r, c), F32),
        compiler_params=pltpu.CompilerParams(dimension_semantics=("parallel",), vmem_limit_bytes=VMEM_LIMIT),
    )(parts)


def _pack(arrs):
    flat = jnp.concatenate([a.reshape(-1).astype(F32) for a in arrs])
    n = flat.shape[0]
    return jnp.pad(flat, (0, _round_up(n, LANES * SUBLANES) - n)).reshape(-1, LANES)


def _unpack(packed, like):
    flat, out, pos = packed.reshape(-1), [], 0
    for a in like:
        out.append(flat[pos:pos + a.size].reshape(a.shape))
        pos += a.size
    return out


def kernel(x, c, w_ada, b_ada, w_in, rg_conv_w, rg_conv_b, rg_w_a, rg_b_a, rg_w_x, rg_b_x, rg_lambda, dn_conv_w, dn_a_log, dn_dt_bias, dn_norm_w, w_proj_a, w_proj_b, w_out, ln1_g, ln1_b, ffn_w_gate, ffn_w_up, ffn_conv_w, ffn_conv_b, ffn_w_down, ln2_g, ln2_b, loss_target, m_w_ada, m_b_ada, m_w_in, m_rg_conv_w, m_rg_conv_b, m_rg_w_a, m_rg_b_a, m_rg_w_x, m_rg_b_x, m_rg_lambda, m_dn_conv_w, m_dn_a_log, m_dn_dt_bias, m_dn_norm_w, m_w_proj_a, m_w_proj_b, m_w_out, m_ln1_g, m_ln1_b, m_ffn_w_gate, m_ffn_w_up, m_ffn_conv_w, m_ffn_conv_b, m_ffn_w_down, m_ln2_g, m_ln2_b, v_w_ada, v_b_ada, v_w_in, v_rg_conv_w, v_rg_conv_b, v_rg_w_a, v_rg_b_a, v_rg_w_x, v_rg_b_x, v_rg_lambda, v_dn_conv_w, v_dn_a_log, v_dn_dt_bias, v_dn_norm_w, v_w_proj_a, v_w_proj_b, v_w_out, v_ln1_g, v_ln1_b, v_ffn_w_gate, v_ffn_w_up, v_ffn_conv_w, v_ffn_conv_b, v_ffn_w_down, v_ln2_g, v_ln2_b):
    names = ['w_ada', 'b_ada', 'w_in', 'rg_conv_w', 'rg_conv_b', 'rg_w_a', 'rg_b_a', 'rg_w_x', 'rg_b_x', 'rg_lambda',
             'dn_conv_w', 'dn_a_log', 'dn_dt_bias', 'dn_norm_w', 'w_proj_a', 'w_proj_b', 'w_out', 'ln1_g', 'ln1_b',
             'ffn_w_gate', 'ffn_w_up', 'ffn_conv_w', 'ffn_conv_b', 'ffn_w_down', 'ln2_g', 'ln2_b']
    loc = locals()
    W = {n: loc[n][0] for n in names}
    M = {n: loc['m_' + n][0] for n in names}
    V = {n: loc['v_' + n][0] for n in names}

    me = 4 * lax.axis_index("x") + 2 * lax.axis_index("y") + lax.axis_index("c")
    xs, tgt = x[0], loss_target[0]
    t, d = xs.shape
    d_rnn = W['rg_conv_b'].shape[0]
    n_blk = W['rg_w_a'].shape[0]
    n_vh = W['dn_a_log'].shape[0]
    assert W['dn_norm_w'].shape[0] == LANES
    vdim = n_vh * LANES
    d_ff = W['ffn_conv_b'].shape[0]
    d_in = W['w_in'].shape[1] * N_DEV
    qk = (d_in - 2 * d_rnn - 2 * vdim - 2 * n_vh - 2 * d) // 2
    assert vdim == 2 * qk and qk % LANES == 0 and n_vh <= LANES
    splits = (d_rnn, d_rnn, qk, qk, vdim, vdim, n_vh, n_vh, d, d)
    offs = [0]
    for s_ in splits:
        offs.append(offs[-1] + s_)

    tb = _tile(t, 256, SUBLANES)

    big = ['w_in', 'w_proj_a', 'w_proj_b', 'w_out', 'ffn_w_gate', 'ffn_w_up', 'ffn_w_down']
    small_sh = ['rg_conv_w', 'dn_conv_w', 'ffn_conv_w']
    first = all_gather("gather_first", [W['w_in'].astype(WIRE_DTYPE)] + [W[n] for n in small_sh] + [c])
    g_in, g_rcw, g_dcw, g_fcw, c_all = first
    cols = lambda g: jnp.transpose(g, (1, 0, 2)).reshape(g.shape[1], -1)
    rows = lambda g: g.reshape(-1, g.shape[2])
    w_in_f = cols(g_in)
    padl = lambda a: jnp.pad(a, ((0, 0), (0, LANES - a.shape[1])))
    groups = [w_in_f[:, offs[i]:offs[i + 1]] for i in range(10)]
    groups[6], groups[7] = padl(groups[6]), padl(groups[7])
    go = [0]
    for g_ in groups:
        go.append(go[-1] + g_.shape[1])
    n_pad = _round_up(go[-1], 512)
    wp = jnp.pad(jnp.concatenate(groups, axis=1), ((0, 0), (0, n_pad - go[-1])))
    o_xr, o_gr, o_q, o_k, o_v, o_z, o_a, o_b, o_ga, o_gb = go[:10]
    rcw, dcw, fcw = cols(g_rcw), cols(g_dcw), cols(g_fcw)
    eye_b = jnp.eye(n_blk, dtype=F32)
    bd = lambda w: (w[:, :, None, :] * eye_b[:, None, :, None]).reshape(d_rnn, d_rnn)
    w_bd = jnp.concatenate([bd(W['rg_w_a']), bd(W['rg_w_x'])], axis=1)
    row1 = lambda a: a.reshape(1, -1)
    padv = lambda a: jnp.pad(row1(a), ((0, 0), (0, LANES - a.shape[0])))
    nw_t = jnp.tile(row1(W['dn_norm_w']), (1, n_vh))

    c_pad =jnp.pad(c_all.reshape(N_DEV, d), ((0, LANES - N_DEV), (0, 0)))
    ada_w = W['w_ada'].shape[1]
    b_ada_me = lax.dynamic_slice(W['b_ada'], (me * ada_w,), (ada_w,)).reshape(1, ada_w)
    ada_sh = mm(c_pad, W['w_ada'], name="ada_fwd", a_act="silu", bias=b_ada_me)
    (ada_all,) = all_gather("gather_ada", [ada_sh[:N_DEV]])
    ada_me = lax.dynamic_slice(ada_all, (0, me, 0), (N_DEV, 1, ada_w)).reshape(6, 1, d)
    sh1, sc1, gt1, sh2, sc2, gt2 = [ada_me[i] for i in range(6)]

    nt = t // tb

    def act(a, bw, col0=0, width=None, grad=True, rows=tb):
        width = a.shape[1] if width is None else width
        assert col0 % bw == 0 and width % bw == 0
        c0 = col0 // bw
        return In(a, (rows, bw), lambda o, s: (s, c0 + o), grad=grad, gshape=(t, width), gimap=lambda o, s: (s, o))

    def prm(a, bw, parts=None):
        return In(a, (a.shape[0], bw), lambda o, s: (0, o), acc=True, parts=parts)

    def out(width, bw, rows=tb, dtype=F32):
        return Out((t, width), (rows, bw), lambda o, s: (s, o), dtype)

    tbh = _tile(t, 2048, SUBLANES)
    nth = t // tbh
    tbc = _tile(t, 1024, SUBLANES)
    ntc = t // tbc

    krows = lambda k_: [(slice(j, j + 1), slice(None)) for j in range(k_)]

    mod1_ins = [act(xs, d), prm(sc1, d), prm(sh1, d)]
    (h1,), _, (h1_t,) = stage_fwd("mod1_fwd", f_modulate, (1, nt), mod1_ins, [out(d, d, dtype=MXU_DTYPE)],
                                  transposed=[0])
    proj, g_pa, g_pb, g_out, g_fg, g_fu, g_fd = mm(h1, wp, name="proj_fwd",
                                                   gather=[W[n].astype(WIRE_DTYPE) for n in big[1:]])
    w_pa, w_pb, w_o, w_fd = rows(g_pa), rows(g_pb), rows(g_out), rows(g_fd)
    w_gate, w_up = cols(g_fg), cols(g_fu)
    w_gu = jnp.concatenate([w_gate, w_up], axis=1)

    cb_r = _tile(math.gcd(d_rnn, o_gr), 256)
    rgc_ins = [act(proj, cb_r, o_xr, d_rnn, rows=tbc), prm(rcw, cb_r, krows(4)), prm(row1(W['rg_conv_b']), cb_r)]
    rgc_grid, rgc_car, rgc_outs = (d_rnn // cb_r, ntc), [(SUBLANES, cb_r)], [out(d_rnn, cb_r, tbc)]
    (xc,), rgc_hist, (xc_t,) = stage_fwd("rg_conv_fwd", f_rg_conv, rgc_grid, rgc_ins, rgc_outs, rgc_car, transposed=[0])
    gates = mm(xc, w_bd, name="rg_gates_fwd")
    lru_ins = [act(xc, cb_r), act(gates, cb_r, 0, d_rnn), act(gates, cb_r, d_rnn, d_rnn), act(proj, cb_r, o_gr, d_rnn),
               prm(row1(W['rg_b_a']), cb_r), prm(row1(W['rg_b_x']), cb_r), prm(row1(W['rg_lambda']), cb_r)]
    lru_grid, lru_car = (d_rnn // cb_r, nt), [(1, cb_r)]
    (rec,), lru_hist, (rec_t,) = stage_fwd("rglru_fwd", f_rglru, lru_grid, lru_ins,
                                           [out(d_rnn, cb_r, dtype=MXU_DTYPE)], lru_car,
                                           transposed=[0])
    y_a = mm(rec, w_pa, name="proj_a_fwd")

    dnc = {}
    for nm, col0, width, w0, cb_, f_ in (("q", o_q, qk, 0, LANES, functools.partial(f_dn_conv_norm, LANES ** -0.5)),
                                         ("k", o_k, qk, qk, LANES, functools.partial(f_dn_conv_norm, 1.0)),
                                         ("v", o_v, vdim, 2 * qk, _tile(math.gcd(vdim, o_v), 256), f_dn_conv)):
        ins_ = [act(proj, cb_, col0, width, rows=tbc), prm(dcw[:, w0:w0 + width], cb_, krows(4))]
        grid_, outs_, car_ = (width // cb_, ntc), [out(width, cb_, tbc)], [(SUBLANES, cb_)]
        (y_,), hist_ = stage_fwd("dn_conv_%s_fwd" % nm, f_, grid_, ins_, outs_, car_)
        dnc[nm] = (y_, f_, ins_, grid_, outs_, car_, hist_)
    qn, kn, v_c = dnc["q"][0], dnc["k"][0], dnc["v"][0]
    gate_ins = [act(proj, LANES, o_a, LANES), act(proj, LANES, o_b, LANES),
                prm(padv(W['dn_a_log']), LANES), prm(padv(W['dn_dt_bias']), LANES)]
    gate_outs = [out(LANES, LANES), out(LANES, LANES)]
    (g_dn, beta_dn), _ = stage_fwd("dn_gates_fwd", f_dn_gates, (1, nt), gate_ins, gate_outs)
    n_ch = t // DN_CHUNK
    gt_dn = jnp.transpose(g_dn.reshape(n_ch, DN_CHUNK, LANES)[:, :, :n_vh], (0, 2, 1))
    dn_mid = delta_intra_fwd(qn, kn, v_c, 0, g_dn, gt_dn, beta_dn, n_vh)
    o_dn, dn_hist = delta_inter_fwd(*dn_mid, g_dn, n_vh)
    dno_ins = [act(o_dn, LANES, rows=tbh), act(proj, LANES, o_z, vdim, rows=tbh), prm(nw_t, LANES)]
    dno_grid, dno_outs = (n_vh, nth), [out(vdim, LANES, tbh, MXU_DTYPE)]
    (dn,), _, (dn_t,) = stage_fwd("dn_out_fwd", f_dn_out, dno_grid, dno_ins, dno_outs, transposed=[0])
    y_b = mm(dn, w_pb, name="proj_b_fwd")

    cb_m = _tile(math.gcd(math.gcd(d, o_ga), o_gb), 512)
    mrg_ins = [act(proj, cb_m, o_ga, d, rows=tbc), act(proj, cb_m, o_gb, d, rows=tbc), act(y_a, cb_m, rows=tbc),
               act(y_b, cb_m, rows=tbc)]
    mrg_grid, mrg_outs = (d // cb_m, ntc), [out(d, cb_m, tbc, MXU_DTYPE)]
    (merged,), _, (merged_t,) = stage_fwd("merge_fwd", f_merge, mrg_grid, mrg_ins, mrg_outs, transposed=[0])
    mix = mm(merged, w_o, name="w_out_fwd")
    ln1_ins = [act(xs, d), act(mix, d), prm(gt1, d), prm(row1(W['ln1_g']), d), prm(row1(W['ln1_b']), d),
               prm(sc2, d), prm(sh2, d)]
    ln1_outs = [out(d, d), out(d, d, dtype=MXU_DTYPE)]
    (x1, h2), _, (h2_t,) = stage_fwd("ln1_mod2_fwd", f_deepnorm_mod, (1, nt), ln1_ins, ln1_outs, transposed=[1])

    gu = mm(h2, w_gu, name="ffn_in_fwd")
    cb_f = _tile(d_ff, 256)
    ffa_ins = [act(gu, cb_f, 0, d_ff, rows=tbc), act(gu, cb_f, d_ff, d_ff, rows=tbc), prm(fcw, cb_f, krows(3)),
               prm(row1(W['ffn_conv_b']), cb_f)]
    ffa_grid, ffa_car, ffa_outs = (d_ff // cb_f, ntc), [(SUBLANES, cb_f)], [out(d_ff, cb_f, tbc, MXU_DTYPE)]
    (act_ff,), ffa_hist, (act_t,) = stage_fwd("ffn_act_fwd", f_ffn_act, ffa_grid, ffa_ins, ffa_outs, ffa_car,
                                              transposed=[0])
    ff = mm(act_ff, w_fd, name="ffn_down_fwd")
    ln2_ins = [act(x1, d), act(ff, d), prm(gt2, d), prm(row1(W['ln2_g']), d), prm(row1(W['ln2_b']), d),
               act(tgt, d, grad=False)]
    ln2_outs = [Out((t, 1), (tb, 1), lambda o, s: (s, 0))]
    (loss_rows,), _ = stage_fwd("ln2_loss_fwd", f_deepnorm_loss, (1, nt), ln2_ins, ln2_outs)

    dx1_a, d_ff_o, d_gt2, d_ln2g, d_ln2b = stage_bwd("ln2_loss_bwd", f_deepnorm_loss, (1, nt), ln2_ins, ln2_outs,
                                                     [jnp.ones((t, 1), F32)], gdtypes={1: MXU_DTYPE})
    d_act = mm(d_ff_o, w_fd, name="ffn_down_bwd_x", tb=True)
    gw_fd = mm(act_t, d_ff_o, name="ffn_down_bwd_w")
    d_gp, d_up, d_fcw, d_fcb = stage_bwd("ffn_act_bwd", f_ffn_act, ffa_grid, ffa_ins, ffa_outs, [d_act],
                                         ffa_car, ffa_hist, gdtypes={0: MXU_DTYPE, 1: MXU_DTYPE})
    col_blocks = lambda g: jnp.transpose(g.reshape(g.shape[0], N_DEV, -1), (1, 0, 2)).astype(WIRE_DTYPE)
    row_blocks = lambda g: g.reshape(N_DEV, -1, g.shape[1]).astype(WIRE_DTYPE)
    big_parts = {}
    d_h2, big_parts['ffn_w_down'] = mm([d_gp, d_up], [w_gate, w_up], name="ffn_in_bwd_x", tb=True,
                                       scatter=[row_blocks(gw_fd)])
    gw_gate, gw_up = mm(h2_t, d_gp, name="ffn_gate_bwd_w"), mm(h2_t, d_up, name="ffn_up_bwd_w")
    dx_a, d_mix, d_gt1, d_ln1g, d_ln1b, d_sc2, d_sh2 = stage_bwd("ln1_mod2_bwd", f_deepnorm_mod, (1, nt), ln1_ins,
                                                                 ln1_outs, [dx1_a, d_h2], gdtypes={1: MXU_DTYPE})
    d_merged = mm(d_mix, w_o, name="w_out_bwd_x", tb=True)
    gw_o = mm(merged_t, d_mix, name="w_out_bwd_w")
    d_ga, d_gb, d_ya, d_yb = stage_bwd("merge_bwd", f_merge, mrg_grid, mrg_ins, mrg_outs, [d_merged],
                                       gdtypes={k_: MXU_DTYPE for k_ in range(4)})
    d_rec = mm(d_ya, w_pa, name="proj_a_bwd_x", tb=True)
    gw_pa = mm(rec_t, d_ya, name="proj_a_bwd_w")
    d_dn = mm(d_yb, w_pb, name="proj_b_bwd_x", tb=True)
    gw_pb = mm(dn_t, d_yb, name="proj_b_bwd_w")

    d_o, d_z, d_nwt = stage_bwd("dn_out_bwd", f_dn_out, dno_grid, dno_ins, dno_outs, [d_dn], gdtypes={1: MXU_DTYPE})
    *d_mid, d_g_state = delta_inter_bwd(*dn_mid, g_dn, dn_hist, d_o, n_vh)
    d_qn, d_kn, d_v, d_g_col, d_gt, d_beta = delta_intra_bwd(qn, kn, v_c, 0, g_dn, gt_dn, beta_dn, d_mid, n_vh)
    d_g_row = jnp.pad(jnp.transpose(d_gt, (0, 2, 1)).reshape(t, n_vh), ((0, 0), (0, LANES - n_vh)))
    d_a, d_b, d_alog, d_dtb = stage_bwd("dn_gates_bwd", f_dn_gates, (1, nt), gate_ins, gate_outs,
                                        [(d_g_state, d_g_col, d_g_row), d_beta], gdtypes={0: MXU_DTYPE, 1: MXU_DTYPE})
    d_win, d_dcw = {}, []
    for nm, cot in (("q", d_qn), ("k", d_kn), ("v", d_v)):
        _, f_, ins_, grid_, outs_, car_, hist_ = dnc[nm]
        d_win[nm], dw_ = stage_bwd("dn_conv_%s_bwd" % nm, f_, grid_, ins_, outs_, [cot], car_, hist_,
                                   gdtypes={0: MXU_DTYPE})
        d_dcw.append(dw_)
    d_dcw = jnp.concatenate(d_dcw, axis=1)

    d_xc_a, d_pr, d_pi, d_gr, d_ba, d_bx, d_lam = stage_bwd(
        "rglru_bwd", f_rglru, lru_grid, lru_ins, [out(d_rnn, cb_r)], [d_rec], lru_car, lru_hist,
        gdtypes={1: MXU_DTYPE, 2: MXU_DTYPE, 3: MXU_DTYPE})
    d_xc_b, big_parts['w_out'], big_parts['w_proj_a'], big_parts['w_proj_b'] = mm(
        [d_pr, d_pi], [w_bd[:, :d_rnn], w_bd[:, d_rnn:]], name="rg_gates_bwd_x", tb=True,
        scatter=[row_blocks(gw_o), row_blocks(gw_pa), row_blocks(gw_pb)])
    gw_bd_a, gw_bd_x = mm(xc_t, d_pr, name="rg_gate_a_bwd_w"), mm(xc_t, d_pi, name="rg_gate_x_bwd_w")
    d_xr, d_rcw, d_rcb = stage_bwd("rg_conv_bwd", f_rg_conv, rgc_grid, rgc_ins, rgc_outs, [(d_xc_a, d_xc_b)],
                                   rgc_car, rgc_hist, gdtypes={0: MXU_DTYPE})

    diag = lambda g: jnp.einsum('nimj,nm->nij', g.reshape(n_blk, d_rnn // n_blk, n_blk, d_rnn // n_blk), eye_b)
    small_names = ['rg_conv_w', 'rg_conv_b', 'rg_w_a', 'rg_b_a', 'rg_w_x', 'rg_b_x', 'rg_lambda', 'dn_conv_w',
                   'dn_a_log', 'dn_dt_bias', 'dn_norm_w', 'ln1_g', 'ln1_b', 'ffn_conv_w', 'ffn_conv_b', 'ln2_g', 'ln2_b']
    small_loc = {
        'rg_conv_w': d_rcw, 'rg_conv_b': d_rcb,
        'rg_w_a': diag(gw_bd_a), 'rg_b_a': d_ba, 'rg_w_x': diag(gw_bd_x), 'rg_b_x': d_bx,
        'rg_lambda': d_lam, 'dn_conv_w': d_dcw, 'dn_a_log': d_alog[:, :n_vh], 'dn_dt_bias': d_dtb[:, :n_vh],
        'dn_norm_w': jnp.sum(d_nwt.reshape(n_vh, LANES), axis=0), 'ln1_g': d_ln1g, 'ln1_b': d_ln1b,
        'ffn_conv_w': d_fcw, 'ffn_conv_b': d_fcb, 'ln2_g': d_ln2g, 'ln2_b': d_ln2b}
    small_list = [small_loc[n] for n in small_names]

    d_segs = [d_xr, d_gr, d_win["q"], d_win["k"], d_win["v"], d_z, d_a, d_b, d_ga, d_gb]
    riders = {4: ('ffn_w_gate', gw_gate), 5: ('ffn_w_up', gw_up)}
    gw_segs = []
    for i, dg in enumerate(d_segs):
        if i in riders:
            g_, big_parts[riders[i][0]] = mm(h1_t, dg, name="proj_bwd_w%d" % i, scatter=[col_blocks(riders[i][1])])
        else:
            g_ = mm(h1_t, dg, name="proj_bwd_w%d" % i)
        gw_segs.append(g_)
    gw_in = jnp.concatenate([g_[:, :splits[i]] for i, g_ in enumerate(gw_segs)], axis=1)
    half = len(d_segs) // 2
    d_h1_a, big_parts['w_in'] = mm(d_segs[:half], groups[:half], name="proj_bwd_x0", tb=True,
                                   scatter=[col_blocks(gw_in)], **MM_SPLIT_CAPS)
    d_h1_b, small_all = mm(d_segs[half:], groups[half:], name="proj_bwd_x1", tb=True,
                           gather=[_pack(small_list)], **MM_SPLIT_CAPS)
    grad_x, d_sc1, d_sh1 = stage_bwd("mod1_bwd", f_modulate, (1, nt), mod1_ins, [out(d, d)], [(d_h1_a, d_h1_b)],
                                     add_to={0: dx_a})

    g_small = dict(zip(small_names, _unpack(sum_parts("sum_small_grads", small_all), small_list)))
    d_ada_me = jnp.concatenate([d_sh1, d_sc1, d_gt1, d_sh2, d_sc2, d_gt2], axis=1)
    (d_ada_all,) = all_gather("gather_d_ada", [d_ada_me.reshape(-1, LANES)])
    g_small['b_ada'] = sum_parts("sum_d_ada", d_ada_all)
    small_names = ['b_ada'] + small_names
    d_ada_cols = lax.dynamic_slice(d_ada_all.reshape(N_DEV, 6 * d), (0, me * ada_w), (N_DEV, ada_w))
    d_ada_pad = jnp.pad(d_ada_cols, ((0, LANES - N_DEV), (0, 0)))
    gw_ada = mm(c_pad, d_ada_pad, name="ada_bwd_w", ta=True, a_act="silu")

    res = {}
    big_parts['w_ada'] = gw_ada[None]
    for n in ['w_ada'] + big:
        res[n] = adamw("adamw_" + n, W[n], big_parts[n], M[n], V[n])
    for n in small_sh:
        w_ = W[n].shape[1]
        g_small[n] = lax.dynamic_slice(g_small[n], (0, me * w_), (W[n].shape[0], w_))
    for n in small_names:
        g_small[n] = g_small[n].reshape(W[n].shape)
    pk = lambda dct: _pack([dct[n] for n in small_names])
    s_g, s_d, s_m, s_v = adamw("adamw_small", pk(W), pk(g_small)[None], pk(M), pk(V))
    like = [W[n] for n in small_names]
    for n, g_, d_, m_, v_ in zip(small_names, _unpack(s_g, like), _unpack(s_d, like), _unpack(s_m, like), _unpack(s_v, like)):
        res[n] = (g_, d_, m_, v_)

    loss = lax.psum(jnp.sum(loss_rows), ("x", "y", "c"))
    outs = [loss, grad_x[None]]
    for j in range(4):
        outs += [res[n][j].reshape(loc[n].shape) for n in names]
    return tuple(outs)
```

```python
import functools
import math

import jax
import jax.numpy as jnp
from jax import lax
from jax.experimental import pallas as pl
from jax.experimental.pallas import tpu as pltpu

F32 = jnp.float32
BF16 = jnp.bfloat16
MXU_DTYPE = BF16
WIRE_DTYPE = BF16
DN_DTYPE = BF16
HI = lax.Precision.HIGHEST
MESH = pl.DeviceIdType.MESH

N_DEV = 8
LANES = 128
SUBLANES = 8
VMEM_LIMIT = 56 * 1024 * 1024
MM_TM_CAP, MM_TN_CAP, MM_TK_CAP = 1536, 1536, 2048
MM_SPLIT_CAPS = dict(tm_cap=1024, tn_cap=1024, tk_cap=1024)

RG_C = 8.0
DN_CHUNK = 64
DN_HEAD_GROUP = 16
DN_INTER_CHUNKS = 4
DN_INTRA_CHUNKS = 2
LN_EPS = 1e-5
RMS_EPS = 1e-6
L2_EPS = 1e-6
DEPTH = 1
DEEPNORM_ALPHA = (2 * DEPTH) ** 0.25
ADAM_LR = 0.001
ADAM_B1 = 0.9
ADAM_B2 = 0.999
ADAM_EPS = 1e-08
ADAM_WD = 0.01
ADAM_STEP = 10


def _tile(n, cap, unit=LANES):
    best = None
    for t in range(unit, min(n, cap) + 1, unit):
        if n % t == 0:
            best = t
    return best if best is not None else n


def _round_up(n, m):
    return (n + m - 1) // m * m


_HBM = pl.BlockSpec(memory_space=pl.ANY)


def _exchange_sems(n):
    return [pltpu.SemaphoreType.DMA((n, N_DEV - 1)), pltpu.SemaphoreType.DMA((n, N_DEV - 1)),
            pltpu.SemaphoreType.DMA((n,))]


def _exchange_out_shape(arrs, scatter):
    return [jax.ShapeDtypeStruct(a.shape if scatter else (N_DEV,) + a.shape, a.dtype) for a in arrs]


def _exchange_copies(in_refs, out_refs, sems, scatter, phase):
    send_sems, recv_sems, local_sems = sems
    x, y, c = lax.axis_index("x"), lax.axis_index("y"), lax.axis_index("c")
    me = 4 * x + 2 * y + c
    peers = [(x ^ ((k >> 2) & 1), y ^ ((k >> 1) & 1), c ^ (k & 1)) for k in range(N_DEV)]
    row = [4 * p[0] + 2 * p[1] + p[2] for p in peers]
    n = len(in_refs)

    def local(i):
        return pltpu.make_async_copy(in_refs[i].at[me] if scatter else in_refs[i], out_refs[i].at[me], local_sems.at[i])

    def remote(i, k, src, dst_row, to):
        return pltpu.make_async_remote_copy(src_ref=src, dst_ref=out_refs[i].at[dst_row],
                                            send_sem=send_sems.at[i, k - 1], recv_sem=recv_sems.at[i, k - 1],
                                            device_id=to, device_id_type=MESH)

    if scatter:
        sends = [(i, k, in_refs[i].at[row[k]], me, peers[k]) for k in range(1, N_DEV) for i in range(n)]
        passed = []
    else:
        sends = [(i, k, in_refs[i], me, peers[k]) for k in (1, 2, 4, 6) for i in range(n)]
        passed = [(i, k + 1, out_refs[i].at[row[k]], row[k], peers[1]) for k in (2, 4, 6) for i in range(n)]
    arrival = lambda i, k: remote(i, k, in_refs[i].at[me] if scatter else in_refs[i], row[k], peers[k])

    if phase == "start":
        for i in range(n):
            local(i).start()
        for cp in sends:
            remote(*cp).start()
    else:
        for cp in passed:
            arrival(cp[0], cp[1] - 1).wait_recv()
            remote(*cp).start()
        waited = {(cp[0], cp[1] - 1) for cp in passed}
        for k in range(1, N_DEV):
            for i in range(n):
                if (i, k) not in waited:
                    arrival(i, k).wait_recv()
        for cp in sends + passed:
            remote(*cp).wait_send()
        for i in range(n):
            local(i).wait()


def mm(a, b, *, name, ta=False, tb=False, a_act=None, bias=None, out_dtype=F32,
       tm_cap=MM_TM_CAP, tn_cap=MM_TN_CAP, tk_cap=MM_TK_CAP, gather=(), scatter=()):
    a_segs = list(a) if isinstance(a, (list, tuple)) else [a]
    b_segs = list(b) if isinstance(b, (list, tuple)) else [b]
    ns = len(a_segs)
    assert ns == len(b_segs) and (ns == 1 or a_act is None)
    m = a_segs[0].shape[1] if ta else a_segs[0].shape[0]
    n = b_segs[0].shape[0] if tb else b_segs[0].shape[1]
    ks = [x.shape[0] if ta else x.shape[1] for x in a_segs]
    assert ks == [y.shape[1] if tb else y.shape[0] for y in b_segs], (ks, ta, tb)
    tm, tn = _tile(m, tm_cap), _tile(n, tn_cap)
    tks = [_tile(k_, tk_cap) for k_ in ks]
    cnt = [k_ // t_ for k_, t_ in zip(ks, tks)]
    lo = [sum(cnt[:s]) for s in range(ns)]
    nk = sum(cnt)
    grid = (m // tm, n // tn, nk)
    dims = (((0 if ta else 1,), (1 if tb else 0,)), ((), ()))
    xch = list(gather) + list(scatter)
    nx, ng = len(xch), len(gather)
    n_main = 2 * ns + (bias is not None)

    def body(*refs):
        a_refs, b_refs = refs[:ns], refs[ns:2 * ns]
        bias_ref = refs[2 * ns] if bias is not None else None
        x_in, o_ref, x_out = refs[n_main:n_main + nx], refs[n_main + nx], refs[n_main + nx + 1:n_main + 2 * nx + 1]
        rest = refs[n_main + 2 * nx + 1:]
        acc_ref = rest[0] if nk > 1 else None
        sems = rest[1 if nk > 1 else 0:]
        groups = []
        if ng:
            groups.append((x_in[:ng], x_out[:ng], sems[:3], False))
        if nx > ng:
            groups.append((x_in[ng:], x_out[ng:], sems[-3:], True))
        if nx:
            step = (pl.program_id(0) * grid[1] + pl.program_id(1)) * grid[2] + pl.program_id(2)

            @pl.when(step == 0)
            def _():
                for gi, go_, gs, sc in groups:
                    _exchange_copies(gi, go_, gs, sc, "start")
        kk = pl.program_id(2)

        def finish(r):
            if bias is not None:
                r = r + bias_ref[...]
            o_ref[...] = r.astype(o_ref.dtype)

        def segment(s):
            av = a_refs[s][...]
            if a_act == "silu":
                av = jax.nn.silu(av.astype(F32))
            prod = lax.dot_general(av.astype(MXU_DTYPE), b_refs[s][...].astype(MXU_DTYPE), dims,
                                   preferred_element_type=F32)
            if nk == 1:
                finish(prod)
                return
            opens, closes = lo[s] == 0, lo[s] + cnt[s] == nk
            if opens:
                @pl.when(kk == 0)
                def _():
                    acc_ref[...] = prod
            inner = [kk > 0] * opens + [kk < nk - 1] * closes
            if inner:
                @pl.when(functools.reduce(lambda p, q: p & q, inner))
                def _():
                    acc_ref[...] += prod
            else:
                acc_ref[...] += prod
            if closes:
                @pl.when(kk == nk - 1)
                def _():
                    finish(acc_ref[...] + prod)

        for s in range(ns):
            if ns == 1:
                segment(s)
            else:
                pl.when((kk >= lo[s]) & (kk < lo[s] + cnt[s]))(functools.partial(segment, s))

        if nx:
            @pl.when(step == grid[0] * grid[1] * grid[2] - 1)
            def _():
                for gi, go_, gs, sc in groups:
                    _exchange_copies(gi, go_, gs, sc, "wait")

    def seg_index(s):
        return lambda q: jnp.clip(q - lo[s], 0, cnt[s] - 1) if ns > 1 else q

    a_specs, b_specs = [], []
    for s in range(ns):
        qi, tk = seg_index(s), tks[s]
        a_specs.append(pl.BlockSpec((tk, tm), (lambda qi: lambda i, j, q: (qi(q), i))(qi)) if ta
                       else pl.BlockSpec((tm, tk), (lambda qi: lambda i, j, q: (i, qi(q)))(qi)))
        b_specs.append(pl.BlockSpec((tn, tk), (lambda qi: lambda i, j, q: (j, qi(q)))(qi)) if tb
                       else pl.BlockSpec((tk, tn), (lambda qi: lambda i, j, q: (qi(q), j))(qi)))
    in_specs, args = a_specs + b_specs, a_segs + b_segs
    if bias is not None:
        in_specs.append(pl.BlockSpec((1, tn), lambda i, j, q: (0, j)))
        args.append(bias)
    o_spec, o_shape = pl.BlockSpec((tm, tn), lambda i, j, q: (i, j)), jax.ShapeDtypeStruct((m, n), out_dtype)
    acc = [pltpu.VMEM((tm, tn), F32)] if nk > 1 else []
    if not nx:
        return pl.pallas_call(
            body, name=name, grid=grid, in_specs=in_specs, out_specs=o_spec, out_shape=o_shape, scratch_shapes=acc,
            compiler_params=pltpu.CompilerParams(dimension_semantics=("parallel", "parallel", "arbitrary"),
                                                 vmem_limit_bytes=VMEM_LIMIT),
        )(*args)
    return pl.pallas_call(
        body, name=name, grid=grid, in_specs=in_specs + [_HBM] * nx, out_specs=[o_spec] + [_HBM] * nx,
        out_shape=[o_shape] + _exchange_out_shape(list(gather), False) + _exchange_out_shape(list(scatter), True),
        scratch_shapes=acc + (_exchange_sems(ng) if ng else []) + (_exchange_sems(nx - ng) if nx > ng else []),
        compiler_params=pltpu.CompilerParams(dimension_semantics=("arbitrary", "arbitrary", "arbitrary"),
                                             vmem_limit_bytes=VMEM_LIMIT, has_side_effects=True),
    )(*args, *xch)


class In:
    def __init__(self, arr, block, imap, acc=False, grad=True, parts=None, gshape=None, gimap=None):
        self.arr, self.block, self.imap, self.acc, self.grad, self.parts = arr, block, imap, acc, grad, parts
        self.gshape = arr.shape if gshape is None else gshape
        self.gimap = imap if gimap is None else gimap


class Out:
    def __init__(self, shape, block, imap, dtype=F32):
        self.shape, self.block, self.imap, self.dtype = shape, block, imap, dtype


def _load(in_refs, ins):
    vals = []
    for r, i in zip(in_refs, ins):
        if i.parts is None:
            vals.append(r[...])
        else:
            vals.extend(r[p] for p in i.parts)
    return vals


def _stage_params():
    return pltpu.CompilerParams(dimension_semantics=("parallel", "arbitrary"), vmem_limit_bytes=VMEM_LIMIT)


def stage_fwd(name, f, grid, ins, outs, carries=(), transposed=()):
    n_in, n_out, n_c, n_t = len(ins), len(outs), len(carries), len(transposed)

    def body(*refs):
        in_refs, out_refs = refs[:n_in], refs[n_in:n_in + n_out]
        hist_refs = refs[n_in + n_out:n_in + n_out + n_c]
        t_refs = refs[n_in + n_out + n_c:n_in + n_out + n_c + n_t]
        c_refs = refs[n_in + n_out + n_c + n_t:]
        if n_c:
            @pl.when(pl.program_id(1) == 0)
            def _():
                for c in c_refs:
                    c[...] = jnp.zeros_like(c)
        cin = [c[...] for c in c_refs]
        for h, c in zip(hist_refs, cin):
            h[...] = c
        o, cout = f(*_load(in_refs, ins), *cin)
        for r, v in zip(out_refs, o):
            r[...] = v.astype(r.dtype)
        for r, k in zip(t_refs, transposed):
            r[...] = o[k].T.astype(r.dtype)
        for c, v in zip(c_refs, cout):
            c[...] = v

    hist_spec = lambda c: pl.BlockSpec((None, None) + tuple(c), lambda o, s: (o, s) + (0,) * len(c))
    flip = lambda o_: pl.BlockSpec(o_.block[::-1], (lambda im: lambda o, s: im(o, s)[::-1])(o_.imap))
    res = pl.pallas_call(
        body, name=name, grid=grid,
        in_specs=[pl.BlockSpec(i.block, i.imap) for i in ins],
        out_specs=[pl.BlockSpec(o.block, o.imap) for o in outs] + [hist_spec(c) for c in carries]
        + [flip(outs[k]) for k in transposed],
        out_shape=[jax.ShapeDtypeStruct(o.shape, o.dtype) for o in outs]
        + [jax.ShapeDtypeStruct(tuple(grid) + tuple(c), F32) for c in carries]
        + [jax.ShapeDtypeStruct(outs[k].shape[::-1], MXU_DTYPE) for k in transposed],
        scratch_shapes=[pltpu.VMEM(tuple(c), F32) for c in carries],
        compiler_params=_stage_params(),
    )(*[i.arr for i in ins])
    res = list(res)
    if transposed:
        return res[:n_out], res[n_out:n_out + n_c], res[n_out + n_c:]
    return res[:n_out], res[n_out:]


def stage_bwd(name, f, grid, ins, outs, cots, carries=(), hists=(), add_to=None, gdtypes=None):
    n_in, n_out, n_c = len(ins), len(outs), len(carries)
    ns = grid[1]
    add_to = add_to or {}
    gdtypes = gdtypes or {}
    add_idx = sorted(add_to)
    g_idx = [k for k, i in enumerate(ins) if i.grad]
    cots = [c if isinstance(c, (tuple, list)) else (c,) for c in cots]
    n_cot = [len(c) for c in cots]
    rev = lambda imap: (lambda o, s: imap(o, ns - 1 - s))

    def body(*refs):
        p = 0
        in_refs = refs[p:p + n_in]; p += n_in
        cot_refs = []
        for cnt in n_cot:
            cot_refs.append(refs[p:p + cnt]); p += cnt
        hist_refs = refs[p:p + n_c]; p += n_c
        add_refs = refs[p:p + len(add_idx)]; p += len(add_idx)
        g_refs = refs[p:p + len(g_idx)]; p += len(g_idx)
        dc_refs = refs[p:]
        first = pl.program_id(1) == 0
        if n_c:
            @pl.when(first)
            def _():
                for c in dc_refs:
                    c[...] = jnp.zeros_like(c)
        vals = _load(in_refs, ins)
        cin = [h[...] for h in hist_refs]
        (o, cout), vjp = jax.vjp(lambda *a: f(*a), *vals, *cin)
        cot_o = []
        for crs, v in zip(cot_refs, o):
            c = crs[0][...].astype(v.dtype)
            for extra in crs[1:]:
                c = c + extra[...].astype(v.dtype)
            cot_o.append(c)
        cot_c = tuple(c[...] for c in dc_refs)
        grads = vjp((tuple(cot_o), cot_c))
        pos, per_in = 0, []
        for i in ins:
            cnt = 1 if i.parts is None else len(i.parts)
            per_in.append(grads[pos:pos + cnt])
            pos += cnt
        dcin = grads[pos:]
        for gr, k in zip(g_refs, g_idx):
            i, gs = ins[k], per_in[k]
            if i.acc:
                @pl.when(first)
                def _(gr=gr):
                    gr[...] = jnp.zeros_like(gr)
                if i.parts is None:
                    gr[...] += gs[0].astype(gr.dtype)
                else:
                    for pt, g in zip(i.parts, gs):
                        gr[pt] += g.astype(gr.dtype)
            else:
                g = gs[0]
                if k in add_to:
                    g = g + add_refs[add_idx.index(k)][...].astype(g.dtype)
                gr[...] = g.astype(gr.dtype)
        for c, v in zip(dc_refs, dcin):
            c[...] = v

    in_specs = [pl.BlockSpec(i.block, rev(i.imap)) for i in ins]
    for o_, cnt in zip(outs, n_cot):
        in_specs += [pl.BlockSpec(o_.block, rev(o_.imap))] * cnt
    in_specs += [pl.BlockSpec((None, None) + tuple(c), (lambda c: (lambda o, s: (o, ns - 1 - s) + (0,) * len(c)))(c))
                 for c in carries]
    in_specs += [pl.BlockSpec(ins[k].block, rev(ins[k].gimap)) for k in add_idx]
    out_specs, out_shape = [], []
    for k in g_idx:
        i = ins[k]
        if i.acc:
            out_specs.append(pl.BlockSpec(i.block, (lambda im: (lambda o, s: im(o, 0)))(i.imap)))
        else:
            out_specs.append(pl.BlockSpec(i.block, rev(i.gimap)))
        out_shape.append(jax.ShapeDtypeStruct(i.gshape, gdtypes.get(k, F32)))
    res = pl.pallas_call(
        body, name=name, grid=grid, in_specs=in_specs, out_specs=out_specs, out_shape=out_shape,
        scratch_shapes=[pltpu.VMEM(tuple(c), F32) for c in carries],
        compiler_params=_stage_params(),
    )(*[i.arr for i in ins], *[a for c in cots for a in c], *hists, *[add_to[k] for k in add_idx])
    return list(res)


def _iota_rows(shape):
    return lax.broadcasted_iota(jnp.int32, shape, 0)


@functools.partial(jax.custom_vjp, nondiff_argnums=(1,))
def _roll_rows(x, s):
    return pltpu.roll(x, s % x.shape[0], 0)


def _roll_rows_fwd(x, s):
    return _roll_rows(x, s), None


def _roll_rows_bwd(s, _, g):
    return (_roll_rows(g, -s),)


_roll_rows.defvjp(_roll_rows_fwd, _roll_rows_bwd)


@jax.custom_vjp
def _drop_head(xx):
    return xx[SUBLANES:]


def _drop_head_fwd(xx):
    return xx[SUBLANES:], None


def _drop_head_bwd(_, g):
    return (jnp.concatenate([jnp.zeros((SUBLANES, g.shape[1]), g.dtype), g], axis=0),)


_drop_head.defvjp(_drop_head_fwd, _drop_head_bwd)


@jax.custom_vjp
def _last_rows(x):
    return x[x.shape[0] - SUBLANES:]


def _last_rows_fwd(x):
    return x[x.shape[0] - SUBLANES:], x.shape[0]


def _last_rows_bwd(n, g):
    return (jnp.concatenate([jnp.zeros((n - SUBLANES, g.shape[1]), g.dtype), g], axis=0),)


_last_rows.defvjp(_last_rows_fwd, _last_rows_bwd)


def _last_row(x):
    n = x.shape[0]
    return jnp.sum(jnp.where(_iota_rows(x.shape) == n - 1, x, 0.0), axis=0, keepdims=True)


def _scan_steps(n):
    s = 1
    while s < n:
        yield s
        s *= 2


def _block_scan_log(a, u, h0):
    n = a.shape[0]
    row = _iota_rows(a.shape)
    for s in _scan_steps(n):
        keep = row >= s
        a_s = jnp.where(keep, pltpu.roll(a, s, 0), 1.0)
        u_s = jnp.where(keep, pltpu.roll(u, s, 0), 0.0)
        u = u + a * u_s
        a = a * a_s
    return u + a * h0


def _block_scan_impl(a, u, edge, reverse=False):
    n, c = a.shape
    nt = n // SUBLANES
    a, u = a.reshape(nt, SUBLANES, c), u.reshape(nt, SUBLANES, c)
    row = lax.broadcasted_iota(jnp.int32, a.shape, 1)
    for s in _scan_steps(SUBLANES):
        keep, shift = (row < SUBLANES - s, SUBLANES - s) if reverse else (row >= s, s)
        a_s = jnp.where(keep, pltpu.roll(a, shift, 1), 1.0)
        u_s = jnp.where(keep, pltpu.roll(u, shift, 1), 0.0)
        u = u + a * u_s
        a = a * a_s
    carry = jnp.broadcast_to(edge, (SUBLANES, c))
    tiles = [None] * nt
    at = 0 if reverse else SUBLANES - 1
    for i in (reversed(range(nt)) if reverse else range(nt)):
        tiles[i] = u[i] + a[i] * carry
        carry = jnp.broadcast_to(tiles[i][at:at + 1, :], (SUBLANES, c))
    return jnp.stack(tiles).reshape(n, c)


@jax.custom_vjp
def _block_scan(a, u, h0):
    return _block_scan_log(a, u, h0)


def _block_scan_fwd(a, u, h0):
    h = _block_scan_impl(a, u, h0)
    return h, (a, h, h0)


def _block_scan_bwd(res, dh):
    a, h, h0 = res
    n = a.shape[0]
    row = _iota_rows(a.shape)
    lam = _block_scan_impl(pltpu.roll(a, n - 1, 0), dh, jnp.zeros_like(h0), reverse=True)
    h_prev = jnp.where(row >= 1, pltpu.roll(h, 1, 0), jnp.broadcast_to(h0, h.shape))
    d_h0 = jnp.sum(jnp.where(row == 0, a * lam, 0.0), axis=0, keepdims=True)
    return lam * h_prev, lam, d_h0


_block_scan.defvjp(_block_scan_fwd, _block_scan_bwd)


def _dot_hi(a, b, dims=(((1,), (0,)), ((), ()))):
    return lax.dot_general(a, b, dims, precision=HI, preferred_element_type=F32)


_NN, _NT, _TN = "nn", "nt", "tn"
_CONTRACT = {_NN: (1, 0), _NT: (1, 1), _TN: (0, 0)}


def _raw_dot(a, b, kind):
    ca, cb = _CONTRACT[kind]
    lead = a.ndim - 2
    dims = (((ca + lead,), (cb + lead,)), (tuple(range(lead)), tuple(range(lead))))
    return lax.dot_general(a.astype(DN_DTYPE), b.astype(DN_DTYPE), dims, preferred_element_type=F32)


@jax.custom_vjp
def _nn(a, b):
    return _raw_dot(a, b, _NN)


_nn.defvjp(lambda a, b: (_raw_dot(a, b, _NN), (a, b)),
           lambda r, g: (_raw_dot(g, r[1], _NT), _raw_dot(r[0], g, _TN)))


@jax.custom_vjp
def _nt(a, b):
    return _raw_dot(a, b, _NT)


_nt.defvjp(lambda a, b: (_raw_dot(a, b, _NT), (a, b)),
           lambda r, g: (_raw_dot(g, r[1], _NN), _raw_dot(g, r[0], _TN)))


@jax.custom_vjp
def _tn(a, b):
    return _raw_dot(a, b, _TN)


_tn.defvjp(lambda a, b: (_raw_dot(a, b, _TN), (a, b)),
           lambda r, g: (_raw_dot(r[1], g, _NT), _raw_dot(r[0], g, _NN)))


def _neumann_inverse(a):
    n = a.shape[-1]
    eye = (lax.broadcasted_iota(jnp.int32, (n, n), 0) == lax.broadcasted_iota(jnp.int32, (n, n), 1)).astype(F32)
    p = _raw_dot(a, a, _NN)
    e = p
    for _ in range(int(math.log2(n)) - 2):
        p = _raw_dot(p, p, _NN)
        e = e + p + _raw_dot(e, p, _NN)
    return eye - a + e - _raw_dot(a, e, _NN)


@jax.custom_vjp
def _unit_lower_inverse(a):
    return _neumann_inverse(a)


def _unit_lower_inverse_fwd(a):
    x = _neumann_inverse(a)
    return x, x


def _unit_lower_inverse_bwd(x, g):
    return (-_raw_dot(_raw_dot(x, g, _TN), x, _NT),)


_unit_lower_inverse.defvjp(_unit_lower_inverse_fwd, _unit_lower_inverse_bwd)


def _softplus(x):
    return jnp.maximum(x, 0.0) + jnp.log1p(jnp.exp(-jnp.abs(x)))


def _neg_expm1(x):
    series = -x * (1.0 + x * (0.5 + x * (1.0 / 6.0 + x * (1.0 / 24.0 + x * (1.0 / 120.0)))))
    return jnp.where(x > -0.03, series, 1.0 - jnp.exp(x))


def f_modulate(x, sc, sh):
    return (x * (1.0 + sc) + sh,), ()


def _deepnorm(x, y, gt, g, b):
    v = DEEPNORM_ALPHA * x + (1.0 + gt) * y
    mu = jnp.mean(v, axis=-1, keepdims=True)
    vc = v - mu
    var = jnp.mean(vc * vc, axis=-1, keepdims=True)
    return vc * lax.rsqrt(var + LN_EPS) * g + b


def f_deepnorm_mod(x, y, gt, g, b, sc, sh):
    x1 = _deepnorm(x, y, gt, g, b)
    return (x1, x1 * (1.0 + sc) + sh), ()


def f_deepnorm_loss(x, y, gt, g, b, target):
    err = _deepnorm(x, y, gt, g, b) - target
    return (0.5 * jnp.mean(err * err, axis=-1, keepdims=True),), ()


def _causal_conv(x, prev, ws):
    xx = jnp.concatenate([prev, x], axis=0)
    k = len(ws)
    y = ws[k - 1] * x
    for j in range(k - 1):
        y = y + ws[j] * _drop_head(_roll_rows(xx, k - 1 - j))
    return y


def f_rg_conv(x, w0, w1, w2, w3, b, prev):
    return (_causal_conv(x, prev, (w0, w1, w2, w3)) + b,), (_last_rows(x),)


def f_dn_conv(x, w0, w1, w2, w3, prev):
    return (jax.nn.silu(_causal_conv(x, prev, (w0, w1, w2, w3))),), (_last_rows(x),)


def f_ffn_act(gp, up, w0, w1, w2, b, prev):
    return (jax.nn.gelu(_causal_conv(gp, prev, (w0, w1, w2)) + b) * up,), (_last_rows(gp),)


def f_rglru(xc, pre_r, pre_i, gr, b_a, b_x, lam, h0):
    gate_r = jax.nn.sigmoid(pre_r + b_a)
    gate_i = jax.nn.sigmoid(pre_i + b_x)
    log_a = -RG_C * gate_r * _softplus(-lam)
    a = jnp.exp(log_a)
    mult = jnp.sqrt(_neg_expm1(2.0 * log_a))
    h = _block_scan(a, mult * gate_i * xc, h0)
    return (h * jax.nn.gelu(gr),), (_last_row(h),)


def f_dn_conv_norm(scale, x, w0, w1, w2, w3, prev):
    y = jax.nn.silu(_causal_conv(x, prev, (w0, w1, w2, w3)))
    return (y * lax.rsqrt(jnp.sum(y * y, axis=-1, keepdims=True) + L2_EPS) * scale,), (_last_rows(x),)


def f_dn_gates(a_in, b_in, a_log, dt_bias):
    g = -jnp.exp(a_log) * _softplus(a_in + dt_bias)
    n = g.shape[0]
    shift = int(math.log2(DN_CHUNK))
    ri = lax.broadcasted_iota(jnp.int32, (n, n), 0)
    ci = lax.broadcasted_iota(jnp.int32, (n, n), 1)
    tri = ((lax.shift_right_logical(ri, shift) == lax.shift_right_logical(ci, shift)) & (ri >= ci)).astype(F32)
    return (_dot_hi(tri, g), jax.nn.sigmoid(b_in)), ()


def f_dn_out(o, z, nw):
    r = lax.rsqrt(jnp.mean(o * o, axis=-1, keepdims=True) + RMS_EPS)
    return (o * r * nw * jax.nn.silu(z),), ()


def f_merge(ga, gb, ya, yb):
    return (jax.nn.sigmoid(ga) * ya + jax.nn.sigmoid(gb) * yb,), ()


def _delta_intra(q, k, v, g_i, g_j, beta):
    c = q.shape[-2]
    ri = lax.broadcasted_iota(jnp.int32, (c, c), 0)
    ci = lax.broadcasted_iota(jnp.int32, (c, c), 1)
    decay = jnp.exp(jnp.where(ri >= ci, g_i - g_j, -jnp.inf))
    g_last = jnp.sum(jnp.where(_iota_rows((c, 1)) == c - 1, g_i, 0.0), axis=-2, keepdims=True)
    exp_g = jnp.exp(g_i)
    kb = k * beta
    t_inv = _unit_lower_inverse(jnp.where(ri > ci, _nt(kb, k) * decay, 0.0))
    u = _nn(t_inv, v * beta)
    w = _nn(t_inv, kb * exp_g)
    return u, w, _nt(q, k) * decay, q * exp_g, k * jnp.exp(g_last - g_i)


def _delta_inter(u, w, qk, q_dec, k_dec, g_last, state):
    v_new = u - _nn(w, state)
    o = _nn(q_dec, state) + _nn(qk, v_new)
    return o, jnp.exp(g_last) * state + _tn(k_dec, v_new)


def _delta_params(sem):
    return pltpu.CompilerParams(dimension_semantics=(sem,), vmem_limit_bytes=VMEM_LIMIT)


def _head_groups(n_vh):
    hb = min(DN_HEAD_GROUP, n_vh)
    return [range(h0, h0 + hb) for h0 in range(0, n_vh, hb)]


def _stack(hs, f):
    return jnp.stack([f(h) for h in hs])


def _rows(ci):
    return slice(ci * DN_CHUNK, (ci + 1) * DN_CHUNK)


def _intra_pairs(n_vh):
    return [(ci, h) for ci in range(DN_INTRA_CHUNKS) for h in range(n_vh)]


def _pair_stack(pairs, ref, width=LANES, head_of=lambda h: h):
    return jnp.stack([ref[_rows(ci), head_of(h) * LANES:head_of(h) * LANES + width] for ci, h in pairs])


def _intra_operands(pairs, rep, q_ref, k_ref, v_ref, g_ref, gt_ref, b_ref):
    qk_head = lambda h: h // rep
    return (_pair_stack(pairs, q_ref, head_of=qk_head), _pair_stack(pairs, k_ref, head_of=qk_head),
            _pair_stack(pairs, v_ref), jnp.stack([g_ref[_rows(ci), h:h + 1] for ci, h in pairs]),
            jnp.stack([gt_ref[ci, h:h + 1, :] for ci, h in pairs]),
            jnp.stack([b_ref[_rows(ci), h:h + 1] for ci, h in pairs]))


def _intra_spec(width, col=0):
    return pl.BlockSpec((DN_INTRA_CHUNKS * DN_CHUNK, width), lambda s: (s, col))


def _inter_operands(hs, ci, u_ref, w_ref, qk_ref, qd_ref, kd_ref, g_ref):
    f32 = lambda ref, width=LANES: _stack(hs, lambda h: ref[_rows(ci), h * LANES:h * LANES + width].astype(F32))
    last = (ci + 1) * DN_CHUNK - 1
    return (f32(u_ref), f32(w_ref), f32(qk_ref, DN_CHUNK), f32(qd_ref), f32(kd_ref),
            _stack(hs, lambda h: g_ref[last:last + 1, h:h + 1]))


def _inter_spec(width, steps, reverse=False):
    rows = DN_INTER_CHUNKS * DN_CHUNK
    return pl.BlockSpec((rows, width), (lambda s: (steps - 1 - s, 0)) if reverse else (lambda s: (s, 0)))


def delta_intra_fwd(qn, kn, qkv, v_blk, big_g, big_gt, beta, n_vh):
    t, qk_w = qn.shape
    vdim = n_vh * LANES
    rep = vdim // qk_w
    nc = t // DN_CHUNK

    pairs = _intra_pairs(n_vh)

    def body(q_ref, k_ref, v_ref, g_ref, gt_ref, b_ref, u_ref, w_ref, qk_ref, qd_ref, kd_ref):
        u, w, qk, qd, kd = _delta_intra(*_intra_operands(pairs, rep, q_ref, k_ref, v_ref, g_ref, gt_ref, b_ref))
        for i, (ci, h) in enumerate(pairs):
            at = (_rows(ci), slice(h * LANES, (h + 1) * LANES))
            u_ref[at] = u[i]
            w_ref[at] = w[i].astype(w_ref.dtype)
            qk_ref[at] = jnp.concatenate([qk[i], jnp.zeros_like(qk[i])], axis=1).astype(qk_ref.dtype)
            qd_ref[at] = qd[i].astype(qd_ref.dtype)
            kd_ref[at] = kd[i].astype(kd_ref.dtype)

    return pl.pallas_call(
        body, name="delta_intra_fwd", grid=(nc // DN_INTRA_CHUNKS,),
        in_specs=[_intra_spec(qk_w), _intra_spec(qk_w), _intra_spec(vdim, v_blk), _intra_spec(LANES),
                  pl.BlockSpec((DN_INTRA_CHUNKS, n_vh, DN_CHUNK), lambda s: (s, 0, 0)), _intra_spec(LANES)],
        out_specs=[_intra_spec(vdim)] * 5,
        out_shape=[jax.ShapeDtypeStruct((t, vdim), F32)] + [jax.ShapeDtypeStruct((t, vdim), DN_DTYPE)] * 4,
        compiler_params=_delta_params("parallel"),
    )(qn, kn, qkv, big_g, big_gt, beta)


def delta_inter_fwd(u, w, qk, q_dec, k_dec, big_g, n_vh):
    t, vdim = u.shape
    nc = t // DN_CHUNK

    cpb = DN_INTER_CHUNKS
    steps = nc // cpb

    def body(u_ref, w_ref, qk_ref, qd_ref, kd_ref, g_ref, o_ref, hist_ref, s_ref):
        @pl.when(pl.program_id(0) == 0)
        def _():
            s_ref[...] = jnp.zeros_like(s_ref)
        for ci in range(cpb):
            for hs in _head_groups(n_vh):
                grp = slice(hs[0], hs[-1] + 1)
                st = s_ref[grp]
                hist_ref[ci, grp] = st
                o, ns = _delta_inter(*_inter_operands(hs, ci, u_ref, w_ref, qk_ref, qd_ref, kd_ref, g_ref), st)
                for i, h in enumerate(hs):
                    o_ref[_rows(ci), h * LANES:(h + 1) * LANES] = o[i]
                s_ref[grp] = ns

    return pl.pallas_call(
        body, name="delta_inter_fwd", grid=(steps,),
        in_specs=[_inter_spec(vdim, steps)] * 5 + [_inter_spec(LANES, steps)],
        out_specs=[_inter_spec(vdim, steps), pl.BlockSpec((cpb, n_vh, LANES, LANES), lambda s: (s, 0, 0, 0))],
        out_shape=[jax.ShapeDtypeStruct((t, vdim), F32), jax.ShapeDtypeStruct((nc, n_vh, LANES, LANES), F32)],
        scratch_shapes=[pltpu.VMEM((n_vh, LANES, LANES), F32)],
        compiler_params=_delta_params("arbitrary"),
    )(u, w, qk, q_dec, k_dec, big_g)


def delta_inter_bwd(u, w, qk, q_dec, k_dec, big_g, hist, d_o, n_vh):
    t, vdim = u.shape
    nc = t // DN_CHUNK

    cpb = DN_INTER_CHUNKS
    steps = nc // cpb

    def body(u_ref, w_ref, qk_ref, qd_ref, kd_ref, g_ref, hist_ref, do_ref,
             du_ref, dw_ref, dqk_ref, dqd_ref, dkd_ref, dg_ref, ds_ref):
        @pl.when(pl.program_id(0) == 0)
        def _():
            ds_ref[...] = jnp.zeros_like(ds_ref)
        lane = lax.broadcasted_iota(jnp.int32, (1, LANES), 1)
        last = _iota_rows((DN_CHUNK, LANES)) == DN_CHUNK - 1
        for ci in reversed(range(cpb)):
            dgl_all = jnp.zeros((1, LANES), F32)
            for hs in _head_groups(n_vh):
                grp = slice(hs[0], hs[-1] + 1)
                prim = _inter_operands(hs, ci, u_ref, w_ref, qk_ref, qd_ref, kd_ref, g_ref) + (hist_ref[ci, grp],)
                _, vjp = jax.vjp(_delta_inter, *prim)
                cot_o = _stack(hs, lambda h: do_ref[_rows(ci), h * LANES:(h + 1) * LANES])
                du, dw, dqk, dqd, dkd, dgl, dst = vjp((cot_o, ds_ref[grp]))
                ds_ref[grp] = dst
                for i, h in enumerate(hs):
                    sl = slice(h * LANES, (h + 1) * LANES)
                    du_ref[_rows(ci), sl] = du[i].astype(du_ref.dtype)
                    dw_ref[_rows(ci), sl] = dw[i].astype(dw_ref.dtype)
                    dqk_ref[_rows(ci), sl] = jnp.concatenate([dqk[i], jnp.zeros_like(dqk[i])], axis=1)
                    dqd_ref[_rows(ci), sl] = dqd[i]
                    dkd_ref[_rows(ci), sl] = dkd[i]
                    dgl_all = dgl_all + dgl[i] * (lane == h).astype(F32)
            dg_ref[_rows(ci), :] = jnp.where(last, jnp.broadcast_to(dgl_all, (DN_CHUNK, LANES)), 0.0)

    rv = lambda w_: _inter_spec(w_, steps, reverse=True)
    return pl.pallas_call(
        body, name="delta_inter_bwd", grid=(steps,),
        in_specs=[rv(vdim)] * 5 + [rv(LANES), pl.BlockSpec((cpb, n_vh, LANES, LANES), lambda s: (steps - 1 - s, 0, 0, 0)),
                                   rv(vdim)],
        out_specs=[rv(vdim)] * 5 + [rv(LANES)],
        out_shape=[jax.ShapeDtypeStruct((t, vdim), DN_DTYPE)] * 2 + [jax.ShapeDtypeStruct((t, vdim), F32)] * 3
        + [jax.ShapeDtypeStruct((t, LANES), F32)],
        scratch_shapes=[pltpu.VMEM((n_vh, LANES, LANES), F32)],
        compiler_params=_delta_params("arbitrary"),
    )(u, w, qk, q_dec, k_dec, big_g, hist, d_o)


def delta_intra_bwd(qn, kn, qkv, v_blk, big_g, big_gt, beta, cots, n_vh):
    t, qk_w = qn.shape
    vdim = n_vh * LANES
    rep = vdim // qk_w
    nc = t // DN_CHUNK

    pairs = _intra_pairs(n_vh)

    def body(q_ref, k_ref, v_ref, g_ref, gt_ref, b_ref, du_ref, dw_ref, dqk_ref, dqd_ref, dkd_ref,
             dq_ref, dk_ref, dv_ref, dg_ref, dgt_ref, db_ref):
        lane = lax.broadcasted_iota(jnp.int32, (1, LANES), 1)
        _, vjp = jax.vjp(_delta_intra, *_intra_operands(pairs, rep, q_ref, k_ref, v_ref, g_ref, gt_ref, b_ref))
        dq, dk, dv, dgi, dgj, db = vjp((_pair_stack(pairs, du_ref).astype(F32), _pair_stack(pairs, dw_ref).astype(F32),
                                        _pair_stack(pairs, dqk_ref, DN_CHUNK), _pair_stack(pairs, dqd_ref),
                                        _pair_stack(pairs, dkd_ref)))
        dg_all, db_all = {}, {}
        dq_acc, dk_acc = None, None
        for i, (ci, h) in enumerate(pairs):
            j = h // rep
            dv_ref[_rows(ci), h * LANES:(h + 1) * LANES] = dv[i]
            dgt_ref[ci, h:h + 1, :] = dgj[i]
            onehot = (lane == h).astype(F32)
            dg_all[ci] = dgi[i] * onehot + dg_all.get(ci, 0.0)
            db_all[ci] = db[i] * onehot + db_all.get(ci, 0.0)
            dq_acc = dq[i] if h % rep == 0 else dq_acc + dq[i]
            dk_acc = dk[i] if h % rep == 0 else dk_acc + dk[i]
            if h % rep == rep - 1:
                dq_ref[_rows(ci), j * LANES:(j + 1) * LANES] = dq_acc
                dk_ref[_rows(ci), j * LANES:(j + 1) * LANES] = dk_acc
        for ci in dg_all:
            dg_ref[_rows(ci), :] = dg_all[ci]
            db_ref[_rows(ci), :] = db_all[ci]

    gt_spec = pl.BlockSpec((DN_INTRA_CHUNKS, n_vh, DN_CHUNK), lambda s: (s, 0, 0))
    return pl.pallas_call(
        body, name="delta_intra_bwd", grid=(nc // DN_INTRA_CHUNKS,),
        in_specs=[_intra_spec(qk_w), _intra_spec(qk_w), _intra_spec(vdim, v_blk), _intra_spec(LANES), gt_spec,
                  _intra_spec(LANES)] + [_intra_spec(vdim)] * 5,
        out_specs=[_intra_spec(qk_w), _intra_spec(qk_w), _intra_spec(vdim), _intra_spec(LANES), gt_spec,
                   _intra_spec(LANES)],
        out_shape=[jax.ShapeDtypeStruct((t, qk_w), F32), jax.ShapeDtypeStruct((t, qk_w), F32),
                   jax.ShapeDtypeStruct((t, vdim), F32), jax.ShapeDtypeStruct((t, LANES), F32),
                   jax.ShapeDtypeStruct((nc, n_vh, DN_CHUNK), F32), jax.ShapeDtypeStruct((t, LANES), F32)],
        compiler_params=_delta_params("parallel"),
    )(qn, kn, qkv, big_g, big_gt, beta, *cots)


def all_gather(name, arrs):
    n = len(arrs)

    def body(*refs):
        in_refs, out_refs, sems = refs[:n], refs[n:2 * n], refs[2 * n:]
        _exchange_copies(in_refs, out_refs, sems, False, "start")
        _exchange_copies(in_refs, out_refs, sems, False, "wait")

    res = pl.pallas_call(
        body, name=name,
        in_specs=[_HBM] * n, out_specs=[_HBM] * n,
        out_shape=_exchange_out_shape(arrs, False), scratch_shapes=_exchange_sems(n),
        compiler_params=pltpu.CompilerParams(has_side_effects=True),
    )(*arrs)
    return list(res)


def _adamw_math(w, g, m, v):
    m = ADAM_B1 * m + (1.0 - ADAM_B1) * g
    v = ADAM_B2 * v + (1.0 - ADAM_B2) * (g * g)
    m_hat = m / (1.0 - ADAM_B1 ** ADAM_STEP)
    v_hat = v / (1.0 - ADAM_B2 ** ADAM_STEP)
    delta = -ADAM_LR * (m_hat / (jnp.sqrt(v_hat) + ADAM_EPS) + ADAM_WD * w)
    return delta, m, v


def adamw(name, w, parts, m, v, rows_cap=128):
    r, c = w.shape
    np_ = parts.shape[0]
    tr = _tile(r, rows_cap, SUBLANES * (4 // parts.dtype.itemsize))

    def body(w_ref, p_ref, m_ref, v_ref, g_ref, d_ref, nm_ref, nv_ref):
        g = p_ref[0].astype(F32)
        for k in range(1, np_):
            g = g + p_ref[k].astype(F32)
        delta, nm, nv = _adamw_math(w_ref[...], g, m_ref[...], v_ref[...])
        g_ref[...] = g
        d_ref[...] = delta
        nm_ref[...] = nm
        nv_ref[...] = nv

    spec = pl.BlockSpec((tr, c), lambda i: (i, 0))
    return pl.pallas_call(
        body, name=name, grid=(r // tr,),
        in_specs=[spec, pl.BlockSpec((np_, tr, c), lambda i: (0, i, 0)), spec, spec],
        out_specs=[spec] * 4, out_shape=[jax.ShapeDtypeStruct((r, c), F32)] * 4,
        compiler_params=pltpu.CompilerParams(dimension_semantics=("parallel",), vmem_limit_bytes=VMEM_LIMIT),
    )(w, parts, m, v)


def sum_parts(name, parts, rows_cap=256):
    np_, r, c = parts.shape
    tr = _tile(r, rows_cap, SUBLANES)

    def body(p_ref, o_ref):
        g = p_ref[0].astype(F32)
        for k in range(1, np_):
            g = g + p_ref[k].astype(F32)
        o_ref[...] = g

    return pl.pallas_call(
        body, name=name, grid=(r // tr,),
        in_specs=[pl.BlockSpec((np_, tr, c), lambda i: (0, i, 0))],
        out_specs=pl.BlockSpec((tr, c), lambda i: (i, 0)),
        out_shape=jax.ShapeDtypeStruct((r, c), F32),
        compiler_params=pltpu.CompilerParams(dimension_semantics=("parallel",), vmem_limit_bytes=VMEM_LIMIT),
    )(parts)


def _pack(arrs):
    flat = jnp.concatenate([a.reshape(-1).astype(F32) for a in arrs])
    n = flat.shape[0]
    return jnp.pad(flat, (0, _round_up(n, LANES * SUBLANES) - n)).reshape(-1, LANES)


def _unpack(packed, like):
    flat, out, pos = packed.reshape(-1), [], 0
    for a in like:
        out.append(flat[pos:pos + a.size].reshape(a.shape))
        pos += a.size
    return out


def kernel(x, c, w_ada, b_ada, w_in, rg_conv_w, rg_conv_b, rg_w_a, rg_b_a, rg_w_x, rg_b_x, rg_lambda, dn_conv_w, dn_a_log, dn_dt_bias, dn_norm_w, w_proj_a, w_proj_b, w_out, ln1_g, ln1_b, ffn_w_gate, ffn_w_up, ffn_conv_w, ffn_conv_b, ffn_w_down, ln2_g, ln2_b, loss_target, m_w_ada, m_b_ada, m_w_in, m_rg_conv_w, m_rg_conv_b, m_rg_w_a, m_rg_b_a, m_rg_w_x, m_rg_b_x, m_rg_lambda, m_dn_conv_w, m_dn_a_log, m_dn_dt_bias, m_dn_norm_w, m_w_proj_a, m_w_proj_b, m_w_out, m_ln1_g, m_ln1_b, m_ffn_w_gate, m_ffn_w_up, m_ffn_conv_w, m_ffn_conv_b, m_ffn_w_down, m_ln2_g, m_ln2_b, v_w_ada, v_b_ada, v_w_in, v_rg_conv_w, v_rg_conv_b, v_rg_w_a, v_rg_b_a, v_rg_w_x, v_rg_b_x, v_rg_lambda, v_dn_conv_w, v_dn_a_log, v_dn_dt_bias, v_dn_norm_w, v_w_proj_a, v_w_proj_b, v_w_out, v_ln1_g, v_ln1_b, v_ffn_w_gate, v_ffn_w_up, v_ffn_conv_w, v_ffn_conv_b, v_ffn_w_down, v_ln2_g, v_ln2_b):
    names = ['w_ada', 'b_ada', 'w_in', 'rg_conv_w', 'rg_conv_b', 'rg_w_a', 'rg_b_a', 'rg_w_x', 'rg_b_x', 'rg_lambda',
             'dn_conv_w', 'dn_a_log', 'dn_dt_bias', 'dn_norm_w', 'w_proj_a', 'w_proj_b', 'w_out', 'ln1_g', 'ln1_b',
             'ffn_w_gate', 'ffn_w_up', 'ffn_conv_w', 'ffn_conv_b', 'ffn_w_down', 'ln2_g', 'ln2_b']
    loc = locals()
    W = {n: loc[n][0] for n in names}
    M = {n: loc['m_' + n][0] for n in names}
    V = {n: loc['v_' + n][0] for n in names}

    me = 4 * lax.axis_index("x") + 2 * lax.axis_index("y") + lax.axis_index("c")
    xs, tgt = x[0], loss_target[0]
    t, d = xs.shape
    d_rnn = W['rg_conv_b'].shape[0]
    n_blk = W['rg_w_a'].shape[0]
    n_vh = W['dn_a_log'].shape[0]
    assert W['dn_norm_w'].shape[0] == LANES
    vdim = n_vh * LANES
    d_ff = W['ffn_conv_b'].shape[0]
    d_in = W['w_in'].shape[1] * N_DEV
    qk = (d_in - 2 * d_rnn - 2 * vdim - 2 * n_vh - 2 * d) // 2
    assert vdim == 2 * qk and qk % LANES == 0 and n_vh <= LANES
    splits = (d_rnn, d_rnn, qk, qk, vdim, vdim, n_vh, n_vh, d, d)
    offs = [0]
    for s_ in splits:
        offs.append(offs[-1] + s_)

    tb = _tile(t, 256, SUBLANES)

    big = ['w_in', 'w_proj_a', 'w_proj_b', 'w_out', 'ffn_w_gate', 'ffn_w_up', 'ffn_w_down']
    small_sh = ['rg_conv_w', 'dn_conv_w', 'ffn_conv_w']
    first = all_gather("gather_first", [W['w_in'].astype(WIRE_DTYPE)] + [W[n] for n in small_sh] + [c])
    g_in, g_rcw, g_dcw, g_fcw, c_all = first
    cols = lambda g: jnp.transpose(g, (1, 0, 2)).reshape(g.shape[1], -1)
    rows = lambda g: g.reshape(-1, g.shape[2])
    w_in_f = cols(g_in)
    padl = lambda a: jnp.pad(a, ((0, 0), (0, LANES - a.shape[1])))
    groups = [w_in_f[:, offs[i]:offs[i + 1]] for i in range(10)]
    groups[6], groups[7] = padl(groups[6]), padl(groups[7])
    go = [0]
    for g_ in groups:
        go.append(go[-1] + g_.shape[1])
    n_pad = _round_up(go[-1], 512)
    wp = jnp.pad(jnp.concatenate(groups, axis=1), ((0, 0), (0, n_pad - go[-1])))
    o_xr, o_gr, o_q, o_k, o_v, o_z, o_a, o_b, o_ga, o_gb = go[:10]
    rcw, dcw, fcw = cols(g_rcw), cols(g_dcw), cols(g_fcw)
    eye_b = jnp.eye(n_blk, dtype=F32)
    bd = lambda w: (w[:, :, None, :] * eye_b[:, None, :, None]).reshape(d_rnn, d_rnn)
    w_bd = jnp.concatenate([bd(W['rg_w_a']), bd(W['rg_w_x'])], axis=1)
    row1 = lambda a: a.reshape(1, -1)
    padv = lambda a: jnp.pad(row1(a), ((0, 0), (0, LANES - a.shape[0])))
    nw_t = jnp.tile(row1(W['dn_norm_w']), (1, n_vh))

    c_pad =jnp.pad(c_all.reshape(N_DEV, d), ((0, LANES - N_DEV), (0, 0)))
    ada_w = W['w_ada'].shape[1]
    b_ada_me = lax.dynamic_slice(W['b_ada'], (me * ada_w,), (ada_w,)).reshape(1, ada_w)
    ada_sh = mm(c_pad, W['w_ada'], name="ada_fwd", a_act="silu", bias=b_ada_me)
    (ada_all,) = all_gather("gather_ada", [ada_sh[:N_DEV]])
    ada_me = lax.dynamic_slice(ada_all, (0, me, 0), (N_DEV, 1, ada_w)).reshape(6, 1, d)
    sh1, sc1, gt1, sh2, sc2, gt2 = [ada_me[i] for i in range(6)]

    nt = t // tb

    def act(a, bw, col0=0, width=None, grad=True, rows=tb):
        width = a.shape[1] if width is None else width
        assert col0 % bw == 0 and width % bw == 0
        c0 = col0 // bw
        return In(a, (rows, bw), lambda o, s: (s, c0 + o), grad=grad, gshape=(t, width), gimap=lambda o, s: (s, o))

    def prm(a, bw, parts=None):
        return In(a, (a.shape[0], bw), lambda o, s: (0, o), acc=True, parts=parts)

    def out(width, bw, rows=tb, dtype=F32):
        return Out((t, width), (rows, bw), lambda o, s: (s, o), dtype)

    tbh = _tile(t, 2048, SUBLANES)
    nth = t // tbh
    tbc = _tile(t, 1024, SUBLANES)
    ntc = t // tbc

    krows = lambda k_: [(slice(j, j + 1), slice(None)) for j in range(k_)]

    mod1_ins = [act(xs, d), prm(sc1, d), prm(sh1, d)]
    (h1,), _, (h1_t,) = stage_fwd("mod1_fwd", f_modulate, (1, nt), mod1_ins, [out(d, d, dtype=MXU_DTYPE)],
                                  transposed=[0])
    proj, g_pa, g_pb, g_out, g_fg, g_fu, g_fd = mm(h1, wp, name="proj_fwd",
                                                   gather=[W[n].astype(WIRE_DTYPE) for n in big[1:]])
    w_pa, w_pb, w_o, w_fd = rows(g_pa), rows(g_pb), rows(g_out), rows(g_fd)
    w_gate, w_up = cols(g_fg), cols(g_fu)
    w_gu = jnp.concatenate([w_gate, w_up], axis=1)

    cb_r = _tile(math.gcd(d_rnn, o_gr), 256)
    rgc_ins = [act(proj, cb_r, o_xr, d_rnn, rows=tbc), prm(rcw, cb_r, krows(4)), prm(row1(W['rg_conv_b']), cb_r)]
    rgc_grid, rgc_car, rgc_outs = (d_rnn // cb_r, ntc), [(SUBLANES, cb_r)], [out(d_rnn, cb_r, tbc)]
    (xc,), rgc_hist, (xc_t,) = stage_fwd("rg_conv_fwd", f_rg_conv, rgc_grid, rgc_ins, rgc_outs, rgc_car, transposed=[0])
    gates = mm(xc, w_bd, name="rg_gates_fwd")
    lru_ins = [act(xc, cb_r), act(gates, cb_r, 0, d_rnn), act(gates, cb_r, d_rnn, d_rnn), act(proj, cb_r, o_gr, d_rnn),
               prm(row1(W['rg_b_a']), cb_r), prm(row1(W['rg_b_x']), cb_r), prm(row1(W['rg_lambda']), cb_r)]
    lru_grid, lru_car = (d_rnn // cb_r, nt), [(1, cb_r)]
    (rec,), lru_hist, (rec_t,) = stage_fwd("rglru_fwd", f_rglru, lru_grid, lru_ins,
                                           [out(d_rnn, cb_r, dtype=MXU_DTYPE)], lru_car,
                                           transposed=[0])
    y_a = mm(rec, w_pa, name="proj_a_fwd")

    dnc = {}
    for nm, col0, width, w0, cb_, f_ in (("q", o_q, qk, 0, LANES, functools.partial(f_dn_conv_norm, LANES ** -0.5)),
                                         ("k", o_k, qk, qk, LANES, functools.partial(f_dn_conv_norm, 1.0)),
                                         ("v", o_v, vdim, 2 * qk, _tile(math.gcd(vdim, o_v), 256), f_dn_conv)):
        ins_ = [act(proj, cb_, col0, width, rows=tbc), prm(dcw[:, w0:w0 + width], cb_, krows(4))]
        grid_, outs_, car_ = (width // cb_, ntc), [out(width, cb_, tbc)], [(SUBLANES, cb_)]
        (y_,), hist_ = stage_fwd("dn_conv_%s_fwd" % nm, f_, grid_, ins_, outs_, car_)
        dnc[nm] = (y_, f_, ins_, grid_, outs_, car_, hist_)
    qn, kn, v_c = dnc["q"][0], dnc["k"][0], dnc["v"][0]
    gate_ins = [act(proj, LANES, o_a, LANES), act(proj, LANES, o_b, LANES),
                prm(padv(W['dn_a_log']), LANES), prm(padv(W['dn_dt_bias']), LANES)]
    gate_outs = [out(LANES, LANES), out(LANES, LANES)]
    (g_dn, beta_dn), _ = stage_fwd("dn_gates_fwd", f_dn_gates, (1, nt), gate_ins, gate_outs)
    n_ch = t // DN_CHUNK
    gt_dn = jnp.transpose(g_dn.reshape(n_ch, DN_CHUNK, LANES)[:, :, :n_vh], (0, 2, 1))
    dn_mid = delta_intra_fwd(qn, kn, v_c, 0, g_dn, gt_dn, beta_dn, n_vh)
    o_dn, dn_hist = delta_inter_fwd(*dn_mid, g_dn, n_vh)
    dno_ins = [act(o_dn, LANES, rows=tbh), act(proj, LANES, o_z, vdim, rows=tbh), prm(nw_t, LANES)]
    dno_grid, dno_outs = (n_vh, nth), [out(vdim, LANES, tbh, MXU_DTYPE)]
    (dn,), _, (dn_t,) = stage_fwd("dn_out_fwd", f_dn_out, dno_grid, dno_ins, dno_outs, transposed=[0])
    y_b = mm(dn, w_pb, name="proj_b_fwd")

    cb_m = _tile(math.gcd(math.gcd(d, o_ga), o_gb), 512)
    mrg_ins = [act(proj, cb_m, o_ga, d, rows=tbc), act(proj, cb_m, o_gb, d, rows=tbc), act(y_a, cb_m, rows=tbc),
               act(y_b, cb_m, rows=tbc)]
    mrg_grid, mrg_outs = (d // cb_m, ntc), [out(d, cb_m, tbc, MXU_DTYPE)]
    (merged,), _, (merged_t,) = stage_fwd("merge_fwd", f_merge, mrg_grid, mrg_ins, mrg_outs, transposed=[0])
    mix = mm(merged, w_o, name="w_out_fwd")
    ln1_ins = [act(xs, d), act(mix, d), prm(gt1, d), prm(row1(W['ln1_g']), d), prm(row1(W['ln1_b']), d),
               prm(sc2, d), prm(sh2, d)]
    ln1_outs = [out(d, d), out(d, d, dtype=MXU_DTYPE)]
    (x1, h2), _, (h2_t,) = stage_fwd("ln1_mod2_fwd", f_deepnorm_mod, (1, nt), ln1_ins, ln1_outs, transposed=[1])

    gu = mm(h2, w_gu, name="ffn_in_fwd")
    cb_f = _tile(d_ff, 256)
    ffa_ins = [act(gu, cb_f, 0, d_ff, rows=tbc), act(gu, cb_f, d_ff, d_ff, rows=tbc), prm(fcw, cb_f, krows(3)),
               prm(row1(W['ffn_conv_b']), cb_f)]
    ffa_grid, ffa_car, ffa_outs = (d_ff // cb_f, ntc), [(SUBLANES, cb_f)], [out(d_ff, cb_f, tbc, MXU_DTYPE)]
    (act_ff,), ffa_hist, (act_t,) = stage_fwd("ffn_act_fwd", f_ffn_act, ffa_grid, ffa_ins, ffa_outs, ffa_car,
                                              transposed=[0])
    ff = mm(act_ff, w_fd, name="ffn_down_fwd")
    ln2_ins = [act(x1, d), act(ff, d), prm(gt2, d), prm(row1(W['ln2_g']), d), prm(row1(W['ln2_b']), d),
               act(tgt, d, grad=False)]
    ln2_outs = [Out((t, 1), (tb, 1), lambda o, s: (s, 0))]
    (loss_rows,), _ = stage_fwd("ln2_loss_fwd", f_deepnorm_loss, (1, nt), ln2_ins, ln2_outs)

    dx1_a, d_ff_o, d_gt2, d_ln2g, d_ln2b = stage_bwd("ln2_loss_bwd", f_deepnorm_loss, (1, nt), ln2_ins, ln2_outs,
                                                     [jnp.ones((t, 1), F32)], gdtypes={1: MXU_DTYPE})
    d_act = mm(d_ff_o, w_fd, name="ffn_down_bwd_x", tb=True)
    gw_fd = mm(act_t, d_ff_o, name="ffn_down_bwd_w")
    d_gp, d_up, d_fcw, d_fcb = stage_bwd("ffn_act_bwd", f_ffn_act, ffa_grid, ffa_ins, ffa_outs, [d_act],
                                         ffa_car, ffa_hist, gdtypes={0: MXU_DTYPE, 1: MXU_DTYPE})
    col_blocks = lambda g: jnp.transpose(g.reshape(g.shape[0], N_DEV, -1), (1, 0, 2)).astype(WIRE_DTYPE)
    row_blocks = lambda g: g.reshape(N_DEV, -1, g.shape[1]).astype(WIRE_DTYPE)
    big_parts = {}
    d_h2, big_parts['ffn_w_down'] = mm([d_gp, d_up], [w_gate, w_up], name="ffn_in_bwd_x", tb=True,
                                       scatter=[row_blocks(gw_fd)])
    gw_gate, gw_up = mm(h2_t, d_gp, name="ffn_gate_bwd_w"), mm(h2_t, d_up, name="ffn_up_bwd_w")
    dx_a, d_mix, d_gt1, d_ln1g, d_ln1b, d_sc2, d_sh2 = stage_bwd("ln1_mod2_bwd", f_deepnorm_mod, (1, nt), ln1_ins,
                                                                 ln1_outs, [dx1_a, d_h2], gdtypes={1: MXU_DTYPE})
    d_merged = mm(d_mix, w_o, name="w_out_bwd_x", tb=True)
    gw_o = mm(merged_t, d_mix, name="w_out_bwd_w")
    d_ga, d_gb, d_ya, d_yb = stage_bwd("merge_bwd", f_merge, mrg_grid, mrg_ins, mrg_outs, [d_merged],
                                       gdtypes={k_: MXU_DTYPE for k_ in range(4)})
    d_rec = mm(d_ya, w_pa, name="proj_a_bwd_x", tb=True)
    gw_pa = mm(rec_t, d_ya, name="proj_a_bwd_w")
    d_dn = mm(d_yb, w_pb, name="proj_b_bwd_x", tb=True)
    gw_pb = mm(dn_t, d_yb, name="proj_b_bwd_w")

    d_o, d_z, d_nwt = stage_bwd("dn_out_bwd", f_dn_out, dno_grid, dno_ins, dno_outs, [d_dn], gdtypes={1: MXU_DTYPE})
    *d_mid, d_g_state = delta_inter_bwd(*dn_mid, g_dn, dn_hist, d_o, n_vh)
    d_qn, d_kn, d_v, d_g_col, d_gt, d_beta = delta_intra_bwd(qn, kn, v_c, 0, g_dn, gt_dn, beta_dn, d_mid, n_vh)
    d_g_row = jnp.pad(jnp.transpose(d_gt, (0, 2, 1)).reshape(t, n_vh), ((0, 0), (0, LANES - n_vh)))
    d_a, d_b, d_alog, d_dtb = stage_bwd("dn_gates_bwd", f_dn_gates, (1, nt), gate_ins, gate_outs,
                                        [(d_g_state, d_g_col, d_g_row), d_beta], gdtypes={0: MXU_DTYPE, 1: MXU_DTYPE})
    d_win, d_dcw = {}, []
    for nm, cot in (("q", d_qn), ("k", d_kn), ("v", d_v)):
        _, f_, ins_, grid_, outs_, car_, hist_ = dnc[nm]
        d_win[nm], dw_ = stage_bwd("dn_conv_%s_bwd" % nm, f_, grid_, ins_, outs_, [cot], car_, hist_,
                                   gdtypes={0: MXU_DTYPE})
        d_dcw.append(dw_)
    d_dcw = jnp.concatenate(d_dcw, axis=1)

    d_xc_a, d_pr, d_pi, d_gr, d_ba, d_bx, d_lam = stage_bwd(
        "rglru_bwd", f_rglru, lru_grid, lru_ins, [out(d_rnn, cb_r)], [d_rec], lru_car, lru_hist,
        gdtypes={1: MXU_DTYPE, 2: MXU_DTYPE, 3: MXU_DTYPE})
    d_xc_b, big_parts['w_out'], big_parts['w_proj_a'], big_parts['w_proj_b'] = mm(
        [d_pr, d_pi], [w_bd[:, :d_rnn], w_bd[:, d_rnn:]], name="rg_gates_bwd_x", tb=True,
        scatter=[row_blocks(gw_o), row_blocks(gw_pa), row_blocks(gw_pb)])
    gw_bd_a, gw_bd_x = mm(xc_t, d_pr, name="rg_gate_a_bwd_w"), mm(xc_t, d_pi, name="rg_gate_x_bwd_w")
    d_xr, d_rcw, d_rcb = stage_bwd("rg_conv_bwd", f_rg_conv, rgc_grid, rgc_ins, rgc_outs, [(d_xc_a, d_xc_b)],
                                   rgc_car, rgc_hist, gdtypes={0: MXU_DTYPE})

    diag = lambda g: jnp.einsum('nimj,nm->nij', g.reshape(n_blk, d_rnn // n_blk, n_blk, d_rnn // n_blk), eye_b)
    small_names = ['rg_conv_w', 'rg_conv_b', 'rg_w_a', 'rg_b_a', 'rg_w_x', 'rg_b_x', 'rg_lambda', 'dn_conv_w',
                   'dn_a_log', 'dn_dt_bias', 'dn_norm_w', 'ln1_g', 'ln1_b', 'ffn_conv_w', 'ffn_conv_b', 'ln2_g', 'ln2_b']
    small_loc = {
        'rg_conv_w': d_rcw, 'rg_conv_b': d_rcb,
        'rg_w_a': diag(gw_bd_a), 'rg_b_a': d_ba, 'rg_w_x': diag(gw_bd_x), 'rg_b_x': d_bx,
        'rg_lambda': d_lam, 'dn_conv_w': d_dcw, 'dn_a_log': d_alog[:, :n_vh], 'dn_dt_bias': d_dtb[:, :n_vh],
        'dn_norm_w': jnp.sum(d_nwt.reshape(n_vh, LANES), axis=0), 'ln1_g': d_ln1g, 'ln1_b': d_ln1b,
        'ffn_conv_w': d_fcw, 'ffn_conv_b': d_fcb, 'ln2_g': d_ln2g, 'ln2_b': d_ln2b}
    small_list = [small_loc[n] for n in small_names]

    d_segs = [d_xr, d_gr, d_win["q"], d_win["k"], d_win["v"], d_z, d_a, d_b, d_ga, d_gb]
    riders = {4: ('ffn_w_gate', gw_gate), 5: ('ffn_w_up', gw_up)}
    gw_segs = []
    for i, dg in enumerate(d_segs):
        if i in riders:
            g_, big_parts[riders[i][0]] = mm(h1_t, dg, name="proj_bwd_w%d" % i, scatter=[col_blocks(riders[i][1])])
        else:
            g_ = mm(h1_t, dg, name="proj_bwd_w%d" % i)
        gw_segs.append(g_)
    gw_in = jnp.concatenate([g_[:, :splits[i]] for i, g_ in enumerate(gw_segs)], axis=1)
    half = len(d_segs) // 2
    d_h1_a, big_parts['w_in'] = mm(d_segs[:half], groups[:half], name="proj_bwd_x0", tb=True,
                                   scatter=[col_blocks(gw_in)], **MM_SPLIT_CAPS)
    d_h1_b, small_all = mm(d_segs[half:], groups[half:], name="proj_bwd_x1", tb=True,
                           gather=[_pack(small_list)], **MM_SPLIT_CAPS)
    grad_x, d_sc1, d_sh1 = stage_bwd("mod1_bwd", f_modulate, (1, nt), mod1_ins, [out(d, d)], [(d_h1_a, d_h1_b)],
                                     add_to={0: dx_a})

    g_small = dict(zip(small_names, _unpack(sum_parts("sum_small_grads", small_all), small_list)))
    d_ada_me = jnp.concatenate([d_sh1, d_sc1, d_gt1, d_sh2, d_sc2, d_gt2], axis=1)
    (d_ada_all,) = all_gather("gather_d_ada", [d_ada_me.reshape(-1, LANES)])
    g_small['b_ada'] = sum_parts("sum_d_ada", d_ada_all)
    small_names = ['b_ada'] + small_names
    d_ada_cols = lax.dynamic_slice(d_ada_all.reshape(N_DEV, 6 * d), (0, me * ada_w), (N_DEV, ada_w))
    d_ada_pad = jnp.pad(d_ada_cols, ((0, LANES - N_DEV), (0, 0)))
    gw_ada = mm(c_pad, d_ada_pad, name="ada_bwd_w", ta=True, a_act="silu")

    res = {}
    big_parts['w_ada'] = gw_ada[None]
    for n in ['w_ada'] + big:
        res[n] = adamw("adamw_" + n, W[n], big_parts[n], M[n], V[n])
    for n in small_sh:
        w_ = W[n].shape[1]
        g_small[n] = lax.dynamic_slice(g_small[n], (0, me * w_), (W[n].shape[0], w_))
    for n in small_names:
        g_small[n] = g_small[n].reshape(W[n].shape)
    pk = lambda dct: _pack([dct[n] for n in small_names])
    s_g, s_d, s_m, s_v = adamw("adamw_small", pk(W), pk(g_small)[None], pk(M), pk(V))
    like = [W[n] for n in small_names]
    for n, g_, d_, m_, v_ in zip(small_names, _unpack(s_g, like), _unpack(s_d, like), _unpack(s_m, like), _unpack(s_v, like)):
        res[n] = (g_, d_, m_, v_)

    loss = lax.psum(jnp.sum(loss_rows), ("x", "y", "c"))
    outs = [loss, grad_x[None]]
    for j in range(4):
        outs += [res[n][j].reshape(loc[n].shape) for n in names]
    return tuple(outs)
```

```python
import functools
import math

import jax
import jax.numpy as jnp
from jax import lax
from jax.experimental import pallas as pl
from jax.experimental.pallas import tpu as pltpu

F32 = jnp.float32
BF16 = jnp.bfloat16
MXU_DTYPE = BF16
WIRE_DTYPE = BF16
DN_DTYPE = BF16
HI = lax.Precision.HIGHEST
MESH = pl.DeviceIdType.MESH

N_DEV = 8
LANES = 128
SUBLANES = 8
VMEM_LIMIT = 56 * 1024 * 1024
MM_TM_CAP, MM_TN_CAP, MM_TK_CAP = 1536, 1536, 2048
MM_SPLIT_CAPS = dict(tm_cap=1024, tn_cap=1024, tk_cap=1024)

RG_C = 8.0
DN_CHUNK = 64
DN_HEAD_GROUP = 16
DN_INTER_CHUNKS = 4
DN_INTRA_CHUNKS = 2
LN_EPS = 1e-5
RMS_EPS = 1e-6
L2_EPS = 1e-6
DEPTH = 1
DEEPNORM_ALPHA = (2 * DEPTH) ** 0.25
ADAM_LR = 0.001
ADAM_B1 = 0.9
ADAM_B2 = 0.999
ADAM_EPS = 1e-08
ADAM_WD = 0.01
ADAM_STEP = 10


def _tile(n, cap, unit=LANES):
    best = None
    for t in range(unit, min(n, cap) + 1, unit):
        if n % t == 0:
            best = t
    return best if best is not None else n


def _round_up(n, m):
    return (n + m - 1) // m * m


_HBM = pl.BlockSpec(memory_space=pl.ANY)


def _exchange_sems(n):
    return [pltpu.SemaphoreType.DMA((n, N_DEV - 1)), pltpu.SemaphoreType.DMA((n, N_DEV - 1)),
            pltpu.SemaphoreType.DMA((n,))]


def _exchange_out_shape(arrs, scatter):
    return [jax.ShapeDtypeStruct(a.shape if scatter else (N_DEV,) + a.shape, a.dtype) for a in arrs]


def _exchange_copies(in_refs, out_refs, sems, scatter, phase):
    send_sems, recv_sems, local_sems = sems
    x, y, c = lax.axis_index("x"), lax.axis_index("y"), lax.axis_index("c")
    me = 4 * x + 2 * y + c
    peers = [(x ^ ((k >> 2) & 1), y ^ ((k >> 1) & 1), c ^ (k & 1)) for k in range(N_DEV)]
    row = [4 * p[0] + 2 * p[1] + p[2] for p in peers]
    n = len(in_refs)

    def local(i):
        return pltpu.make_async_copy(in_refs[i].at[me] if scatter else in_refs[i], out_refs[i].at[me], local_sems.at[i])

    def remote(i, k, src, dst_row, to):
        return pltpu.make_async_remote_copy(src_ref=src, dst_ref=out_refs[i].at[dst_row],
                                            send_sem=send_sems.at[i, k - 1], recv_sem=recv_sems.at[i, k - 1],
                                            device_id=to, device_id_type=MESH)

    if scatter:
        sends = [(i, k, in_refs[i].at[row[k]], me, peers[k]) for k in range(1, N_DEV) for i in range(n)]
        passed = []
    else:
        sends = [(i, k, in_refs[i], me, peers[k]) for k in (1, 2, 4, 6) for i in range(n)]
        passed = [(i, k + 1, out_refs[i].at[row[k]], row[k], peers[1]) for k in (2, 4, 6) for i in range(n)]
    arrival = lambda i, k: remote(i, k, in_refs[i].at[me] if scatter else in_refs[i], row[k], peers[k])

    if phase == "start":
        for i in range(n):
            local(i).start()
        for cp in sends:
            remote(*cp).start()
    else:
        for cp in passed:
            arrival(cp[0], cp[1] - 1).wait_recv()
            remote(*cp).start()
        waited = {(cp[0], cp[1] - 1) for cp in passed}
        for k in range(1, N_DEV):
            for i in range(n):
                if (i, k) not in waited:
                    arrival(i, k).wait_recv()
        for cp in sends + passed:
            remote(*cp).wait_send()
        for i in range(n):
            local(i).wait()


def mm(a, b, *, name, ta=False, tb=False, a_act=None, bias=None, out_dtype=F32,
       tm_cap=MM_TM_CAP, tn_cap=MM_TN_CAP, tk_cap=MM_TK_CAP, gather=(), scatter=()):
    a_segs = list(a) if isinstance(a, (list, tuple)) else [a]
    b_segs = list(b) if isinstance(b, (list, tuple)) else [b]
    ns = len(a_segs)
    assert ns == len(b_segs) and (ns == 1 or a_act is None)
    m = a_segs[0].shape[1] if ta else a_segs[0].shape[0]
    n = b_segs[0].shape[0] if tb else b_segs[0].shape[1]
    ks = [x.shape[0] if ta else x.shape[1] for x in a_segs]
    assert ks == [y.shape[1] if tb else y.shape[0] for y in b_segs], (ks, ta, tb)
    tm, tn = _tile(m, tm_cap), _tile(n, tn_cap)
    tks = [_tile(k_, tk_cap) for k_ in ks]
    cnt = [k_ // t_ for k_, t_ in zip(ks, tks)]
    lo = [sum(cnt[:s]) for s in range(ns)]
    nk = sum(cnt)
    grid = (m // tm, n // tn, nk)
    dims = (((0 if ta else 1,), (1 if tb else 0,)), ((), ()))
    xch = list(gather) + list(scatter)
    nx, ng = len(xch), len(gather)
    n_main = 2 * ns + (bias is not None)

    def body(*refs):
        a_refs, b_refs = refs[:ns], refs[ns:2 * ns]
        bias_ref = refs[2 * ns] if bias is not None else None
        x_in, o_ref, x_out = refs[n_main:n_main + nx], refs[n_main + nx], refs[n_main + nx + 1:n_main + 2 * nx + 1]
        rest = refs[n_main + 2 * nx + 1:]
        acc_ref = rest[0] if nk > 1 else None
        sems = rest[1 if nk > 1 else 0:]
        groups = []
        if ng:
            groups.append((x_in[:ng], x_out[:ng], sems[:3], False))
        if nx > ng:
            groups.append((x_in[ng:], x_out[ng:], sems[-3:], True))
        if nx:
            step = (pl.program_id(0) * grid[1] + pl.program_id(1)) * grid[2] + pl.program_id(2)

            @pl.when(step == 0)
            def _():
                for gi, go_, gs, sc in groups:
                    _exchange_copies(gi, go_, gs, sc, "start")
        kk = pl.program_id(2)

        def finish(r):
            if bias is not None:
                r = r + bias_ref[...]
            o_ref[...] = r.astype(o_ref.dtype)

        def segment(s):
            av = a_refs[s][...]
            if a_act == "silu":
                av = jax.nn.silu(av.astype(F32))
            prod = lax.dot_general(av.astype(MXU_DTYPE), b_refs[s][...].astype(MXU_DTYPE), dims,
                                   preferred_element_type=F32)
            if nk == 1:
                finish(prod)
                return
            opens, closes = lo[s] == 0, lo[s] + cnt[s] == nk
            if opens:
                @pl.when(kk == 0)
                def _():
                    acc_ref[...] = prod
            inner = [kk > 0] * opens + [kk < nk - 1] * closes
            if inner:
                @pl.when(functools.reduce(lambda p, q: p & q, inner))
                def _():
                    acc_ref[...] += prod
            else:
                acc_ref[...] += prod
            if closes:
                @pl.when(kk == nk - 1)
                def _():
                    finish(acc_ref[...] + prod)

        for s in range(ns):
            if ns == 1:
                segment(s)
            else:
                pl.when((kk >= lo[s]) & (kk < lo[s] + cnt[s]))(functools.partial(segment, s))

        if nx:
            @pl.when(step == grid[0] * grid[1] * grid[2] - 1)
            def _():
                for gi, go_, gs, sc in groups:
                    _exchange_copies(gi, go_, gs, sc, "wait")

    def seg_index(s):
        return lambda q: jnp.clip(q - lo[s], 0, cnt[s] - 1) if ns > 1 else q

    a_specs, b_specs = [], []
    for s in range(ns):
        qi, tk = seg_index(s), tks[s]
        a_specs.append(pl.BlockSpec((tk, tm), (lambda qi: lambda i, j, q: (qi(q), i))(qi)) if ta
                       else pl.BlockSpec((tm, tk), (lambda qi: lambda i, j, q: (i, qi(q)))(qi)))
        b_specs.append(pl.BlockSpec((tn, tk), (lambda qi: lambda i, j, q: (j, qi(q)))(qi)) if tb
                       else pl.BlockSpec((tk, tn), (lambda qi: lambda i, j, q: (qi(q), j))(qi)))
    in_specs, args = a_specs + b_specs, a_segs + b_segs
    if bias is not None:
        in_specs.append(pl.BlockSpec((1, tn), lambda i, j, q: (0, j)))
        args.append(bias)
    o_spec, o_shape = pl.BlockSpec((tm, tn), lambda i, j, q: (i, j)), jax.ShapeDtypeStruct((m, n), out_dtype)
    acc = [pltpu.VMEM((tm, tn), F32)] if nk > 1 else []
    if not nx:
        return pl.pallas_call(
            body, name=name, grid=grid, in_specs=in_specs, out_specs=o_spec, out_shape=o_shape, scratch_shapes=acc,
            compiler_params=pltpu.CompilerParams(dimension_semantics=("parallel", "parallel", "arbitrary"),
                                                 vmem_limit_bytes=VMEM_LIMIT),
        )(*args)
    return pl.pallas_call(
        body, name=name, grid=grid, in_specs=in_specs + [_HBM] * nx, out_specs=[o_spec] + [_HBM] * nx,
        out_shape=[o_shape] + _exchange_out_shape(list(gather), False) + _exchange_out_shape(list(scatter), True),
        scratch_shapes=acc + (_exchange_sems(ng) if ng else []) + (_exchange_sems(nx - ng) if nx > ng else []),
        compiler_params=pltpu.CompilerParams(dimension_semantics=("arbitrary", "arbitrary", "arbitrary"),
                                             vmem_limit_bytes=VMEM_LIMIT, has_side_effects=True),
    )(*args, *xch)


class In:
    def __init__(self, arr, block, imap, acc=False, grad=True, parts=None, gshape=None, gimap=None):
        self.arr, self.block, self.imap, self.acc, self.grad, self.parts = arr, block, imap, acc, grad, parts
        self.gshape = arr.shape if gshape is None else gshape
        self.gimap = imap if gimap is None else gimap


class Out:
    def __init__(self, shape, block, imap, dtype=F32):
        self.shape, self.block, self.imap, self.dtype = shape, block, imap, dtype


def _load(in_refs, ins):
    vals = []
    for r, i in zip(in_refs, ins):
        if i.parts is None:
            vals.append(r[...])
        else:
            vals.extend(r[p] for p in i.parts)
    return vals


def _stage_params():
    return pltpu.CompilerParams(dimension_semantics=("parallel", "arbitrary"), vmem_limit_bytes=VMEM_LIMIT)


def stage_fwd(name, f, grid, ins, outs, carries=(), transposed=()):
    n_in, n_out, n_c, n_t = len(ins), len(outs), len(carries), len(transposed)

    def body(*refs):
        in_refs, out_refs = refs[:n_in], refs[n_in:n_in + n_out]
        hist_refs = refs[n_in + n_out:n_in + n_out + n_c]
        t_refs = refs[n_in + n_out + n_c:n_in + n_out + n_c + n_t]
        c_refs = refs[n_in + n_out + n_c + n_t:]
        if n_c:
            @pl.when(pl.program_id(1) == 0)
            def _():
                for c in c_refs:
                    c[...] = jnp.zeros_like(c)
        cin = [c[...] for c in c_refs]
        for h, c in zip(hist_refs, cin):
            h[...] = c
        o, cout = f(*_load(in_refs, ins), *cin)
        for r, v in zip(out_refs, o):
            r[...] = v.astype(r.dtype)
        for r, k in zip(t_refs, transposed):
            r[...] = o[k].T.astype(r.dtype)
        for c, v in zip(c_refs, cout):
            c[...] = v

    hist_spec = lambda c: pl.BlockSpec((None, None) + tuple(c), lambda o, s: (o, s) + (0,) * len(c))
    flip = lambda o_: pl.BlockSpec(o_.block[::-1], (lambda im: lambda o, s: im(o, s)[::-1])(o_.imap))
    res = pl.pallas_call(
        body, name=name, grid=grid,
        in_specs=[pl.BlockSpec(i.block, i.imap) for i in ins],
        out_specs=[pl.BlockSpec(o.block, o.imap) for o in outs] + [hist_spec(c) for c in carries]
        + [flip(outs[k]) for k in transposed],
        out_shape=[jax.ShapeDtypeStruct(o.shape, o.dtype) for o in outs]
        + [jax.ShapeDtypeStruct(tuple(grid) + tuple(c), F32) for c in carries]
        + [jax.ShapeDtypeStruct(outs[k].shape[::-1], MXU_DTYPE) for k in transposed],
        scratch_shapes=[pltpu.VMEM(tuple(c), F32) for c in carries],
        compiler_params=_stage_params(),
    )(*[i.arr for i in ins])
    res = list(res)
    if transposed:
        return res[:n_out], res[n_out:n_out + n_c], res[n_out + n_c:]
    return res[:n_out], res[n_out:]


def stage_bwd(name, f, grid, ins, outs, cots, carries=(), hists=(), add_to=None, gdtypes=None):
    n_in, n_out, n_c = len(ins), len(outs), len(carries)
    ns = grid[1]
    add_to = add_to or {}
    gdtypes = gdtypes or {}
    add_idx = sorted(add_to)
    g_idx = [k for k, i in enumerate(ins) if i.grad]
    cots = [c if isinstance(c, (tuple, list)) else (c,) for c in cots]
    n_cot = [len(c) for c in cots]
    rev = lambda imap: (lambda o, s: imap(o, ns - 1 - s))

    def body(*refs):
        p = 0
        in_refs = refs[p:p + n_in]; p += n_in
        cot_refs = []
        for cnt in n_cot:
            cot_refs.append(refs[p:p + cnt]); p += cnt
        hist_refs = refs[p:p + n_c]; p += n_c
        add_refs = refs[p:p + len(add_idx)]; p += len(add_idx)
        g_refs = refs[p:p + len(g_idx)]; p += len(g_idx)
        dc_refs = refs[p:]
        first = pl.program_id(1) == 0
        if n_c:
            @pl.when(first)
            def _():
                for c in dc_refs:
                    c[...] = jnp.zeros_like(c)
        vals = _load(in_refs, ins)
        cin = [h[...] for h in hist_refs]
        (o, cout), vjp = jax.vjp(lambda *a: f(*a), *vals, *cin)
        cot_o = []
        for crs, v in zip(cot_refs, o):
            c = crs[0][...].astype(v.dtype)
            for extra in crs[1:]:
                c = c + extra[...].astype(v.dtype)
            cot_o.append(c)
        cot_c = tuple(c[...] for c in dc_refs)
        grads = vjp((tuple(cot_o), cot_c))
        pos, per_in = 0, []
        for i in ins:
            cnt = 1 if i.parts is None else len(i.parts)
            per_in.append(grads[pos:pos + cnt])
            pos += cnt
        dcin = grads[pos:]
        for gr, k in zip(g_refs, g_idx):
            i, gs = ins[k], per_in[k]
            if i.acc:
                @pl.when(first)
                def _(gr=gr):
                    gr[...] = jnp.zeros_like(gr)
                if i.parts is None:
                    gr[...] += gs[0].astype(gr.dtype)
                else:
                    for pt, g in zip(i.parts, gs):
                        gr[pt] += g.astype(gr.dtype)
            else:
                g = gs[0]
                if k in add_to:
                    g = g + add_refs[add_idx.index(k)][...].astype(g.dtype)
                gr[...] = g.astype(gr.dtype)
        for c, v in zip(dc_refs, dcin):
            c[...] = v

    in_specs = [pl.BlockSpec(i.block, rev(i.imap)) for i in ins]
    for o_, cnt in zip(outs, n_cot):
        in_specs += [pl.BlockSpec(o_.block, rev(o_.imap))] * cnt
    in_specs += [pl.BlockSpec((None, None) + tuple(c), (lambda c: (lambda o, s: (o, ns - 1 - s) + (0,) * len(c)))(c))
                 for c in carries]
    in_specs += [pl.BlockSpec(ins[k].block, rev(ins[k].gimap)) for k in add_idx]
    out_specs, out_shape = [], []
    for k in g_idx:
        i = ins[k]
        if i.acc:
            out_specs.append(pl.BlockSpec(i.block, (lambda im: (lambda o, s: im(o, 0)))(i.imap)))
        else:
            out_specs.append(pl.BlockSpec(i.block, rev(i.gimap)))
        out_shape.append(jax.ShapeDtypeStruct(i.gshape, gdtypes.get(k, F32)))
    res = pl.pallas_call(
        body, name=name, grid=grid, in_specs=in_specs, out_specs=out_specs, out_shape=out_shape,
        scratch_shapes=[pltpu.VMEM(tuple(c), F32) for c in carries],
        compiler_params=_stage_params(),
    )(*[i.arr for i in ins], *[a for c in cots for a in c], *hists, *[add_to[k] for k in add_idx])
    return list(res)


def _iota_rows(shape):
    return lax.broadcasted_iota(jnp.int32, shape, 0)


@functools.partial(jax.custom_vjp, nondiff_argnums=(1,))
def _roll_rows(x, s):
    return pltpu.roll(x, s % x.shape[0], 0)


def _roll_rows_fwd(x, s):
    return _roll_rows(x, s), None


def _roll_rows_bwd(s, _, g):
    return (_roll_rows(g, -s),)


_roll_rows.defvjp(_roll_rows_fwd, _roll_rows_bwd)


@jax.custom_vjp
def _drop_head(xx):
    return xx[SUBLANES:]


def _drop_head_fwd(xx):
    return xx[SUBLANES:], None


def _drop_head_bwd(_, g):
    return (jnp.concatenate([jnp.zeros((SUBLANES, g.shape[1]), g.dtype), g], axis=0),)


_drop_head.defvjp(_drop_head_fwd, _drop_head_bwd)


@jax.custom_vjp
def _last_rows(x):
    return x[x.shape[0] - SUBLANES:]


def _last_rows_fwd(x):
    return x[x.shape[0] - SUBLANES:], x.shape[0]


def _last_rows_bwd(n, g):
    return (jnp.concatenate([jnp.zeros((n - SUBLANES, g.shape[1]), g.dtype), g], axis=0),)


_last_rows.defvjp(_last_rows_fwd, _last_rows_bwd)


def _last_row(x):
    n = x.shape[0]
    return jnp.sum(jnp.where(_iota_rows(x.shape) == n - 1, x, 0.0), axis=0, keepdims=True)


def _scan_steps(n):
    s = 1
    while s < n:
        yield s
        s *= 2


def _block_scan_log(a, u, h0):
    n = a.shape[0]
    row = _iota_rows(a.shape)
    for s in _scan_steps(n):
        keep = row >= s
        a_s = jnp.where(keep, pltpu.roll(a, s, 0), 1.0)
        u_s = jnp.where(keep, pltpu.roll(u, s, 0), 0.0)
        u = u + a * u_s
        a = a * a_s
    return u + a * h0


def _block_scan_impl(a, u, edge, reverse=False):
    n, c = a.shape
    nt = n // SUBLANES
    a, u = a.reshape(nt, SUBLANES, c), u.reshape(nt, SUBLANES, c)
    row = lax.broadcasted_iota(jnp.int32, a.shape, 1)
    for s in _scan_steps(SUBLANES):
        keep, shift = (row < SUBLANES - s, SUBLANES - s) if reverse else (row >= s, s)
        a_s = jnp.where(keep, pltpu.roll(a, shift, 1), 1.0)
        u_s = jnp.where(keep, pltpu.roll(u, shift, 1), 0.0)
        u = u + a * u_s
        a = a * a_s
    carry = jnp.broadcast_to(edge, (SUBLANES, c))
    tiles = [None] * nt
    at = 0 if reverse else SUBLANES - 1
    for i in (reversed(range(nt)) if reverse else range(nt)):
        tiles[i] = u[i] + a[i] * carry
        carry = jnp.broadcast_to(tiles[i][at:at + 1, :], (SUBLANES, c))
    return jnp.stack(tiles).reshape(n, c)


@jax.custom_vjp
def _block_scan(a, u, h0):
    return _block_scan_log(a, u, h0)


def _block_scan_fwd(a, u, h0):
    h = _block_scan_impl(a, u, h0)
    return h, (a, h, h0)


def _block_scan_bwd(res, dh):
    a, h, h0 = res
    n = a.shape[0]
    row = _iota_rows(a.shape)
    lam = _block_scan_impl(pltpu.roll(a, n - 1, 0), dh, jnp.zeros_like(h0), reverse=True)
    h_prev = jnp.where(row >= 1, pltpu.roll(h, 1, 0), jnp.broadcast_to(h0, h.shape))
    d_h0 = jnp.sum(jnp.where(row == 0, a * lam, 0.0), axis=0, keepdims=True)
    return lam * h_prev, lam, d_h0


_block_scan.defvjp(_block_scan_fwd, _block_scan_bwd)


def _dot_hi(a, b, dims=(((1,), (0,)), ((), ()))):
    return lax.dot_general(a, b, dims, precision=HI, preferred_element_type=F32)


_NN, _NT, _TN = "nn", "nt", "tn"
_CONTRACT = {_NN: (1, 0), _NT: (1, 1), _TN: (0, 0)}


def _raw_dot(a, b, kind):
    ca, cb = _CONTRACT[kind]
    lead = a.ndim - 2
    dims = (((ca + lead,), (cb + lead,)), (tuple(range(lead)), tuple(range(lead))))
    return lax.dot_general(a.astype(DN_DTYPE), b.astype(DN_DTYPE), dims, preferred_element_type=F32)


@jax.custom_vjp
def _nn(a, b):
    return _raw_dot(a, b, _NN)


_nn.defvjp(lambda a, b: (_raw_dot(a, b, _NN), (a, b)),
           lambda r, g: (_raw_dot(g, r[1], _NT), _raw_dot(r[0], g, _TN)))


@jax.custom_vjp
def _nt(a, b):
    return _raw_dot(a, b, _NT)


_nt.defvjp(lambda a, b: (_raw_dot(a, b, _NT), (a, b)),
           lambda r, g: (_raw_dot(g, r[1], _NN), _raw_dot(g, r[0], _TN)))


@jax.custom_vjp
def _tn(a, b):
    return _raw_dot(a, b, _TN)


_tn.defvjp(lambda a, b: (_raw_dot(a, b, _TN), (a, b)),
           lambda r, g: (_raw_dot(r[1], g, _NT), _raw_dot(r[0], g, _NN)))


def _neumann_inverse(a):
    n = a.shape[-1]
    eye = (lax.broadcasted_iota(jnp.int32, (n, n), 0) == lax.broadcasted_iota(jnp.int32, (n, n), 1)).astype(F32)
    p = _raw_dot(a, a, _NN)
    e = p
    for _ in range(int(math.log2(n)) - 2):
        p = _raw_dot(p, p, _NN)
        e = e + p + _raw_dot(e, p, _NN)
    return eye - a + e - _raw_dot(a, e, _NN)


@jax.custom_vjp
def _unit_lower_inverse(a):
    return _neumann_inverse(a)


def _unit_lower_inverse_fwd(a):
    x = _neumann_inverse(a)
    return x, x


def _unit_lower_inverse_bwd(x, g):
    return (-_raw_dot(_raw_dot(x, g, _TN), x, _NT),)


_unit_lower_inverse.defvjp(_unit_lower_inverse_fwd, _unit_lower_inverse_bwd)


def _softplus(x):
    return jnp.maximum(x, 0.0) + jnp.log1p(jnp.exp(-jnp.abs(x)))


def _neg_expm1(x):
    series = -x * (1.0 + x * (0.5 + x * (1.0 / 6.0 + x * (1.0 / 24.0 + x * (1.0 / 120.0)))))
    return jnp.where(x > -0.03, series, 1.0 - jnp.exp(x))


def f_modulate(x, sc, sh):
    return (x * (1.0 + sc) + sh,), ()


def _deepnorm(x, y, gt, g, b):
    v = DEEPNORM_ALPHA * x + (1.0 + gt) * y
    mu = jnp.mean(v, axis=-1, keepdims=True)
    vc = v - mu
    var = jnp.mean(vc * vc, axis=-1, keepdims=True)
    return vc * lax.rsqrt(var + LN_EPS) * g + b


def f_deepnorm_mod(x, y, gt, g, b, sc, sh):
    x1 = _deepnorm(x, y, gt, g, b)
    return (x1, x1 * (1.0 + sc) + sh), ()


def f_deepnorm_loss(x, y, gt, g, b, target):
    err = _deepnorm(x, y, gt, g, b) - target
    return (0.5 * jnp.mean(err * err, axis=-1, keepdims=True),), ()


def _causal_conv(x, prev, ws):
    xx = jnp.concatenate([prev, x], axis=0)
    k = len(ws)
    y = ws[k - 1] * x
    for j in range(k - 1):
        y = y + ws[j] * _drop_head(_roll_rows(xx, k - 1 - j))
    return y


def f_rg_conv(x, w0, w1, w2, w3, b, prev):
    return (_causal_conv(x, prev, (w0, w1, w2, w3)) + b,), (_last_rows(x),)


def f_dn_conv(x, w0, w1, w2, w3, prev):
    return (jax.nn.silu(_causal_conv(x, prev, (w0, w1, w2, w3))),), (_last_rows(x),)


def f_ffn_act(gp, up, w0, w1, w2, b, prev):
    return (jax.nn.gelu(_causal_conv(gp, prev, (w0, w1, w2)) + b) * up,), (_last_rows(gp),)


def f_rglru(xc, pre_r, pre_i, gr, b_a, b_x, lam, h0):
    gate_r = jax.nn.sigmoid(pre_r + b_a)
    gate_i = jax.nn.sigmoid(pre_i + b_x)
    log_a = -RG_C * gate_r * _softplus(-lam)
    a = jnp.exp(log_a)
    mult = jnp.sqrt(_neg_expm1(2.0 * log_a))
    h = _block_scan(a, mult * gate_i * xc, h0)
    return (h * jax.nn.gelu(gr),), (_last_row(h),)


def f_dn_conv_norm(scale, x, w0, w1, w2, w3, prev):
    y = jax.nn.silu(_causal_conv(x, prev, (w0, w1, w2, w3)))
    return (y * lax.rsqrt(jnp.sum(y * y, axis=-1, keepdims=True) + L2_EPS) * scale,), (_last_rows(x),)


def f_dn_gates(a_in, b_in, a_log, dt_bias):
    g = -jnp.exp(a_log) * _softplus(a_in + dt_bias)
    n = g.shape[0]
    shift = int(math.log2(DN_CHUNK))
    ri = lax.broadcasted_iota(jnp.int32, (n, n), 0)
    ci = lax.broadcasted_iota(jnp.int32, (n, n), 1)
    tri = ((lax.shift_right_logical(ri, shift) == lax.shift_right_logical(ci, shift)) & (ri >= ci)).astype(F32)
    return (_dot_hi(tri, g), jax.nn.sigmoid(b_in)), ()


def f_dn_out(o, z, nw):
    r = lax.rsqrt(jnp.mean(o * o, axis=-1, keepdims=True) + RMS_EPS)
    return (o * r * nw * jax.nn.silu(z),), ()


def f_merge(ga, gb, ya, yb):
    return (jax.nn.sigmoid(ga) * ya + jax.nn.sigmoid(gb) * yb,), ()


def _delta_intra(q, k, v, g_i, g_j, beta):
    c = q.shape[-2]
    ri = lax.broadcasted_iota(jnp.int32, (c, c), 0)
    ci = lax.broadcasted_iota(jnp.int32, (c, c), 1)
    decay = jnp.exp(jnp.where(ri >= ci, g_i - g_j, -jnp.inf))
    g_last = jnp.sum(jnp.where(_iota_rows((c, 1)) == c - 1, g_i, 0.0), axis=-2, keepdims=True)
    exp_g = jnp.exp(g_i)
    kb = k * beta
    t_inv = _unit_lower_inverse(jnp.where(ri > ci, _nt(kb, k) * decay, 0.0))
    u = _nn(t_inv, v * beta)
    w = _nn(t_inv, kb * exp_g)
    return u, w, _nt(q, k) * decay, q * exp_g, k * jnp.exp(g_last - g_i)


def _delta_inter(u, w, qk, q_dec, k_dec, g_last, state):
    v_new = u - _nn(w, state)
    o = _nn(q_dec, state) + _nn(qk, v_new)
    return o, jnp.exp(g_last) * state + _tn(k_dec, v_new)


def _delta_params(sem):
    return pltpu.CompilerParams(dimension_semantics=(sem,), vmem_limit_bytes=VMEM_LIMIT)


def _head_groups(n_vh):
    hb = min(DN_HEAD_GROUP, n_vh)
    return [range(h0, h0 + hb) for h0 in range(0, n_vh, hb)]


def _stack(hs, f):
    return jnp.stack([f(h) for h in hs])


def _rows(ci):
    return slice(ci * DN_CHUNK, (ci + 1) * DN_CHUNK)


def _intra_pairs(n_vh):
    return [(ci, h) for ci in range(DN_INTRA_CHUNKS) for h in range(n_vh)]


def _pair_stack(pairs, ref, width=LANES, head_of=lambda h: h):
    return jnp.stack([ref[_rows(ci), head_of(h) * LANES:head_of(h) * LANES + width] for ci, h in pairs])


def _intra_operands(pairs, rep, q_ref, k_ref, v_ref, g_ref, gt_ref, b_ref):
    qk_head = lambda h: h // rep
    return (_pair_stack(pairs, q_ref, head_of=qk_head), _pair_stack(pairs, k_ref, head_of=qk_head),
            _pair_stack(pairs, v_ref), jnp.stack([g_ref[_rows(ci), h:h + 1] for ci, h in pairs]),
            jnp.stack([gt_ref[ci, h:h + 1, :] for ci, h in pairs]),
            jnp.stack([b_ref[_rows(ci), h:h + 1] for ci, h in pairs]))


def _intra_spec(width, col=0):
    return pl.BlockSpec((DN_INTRA_CHUNKS * DN_CHUNK, width), lambda s: (s, col))


def _inter_operands(hs, ci, u_ref, w_ref, qk_ref, qd_ref, kd_ref, g_ref):
    f32 = lambda ref, width=LANES: _stack(hs, lambda h: ref[_rows(ci), h * LANES:h * LANES + width].astype(F32))
    last = (ci + 1) * DN_CHUNK - 1
    return (f32(u_ref), f32(w_ref), f32(qk_ref, DN_CHUNK), f32(qd_ref), f32(kd_ref),
            _stack(hs, lambda h: g_ref[last:last + 1, h:h + 1]))


def _inter_spec(width, steps, reverse=False):
    rows = DN_INTER_CHUNKS * DN_CHUNK
    return pl.BlockSpec((rows, width), (lambda s: (steps - 1 - s, 0)) if reverse else (lambda s: (s, 0)))


def delta_intra_fwd(qn, kn, qkv, v_blk, big_g, big_gt, beta, n_vh):
    t, qk_w = qn.shape
    vdim = n_vh * LANES
    rep = vdim // qk_w
    nc = t // DN_CHUNK

    pairs = _intra_pairs(n_vh)

    def body(q_ref, k_ref, v_ref, g_ref, gt_ref, b_ref, u_ref, w_ref, qk_ref, qd_ref, kd_ref):
        u, w, qk, qd, kd = _delta_intra(*_intra_operands(pairs, rep, q_ref, k_ref, v_ref, g_ref, gt_ref, b_ref))
        for i, (ci, h) in enumerate(pairs):
            at = (_rows(ci), slice(h * LANES, (h + 1) * LANES))
            u_ref[at] = u[i]
            w_ref[at] = w[i].astype(w_ref.dtype)
            qk_ref[at] = jnp.concatenate([qk[i], jnp.zeros_like(qk[i])], axis=1).astype(qk_ref.dtype)
            qd_ref[at] = qd[i].astype(qd_ref.dtype)
            kd_ref[at] = kd[i].astype(kd_ref.dtype)

    return pl.pallas_call(
        body, name="delta_intra_fwd", grid=(nc // DN_INTRA_CHUNKS,),
        in_specs=[_intra_spec(qk_w), _intra_spec(qk_w), _intra_spec(vdim, v_blk), _intra_spec(LANES),
                  pl.BlockSpec((DN_INTRA_CHUNKS, n_vh, DN_CHUNK), lambda s: (s, 0, 0)), _intra_spec(LANES)],
        out_specs=[_intra_spec(vdim)] * 5,
        out_shape=[jax.ShapeDtypeStruct((t, vdim), F32)] + [jax.ShapeDtypeStruct((t, vdim), DN_DTYPE)] * 4,
        compiler_params=_delta_params("parallel"),
    )(qn, kn, qkv, big_g, big_gt, beta)


def delta_inter_fwd(u, w, qk, q_dec, k_dec, big_g, n_vh):
    t, vdim = u.shape
    nc = t // DN_CHUNK

    cpb = DN_INTER_CHUNKS
    steps = nc // cpb

    def body(u_ref, w_ref, qk_ref, qd_ref, kd_ref, g_ref, o_ref, hist_ref, s_ref):
        @pl.when(pl.program_id(0) == 0)
        def _():
            s_ref[...] = jnp.zeros_like(s_ref)
        for ci in range(cpb):
            for hs in _head_groups(n_vh):
                grp = slice(hs[0], hs[-1] + 1)
                st = s_ref[grp]
                hist_ref[ci, grp] = st
                o, ns = _delta_inter(*_inter_operands(hs, ci, u_ref, w_ref, qk_ref, qd_ref, kd_ref, g_ref), st)
                for i, h in enumerate(hs):
                    o_ref[_rows(ci), h * LANES:(h + 1) * LANES] = o[i]
                s_ref[grp] = ns

    return pl.pallas_call(
        body, name="delta_inter_fwd", grid=(steps,),
        in_specs=[_inter_spec(vdim, steps)] * 5 + [_inter_spec(LANES, steps)],
        out_specs=[_inter_spec(vdim, steps), pl.BlockSpec((cpb, n_vh, LANES, LANES), lambda s: (s, 0, 0, 0))],
        out_shape=[jax.ShapeDtypeStruct((t, vdim), F32), jax.ShapeDtypeStruct((nc, n_vh, LANES, LANES), F32)],
        scratch_shapes=[pltpu.VMEM((n_vh, LANES, LANES), F32)],
        compiler_params=_delta_params("arbitrary"),
    )(u, w, qk, q_dec, k_dec, big_g)


def delta_inter_bwd(u, w, qk, q_dec, k_dec, big_g, hist, d_o, n_vh):
    t, vdim = u.shape
    nc = t // DN_CHUNK

    cpb = DN_INTER_CHUNKS
    steps = nc // cpb

    def body(u_ref, w_ref, qk_ref, qd_ref, kd_ref, g_ref, hist_ref, do_ref,
             du_ref, dw_ref, dqk_ref, dqd_ref, dkd_ref, dg_ref, ds_ref):
        @pl.when(pl.program_id(0) == 0)
        def _():
            ds_ref[...] = jnp.zeros_like(ds_ref)
        lane = lax.broadcasted_iota(jnp.int32, (1, LANES), 1)
        last = _iota_rows((DN_CHUNK, LANES)) == DN_CHUNK - 1
        for ci in reversed(range(cpb)):
            dgl_all = jnp.zeros((1, LANES), F32)
            for hs in _head_groups(n_vh):
                grp = slice(hs[0], hs[-1] + 1)
                prim = _inter_operands(hs, ci, u_ref, w_ref, qk_ref, qd_ref, kd_ref, g_ref) + (hist_ref[ci, grp],)
                _, vjp = jax.vjp(_delta_inter, *prim)
                cot_o = _stack(hs, lambda h: do_ref[_rows(ci), h * LANES:(h + 1) * LANES])
                du, dw, dqk, dqd, dkd, dgl, dst = vjp((cot_o, ds_ref[grp]))
                ds_ref[grp] = dst
                for i, h in enumerate(hs):
                    sl = slice(h * LANES, (h + 1) * LANES)
                    du_ref[_rows(ci), sl] = du[i].astype(du_ref.dtype)
                    dw_ref[_rows(ci), sl] = dw[i].astype(dw_ref.dtype)
                    dqk_ref[_rows(ci), sl] = jnp.concatenate([dqk[i], jnp.zeros_like(dqk[i])], axis=1)
                    dqd_ref[_rows(ci), sl] = dqd[i]
                    dkd_ref[_rows(ci), sl] = dkd[i]
                    dgl_all = dgl_all + dgl[i] * (lane == h).astype(F32)
            dg_ref[_rows(ci), :] = jnp.where(last, jnp.broadcast_to(dgl_all, (DN_CHUNK, LANES)), 0.0)

    rv = lambda w_: _inter_spec(w_, steps, reverse=True)
    return pl.pallas_call(
        body, name="delta_inter_bwd", grid=(steps,),
        in_specs=[rv(vdim)] * 5 + [rv(LANES), pl.BlockSpec((cpb, n_vh, LANES, LANES), lambda s: (steps - 1 - s, 0, 0, 0)),
                                   rv(vdim)],
        out_specs=[rv(vdim)] * 5 + [rv(LANES)],
        out_shape=[jax.ShapeDtypeStruct((t, vdim), DN_DTYPE)] * 2 + [jax.ShapeDtypeStruct((t, vdim), F32)] * 3
        + [jax.ShapeDtypeStruct((t, LANES), F32)],
        scratch_shapes=[pltpu.VMEM((n_vh, LANES, LANES), F32)],
        compiler_params=_delta_params("arbitrary"),
    )(u, w, qk, q_dec, k_dec, big_g, hist, d_o)


def delta_intra_bwd(qn, kn, qkv, v_blk, big_g, big_gt, beta, cots, n_vh):
    t, qk_w = qn.shape
    vdim = n_vh * LANES
    rep = vdim // qk_w
    nc = t // DN_CHUNK

    pairs = _intra_pairs(n_vh)

    def body(q_ref, k_ref, v_ref, g_ref, gt_ref, b_ref, du_ref, dw_ref, dqk_ref, dqd_ref, dkd_ref,
             dq_ref, dk_ref, dv_ref, dg_ref, dgt_ref, db_ref):
        lane = lax.broadcasted_iota(jnp.int32, (1, LANES), 1)
        _, vjp = jax.vjp(_delta_intra, *_intra_operands(pairs, rep, q_ref, k_ref, v_ref, g_ref, gt_ref, b_ref))
        dq, dk, dv, dgi, dgj, db = vjp((_pair_stack(pairs, du_ref).astype(F32), _pair_stack(pairs, dw_ref).astype(F32),
                                        _pair_stack(pairs, dqk_ref, DN_CHUNK), _pair_stack(pairs, dqd_ref),
                                        _pair_stack(pairs, dkd_ref)))
        dg_all, db_all = {}, {}
        dq_acc, dk_acc = None, None
        for i, (ci, h) in enumerate(pairs):
            j = h // rep
            dv_ref[_rows(ci), h * LANES:(h + 1) * LANES] = dv[i]
            dgt_ref[ci, h:h + 1, :] = dgj[i]
            onehot = (lane == h).astype(F32)
            dg_all[ci] = dgi[i] * onehot + dg_all.get(ci, 0.0)
            db_all[ci] = db[i] * onehot + db_all.get(ci, 0.0)
            dq_acc = dq[i] if h % rep == 0 else dq_acc + dq[i]
            dk_acc = dk[i] if h % rep == 0 else dk_acc + dk[i]
            if h % rep == rep - 1:
                dq_ref[_rows(ci), j * LANES:(j + 1) * LANES] = dq_acc
                dk_ref[_rows(ci), j * LANES:(j + 1) * LANES] = dk_acc
        for ci in dg_all:
            dg_ref[_rows(ci), :] = dg_all[ci]
            db_ref[_rows(ci), :] = db_all[ci]

    gt_spec = pl.BlockSpec((DN_INTRA_CHUNKS, n_vh, DN_CHUNK), lambda s: (s, 0, 0))
    return pl.pallas_call(
        body, name="delta_intra_bwd", grid=(nc // DN_INTRA_CHUNKS,),
        in_specs=[_intra_spec(qk_w), _intra_spec(qk_w), _intra_spec(vdim, v_blk), _intra_spec(LANES), gt_spec,
                  _intra_spec(LANES)] + [_intra_spec(vdim)] * 5,
        out_specs=[_intra_spec(qk_w), _intra_spec(qk_w), _intra_spec(vdim), _intra_spec(LANES), gt_spec,
                   _intra_spec(LANES)],
        out_shape=[jax.ShapeDtypeStruct((t, qk_w), F32), jax.ShapeDtypeStruct((t, qk_w), F32),
                   jax.ShapeDtypeStruct((t, vdim), F32), jax.ShapeDtypeStruct((t, LANES), F32),
                   jax.ShapeDtypeStruct((nc, n_vh, DN_CHUNK), F32), jax.ShapeDtypeStruct((t, LANES), F32)],
        compiler_params=_delta_params("parallel"),
    )(qn, kn, qkv, big_g, big_gt, beta, *cots)


def all_gather(name, arrs):
    n = len(arrs)

    def body(*refs):
        in_refs, out_refs, sems = refs[:n], refs[n:2 * n], refs[2 * n:]
        _exchange_copies(in_refs, out_refs, sems, False, "start")
        _exchange_copies(in_refs, out_refs, sems, False, "wait")

    res = pl.pallas_call(
        body, name=name,
        in_specs=[_HBM] * n, out_specs=[_HBM] * n,
        out_shape=_exchange_out_shape(arrs, False), scratch_shapes=_exchange_sems(n),
        compiler_params=pltpu.CompilerParams(has_side_effects=True),
    )(*arrs)
    return list(res)


def _adamw_math(w, g, m, v):
    m = ADAM_B1 * m + (1.0 - ADAM_B1) * g
    v = ADAM_B2 * v + (1.0 - ADAM_B2) * (g * g)
    m_hat = m / (1.0 - ADAM_B1 ** ADAM_STEP)
    v_hat = v / (1.0 - ADAM_B2 ** ADAM_STEP)
    delta = -ADAM_LR * (m_hat / (jnp.sqrt(v_hat) + ADAM_EPS) + ADAM_WD * w)
    return delta, m, v


def adamw(name, w, parts, m, v, rows_cap=128):
    r, c = w.shape
    np_ = parts.shape[0]
    tr = _tile(r, rows_cap, SUBLANES * (4 // parts.dtype.itemsize))

    def body(w_ref, p_ref, m_ref, v_ref, g_ref, d_ref, nm_ref, nv_ref):
        g = p_ref[0].astype(F32)
        for k in range(1, np_):
            g = g + p_ref[k].astype(F32)
        delta, nm, nv = _adamw_math(w_ref[...], g, m_ref[...], v_ref[...])
        g_ref[...] = g
        d_ref[...] = delta
        nm_ref[...] = nm
        nv_ref[...] = nv

    spec = pl.BlockSpec((tr, c), lambda i: (i, 0))
    return pl.pallas_call(
        body, name=name, grid=(r // tr,),
        in_specs=[spec, pl.BlockSpec((np_, tr, c), lambda i: (0, i, 0)), spec, spec],
        out_specs=[spec] * 4, out_shape=[jax.ShapeDtypeStruct((r, c), F32)] * 4,
        compiler_params=pltpu.CompilerParams(dimension_semantics=("parallel",), vmem_limit_bytes=VMEM_LIMIT),
    )(w, parts, m, v)


def sum_parts(name, parts, rows_cap=256):
    np_, r, c = parts.shape
    tr = _tile(r, rows_cap, SUBLANES)

    def body(p_ref, o_ref):
        g = p_ref[0].astype(F32)
        for k in range(1, np_):
            g = g + p_ref[k].astype(F32)
        o_ref[...] = g

    return pl.pallas_call(
        body, name=name, grid=(r // tr,),
        in_specs=[pl.BlockSpec((np_, tr, c), lambda i: (0, i, 0))],
        out_specs=pl.BlockSpec((tr, c), lambda i: (i, 0)),
        out_shape=jax.ShapeDtypeStruct((r, c), F32),
        compiler_params=pltpu.CompilerParams(dimension_semantics=("parallel",), vmem_limit_bytes=VMEM_LIMIT),
    )(parts)


def _pack(arrs):
    flat = jnp.concatenate([a.reshape(-1).astype(F32) for a in arrs])
    n = flat.shape[0]
    return jnp.pad(flat, (0, _round_up(n, LANES * SUBLANES) - n)).reshape(-1, LANES)


def _unpack(packed, like):
    flat, out, pos = packed.reshape(-1), [], 0
    for a in like:
        out.append(flat[pos:pos + a.size].reshape(a.shape))
        pos += a.size
    return out


def kernel(x, c, w_ada, b_ada, w_in, rg_conv_w, rg_conv_b, rg_w_a, rg_b_a, rg_w_x, rg_b_x, rg_lambda, dn_conv_w, dn_a_log, dn_dt_bias, dn_norm_w, w_proj_a, w_proj_b, w_out, ln1_g, ln1_b, ffn_w_gate, ffn_w_up, ffn_conv_w, ffn_conv_b, ffn_w_down, ln2_g, ln2_b, loss_target, m_w_ada, m_b_ada, m_w_in, m_rg_conv_w, m_rg_conv_b, m_rg_w_a, m_rg_b_a, m_rg_w_x, m_rg_b_x, m_rg_lambda, m_dn_conv_w, m_dn_a_log, m_dn_dt_bias, m_dn_norm_w, m_w_proj_a, m_w_proj_b, m_w_out, m_ln1_g, m_ln1_b, m_ffn_w_gate, m_ffn_w_up, m_ffn_conv_w, m_ffn_conv_b, m_ffn_w_down, m_ln2_g, m_ln2_b, v_w_ada, v_b_ada, v_w_in, v_rg_conv_w, v_rg_conv_b, v_rg_w_a, v_rg_b_a, v_rg_w_x, v_rg_b_x, v_rg_lambda, v_dn_conv_w, v_dn_a_log, v_dn_dt_bias, v_dn_norm_w, v_w_proj_a, v_w_proj_b, v_w_out, v_ln1_g, v_ln1_b, v_ffn_w_gate, v_ffn_w_up, v_ffn_conv_w, v_ffn_conv_b, v_ffn_w_down, v_ln2_g, v_ln2_b):
    names = ['w_ada', 'b_ada', 'w_in', 'rg_conv_w', 'rg_conv_b', 'rg_w_a', 'rg_b_a', 'rg_w_x', 'rg_b_x', 'rg_lambda',
             'dn_conv_w', 'dn_a_log', 'dn_dt_bias', 'dn_norm_w', 'w_proj_a', 'w_proj_b', 'w_out', 'ln1_g', 'ln1_b',
             'ffn_w_gate', 'ffn_w_up', 'ffn_conv_w', 'ffn_conv_b', 'ffn_w_down', 'ln2_g', 'ln2_b']
    loc = locals()
    W = {n: loc[n][0] for n in names}
    M = {n: loc['m_' + n][0] for n in names}
    V = {n: loc['v_' + n][0] for n in names}

    me = 4 * lax.axis_index("x") + 2 * lax.axis_index("y") + lax.axis_index("c")
    xs, tgt = x[0], loss_target[0]
    t, d = xs.shape
    d_rnn = W['rg_conv_b'].shape[0]
    n_blk = W['rg_w_a'].shape[0]
    n_vh = W['dn_a_log'].shape[0]
    assert W['dn_norm_w'].shape[0] == LANES
    vdim = n_vh * LANES
    d_ff = W['ffn_conv_b'].shape[0]
    d_in = W['w_in'].shape[1] * N_DEV
    qk = (d_in - 2 * d_rnn - 2 * vdim - 2 * n_vh - 2 * d) // 2
    assert vdim == 2 * qk and qk % LANES == 0 and n_vh <= LANES
    splits = (d_rnn, d_rnn, qk, qk, vdim, vdim, n_vh, n_vh, d, d)
    offs = [0]
    for s_ in splits:
        offs.append(offs[-1] + s_)

    tb = _tile(t, 256, SUBLANES)

    big = ['w_in', 'w_proj_a', 'w_proj_b', 'w_out', 'ffn_w_gate', 'ffn_w_up', 'ffn_w_down']
    small_sh = ['rg_conv_w', 'dn_conv_w', 'ffn_conv_w']
    first = all_gather("gather_first", [W['w_in'].astype(WIRE_DTYPE)] + [W[n] for n in small_sh] + [c])
    g_in, g_rcw, g_dcw, g_fcw, c_all = first
    cols = lambda g: jnp.transpose(g, (1, 0, 2)).reshape(g.shape[1], -1)
    rows = lambda g: g.reshape(-1, g.shape[2])
    w_in_f = cols(g_in)
    padl = lambda a: jnp.pad(a, ((0, 0), (0, LANES - a.shape[1])))
    groups = [w_in_f[:, offs[i]:offs[i + 1]] for i in range(10)]
    groups[6], groups[7] = padl(groups[6]), padl(groups[7])
    go = [0]
    for g_ in groups:
        go.append(go[-1] + g_.shape[1])
    n_pad = _round_up(go[-1], 512)
    wp = jnp.pad(jnp.concatenate(groups, axis=1), ((0, 0), (0, n_pad - go[-1])))
    o_xr, o_gr, o_q, o_k, o_v, o_z, o_a, o_b, o_ga, o_gb = go[:10]
    rcw, dcw, fcw = cols(g_rcw), cols(g_dcw), cols(g_fcw)
    eye_b = jnp.eye(n_blk, dtype=F32)
    bd = lambda w: (w[:, :, None, :] * eye_b[:, None, :, None]).reshape(d_rnn, d_rnn)
    w_bd = jnp.concatenate([bd(W['rg_w_a']), bd(W['rg_w_x'])], axis=1)
    row1 = lambda a: a.reshape(1, -1)
    padv = lambda a: jnp.pad(row1(a), ((0, 0), (0, LANES - a.shape[0])))
    nw_t = jnp.tile(row1(W['dn_norm_w']), (1, n_vh))

    c_pad =jnp.pad(c_all.reshape(N_DEV, d), ((0, LANES - N_DEV), (0, 0)))
    ada_w = W['w_ada'].shape[1]
    b_ada_me = lax.dynamic_slice(W['b_ada'], (me * ada_w,), (ada_w,)).reshape(1, ada_w)
    ada_sh = mm(c_pad, W['w_ada'], name="ada_fwd", a_act="silu", bias=b_ada_me)
    (ada_all,) = all_gather("gather_ada", [ada_sh[:N_DEV]])
    ada_me = lax.dynamic_slice(ada_all, (0, me, 0), (N_DEV, 1, ada_w)).reshape(6, 1, d)
    sh1, sc1, gt1, sh2, sc2, gt2 = [ada_me[i] for i in range(6)]

    nt = t // tb

    def act(a, bw, col0=0, width=None, grad=True, rows=tb):
        width = a.shape[1] if width is None else width
        assert col0 % bw == 0 and width % bw == 0
        c0 = col0 // bw
        return In(a, (rows, bw), lambda o, s: (s, c0 + o), grad=grad, gshape=(t, width), gimap=lambda o, s: (s, o))

    def prm(a, bw, parts=None):
        return In(a, (a.shape[0], bw), lambda o, s: (0, o), acc=True, parts=parts)

    def out(width, bw, rows=tb, dtype=F32):
        return Out((t, width), (rows, bw), lambda o, s: (s, o), dtype)

    tbh = _tile(t, 2048, SUBLANES)
    nth = t // tbh
    tbc = _tile(t, 2048, SUBLANES)
    ntc = t // tbc

    krows = lambda k_: [(slice(j, j + 1), slice(None)) for j in range(k_)]

    mod1_ins = [act(xs, d), prm(sc1, d), prm(sh1, d)]
    (h1,), _, (h1_t,) = stage_fwd("mod1_fwd", f_modulate, (1, nt), mod1_ins, [out(d, d, dtype=MXU_DTYPE)],
                                  transposed=[0])
    proj, g_pa, g_pb, g_out, g_fg, g_fu, g_fd = mm(h1, wp, name="proj_fwd",
                                                   gather=[W[n].astype(WIRE_DTYPE) for n in big[1:]])
    w_pa, w_pb, w_o, w_fd = rows(g_pa), rows(g_pb), rows(g_out), rows(g_fd)
    w_gate, w_up = cols(g_fg), cols(g_fu)
    w_gu = jnp.concatenate([w_gate, w_up], axis=1)

    cb_r = _tile(math.gcd(d_rnn, o_gr), 256)
    rgc_ins = [act(proj, cb_r, o_xr, d_rnn, rows=tbc), prm(rcw, cb_r, krows(4)), prm(row1(W['rg_conv_b']), cb_r)]
    rgc_grid, rgc_car, rgc_outs = (d_rnn // cb_r, ntc), [(SUBLANES, cb_r)], [out(d_rnn, cb_r, tbc)]
    (xc,), rgc_hist, (xc_t,) = stage_fwd("rg_conv_fwd", f_rg_conv, rgc_grid, rgc_ins, rgc_outs, rgc_car, transposed=[0])
    gates = mm(xc, w_bd, name="rg_gates_fwd")
    lru_ins = [act(xc, cb_r), act(gates, cb_r, 0, d_rnn), act(gates, cb_r, d_rnn, d_rnn), act(proj, cb_r, o_gr, d_rnn),
               prm(row1(W['rg_b_a']), cb_r), prm(row1(W['rg_b_x']), cb_r), prm(row1(W['rg_lambda']), cb_r)]
    lru_grid, lru_car = (d_rnn // cb_r, nt), [(1, cb_r)]
    (rec,), lru_hist, (rec_t,) = stage_fwd("rglru_fwd", f_rglru, lru_grid, lru_ins,
                                           [out(d_rnn, cb_r, dtype=MXU_DTYPE)], lru_car,
                                           transposed=[0])
    y_a = mm(rec, w_pa, name="proj_a_fwd")

    dnc = {}
    for nm, col0, width, w0, cb_, f_ in (("q", o_q, qk, 0, LANES, functools.partial(f_dn_conv_norm, LANES ** -0.5)),
                                         ("k", o_k, qk, qk, LANES, functools.partial(f_dn_conv_norm, 1.0)),
                                         ("v", o_v, vdim, 2 * qk, _tile(math.gcd(vdim, o_v), 256), f_dn_conv)):
        ins_ = [act(proj, cb_, col0, width, rows=tbc), prm(dcw[:, w0:w0 + width], cb_, krows(4))]
        grid_, outs_, car_ = (width // cb_, ntc), [out(width, cb_, tbc)], [(SUBLANES, cb_)]
        (y_,), hist_ = stage_fwd("dn_conv_%s_fwd" % nm, f_, grid_, ins_, outs_, car_)
        dnc[nm] = (y_, f_, ins_, grid_, outs_, car_, hist_)
    qn, kn, v_c = dnc["q"][0], dnc["k"][0], dnc["v"][0]
    gate_ins = [act(proj, LANES, o_a, LANES), act(proj, LANES, o_b, LANES),
                prm(padv(W['dn_a_log']), LANES), prm(padv(W['dn_dt_bias']), LANES)]
    gate_outs = [out(LANES, LANES), out(LANES, LANES)]
    (g_dn, beta_dn), _ = stage_fwd("dn_gates_fwd", f_dn_gates, (1, nt), gate_ins, gate_outs)
    n_ch = t // DN_CHUNK
    gt_dn = jnp.transpose(g_dn.reshape(n_ch, DN_CHUNK, LANES)[:, :, :n_vh], (0, 2, 1))
    dn_mid = delta_intra_fwd(qn, kn, v_c, 0, g_dn, gt_dn, beta_dn, n_vh)
    o_dn, dn_hist = delta_inter_fwd(*dn_mid, g_dn, n_vh)
    dno_ins = [act(o_dn, LANES, rows=tbh), act(proj, LANES, o_z, vdim, rows=tbh), prm(nw_t, LANES)]
    dno_grid, dno_outs = (n_vh, nth), [out(vdim, LANES, tbh, MXU_DTYPE)]
    (dn,), _, (dn_t,) = stage_fwd("dn_out_fwd", f_dn_out, dno_grid, dno_ins, dno_outs, transposed=[0])
    y_b = mm(dn, w_pb, name="proj_b_fwd")

    cb_m = _tile(math.gcd(math.gcd(d, o_ga), o_gb), 512)
    mrg_ins = [act(proj, cb_m, o_ga, d, rows=tbc), act(proj, cb_m, o_gb, d, rows=tbc), act(y_a, cb_m, rows=tbc),
               act(y_b, cb_m, rows=tbc)]
    mrg_grid, mrg_outs = (d // cb_m, ntc), [out(d, cb_m, tbc, MXU_DTYPE)]
    (merged,), _, (merged_t,) = stage_fwd("merge_fwd", f_merge, mrg_grid, mrg_ins, mrg_outs, transposed=[0])
    mix = mm(merged, w_o, name="w_out_fwd")
    ln1_ins = [act(xs, d), act(mix, d), prm(gt1, d), prm(row1(W['ln1_g']), d), prm(row1(W['ln1_b']), d),
               prm(sc2, d), prm(sh2, d)]
    ln1_outs = [out(d, d), out(d, d, dtype=MXU_DTYPE)]
    (x1, h2), _, (h2_t,) = stage_fwd("ln1_mod2_fwd", f_deepnorm_mod, (1, nt), ln1_ins, ln1_outs, transposed=[1])

    gu = mm(h2, w_gu, name="ffn_in_fwd")
    cb_f = _tile(d_ff, 256)
    ffa_ins = [act(gu, cb_f, 0, d_ff, rows=tbc), act(gu, cb_f, d_ff, d_ff, rows=tbc), prm(fcw, cb_f, krows(3)),
               prm(row1(W['ffn_conv_b']), cb_f)]
    ffa_grid, ffa_car, ffa_outs = (d_ff // cb_f, ntc), [(SUBLANES, cb_f)], [out(d_ff, cb_f, tbc, MXU_DTYPE)]
    (act_ff,), ffa_hist, (act_t,) = stage_fwd("ffn_act_fwd", f_ffn_act, ffa_grid, ffa_ins, ffa_outs, ffa_car,
                                              transposed=[0])
    ff = mm(act_ff, w_fd, name="ffn_down_fwd")
    ln2_ins = [act(x1, d), act(ff, d), prm(gt2, d), prm(row1(W['ln2_g']), d), prm(row1(W['ln2_b']), d),
               act(tgt, d, grad=False)]
    ln2_outs = [Out((t, 1), (tb, 1), lambda o, s: (s, 0))]
    (loss_rows,), _ = stage_fwd("ln2_loss_fwd", f_deepnorm_loss, (1, nt), ln2_ins, ln2_outs)

    dx1_a, d_ff_o, d_gt2, d_ln2g, d_ln2b = stage_bwd("ln2_loss_bwd", f_deepnorm_loss, (1, nt), ln2_ins, ln2_outs,
                                                     [jnp.ones((t, 1), F32)], gdtypes={1: MXU_DTYPE})
    d_act = mm(d_ff_o, w_fd, name="ffn_down_bwd_x", tb=True)
    gw_fd = mm(act_t, d_ff_o, name="ffn_down_bwd_w")
    d_gp, d_up, d_fcw, d_fcb = stage_bwd("ffn_act_bwd", f_ffn_act, ffa_grid, ffa_ins, ffa_outs, [d_act],
                                         ffa_car, ffa_hist, gdtypes={0: MXU_DTYPE, 1: MXU_DTYPE})
    col_blocks = lambda g: jnp.transpose(g.reshape(g.shape[0], N_DEV, -1), (1, 0, 2)).astype(WIRE_DTYPE)
    row_blocks = lambda g: g.reshape(N_DEV, -1, g.shape[1]).astype(WIRE_DTYPE)
    big_parts = {}
    d_h2, big_parts['ffn_w_down'] = mm([d_gp, d_up], [w_gate, w_up], name="ffn_in_bwd_x", tb=True,
                                       scatter=[row_blocks(gw_fd)])
    gw_gate, gw_up = mm(h2_t, d_gp, name="ffn_gate_bwd_w"), mm(h2_t, d_up, name="ffn_up_bwd_w")
    dx_a, d_mix, d_gt1, d_ln1g, d_ln1b, d_sc2, d_sh2 = stage_bwd("ln1_mod2_bwd", f_deepnorm_mod, (1, nt), ln1_ins,
                                                                 ln1_outs, [dx1_a, d_h2], gdtypes={1: MXU_DTYPE})
    d_merged = mm(d_mix, w_o, name="w_out_bwd_x", tb=True)
    gw_o = mm(merged_t, d_mix, name="w_out_bwd_w")
    d_ga, d_gb, d_ya, d_yb = stage_bwd("merge_bwd", f_merge, mrg_grid, mrg_ins, mrg_outs, [d_merged],
                                       gdtypes={k_: MXU_DTYPE for k_ in range(4)})
    d_rec = mm(d_ya, w_pa, name="proj_a_bwd_x", tb=True)
    gw_pa = mm(rec_t, d_ya, name="proj_a_bwd_w")
    d_dn = mm(d_yb, w_pb, name="proj_b_bwd_x", tb=True)
    gw_pb = mm(dn_t, d_yb, name="proj_b_bwd_w")

    d_o, d_z, d_nwt = stage_bwd("dn_out_bwd", f_dn_out, dno_grid, dno_ins, dno_outs, [d_dn], gdtypes={1: MXU_DTYPE})
    *d_mid, d_g_state = delta_inter_bwd(*dn_mid, g_dn, dn_hist, d_o, n_vh)
    d_qn, d_kn, d_v, d_g_col, d_gt, d_beta = delta_intra_bwd(qn, kn, v_c, 0, g_dn, gt_dn, beta_dn, d_mid, n_vh)
    d_g_row = jnp.pad(jnp.transpose(d_gt, (0, 2, 1)).reshape(t, n_vh), ((0, 0), (0, LANES - n_vh)))
    d_a, d_b, d_alog, d_dtb = stage_bwd("dn_gates_bwd", f_dn_gates, (1, nt), gate_ins, gate_outs,
                                        [(d_g_state, d_g_col, d_g_row), d_beta], gdtypes={0: MXU_DTYPE, 1: MXU_DTYPE})
    d_win, d_dcw = {}, []
    for nm, cot in (("q", d_qn), ("k", d_kn), ("v", d_v)):
        _, f_, ins_, grid_, outs_, car_, hist_ = dnc[nm]
        d_win[nm], dw_ = stage_bwd("dn_conv_%s_bwd" % nm, f_, grid_, ins_, outs_, [cot], car_, hist_,
                                   gdtypes={0: MXU_DTYPE})
        d_dcw.append(dw_)
    d_dcw = jnp.concatenate(d_dcw, axis=1)

    d_xc_a, d_pr, d_pi, d_gr, d_ba, d_bx, d_lam = stage_bwd(
        "rglru_bwd", f_rglru, lru_grid, lru_ins, [out(d_rnn, cb_r)], [d_rec], lru_car, lru_hist,
        gdtypes={1: MXU_DTYPE, 2: MXU_DTYPE, 3: MXU_DTYPE})
    d_xc_b, big_parts['w_out'], big_parts['w_proj_a'], big_parts['w_proj_b'] = mm(
        [d_pr, d_pi], [w_bd[:, :d_rnn], w_bd[:, d_rnn:]], name="rg_gates_bwd_x", tb=True,
        scatter=[row_blocks(gw_o), row_blocks(gw_pa), row_blocks(gw_pb)])
    gw_bd_a, gw_bd_x = mm(xc_t, d_pr, name="rg_gate_a_bwd_w"), mm(xc_t, d_pi, name="rg_gate_x_bwd_w")
    d_xr, d_rcw, d_rcb = stage_bwd("rg_conv_bwd", f_rg_conv, rgc_grid, rgc_ins, rgc_outs, [(d_xc_a, d_xc_b)],
                                   rgc_car, rgc_hist, gdtypes={0: MXU_DTYPE})

    diag = lambda g: jnp.einsum('nimj,nm->nij', g.reshape(n_blk, d_rnn // n_blk, n_blk, d_rnn // n_blk), eye_b)
    small_names = ['rg_conv_w', 'rg_conv_b', 'rg_w_a', 'rg_b_a', 'rg_w_x', 'rg_b_x', 'rg_lambda', 'dn_conv_w',
                   'dn_a_log', 'dn_dt_bias', 'dn_norm_w', 'ln1_g', 'ln1_b', 'ffn_conv_w', 'ffn_conv_b', 'ln2_g', 'ln2_b']
    small_loc = {
        'rg_conv_w': d_rcw, 'rg_conv_b': d_rcb,
        'rg_w_a': diag(gw_bd_a), 'rg_b_a': d_ba, 'rg_w_x': diag(gw_bd_x), 'rg_b_x': d_bx,
        'rg_lambda': d_lam, 'dn_conv_w': d_dcw, 'dn_a_log': d_alog[:, :n_vh], 'dn_dt_bias': d_dtb[:, :n_vh],
        'dn_norm_w': jnp.sum(d_nwt.reshape(n_vh, LANES), axis=0), 'ln1_g': d_ln1g, 'ln1_b': d_ln1b,
        'ffn_conv_w': d_fcw, 'ffn_conv_b': d_fcb, 'ln2_g': d_ln2g, 'ln2_b': d_ln2b}
    small_list = [small_loc[n] for n in small_names]

    d_segs = [d_xr, d_gr, d_win["q"], d_win["k"], d_win["v"], d_z, d_a, d_b, d_ga, d_gb]
    riders = {4: ('ffn_w_gate', gw_gate), 5: ('ffn_w_up', gw_up)}
    gw_segs = []
    for i, dg in enumerate(d_segs):
        if i in riders:
            g_, big_parts[riders[i][0]] = mm(h1_t, dg, name="proj_bwd_w%d" % i, scatter=[col_blocks(riders[i][1])])
        else:
            g_ = mm(h1_t, dg, name="proj_bwd_w%d" % i)
        gw_segs.append(g_)
    gw_in = jnp.concatenate([g_[:, :splits[i]] for i, g_ in enumerate(gw_segs)], axis=1)
    half = len(d_segs) // 2
    d_h1_a, big_parts['w_in'] = mm(d_segs[:half], groups[:half], name="proj_bwd_x0", tb=True,
                                   scatter=[col_blocks(gw_in)], **MM_SPLIT_CAPS)
    d_h1_b, small_all = mm(d_segs[half:], groups[half:], name="proj_bwd_x1", tb=True,
                           gather=[_pack(small_list)], **MM_SPLIT_CAPS)
    grad_x, d_sc1, d_sh1 = stage_bwd("mod1_bwd", f_modulate, (1, nt), mod1_ins, [out(d, d)], [(d_h1_a, d_h1_b)],
                                     add_to={0: dx_a})

    g_small = dict(zip(small_names, _unpack(sum_parts("sum_small_grads", small_all), small_list)))
    d_ada_me = jnp.concatenate([d_sh1, d_sc1, d_gt1, d_sh2, d_sc2, d_gt2], axis=1)
    (d_ada_all,) = all_gather("gather_d_ada", [d_ada_me.reshape(-1, LANES)])
    g_small['b_ada'] = sum_parts("sum_d_ada", d_ada_all)
    small_names = ['b_ada'] + small_names
    d_ada_cols = lax.dynamic_slice(d_ada_all.reshape(N_DEV, 6 * d), (0, me * ada_w), (N_DEV, ada_w))
    d_ada_pad = jnp.pad(d_ada_cols, ((0, LANES - N_DEV), (0, 0)))
    gw_ada = mm(c_pad, d_ada_pad, name="ada_bwd_w", ta=True, a_act="silu")

    res = {}
    big_parts['w_ada'] = gw_ada[None]
    for n in ['w_ada'] + big:
        res[n] = adamw("adamw_" + n, W[n], big_parts[n], M[n], V[n])
    for n in small_sh:
        w_ = W[n].shape[1]
        g_small[n] = lax.dynamic_slice(g_small[n], (0, me * w_), (W[n].shape[0], w_))
    for n in small_names:
        g_small[n] = g_small[n].reshape(W[n].shape)
    pk = lambda dct: _pack([dct[n] for n in small_names])
    s_g, s_d, s_m, s_v = adamw("adamw_small", pk(W), pk(g_small)[None], pk(M), pk(V))
    like = [W[n] for n in small_names]
    for n, g_, d_, m_, v_ in zip(small_names, _unpack(s_g, like), _unpack(s_d, like), _unpack(s_m, like), _unpack(s_v, like)):
        res[n] = (g_, d_, m_, v_)

    loss = lax.psum(jnp.sum(loss_rows), ("x", "y", "c"))
    outs = [loss, grad_x[None]]
    for j in range(4):
        outs += [res[n][j].reshape(loc[n].shape) for n in names]
    return tuple(outs)
```

```python
import functools
import math

import jax
import jax.numpy as jnp
from jax import lax
from jax.experimental import pallas as pl
from jax.experimental.pallas import tpu as pltpu

F32 = jnp.float32
BF16 = jnp.bfloat16
MXU_DTYPE = BF16
WIRE_DTYPE = BF16
DN_DTYPE = BF16
HI = lax.Precision.HIGHEST
MESH = pl.DeviceIdType.MESH

N_DEV = 8
LANES = 128
SUBLANES = 8
VMEM_LIMIT = 56 * 1024 * 1024
MM_TM_CAP, MM_TN_CAP, MM_TK_CAP = 1536, 1536, 2048
MM_SPLIT_CAPS = dict(tm_cap=1024, tn_cap=1024, tk_cap=1024)

RG_C = 8.0
DN_CHUNK = 64
DN_HEAD_GROUP = 16
DN_INTER_CHUNKS = 4
DN_INTRA_CHUNKS = 2
LN_EPS = 1e-5
RMS_EPS = 1e-6
L2_EPS = 1e-6
DEPTH = 1
DEEPNORM_ALPHA = (2 * DEPTH) ** 0.25
ADAM_LR = 0.001
ADAM_B1 = 0.9
ADAM_B2 = 0.999
ADAM_EPS = 1e-08
ADAM_WD = 0.01
ADAM_STEP = 10


def _tile(n, cap, unit=LANES):
    best = None
    for t in range(unit, min(n, cap) + 1, unit):
        if n % t == 0:
            best = t
    return best if best is not None else n


def _round_up(n, m):
    return (n + m - 1) // m * m


_HBM = pl.BlockSpec(memory_space=pl.ANY)


def _exchange_sems(n):
    return [pltpu.SemaphoreType.DMA((n, N_DEV - 1)), pltpu.SemaphoreType.DMA((n, N_DEV - 1)),
            pltpu.SemaphoreType.DMA((n,))]


def _exchange_out_shape(arrs, scatter):
    return [jax.ShapeDtypeStruct(a.shape if scatter else (N_DEV,) + a.shape, a.dtype) for a in arrs]


def _exchange_copies(in_refs, out_refs, sems, scatter, phase):
    send_sems, recv_sems, local_sems = sems
    x, y, c = lax.axis_index("x"), lax.axis_index("y"), lax.axis_index("c")
    me = 4 * x + 2 * y + c
    peers = [(x ^ ((k >> 2) & 1), y ^ ((k >> 1) & 1), c ^ (k & 1)) for k in range(N_DEV)]
    row = [4 * p[0] + 2 * p[1] + p[2] for p in peers]
    n = len(in_refs)

    def local(i):
        return pltpu.make_async_copy(in_refs[i].at[me] if scatter else in_refs[i], out_refs[i].at[me], local_sems.at[i])

    def remote(i, k, src, dst_row, to):
        return pltpu.make_async_remote_copy(src_ref=src, dst_ref=out_refs[i].at[dst_row],
                                            send_sem=send_sems.at[i, k - 1], recv_sem=recv_sems.at[i, k - 1],
                                            device_id=to, device_id_type=MESH)

    if scatter:
        sends = [(i, k, in_refs[i].at[row[k]], me, peers[k]) for k in range(1, N_DEV) for i in range(n)]
        passed = []
    else:
        sends = [(i, k, in_refs[i], me, peers[k]) for k in (1, 2, 4, 6) for i in range(n)]
        passed = [(i, k + 1, out_refs[i].at[row[k]], row[k], peers[1]) for k in (2, 4, 6) for i in range(n)]
    arrival = lambda i, k: remote(i, k, in_refs[i].at[me] if scatter else in_refs[i], row[k], peers[k])

    if phase == "start":
        for i in range(n):
            local(i).start()
        for cp in sends:
            remote(*cp).start()
    else:
        for cp in passed:
            arrival(cp[0], cp[1] - 1).wait_recv()
            remote(*cp).start()
        waited = {(cp[0], cp[1] - 1) for cp in passed}
        for k in range(1, N_DEV):
            for i in range(n):
                if (i, k) not in waited:
                    arrival(i, k).wait_recv()
        for cp in sends + passed:
            remote(*cp).wait_send()
        for i in range(n):
            local(i).wait()


def mm(a, b, *, name, ta=False, tb=False, a_act=None, bias=None, out_dtype=F32,
       tm_cap=MM_TM_CAP, tn_cap=MM_TN_CAP, tk_cap=MM_TK_CAP, gather=(), scatter=()):
    a_segs = list(a) if isinstance(a, (list, tuple)) else [a]
    b_segs = list(b) if isinstance(b, (list, tuple)) else [b]
    ns = len(a_segs)
    assert ns == len(b_segs) and (ns == 1 or a_act is None)
    m = a_segs[0].shape[1] if ta else a_segs[0].shape[0]
    n = b_segs[0].shape[0] if tb else b_segs[0].shape[1]
    ks = [x.shape[0] if ta else x.shape[1] for x in a_segs]
    assert ks == [y.shape[1] if tb else y.shape[0] for y in b_segs], (ks, ta, tb)
    tm, tn = _tile(m, tm_cap), _tile(n, tn_cap)
    tks = [_tile(k_, tk_cap) for k_ in ks]
    cnt = [k_ // t_ for k_, t_ in zip(ks, tks)]
    lo = [sum(cnt[:s]) for s in range(ns)]
    nk = sum(cnt)
    grid = (m // tm, n // tn, nk)
    dims = (((0 if ta else 1,), (1 if tb else 0,)), ((), ()))
    xch = list(gather) + list(scatter)
    nx, ng = len(xch), len(gather)
    n_main = 2 * ns + (bias is not None)

    def body(*refs):
        a_refs, b_refs = refs[:ns], refs[ns:2 * ns]
        bias_ref = refs[2 * ns] if bias is not None else None
        x_in, o_ref, x_out = refs[n_main:n_main + nx], refs[n_main + nx], refs[n_main + nx + 1:n_main + 2 * nx + 1]
        rest = refs[n_main + 2 * nx + 1:]
        acc_ref = rest[0] if nk > 1 else None
        sems = rest[1 if nk > 1 else 0:]
        groups = []
        if ng:
            groups.append((x_in[:ng], x_out[:ng], sems[:3], False))
        if nx > ng:
            groups.append((x_in[ng:], x_out[ng:], sems[-3:], True))
        if nx:
            step = (pl.program_id(0) * grid[1] + pl.program_id(1)) * grid[2] + pl.program_id(2)

            @pl.when(step == 0)
            def _():
                for gi, go_, gs, sc in groups:
                    _exchange_copies(gi, go_, gs, sc, "start")
        kk = pl.program_id(2)

        def finish(r):
            if bias is not None:
                r = r + bias_ref[...]
            o_ref[...] = r.astype(o_ref.dtype)

        def segment(s):
            av = a_refs[s][...]
            if a_act == "silu":
                av = jax.nn.silu(av.astype(F32))
            prod = lax.dot_general(av.astype(MXU_DTYPE), b_refs[s][...].astype(MXU_DTYPE), dims,
                                   preferred_element_type=F32)
            if nk == 1:
                finish(prod)
                return
            opens, closes = lo[s] == 0, lo[s] + cnt[s] == nk
            if opens:
                @pl.when(kk == 0)
                def _():
                    acc_ref[...] = prod
            inner = [kk > 0] * opens + [kk < nk - 1] * closes
            if inner:
                @pl.when(functools.reduce(lambda p, q: p & q, inner))
                def _():
                    acc_ref[...] += prod
            else:
                acc_ref[...] += prod
            if closes:
                @pl.when(kk == nk - 1)
                def _():
                    finish(acc_ref[...] + prod)

        for s in range(ns):
            if ns == 1:
                segment(s)
            else:
                pl.when((kk >= lo[s]) & (kk < lo[s] + cnt[s]))(functools.partial(segment, s))

        if nx:
            @pl.when(step == grid[0] * grid[1] * grid[2] - 1)
            def _():
                for gi, go_, gs, sc in groups:
                    _exchange_copies(gi, go_, gs, sc, "wait")

    def seg_index(s):
        return lambda q: jnp.clip(q - lo[s], 0, cnt[s] - 1) if ns > 1 else q

    a_specs, b_specs = [], []
    for s in range(ns):
        qi, tk = seg_index(s), tks[s]
        a_specs.append(pl.BlockSpec((tk, tm), (lambda qi: lambda i, j, q: (qi(q), i))(qi)) if ta
                       else pl.BlockSpec((tm, tk), (lambda qi: lambda i, j, q: (i, qi(q)))(qi)))
        b_specs.append(pl.BlockSpec((tn, tk), (lambda qi: lambda i, j, q: (j, qi(q)))(qi)) if tb
                       else pl.BlockSpec((tk, tn), (lambda qi: lambda i, j, q: (qi(q), j))(qi)))
    in_specs, args = a_specs + b_specs, a_segs + b_segs
    if bias is not None:
        in_specs.append(pl.BlockSpec((1, tn), lambda i, j, q: (0, j)))
        args.append(bias)
    o_spec, o_shape = pl.BlockSpec((tm, tn), lambda i, j, q: (i, j)), jax.ShapeDtypeStruct((m, n), out_dtype)
    acc = [pltpu.VMEM((tm, tn), F32)] if nk > 1 else []
    if not nx:
        return pl.pallas_call(
            body, name=name, grid=grid, in_specs=in_specs, out_specs=o_spec, out_shape=o_shape, scratch_shapes=acc,
            compiler_params=pltpu.CompilerParams(dimension_semantics=("parallel", "parallel", "arbitrary"),
                                                 vmem_limit_bytes=VMEM_LIMIT),
        )(*args)
    return pl.pallas_call(
        body, name=name, grid=grid, in_specs=in_specs + [_HBM] * nx, out_specs=[o_spec] + [_HBM] * nx,
        out_shape=[o_shape] + _exchange_out_shape(list(gather), False) + _exchange_out_shape(list(scatter), True),
        scratch_shapes=acc + (_exchange_sems(ng) if ng else []) + (_exchange_sems(nx - ng) if nx > ng else []),
        compiler_params=pltpu.CompilerParams(dimension_semantics=("arbitrary", "arbitrary", "arbitrary"),
                                             vmem_limit_bytes=VMEM_LIMIT, has_side_effects=True),
    )(*args, *xch)


class In:
    def __init__(self, arr, block, imap, acc=False, grad=True, parts=None, gshape=None, gimap=None):
        self.arr, self.block, self.imap, self.acc, self.grad, self.parts = arr, block, imap, acc, grad, parts
        self.gshape = arr.shape if gshape is None else gshape
        self.gimap = imap if gimap is None else gimap


class Out:
    def __init__(self, shape, block, imap, dtype=F32):
        self.shape, self.block, self.imap, self.dtype = shape, block, imap, dtype


def _load(in_refs, ins):
    vals = []
    for r, i in zip(in_refs, ins):
        if i.parts is None:
            vals.append(r[...])
        else:
            vals.extend(r[p] for p in i.parts)
    return vals


def _stage_params():
    return pltpu.CompilerParams(dimension_semantics=("parallel", "arbitrary"), vmem_limit_bytes=VMEM_LIMIT)


def stage_fwd(name, f, grid, ins, outs, carries=(), transposed=()):
    n_in, n_out, n_c, n_t = len(ins), len(outs), len(carries), len(transposed)

    def body(*refs):
        in_refs, out_refs = refs[:n_in], refs[n_in:n_in + n_out]
        hist_refs = refs[n_in + n_out:n_in + n_out + n_c]
        t_refs = refs[n_in + n_out + n_c:n_in + n_out + n_c + n_t]
        c_refs = refs[n_in + n_out + n_c + n_t:]
        if n_c:
            @pl.when(pl.program_id(1) == 0)
            def _():
                for c in c_refs:
                    c[...] = jnp.zeros_like(c)
        cin = [c[...] for c in c_refs]
        for h, c in zip(hist_refs, cin):
            h[...] = c
        o, cout = f(*_load(in_refs, ins), *cin)
        for r, v in zip(out_refs, o):
            r[...] = v.astype(r.dtype)
        for r, k in zip(t_refs, transposed):
            r[...] = o[k].T.astype(r.dtype)
        for c, v in zip(c_refs, cout):
            c[...] = v

    hist_spec = lambda c: pl.BlockSpec((None, None) + tuple(c), lambda o, s: (o, s) + (0,) * len(c))
    flip = lambda o_: pl.BlockSpec(o_.block[::-1], (lambda im: lambda o, s: im(o, s)[::-1])(o_.imap))
    res = pl.pallas_call(
        body, name=name, grid=grid,
        in_specs=[pl.BlockSpec(i.block, i.imap) for i in ins],
        out_specs=[pl.BlockSpec(o.block, o.imap) for o in outs] + [hist_spec(c) for c in carries]
        + [flip(outs[k]) for k in transposed],
        out_shape=[jax.ShapeDtypeStruct(o.shape, o.dtype) for o in outs]
        + [jax.ShapeDtypeStruct(tuple(grid) + tuple(c), F32) for c in carries]
        + [jax.ShapeDtypeStruct(outs[k].shape[::-1], MXU_DTYPE) for k in transposed],
        scratch_shapes=[pltpu.VMEM(tuple(c), F32) for c in carries],
        compiler_params=_stage_params(),
    )(*[i.arr for i in ins])
    res = list(res)
    if transposed:
        return res[:n_out], res[n_out:n_out + n_c], res[n_out + n_c:]
    return res[:n_out], res[n_out:]


def stage_bwd(name, f, grid, ins, outs, cots, carries=(), hists=(), add_to=None, gdtypes=None):
    n_in, n_out, n_c = len(ins), len(outs), len(carries)
    ns = grid[1]
    add_to = add_to or {}
    gdtypes = gdtypes or {}
    add_idx = sorted(add_to)
    g_idx = [k for k, i in enumerate(ins) if i.grad]
    cots = [c if isinstance(c, (tuple, list)) else (c,) for c in cots]
    n_cot = [len(c) for c in cots]
    rev = lambda imap: (lambda o, s: imap(o, ns - 1 - s))

    def body(*refs):
        p = 0
        in_refs = refs[p:p + n_in]; p += n_in
        cot_refs = []
        for cnt in n_cot:
            cot_refs.append(refs[p:p + cnt]); p += cnt
        hist_refs = refs[p:p + n_c]; p += n_c
        add_refs = refs[p:p + len(add_idx)]; p += len(add_idx)
        g_refs = refs[p:p + len(g_idx)]; p += len(g_idx)
        dc_refs = refs[p:]
        first = pl.program_id(1) == 0
        if n_c:
            @pl.when(first)
            def _():
                for c in dc_refs:
                    c[...] = jnp.zeros_like(c)
        vals = _load(in_refs, ins)
        cin = [h[...] for h in hist_refs]
        (o, cout), vjp = jax.vjp(lambda *a: f(*a), *vals, *cin)
        cot_o = []
        for crs, v in zip(cot_refs, o):
            c = crs[0][...].astype(v.dtype)
            for extra in crs[1:]:
                c = c + extra[...].astype(v.dtype)
            cot_o.append(c)
        cot_c = tuple(c[...] for c in dc_refs)
        grads = vjp((tuple(cot_o), cot_c))
        pos, per_in = 0, []
        for i in ins:
            cnt = 1 if i.parts is None else len(i.parts)
            per_in.append(grads[pos:pos + cnt])
            pos += cnt
        dcin = grads[pos:]
        for gr, k in zip(g_refs, g_idx):
            i, gs = ins[k], per_in[k]
            if i.acc:
                @pl.when(first)
                def _(gr=gr):
                    gr[...] = jnp.zeros_like(gr)
                if i.parts is None:
                    gr[...] += gs[0].astype(gr.dtype)
                else:
                    for pt, g in zip(i.parts, gs):
                        gr[pt] += g.astype(gr.dtype)
            else:
                g = gs[0]
                if k in add_to:
                    g = g + add_refs[add_idx.index(k)][...].astype(g.dtype)
                gr[...] = g.astype(gr.dtype)
        for c, v in zip(dc_refs, dcin):
            c[...] = v

    in_specs = [pl.BlockSpec(i.block, rev(i.imap)) for i in ins]
    for o_, cnt in zip(outs, n_cot):
        in_specs += [pl.BlockSpec(o_.block, rev(o_.imap))] * cnt
    in_specs += [pl.BlockSpec((None, None) + tuple(c), (lambda c: (lambda o, s: (o, ns - 1 - s) + (0,) * len(c)))(c))
                 for c in carries]
    in_specs += [pl.BlockSpec(ins[k].block, rev(ins[k].gimap)) for k in add_idx]
    out_specs, out_shape = [], []
    for k in g_idx:
        i = ins[k]
        if i.acc:
            out_specs.append(pl.BlockSpec(i.block, (lambda im: (lambda o, s: im(o, 0)))(i.imap)))
        else:
            out_specs.append(pl.BlockSpec(i.block, rev(i.gimap)))
        out_shape.append(jax.ShapeDtypeStruct(i.gshape, gdtypes.get(k, F32)))
    res = pl.pallas_call(
        body, name=name, grid=grid, in_specs=in_specs, out_specs=out_specs, out_shape=out_shape,
        scratch_shapes=[pltpu.VMEM(tuple(c), F32) for c in carries],
        compiler_params=_stage_params(),
    )(*[i.arr for i in ins], *[a for c in cots for a in c], *hists, *[add_to[k] for k in add_idx])
    return list(res)


def _iota_rows(shape):
    return lax.broadcasted_iota(jnp.int32, shape, 0)


@functools.partial(jax.custom_vjp, nondiff_argnums=(1,))
def _roll_rows(x, s):
    return pltpu.roll(x, s % x.shape[0], 0)


def _roll_rows_fwd(x, s):
    return _roll_rows(x, s), None


def _roll_rows_bwd(s, _, g):
    return (_roll_rows(g, -s),)


_roll_rows.defvjp(_roll_rows_fwd, _roll_rows_bwd)


@jax.custom_vjp
def _drop_head(xx):
    return xx[SUBLANES:]


def _drop_head_fwd(xx):
    return xx[SUBLANES:], None


def _drop_head_bwd(_, g):
    return (jnp.concatenate([jnp.zeros((SUBLANES, g.shape[1]), g.dtype), g], axis=0),)


_drop_head.defvjp(_drop_head_fwd, _drop_head_bwd)


@jax.custom_vjp
def _last_rows(x):
    return x[x.shape[0] - SUBLANES:]


def _last_rows_fwd(x):
    return x[x.shape[0] - SUBLANES:], x.shape[0]


def _last_rows_bwd(n, g):
    return (jnp.concatenate([jnp.zeros((n - SUBLANES, g.shape[1]), g.dtype), g], axis=0),)


_last_rows.defvjp(_last_rows_fwd, _last_rows_bwd)


def _last_row(x):
    n = x.shape[0]
    return jnp.sum(jnp.where(_iota_rows(x.shape) == n - 1, x, 0.0), axis=0, keepdims=True)


def _scan_steps(n):
    s = 1
    while s < n:
        yield s
        s *= 2


def _block_scan_log(a, u, h0):
    n = a.shape[0]
    row = _iota_rows(a.shape)
    for s in _scan_steps(n):
        keep = row >= s
        a_s = jnp.where(keep, pltpu.roll(a, s, 0), 1.0)
        u_s = jnp.where(keep, pltpu.roll(u, s, 0), 0.0)
        u = u + a * u_s
        a = a * a_s
    return u + a * h0


def _block_scan_impl(a, u, edge, reverse=False):
    n, c = a.shape
    nt = n // SUBLANES
    a, u = a.reshape(nt, SUBLANES, c), u.reshape(nt, SUBLANES, c)
    row = lax.broadcasted_iota(jnp.int32, a.shape, 1)
    for s in _scan_steps(SUBLANES):
        keep, shift = (row < SUBLANES - s, SUBLANES - s) if reverse else (row >= s, s)
        a_s = jnp.where(keep, pltpu.roll(a, shift, 1), 1.0)
        u_s = jnp.where(keep, pltpu.roll(u, shift, 1), 0.0)
        u = u + a * u_s
        a = a * a_s
    carry = jnp.broadcast_to(edge, (SUBLANES, c))
    tiles = [None] * nt
    at = 0 if reverse else SUBLANES - 1
    for i in (reversed(range(nt)) if reverse else range(nt)):
        tiles[i] = u[i] + a[i] * carry
        carry = jnp.broadcast_to(tiles[i][at:at + 1, :], (SUBLANES, c))
    return jnp.stack(tiles).reshape(n, c)


@jax.custom_vjp
def _block_scan(a, u, h0):
    return _block_scan_log(a, u, h0)


def _block_scan_fwd(a, u, h0):
    h = _block_scan_impl(a, u, h0)
    return h, (a, h, h0)


def _block_scan_bwd(res, dh):
    a, h, h0 = res
    n = a.shape[0]
    row = _iota_rows(a.shape)
    lam = _block_scan_impl(pltpu.roll(a, n - 1, 0), dh, jnp.zeros_like(h0), reverse=True)
    h_prev = jnp.where(row >= 1, pltpu.roll(h, 1, 0), jnp.broadcast_to(h0, h.shape))
    d_h0 = jnp.sum(jnp.where(row == 0, a * lam, 0.0), axis=0, keepdims=True)
    return lam * h_prev, lam, d_h0


_block_scan.defvjp(_block_scan_fwd, _block_scan_bwd)


def _dot_hi(a, b, dims=(((1,), (0,)), ((), ()))):
    return lax.dot_general(a, b, dims, precision=HI, preferred_element_type=F32)


_NN, _NT, _TN = "nn", "nt", "tn"
_CONTRACT = {_NN: (1, 0), _NT: (1, 1), _TN: (0, 0)}


def _raw_dot(a, b, kind):
    ca, cb = _CONTRACT[kind]
    lead = a.ndim - 2
    dims = (((ca + lead,), (cb + lead,)), (tuple(range(lead)), tuple(range(lead))))
    return lax.dot_general(a.astype(DN_DTYPE), b.astype(DN_DTYPE), dims, preferred_element_type=F32)


@jax.custom_vjp
def _nn(a, b):
    return _raw_dot(a, b, _NN)


_nn.defvjp(lambda a, b: (_raw_dot(a, b, _NN), (a, b)),
           lambda r, g: (_raw_dot(g, r[1], _NT), _raw_dot(r[0], g, _TN)))


@jax.custom_vjp
def _nt(a, b):
    return _raw_dot(a, b, _NT)


_nt.defvjp(lambda a, b: (_raw_dot(a, b, _NT), (a, b)),
           lambda r, g: (_raw_dot(g, r[1], _NN), _raw_dot(g, r[0], _TN)))


@jax.custom_vjp
def _tn(a, b):
    return _raw_dot(a, b, _TN)


_tn.defvjp(lambda a, b: (_raw_dot(a, b, _TN), (a, b)),
           lambda r, g: (_raw_dot(r[1], g, _NT), _raw_dot(r[0], g, _NN)))


def _neumann_inverse(a):
    n = a.shape[-1]
    eye = (lax.broadcasted_iota(jnp.int32, (n, n), 0) == lax.broadcasted_iota(jnp.int32, (n, n), 1)).astype(F32)
    p = _raw_dot(a, a, _NN)
    e = p
    for _ in range(int(math.log2(n)) - 2):
        p = _raw_dot(p, p, _NN)
        e = e + p + _raw_dot(e, p, _NN)
    return eye - a + e - _raw_dot(a, e, _NN)


@jax.custom_vjp
def _unit_lower_inverse(a):
    return _neumann_inverse(a)


def _unit_lower_inverse_fwd(a):
    x = _neumann_inverse(a)
    return x, x


def _unit_lower_inverse_bwd(x, g):
    return (-_raw_dot(_raw_dot(x, g, _TN), x, _NT),)


_unit_lower_inverse.defvjp(_unit_lower_inverse_fwd, _unit_lower_inverse_bwd)


def _softplus(x):
    return jnp.maximum(x, 0.0) + jnp.log1p(jnp.exp(-jnp.abs(x)))


def _neg_expm1(x):
    series = -x * (1.0 + x * (0.5 + x * (1.0 / 6.0 + x * (1.0 / 24.0 + x * (1.0 / 120.0)))))
    return jnp.where(x > -0.03, series, 1.0 - jnp.exp(x))


def f_modulate(x, sc, sh):
    return (x * (1.0 + sc) + sh,), ()


def _deepnorm(x, y, gt, g, b):
    v = DEEPNORM_ALPHA * x + (1.0 + gt) * y
    mu = jnp.mean(v, axis=-1, keepdims=True)
    vc = v - mu
    var = jnp.mean(vc * vc, axis=-1, keepdims=True)
    return vc * lax.rsqrt(var + LN_EPS) * g + b


def f_deepnorm_mod(x, y, gt, g, b, sc, sh):
    x1 = _deepnorm(x, y, gt, g, b)
    return (x1, x1 * (1.0 + sc) + sh), ()


def f_deepnorm_loss(x, y, gt, g, b, target):
    err = _deepnorm(x, y, gt, g, b) - target
    return (0.5 * jnp.mean(err * err, axis=-1, keepdims=True),), ()


def _causal_conv(x, prev, ws):
    xx = jnp.concatenate([prev, x], axis=0)
    k = len(ws)
    y = ws[k - 1] * x
    for j in range(k - 1):
        y = y + ws[j] * _drop_head(_roll_rows(xx, k - 1 - j))
    return y


def f_rg_conv(x, w0, w1, w2, w3, b, prev):
    return (_causal_conv(x, prev, (w0, w1, w2, w3)) + b,), (_last_rows(x),)


def f_dn_conv(x, w0, w1, w2, w3, prev):
    return (jax.nn.silu(_causal_conv(x, prev, (w0, w1, w2, w3))),), (_last_rows(x),)


def f_ffn_act(gp, up, w0, w1, w2, b, prev):
    return (jax.nn.gelu(_causal_conv(gp, prev, (w0, w1, w2)) + b) * up,), (_last_rows(gp),)


def f_rglru(xc, pre_r, pre_i, gr, b_a, b_x, lam, h0):
    gate_r = jax.nn.sigmoid(pre_r + b_a)
    gate_i = jax.nn.sigmoid(pre_i + b_x)
    log_a = -RG_C * gate_r * _softplus(-lam)
    a = jnp.exp(log_a)
    mult = jnp.sqrt(_neg_expm1(2.0 * log_a))
    h = _block_scan(a, mult * gate_i * xc, h0)
    return (h * jax.nn.gelu(gr),), (_last_row(h),)


def f_dn_conv_norm(scale, x, w0, w1, w2, w3, prev):
    y = jax.nn.silu(_causal_conv(x, prev, (w0, w1, w2, w3)))
    return (y * lax.rsqrt(jnp.sum(y * y, axis=-1, keepdims=True) + L2_EPS) * scale,), (_last_rows(x),)


def f_dn_gates(a_in, b_in, a_log, dt_bias):
    g = -jnp.exp(a_log) * _softplus(a_in + dt_bias)
    n = g.shape[0]
    shift = int(math.log2(DN_CHUNK))
    ri = lax.broadcasted_iota(jnp.int32, (n, n), 0)
    ci = lax.broadcasted_iota(jnp.int32, (n, n), 1)
    tri = ((lax.shift_right_logical(ri, shift) == lax.shift_right_logical(ci, shift)) & (ri >= ci)).astype(F32)
    return (_dot_hi(tri, g), jax.nn.sigmoid(b_in)), ()


def f_dn_out(o, z, nw):
    r = lax.rsqrt(jnp.mean(o * o, axis=-1, keepdims=True) + RMS_EPS)
    return (o * r * nw * jax.nn.silu(z),), ()


def f_merge(ga, gb, ya, yb):
    return (jax.nn.sigmoid(ga) * ya + jax.nn.sigmoid(gb) * yb,), ()


def _delta_intra(q, k, v, g_i, g_j, beta):
    c = q.shape[-2]
    ri = lax.broadcasted_iota(jnp.int32, (c, c), 0)
    ci = lax.broadcasted_iota(jnp.int32, (c, c), 1)
    decay = jnp.exp(jnp.where(ri >= ci, g_i - g_j, -jnp.inf))
    g_last = jnp.sum(jnp.where(_iota_rows((c, 1)) == c - 1, g_i, 0.0), axis=-2, keepdims=True)
    exp_g = jnp.exp(g_i)
    kb = k * beta
    t_inv = _unit_lower_inverse(jnp.where(ri > ci, _nt(kb, k) * decay, 0.0))
    u = _nn(t_inv, v * beta)
    w = _nn(t_inv, kb * exp_g)
    return u, w, _nt(q, k) * decay, q * exp_g, k * jnp.exp(g_last - g_i)


def _delta_inter(u, w, qk, q_dec, k_dec, g_last, state):
    v_new = u - _nn(w, state)
    o = _nn(q_dec, state) + _nn(qk, v_new)
    return o, jnp.exp(g_last) * state + _tn(k_dec, v_new)


def _delta_params(sem):
    return pltpu.CompilerParams(dimension_semantics=(sem,), vmem_limit_bytes=VMEM_LIMIT)


def _head_groups(n_vh):
    hb = min(DN_HEAD_GROUP, n_vh)
    return [range(h0, h0 + hb) for h0 in range(0, n_vh, hb)]


def _stack(hs, f):
    return jnp.stack([f(h) for h in hs])


def _rows(ci):
    return slice(ci * DN_CHUNK, (ci + 1) * DN_CHUNK)


def _intra_pairs(n_vh):
    return [(ci, h) for ci in range(DN_INTRA_CHUNKS) for h in range(n_vh)]


def _pair_stack(pairs, ref, width=LANES, head_of=lambda h: h):
    return jnp.stack([ref[_rows(ci), head_of(h) * LANES:head_of(h) * LANES + width] for ci, h in pairs])


def _intra_operands(pairs, rep, q_ref, k_ref, v_ref, g_ref, gt_ref, b_ref):
    qk_head = lambda h: h // rep
    return (_pair_stack(pairs, q_ref, head_of=qk_head), _pair_stack(pairs, k_ref, head_of=qk_head),
            _pair_stack(pairs, v_ref), jnp.stack([g_ref[_rows(ci), h:h + 1] for ci, h in pairs]),
            jnp.stack([gt_ref[ci, h:h + 1, :] for ci, h in pairs]),
            jnp.stack([b_ref[_rows(ci), h:h + 1] for ci, h in pairs]))


def _intra_spec(width, col=0):
    return pl.BlockSpec((DN_INTRA_CHUNKS * DN_CHUNK, width), lambda s: (s, col))


def _inter_operands(hs, ci, u_ref, w_ref, qk_ref, qd_ref, kd_ref, g_ref):
    f32 = lambda ref, width=LANES: _stack(hs, lambda h: ref[_rows(ci), h * LANES:h * LANES + width].astype(F32))
    last = (ci + 1) * DN_CHUNK - 1
    return (f32(u_ref), f32(w_ref), f32(qk_ref, DN_CHUNK), f32(qd_ref), f32(kd_ref),
            _stack(hs, lambda h: g_ref[last:last + 1, h:h + 1]))


def _inter_spec(width, steps, reverse=False):
    rows = DN_INTER_CHUNKS * DN_CHUNK
    return pl.BlockSpec((rows, width), (lambda s: (steps - 1 - s, 0)) if reverse else (lambda s: (s, 0)))


def delta_intra_fwd(qn, kn, qkv, v_blk, big_g, big_gt, beta, n_vh):
    t, qk_w = qn.shape
    vdim = n_vh * LANES
    rep = vdim // qk_w
    nc = t // DN_CHUNK

    pairs = _intra_pairs(n_vh)

    def body(q_ref, k_ref, v_ref, g_ref, gt_ref, b_ref, u_ref, w_ref, qk_ref, qd_ref, kd_ref):
        u, w, qk, qd, kd = _delta_intra(*_intra_operands(pairs, rep, q_ref, k_ref, v_ref, g_ref, gt_ref, b_ref))
        for i, (ci, h) in enumerate(pairs):
            at = (_rows(ci), slice(h * LANES, (h + 1) * LANES))
            u_ref[at] = u[i]
            w_ref[at] = w[i].astype(w_ref.dtype)
            qk_ref[at] = jnp.concatenate([qk[i], jnp.zeros_like(qk[i])], axis=1).astype(qk_ref.dtype)
            qd_ref[at] = qd[i].astype(qd_ref.dtype)
            kd_ref[at] = kd[i].astype(kd_ref.dtype)

    return pl.pallas_call(
        body, name="delta_intra_fwd", grid=(nc // DN_INTRA_CHUNKS,),
        in_specs=[_intra_spec(qk_w), _intra_spec(qk_w), _intra_spec(vdim, v_blk), _intra_spec(LANES),
                  pl.BlockSpec((DN_INTRA_CHUNKS, n_vh, DN_CHUNK), lambda s: (s, 0, 0)), _intra_spec(LANES)],
        out_specs=[_intra_spec(vdim)] * 5,
        out_shape=[jax.ShapeDtypeStruct((t, vdim), F32)] + [jax.ShapeDtypeStruct((t, vdim), DN_DTYPE)] * 4,
        compiler_params=_delta_params("parallel"),
    )(qn, kn, qkv, big_g, big_gt, beta)


def delta_inter_fwd(u, w, qk, q_dec, k_dec, big_g, n_vh):
    t, vdim = u.shape
    nc = t // DN_CHUNK

    cpb = DN_INTER_CHUNKS
    steps = nc // cpb

    def body(u_ref, w_ref, qk_ref, qd_ref, kd_ref, g_ref, o_ref, hist_ref, s_ref):
        @pl.when(pl.program_id(0) == 0)
        def _():
            s_ref[...] = jnp.zeros_like(s_ref)
        for ci in range(cpb):
            for hs in _head_groups(n_vh):
                grp = slice(hs[0], hs[-1] + 1)
                st = s_ref[grp]
                hist_ref[ci, grp] = st
                o, ns = _delta_inter(*_inter_operands(hs, ci, u_ref, w_ref, qk_ref, qd_ref, kd_ref, g_ref), st)
                for i, h in enumerate(hs):
                    o_ref[_rows(ci), h * LANES:(h + 1) * LANES] = o[i]
                s_ref[grp] = ns

    return pl.pallas_call(
        body, name="delta_inter_fwd", grid=(steps,),
        in_specs=[_inter_spec(vdim, steps)] * 5 + [_inter_spec(LANES, steps)],
        out_specs=[_inter_spec(vdim, steps), pl.BlockSpec((cpb, n_vh, LANES, LANES), lambda s: (s, 0, 0, 0))],
        out_shape=[jax.ShapeDtypeStruct((t, vdim), F32), jax.ShapeDtypeStruct((nc, n_vh, LANES, LANES), F32)],
        scratch_shapes=[pltpu.VMEM((n_vh, LANES, LANES), F32)],
        compiler_params=_delta_params("arbitrary"),
    )(u, w, qk, q_dec, k_dec, big_g)


def delta_inter_bwd(u, w, qk, q_dec, k_dec, big_g, hist, d_o, n_vh):
    t, vdim = u.shape
    nc = t // DN_CHUNK

    cpb = DN_INTER_CHUNKS
    steps = nc // cpb

    def body(u_ref, w_ref, qk_ref, qd_ref, kd_ref, g_ref, hist_ref, do_ref,
             du_ref, dw_ref, dqk_ref, dqd_ref, dkd_ref, dg_ref, ds_ref):
        @pl.when(pl.program_id(0) == 0)
        def _():
            ds_ref[...] = jnp.zeros_like(ds_ref)
        lane = lax.broadcasted_iota(jnp.int32, (1, LANES), 1)
        last = _iota_rows((DN_CHUNK, LANES)) == DN_CHUNK - 1
        for ci in reversed(range(cpb)):
            dgl_all = jnp.zeros((1, LANES), F32)
            for hs in _head_groups(n_vh):
                grp = slice(hs[0], hs[-1] + 1)
                prim = _inter_operands(hs, ci, u_ref, w_ref, qk_ref, qd_ref, kd_ref, g_ref) + (hist_ref[ci, grp],)
                _, vjp = jax.vjp(_delta_inter, *prim)
                cot_o = _stack(hs, lambda h: do_ref[_rows(ci), h * LANES:(h + 1) * LANES])
                du, dw, dqk, dqd, dkd, dgl, dst = vjp((cot_o, ds_ref[grp]))
                ds_ref[grp] = dst
                for i, h in enumerate(hs):
                    sl = slice(h * LANES, (h + 1) * LANES)
                    du_ref[_rows(ci), sl] = du[i].astype(du_ref.dtype)
                    dw_ref[_rows(ci), sl] = dw[i].astype(dw_ref.dtype)
                    dqk_ref[_rows(ci), sl] = jnp.concatenate([dqk[i], jnp.zeros_like(dqk[i])], axis=1)
                    dqd_ref[_rows(ci), sl] = dqd[i]
                    dkd_ref[_rows(ci), sl] = dkd[i]
                    dgl_all = dgl_all + dgl[i] * (lane == h).astype(F32)
            dg_ref[_rows(ci), :] = jnp.where(last, jnp.broadcast_to(dgl_all, (DN_CHUNK, LANES)), 0.0)

    rv = lambda w_: _inter_spec(w_, steps, reverse=True)
    return pl.pallas_call(
        body, name="delta_inter_bwd", grid=(steps,),
        in_specs=[rv(vdim)] * 5 + [rv(LANES), pl.BlockSpec((cpb, n_vh, LANES, LANES), lambda s: (steps - 1 - s, 0, 0, 0)),
                                   rv(vdim)],
        out_specs=[rv(vdim)] * 5 + [rv(LANES)],
        out_shape=[jax.ShapeDtypeStruct((t, vdim), DN_DTYPE)] * 2 + [jax.ShapeDtypeStruct((t, vdim), F32)] * 3
        + [jax.ShapeDtypeStruct((t, LANES), F32)],
        scratch_shapes=[pltpu.VMEM((n_vh, LANES, LANES), F32)],
        compiler_params=_delta_params("arbitrary"),
    )(u, w, qk, q_dec, k_dec, big_g, hist, d_o)


def delta_intra_bwd(qn, kn, qkv, v_blk, big_g, big_gt, beta, cots, n_vh):
    t, qk_w = qn.shape
    vdim = n_vh * LANES
    rep = vdim // qk_w
    nc = t // DN_CHUNK

    pairs = _intra_pairs(n_vh)

    def body(q_ref, k_ref, v_ref, g_ref, gt_ref, b_ref, du_ref, dw_ref, dqk_ref, dqd_ref, dkd_ref,
             dq_ref, dk_ref, dv_ref, dg_ref, dgt_ref, db_ref):
        lane = lax.broadcasted_iota(jnp.int32, (1, LANES), 1)
        _, vjp = jax.vjp(_delta_intra, *_intra_operands(pairs, rep, q_ref, k_ref, v_ref, g_ref, gt_ref, b_ref))
        dq, dk, dv, dgi, dgj, db = vjp((_pair_stack(pairs, du_ref).astype(F32), _pair_stack(pairs, dw_ref).astype(F32),
                                        _pair_stack(pairs, dqk_ref, DN_CHUNK), _pair_stack(pairs, dqd_ref),
                                        _pair_stack(pairs, dkd_ref)))
        dg_all, db_all = {}, {}
        dq_acc, dk_acc = None, None
        for i, (ci, h) in enumerate(pairs):
            j = h // rep
            dv_ref[_rows(ci), h * LANES:(h + 1) * LANES] = dv[i]
            dgt_ref[ci, h:h + 1, :] = dgj[i]
            onehot = (lane == h).astype(F32)
            dg_all[ci] = dgi[i] * onehot + dg_all.get(ci, 0.0)
            db_all[ci] = db[i] * onehot + db_all.get(ci, 0.0)
            dq_acc = dq[i] if h % rep == 0 else dq_acc + dq[i]
            dk_acc = dk[i] if h % rep == 0 else dk_acc + dk[i]
            if h % rep == rep - 1:
                dq_ref[_rows(ci), j * LANES:(j + 1) * LANES] = dq_acc
                dk_ref[_rows(ci), j * LANES:(j + 1) * LANES] = dk_acc
        for ci in dg_all:
            dg_ref[_rows(ci), :] = dg_all[ci]
            db_ref[_rows(ci), :] = db_all[ci]

    gt_spec = pl.BlockSpec((DN_INTRA_CHUNKS, n_vh, DN_CHUNK), lambda s: (s, 0, 0))
    return pl.pallas_call(
        body, name="delta_intra_bwd", grid=(nc // DN_INTRA_CHUNKS,),
        in_specs=[_intra_spec(qk_w), _intra_spec(qk_w), _intra_spec(vdim, v_blk), _intra_spec(LANES), gt_spec,
                  _intra_spec(LANES)] + [_intra_spec(vdim)] * 5,
        out_specs=[_intra_spec(qk_w), _intra_spec(qk_w), _intra_spec(vdim), _intra_spec(LANES), gt_spec,
                   _intra_spec(LANES)],
        out_shape=[jax.ShapeDtypeStruct((t, qk_w), F32), jax.ShapeDtypeStruct((t, qk_w), F32),
                   jax.ShapeDtypeStruct((t, vdim), F32), jax.ShapeDtypeStruct((t, LANES), F32),
                   jax.ShapeDtypeStruct((nc, n_vh, DN_CHUNK), F32), jax.ShapeDtypeStruct((t, LANES), F32)],
        compiler_params=_delta_params("parallel"),
    )(qn, kn, qkv, big_g, big_gt, beta, *cots)


def all_gather(name, arrs):
    n = len(arrs)

    def body(*refs):
        in_refs, out_refs, sems = refs[:n], refs[n:2 * n], refs[2 * n:]
        _exchange_copies(in_refs, out_refs, sems, False, "start")
        _exchange_copies(in_refs, out_refs, sems, False, "wait")

    res = pl.pallas_call(
        body, name=name,
        in_specs=[_HBM] * n, out_specs=[_HBM] * n,
        out_shape=_exchange_out_shape(arrs, False), scratch_shapes=_exchange_sems(n),
        compiler_params=pltpu.CompilerParams(has_side_effects=True),
    )(*arrs)
    return list(res)


def _adamw_math(w, g, m, v):
    m = ADAM_B1 * m + (1.0 - ADAM_B1) * g
    v = ADAM_B2 * v + (1.0 - ADAM_B2) * (g * g)
    m_hat = m / (1.0 - ADAM_B1 ** ADAM_STEP)
    v_hat = v / (1.0 - ADAM_B2 ** ADAM_STEP)
    delta = -ADAM_LR * (m_hat / (jnp.sqrt(v_hat) + ADAM_EPS) + ADAM_WD * w)
    return delta, m, v


def adamw(name, w, parts, m, v, rows_cap=128):
    r, c = w.shape
    np_ = parts.shape[0]
    tr = _tile(r, rows_cap, SUBLANES * (4 // parts.dtype.itemsize))

    def body(w_ref, p_ref, m_ref, v_ref, g_ref, d_ref, nm_ref, nv_ref):
        g = p_ref[0].astype(F32)
        for k in range(1, np_):
            g = g + p_ref[k].astype(F32)
        delta, nm, nv = _adamw_math(w_ref[...], g, m_ref[...], v_ref[...])
        g_ref[...] = g
        d_ref[...] = delta
        nm_ref[...] = nm
        nv_ref[...] = nv

    spec = pl.BlockSpec((tr, c), lambda i: (i, 0))
    return pl.pallas_call(
        body, name=name, grid=(r // tr,),
        in_specs=[spec, pl.BlockSpec((np_, tr, c), lambda i: (0, i, 0)), spec, spec],
        out_specs=[spec] * 4, out_shape=[jax.ShapeDtypeStruct((r, c), F32)] * 4,
        compiler_params=pltpu.CompilerParams(dimension_semantics=("parallel",), vmem_limit_bytes=VMEM_LIMIT),
    )(w, parts, m, v)


def sum_parts(name, parts, rows_cap=256):
    np_, r, c = parts.shape
    tr = _tile(r, rows_cap, SUBLANES)

    def body(p_ref, o_ref):
        g = p_ref[0].astype(F32)
        for k in range(1, np_):
            g = g + p_ref[k].astype(F32)
        o_ref[...] = g

    return pl.pallas_call(
        body, name=name, grid=(r // tr,),
        in_specs=[pl.BlockSpec((np_, tr, c), lambda i: (0, i, 0))],
        out_specs=pl.BlockSpec((tr, c), lambda i: (i, 0)),
        out_shape=jax.ShapeDtypeStruct((r, c), F32),
        compiler_params=pltpu.CompilerParams(dimension_semantics=("parallel",), vmem_limit_bytes=VMEM_LIMIT),
    )(parts)


def _pack(arrs):
    flat = jnp.concatenate([a.reshape(-1).astype(F32) for a in arrs])
    n = flat.shape[0]
    return jnp.pad(flat, (0, _round_up(n, LANES * SUBLANES) - n)).reshape(-1, LANES)


def _unpack(packed, like):
    flat, out, pos = packed.reshape(-1), [], 0
    for a in like:
        out.append(flat[pos:pos + a.size].reshape(a.shape))
        pos += a.size
    return out


def kernel(x, c, w_ada, b_ada, w_in, rg_conv_w, rg_conv_b, rg_w_a, rg_b_a, rg_w_x, rg_b_x, rg_lambda, dn_conv_w, dn_a_log, dn_dt_bias, dn_norm_w, w_proj_a, w_proj_b, w_out, ln1_g, ln1_b, ffn_w_gate, ffn_w_up, ffn_conv_w, ffn_conv_b, ffn_w_down, ln2_g, ln2_b, loss_target, m_w_ada, m_b_ada, m_w_in, m_rg_conv_w, m_rg_conv_b, m_rg_w_a, m_rg_b_a, m_rg_w_x, m_rg_b_x, m_rg_lambda, m_dn_conv_w, m_dn_a_log, m_dn_dt_bias, m_dn_norm_w, m_w_proj_a, m_w_proj_b, m_w_out, m_ln1_g, m_ln1_b, m_ffn_w_gate, m_ffn_w_up, m_ffn_conv_w, m_ffn_conv_b, m_ffn_w_down, m_ln2_g, m_ln2_b, v_w_ada, v_b_ada, v_w_in, v_rg_conv_w, v_rg_conv_b, v_rg_w_a, v_rg_b_a, v_rg_w_x, v_rg_b_x, v_rg_lambda, v_dn_conv_w, v_dn_a_log, v_dn_dt_bias, v_dn_norm_w, v_w_proj_a, v_w_proj_b, v_w_out, v_ln1_g, v_ln1_b, v_ffn_w_gate, v_ffn_w_up, v_ffn_conv_w, v_ffn_conv_b, v_ffn_w_down, v_ln2_g, v_ln2_b):
    names = ['w_ada', 'b_ada', 'w_in', 'rg_conv_w', 'rg_conv_b', 'rg_w_a', 'rg_b_a', 'rg_w_x', 'rg_b_x', 'rg_lambda',
             'dn_conv_w', 'dn_a_log', 'dn_dt_bias', 'dn_norm_w', 'w_proj_a', 'w_proj_b', 'w_out', 'ln1_g', 'ln1_b',
             'ffn_w_gate', 'ffn_w_up', 'ffn_conv_w', 'ffn_conv_b', 'ffn_w_down', 'ln2_g', 'ln2_b']
    loc = locals()
    W = {n: loc[n][0] for n in names}
    M = {n: loc['m_' + n][0] for n in names}
    V = {n: loc['v_' + n][0] for n in names}

    me = 4 * lax.axis_index("x") + 2 * lax.axis_index("y") + lax.axis_index("c")
    xs, tgt = x[0], loss_target[0]
    t, d = xs.shape
    d_rnn = W['rg_conv_b'].shape[0]
    n_blk = W['rg_w_a'].shape[0]
    n_vh = W['dn_a_log'].shape[0]
    assert W['dn_norm_w'].shape[0] == LANES
    vdim = n_vh * LANES
    d_ff = W['ffn_conv_b'].shape[0]
    d_in = W['w_in'].shape[1] * N_DEV
    qk = (d_in - 2 * d_rnn - 2 * vdim - 2 * n_vh - 2 * d) // 2
    assert vdim == 2 * qk and qk % LANES == 0 and n_vh <= LANES
    splits = (d_rnn, d_rnn, qk, qk, vdim, vdim, n_vh, n_vh, d, d)
    offs = [0]
    for s_ in splits:
        offs.append(offs[-1] + s_)

    tb = _tile(t, 256, SUBLANES)

    big = ['w_in', 'w_proj_a', 'w_proj_b', 'w_out', 'ffn_w_gate', 'ffn_w_up', 'ffn_w_down']
    small_sh = ['rg_conv_w', 'dn_conv_w', 'ffn_conv_w']
    first = all_gather("gather_first", [W['w_in'].astype(WIRE_DTYPE)] + [W[n] for n in small_sh] + [c])
    g_in, g_rcw, g_dcw, g_fcw, c_all = first
    cols = lambda g: jnp.transpose(g, (1, 0, 2)).reshape(g.shape[1], -1)
    rows = lambda g: g.reshape(-1, g.shape[2])
    w_in_f = cols(g_in)
    padl = lambda a: jnp.pad(a, ((0, 0), (0, LANES - a.shape[1])))
    groups = [w_in_f[:, offs[i]:offs[i + 1]] for i in range(10)]
    groups[6], groups[7] = padl(groups[6]), padl(groups[7])
    go = [0]
    for g_ in groups:
        go.append(go[-1] + g_.shape[1])
    n_pad = _round_up(go[-1], 512)
    wp = jnp.pad(jnp.concatenate(groups, axis=1), ((0, 0), (0, n_pad - go[-1])))
    o_xr, o_gr, o_q, o_k, o_v, o_z, o_a, o_b, o_ga, o_gb = go[:10]
    rcw, dcw, fcw = cols(g_rcw), cols(g_dcw), cols(g_fcw)
    eye_b = jnp.eye(n_blk, dtype=F32)
    bd = lambda w: (w[:, :, None, :] * eye_b[:, None, :, None]).reshape(d_rnn, d_rnn)
    w_bd = jnp.concatenate([bd(W['rg_w_a']), bd(W['rg_w_x'])], axis=1)
    row1 = lambda a: a.reshape(1, -1)
    padv = lambda a: jnp.pad(row1(a), ((0, 0), (0, LANES - a.shape[0])))
    nw_t = jnp.tile(row1(W['dn_norm_w']), (1, n_vh))

    c_pad =jnp.pad(c_all.reshape(N_DEV, d), ((0, LANES - N_DEV), (0, 0)))
    ada_w = W['w_ada'].shape[1]
    b_ada_me = lax.dynamic_slice(W['b_ada'], (me * ada_w,), (ada_w,)).reshape(1, ada_w)
    ada_sh = mm(c_pad, W['w_ada'], name="ada_fwd", a_act="silu", bias=b_ada_me)
    (ada_all,) = all_gather("gather_ada", [ada_sh[:N_DEV]])
    ada_me = lax.dynamic_slice(ada_all, (0, me, 0), (N_DEV, 1, ada_w)).reshape(6, 1, d)
    sh1, sc1, gt1, sh2, sc2, gt2 = [ada_me[i] for i in range(6)]

    nt = t // tb

    def act(a, bw, col0=0, width=None, grad=True, rows=tb):
        width = a.shape[1] if width is None else width
        assert col0 % bw == 0 and width % bw == 0
        c0 = col0 // bw
        return In(a, (rows, bw), lambda o, s: (s, c0 + o), grad=grad, gshape=(t, width), gimap=lambda o, s: (s, o))

    def prm(a, bw, parts=None):
        return In(a, (a.shape[0], bw), lambda o, s: (0, o), acc=True, parts=parts)

    def out(width, bw, rows=tb, dtype=F32):
        return Out((t, width), (rows, bw), lambda o, s: (s, o), dtype)

    tbh = _tile(t, 2048, SUBLANES)
    nth = t // tbh
    tbc = _tile(t, 2048, SUBLANES)
    ntc = t // tbc

    krows = lambda k_: [(slice(j, j + 1), slice(None)) for j in range(k_)]

    mod1_ins = [act(xs, d), prm(sc1, d), prm(sh1, d)]
    (h1,), _, (h1_t,) = stage_fwd("mod1_fwd", f_modulate, (1, nt), mod1_ins, [out(d, d, dtype=MXU_DTYPE)],
                                  transposed=[0])
    proj, g_pa, g_pb, g_out, g_fg, g_fu, g_fd = mm(h1, wp, name="proj_fwd",
                                                   gather=[W[n].astype(WIRE_DTYPE) for n in big[1:]])
    w_pa, w_pb, w_o, w_fd = rows(g_pa), rows(g_pb), rows(g_out), rows(g_fd)
    w_gate, w_up = cols(g_fg), cols(g_fu)

    cb_r = _tile(math.gcd(d_rnn, o_gr), 256)
    rgc_ins = [act(proj, cb_r, o_xr, d_rnn, rows=tbc), prm(rcw, cb_r, krows(4)), prm(row1(W['rg_conv_b']), cb_r)]
    rgc_grid, rgc_car, rgc_outs = (d_rnn // cb_r, ntc), [(SUBLANES, cb_r)], [out(d_rnn, cb_r, tbc)]
    (xc,), rgc_hist, (xc_t,) = stage_fwd("rg_conv_fwd", f_rg_conv, rgc_grid, rgc_ins, rgc_outs, rgc_car, transposed=[0])
    gates = mm(xc, w_bd, name="rg_gates_fwd")
    lru_ins = [act(xc, cb_r), act(gates, cb_r, 0, d_rnn), act(gates, cb_r, d_rnn, d_rnn), act(proj, cb_r, o_gr, d_rnn),
               prm(row1(W['rg_b_a']), cb_r), prm(row1(W['rg_b_x']), cb_r), prm(row1(W['rg_lambda']), cb_r)]
    lru_grid, lru_car = (d_rnn // cb_r, nt), [(1, cb_r)]
    (rec,), lru_hist, (rec_t,) = stage_fwd("rglru_fwd", f_rglru, lru_grid, lru_ins,
                                           [out(d_rnn, cb_r, dtype=MXU_DTYPE)], lru_car,
                                           transposed=[0])
    y_a = mm(rec, w_pa, name="proj_a_fwd")

    dnc = {}
    for nm, col0, width, w0, cb_, f_ in (("q", o_q, qk, 0, LANES, functools.partial(f_dn_conv_norm, LANES ** -0.5)),
                                         ("k", o_k, qk, qk, LANES, functools.partial(f_dn_conv_norm, 1.0)),
                                         ("v", o_v, vdim, 2 * qk, _tile(math.gcd(vdim, o_v), 256), f_dn_conv)):
        ins_ = [act(proj, cb_, col0, width, rows=tbc), prm(dcw[:, w0:w0 + width], cb_, krows(4))]
        grid_, outs_, car_ = (width // cb_, ntc), [out(width, cb_, tbc)], [(SUBLANES, cb_)]
        (y_,), hist_ = stage_fwd("dn_conv_%s_fwd" % nm, f_, grid_, ins_, outs_, car_)
        dnc[nm] = (y_, f_, ins_, grid_, outs_, car_, hist_)
    qn, kn, v_c = dnc["q"][0], dnc["k"][0], dnc["v"][0]
    gate_ins = [act(proj, LANES, o_a, LANES), act(proj, LANES, o_b, LANES),
                prm(padv(W['dn_a_log']), LANES), prm(padv(W['dn_dt_bias']), LANES)]
    gate_outs = [out(LANES, LANES), out(LANES, LANES)]
    (g_dn, beta_dn), _ = stage_fwd("dn_gates_fwd", f_dn_gates, (1, nt), gate_ins, gate_outs)
    n_ch = t // DN_CHUNK
    gt_dn = jnp.transpose(g_dn.reshape(n_ch, DN_CHUNK, LANES)[:, :, :n_vh], (0, 2, 1))
    dn_mid = delta_intra_fwd(qn, kn, v_c, 0, g_dn, gt_dn, beta_dn, n_vh)
    o_dn, dn_hist = delta_inter_fwd(*dn_mid, g_dn, n_vh)
    dno_ins = [act(o_dn, LANES, rows=tbh), act(proj, LANES, o_z, vdim, rows=tbh), prm(nw_t, LANES)]
    dno_grid, dno_outs = (n_vh, nth), [out(vdim, LANES, tbh, MXU_DTYPE)]
    (dn,), _, (dn_t,) = stage_fwd("dn_out_fwd", f_dn_out, dno_grid, dno_ins, dno_outs, transposed=[0])
    y_b = mm(dn, w_pb, name="proj_b_fwd")

    cb_m = _tile(math.gcd(math.gcd(d, o_ga), o_gb), 512)
    mrg_ins = [act(proj, cb_m, o_ga, d, rows=tbc), act(proj, cb_m, o_gb, d, rows=tbc), act(y_a, cb_m, rows=tbc),
               act(y_b, cb_m, rows=tbc)]
    mrg_grid, mrg_outs = (d // cb_m, ntc), [out(d, cb_m, tbc, MXU_DTYPE)]
    (merged,), _, (merged_t,) = stage_fwd("merge_fwd", f_merge, mrg_grid, mrg_ins, mrg_outs, transposed=[0])
    mix = mm(merged, w_o, name="w_out_fwd")
    ln1_ins = [act(xs, d), act(mix, d), prm(gt1, d), prm(row1(W['ln1_g']), d), prm(row1(W['ln1_b']), d),
               prm(sc2, d), prm(sh2, d)]
    ln1_outs = [out(d, d), out(d, d, dtype=MXU_DTYPE)]
    (x1, h2), _, (h2_t,) = stage_fwd("ln1_mod2_fwd", f_deepnorm_mod, (1, nt), ln1_ins, ln1_outs, transposed=[1])

    gp_ff, up_ff = mm(h2, w_gate, name="ffn_gate_fwd"), mm(h2, w_up, name="ffn_up_fwd")
    cb_f = _tile(d_ff, 256)
    ffa_ins = [act(gp_ff, cb_f, rows=tbc), act(up_ff, cb_f, rows=tbc), prm(fcw, cb_f, krows(3)),
               prm(row1(W['ffn_conv_b']), cb_f)]
    ffa_grid, ffa_car, ffa_outs = (d_ff // cb_f, ntc), [(SUBLANES, cb_f)], [out(d_ff, cb_f, tbc, MXU_DTYPE)]
    (act_ff,), ffa_hist, (act_t,) = stage_fwd("ffn_act_fwd", f_ffn_act, ffa_grid, ffa_ins, ffa_outs, ffa_car,
                                              transposed=[0])
    ff = mm(act_ff, w_fd, name="ffn_down_fwd")
    ln2_ins = [act(x1, d), act(ff, d), prm(gt2, d), prm(row1(W['ln2_g']), d), prm(row1(W['ln2_b']), d),
               act(tgt, d, grad=False)]
    ln2_outs = [Out((t, 1), (tb, 1), lambda o, s: (s, 0))]
    (loss_rows,), _ = stage_fwd("ln2_loss_fwd", f_deepnorm_loss, (1, nt), ln2_ins, ln2_outs)

    dx1_a, d_ff_o, d_gt2, d_ln2g, d_ln2b = stage_bwd("ln2_loss_bwd", f_deepnorm_loss, (1, nt), ln2_ins, ln2_outs,
                                                     [jnp.ones((t, 1), F32)], gdtypes={1: MXU_DTYPE})
    d_act = mm(d_ff_o, w_fd, name="ffn_down_bwd_x", tb=True)
    gw_fd = mm(act_t, d_ff_o, name="ffn_down_bwd_w")
    d_gp, d_up, d_fcw, d_fcb = stage_bwd("ffn_act_bwd", f_ffn_act, ffa_grid, ffa_ins, ffa_outs, [d_act],
                                         ffa_car, ffa_hist, gdtypes={0: MXU_DTYPE, 1: MXU_DTYPE})
    col_blocks = lambda g: jnp.transpose(g.reshape(g.shape[0], N_DEV, -1), (1, 0, 2)).astype(WIRE_DTYPE)
    row_blocks = lambda g: g.reshape(N_DEV, -1, g.shape[1]).astype(WIRE_DTYPE)
    big_parts = {}
    d_h2, big_parts['ffn_w_down'] = mm([d_gp, d_up], [w_gate, w_up], name="ffn_in_bwd_x", tb=True,
                                       scatter=[row_blocks(gw_fd)])
    gw_gate, gw_up = mm(h2_t, d_gp, name="ffn_gate_bwd_w"), mm(h2_t, d_up, name="ffn_up_bwd_w")
    dx_a, d_mix, d_gt1, d_ln1g, d_ln1b, d_sc2, d_sh2 = stage_bwd("ln1_mod2_bwd", f_deepnorm_mod, (1, nt), ln1_ins,
                                                                 ln1_outs, [dx1_a, d_h2], gdtypes={1: MXU_DTYPE})
    d_merged = mm(d_mix, w_o, name="w_out_bwd_x", tb=True)
    gw_o = mm(merged_t, d_mix, name="w_out_bwd_w")
    d_ga, d_gb, d_ya, d_yb = stage_bwd("merge_bwd", f_merge, mrg_grid, mrg_ins, mrg_outs, [d_merged],
                                       gdtypes={k_: MXU_DTYPE for k_ in range(4)})
    d_rec = mm(d_ya, w_pa, name="proj_a_bwd_x", tb=True)
    gw_pa = mm(rec_t, d_ya, name="proj_a_bwd_w")
    d_dn = mm(d_yb, w_pb, name="proj_b_bwd_x", tb=True)
    gw_pb = mm(dn_t, d_yb, name="proj_b_bwd_w")

    d_o, d_z, d_nwt = stage_bwd("dn_out_bwd", f_dn_out, dno_grid, dno_ins, dno_outs, [d_dn], gdtypes={1: MXU_DTYPE})
    *d_mid, d_g_state = delta_inter_bwd(*dn_mid, g_dn, dn_hist, d_o, n_vh)
    d_qn, d_kn, d_v, d_g_col, d_gt, d_beta = delta_intra_bwd(qn, kn, v_c, 0, g_dn, gt_dn, beta_dn, d_mid, n_vh)
    d_g_row = jnp.pad(jnp.transpose(d_gt, (0, 2, 1)).reshape(t, n_vh), ((0, 0), (0, LANES - n_vh)))
    d_a, d_b, d_alog, d_dtb = stage_bwd("dn_gates_bwd", f_dn_gates, (1, nt), gate_ins, gate_outs,
                                        [(d_g_state, d_g_col, d_g_row), d_beta], gdtypes={0: MXU_DTYPE, 1: MXU_DTYPE})
    d_win, d_dcw = {}, []
    for nm, cot in (("q", d_qn), ("k", d_kn), ("v", d_v)):
        _, f_, ins_, grid_, outs_, car_, hist_ = dnc[nm]
        d_win[nm], dw_ = stage_bwd("dn_conv_%s_bwd" % nm, f_, grid_, ins_, outs_, [cot], car_, hist_,
                                   gdtypes={0: MXU_DTYPE})
        d_dcw.append(dw_)
    d_dcw = jnp.concatenate(d_dcw, axis=1)

    d_xc_a, d_pr, d_pi, d_gr, d_ba, d_bx, d_lam = stage_bwd(
        "rglru_bwd", f_rglru, lru_grid, lru_ins, [out(d_rnn, cb_r)], [d_rec], lru_car, lru_hist,
        gdtypes={1: MXU_DTYPE, 2: MXU_DTYPE, 3: MXU_DTYPE})
    d_xc_b, big_parts['w_out'], big_parts['w_proj_a'], big_parts['w_proj_b'] = mm(
        [d_pr, d_pi], [w_bd[:, :d_rnn], w_bd[:, d_rnn:]], name="rg_gates_bwd_x", tb=True,
        scatter=[row_blocks(gw_o), row_blocks(gw_pa), row_blocks(gw_pb)])
    gw_bd_a, gw_bd_x = mm(xc_t, d_pr, name="rg_gate_a_bwd_w"), mm(xc_t, d_pi, name="rg_gate_x_bwd_w")
    d_xr, d_rcw, d_rcb = stage_bwd("rg_conv_bwd", f_rg_conv, rgc_grid, rgc_ins, rgc_outs, [(d_xc_a, d_xc_b)],
                                   rgc_car, rgc_hist, gdtypes={0: MXU_DTYPE})

    diag = lambda g: jnp.einsum('nimj,nm->nij', g.reshape(n_blk, d_rnn // n_blk, n_blk, d_rnn // n_blk), eye_b)
    small_names = ['rg_conv_w', 'rg_conv_b', 'rg_w_a', 'rg_b_a', 'rg_w_x', 'rg_b_x', 'rg_lambda', 'dn_conv_w',
                   'dn_a_log', 'dn_dt_bias', 'dn_norm_w', 'ln1_g', 'ln1_b', 'ffn_conv_w', 'ffn_conv_b', 'ln2_g', 'ln2_b']
    small_loc = {
        'rg_conv_w': d_rcw, 'rg_conv_b': d_rcb,
        'rg_w_a': diag(gw_bd_a), 'rg_b_a': d_ba, 'rg_w_x': diag(gw_bd_x), 'rg_b_x': d_bx,
        'rg_lambda': d_lam, 'dn_conv_w': d_dcw, 'dn_a_log': d_alog[:, :n_vh], 'dn_dt_bias': d_dtb[:, :n_vh],
        'dn_norm_w': jnp.sum(d_nwt.reshape(n_vh, LANES), axis=0), 'ln1_g': d_ln1g, 'ln1_b': d_ln1b,
        'ffn_conv_w': d_fcw, 'ffn_conv_b': d_fcb, 'ln2_g': d_ln2g, 'ln2_b': d_ln2b}
    small_list = [small_loc[n] for n in small_names]

    d_segs = [d_xr, d_gr, d_win["q"], d_win["k"], d_win["v"], d_z, d_a, d_b, d_ga, d_gb]
    riders = {4: ('ffn_w_gate', gw_gate), 5: ('ffn_w_up', gw_up)}
    gw_segs = []
    for i, dg in enumerate(d_segs):
        if i in riders:
            g_, big_parts[riders[i][0]] = mm(h1_t, dg, name="proj_bwd_w%d" % i, scatter=[col_blocks(riders[i][1])])
        else:
            g_ = mm(h1_t, dg, name="proj_bwd_w%d" % i)
        gw_segs.append(g_)
    gw_in = jnp.concatenate([g_[:, :splits[i]] for i, g_ in enumerate(gw_segs)], axis=1)
    half = len(d_segs) // 2
    d_h1_a, big_parts['w_in'] = mm(d_segs[:half], groups[:half], name="proj_bwd_x0", tb=True,
                                   scatter=[col_blocks(gw_in)], **MM_SPLIT_CAPS)
    d_h1_b, small_all = mm(d_segs[half:], groups[half:], name="proj_bwd_x1", tb=True,
                           gather=[_pack(small_list)], **MM_SPLIT_CAPS)
    grad_x, d_sc1, d_sh1 = stage_bwd("mod1_bwd", f_modulate, (1, nt), mod1_ins, [out(d, d)], [(d_h1_a, d_h1_b)],
                                     add_to={0: dx_a})

    g_small = dict(zip(small_names, _unpack(sum_parts("sum_small_grads", small_all), small_list)))
    d_ada_me = jnp.concatenate([d_sh1, d_sc1, d_gt1, d_sh2, d_sc2, d_gt2], axis=1)
    (d_ada_all,) = all_gather("gather_d_ada", [d_ada_me.reshape(-1, LANES)])
    g_small['b_ada'] = sum_parts("sum_d_ada", d_ada_all)
    small_names = ['b_ada'] + small_names
    d_ada_cols = lax.dynamic_slice(d_ada_all.reshape(N_DEV, 6 * d), (0, me * ada_w), (N_DEV, ada_w))
    d_ada_pad = jnp.pad(d_ada_cols, ((0, LANES - N_DEV), (0, 0)))
    gw_ada = mm(c_pad, d_ada_pad, name="ada_bwd_w", ta=True, a_act="silu")

    res = {}
    big_parts['w_ada'] = gw_ada[None]
    for n in ['w_ada'] + big:
        res[n] = adamw("adamw_" + n, W[n], big_parts[n], M[n], V[n])
    for n in small_sh:
        w_ = W[n].shape[1]
        g_small[n] = lax.dynamic_slice(g_small[n], (0, me * w_), (W[n].shape[0], w_))
    for n in small_names:
        g_small[n] = g_small[n].reshape(W[n].shape)
    pk = lambda dct: _pack([dct[n] for n in small_names])
    s_g, s_d, s_m, s_v = adamw("adamw_small", pk(W), pk(g_small)[None], pk(M), pk(V))
    like = [W[n] for n in small_names]
    for n, g_, d_, m_, v_ in zip(small_names, _unpack(s_g, like), _unpack(s_d, like), _unpack(s_m, like), _unpack(s_v, like)):
        res[n] = (g_, d_, m_, v_)

    loss = lax.psum(jnp.sum(loss_rows), ("x", "y", "c"))
    outs = [loss, grad_x[None]]
    for j in range(4):
        outs += [res[n][j].reshape(loc[n].shape) for n in names]
    return tuple(outs)
```
